```python
import jax
import jax.numpy as jnp
from jax import lax
import numpy as np

D_MODEL = 2048
BATCH = 4
SEQ = 2048
DEPTH = 2

GRID_W = 64
CTX_LEN = 256
NORM_EPS = 1e-6
ROPE_BASE = 10000.0
Q_BLOCK = 128

NA_HEADS = 8
NA_HEAD_DIM = 128
NA_WIDTH = NA_HEADS * NA_HEAD_DIM
NA_WIN_ROWS = 8
NA_WIN_COLS = 16

RW_HEAD_DIM = 64
RW_HEADS = 16
RW_WIDTH = RW_HEADS * RW_HEAD_DIM
RW_DECAY_RANK = 64
RW_ICLR_RANK = 64
RW_SHIFT_COLS = 3 * RW_WIDTH + 2 * RW_DECAY_RANK + 2 * RW_ICLR_RANK
RW_GN_EPS = 64e-5

MLA_HEADS = 8
MLA_Q_RANK = 512
MLA_KV_RANK = 512
MLA_NOPE_DIM = 128
MLA_ROPE_DIM = 64
MLA_V_DIM = 128
MLA_WIDTH = MLA_HEADS * MLA_V_DIM

HG_HEADS = 8
HG_EXPAND = 128
HG_HEAD_I = 128
HG_FDIM = HG_HEADS * HG_EXPAND
HG_WIDTH = HG_HEADS * HG_HEAD_I
HG_CHUNK = 32

EVEN_MIX = NA_WIDTH + RW_WIDTH
EVEN_IN = 3 * NA_WIDTH + RW_SHIFT_COLS + EVEN_MIX
ODD_MIX = MLA_WIDTH + HG_WIDTH
ODD_IN = MLA_Q_RANK + MLA_KV_RANK + MLA_ROPE_DIM + 3 * HG_FDIM + HG_WIDTH + ODD_MIX

kernel_name = 'hybrid_na_rwkv7_mla_hgrn2_dit'


def rms_norm(x, w):
    xf = x.astype(jnp.float32)
    y = xf * lax.rsqrt(jnp.mean(jnp.square(xf), axis=-1, keepdims=True) + NORM_EPS)
    return (y * w.astype(jnp.float32)).astype(x.dtype)


def adaln(cond, w, b):
    m = jax.nn.silu(cond) @ w + b
    return jnp.split(m, 3, axis=-1)


def heads(t, n):
    B, T, _ = t.shape
    return t.reshape(B, T, n, -1).transpose(0, 2, 1, 3)


def merge(t):
    B, n, T, d = t.shape
    return t.transpose(0, 2, 1, 3).reshape(B, T, n * d)


def softmax_f32(s, dtype):
    return jax.nn.softmax(s.astype(jnp.float32), axis=-1).astype(dtype)


def dense_attention(q, k, v):
    s = jnp.einsum('bhqd,bhkd->bhqk', q, k) * (q.shape[-1] ** -0.5)
    return jnp.einsum('bhqk,bhkd->bhqd', softmax_f32(s, v.dtype), v)


def axial_rope_angles(n_tokens, dim):
    t = jnp.arange(n_tokens, dtype=jnp.int32)
    pos = jnp.stack([t // GRID_W, t % GRID_W], axis=-1).astype(jnp.float32)
    n_freq = dim // 4
    inv = ROPE_BASE ** (-jnp.arange(n_freq, dtype=jnp.float32) / n_freq)
    return (pos[:, :, None] * inv).reshape(n_tokens, dim // 2)


def apply_rope(x, ang):
    xf = x.astype(jnp.float32).reshape(*x.shape[:-1], -1, 2)
    cos, sin = jnp.cos(ang), jnp.sin(ang)
    x0, x1 = xf[..., 0], xf[..., 1]
    y = jnp.stack([x0 * cos - x1 * sin, x0 * sin + x1 * cos], axis=-1)
    return y.reshape(x.shape).astype(x.dtype)


def neighbourhood_attention(q, k, v, kc, vc, rpb):
    B, H, T, d = q.shape
    rows = T // GRID_W
    kh, kw = min(NA_WIN_ROWS, rows), NA_WIN_COLS
    scale = d ** -0.5
    qg = q.reshape(B, H, rows, GRID_W, d)
    kg = k.reshape(B, H, rows, GRID_W, d)
    vg = v.reshape(B, H, rows, GRID_W, d)
    col = np.arange(GRID_W)
    col_start = np.clip(col - kw // 2, 0, GRID_W - kw)
    col_idx = col_start[:, None] + np.arange(kw)
    dx_idx = col_idx - col[:, None] + (NA_WIN_COLS - 1)
    rpb_cols = rpb[:, :, dx_idx]

    def one_row(r):
        r0 = jnp.clip(r - kh // 2, 0, rows - kh)
        q_r = lax.dynamic_index_in_dim(qg, r, axis=2, keepdims=False)
        k_w = lax.dynamic_slice_in_dim(kg, r0, kh, axis=2)[:, :, :, col_idx]
        v_w = lax.dynamic_slice_in_dim(vg, r0, kh, axis=2)[:, :, :, col_idx]
        dy_idx = r0 + jnp.arange(kh) - r + (NA_WIN_ROWS - 1)
        bias = jnp.take(rpb_cols, dy_idx, axis=1).transpose(0, 2, 1, 3)
        s_loc = jnp.einsum('bhjd,bhajkd->bhjak', q_r, k_w) * scale + bias[None]
        s_ctx = jnp.einsum('bhjd,bhld->bhjl', q_r, kc) * scale
        s = jnp.concatenate([s_loc.reshape(B, H, GRID_W, kh * kw), s_ctx], axis=-1)
        p = softmax_f32(s, v.dtype)
        p_loc = p[..., :kh * kw].reshape(B, H, GRID_W, kh, kw)
        return (jnp.einsum('bhjak,bhajkd->bhjd', p_loc, v_w)
                + jnp.einsum('bhjl,bhld->bhjd', p[..., kh * kw:], vc))

    out = lax.map(one_row, jnp.arange(rows))
    return out.transpose(1, 2, 0, 3, 4).reshape(B, H, T, d)


def token_shift(u, mu_prev, mu_next):
    prev = jnp.pad(u[:, :-1], ((0, 0), (1, 0), (0, 0)))
    nxt = jnp.pad(u[:, 1:], ((0, 0), (0, 1), (0, 0)))
    return u + mu_prev * (prev - u) + mu_next * (nxt - u)


def rwkv7_terms(u, w0, w2, a0, a2, k_k, k_a):
    B, T, _ = u.shape
    W, R = RW_WIDTH, RW_DECAY_RANK
    r, k, v, wd, ad = jnp.split(u.astype(jnp.float32), [W, 2 * W, 3 * W, 3 * W + 2 * R], axis=-1)
    wd = wd.reshape(B, T, 2, R)
    ad = ad.reshape(B, T, 2, RW_ICLR_RANK)
    w = w0 + jnp.einsum('btdr,drc->btdc', jnp.tanh(wd), w2)
    decay = jnp.exp(-jnp.exp(-jax.nn.softplus(-w) - 0.5))
    a = jax.nn.sigmoid(a0 + jnp.einsum('btdr,drc->btdc', ad, a2))
    kk = (k * k_k).reshape(B, T, RW_HEADS, RW_HEAD_DIM)
    kk = kk / jnp.maximum(jnp.linalg.norm(kk, axis=-1, keepdims=True), 1e-12)
    k_dir = k[:, :, None] * (1.0 + (a - 1.0) * k_a)

    def hd(t):
        return t.reshape(*t.shape[:-1], RW_HEADS, RW_HEAD_DIM)

    return hd(r), hd(decay), hd(k_dir), hd(v), kk, hd(a)


def dir_stack(t_fwd, t_bwd):
    return jnp.moveaxis(jnp.stack([t_fwd, jnp.flip(t_bwd, 1)], axis=0), 2, 0)


def dir_merge(y):
    y = jnp.moveaxis(y, 0, 2)
    return y[0] + jnp.flip(y[1], 1)


def rwkv7_scan(terms, s0):
    r, decay, k_dir, v, kk, a = terms
    xs = (dir_stack(r, r), dir_stack(decay[:, :, 0], decay[:, :, 1]),
          dir_stack(k_dir[:, :, 0], k_dir[:, :, 1]), dir_stack(v, v),
          dir_stack(-kk, -kk), dir_stack(kk * a[:, :, 0], kk * a[:, :, 1]))

    def step(S, inp):
        r_t, w_t, k_t, v_t, a_t, b_t = inp
        sa = jnp.einsum('...vk,...k->...v', S, a_t)
        S = S * w_t[..., None, :] + sa[..., :, None] * b_t[..., None, :] + v_t[..., :, None] * k_t[..., None, :]
        return S, jnp.einsum('...vk,...k->...v', S, r_t)

    s_fin, y = lax.scan(step, s0, xs)
    return dir_merge(y), s_fin


def rwkv7_readout(y, terms, r_k, ln_w, ln_b, dtype):
    r, _, k_dir, v, _, _ = terms
    B, T = y.shape[:2]
    mu = jnp.mean(y, axis=-1, keepdims=True)
    var = jnp.mean(jnp.square(y - mu), axis=-1, keepdims=True)
    yn = ((y - mu) * lax.rsqrt(var + RW_GN_EPS)).reshape(B, T, RW_WIDTH) * ln_w + ln_b
    bonus = jnp.sum(r[:, :, None] * k_dir * r_k, axis=(2, 4))[..., None] * v
    return (yn + bonus.reshape(B, T, RW_WIDTH)).astype(dtype)


def rwkv7_mixer(u, uc, w0, w2, a0, a2, k_k, k_a, r_k, ln_w, ln_b, ctx_out):
    terms = rwkv7_terms(u, w0, w2, a0, a2, k_k, k_a)
    terms_c = rwkv7_terms(uc, w0, w2, a0, a2, k_k, k_a)
    s0 = jnp.zeros((2, u.shape[0], RW_HEADS, RW_HEAD_DIM, RW_HEAD_DIM), jnp.float32)
    yc, s_ctx = rwkv7_scan(terms_c, s0)
    y, _ = rwkv7_scan(terms, s_ctx)
    out = rwkv7_readout(y, terms, r_k, ln_w, ln_b, u.dtype)
    if not ctx_out:
        return out, None
    return out, rwkv7_readout(yc, terms_c, r_k, ln_w, ln_b, u.dtype)


def mla_queries(cq, q_norm, w_uq):
    q = heads(rms_norm(cq, q_norm) @ w_uq, MLA_HEADS)
    return q[..., :MLA_NOPE_DIM], q[..., MLA_NOPE_DIM:]


def mla_keys(ckv, kpe, kv_norm, w_ukv):
    kv = heads(rms_norm(ckv, kv_norm) @ w_ukv, MLA_HEADS)
    return kv[..., :MLA_NOPE_DIM], kpe, kv[..., MLA_NOPE_DIM:]


def mla_attention(q_nope, q_pe, k_nope, k_pe, v):
    B, H, T, _ = q_nope.shape
    nb = T // Q_BLOCK
    scale = (MLA_NOPE_DIM + MLA_ROPE_DIM) ** -0.5
    qn = jnp.moveaxis(q_nope.reshape(B, H, nb, Q_BLOCK, -1), 2, 0)
    qp = jnp.moveaxis(q_pe.reshape(B, H, nb, Q_BLOCK, -1), 2, 0)

    def block(args):
        qn_b, qp_b = args
        s = (jnp.einsum('bhqd,bhkd->bhqk', qn_b, k_nope) + jnp.einsum('bhqd,bkd->bhqk', qp_b, k_pe)) * scale
        return jnp.einsum('bhqk,bhkd->bhqd', softmax_f32(s, v.dtype), v)

    out = lax.map(block, (qn, qp))
    return jnp.moveaxis(out, 0, 2).reshape(B, H, T, -1)


def hgrn2_chunk_scan(q, log_f, k, v, s0):
    *lead, T, _ = q.shape
    dv = v.shape[-1]
    n = T // HG_CHUNK

    def chunks(t):
        return t.reshape(*lead, n, HG_CHUNK, t.shape[-1])

    qc, gc, kc, vc = chunks(q), chunks(log_f), chunks(k), chunks(v)
    b = jnp.cumsum(gc, axis=-2)
    b_last = b[..., -1:, :]
    q_in = qc * jnp.exp(b)
    k_in = kc * jnp.exp(-b)
    k_end = kc * jnp.exp(b_last - b)
    mask = jnp.tril(jnp.ones((HG_CHUNK, HG_CHUNK), jnp.float32))
    attn = jnp.einsum('...id,...jd->...ij', q_in, k_in) * mask
    o_intra = jnp.einsum('...ij,...je->...ie', attn, vc)
    ds = jnp.einsum('...cd,...ce->...de', k_end, vc)
    dec = jnp.exp(b_last[..., 0, :])

    def step(S, inp):
        d_c, ds_c = inp
        return d_c[..., :, None] * S + ds_c, S

    s_fin, s_prev = lax.scan(step, s0, (jnp.moveaxis(dec, -2, 0), jnp.moveaxis(ds, -3, 0)))
    s_prev = jnp.moveaxis(s_prev, 0, -3)
    o = o_intra + jnp.einsum('...cd,...de->...ce', q_in, s_prev)
    return o.reshape(*lead, T, dv), s_fin


def hgrn2_terms(zz, lb):
    F = HG_FDIM
    B, T, _ = zz.shape
    q, f_fw, f_bw, i = jnp.split(zz, [F, 2 * F, 3 * F], axis=-1)
    fz = jnp.stack([f_fw, f_bw], axis=0).astype(jnp.float32)
    log_f = jnp.logaddexp(jnp.log(lb), jnp.log1p(-lb) + jax.nn.log_sigmoid(fz))
    k = (1.0 - lb) * jax.nn.sigmoid(-fz)

    def dir_heads(t):
        t = t.reshape(2, B, T, HG_HEADS, -1).transpose(0, 1, 3, 2, 4)
        return jnp.stack([t[0], jnp.flip(t[1], 2)], axis=0)

    def shared(t):
        t = heads(t, HG_HEADS).astype(jnp.float32)
        return jnp.stack([t, jnp.flip(t, 2)], axis=0)

    return shared(q), dir_heads(log_f), dir_heads(k), shared(i)


def hgrn2_mixer(zz, zzc, lb, norm_w, ctx_out):
    s0 = jnp.zeros((2, zz.shape[0], HG_HEADS, HG_EXPAND, HG_HEAD_I), jnp.float32)
    oc, s_ctx = hgrn2_chunk_scan(*hgrn2_terms(zzc, lb), s0)
    ol, _ = hgrn2_chunk_scan(*hgrn2_terms(zz, lb), s_ctx)

    def readout(o):
        o = o[0] + jnp.flip(o[1], 2)
        return merge(rms_norm(o, norm_w)).astype(zz.dtype)

    if not ctx_out:
        return readout(ol), None
    return readout(ol), readout(oc)


def even_mixers(z, zc, rpb, mu, w0, w2, a0, a2, k_k, k_a, r_k, ln_w, ln_b, ctx_out):
    s1 = 3 * NA_WIDTH
    s2 = s1 + RW_SHIFT_COLS
    q, k, v = (heads(t, NA_HEADS) for t in jnp.split(z[..., :s1], 3, axis=-1))
    qc_raw, kc_raw, vc_raw = jnp.split(zc[..., :s1], 3, axis=-1)
    kc, vc = heads(kc_raw, NA_HEADS), heads(vc_raw, NA_HEADS)
    y_na = merge(neighbourhood_attention(q, k, v, kc, vc, rpb))
    u = token_shift(z[..., s1:s2], mu[0], mu[1])
    uc = token_shift(zc[..., s1:s2], mu[0], mu[1])
    y_rw, yc_rw = rwkv7_mixer(u, uc, w0, w2, a0, a2, k_k, k_a, r_k, ln_w, ln_b, ctx_out)
    y = jnp.concatenate([y_na, y_rw], axis=-1) * jax.nn.silu(z[..., s2:])
    if not ctx_out:
        return y, None
    yc_na = merge(dense_attention(heads(qc_raw, NA_HEADS), kc, vc))
    yc = jnp.concatenate([yc_na, yc_rw], axis=-1) * jax.nn.silu(zc[..., s2:])
    return y, yc


def odd_mixers(z, zc, q_norm, w_uq, kv_norm, w_ukv, lb, hg_norm_w, ctx_out):
    o1 = MLA_Q_RANK
    o2 = o1 + MLA_KV_RANK
    o3 = o2 + MLA_ROPE_DIM
    o4 = o3 + 3 * HG_FDIM + HG_WIDTH
    ang = axial_rope_angles(z.shape[1], MLA_ROPE_DIM)
    qn, qp = mla_queries(z[..., :o1], q_norm, w_uq)
    kn, kp, v = mla_keys(z[..., o1:o2], z[..., o2:o3], kv_norm, w_ukv)
    knc, kpc, vc = mla_keys(zc[..., o1:o2], zc[..., o2:o3], kv_norm, w_ukv)
    qp = apply_rope(qp, ang)
    kp = apply_rope(kp, ang)
    y_mla = merge(mla_attention(qn, qp, jnp.concatenate([kn, knc], axis=2),
                                jnp.concatenate([kp, kpc], axis=1), jnp.concatenate([v, vc], axis=2)))
    y_hg, yc_hg = hgrn2_mixer(z[..., o3:o4], zc[..., o3:o4], lb, hg_norm_w, ctx_out)
    y = jnp.concatenate([y_mla, y_hg], axis=-1) * jax.nn.silu(z[..., o4:])
    if not ctx_out:
        return y, None
    qnc, qpc = mla_queries(zc[..., :o1], q_norm, w_uq)
    yc_mla = merge(mla_attention(qnc, qpc, knc, kpc, vc))
    yc = jnp.concatenate([yc_mla, yc_hg], axis=-1) * jax.nn.silu(zc[..., o4:])
    return y, yc


def setup_inputs(seed: int = 0) -> dict:
    key = jax.random.key(seed)
    ks = iter(jax.random.split(key, 40))
    f32 = jnp.float32
    D = D_MODEL
    ne = (DEPTH + 1) // 2
    no = DEPTH // 2

    def nrm(shape, scale):
        return jax.random.normal(next(ks), shape, f32) * scale

    def unif(shape, lo, hi):
        return jax.random.uniform(next(ks), shape, f32, lo, hi)

    return {
        'x': nrm((BATCH, SEQ, D), 1.0),
        'c': nrm((BATCH, D), 1.0),
        'ctx': nrm((BATCH, CTX_LEN, D), 1.0),
        'c_ctx': nrm((D,), 1.0),
        'ada_w': nrm((DEPTH, D, 3 * D), 0.5 * D ** -0.5),
        'ada_b': nrm((DEPTH, 3 * D), 0.02),
        'norm_w': 1.0 + nrm((DEPTH, D), 0.02),
        'e_w_in': nrm((ne, D, EVEN_IN), D ** -0.5),
        'e_w_out': nrm((ne, EVEN_MIX, D), EVEN_MIX ** -0.5),
        'na_rpb': nrm((ne, NA_HEADS, 2 * NA_WIN_ROWS - 1, 2 * NA_WIN_COLS - 1), 0.2),
        'rw_mu': unif((ne, 2, RW_SHIFT_COLS), 0.0, 0.5),
        'rw_w0': unif((ne, 2, RW_WIDTH), -6.0, 1.0),
        'rw_w2': nrm((ne, 2, RW_DECAY_RANK, RW_WIDTH), 0.1),
        'rw_a0': nrm((ne, 2, RW_WIDTH), 0.5),
        'rw_a2': nrm((ne, 2, RW_ICLR_RANK, RW_WIDTH), 0.1),
        'rw_k_k': 0.85 + nrm((ne, RW_WIDTH), 0.05),
        'rw_k_a': 1.0 + nrm((ne, RW_WIDTH), 0.05),
        'rw_r_k': nrm((ne, RW_HEADS, RW_HEAD_DIM), 0.1),
        'rw_ln_w': 1.0 + nrm((ne, RW_WIDTH), 0.02),
        'rw_ln_b': nrm((ne, RW_WIDTH), 0.02),
        'o_w_in': nrm((no, D, ODD_IN), D ** -0.5),
        'o_w_out': nrm((no, ODD_MIX, D), ODD_MIX ** -0.5),
        'mla_q_norm': 1.0 + nrm((no, MLA_Q_RANK), 0.02),
        'mla_w_uq': nrm((no, MLA_Q_RANK, MLA_HEADS * (MLA_NOPE_DIM + MLA_ROPE_DIM)), MLA_Q_RANK ** -0.5),
        'mla_kv_norm': 1.0 + nrm((no, MLA_KV_RANK), 0.02),
        'mla_w_ukv': nrm((no, MLA_KV_RANK, MLA_HEADS * (MLA_NOPE_DIM + MLA_V_DIM)), MLA_KV_RANK ** -0.5),
        'hg_lower_bounds': 1.0 + nrm((DEPTH, HG_FDIM), 0.1),
        'hg_norm_w': 1.0 + nrm((no, HG_HEAD_I), 0.02),
        'final_norm_w': 1.0 + nrm((D,), 0.02),
    }


def reference(x, c, ctx, c_ctx, ada_w, ada_b, norm_w, e_w_in, e_w_out, na_rpb, rw_mu, rw_w0, rw_w2,
              rw_a0, rw_a2, rw_k_k, rw_k_a, rw_r_k, rw_ln_w, rw_ln_b, o_w_in, o_w_out, mla_q_norm,
              mla_w_uq, mla_kv_norm, mla_w_ukv, hg_lower_bounds, hg_norm_w, final_norm_w):
    s = jax.nn.softmax(hg_lower_bounds.astype(jnp.float32), axis=0)
    lower_bounds = jnp.cumsum(s, axis=0) - s[0]
    xc = ctx
    for l in range(DEPTH):
        ctx_out = l < DEPTH - 1
        i = l // 2
        shift, scale, gate = adaln(c[:, None, :], ada_w[l], ada_b[l])
        shift_c, scale_c, gate_c = adaln(c_ctx, ada_w[l], ada_b[l])
        h = rms_norm(x, norm_w[l]) * (1.0 + scale) + shift
        hc = rms_norm(xc, norm_w[l]) * (1.0 + scale_c) + shift_c
        if l % 2 == 0:
            y, yc = even_mixers(h @ e_w_in[i], hc @ e_w_in[i], na_rpb[i], rw_mu[i], rw_w0[i], rw_w2[i],
                                rw_a0[i], rw_a2[i], rw_k_k[i], rw_k_a[i], rw_r_k[i], rw_ln_w[i],
                                rw_ln_b[i], ctx_out)
            w_out = e_w_out[i]
        else:
            y, yc = odd_mixers(h @ o_w_in[i], hc @ o_w_in[i], mla_q_norm[i], mla_w_uq[i], mla_kv_norm[i],
                               mla_w_ukv[i], lower_bounds[l], hg_norm_w[i], ctx_out)
            w_out = o_w_out[i]
        x = x + gate * (y @ w_out)
        if ctx_out:
            xc = xc + gate_c * (yc @ w_out)
    return rms_norm(x, final_norm_w)
```

```python
import functools

import numpy as np
import jax
import jax.numpy as jnp
from jax import lax
from jax.experimental import pallas as pl
from jax.experimental.pallas import tpu as pltpu

F32 = jnp.float32
BF16 = jnp.bfloat16

GRID_W = 64
NORM_EPS = 1e-6
ROPE_BASE = 10000.0

NA_HEADS = 8
NA_HEAD_DIM = 128
NA_WIDTH = NA_HEADS * NA_HEAD_DIM
NA_WIN_ROWS = 8
NA_WIN_COLS = 16
NA_QROWS = 4
NA_KROWS = 12

RW_HEAD_DIM = 64
RW_HEADS = 16
RW_WIDTH = RW_HEADS * RW_HEAD_DIM
RW_RANK = 64
RW_SHIFT_COLS = 3 * RW_WIDTH + 4 * RW_RANK
RW_PAD_COLS = 3584
RW_GN_EPS = 64e-5
RW_SCAN_CHUNK = 32

MLA_HEADS = 8
MLA_Q_RANK = 512
MLA_KV_RANK = 512
MLA_NOPE = 128
MLA_ROPE = 64
MLA_V = 128
MLA_QK_PAD = 256
MLA_WIDTH = MLA_HEADS * MLA_V
MLA_A_COLS = 1536

HG_HEADS = 8
HG_DK = 128
HG_DV = 128
HG_FDIM = HG_HEADS * HG_DK
HG_WIDTH = HG_HEADS * HG_DV
HG_CHUNK = 32
HG_BLOCK = 256

LANES = 128

GATE_COLS = 2048
EV_Q_BLK = GATE_COLS // LANES
EV_RW_BLK = EV_Q_BLK + 3 * NA_WIDTH // LANES
OD_A_BLK = GATE_COLS // LANES
OD_HG_BLK = OD_A_BLK + MLA_A_COLS // LANES
VMEM_LIMIT = 48 * 1024 * 1024
NEG_INF = -1e30


def _cparams(n_axes):
    return pltpu.CompilerParams(dimension_semantics=("arbitrary",) * n_axes, vmem_limit_bytes=VMEM_LIMIT)


def _sigmoid(x):
    return 1.0 / (1.0 + jnp.exp(-x))


def _silu(x):
    return x * _sigmoid(x)


def _dot(a, b):
    return jnp.dot(a.astype(BF16), b.astype(BF16), preferred_element_type=F32)


def _dot_nt(a, b):
    return lax.dot_general(a.astype(BF16), b.astype(BF16), (((1,), (1,)), ((), ())), preferred_element_type=F32)


def _dot_tn(a, b):
    return lax.dot_general(a.astype(BF16), b.astype(BF16), (((0,), (0,)), ((), ())), preferred_element_type=F32)


def _mod_kernel(c_ref, w_ref, b_ref, o_ref):
    o_ref[...] = _dot(_silu(c_ref[...]), w_ref[...]) + b_ref[...]


def modulation(cond, ada_w, ada_b, tn=512):
    depth, d, n = ada_w.shape
    return pl.pallas_call(
        _mod_kernel,
        out_shape=jax.ShapeDtypeStruct((depth, 8, n), F32),
        grid=(depth, n // tn),
        in_specs=[pl.BlockSpec((8, d), lambda l, j: (0, 0)),
                  pl.BlockSpec((None, d, tn), lambda l, j: (l, 0, j)),
                  pl.BlockSpec((None, 1, tn), lambda l, j: (l, 0, j))],
        out_specs=pl.BlockSpec((None, 8, tn), lambda l, j: (l, 0, j)),
        compiler_params=_cparams(2),
        name="adaln_modulation",
    )(cond, ada_w, ada_b.reshape(depth, 1, n))


def _inproj_kernel(x_ref, nw_ref, shift_ref, scale_ref, w_ref, o_ref, h_ref):
    @pl.when(pl.program_id(1) == 0)
    def _():
        x = x_ref[...]
        y = x * lax.rsqrt(jnp.mean(x * x, axis=-1, keepdims=True) + NORM_EPS) * nw_ref[...]
        h_ref[...] = (y * (1.0 + scale_ref[...]) + shift_ref[...]).astype(BF16)

    o_ref[...] = jnp.dot(h_ref[...], w_ref[...], preferred_element_type=F32)


def in_projection(x, norm_w, mod, mod_row, w, name, tm=1024, tn=512):
    m, d = x.shape
    n = w.shape[1]
    tm = min(tm, m)
    return pl.pallas_call(
        _inproj_kernel,
        out_shape=jax.ShapeDtypeStruct((m, n), F32),
        grid=(m // tm, n // tn),
        in_specs=[pl.BlockSpec((tm, d), lambda i, j: (i, 0)),
                  pl.BlockSpec((1, d), lambda i, j: (0, 0)),
                  pl.BlockSpec((None, 1, d), lambda i, j: (mod_row(i), 0, 0)),
                  pl.BlockSpec((None, 1, d), lambda i, j: (mod_row(i), 0, 1)),
                  pl.BlockSpec((d, tn), lambda i, j: (0, j))],
        out_specs=pl.BlockSpec((tm, tn), lambda i, j: (i, j)),
        scratch_shapes=[pltpu.VMEM((tm, d), BF16)],
        compiler_params=_cparams(2),
        name=name,
    )(x, norm_w.reshape(1, d), mod, mod, w)


def _outproj_kernel(ya_ref, yb_ref, g_ref, w_ref, x_ref, gm_ref, *rest, ka, final):
    o_ref = rest[-1]
    sg = _silu(g_ref[...])
    acc = _dot(ya_ref[...].astype(F32) * sg[:, :ka], w_ref[:ka, :])
    acc += _dot(yb_ref[...].astype(F32) * sg[:, ka:], w_ref[ka:, :])
    out = x_ref[...] + gm_ref[...] * acc
    if final:
        out = out * lax.rsqrt(jnp.mean(out * out, axis=-1, keepdims=True) + NORM_EPS) * rest[0][...]
    o_ref[...] = out


def out_projection(ya, yb, gate, w, x, mod, mod_row, name, final_w=None, tm=256):
    m, d = x.shape
    ka, kb = ya.shape[1], yb.shape[1]
    tm = min(tm, m)
    ins = [ya, yb, gate, w, x, mod]
    specs = [pl.BlockSpec((tm, ka), lambda i: (i, 0)),
             pl.BlockSpec((tm, kb), lambda i: (i, 0)),
             pl.BlockSpec((tm, ka + kb), lambda i: (i, 0)),
             pl.BlockSpec((ka + kb, d), lambda i: (0, 0)),
             pl.BlockSpec((tm, d), lambda i: (i, 0)),
             pl.BlockSpec((None, 1, d), lambda i: (mod_row(i), 0, 2))]
    if final_w is not None:
        ins.append(final_w.reshape(1, d))
        specs.append(pl.BlockSpec((1, d), lambda i: (0, 0)))
    return pl.pallas_call(
        functools.partial(_outproj_kernel, ka=ka, final=final_w is not None),
        out_shape=jax.ShapeDtypeStruct((m, d), F32),
        grid=(m // tm,),
        in_specs=specs,
        out_specs=pl.BlockSpec((tm, d), lambda i: (i, 0)),
        compiler_params=_cparams(1),
        name=name,
    )(*ins)


def _softmax_pv(parts):
    m = functools.reduce(jnp.maximum, [jnp.max(s, axis=-1, keepdims=True) for s, _ in parts])
    ps = [jnp.exp(s - m) for s, _ in parts]
    denom = functools.reduce(lambda a, b: a + b, [jnp.sum(p, axis=-1, keepdims=True) for p in ps])
    num = functools.reduce(lambda a, b: a + b, [_dot(p, v) for p, (_, v) in zip(ps, parts)])
    return num / denom


def _na_kernel(q_ref, k_ref, v_ref, kc_ref, vc_ref, bias_ref, o_ref, *, rows, scale):
    j = pl.program_id(1)
    start = pl.multiple_of(jnp.clip(NA_QROWS * j - NA_WIN_ROWS // 2, 0, rows - NA_KROWS) * GRID_W, GRID_W)
    nk = NA_KROWS * GRID_W
    q = q_ref[...]
    s_loc = _dot_nt(q, k_ref[pl.ds(start, nk), :]) * scale + bias_ref[...]
    s_ctx = _dot_nt(q, kc_ref[...]) * scale
    o_ref[...] = _softmax_pv([(s_loc, v_ref[pl.ds(start, nk), :]), (s_ctx, vc_ref[...])])


def _na_geometry(rows):
    kh = min(NA_WIN_ROWS, rows)
    geos, ids = [], []
    for j in range(rows // NA_QROWS):
        ks = int(np.clip(NA_QROWS * j - NA_WIN_ROWS // 2, 0, rows - NA_KROWS))
        r = NA_QROWS * j + np.arange(NA_QROWS)[:, None]
        kr = ks + np.arange(NA_KROWS)[None, :]
        r0 = np.clip(r - kh // 2, 0, rows - kh)
        valid = (kr >= r0) & (kr < r0 + kh)
        dy = np.where(valid, kr - r + NA_WIN_ROWS - 1, 0)
        key = (dy.tobytes(), valid.tobytes())
        if key not in [g[0] for g in geos]:
            geos.append((key, dy, valid))
        ids.append([g[0] for g in geos].index(key))
    return np.array(ids, np.int32), np.stack([g[1] for g in geos]), np.stack([g[2] for g in geos])


def _na_bias(rpb, rows):
    col = np.arange(GRID_W)
    cs = np.clip(col - NA_WIN_COLS // 2, 0, GRID_W - NA_WIN_COLS)
    kc = col[None, :]
    in_win = (kc >= cs[:, None]) & (kc < cs[:, None] + NA_WIN_COLS)
    dx = np.where(in_win, kc - col[:, None] + NA_WIN_COLS - 1, 0)
    tab = jnp.where(in_win[None, None], rpb[:, :, dx], NEG_INF)
    ids, dy, valid = _na_geometry(rows)
    blocks = jnp.where(valid[None, :, :, :, None, None], tab[:, dy], NEG_INF)
    g = dy.shape[0]
    blocks = blocks.transpose(0, 1, 2, 4, 3, 5).reshape(rpb.shape[0], g, NA_QROWS * GRID_W, NA_KROWS * GRID_W)
    return ids, blocks


def na_attention(z, z_c, rpb, batch, t, l):
    rows = t // GRID_W
    nj = rows // NA_QROWS
    tq = NA_QROWS * GRID_W
    ids, bias = _na_bias(rpb, rows)
    ids = jnp.asarray(ids)
    h8 = NA_HEADS
    return pl.pallas_call(
        functools.partial(_na_kernel_wrapped, rows=rows, scale=NA_HEAD_DIM ** -0.5),
        out_shape=jax.ShapeDtypeStruct((batch * t, NA_WIDTH), F32),
        grid_spec=pltpu.PrefetchScalarGridSpec(
            num_scalar_prefetch=1,
            grid=(h8, nj, batch),
            in_specs=[pl.BlockSpec((tq, LANES), lambda h, j, b, g: (b * nj + j, EV_Q_BLK + h)),
                      pl.BlockSpec((t, LANES), lambda h, j, b, g: (b, EV_Q_BLK + h8 + h)),
                      pl.BlockSpec((t, LANES), lambda h, j, b, g: (b, EV_Q_BLK + 2 * h8 + h)),
                      pl.BlockSpec((l, LANES), lambda h, j, b, g: (b, EV_Q_BLK + h8 + h)),
                      pl.BlockSpec((l, LANES), lambda h, j, b, g: (b, EV_Q_BLK + 2 * h8 + h)),
                      pl.BlockSpec((None, None, tq, NA_KROWS * GRID_W), lambda h, j, b, g: (h, g[j], 0, 0))],
            out_specs=pl.BlockSpec((tq, LANES), lambda h, j, b, g: (b * nj + j, h))),
        compiler_params=_cparams(3),
        name="na_attention",
    )(ids, z, z, z, z_c, z_c, bias)


def _na_kernel_wrapped(g_ref, *refs, rows, scale):
    del g_ref
    _na_kernel(*refs, rows=rows, scale=scale)


def _dense_attn_kernel(q_ref, k_ref, v_ref, o_ref, *, scale):
    s = _dot_nt(q_ref[...], k_ref[...]) * scale
    o_ref[...] = _softmax_pv([(s, v_ref[...])])


def ctx_attention(z_c, batch, l):
    h8 = NA_HEADS
    return pl.pallas_call(
        functools.partial(_dense_attn_kernel, scale=NA_HEAD_DIM ** -0.5),
        out_shape=jax.ShapeDtypeStruct((batch * l, NA_WIDTH), F32),
        grid=(batch, h8),
        in_specs=[pl.BlockSpec((l, LANES), lambda b, h: (b, EV_Q_BLK + h)),
                  pl.BlockSpec((l, LANES), lambda b, h: (b, EV_Q_BLK + h8 + h)),
                  pl.BlockSpec((l, LANES), lambda b, h: (b, EV_Q_BLK + 2 * h8 + h))],
        out_specs=pl.BlockSpec((l, LANES), lambda b, h: (b, h)),
        compiler_params=_cparams(2),
        name="ctx_attention",
    )(z_c, z_c, z_c)


def _token_shift(u, mu):
    t = u.shape[0]
    row = lax.broadcasted_iota(jnp.int32, u.shape, 0)
    prev = jnp.where(row == 0, 0.0, pltpu.roll(u, 1, axis=0))
    nxt = jnp.where(row == t - 1, 0.0, pltpu.roll(u, t - 1, axis=0))
    return u + mu[0:1] * (prev - u) + mu[1:2] * (nxt - u)


def _head_sum(x):
    first = lax.broadcasted_iota(jnp.int32, x.shape, 1) < RW_HEAD_DIM
    s0 = jnp.sum(jnp.where(first, x, 0.0), axis=-1, keepdims=True)
    s1 = jnp.sum(jnp.where(first, 0.0, x), axis=-1, keepdims=True)
    return jnp.where(first, s0, s1)


def _softplus(x):
    return jnp.maximum(x, 0.0) + jnp.log1p(jnp.exp(-jnp.abs(x)))


def _rw_terms_kernel(ur_ref, uk_ref, uv_ref, uw_ref, ua_ref, mur_ref, muk_ref, muv_ref, muw_ref, mua_ref,
                     w0_ref, w2_ref, a0_ref, a2_ref, kk_ref, ka_ref, rk_ref,
                     r_o, v_o, kkn_o, dec0_o, dec1_o, kd0_o, kd1_o, b0_o, b1_o, bonus_o):
    r = _token_shift(ur_ref[...], mur_ref[...])
    k = _token_shift(uk_ref[...], muk_ref[...])
    v = _token_shift(uv_ref[...], muv_ref[...])
    wd = jnp.tanh(_token_shift(uw_ref[...], muw_ref[...]))
    ad = _token_shift(ua_ref[...], mua_ref[...])
    kk = k * kk_ref[...]
    kk = kk / jnp.maximum(jnp.sqrt(_head_sum(kk * kk)), 1e-12)
    r_o[...] = r
    v_o[...] = v
    kkn_o[...] = kk
    kd_sum = jnp.zeros_like(k)
    for d, (dec_o, kd_o, b_o) in enumerate(((dec0_o, kd0_o, b0_o), (dec1_o, kd1_o, b1_o))):
        w = w0_ref[d:d + 1, :] + _dot(wd, w2_ref[d])
        dec_o[...] = jnp.exp(-jnp.exp(-_softplus(-w) - 0.5))
        a = _sigmoid(a0_ref[d:d + 1, :] + _dot(ad, a2_ref[d]))
        kd = k * (1.0 + (a - 1.0) * ka_ref[...])
        kd_o[...] = kd
        b_o[...] = kk * a
        kd_sum = kd_sum + kd
    bonus_o[...] = _head_sum(r * kd_sum * rk_ref[...]) * v


def rwkv_terms(rw, n_seq, t, mu, w0, w2p, a0, a2p, k_k, k_a, r_k):
    npair = RW_WIDTH // LANES
    col = lambda off: pl.BlockSpec((t, LANES), lambda b, p, off=off: (b, EV_RW_BLK + off + p))
    fix = lambda off: pl.BlockSpec((t, LANES), lambda b, p, off=off: (b, EV_RW_BLK + off))
    mcol = lambda off: pl.BlockSpec((2, LANES), lambda b, p, off=off: (0, off + p))
    mfix = lambda off: pl.BlockSpec((2, LANES), lambda b, p, off=off: (0, off))
    vec2 = pl.BlockSpec((2, LANES), lambda b, p: (0, p))
    vec1 = pl.BlockSpec((1, LANES), lambda b, p: (0, p))
    mat = pl.BlockSpec((2, LANES, LANES), lambda b, p: (0, 0, p))
    out = pl.BlockSpec((t, LANES), lambda b, p: (b, p))
    return pl.pallas_call(
        _rw_terms_kernel,
        out_shape=[jax.ShapeDtypeStruct((n_seq * t, RW_WIDTH), F32)] * 10,
        grid=(n_seq, npair),
        in_specs=[col(0), col(npair), col(2 * npair), fix(3 * npair), fix(3 * npair + 1),
                  mcol(0), mcol(npair), mcol(2 * npair), mfix(3 * npair), mfix(3 * npair + 1),
                  vec2, mat, vec2, mat, vec1, vec1, vec1],
        out_specs=[out] * 10,
        compiler_params=_cparams(2),
        name="rwkv_terms",
    )(rw, rw, rw, rw, rw, mu, mu, mu, mu, mu, w0, w2p, a0, a2p, k_k, k_a, r_k)


def _rw_scan_kernel(r_ref, w_ref, k_ref, v_ref, a_ref, b_ref, y_ref, s_ref, *, chunk):
    n = RW_HEAD_DIM

    @pl.when(pl.program_id(0) == 0)
    def _():
        s_ref[...] = jnp.zeros_like(s_ref)

    def step(t, carry):
        def sa_body(kg, sa):
            for ki in range(8):
                k = kg * 8 + ki
                sa = sa + s_ref[k] * (-a_ref[t, pl.ds(k, 1), :])
            return sa

        sa = lax.fori_loop(0, n // 8, sa_body, jnp.zeros((n, LANES), F32))
        vt = v_ref[t]

        def upd_body(kg, y):
            for ki in range(8):
                k = kg * 8 + ki
                row = pl.ds(k, 1)
                sk = s_ref[k] * w_ref[t, row, :] + sa * b_ref[t, row, :] + vt * k_ref[t, row, :]
                s_ref[k] = sk
                y = y + sk * r_ref[t, row, :]
            return y

        y_ref[t] = lax.fori_loop(0, n // 8, upd_body, jnp.zeros((n, LANES), F32))
        return carry

    lax.fori_loop(0, chunk, step, 0)


def rwkv_scan(r, w, k, v, kkn, b):
    ttot = r.shape[0]
    n = RW_HEAD_DIM
    chunk = RW_SCAN_CHUNK
    spec = pl.BlockSpec((chunk, n, LANES), lambda i: (i, 0, 0))
    return pl.pallas_call(
        functools.partial(_rw_scan_kernel, chunk=chunk),
        out_shape=jax.ShapeDtypeStruct((ttot, n, LANES), F32),
        grid=(ttot // chunk,),
        in_specs=[spec] * 6,
        out_specs=spec,
        scratch_shapes=[pltpu.VMEM((n, n, LANES), F32)],
        compiler_params=_cparams(1),
        name="rwkv_scan",
    )(r, w, k, v, kkn, b)


def _rw_readout_kernel(y_ref, bonus_ref, lw_ref, lb_ref, o_ref):
    y = y_ref[...]
    mu = _head_sum(y) * (1.0 / RW_HEAD_DIM)
    d = y - mu
    var = _head_sum(d * d) * (1.0 / RW_HEAD_DIM)
    o_ref[...] = d * lax.rsqrt(var + RW_GN_EPS) * lw_ref[...] + lb_ref[...] + bonus_ref[...]


def rwkv_readout(y, bonus, ln_w, ln_b, tm=1024):
    m = y.shape[0]
    tm = min(tm, m)
    blk = pl.BlockSpec((tm, LANES), lambda i, p: (i, p))
    vec = pl.BlockSpec((1, LANES), lambda i, p: (0, p))
    return pl.pallas_call(
        _rw_readout_kernel,
        out_shape=jax.ShapeDtypeStruct((m, RW_WIDTH), F32),
        grid=(m // tm, RW_WIDTH // LANES),
        in_specs=[blk, blk, vec, vec],
        out_specs=blk,
        compiler_params=_cparams(2),
        name="rwkv_readout",
    )(y, bonus, ln_w.reshape(1, -1), ln_b.reshape(1, -1))


def _rms(x, w):
    return x * lax.rsqrt(jnp.mean(x * x, axis=-1, keepdims=True) + NORM_EPS) * w


def _rope(x, cos, sin):
    odd = (lax.broadcasted_iota(jnp.int32, x.shape, 1) & 1) == 1
    swapped = jnp.where(odd, pltpu.roll(x, 1, axis=1), pltpu.roll(x, LANES - 1, axis=1))
    return x * cos + swapped * sin


def _mla_q_kernel(c_ref, nw_ref, w_ref, cos_ref, sin_ref, q_ref, h_ref):
    @pl.when(pl.program_id(1) == 0)
    def _():
        h_ref[...] = _rms(c_ref[...], nw_ref[...]).astype(BF16)

    acc = jnp.dot(h_ref[...], w_ref[...], preferred_element_type=F32)
    q_ref[:, :MLA_NOPE] = acc[:, :MLA_NOPE].astype(BF16)
    q_ref[:, MLA_NOPE:] = _rope(acc[:, MLA_NOPE:], cos_ref[...], sin_ref[...]).astype(BF16)


def mla_queries(a_cols, q_norm, w_uq_pad, cos, sin, t, tm=512):
    m = a_cols.shape[0]
    tm = min(tm, t)
    nt = t // tm
    return pl.pallas_call(
        _mla_q_kernel,
        out_shape=jax.ShapeDtypeStruct((m, MLA_HEADS * MLA_QK_PAD), BF16),
        grid=(m // tm, MLA_HEADS),
        in_specs=[pl.BlockSpec((tm, MLA_Q_RANK), lambda i, h: (i, GATE_COLS // MLA_Q_RANK)),
                  pl.BlockSpec((1, MLA_Q_RANK), lambda i, h: (0, 0)),
                  pl.BlockSpec((MLA_Q_RANK, MLA_QK_PAD), lambda i, h: (0, h)),
                  pl.BlockSpec((tm, LANES), lambda i, h: (i % nt, 0)),
                  pl.BlockSpec((tm, LANES), lambda i, h: (i % nt, 0))],
        out_specs=pl.BlockSpec((tm, MLA_QK_PAD), lambda i, h: (i, h)),
        scratch_shapes=[pltpu.VMEM((tm, MLA_Q_RANK), BF16)],
        compiler_params=_cparams(2),
        name="mla_queries",
    )(a_cols, q_norm.reshape(1, -1), w_uq_pad, cos, sin)


def _mla_kv_kernel(c_ref, pe_ref, nw_ref, w_ref, cos_ref, sin_ref, k_ref, v_ref, h_ref):
    @pl.when(pl.program_id(1) == 0)
    def _():
        h_ref[...] = _rms(c_ref[...], nw_ref[...]).astype(BF16)

    acc = jnp.dot(h_ref[...], w_ref[...], preferred_element_type=F32)
    k_ref[:, :MLA_NOPE] = acc[:, :MLA_NOPE].astype(BF16)
    k_ref[:, MLA_NOPE:] = _rope(pe_ref[...], cos_ref[...], sin_ref[...]).astype(BF16)
    v_ref[...] = acc[:, MLA_NOPE:].astype(BF16)


def mla_keys_values(a_cols, kv_norm, w_ukv, cos, sin, t, tm=512):
    m = a_cols.shape[0]
    tm = min(tm, t)
    nt = t // tm
    return pl.pallas_call(
        _mla_kv_kernel,
        out_shape=[jax.ShapeDtypeStruct((m, MLA_HEADS * MLA_QK_PAD), BF16),
                   jax.ShapeDtypeStruct((m, MLA_WIDTH), BF16)],
        grid=(m // tm, MLA_HEADS),
        in_specs=[pl.BlockSpec((tm, MLA_KV_RANK), lambda i, h: (i, (GATE_COLS + MLA_Q_RANK) // MLA_KV_RANK)),
                  pl.BlockSpec((tm, LANES), lambda i, h: (i, OD_A_BLK + (MLA_Q_RANK + MLA_KV_RANK) // LANES)),
                  pl.BlockSpec((1, MLA_KV_RANK), lambda i, h: (0, 0)),
                  pl.BlockSpec((MLA_KV_RANK, MLA_NOPE + MLA_V), lambda i, h: (0, h)),
                  pl.BlockSpec((tm, LANES), lambda i, h: (i % nt, 0)),
                  pl.BlockSpec((tm, LANES), lambda i, h: (i % nt, 0))],
        out_specs=[pl.BlockSpec((tm, MLA_QK_PAD), lambda i, h: (i, h)),
                   pl.BlockSpec((tm, MLA_V), lambda i, h: (i, h))],
        scratch_shapes=[pltpu.VMEM((tm, MLA_KV_RANK), BF16)],
        compiler_params=_cparams(2),
        name="mla_keys_values",
    )(a_cols, a_cols, kv_norm.reshape(1, -1), w_ukv, cos, sin)


def _mla_attn_kernel(q_ref, k_ref, v_ref, kc_ref, vc_ref, o_ref, *, scale):
    q = q_ref[...]
    s_lat = _dot_nt(q, k_ref[...]) * scale
    s_ctx = _dot_nt(q, kc_ref[...]) * scale
    o_ref[...] = _softmax_pv([(s_lat, v_ref[...]), (s_ctx, vc_ref[...])])


def mla_attention(q, k, v, kc, vc, batch, t, l, tq=512):
    tq = min(tq, t)
    nq = t // tq
    return pl.pallas_call(
        functools.partial(_mla_attn_kernel, scale=(MLA_NOPE + MLA_ROPE) ** -0.5),
        out_shape=jax.ShapeDtypeStruct((batch * t, MLA_WIDTH), F32),
        grid=(batch, MLA_HEADS, nq),
        in_specs=[pl.BlockSpec((tq, MLA_QK_PAD), lambda b, h, i: (b * nq + i, h)),
                  pl.BlockSpec((t, MLA_QK_PAD), lambda b, h, i: (b, h)),
                  pl.BlockSpec((t, MLA_V), lambda b, h, i: (b, h)),
                  pl.BlockSpec((l, MLA_QK_PAD), lambda b, h, i: (b, h)),
                  pl.BlockSpec((l, MLA_V), lambda b, h, i: (b, h))],
        out_specs=pl.BlockSpec((tq, MLA_V), lambda b, h, i: (b * nq + i, h)),
        compiler_params=_cparams(3),
        name="mla_attention",
    )(q, k, v, kc, vc)


def _rope_tables(t):
    tok = np.arange(t)
    pos = np.stack([tok // GRID_W, tok % GRID_W], axis=-1).astype(np.float32)
    n_freq = MLA_ROPE // 4
    inv = (ROPE_BASE ** (-jnp.arange(n_freq, dtype=F32) / n_freq))
    ang = (jnp.asarray(pos)[:, :, None] * inv).reshape(t, MLA_ROPE // 2)
    cos = jnp.repeat(jnp.cos(ang), 2, axis=-1)
    sin = jnp.repeat(jnp.sin(ang), 2, axis=-1) * jnp.tile(jnp.array([-1.0, 1.0], F32), MLA_ROPE // 2)
    pad = ((0, 0), (0, LANES - MLA_ROPE))
    return jnp.pad(cos, pad), jnp.pad(sin, pad)


def _hg_kernel(q_ref, f_ref, v_ref, lb_ref, s0_ref, *rest, reverse, finalize, emit_state, nblk):
    st_ref = rest[-1]
    rest = rest[:-1]
    if finalize:
        prev_ref, nw_ref = rest[0], rest[1]
        rest = rest[2:]
    o_ref = rest[0]
    blk = pl.program_id(2)

    @pl.when(blk == 0)
    def _():
        st_ref[...] = s0_ref[...]

    bt = q_ref.shape[0]
    lb = lb_ref[...]
    fz = f_ref[...]
    log_sig = jnp.minimum(fz, 0.0) - jnp.log1p(jnp.exp(-jnp.abs(fz)))
    la = jnp.log(lb)
    lbb = jnp.log1p(-lb) + log_sig
    log_f = jnp.maximum(la, lbb) + jnp.log1p(jnp.exp(-jnp.abs(la - lbb)))
    kf = (1.0 - lb) * _sigmoid(-fz)

    ri = lax.broadcasted_iota(jnp.int32, (bt, bt), 0)
    ci = lax.broadcasted_iota(jnp.int32, (bt, bt), 1)
    same = (ri // HG_CHUNK) == (ci // HG_CHUNK)
    tri = (same & ((ci >= ri) if reverse else (ci <= ri))).astype(F32)
    cum = jnp.dot(tri, log_f, preferred_element_type=F32, precision=lax.Precision.HIGHEST)
    tot = jnp.dot(same.astype(F32), log_f, preferred_element_type=F32, precision=lax.Precision.HIGHEST)
    q_in = q_ref[...] * jnp.exp(cum)
    k_in = kf * jnp.exp(-cum)
    k_end = kf * jnp.exp(tot - cum)
    v = v_ref[...]
    o_intra = _dot(_dot_nt(q_in, k_in) * tri, v)

    st = st_ref[...]
    nchunk = bt // HG_CHUNK
    for c in (range(nchunk - 1, -1, -1) if reverse else range(nchunk)):
        sl = slice(c * HG_CHUNK, (c + 1) * HG_CHUNK)
        o_c = o_intra[sl] + _dot_nt(q_in[sl], st)
        if finalize:
            o_c = o_c + prev_ref[sl, :]
            o_c = o_c * lax.rsqrt(jnp.mean(o_c * o_c, axis=-1, keepdims=True) + NORM_EPS) * nw_ref[...]
        o_ref[sl, :] = o_c
        st = st * jnp.exp(tot[c * HG_CHUNK:c * HG_CHUNK + 1, :]) + _dot_tn(v[sl], k_end[sl])
    st_ref[...] = st
    if emit_state:
        rest[1][...] = st


def hgrn2_pass(hg, lb, s0, batch, t, f_off, reverse, prev=None, norm_w=None, emit_state=False):
    bt = min(HG_BLOCK, t)
    nblk = t // bt
    h8 = HG_HEADS
    finalize = prev is not None
    pos = (lambda i: nblk - 1 - i) if reverse else (lambda i: i)
    blk = lambda off: pl.BlockSpec((bt, LANES), lambda b, h, i, off=off: (b * nblk + pos(i), off + h))
    state = pl.BlockSpec((None, None, HG_DV, HG_DK), lambda b, h, i: (b, h, 0, 0))
    zblk = lambda off: blk(OD_HG_BLK + off)
    ins = [hg, hg, hg, lb.reshape(1, -1), s0]
    specs = [zblk(0), zblk(f_off), zblk(3 * h8), pl.BlockSpec((1, LANES), lambda b, h, i: (0, h)), state]
    if finalize:
        ins += [prev, norm_w.reshape(1, -1)]
        specs += [blk(0), pl.BlockSpec((1, HG_DV), lambda b, h, i: (0, 0))]
    out_shape = [jax.ShapeDtypeStruct((batch * t, HG_WIDTH), F32)]
    out_specs = [blk(0)]
    if emit_state:
        out_shape.append(jax.ShapeDtypeStruct((batch, h8, HG_DV, HG_DK), F32))
        out_specs.append(state)
    return pl.pallas_call(
        functools.partial(_hg_kernel, reverse=reverse, finalize=finalize, emit_state=emit_state, nblk=nblk),
        out_shape=out_shape,
        grid=(batch, h8, nblk),
        in_specs=specs,
        out_specs=out_specs,
        scratch_shapes=[pltpu.VMEM((HG_DV, HG_DK), F32)],
        compiler_params=_cparams(3),
        name="hgrn2_" + ("bwd" if reverse else "fwd"),
    )(*ins)


def _scan_layout(lat_f, lat_b, ctx_f, ctx_b, batch, t, l):
    def seq(c, x, rev):
        c = c.reshape(batch, l, RW_HEADS, RW_HEAD_DIM)
        x = x.reshape(batch, t, RW_HEADS, RW_HEAD_DIM)
        if rev:
            c, x = c[:, ::-1], x[:, ::-1]
        return jnp.concatenate([c, x], axis=1)

    s = jnp.stack([seq(ctx_f, lat_f, False), seq(ctx_b, lat_b, True)], axis=0)
    return s.transpose(2, 4, 0, 1, 3).reshape(l + t, RW_HEAD_DIM, 2 * batch * RW_HEADS)


def _scan_unlayout(y, batch, t, l):
    y = y.reshape(l + t, RW_HEAD_DIM, 2, batch, RW_HEADS).transpose(2, 3, 0, 4, 1)
    yc, yl = y[:, :, :l], y[:, :, l:]
    yl = yl[0] + yl[1][:, ::-1]
    yc = yc[0] + yc[1][:, ::-1]
    return yl.reshape(batch * t, RW_WIDTH), yc.reshape(batch * l, RW_WIDTH)


def _even_layer(x, xc, mod, norm_w, w_in, w_out, rpb, rw_params, batch, t, l):
    d = x.shape[1]
    s2 = 3 * NA_WIDTH + RW_SHIFT_COLS
    zero = jnp.zeros((d, RW_PAD_COLS - RW_SHIFT_COLS), F32)
    w_ext = jnp.concatenate([w_in[:, s2:], w_in[:, :s2], zero], axis=1).astype(BF16)
    tiles_per_seq = max(t // 1024, 1)
    z = in_projection(x, norm_w, mod, lambda i: i // tiles_per_seq, w_ext, "even_in_proj")
    z_c = in_projection(xc, norm_w, mod, lambda i: batch, w_ext, "even_in_proj_ctx")

    y_na = na_attention(z, z_c, rpb, batch, t, l)
    yc_na = ctx_attention(z_c, batch, l)

    mu, w0, w2, a0, a2, k_k, k_a, r_k, ln_w, ln_b = rw_params
    mu = jnp.pad(mu, ((0, 0), (0, RW_PAD_COLS - RW_SHIFT_COLS)))
    zpad = jnp.zeros((RW_RANK, RW_WIDTH), F32)
    w2p = jnp.stack([jnp.concatenate([w2[0], zpad]), jnp.concatenate([zpad, w2[1]])])
    a2p = jnp.stack([jnp.concatenate([a2[0], zpad]), jnp.concatenate([zpad, a2[1]])])
    vecs = (mu, w0, w2p, a0, a2p, k_k.reshape(1, -1), k_a.reshape(1, -1), r_k.reshape(1, -1))
    r, v, kkn, dec0, dec1, kd0, kd1, b0, b1, bonus = rwkv_terms(z, batch, t, *vecs)
    rc, vc, kknc, dec0c, dec1c, kd0c, kd1c, b0c, b1c, bonus_c = rwkv_terms(z_c, batch, l, *vecs)
    lay = functools.partial(_scan_layout, batch=batch, t=t, l=l)
    y = rwkv_scan(lay(r, r, rc, rc), lay(dec0, dec1, dec0c, dec1c), lay(kd0, kd1, kd0c, kd1c),
                  lay(v, v, vc, vc), lay(kkn, kkn, kknc, kknc), lay(b0, b1, b0c, b1c))
    y_l, y_c = _scan_unlayout(y, batch, t, l)
    y_rw = rwkv_readout(y_l, bonus, ln_w, ln_b)
    yc_rw = rwkv_readout(y_c, bonus_c, ln_w, ln_b)

    w_out = w_out.astype(BF16)
    rows_per_seq = max(t // 256, 1)
    x = out_projection(y_na, y_rw, z, w_out, x, mod, lambda i: i // rows_per_seq, "even_out_proj")
    xc = out_projection(yc_na, yc_rw, z_c, w_out, xc, mod, lambda i: batch, "even_out_proj_ctx")
    return x, xc


def _odd_layer(x, xc, mod, norm_w, w_in, w_out, q_norm, w_uq, kv_norm, w_ukv, lb, hg_norm_w, final_w,
               batch, t, l):
    d = x.shape[1]
    o3 = MLA_Q_RANK + MLA_KV_RANK + MLA_ROPE
    o4 = o3 + 4 * HG_FDIM
    zero = jnp.zeros((d, MLA_A_COLS - o3), F32)
    w_ext = jnp.concatenate([w_in[:, o4:], w_in[:, :o3], zero, w_in[:, o3:o4]], axis=1).astype(BF16)
    tiles_per_seq = max(t // 1024, 1)
    z = in_projection(x, norm_w, mod, lambda i: i // tiles_per_seq, w_ext, "odd_in_proj")
    z_c = in_projection(xc, norm_w, mod, lambda i: batch, w_ext, "odd_in_proj_ctx")

    cos, sin = _rope_tables(t)
    qk = MLA_NOPE + MLA_ROPE
    w_uq_pad = jnp.pad(w_uq.reshape(MLA_Q_RANK, MLA_HEADS, qk), ((0, 0), (0, 0), (0, MLA_QK_PAD - qk)))
    w_uq_pad = w_uq_pad.reshape(MLA_Q_RANK, MLA_HEADS * MLA_QK_PAD).astype(BF16)
    w_ukv = w_ukv.astype(BF16)
    q = mla_queries(z, q_norm, w_uq_pad, cos, sin, t)
    k, v = mla_keys_values(z, kv_norm, w_ukv, cos, sin, t)
    ones = jnp.ones((l, LANES), F32)
    kc, vc = mla_keys_values(z_c, kv_norm, w_ukv, ones, jnp.zeros_like(ones), l)
    y_mla = mla_attention(q, k, v, kc, vc, batch, t, l)

    s_zero = jnp.zeros((batch, HG_HEADS, HG_DV, HG_DK), F32)
    _, s_f = hgrn2_pass(z_c, lb, s_zero, batch, l, HG_HEADS, False, emit_state=True)
    _, s_b = hgrn2_pass(z_c, lb, s_zero, batch, l, 2 * HG_HEADS, True, emit_state=True)
    o_f, = hgrn2_pass(z, lb, s_f, batch, t, HG_HEADS, False)
    y_hg, = hgrn2_pass(z, lb, s_b, batch, t, 2 * HG_HEADS, True, prev=o_f, norm_w=hg_norm_w)

    rows_per_seq = max(t // 256, 1)
    return out_projection(y_mla, y_hg, z, w_out.astype(BF16), x, mod, lambda i: i // rows_per_seq,
                          "odd_out_proj", final_w=final_w)


def kernel(x, c, ctx, c_ctx, ada_w, ada_b, norm_w, e_w_in, e_w_out, na_rpb, rw_mu, rw_w0, rw_w2, rw_a0, rw_a2, rw_k_k, rw_k_a, rw_r_k, rw_ln_w, rw_ln_b, o_w_in, o_w_out, mla_q_norm, mla_w_uq, mla_kv_norm, mla_w_ukv, hg_lower_bounds, hg_norm_w, final_norm_w):
    batch, t, d = x.shape
    l = ctx.shape[1]
    assert ada_w.shape[0] == 2, "one even and one odd layer"
    cond = jnp.concatenate([c, c_ctx[None, :], jnp.zeros((8 - batch - 1, d), F32)], axis=0)
    mod = modulation(cond, ada_w, ada_b)
    s = jax.nn.softmax(hg_lower_bounds.astype(F32), axis=0)
    lower = jnp.cumsum(s, axis=0) - s[0]

    xf, xcf = x.reshape(batch * t, d), ctx.reshape(batch * l, d)
    rw_params = (rw_mu[0], rw_w0[0], rw_w2[0], rw_a0[0], rw_a2[0], rw_k_k[0], rw_k_a[0], rw_r_k[0],
                 rw_ln_w[0], rw_ln_b[0])
    xf, xcf = _even_layer(xf, xcf, mod[0].reshape(8, 1, 3 * d), norm_w[0], e_w_in[0], e_w_out[0], na_rpb[0],
                          rw_params, batch, t, l)
    out = _odd_layer(xf, xcf, mod[1].reshape(8, 1, 3 * d), norm_w[1], o_w_in[0], o_w_out[0], mla_q_norm[0],
                     mla_w_uq[0], mla_kv_norm[0], mla_w_ukv[0], lower[1], hg_norm_w[0], final_norm_w,
                     batch, t, l)
    return out.reshape(batch, t, d)
```

```python
import functools
import math

import numpy as np
import jax
import jax.numpy as jnp
from jax import lax
from jax.experimental import pallas as pl
from jax.experimental.pallas import tpu as pltpu

F32 = jnp.float32
BF16 = jnp.bfloat16

GRID_W = 64
NORM_EPS = 1e-6
ROPE_BASE = 10000.0

NA_HEADS = 8
NA_HEAD_DIM = 128
NA_WIDTH = NA_HEADS * NA_HEAD_DIM
NA_WIN_ROWS = 8
NA_WIN_COLS = 16
NA_QROWS = 4
NA_KROWS = 12

RW_HEAD_DIM = 64
RW_HEADS = 16
RW_WIDTH = RW_HEADS * RW_HEAD_DIM
RW_RANK = 64
RW_SHIFT_COLS = 3 * RW_WIDTH + 4 * RW_RANK
RW_PAD_COLS = 3584
RW_GN_EPS = 64e-5
RW_SCAN_CHUNK = 32

MLA_HEADS = 8
MLA_Q_RANK = 512
MLA_KV_RANK = 512
MLA_NOPE = 128
MLA_ROPE = 64
MLA_V = 128
MLA_QK_PAD = 256
MLA_WIDTH = MLA_HEADS * MLA_V
MLA_A_COLS = 1536

HG_HEADS = 8
HG_DK = 128
HG_DV = 128
HG_FDIM = HG_HEADS * HG_DK
HG_WIDTH = HG_HEADS * HG_DV
HG_CHUNK = 32
HG_BLOCK = 256
HG_STEP_HEADS = 2

LANES = 128

GATE_COLS = 2048
EV_Q_BLK = GATE_COLS // LANES
EV_RW_BLK = EV_Q_BLK + 3 * NA_WIDTH // LANES
OD_A_BLK = GATE_COLS // LANES
OD_HG_BLK = OD_A_BLK + MLA_A_COLS // LANES
VMEM_LIMIT = 48 * 1024 * 1024
NEG_INF = -1e30
LOG2E = math.log2(math.e)


def _cparams(n_axes):
    return pltpu.CompilerParams(dimension_semantics=("arbitrary",) * n_axes, vmem_limit_bytes=VMEM_LIMIT)


def _sigmoid(x):
    return 1.0 / (1.0 + jnp.exp(-x))


def _silu(x):
    return x * _sigmoid(x)


def _dot(a, b):
    return jnp.dot(a.astype(BF16), b.astype(BF16), preferred_element_type=F32)


def _dot_nt(a, b):
    return lax.dot_general(a.astype(BF16), b.astype(BF16), (((1,), (1,)), ((), ())), preferred_element_type=F32)


def _dot_tn(a, b):
    return lax.dot_general(a.astype(BF16), b.astype(BF16), (((0,), (0,)), ((), ())), preferred_element_type=F32)


def _mod_kernel(c_ref, w_ref, b_ref, o_ref):
    o_ref[...] = _dot(_silu(c_ref[...]), w_ref[...]) + b_ref[...]


def modulation(cond, ada_w, ada_b, tn=512):
    depth, d, n = ada_w.shape
    return pl.pallas_call(
        _mod_kernel,
        out_shape=jax.ShapeDtypeStruct((depth, 8, n), F32),
        grid=(depth, n // tn),
        in_specs=[pl.BlockSpec((8, d), lambda l, j: (0, 0)),
                  pl.BlockSpec((None, d, tn), lambda l, j: (l, 0, j)),
                  pl.BlockSpec((None, 1, tn), lambda l, j: (l, 0, j))],
        out_specs=pl.BlockSpec((None, 8, tn), lambda l, j: (l, 0, j)),
        compiler_params=_cparams(2),
        name="adaln_modulation",
    )(cond, ada_w, ada_b.reshape(depth, 1, n))


def _inproj_kernel(x_ref, nw_ref, shift_ref, scale_ref, w_ref, o_ref, h_ref):
    @pl.when(pl.program_id(1) == 0)
    def _():
        x = x_ref[...]
        y = x * lax.rsqrt(jnp.mean(x * x, axis=-1, keepdims=True) + NORM_EPS) * nw_ref[...]
        h_ref[...] = (y * (1.0 + scale_ref[...]) + shift_ref[...]).astype(BF16)

    o_ref[...] = jnp.dot(h_ref[...], w_ref[...], preferred_element_type=F32)


def in_projection(x, norm_w, mod, mod_row, w, name, tm=1024, tn=512):
    m, d = x.shape
    n = w.shape[1]
    tm = min(tm, m)
    return pl.pallas_call(
        _inproj_kernel,
        out_shape=jax.ShapeDtypeStruct((m, n), F32),
        grid=(m // tm, n // tn),
        in_specs=[pl.BlockSpec((tm, d), lambda i, j: (i, 0)),
                  pl.BlockSpec((1, d), lambda i, j: (0, 0)),
                  pl.BlockSpec((None, 1, d), lambda i, j: (mod_row(i), 0, 0)),
                  pl.BlockSpec((None, 1, d), lambda i, j: (mod_row(i), 0, 1)),
                  pl.BlockSpec((d, tn), lambda i, j: (0, j))],
        out_specs=pl.BlockSpec((tm, tn), lambda i, j: (i, j)),
        scratch_shapes=[pltpu.VMEM((tm, d), BF16)],
        compiler_params=_cparams(2),
        name=name,
    )(x, norm_w.reshape(1, d), mod, mod, w)


def _outproj_kernel(ya_ref, yb_ref, g_ref, w_ref, x_ref, gm_ref, *rest, ka, final):
    o_ref = rest[-1]
    sg = _silu(g_ref[...])
    acc = _dot(ya_ref[...].astype(F32) * sg[:, :ka], w_ref[:ka, :])
    acc += _dot(yb_ref[...].astype(F32) * sg[:, ka:], w_ref[ka:, :])
    out = x_ref[...] + gm_ref[...] * acc
    if final:
        out = out * lax.rsqrt(jnp.mean(out * out, axis=-1, keepdims=True) + NORM_EPS) * rest[0][...]
    o_ref[...] = out


def out_projection(ya, yb, gate, w, x, mod, mod_row, name, final_w=None, tm=256):
    m, d = x.shape
    ka, kb = ya.shape[1], yb.shape[1]
    tm = min(tm, m)
    ins = [ya, yb, gate, w, x, mod]
    specs = [pl.BlockSpec((tm, ka), lambda i: (i, 0)),
             pl.BlockSpec((tm, kb), lambda i: (i, 0)),
             pl.BlockSpec((tm, ka + kb), lambda i: (i, 0)),
             pl.BlockSpec((ka + kb, d), lambda i: (0, 0)),
             pl.BlockSpec((tm, d), lambda i: (i, 0)),
             pl.BlockSpec((None, 1, d), lambda i: (mod_row(i), 0, 2))]
    if final_w is not None:
        ins.append(final_w.reshape(1, d))
        specs.append(pl.BlockSpec((1, d), lambda i: (0, 0)))
    return pl.pallas_call(
        functools.partial(_outproj_kernel, ka=ka, final=final_w is not None),
        out_shape=jax.ShapeDtypeStruct((m, d), F32),
        grid=(m // tm,),
        in_specs=specs,
        out_specs=pl.BlockSpec((tm, d), lambda i: (i, 0)),
        compiler_params=_cparams(1),
        name=name,
    )(*ins)


def _softmax_pv(parts, scale):
    c = scale * LOG2E
    logits = [s * c if bias is None else s * c + bias * LOG2E for s, bias, _ in parts]
    m = functools.reduce(jnp.maximum, [jnp.max(x, axis=-1, keepdims=True) for x in logits])
    ps = [jnp.exp2(x - m) for x in logits]
    denom = functools.reduce(lambda a, b: a + b, [jnp.sum(p, axis=-1, keepdims=True) for p in ps])
    num = functools.reduce(lambda a, b: a + b, [_dot(p, v) for p, (_, _, v) in zip(ps, parts)])
    return num / denom


def _na_kernel(g_ref, q_ref, k_ref, v_ref, kc_ref, vc_ref, bias_ref, o_ref, *, rows, scale):
    del g_ref
    j = pl.program_id(1)
    start = pl.multiple_of(jnp.clip(NA_QROWS * j - NA_WIN_ROWS // 2, 0, rows - NA_KROWS) * GRID_W, GRID_W)
    nk = NA_KROWS * GRID_W
    q = q_ref[...]
    s_loc = _dot_nt(q, k_ref[pl.ds(start, nk), :])
    s_ctx = _dot_nt(q, kc_ref[...])
    o_ref[...] = _softmax_pv([(s_loc, bias_ref[...], v_ref[pl.ds(start, nk), :]), (s_ctx, None, vc_ref[...])], scale)


def _na_geometry(rows):
    kh = min(NA_WIN_ROWS, rows)
    geos, ids = [], []
    for j in range(rows // NA_QROWS):
        ks = int(np.clip(NA_QROWS * j - NA_WIN_ROWS // 2, 0, rows - NA_KROWS))
        r = NA_QROWS * j + np.arange(NA_QROWS)[:, None]
        kr = ks + np.arange(NA_KROWS)[None, :]
        r0 = np.clip(r - kh // 2, 0, rows - kh)
        valid = (kr >= r0) & (kr < r0 + kh)
        dy = np.where(valid, kr - r + NA_WIN_ROWS - 1, 0)
        key = (dy.tobytes(), valid.tobytes())
        if key not in [g[0] for g in geos]:
            geos.append((key, dy, valid))
        ids.append([g[0] for g in geos].index(key))
    return np.array(ids, np.int32), np.stack([g[1] for g in geos]), np.stack([g[2] for g in geos])


def _na_bias(rpb, rows):
    col = np.arange(GRID_W)
    cs = np.clip(col - NA_WIN_COLS // 2, 0, GRID_W - NA_WIN_COLS)
    kc = col[None, :]
    in_win = (kc >= cs[:, None]) & (kc < cs[:, None] + NA_WIN_COLS)
    dx = np.where(in_win, kc - col[:, None] + NA_WIN_COLS - 1, 0)
    tab = jnp.where(in_win[None, None], rpb[:, :, dx], NEG_INF)
    ids, dy, valid = _na_geometry(rows)
    blocks = jnp.where(valid[None, :, :, :, None, None], tab[:, dy], NEG_INF)
    g = dy.shape[0]
    blocks = blocks.transpose(0, 1, 2, 4, 3, 5).reshape(rpb.shape[0], g, NA_QROWS * GRID_W, NA_KROWS * GRID_W)
    return ids, blocks


def na_attention(z, z_c, rpb, batch, t, l):
    rows = t // GRID_W
    nj = rows // NA_QROWS
    tq = NA_QROWS * GRID_W
    ids, bias = _na_bias(rpb, rows)
    ids = jnp.asarray(ids)
    h8 = NA_HEADS
    return pl.pallas_call(
        functools.partial(_na_kernel, rows=rows, scale=NA_HEAD_DIM ** -0.5),
        out_shape=jax.ShapeDtypeStruct((batch * t, NA_WIDTH), F32),
        grid_spec=pltpu.PrefetchScalarGridSpec(
            num_scalar_prefetch=1,
            grid=(h8, nj, batch),
            in_specs=[pl.BlockSpec((tq, LANES), lambda h, j, b, g: (b * nj + j, EV_Q_BLK + h)),
                      pl.BlockSpec((t, LANES), lambda h, j, b, g: (b, EV_Q_BLK + h8 + h)),
                      pl.BlockSpec((t, LANES), lambda h, j, b, g: (b, EV_Q_BLK + 2 * h8 + h)),
                      pl.BlockSpec((l, LANES), lambda h, j, b, g: (b, EV_Q_BLK + h8 + h)),
                      pl.BlockSpec((l, LANES), lambda h, j, b, g: (b, EV_Q_BLK + 2 * h8 + h)),
                      pl.BlockSpec((None, None, tq, NA_KROWS * GRID_W), lambda h, j, b, g: (h, g[j], 0, 0))],
            out_specs=pl.BlockSpec((tq, LANES), lambda h, j, b, g: (b * nj + j, h))),
        compiler_params=_cparams(3),
        name="na_attention",
    )(ids, z, z, z, z_c, z_c, bias)


def _dense_attn_kernel(q_ref, k_ref, v_ref, o_ref, *, scale):
    o_ref[...] = _softmax_pv([(_dot_nt(q_ref[...], k_ref[...]), None, v_ref[...])], scale)


def ctx_attention(z_c, batch, l):
    h8 = NA_HEADS
    return pl.pallas_call(
        functools.partial(_dense_attn_kernel, scale=NA_HEAD_DIM ** -0.5),
        out_shape=jax.ShapeDtypeStruct((batch * l, NA_WIDTH), F32),
        grid=(batch, h8),
        in_specs=[pl.BlockSpec((l, LANES), lambda b, h: (b, EV_Q_BLK + h)),
                  pl.BlockSpec((l, LANES), lambda b, h: (b, EV_Q_BLK + h8 + h)),
                  pl.BlockSpec((l, LANES), lambda b, h: (b, EV_Q_BLK + 2 * h8 + h))],
        out_specs=pl.BlockSpec((l, LANES), lambda b, h: (b, h)),
        compiler_params=_cparams(2),
        name="ctx_attention",
    )(z_c, z_c, z_c)


def _head_sum(x):
    first = lax.broadcasted_iota(jnp.int32, x.shape, 1) < RW_HEAD_DIM
    s0 = jnp.sum(jnp.where(first, x, 0.0), axis=-1, keepdims=True)
    s1 = jnp.sum(jnp.where(first, 0.0, x), axis=-1, keepdims=True)
    return jnp.where(first, s0, s1)


def _softplus(x):
    return jnp.maximum(x, 0.0) + jnp.log1p(jnp.exp(-jnp.abs(x)))


def _rw_terms_kernel(ur_ref, uk_ref, uv_ref, uwa_ref, pr_ref, pk_ref, pv_ref, pwa_ref, nr_ref, nk_ref, nv_ref,
                     nwa_ref, mu_ref, w0_ref, w2_ref, a0_ref, a2_ref, kk_ref, ka_ref, rk_ref,
                     r_o, v_o, kkn_o, dec0_o, dec1_o, kd0_o, kd1_o, b0_o, b1_o, bonus_o, *, nt):
    i = pl.program_id(1)
    tm = ur_ref.shape[0]

    def shifted(u_ref, p_ref, n_ref, col0):
        u = u_ref[...]
        mu = mu_ref[:, col0:col0 + u.shape[1]]
        row = lax.broadcasted_iota(jnp.int32, u.shape, 0)
        before = jnp.where(i == 0, 0.0, p_ref[7:8, :])
        after = jnp.where(i == nt - 1, 0.0, n_ref[0:1, :])
        prev = jnp.where(row == 0, before, pltpu.roll(u, 1, axis=0))
        nxt = jnp.where(row == tm - 1, after, pltpu.roll(u, tm - 1, axis=0))
        return u + mu[0:1] * (prev - u) + mu[1:2] * (nxt - u)

    r_all = shifted(ur_ref, pr_ref, nr_ref, 0)
    k_all = shifted(uk_ref, pk_ref, nk_ref, RW_WIDTH)
    v_all = shifted(uv_ref, pv_ref, nv_ref, 2 * RW_WIDTH)
    wa = shifted(uwa_ref, pwa_ref, nwa_ref, 3 * RW_WIDTH)
    wd = jnp.tanh(wa[:, :LANES]).astype(BF16)
    ad = wa[:, LANES:].astype(BF16)
    for p in range(RW_WIDTH // LANES):
        sl = slice(p * LANES, (p + 1) * LANES)
        r, k, v = r_all[:, sl], k_all[:, sl], v_all[:, sl]
        kk = k * kk_ref[:, sl]
        kk = kk / jnp.maximum(jnp.sqrt(_head_sum(kk * kk)), 1e-12)
        r_o[:, p, :] = r
        v_o[:, p, :] = v
        kkn_o[:, p, :] = kk
        kd_sum = jnp.zeros_like(k)
        for d, (dec_o, kd_o, b_o) in enumerate(((dec0_o, kd0_o, b0_o), (dec1_o, kd1_o, b1_o))):
            w = w0_ref[d:d + 1, sl] + jnp.dot(wd, w2_ref[d, :, sl].astype(BF16), preferred_element_type=F32)
            dec_o[:, p, :] = jnp.exp(-jnp.exp(-_softplus(-w) - 0.5))
            a = _sigmoid(a0_ref[d:d + 1, sl] + jnp.dot(ad, a2_ref[d, :, sl].astype(BF16), preferred_element_type=F32))
            kd = k * (1.0 + (a - 1.0) * ka_ref[:, sl])
            kd_o[:, p, :] = kd
            b_o[:, p, :] = kk * a
            kd_sum = kd_sum + kd
        bonus_o[:, p, :] = _head_sum(r * kd_sum * rk_ref[:, sl]) * v


def rwkv_terms(z, n_seq, t, mu, w0, w2p, a0, a2p, k_k, k_a, r_k, tm=256):
    tm = min(tm, t)
    nt = t // tm
    npair = RW_WIDTH // LANES
    wide0 = EV_RW_BLK * LANES // RW_WIDTH
    wa_blk = (EV_RW_BLK + 3 * npair) // 2
    n8 = tm // 8

    def cur(width, cb):
        return pl.BlockSpec((tm, width), lambda b, i: (b * nt + i, cb))

    def before(width, cb):
        return pl.BlockSpec((8, width), lambda b, i: (jnp.maximum((b * nt + i) * n8 - 1, 0), cb))

    def after(width, cb):
        return pl.BlockSpec((8, width), lambda b, i: (jnp.minimum((b * nt + i + 1) * n8, n_seq * nt * n8 - 1), cb))

    cols = [(RW_WIDTH, wide0), (RW_WIDTH, wide0 + 1), (RW_WIDTH, wide0 + 2), (2 * LANES, wa_blk)]
    whole = lambda a: pl.BlockSpec(a.shape, lambda b, i: (0,) * a.ndim)
    params = (mu, w0, w2p, a0, a2p, k_k, k_a, r_k)
    out = pl.BlockSpec((tm, npair, LANES), lambda b, i: (i, b, 0))
    return pl.pallas_call(
        functools.partial(_rw_terms_kernel, nt=nt),
        out_shape=[jax.ShapeDtypeStruct((t, n_seq * npair, LANES), F32)] * 10,
        grid=(n_seq, nt),
        in_specs=([cur(*c) for c in cols] + [before(*c) for c in cols] + [after(*c) for c in cols]
                  + [whole(a) for a in params]),
        out_specs=[out] * 10,
        compiler_params=_cparams(2),
        name="rwkv_terms",
    )(*([z] * 12), *params)


def _to_lanes(n):
    q = jnp.concatenate([n, pltpu.roll(n, RW_HEAD_DIM, axis=1)], axis=0)
    return q.T[:RW_HEAD_DIM]


def _rw_scan_kernel(rf, rb, vf, vb, af, ab, w0, w1, k0, k1, b0, b1, s0_ref, yf_ref, yb_ref, sfin_ref, s_ref, z_ref,
                    *, chunk):
    n = RW_HEAD_DIM

    @pl.when(pl.program_id(0) == 0)
    def _():
        s_ref[...] = s0_ref[...]

    def prep(t, carry):
        tb = chunk - 1 - t
        for slot, (f_ref, b_ref) in enumerate(((rf, rb), (w0, w1), (k0, k1), (vf, vb), (af, ab), (b0, b1))):
            z_ref[slot, t] = _to_lanes(jnp.concatenate([f_ref[t], b_ref[tb]], axis=0))
        return carry

    lax.fori_loop(0, chunk, prep, 0)

    def step(t, carry):
        def sa_body(kg, sa):
            for ki in range(8):
                k = kg * 8 + ki
                sa = sa + s_ref[k] * (-z_ref[4, t, pl.ds(k, 1), :])
            return sa

        sa = lax.fori_loop(0, n // 8, sa_body, jnp.zeros((n, LANES), F32))
        vt = z_ref[3, t]

        def upd_body(kg, y):
            for ki in range(8):
                k = kg * 8 + ki
                row = pl.ds(k, 1)
                sk = s_ref[k] * z_ref[1, t, row, :] + sa * z_ref[5, t, row, :] + vt * z_ref[2, t, row, :]
                s_ref[k] = sk
                y = y + sk * z_ref[0, t, row, :]
            return y

        y = lax.fori_loop(0, n // 8, upd_body, jnp.zeros((n, LANES), F32))
        rows = _to_lanes(y)
        yf_ref[t] = rows[:n // 2]
        yb_ref[chunk - 1 - t] = rows[n // 2:]
        return carry

    lax.fori_loop(0, chunk, step, 0)
    sfin_ref[...] = s_ref[...]


def rwkv_scan(r, v, kkn, dec0, dec1, kd0, kd1, b0, b1, s0):
    t, nq, _ = r.shape
    n = RW_HEAD_DIM
    chunk = RW_SCAN_CHUNK
    nc = t // chunk
    fwd = pl.BlockSpec((chunk, nq, LANES), lambda i: (i, 0, 0))
    bwd = pl.BlockSpec((chunk, nq, LANES), lambda i: (nc - 1 - i, 0, 0))
    state = pl.BlockSpec((n, n, LANES), lambda i: (0, 0, 0))
    slab = jax.ShapeDtypeStruct((t, nq, LANES), F32)
    return pl.pallas_call(
        functools.partial(_rw_scan_kernel, chunk=chunk),
        out_shape=[slab, slab, jax.ShapeDtypeStruct((n, n, LANES), F32)],
        grid=(nc,),
        in_specs=[fwd, bwd, fwd, bwd, fwd, bwd, fwd, bwd, fwd, bwd, fwd, bwd, state],
        out_specs=[fwd, bwd, state],
        scratch_shapes=[pltpu.VMEM((n, n, LANES), F32), pltpu.VMEM((6, chunk, n, LANES), F32)],
        compiler_params=_cparams(1),
        name="rwkv_scan",
    )(r, r, v, v, kkn, kkn, dec0, dec1, kd0, kd1, b0, b1, s0)


def _rw_readout_kernel(yf_ref, yb_ref, bonus_ref, lw_ref, lb_ref, o_ref):
    npair = RW_WIDTH // LANES
    for q in range(yf_ref.shape[1]):
        b, p = divmod(q, npair)
        sl = slice(p * LANES, (p + 1) * LANES)
        y = yf_ref[:, q, :] + yb_ref[:, q, :]
        mu = _head_sum(y) * (1.0 / RW_HEAD_DIM)
        d = y - mu
        var = _head_sum(d * d) * (1.0 / RW_HEAD_DIM)
        o_ref[b, :, sl] = d * lax.rsqrt(var + RW_GN_EPS) * lw_ref[:, sl] + lb_ref[:, sl] + bonus_ref[:, q, :]


def rwkv_readout(y_f, y_b, bonus, ln_w, ln_b, tm=256):
    t, nq, _ = y_f.shape
    tm = min(tm, t)
    batch = nq // (RW_WIDTH // LANES)
    slab = pl.BlockSpec((tm, nq, LANES), lambda i: (i, 0, 0))
    vec = pl.BlockSpec((1, RW_WIDTH), lambda i: (0, 0))
    return pl.pallas_call(
        _rw_readout_kernel,
        out_shape=jax.ShapeDtypeStruct((batch, t, RW_WIDTH), F32),
        grid=(t // tm,),
        in_specs=[slab, slab, slab, vec, vec],
        out_specs=pl.BlockSpec((batch, tm, RW_WIDTH), lambda i: (0, i, 0)),
        compiler_params=_cparams(1),
        name="rwkv_readout",
    )(y_f, y_b, bonus, ln_w.reshape(1, -1), ln_b.reshape(1, -1))


def _rms(x, w):
    return x * lax.rsqrt(jnp.mean(x * x, axis=-1, keepdims=True) + NORM_EPS) * w


def _rope(x, cos, sin):
    odd = (lax.broadcasted_iota(jnp.int32, x.shape, 1) & 1) == 1
    swapped = jnp.where(odd, pltpu.roll(x, 1, axis=1), pltpu.roll(x, LANES - 1, axis=1))
    return x * cos + swapped * sin


def _mla_q_kernel(c_ref, nw_ref, w_ref, cos_ref, sin_ref, q_ref, h_ref):
    @pl.when(pl.program_id(1) == 0)
    def _():
        h_ref[...] = _rms(c_ref[...], nw_ref[...]).astype(BF16)

    acc = jnp.dot(h_ref[...], w_ref[...], preferred_element_type=F32)
    q_ref[:, :MLA_NOPE] = acc[:, :MLA_NOPE].astype(BF16)
    q_ref[:, MLA_NOPE:] = _rope(acc[:, MLA_NOPE:], cos_ref[...], sin_ref[...]).astype(BF16)


def mla_queries(z, q_norm, w_uq_pad, cos, sin, t, tm=512):
    m = z.shape[0]
    tm = min(tm, t)
    nt = t // tm
    return pl.pallas_call(
        _mla_q_kernel,
        out_shape=jax.ShapeDtypeStruct((m, MLA_HEADS * MLA_QK_PAD), BF16),
        grid=(m // tm, MLA_HEADS),
        in_specs=[pl.BlockSpec((tm, MLA_Q_RANK), lambda i, h: (i, GATE_COLS // MLA_Q_RANK)),
                  pl.BlockSpec((1, MLA_Q_RANK), lambda i, h: (0, 0)),
                  pl.BlockSpec((MLA_Q_RANK, MLA_QK_PAD), lambda i, h: (0, h)),
                  pl.BlockSpec((tm, LANES), lambda i, h: (i % nt, 0)),
                  pl.BlockSpec((tm, LANES), lambda i, h: (i % nt, 0))],
        out_specs=pl.BlockSpec((tm, MLA_QK_PAD), lambda i, h: (i, h)),
        scratch_shapes=[pltpu.VMEM((tm, MLA_Q_RANK), BF16)],
        compiler_params=_cparams(2),
        name="mla_queries",
    )(z, q_norm.reshape(1, -1), w_uq_pad, cos, sin)


def _mla_kv_kernel(c_ref, pe_ref, nw_ref, w_ref, cos_ref, sin_ref, k_ref, v_ref, h_ref):
    @pl.when(pl.program_id(1) == 0)
    def _():
        h_ref[...] = _rms(c_ref[...], nw_ref[...]).astype(BF16)

    acc = jnp.dot(h_ref[...], w_ref[...], preferred_element_type=F32)
    k_ref[:, :MLA_NOPE] = acc[:, :MLA_NOPE].astype(BF16)
    k_ref[:, MLA_NOPE:] = _rope(pe_ref[...], cos_ref[...], sin_ref[...]).astype(BF16)
    v_ref[...] = acc[:, MLA_NOPE:].astype(BF16)


def mla_keys_values(z, kv_norm, w_ukv, cos, sin, t, tm=512):
    m = z.shape[0]
    tm = min(tm, t)
    nt = t // tm
    return pl.pallas_call(
        _mla_kv_kernel,
        out_shape=[jax.ShapeDtypeStruct((m, MLA_HEADS * MLA_QK_PAD), BF16),
                   jax.ShapeDtypeStruct((m, MLA_WIDTH), BF16)],
        grid=(m // tm, MLA_HEADS),
        in_specs=[pl.BlockSpec((tm, MLA_KV_RANK), lambda i, h: (i, (GATE_COLS + MLA_Q_RANK) // MLA_KV_RANK)),
                  pl.BlockSpec((tm, LANES), lambda i, h: (i, OD_A_BLK + (MLA_Q_RANK + MLA_KV_RANK) // LANES)),
                  pl.BlockSpec((1, MLA_KV_RANK), lambda i, h: (0, 0)),
                  pl.BlockSpec((MLA_KV_RANK, MLA_NOPE + MLA_V), lambda i, h: (0, h)),
                  pl.BlockSpec((tm, LANES), lambda i, h: (i % nt, 0)),
                  pl.BlockSpec((tm, LANES), lambda i, h: (i % nt, 0))],
        out_specs=[pl.BlockSpec((tm, MLA_QK_PAD), lambda i, h: (i, h)),
                   pl.BlockSpec((tm, MLA_V), lambda i, h: (i, h))],
        scratch_shapes=[pltpu.VMEM((tm, MLA_KV_RANK), BF16)],
        compiler_params=_cparams(2),
        name="mla_keys_values",
    )(z, z, kv_norm.reshape(1, -1), w_ukv, cos, sin)


def _mla_attn_kernel(q_ref, k_ref, v_ref, kc_ref, vc_ref, o_ref, *, scale):
    q = q_ref[...]
    o_ref[...] = _softmax_pv([(_dot_nt(q, k_ref[...]), None, v_ref[...]),
                              (_dot_nt(q, kc_ref[...]), None, vc_ref[...])], scale)


def mla_attention(q, k, v, kc, vc, batch, t, l, tq=512):
    tq = min(tq, t)
    nq = t // tq
    return pl.pallas_call(
        functools.partial(_mla_attn_kernel, scale=(MLA_NOPE + MLA_ROPE) ** -0.5),
        out_shape=jax.ShapeDtypeStruct((batch * t, MLA_WIDTH), F32),
        grid=(batch, MLA_HEADS, nq),
        in_specs=[pl.BlockSpec((tq, MLA_QK_PAD), lambda b, h, i: (b * nq + i, h)),
                  pl.BlockSpec((t, MLA_QK_PAD), lambda b, h, i: (b, h)),
                  pl.BlockSpec((t, MLA_V), lambda b, h, i: (b, h)),
                  pl.BlockSpec((l, MLA_QK_PAD), lambda b, h, i: (b, h)),
                  pl.BlockSpec((l, MLA_V), lambda b, h, i: (b, h))],
        out_specs=pl.BlockSpec((tq, MLA_V), lambda b, h, i: (b * nq + i, h)),
        compiler_params=_cparams(3),
        name="mla_attention",
    )(q, k, v, kc, vc)


def _rope_tables(t):
    tok = np.arange(t)
    pos = np.stack([tok // GRID_W, tok % GRID_W], axis=-1).astype(np.float32)
    n_freq = MLA_ROPE // 4
    inv = (ROPE_BASE ** (-jnp.arange(n_freq, dtype=F32) / n_freq))
    ang = (jnp.asarray(pos)[:, :, None] * inv).reshape(t, MLA_ROPE // 2)
    cos = jnp.repeat(jnp.cos(ang), 2, axis=-1)
    sin = jnp.repeat(jnp.sin(ang), 2, axis=-1) * jnp.tile(jnp.array([-1.0, 1.0], F32), MLA_ROPE // 2)
    pad = ((0, 0), (0, LANES - MLA_ROPE))
    return jnp.pad(cos, pad), jnp.pad(sin, pad)


def _split3(x):
    hi = x.astype(BF16)
    r1 = x - hi.astype(F32)
    mid = r1.astype(BF16)
    lo = (r1 - mid.astype(F32)).astype(BF16)
    return hi, mid, lo


def _hg_kernel(q_ref, f_ref, v_ref, lb_ref, s0_ref, *rest, reverse, finalize, emit_state):
    st_ref = rest[-1]
    rest = rest[:-1]
    if finalize:
        prev_ref, nw_ref = rest[0], rest[1]
        rest = rest[2:]
    o_ref = rest[0]

    @pl.when(pl.program_id(2) == 0)
    def _():
        st_ref[...] = s0_ref[...]

    bt = q_ref.shape[0]
    nchunk = bt // HG_CHUNK
    order = range(nchunk - 1, -1, -1) if reverse else range(nchunk)
    ri = lax.broadcasted_iota(jnp.int32, (bt, bt), 0)
    ci = lax.broadcasted_iota(jnp.int32, (bt, bt), 1)
    same = (ri // HG_CHUNK) == (ci // HG_CHUNK)
    tri = same & ((ci >= ri) if reverse else (ci <= ri))
    tri_f = tri.astype(F32)
    tri_b = tri.astype(BF16)

    lb = lb_ref[...]
    fz = f_ref[...]
    log_sig = jnp.minimum(fz, 0.0) - jnp.log1p(jnp.exp(-jnp.abs(fz)))
    la = jnp.log(lb)
    lbb = jnp.log1p(-lb) + log_sig
    log_f = jnp.maximum(la, lbb) + jnp.log1p(jnp.exp(-jnp.abs(la - lbb)))
    kf_all = (1.0 - lb) * _sigmoid(-fz)
    pieces = jnp.dot(tri_b, jnp.concatenate(_split3(log_f), axis=1), preferred_element_type=F32)
    width = log_f.shape[1]
    cum_all = pieces[:, :width] + pieces[:, width:2 * width] + pieces[:, 2 * width:]

    for hh in range(HG_STEP_HEADS):
        sl = slice(hh * HG_DK, (hh + 1) * HG_DK)
        cum, kf, v = cum_all[:, sl], kf_all[:, sl], v_ref[:, sl]
        last = [cum[c * HG_CHUNK:c * HG_CHUNK + 1] if reverse else cum[(c + 1) * HG_CHUNK - 1:(c + 1) * HG_CHUNK]
                for c in range(nchunk)]
        tot = jnp.concatenate([jnp.broadcast_to(x, (HG_CHUNK, HG_DK)) for x in last], axis=0)
        q_in = q_ref[:, sl] * jnp.exp(cum)
        k_in = kf * jnp.exp(-cum)
        k_end = kf * jnp.exp(tot - cum)
        o_intra = _dot(_dot_nt(q_in, k_in) * tri_f, v)
        st = st_ref[hh]
        for c in order:
            rows = slice(c * HG_CHUNK, (c + 1) * HG_CHUNK)
            o_c = o_intra[rows] + _dot_nt(q_in[rows], st)
            if finalize:
                o_c = o_c + prev_ref[rows, sl]
                o_c = o_c * lax.rsqrt(jnp.mean(o_c * o_c, axis=-1, keepdims=True) + NORM_EPS) * nw_ref[...]
            o_ref[rows, sl] = o_c
            st = st * jnp.exp(last[c]) + _dot_tn(v[rows], k_end[rows])
        st_ref[hh] = st
        if emit_state:
            rest[1][hh] = st


def hgrn2_pass(z, lb, s0, batch, t, f_off, reverse, prev=None, norm_w=None, emit_state=False):
    bt = min(HG_BLOCK, t)
    nblk = t // bt
    nh = HG_STEP_HEADS
    wide = nh * LANES
    finalize = prev is not None
    pos = (lambda i: nblk - 1 - i) if reverse else (lambda i: i)
    blk = lambda off: pl.BlockSpec((bt, wide), lambda b, h, i, off=off: (b * nblk + pos(i), off // nh + h))
    state = pl.BlockSpec((None, nh, HG_DV, HG_DK), lambda b, h, i: (b, h, 0, 0))
    ins = [z, z, z, lb.reshape(1, -1), s0]
    specs = [blk(OD_HG_BLK), blk(OD_HG_BLK + f_off), blk(OD_HG_BLK + 3 * HG_HEADS),
             pl.BlockSpec((1, wide), lambda b, h, i: (0, h)), state]
    if finalize:
        ins += [prev, norm_w.reshape(1, -1)]
        specs += [blk(0), pl.BlockSpec((1, HG_DV), lambda b, h, i: (0, 0))]
    out_shape = [jax.ShapeDtypeStruct((batch * t, HG_WIDTH), F32)]
    out_specs = [blk(0)]
    if emit_state:
        out_shape.append(jax.ShapeDtypeStruct((batch, HG_HEADS, HG_DV, HG_DK), F32))
        out_specs.append(state)
    return pl.pallas_call(
        functools.partial(_hg_kernel, reverse=reverse, finalize=finalize, emit_state=emit_state),
        out_shape=out_shape,
        grid=(batch, HG_HEADS // nh, nblk),
        in_specs=specs,
        out_specs=out_specs,
        scratch_shapes=[pltpu.VMEM((nh, HG_DV, HG_DK), F32)],
        compiler_params=_cparams(3),
        name="hgrn2_" + ("bwd" if reverse else "fwd"),
    )(*ins)


def _even_layer(x, xc, mod, norm_w, w_in, w_out, rpb, rw_params, batch, t, l):
    d = x.shape[1]
    s2 = 3 * NA_WIDTH + RW_SHIFT_COLS
    zero = jnp.zeros((d, RW_PAD_COLS - RW_SHIFT_COLS), F32)
    w_ext = jnp.concatenate([w_in[:, s2:], w_in[:, :s2], zero], axis=1).astype(BF16)
    tiles_per_seq = max(t // 1024, 1)
    z = in_projection(x, norm_w, mod, lambda i: i // tiles_per_seq, w_ext, "even_in_proj")
    z_c = in_projection(xc, norm_w, mod, lambda i: batch, w_ext, "even_in_proj_ctx")

    y_na = na_attention(z, z_c, rpb, batch, t, l)
    yc_na = ctx_attention(z_c, batch, l)

    mu, w0, w2, a0, a2, k_k, k_a, r_k, ln_w, ln_b = rw_params
    mu = jnp.pad(mu, ((0, 0), (0, RW_PAD_COLS - RW_SHIFT_COLS)))
    zpad = jnp.zeros((RW_RANK, RW_WIDTH), F32)
    w2p = jnp.stack([jnp.concatenate([w2[0], zpad]), jnp.concatenate([zpad, w2[1]])])
    a2p = jnp.stack([jnp.concatenate([a2[0], zpad]), jnp.concatenate([zpad, a2[1]])])
    vecs = (mu, w0, w2p, a0, a2p, k_k.reshape(1, -1), k_a.reshape(1, -1), r_k.reshape(1, -1))
    terms = rwkv_terms(z, batch, t, *vecs)
    terms_c = rwkv_terms(z_c, batch, l, *vecs)
    s_zero = jnp.zeros((RW_HEAD_DIM, RW_HEAD_DIM, LANES), F32)
    yc_f, yc_b, s_ctx = rwkv_scan(*terms_c[:9], s_zero)
    y_f, y_b, _ = rwkv_scan(*terms[:9], s_ctx)
    y_rw = rwkv_readout(y_f, y_b, terms[9], ln_w, ln_b).reshape(batch * t, RW_WIDTH)
    yc_rw = rwkv_readout(yc_f, yc_b, terms_c[9], ln_w, ln_b).reshape(batch * l, RW_WIDTH)

    w_out = w_out.astype(BF16)
    rows_per_seq = max(t // 256, 1)
    x = out_projection(y_na, y_rw, z, w_out, x, mod, lambda i: i // rows_per_seq, "even_out_proj")
    xc = out_projection(yc_na, yc_rw, z_c, w_out, xc, mod, lambda i: batch, "even_out_proj_ctx")
    return x, xc


def _odd_layer(x, xc, mod, norm_w, w_in, w_out, q_norm, w_uq, kv_norm, w_ukv, lb, hg_norm_w, final_w,
               batch, t, l):
    d = x.shape[1]
    o3 = MLA_Q_RANK + MLA_KV_RANK + MLA_ROPE
    o4 = o3 + 4 * HG_FDIM
    zero = jnp.zeros((d, MLA_A_COLS - o3), F32)
    w_ext = jnp.concatenate([w_in[:, o4:], w_in[:, :o3], zero, w_in[:, o3:o4]], axis=1).astype(BF16)
    tiles_per_seq = max(t // 1024, 1)
    z = in_projection(x, norm_w, mod, lambda i: i // tiles_per_seq, w_ext, "odd_in_proj")
    z_c = in_projection(xc, norm_w, mod, lambda i: batch, w_ext, "odd_in_proj_ctx")

    cos, sin = _rope_tables(t)
    qk = MLA_NOPE + MLA_ROPE
    w_uq_pad = jnp.pad(w_uq.reshape(MLA_Q_RANK, MLA_HEADS, qk), ((0, 0), (0, 0), (0, MLA_QK_PAD - qk)))
    w_uq_pad = w_uq_pad.reshape(MLA_Q_RANK, MLA_HEADS * MLA_QK_PAD).astype(BF16)
    w_ukv = w_ukv.astype(BF16)
    q = mla_queries(z, q_norm, w_uq_pad, cos, sin, t)
    k, v = mla_keys_values(z, kv_norm, w_ukv, cos, sin, t)
    ones = jnp.ones((l, LANES), F32)
    kc, vc = mla_keys_values(z_c, kv_norm, w_ukv, ones, jnp.zeros_like(ones), l)
    y_mla = mla_attention(q, k, v, kc, vc, batch, t, l)

    s_zero = jnp.zeros((batch, HG_HEADS, HG_DV, HG_DK), F32)
    _, s_f = hgrn2_pass(z_c, lb, s_zero, batch, l, HG_HEADS, False, emit_state=True)
    _, s_b = hgrn2_pass(z_c, lb, s_zero, batch, l, 2 * HG_HEADS, True, emit_state=True)
    o_f, = hgrn2_pass(z, lb, s_f, batch, t, HG_HEADS, False)
    y_hg, = hgrn2_pass(z, lb, s_b, batch, t, 2 * HG_HEADS, True, prev=o_f, norm_w=hg_norm_w)

    rows_per_seq = max(t // 256, 1)
    return out_projection(y_mla, y_hg, z, w_out.astype(BF16), x, mod, lambda i: i // rows_per_seq,
                          "odd_out_proj", final_w=final_w)


def kernel(x, c, ctx, c_ctx, ada_w, ada_b, norm_w, e_w_in, e_w_out, na_rpb, rw_mu, rw_w0, rw_w2, rw_a0, rw_a2, rw_k_k, rw_k_a, rw_r_k, rw_ln_w, rw_ln_b, o_w_in, o_w_out, mla_q_norm, mla_w_uq, mla_kv_norm, mla_w_ukv, hg_lower_bounds, hg_norm_w, final_norm_w):
    batch, t, d = x.shape
    l = ctx.shape[1]
    assert ada_w.shape[0] == 2, "one even and one odd layer"
    cond = jnp.concatenate([c, c_ctx[None, :], jnp.zeros((8 - batch - 1, d), F32)], axis=0)
    mod = modulation(cond, ada_w, ada_b)
    s = jax.nn.softmax(hg_lower_bounds.astype(F32), axis=0)
    lower = jnp.cumsum(s, axis=0) - s[0]

    xf, xcf = x.reshape(batch * t, d), ctx.reshape(batch * l, d)
    rw_params = (rw_mu[0], rw_w0[0], rw_w2[0], rw_a0[0], rw_a2[0], rw_k_k[0], rw_k_a[0], rw_r_k[0],
                 rw_ln_w[0], rw_ln_b[0])
    xf, xcf = _even_layer(xf, xcf, mod[0].reshape(8, 1, 3 * d), norm_w[0], e_w_in[0], e_w_out[0], na_rpb[0],
                          rw_params, batch, t, l)
    out = _odd_layer(xf, xcf, mod[1].reshape(8, 1, 3 * d), norm_w[1], o_w_in[0], o_w_out[0], mla_q_norm[0],
                     mla_w_uq[0], mla_kv_norm[0], mla_w_ukv[0], lower[1], hg_norm_w[0], final_norm_w,
                     batch, t, l)
    return out.reshape(batch, t, d)
```

```python
import functools
import math

import numpy as np
import jax
import jax.numpy as jnp
from jax import lax
from jax.experimental import pallas as pl
from jax.experimental.pallas import tpu as pltpu

F32 = jnp.float32
BF16 = jnp.bfloat16

GRID_W = 64
NORM_EPS = 1e-6
ROPE_BASE = 10000.0

NA_HEADS = 8
NA_HEAD_DIM = 128
NA_WIDTH = NA_HEADS * NA_HEAD_DIM
NA_WIN_ROWS = 8
NA_WIN_COLS = 16
NA_QROWS = 4
NA_KROWS = 12

RW_HEAD_DIM = 64
RW_HEADS = 16
RW_WIDTH = RW_HEADS * RW_HEAD_DIM
RW_RANK = 64
RW_SHIFT_COLS = 3 * RW_WIDTH + 4 * RW_RANK
RW_PAD_COLS = 3584
RW_GN_EPS = 64e-5
RW_SCAN_CHUNK = 32

MLA_HEADS = 8
MLA_Q_RANK = 512
MLA_KV_RANK = 512
MLA_NOPE = 128
MLA_ROPE = 64
MLA_V = 128
MLA_QK_PAD = 256
MLA_WIDTH = MLA_HEADS * MLA_V
MLA_A_COLS = 1536

HG_HEADS = 8
HG_DK = 128
HG_DV = 128
HG_FDIM = HG_HEADS * HG_DK
HG_WIDTH = HG_HEADS * HG_DV
HG_CHUNK = 32
HG_BLOCK = 256
HG_STEP_HEADS = 2

LANES = 128

GATE_COLS = 2048
EV_Q_BLK = GATE_COLS // LANES
EV_RW_BLK = EV_Q_BLK + 3 * NA_WIDTH // LANES
OD_A_BLK = GATE_COLS // LANES
OD_HG_BLK = OD_A_BLK + MLA_A_COLS // LANES
VMEM_LIMIT = 48 * 1024 * 1024
NEG_INF = -1e30
LOG2E = math.log2(math.e)


def _cparams(n_axes):
    return pltpu.CompilerParams(dimension_semantics=("arbitrary",) * n_axes, vmem_limit_bytes=VMEM_LIMIT)


def _sigmoid(x):
    return 1.0 / (1.0 + jnp.exp(-x))


def _silu(x):
    return x * _sigmoid(x)


def _dot(a, b):
    return jnp.dot(a.astype(BF16), b.astype(BF16), preferred_element_type=F32)


def _dot_nt(a, b):
    return lax.dot_general(a.astype(BF16), b.astype(BF16), (((1,), (1,)), ((), ())), preferred_element_type=F32)


def _dot_tn(a, b):
    return lax.dot_general(a.astype(BF16), b.astype(BF16), (((0,), (0,)), ((), ())), preferred_element_type=F32)


def _mod_kernel(c_ref, w_ref, b_ref, o_ref):
    o_ref[...] = _dot(_silu(c_ref[...]), w_ref[...]) + b_ref[...]


def modulation(cond, ada_w, ada_b, tn=512):
    depth, d, n = ada_w.shape
    return pl.pallas_call(
        _mod_kernel,
        out_shape=jax.ShapeDtypeStruct((depth, 8, n), F32),
        grid=(depth, n // tn),
        in_specs=[pl.BlockSpec((8, d), lambda l, j: (0, 0)),
                  pl.BlockSpec((None, d, tn), lambda l, j: (l, 0, j)),
                  pl.BlockSpec((None, 1, tn), lambda l, j: (l, 0, j))],
        out_specs=pl.BlockSpec((None, 8, tn), lambda l, j: (l, 0, j)),
        compiler_params=_cparams(2),
        name="adaln_modulation",
    )(cond, ada_w, ada_b.reshape(depth, 1, n))


def _inproj_kernel(x_ref, nw_ref, shift_ref, scale_ref, w_ref, o_ref, h_ref):
    @pl.when(pl.program_id(1) == 0)
    def _():
        x = x_ref[...]
        y = x * lax.rsqrt(jnp.mean(x * x, axis=-1, keepdims=True) + NORM_EPS) * nw_ref[...]
        h_ref[...] = (y * (1.0 + scale_ref[...]) + shift_ref[...]).astype(BF16)

    o_ref[...] = jnp.dot(h_ref[...], w_ref[...], preferred_element_type=F32)


def in_projection(x, norm_w, mod, mod_row, w, name, tm=1024, tn=512):
    m, d = x.shape
    n = w.shape[1]
    tm = min(tm, m)
    return pl.pallas_call(
        _inproj_kernel,
        out_shape=jax.ShapeDtypeStruct((m, n), F32),
        grid=(m // tm, n // tn),
        in_specs=[pl.BlockSpec((tm, d), lambda i, j: (i, 0)),
                  pl.BlockSpec((1, d), lambda i, j: (0, 0)),
                  pl.BlockSpec((None, 1, d), lambda i, j: (mod_row(i), 0, 0)),
                  pl.BlockSpec((None, 1, d), lambda i, j: (mod_row(i), 0, 1)),
                  pl.BlockSpec((d, tn), lambda i, j: (0, j))],
        out_specs=pl.BlockSpec((tm, tn), lambda i, j: (i, j)),
        scratch_shapes=[pltpu.VMEM((tm, d), BF16)],
        compiler_params=_cparams(2),
        name=name,
    )(x, norm_w.reshape(1, d), mod, mod, w)


def _outproj_kernel(ya_ref, yb_ref, g_ref, w_ref, x_ref, gm_ref, *rest, ka, final):
    o_ref = rest[-1]
    sg = _silu(g_ref[...])
    acc = _dot(ya_ref[...].astype(F32) * sg[:, :ka], w_ref[:ka, :])
    acc += _dot(yb_ref[...].astype(F32) * sg[:, ka:], w_ref[ka:, :])
    out = x_ref[...] + gm_ref[...] * acc
    if final:
        out = out * lax.rsqrt(jnp.mean(out * out, axis=-1, keepdims=True) + NORM_EPS) * rest[0][...]
    o_ref[...] = out


def out_projection(ya, yb, gate, w, x, mod, mod_row, name, final_w=None, tm=256):
    m, d = x.shape
    ka, kb = ya.shape[1], yb.shape[1]
    tm = min(tm, m)
    ins = [ya, yb, gate, w, x, mod]
    specs = [pl.BlockSpec((tm, ka), lambda i: (i, 0)),
             pl.BlockSpec((tm, kb), lambda i: (i, 0)),
             pl.BlockSpec((tm, ka + kb), lambda i: (i, 0)),
             pl.BlockSpec((ka + kb, d), lambda i: (0, 0)),
             pl.BlockSpec((tm, d), lambda i: (i, 0)),
             pl.BlockSpec((None, 1, d), lambda i: (mod_row(i), 0, 2))]
    if final_w is not None:
        ins.append(final_w.reshape(1, d))
        specs.append(pl.BlockSpec((1, d), lambda i: (0, 0)))
    return pl.pallas_call(
        functools.partial(_outproj_kernel, ka=ka, final=final_w is not None),
        out_shape=jax.ShapeDtypeStruct((m, d), F32),
        grid=(m // tm,),
        in_specs=specs,
        out_specs=pl.BlockSpec((tm, d), lambda i: (i, 0)),
        compiler_params=_cparams(1),
        name=name,
    )(*ins)


def _softmax_pv(parts, scale):
    c = scale * LOG2E
    logits = [s * c if bias is None else s * c + bias * LOG2E for s, bias, _ in parts]
    m = functools.reduce(jnp.maximum, [jnp.max(x, axis=-1, keepdims=True) for x in logits])
    ps = [jnp.exp2(x - m) for x in logits]
    denom = functools.reduce(lambda a, b: a + b, [jnp.sum(p, axis=-1, keepdims=True) for p in ps])
    num = functools.reduce(lambda a, b: a + b, [_dot(p, v) for p, (_, _, v) in zip(ps, parts)])
    return num / denom


def _na_kernel(g_ref, q_ref, k_ref, v_ref, kc_ref, vc_ref, bias_ref, o_ref, *, rows, scale):
    del g_ref
    j = pl.program_id(1)
    start = pl.multiple_of(jnp.clip(NA_QROWS * j - NA_WIN_ROWS // 2, 0, rows - NA_KROWS) * GRID_W, GRID_W)
    nk = NA_KROWS * GRID_W
    q = q_ref[...]
    s_loc = _dot_nt(q, k_ref[pl.ds(start, nk), :])
    s_ctx = _dot_nt(q, kc_ref[...])
    o_ref[...] = _softmax_pv([(s_loc, bias_ref[...], v_ref[pl.ds(start, nk), :]), (s_ctx, None, vc_ref[...])], scale)


def _na_geometry(rows):
    kh = min(NA_WIN_ROWS, rows)
    geos, ids = [], []
    for j in range(rows // NA_QROWS):
        ks = int(np.clip(NA_QROWS * j - NA_WIN_ROWS // 2, 0, rows - NA_KROWS))
        r = NA_QROWS * j + np.arange(NA_QROWS)[:, None]
        kr = ks + np.arange(NA_KROWS)[None, :]
        r0 = np.clip(r - kh // 2, 0, rows - kh)
        valid = (kr >= r0) & (kr < r0 + kh)
        dy = np.where(valid, kr - r + NA_WIN_ROWS - 1, 0)
        key = (dy.tobytes(), valid.tobytes())
        if key not in [g[0] for g in geos]:
            geos.append((key, dy, valid))
        ids.append([g[0] for g in geos].index(key))
    return np.array(ids, np.int32), np.stack([g[1] for g in geos]), np.stack([g[2] for g in geos])


def _na_bias(rpb, rows):
    col = np.arange(GRID_W)
    cs = np.clip(col - NA_WIN_COLS // 2, 0, GRID_W - NA_WIN_COLS)
    kc = col[None, :]
    in_win = (kc >= cs[:, None]) & (kc < cs[:, None] + NA_WIN_COLS)
    dx = np.where(in_win, kc - col[:, None] + NA_WIN_COLS - 1, 0)
    tab = jnp.where(in_win[None, None], rpb[:, :, dx], NEG_INF)
    ids, dy, valid = _na_geometry(rows)
    blocks = jnp.where(valid[None, :, :, :, None, None], tab[:, dy], NEG_INF)
    g = dy.shape[0]
    blocks = blocks.transpose(0, 1, 2, 4, 3, 5).reshape(rpb.shape[0], g, NA_QROWS * GRID_W, NA_KROWS * GRID_W)
    return ids, blocks


def na_attention(z, z_c, rpb, batch, t, l):
    rows = t // GRID_W
    nj = rows // NA_QROWS
    tq = NA_QROWS * GRID_W
    ids, bias = _na_bias(rpb, rows)
    ids = jnp.asarray(ids)
    h8 = NA_HEADS
    return pl.pallas_call(
        functools.partial(_na_kernel, rows=rows, scale=NA_HEAD_DIM ** -0.5),
        out_shape=jax.ShapeDtypeStruct((batch * t, NA_WIDTH), F32),
        grid_spec=pltpu.PrefetchScalarGridSpec(
            num_scalar_prefetch=1,
            grid=(h8, nj, batch),
            in_specs=[pl.BlockSpec((tq, LANES), lambda h, j, b, g: (b * nj + j, EV_Q_BLK + h)),
                      pl.BlockSpec((t, LANES), lambda h, j, b, g: (b, EV_Q_BLK + h8 + h)),
                      pl.BlockSpec((t, LANES), lambda h, j, b, g: (b, EV_Q_BLK + 2 * h8 + h)),
                      pl.BlockSpec((l, LANES), lambda h, j, b, g: (b, EV_Q_BLK + h8 + h)),
                      pl.BlockSpec((l, LANES), lambda h, j, b, g: (b, EV_Q_BLK + 2 * h8 + h)),
                      pl.BlockSpec((None, None, tq, NA_KROWS * GRID_W), lambda h, j, b, g: (h, g[j], 0, 0))],
            out_specs=pl.BlockSpec((tq, LANES), lambda h, j, b, g: (b * nj + j, h))),
        compiler_params=_cparams(3),
        name="na_attention",
    )(ids, z, z, z, z_c, z_c, bias)


def _dense_attn_kernel(q_ref, k_ref, v_ref, o_ref, *, scale):
    o_ref[...] = _softmax_pv([(_dot_nt(q_ref[...], k_ref[...]), None, v_ref[...])], scale)


def ctx_attention(z_c, batch, l):
    h8 = NA_HEADS
    return pl.pallas_call(
        functools.partial(_dense_attn_kernel, scale=NA_HEAD_DIM ** -0.5),
        out_shape=jax.ShapeDtypeStruct((batch * l, NA_WIDTH), F32),
        grid=(batch, h8),
        in_specs=[pl.BlockSpec((l, LANES), lambda b, h: (b, EV_Q_BLK + h)),
                  pl.BlockSpec((l, LANES), lambda b, h: (b, EV_Q_BLK + h8 + h)),
                  pl.BlockSpec((l, LANES), lambda b, h: (b, EV_Q_BLK + 2 * h8 + h))],
        out_specs=pl.BlockSpec((l, LANES), lambda b, h: (b, h)),
        compiler_params=_cparams(2),
        name="ctx_attention",
    )(z_c, z_c, z_c)


def _head_sum(x):
    first = lax.broadcasted_iota(jnp.int32, x.shape, 1) < RW_HEAD_DIM
    s0 = jnp.sum(jnp.where(first, x, 0.0), axis=-1, keepdims=True)
    s1 = jnp.sum(jnp.where(first, 0.0, x), axis=-1, keepdims=True)
    return jnp.where(first, s0, s1)


def _softplus(x):
    return jnp.maximum(x, 0.0) + jnp.log1p(jnp.exp(-jnp.abs(x)))


def _rw_terms_kernel(ur_ref, uk_ref, uv_ref, uwa_ref, pr_ref, pk_ref, pv_ref, pwa_ref, nr_ref, nk_ref, nv_ref,
                     nwa_ref, mu_ref, w0_ref, w2_ref, a0_ref, a2_ref, kk_ref, ka_ref, rk_ref,
                     r_o, v_o, kkn_o, dec0_o, dec1_o, kd0_o, kd1_o, b0_o, b1_o, bonus_o, *, nt):
    i = pl.program_id(1)
    tm = ur_ref.shape[0]

    def shifted(u_ref, p_ref, n_ref, col0):
        u = u_ref[...]
        mu = mu_ref[:, col0:col0 + u.shape[1]]
        row = lax.broadcasted_iota(jnp.int32, u.shape, 0)
        before = jnp.where(i == 0, 0.0, p_ref[7:8, :])
        after = jnp.where(i == nt - 1, 0.0, n_ref[0:1, :])
        prev = jnp.where(row == 0, before, pltpu.roll(u, 1, axis=0))
        nxt = jnp.where(row == tm - 1, after, pltpu.roll(u, tm - 1, axis=0))
        return u + mu[0:1] * (prev - u) + mu[1:2] * (nxt - u)

    r_all = shifted(ur_ref, pr_ref, nr_ref, 0)
    k_all = shifted(uk_ref, pk_ref, nk_ref, RW_WIDTH)
    v_all = shifted(uv_ref, pv_ref, nv_ref, 2 * RW_WIDTH)
    wa = shifted(uwa_ref, pwa_ref, nwa_ref, 3 * RW_WIDTH)
    wd = jnp.tanh(wa[:, :LANES]).astype(BF16)
    ad = wa[:, LANES:].astype(BF16)
    for p in range(RW_WIDTH // LANES):
        sl = slice(p * LANES, (p + 1) * LANES)
        r, k, v = r_all[:, sl], k_all[:, sl], v_all[:, sl]
        kk = k * kk_ref[:, sl]
        kk = kk / jnp.maximum(jnp.sqrt(_head_sum(kk * kk)), 1e-12)
        r_o[:, p, :] = r
        v_o[:, p, :] = v
        kkn_o[:, p, :] = kk
        kd_sum = jnp.zeros_like(k)
        for d, (dec_o, kd_o, b_o) in enumerate(((dec0_o, kd0_o, b0_o), (dec1_o, kd1_o, b1_o))):
            w = w0_ref[d:d + 1, sl] + jnp.dot(wd, w2_ref[d, :, sl].astype(BF16), preferred_element_type=F32)
            dec_o[:, p, :] = jnp.exp(-jnp.exp(-_softplus(-w) - 0.5))
            a = _sigmoid(a0_ref[d:d + 1, sl] + jnp.dot(ad, a2_ref[d, :, sl].astype(BF16), preferred_element_type=F32))
            kd = k * (1.0 + (a - 1.0) * ka_ref[:, sl])
            kd_o[:, p, :] = kd
            b_o[:, p, :] = kk * a
            kd_sum = kd_sum + kd
        bonus_o[:, p, :] = _head_sum(r * kd_sum * rk_ref[:, sl]) * v


def rwkv_terms(z, n_seq, t, mu, w0, w2p, a0, a2p, k_k, k_a, r_k, tm=256):
    tm = min(tm, t)
    nt = t // tm
    npair = RW_WIDTH // LANES
    wide0 = EV_RW_BLK * LANES // RW_WIDTH
    wa_blk = (EV_RW_BLK + 3 * npair) // 2
    n8 = tm // 8

    def cur(width, cb):
        return pl.BlockSpec((tm, width), lambda b, i: (b * nt + i, cb))

    def before(width, cb):
        return pl.BlockSpec((8, width), lambda b, i: (jnp.maximum((b * nt + i) * n8 - 1, 0), cb))

    def after(width, cb):
        return pl.BlockSpec((8, width), lambda b, i: (jnp.minimum((b * nt + i + 1) * n8, n_seq * nt * n8 - 1), cb))

    cols = [(RW_WIDTH, wide0), (RW_WIDTH, wide0 + 1), (RW_WIDTH, wide0 + 2), (2 * LANES, wa_blk)]
    whole = lambda a: pl.BlockSpec(a.shape, lambda b, i: (0,) * a.ndim)
    params = (mu, w0, w2p, a0, a2p, k_k, k_a, r_k)
    out = pl.BlockSpec((tm, npair, LANES), lambda b, i: (i, b, 0))
    return pl.pallas_call(
        functools.partial(_rw_terms_kernel, nt=nt),
        out_shape=[jax.ShapeDtypeStruct((t, n_seq * npair, LANES), F32)] * 10,
        grid=(n_seq, nt),
        in_specs=([cur(*c) for c in cols] + [before(*c) for c in cols] + [after(*c) for c in cols]
                  + [whole(a) for a in params]),
        out_specs=[out] * 10,
        compiler_params=_cparams(2),
        name="rwkv_terms",
    )(*([z] * 12), *params)


def _to_lanes(n):
    q = jnp.concatenate([n, pltpu.roll(n, RW_HEAD_DIM, axis=1)], axis=0)
    return q.T[:RW_HEAD_DIM]


def _rw_scan_kernel(rf, rb, vf, vb, af, ab, w0, w1, k0, k1, b0, b1, s0_ref, yf_ref, yb_ref, sfin_ref,
                    s_ref, z_ref, znext_ref, g_ref, ybuf_ref, *, chunk):
    n = RW_HEAD_DIM
    srcs = ((rf, rb), (w0, w1), (k0, k1), (vf, vb), (af, ab), (b0, b1))
    R, K, V, A, B = range(5)

    @pl.when(pl.program_id(0) == 0)
    def _():
        s_ref[...] = s0_ref[...]

    def convert(t_src):
        r, w, k, v, kk, b = (_to_lanes(jnp.concatenate([f[t_src], bw[chunk - 1 - t_src]], axis=0))
                             for f, bw in srcs)
        g_prev = g_ref[0]
        g = g_prev * w
        g_ref[0] = g
        inv = 1.0 / g
        znext_ref[R] = r * g
        znext_ref[K] = k * inv
        znext_ref[V] = v
        znext_ref[A] = -(kk * g_prev)
        znext_ref[B] = b * inv

    def emit_y(t):
        rows = _to_lanes(ybuf_ref[...])
        yf_ref[t] = rows[:n // 2]
        yb_ref[chunk - 1 - t] = rows[n // 2:]

    g_ref[0] = jnp.ones((n, LANES), F32)
    ybuf_ref[...] = jnp.zeros((n, LANES), F32)
    convert(0)

    def step(t, carry):
        emit_y(jnp.maximum(t - 1, 0))
        z_ref[...] = znext_ref[...]
        g_ref[1] = g_ref[0]
        convert(jnp.minimum(t + 1, chunk - 1))
        sa = [jnp.zeros((n, LANES), F32), jnp.zeros((n, LANES), F32)]
        for k in range(n):
            sa[k % 2] = sa[k % 2] + s_ref[k] * z_ref[A, k:k + 1, :]
        sa = sa[0] + sa[1]
        vt = z_ref[V]
        y = [jnp.zeros((n, LANES), F32), jnp.zeros((n, LANES), F32)]
        for k in range(n):
            sk = s_ref[k] + sa * z_ref[B, k:k + 1, :] + vt * z_ref[K, k:k + 1, :]
            s_ref[k] = sk
            y[k % 2] = y[k % 2] + sk * z_ref[R, k:k + 1, :]
        ybuf_ref[...] = y[0] + y[1]
        return carry

    lax.fori_loop(0, chunk, step, 0)
    emit_y(chunk - 1)
    for k in range(n):
        s_ref[k] = s_ref[k] * g_ref[1, k:k + 1, :]
    sfin_ref[...] = s_ref[...]


def rwkv_scan(r, v, kkn, dec0, dec1, kd0, kd1, b0, b1, s0):
    t, nq, _ = r.shape
    n = RW_HEAD_DIM
    chunk = RW_SCAN_CHUNK
    nc = t // chunk
    fwd = pl.BlockSpec((chunk, nq, LANES), lambda i: (i, 0, 0))
    bwd = pl.BlockSpec((chunk, nq, LANES), lambda i: (nc - 1 - i, 0, 0))
    state = pl.BlockSpec((n, n, LANES), lambda i: (0, 0, 0))
    slab = jax.ShapeDtypeStruct((t, nq, LANES), F32)
    return pl.pallas_call(
        functools.partial(_rw_scan_kernel, chunk=chunk),
        out_shape=[slab, slab, jax.ShapeDtypeStruct((n, n, LANES), F32)],
        grid=(nc,),
        in_specs=[fwd, bwd, fwd, bwd, fwd, bwd, fwd, bwd, fwd, bwd, fwd, bwd, state],
        out_specs=[fwd, bwd, state],
        scratch_shapes=[pltpu.VMEM((n, n, LANES), F32), pltpu.VMEM((5, n, LANES), F32),
                        pltpu.VMEM((5, n, LANES), F32), pltpu.VMEM((2, n, LANES), F32),
                        pltpu.VMEM((n, LANES), F32)],
        compiler_params=_cparams(1),
        name="rwkv_scan",
    )(r, r, v, v, kkn, kkn, dec0, dec1, kd0, kd1, b0, b1, s0)


def _rw_readout_kernel(yf_ref, yb_ref, bonus_ref, lw_ref, lb_ref, o_ref):
    npair = RW_WIDTH // LANES
    for q in range(yf_ref.shape[1]):
        b, p = divmod(q, npair)
        sl = slice(p * LANES, (p + 1) * LANES)
        y = yf_ref[:, q, :] + yb_ref[:, q, :]
        mu = _head_sum(y) * (1.0 / RW_HEAD_DIM)
        d = y - mu
        var = _head_sum(d * d) * (1.0 / RW_HEAD_DIM)
        o_ref[b, :, sl] = d * lax.rsqrt(var + RW_GN_EPS) * lw_ref[:, sl] + lb_ref[:, sl] + bonus_ref[:, q, :]


def rwkv_readout(y_f, y_b, bonus, ln_w, ln_b, tm=256):
    t, nq, _ = y_f.shape
    tm = min(tm, t)
    batch = nq // (RW_WIDTH // LANES)
    slab = pl.BlockSpec((tm, nq, LANES), lambda i: (i, 0, 0))
    vec = pl.BlockSpec((1, RW_WIDTH), lambda i: (0, 0))
    return pl.pallas_call(
        _rw_readout_kernel,
        out_shape=jax.ShapeDtypeStruct((batch, t, RW_WIDTH), F32),
        grid=(t // tm,),
        in_specs=[slab, slab, slab, vec, vec],
        out_specs=pl.BlockSpec((batch, tm, RW_WIDTH), lambda i: (0, i, 0)),
        compiler_params=_cparams(1),
        name="rwkv_readout",
    )(y_f, y_b, bonus, ln_w.reshape(1, -1), ln_b.reshape(1, -1))


def _rms(x, w):
    return x * lax.rsqrt(jnp.mean(x * x, axis=-1, keepdims=True) + NORM_EPS) * w


def _rope(x, cos, sin):
    odd = (lax.broadcasted_iota(jnp.int32, x.shape, 1) & 1) == 1
    swapped = jnp.where(odd, pltpu.roll(x, 1, axis=1), pltpu.roll(x, LANES - 1, axis=1))
    return x * cos + swapped * sin


def _mla_q_kernel(c_ref, nw_ref, w_ref, cos_ref, sin_ref, q_ref, h_ref):
    @pl.when(pl.program_id(1) == 0)
    def _():
        h_ref[...] = _rms(c_ref[...], nw_ref[...]).astype(BF16)

    acc = jnp.dot(h_ref[...], w_ref[...], preferred_element_type=F32)
    q_ref[:, :MLA_NOPE] = acc[:, :MLA_NOPE].astype(BF16)
    q_ref[:, MLA_NOPE:] = _rope(acc[:, MLA_NOPE:], cos_ref[...], sin_ref[...]).astype(BF16)


def mla_queries(z, q_norm, w_uq_pad, cos, sin, t, tm=512):
    m = z.shape[0]
    tm = min(tm, t)
    nt = t // tm
    return pl.pallas_call(
        _mla_q_kernel,
        out_shape=jax.ShapeDtypeStruct((m, MLA_HEADS * MLA_QK_PAD), BF16),
        grid=(m // tm, MLA_HEADS),
        in_specs=[pl.BlockSpec((tm, MLA_Q_RANK), lambda i, h: (i, GATE_COLS // MLA_Q_RANK)),
                  pl.BlockSpec((1, MLA_Q_RANK), lambda i, h: (0, 0)),
                  pl.BlockSpec((MLA_Q_RANK, MLA_QK_PAD), lambda i, h: (0, h)),
                  pl.BlockSpec((tm, LANES), lambda i, h: (i % nt, 0)),
                  pl.BlockSpec((tm, LANES), lambda i, h: (i % nt, 0))],
        out_specs=pl.BlockSpec((tm, MLA_QK_PAD), lambda i, h: (i, h)),
        scratch_shapes=[pltpu.VMEM((tm, MLA_Q_RANK), BF16)],
        compiler_params=_cparams(2),
        name="mla_queries",
    )(z, q_norm.reshape(1, -1), w_uq_pad, cos, sin)


def _mla_kv_kernel(c_ref, pe_ref, nw_ref, w_ref, cos_ref, sin_ref, k_ref, v_ref, h_ref):
    @pl.when(pl.program_id(1) == 0)
    def _():
        h_ref[...] = _rms(c_ref[...], nw_ref[...]).astype(BF16)

    acc = jnp.dot(h_ref[...], w_ref[...], preferred_element_type=F32)
    k_ref[:, :MLA_NOPE] = acc[:, :MLA_NOPE].astype(BF16)
    k_ref[:, MLA_NOPE:] = _rope(pe_ref[...], cos_ref[...], sin_ref[...]).astype(BF16)
    v_ref[...] = acc[:, MLA_NOPE:].astype(BF16)


def mla_keys_values(z, kv_norm, w_ukv, cos, sin, t, tm=512):
    m = z.shape[0]
    tm = min(tm, t)
    nt = t // tm
    return pl.pallas_call(
        _mla_kv_kernel,
        out_shape=[jax.ShapeDtypeStruct((m, MLA_HEADS * MLA_QK_PAD), BF16),
                   jax.ShapeDtypeStruct((m, MLA_WIDTH), BF16)],
        grid=(m // tm, MLA_HEADS),
        in_specs=[pl.BlockSpec((tm, MLA_KV_RANK), lambda i, h: (i, (GATE_COLS + MLA_Q_RANK) // MLA_KV_RANK)),
                  pl.BlockSpec((tm, LANES), lambda i, h: (i, OD_A_BLK + (MLA_Q_RANK + MLA_KV_RANK) // LANES)),
                  pl.BlockSpec((1, MLA_KV_RANK), lambda i, h: (0, 0)),
                  pl.BlockSpec((MLA_KV_RANK, MLA_NOPE + MLA_V), lambda i, h: (0, h)),
                  pl.BlockSpec((tm, LANES), lambda i, h: (i % nt, 0)),
                  pl.BlockSpec((tm, LANES), lambda i, h: (i % nt, 0))],
        out_specs=[pl.BlockSpec((tm, MLA_QK_PAD), lambda i, h: (i, h)),
                   pl.BlockSpec((tm, MLA_V), lambda i, h: (i, h))],
        scratch_shapes=[pltpu.VMEM((tm, MLA_KV_RANK), BF16)],
        compiler_params=_cparams(2),
        name="mla_keys_values",
    )(z, z, kv_norm.reshape(1, -1), w_ukv, cos, sin)


def _mla_attn_kernel(q_ref, k_ref, v_ref, kc_ref, vc_ref, o_ref, *, scale):
    q = q_ref[...]
    o_ref[...] = _softmax_pv([(_dot_nt(q, k_ref[...]), None, v_ref[...]),
                              (_dot_nt(q, kc_ref[...]), None, vc_ref[...])], scale)


def mla_attention(q, k, v, kc, vc, batch, t, l, tq=512):
    tq = min(tq, t)
    nq = t // tq
    return pl.pallas_call(
        functools.partial(_mla_attn_kernel, scale=(MLA_NOPE + MLA_ROPE) ** -0.5),
        out_shape=jax.ShapeDtypeStruct((batch * t, MLA_WIDTH), F32),
        grid=(batch, MLA_HEADS, nq),
        in_specs=[pl.BlockSpec((tq, MLA_QK_PAD), lambda b, h, i: (b * nq + i, h)),
                  pl.BlockSpec((t, MLA_QK_PAD), lambda b, h, i: (b, h)),
                  pl.BlockSpec((t, MLA_V), lambda b, h, i: (b, h)),
                  pl.BlockSpec((l, MLA_QK_PAD), lambda b, h, i: (b, h)),
                  pl.BlockSpec((l, MLA_V), lambda b, h, i: (b, h))],
        out_specs=pl.BlockSpec((tq, MLA_V), lambda b, h, i: (b * nq + i, h)),
        compiler_params=_cparams(3),
        name="mla_attention",
    )(q, k, v, kc, vc)


def _rope_tables(t):
    tok = np.arange(t)
    pos = np.stack([tok // GRID_W, tok % GRID_W], axis=-1).astype(np.float32)
    n_freq = MLA_ROPE // 4
    inv = (ROPE_BASE ** (-jnp.arange(n_freq, dtype=F32) / n_freq))
    ang = (jnp.asarray(pos)[:, :, None] * inv).reshape(t, MLA_ROPE // 2)
    cos = jnp.repeat(jnp.cos(ang), 2, axis=-1)
    sin = jnp.repeat(jnp.sin(ang), 2, axis=-1) * jnp.tile(jnp.array([-1.0, 1.0], F32), MLA_ROPE // 2)
    pad = ((0, 0), (0, LANES - MLA_ROPE))
    return jnp.pad(cos, pad), jnp.pad(sin, pad)


def _split3(x):
    hi = x.astype(BF16)
    r1 = x - hi.astype(F32)
    mid = r1.astype(BF16)
    lo = (r1 - mid.astype(F32)).astype(BF16)
    return hi, mid, lo


def _hg_kernel(q_ref, f_ref, v_ref, lb_ref, s0_ref, *rest, reverse, finalize, emit_state):
    st_ref = rest[-1]
    rest = rest[:-1]
    if finalize:
        prev_ref, nw_ref = rest[0], rest[1]
        rest = rest[2:]
    o_ref = rest[0]

    @pl.when(pl.program_id(2) == 0)
    def _():
        st_ref[...] = s0_ref[...]

    bt = q_ref.shape[0]
    nchunk = bt // HG_CHUNK
    order = range(nchunk - 1, -1, -1) if reverse else range(nchunk)
    ri = lax.broadcasted_iota(jnp.int32, (bt, bt), 0)
    ci = lax.broadcasted_iota(jnp.int32, (bt, bt), 1)
    same = (ri // HG_CHUNK) == (ci // HG_CHUNK)
    tri = same & ((ci >= ri) if reverse else (ci <= ri))
    tri_f = tri.astype(F32)
    tri_b = tri.astype(BF16)

    lb = lb_ref[...]
    fz = f_ref[...]
    log_sig = jnp.minimum(fz, 0.0) - jnp.log1p(jnp.exp(-jnp.abs(fz)))
    la = jnp.log(lb)
    lbb = jnp.log1p(-lb) + log_sig
    log_f = jnp.maximum(la, lbb) + jnp.log1p(jnp.exp(-jnp.abs(la - lbb)))
    kf_all = (1.0 - lb) * _sigmoid(-fz)
    pieces = jnp.dot(tri_b, jnp.concatenate(_split3(log_f), axis=1), preferred_element_type=F32)
    width = log_f.shape[1]
    cum_all = pieces[:, :width] + pieces[:, width:2 * width] + pieces[:, 2 * width:]

    for hh in range(HG_STEP_HEADS):
        sl = slice(hh * HG_DK, (hh + 1) * HG_DK)
        cum, kf, v = cum_all[:, sl], kf_all[:, sl], v_ref[:, sl]
        last = [cum[c * HG_CHUNK:c * HG_CHUNK + 1] if reverse else cum[(c + 1) * HG_CHUNK - 1:(c + 1) * HG_CHUNK]
                for c in range(nchunk)]
        tot = jnp.concatenate([jnp.broadcast_to(x, (HG_CHUNK, HG_DK)) for x in last], axis=0)
        q_in = q_ref[:, sl] * jnp.exp(cum)
        k_in = kf * jnp.exp(-cum)
        k_end = kf * jnp.exp(tot - cum)
        o_intra = _dot(_dot_nt(q_in, k_in) * tri_f, v)
        st = st_ref[hh]
        for c in order:
            rows = slice(c * HG_CHUNK, (c + 1) * HG_CHUNK)
            o_c = o_intra[rows] + _dot_nt(q_in[rows], st)
            if finalize:
                o_c = o_c + prev_ref[rows, sl]
                o_c = o_c * lax.rsqrt(jnp.mean(o_c * o_c, axis=-1, keepdims=True) + NORM_EPS) * nw_ref[...]
            o_ref[rows, sl] = o_c
            st = st * jnp.exp(last[c]) + _dot_tn(v[rows], k_end[rows])
        st_ref[hh] = st
        if emit_state:
            rest[1][hh] = st


def hgrn2_pass(z, lb, s0, batch, t, f_off, reverse, prev=None, norm_w=None, emit_state=False):
    bt = min(HG_BLOCK, t)
    nblk = t // bt
    nh = HG_STEP_HEADS
    wide = nh * LANES
    finalize = prev is not None
    pos = (lambda i: nblk - 1 - i) if reverse else (lambda i: i)
    blk = lambda off: pl.BlockSpec((bt, wide), lambda b, h, i, off=off: (b * nblk + pos(i), off // nh + h))
    state = pl.BlockSpec((None, nh, HG_DV, HG_DK), lambda b, h, i: (b, h, 0, 0))
    ins = [z, z, z, lb.reshape(1, -1), s0]
    specs = [blk(OD_HG_BLK), blk(OD_HG_BLK + f_off), blk(OD_HG_BLK + 3 * HG_HEADS),
             pl.BlockSpec((1, wide), lambda b, h, i: (0, h)), state]
    if finalize:
        ins += [prev, norm_w.reshape(1, -1)]
        specs += [blk(0), pl.BlockSpec((1, HG_DV), lambda b, h, i: (0, 0))]
    out_shape = [jax.ShapeDtypeStruct((batch * t, HG_WIDTH), F32)]
    out_specs = [blk(0)]
    if emit_state:
        out_shape.append(jax.ShapeDtypeStruct((batch, HG_HEADS, HG_DV, HG_DK), F32))
        out_specs.append(state)
    return pl.pallas_call(
        functools.partial(_hg_kernel, reverse=reverse, finalize=finalize, emit_state=emit_state),
        out_shape=out_shape,
        grid=(batch, HG_HEADS // nh, nblk),
        in_specs=specs,
        out_specs=out_specs,
        scratch_shapes=[pltpu.VMEM((nh, HG_DV, HG_DK), F32)],
        compiler_params=_cparams(3),
        name="hgrn2_" + ("bwd" if reverse else "fwd"),
    )(*ins)


def _even_layer(x, xc, mod, norm_w, w_in, w_out, rpb, rw_params, batch, t, l):
    d = x.shape[1]
    s2 = 3 * NA_WIDTH + RW_SHIFT_COLS
    zero = jnp.zeros((d, RW_PAD_COLS - RW_SHIFT_COLS), F32)
    w_ext = jnp.concatenate([w_in[:, s2:], w_in[:, :s2], zero], axis=1).astype(BF16)
    tiles_per_seq = max(t // 1024, 1)
    z = in_projection(x, norm_w, mod, lambda i: i // tiles_per_seq, w_ext, "even_in_proj")
    z_c = in_projection(xc, norm_w, mod, lambda i: batch, w_ext, "even_in_proj_ctx")

    y_na = na_attention(z, z_c, rpb, batch, t, l)
    yc_na = ctx_attention(z_c, batch, l)

    mu, w0, w2, a0, a2, k_k, k_a, r_k, ln_w, ln_b = rw_params
    mu = jnp.pad(mu, ((0, 0), (0, RW_PAD_COLS - RW_SHIFT_COLS)))
    zpad = jnp.zeros((RW_RANK, RW_WIDTH), F32)
    w2p = jnp.stack([jnp.concatenate([w2[0], zpad]), jnp.concatenate([zpad, w2[1]])])
    a2p = jnp.stack([jnp.concatenate([a2[0], zpad]), jnp.concatenate([zpad, a2[1]])])
    vecs = (mu, w0, w2p, a0, a2p, k_k.reshape(1, -1), k_a.reshape(1, -1), r_k.reshape(1, -1))
    terms = rwkv_terms(z, batch, t, *vecs)
    terms_c = rwkv_terms(z_c, batch, l, *vecs)
    s_zero = jnp.zeros((RW_HEAD_DIM, RW_HEAD_DIM, LANES), F32)
    yc_f, yc_b, s_ctx = rwkv_scan(*terms_c[:9], s_zero)
    y_f, y_b, _ = rwkv_scan(*terms[:9], s_ctx)
    y_rw = rwkv_readout(y_f, y_b, terms[9], ln_w, ln_b).reshape(batch * t, RW_WIDTH)
    yc_rw = rwkv_readout(yc_f, yc_b, terms_c[9], ln_w, ln_b).reshape(batch * l, RW_WIDTH)

    w_out = w_out.astype(BF16)
    rows_per_seq = max(t // 256, 1)
    x = out_projection(y_na, y_rw, z, w_out, x, mod, lambda i: i // rows_per_seq, "even_out_proj")
    xc = out_projection(yc_na, yc_rw, z_c, w_out, xc, mod, lambda i: batch, "even_out_proj_ctx")
    return x, xc


def _odd_layer(x, xc, mod, norm_w, w_in, w_out, q_norm, w_uq, kv_norm, w_ukv, lb, hg_norm_w, final_w,
               batch, t, l):
    d = x.shape[1]
    o3 = MLA_Q_RANK + MLA_KV_RANK + MLA_ROPE
    o4 = o3 + 4 * HG_FDIM
    zero = jnp.zeros((d, MLA_A_COLS - o3), F32)
    w_ext = jnp.concatenate([w_in[:, o4:], w_in[:, :o3], zero, w_in[:, o3:o4]], axis=1).astype(BF16)
    tiles_per_seq = max(t // 1024, 1)
    z = in_projection(x, norm_w, mod, lambda i: i // tiles_per_seq, w_ext, "odd_in_proj")
    z_c = in_projection(xc, norm_w, mod, lambda i: batch, w_ext, "odd_in_proj_ctx")

    cos, sin = _rope_tables(t)
    qk = MLA_NOPE + MLA_ROPE
    w_uq_pad = jnp.pad(w_uq.reshape(MLA_Q_RANK, MLA_HEADS, qk), ((0, 0), (0, 0), (0, MLA_QK_PAD - qk)))
    w_uq_pad = w_uq_pad.reshape(MLA_Q_RANK, MLA_HEADS * MLA_QK_PAD).astype(BF16)
    w_ukv = w_ukv.astype(BF16)
    q = mla_queries(z, q_norm, w_uq_pad, cos, sin, t)
    k, v = mla_keys_values(z, kv_norm, w_ukv, cos, sin, t)
    ones = jnp.ones((l, LANES), F32)
    kc, vc = mla_keys_values(z_c, kv_norm, w_ukv, ones, jnp.zeros_like(ones), l)
    y_mla = mla_attention(q, k, v, kc, vc, batch, t, l)

    s_zero = jnp.zeros((batch, HG_HEADS, HG_DV, HG_DK), F32)
    _, s_f = hgrn2_pass(z_c, lb, s_zero, batch, l, HG_HEADS, False, emit_state=True)
    _, s_b = hgrn2_pass(z_c, lb, s_zero, batch, l, 2 * HG_HEADS, True, emit_state=True)
    o_f, = hgrn2_pass(z, lb, s_f, batch, t, HG_HEADS, False)
    y_hg, = hgrn2_pass(z, lb, s_b, batch, t, 2 * HG_HEADS, True, prev=o_f, norm_w=hg_norm_w)

    rows_per_seq = max(t // 256, 1)
    return out_projection(y_mla, y_hg, z, w_out.astype(BF16), x, mod, lambda i: i // rows_per_seq,
                          "odd_out_proj", final_w=final_w)


def kernel(x, c, ctx, c_ctx, ada_w, ada_b, norm_w, e_w_in, e_w_out, na_rpb, rw_mu, rw_w0, rw_w2, rw_a0, rw_a2, rw_k_k, rw_k_a, rw_r_k, rw_ln_w, rw_ln_b, o_w_in, o_w_out, mla_q_norm, mla_w_uq, mla_kv_norm, mla_w_ukv, hg_lower_bounds, hg_norm_w, final_norm_w):
    batch, t, d = x.shape
    l = ctx.shape[1]
    assert ada_w.shape[0] == 2, "one even and one odd layer"
    cond = jnp.concatenate([c, c_ctx[None, :], jnp.zeros((8 - batch - 1, d), F32)], axis=0)
    mod = modulation(cond, ada_w, ada_b)
    s = jax.nn.softmax(hg_lower_bounds.astype(F32), axis=0)
    lower = jnp.cumsum(s, axis=0) - s[0]

    xf, xcf = x.reshape(batch * t, d), ctx.reshape(batch * l, d)
    rw_params = (rw_mu[0], rw_w0[0], rw_w2[0], rw_a0[0], rw_a2[0], rw_k_k[0], rw_k_a[0], rw_r_k[0],
                 rw_ln_w[0], rw_ln_b[0])
    xf, xcf = _even_layer(xf, xcf, mod[0].reshape(8, 1, 3 * d), norm_w[0], e_w_in[0], e_w_out[0], na_rpb[0],
                          rw_params, batch, t, l)
    out = _odd_layer(xf, xcf, mod[1].reshape(8, 1, 3 * d), norm_w[1], o_w_in[0], o_w_out[0], mla_q_norm[0],
                     mla_w_uq[0], mla_kv_norm[0], mla_w_ukv[0], lower[1], hg_norm_w[0], final_norm_w,
                     batch, t, l)
    return out.reshape(batch, t, d)
```

```python
import functools
import math

import numpy as np
import jax
import jax.numpy as jnp
from jax import lax
from jax.experimental import pallas as pl
from jax.experimental.pallas import tpu as pltpu

F32 = jnp.float32
BF16 = jnp.bfloat16

GRID_W = 64
NORM_EPS = 1e-6
ROPE_BASE = 10000.0

NA_HEADS = 8
NA_HEAD_DIM = 128
NA_WIDTH = NA_HEADS * NA_HEAD_DIM
NA_WIN_ROWS = 8
NA_WIN_COLS = 16
NA_QROWS = 4
NA_KROWS = 12

RW_HEAD_DIM = 64
RW_HEADS = 16
RW_WIDTH = RW_HEADS * RW_HEAD_DIM
RW_RANK = 64
RW_SHIFT_COLS = 3 * RW_WIDTH + 4 * RW_RANK
RW_PAD_COLS = 3584
RW_GN_EPS = 64e-5
RW_SCAN_CHUNK = 32

MLA_HEADS = 8
MLA_Q_RANK = 512
MLA_KV_RANK = 512
MLA_NOPE = 128
MLA_ROPE = 64
MLA_V = 128
MLA_QK_PAD = 256
MLA_WIDTH = MLA_HEADS * MLA_V
MLA_A_COLS = 1536

HG_HEADS = 8
HG_DK = 128
HG_DV = 128
HG_FDIM = HG_HEADS * HG_DK
HG_WIDTH = HG_HEADS * HG_DV
HG_CHUNK = 32
HG_BLOCK = 256
HG_STEP_HEADS = 2

LANES = 128

GATE_COLS = 2048
EV_Q_BLK = GATE_COLS // LANES
EV_RW_BLK = EV_Q_BLK + 3 * NA_WIDTH // LANES
OD_A_BLK = GATE_COLS // LANES
OD_HG_BLK = OD_A_BLK + MLA_A_COLS // LANES
VMEM_LIMIT = 48 * 1024 * 1024
NEG_INF = -1e30
LOG2E = math.log2(math.e)


def _cparams(n_axes):
    return pltpu.CompilerParams(dimension_semantics=("arbitrary",) * n_axes, vmem_limit_bytes=VMEM_LIMIT)


def _sigmoid(x):
    return 1.0 / (1.0 + jnp.exp(-x))


def _silu(x):
    return x * _sigmoid(x)


def _dot(a, b):
    return jnp.dot(a.astype(BF16), b.astype(BF16), preferred_element_type=F32)


def _dot_nt(a, b):
    return lax.dot_general(a.astype(BF16), b.astype(BF16), (((1,), (1,)), ((), ())), preferred_element_type=F32)


def _dot_tn(a, b):
    return lax.dot_general(a.astype(BF16), b.astype(BF16), (((0,), (0,)), ((), ())), preferred_element_type=F32)


def _mod_kernel(c_ref, w_ref, b_ref, o_ref):
    o_ref[...] = _dot(_silu(c_ref[...]), w_ref[...]) + b_ref[...]


def modulation(cond, ada_w, ada_b, tn=512):
    depth, d, n = ada_w.shape
    return pl.pallas_call(
        _mod_kernel,
        out_shape=jax.ShapeDtypeStruct((depth, 8, n), F32),
        grid=(depth, n // tn),
        in_specs=[pl.BlockSpec((8, d), lambda l, j: (0, 0)),
                  pl.BlockSpec((None, d, tn), lambda l, j: (l, 0, j)),
                  pl.BlockSpec((None, 1, tn), lambda l, j: (l, 0, j))],
        out_specs=pl.BlockSpec((None, 8, tn), lambda l, j: (l, 0, j)),
        compiler_params=_cparams(2),
        name="adaln_modulation",
    )(cond, ada_w, ada_b.reshape(depth, 1, n))


def _inproj_kernel(x_ref, nw_ref, shift_ref, scale_ref, w_ref, o_ref, h_ref):
    @pl.when(pl.program_id(1) == 0)
    def _():
        x = x_ref[...]
        y = x * lax.rsqrt(jnp.mean(x * x, axis=-1, keepdims=True) + NORM_EPS) * nw_ref[...]
        h_ref[...] = (y * (1.0 + scale_ref[...]) + shift_ref[...]).astype(BF16)

    o_ref[...] = jnp.dot(h_ref[...], w_ref[...], preferred_element_type=F32)


def in_projection(x, norm_w, mod, mod_row, w, name, tm=1024, tn=512):
    m, d = x.shape
    n = w.shape[1]
    tm = min(tm, m)
    return pl.pallas_call(
        _inproj_kernel,
        out_shape=jax.ShapeDtypeStruct((m, n), F32),
        grid=(m // tm, n // tn),
        in_specs=[pl.BlockSpec((tm, d), lambda i, j: (i, 0)),
                  pl.BlockSpec((1, d), lambda i, j: (0, 0)),
                  pl.BlockSpec((None, 1, d), lambda i, j: (mod_row(i), 0, 0)),
                  pl.BlockSpec((None, 1, d), lambda i, j: (mod_row(i), 0, 1)),
                  pl.BlockSpec((d, tn), lambda i, j: (0, j))],
        out_specs=pl.BlockSpec((tm, tn), lambda i, j: (i, j)),
        scratch_shapes=[pltpu.VMEM((tm, d), BF16)],
        compiler_params=_cparams(2),
        name=name,
    )(x, norm_w.reshape(1, d), mod, mod, w)


def _outproj_kernel(ya_ref, yb_ref, g_ref, w_ref, x_ref, gm_ref, *rest, ka, final):
    o_ref = rest[-1]
    sg = _silu(g_ref[...])
    acc = _dot(ya_ref[...].astype(F32) * sg[:, :ka], w_ref[:ka, :])
    acc += _dot(yb_ref[...].astype(F32) * sg[:, ka:], w_ref[ka:, :])
    out = x_ref[...] + gm_ref[...] * acc
    if final:
        out = out * lax.rsqrt(jnp.mean(out * out, axis=-1, keepdims=True) + NORM_EPS) * rest[0][...]
    o_ref[...] = out


def out_projection(ya, yb, gate, w, x, mod, mod_row, name, final_w=None, tm=256):
    m, d = x.shape
    ka, kb = ya.shape[1], yb.shape[1]
    tm = min(tm, m)
    ins = [ya, yb, gate, w, x, mod]
    specs = [pl.BlockSpec((tm, ka), lambda i: (i, 0)),
             pl.BlockSpec((tm, kb), lambda i: (i, 0)),
             pl.BlockSpec((tm, ka + kb), lambda i: (i, 0)),
             pl.BlockSpec((ka + kb, d), lambda i: (0, 0)),
             pl.BlockSpec((tm, d), lambda i: (i, 0)),
             pl.BlockSpec((None, 1, d), lambda i: (mod_row(i), 0, 2))]
    if final_w is not None:
        ins.append(final_w.reshape(1, d))
        specs.append(pl.BlockSpec((1, d), lambda i: (0, 0)))
    return pl.pallas_call(
        functools.partial(_outproj_kernel, ka=ka, final=final_w is not None),
        out_shape=jax.ShapeDtypeStruct((m, d), F32),
        grid=(m // tm,),
        in_specs=specs,
        out_specs=pl.BlockSpec((tm, d), lambda i: (i, 0)),
        compiler_params=_cparams(1),
        name=name,
    )(*ins)


def _softmax_pv(parts, scale):
    c = scale * LOG2E
    logits = [s * c if bias is None else s * c + bias for s, bias, _ in parts]
    m = functools.reduce(jnp.maximum, [jnp.max(x, axis=-1, keepdims=True) for x in logits])
    ps = [jnp.exp2(x - m) for x in logits]
    denom = functools.reduce(lambda a, b: a + b, [jnp.sum(p, axis=-1, keepdims=True) for p in ps])
    num = functools.reduce(lambda a, b: a + b, [_dot(p, v) for p, (_, _, v) in zip(ps, parts)])
    return num / denom


def _na_kernel(idx_ref, q_ref, k_ref, v_ref, kc_ref, vc_ref, tab_ref, o_ref, bias_ref, *, rows, scale):
    j = pl.program_id(1)
    npair = NA_KROWS // 2

    @pl.when(pl.program_id(2) == 0)
    def _():
        for qr in range(NA_QROWS):
            for m in range(npair):
                tile = tab_ref[idx_ref[(j * NA_QROWS + qr) * npair + m]]
                bias_ref[qr * GRID_W:(qr + 1) * GRID_W, m * LANES:(m + 1) * LANES] = tile

    start = pl.multiple_of(jnp.clip(NA_QROWS * j - NA_WIN_ROWS // 2, 0, rows - NA_KROWS) * GRID_W, GRID_W)
    nk = NA_KROWS * GRID_W
    q = q_ref[...]
    s_loc = _dot_nt(q, k_ref[pl.ds(start, nk), :])
    s_ctx = _dot_nt(q, kc_ref[...])
    o_ref[...] = _softmax_pv([(s_loc, bias_ref[...], v_ref[pl.ds(start, nk), :]), (s_ctx, None, vc_ref[...])], scale)


def _na_tile_ids(rows):
    kh = min(NA_WIN_ROWS, rows)
    n = 2 * NA_WIN_ROWS - 1
    ids = []
    for j in range(rows // NA_QROWS):
        ks = int(np.clip(NA_QROWS * j - NA_WIN_ROWS // 2, 0, rows - NA_KROWS))
        for qr in range(NA_QROWS):
            r = NA_QROWS * j + qr
            r0 = int(np.clip(r - kh // 2, 0, rows - kh))
            for m in range(NA_KROWS // 2):
                kr = ks + 2 * m
                dy = kr - r + NA_WIN_ROWS - 1
                first, second = r0 <= kr < r0 + kh, r0 <= kr + 1 < r0 + kh
                ids.append(dy if first and second else n + dy if first else 2 * n + dy + 1 if second else 3 * n)
    return np.array(ids, np.int32)


def _na_bias_tiles(rpb):
    col = np.arange(GRID_W)
    cs = np.clip(col - NA_WIN_COLS // 2, 0, GRID_W - NA_WIN_COLS)
    kc = col[None, :]
    in_win = (kc >= cs[:, None]) & (kc < cs[:, None] + NA_WIN_COLS)
    dx = np.where(in_win, kc - col[:, None] + NA_WIN_COLS - 1, 0)
    tab = jnp.where(in_win[None, None], rpb[:, :, dx] * LOG2E, NEG_INF)
    neg = jnp.full_like(tab[:, :1], NEG_INF)
    nxt = jnp.concatenate([tab[:, 1:], neg], axis=1)
    both = jnp.concatenate([tab, nxt], axis=-1)
    first = jnp.concatenate([tab, jnp.broadcast_to(neg, tab.shape)], axis=-1)
    second = jnp.concatenate([jnp.broadcast_to(neg, tab.shape), tab], axis=-1)
    return jnp.concatenate([both, first, second, jnp.concatenate([neg, neg], axis=-1)], axis=1)


def na_attention(z, z_c, rpb, batch, t, l):
    rows = t // GRID_W
    nj = rows // NA_QROWS
    tq = NA_QROWS * GRID_W
    ids = jnp.asarray(_na_tile_ids(rows))
    tiles = _na_bias_tiles(rpb)
    h8 = NA_HEADS
    return pl.pallas_call(
        functools.partial(_na_kernel, rows=rows, scale=NA_HEAD_DIM ** -0.5),
        out_shape=jax.ShapeDtypeStruct((batch * t, NA_WIDTH), F32),
        grid_spec=pltpu.PrefetchScalarGridSpec(
            num_scalar_prefetch=1,
            grid=(h8, nj, batch),
            in_specs=[pl.BlockSpec((tq, LANES), lambda h, j, b, g: (b * nj + j, EV_Q_BLK + h)),
                      pl.BlockSpec((t, LANES), lambda h, j, b, g: (b, EV_Q_BLK + h8 + h)),
                      pl.BlockSpec((t, LANES), lambda h, j, b, g: (b, EV_Q_BLK + 2 * h8 + h)),
                      pl.BlockSpec((l, LANES), lambda h, j, b, g: (b, EV_Q_BLK + h8 + h)),
                      pl.BlockSpec((l, LANES), lambda h, j, b, g: (b, EV_Q_BLK + 2 * h8 + h)),
                      pl.BlockSpec((None,) + tiles.shape[1:], lambda h, j, b, g: (h, 0, 0, 0))],
            out_specs=pl.BlockSpec((tq, LANES), lambda h, j, b, g: (b * nj + j, h)),
            scratch_shapes=[pltpu.VMEM((tq, NA_KROWS * GRID_W), F32)]),
        compiler_params=_cparams(3),
        name="na_attention",
    )(ids, z, z, z, z_c, z_c, tiles)


def _dense_attn_kernel(q_ref, k_ref, v_ref, o_ref, *, scale):
    o_ref[...] = _softmax_pv([(_dot_nt(q_ref[...], k_ref[...]), None, v_ref[...])], scale)


def ctx_attention(z_c, batch, l):
    h8 = NA_HEADS
    return pl.pallas_call(
        functools.partial(_dense_attn_kernel, scale=NA_HEAD_DIM ** -0.5),
        out_shape=jax.ShapeDtypeStruct((batch * l, NA_WIDTH), F32),
        grid=(batch, h8),
        in_specs=[pl.BlockSpec((l, LANES), lambda b, h: (b, EV_Q_BLK + h)),
                  pl.BlockSpec((l, LANES), lambda b, h: (b, EV_Q_BLK + h8 + h)),
                  pl.BlockSpec((l, LANES), lambda b, h: (b, EV_Q_BLK + 2 * h8 + h))],
        out_specs=pl.BlockSpec((l, LANES), lambda b, h: (b, h)),
        compiler_params=_cparams(2),
        name="ctx_attention",
    )(z_c, z_c, z_c)


def _head_sum(x):
    first = lax.broadcasted_iota(jnp.int32, x.shape, 1) < RW_HEAD_DIM
    s0 = jnp.sum(jnp.where(first, x, 0.0), axis=-1, keepdims=True)
    s1 = jnp.sum(jnp.where(first, 0.0, x), axis=-1, keepdims=True)
    return jnp.where(first, s0, s1)


def _softplus(x):
    return jnp.maximum(x, 0.0) + jnp.log1p(jnp.exp(-jnp.abs(x)))


def _rw_terms_kernel(ur_ref, uk_ref, uv_ref, uwa_ref, pr_ref, pk_ref, pv_ref, pwa_ref, nr_ref, nk_ref, nv_ref,
                     nwa_ref, mu_ref, w0_ref, w2_ref, a0_ref, a2_ref, kk_ref, ka_ref, rk_ref,
                     r_o, v_o, kkn_o, dec0_o, dec1_o, kd0_o, kd1_o, b0_o, b1_o, bonus_o, *, nt):
    i = pl.program_id(1)
    tm = ur_ref.shape[0]

    def shifted(u_ref, p_ref, n_ref, col0):
        u = u_ref[...]
        mu = mu_ref[:, col0:col0 + u.shape[1]]
        row = lax.broadcasted_iota(jnp.int32, u.shape, 0)
        before = jnp.where(i == 0, 0.0, p_ref[7:8, :])
        after = jnp.where(i == nt - 1, 0.0, n_ref[0:1, :])
        prev = jnp.where(row == 0, before, pltpu.roll(u, 1, axis=0))
        nxt = jnp.where(row == tm - 1, after, pltpu.roll(u, tm - 1, axis=0))
        return u + mu[0:1] * (prev - u) + mu[1:2] * (nxt - u)

    r_all = shifted(ur_ref, pr_ref, nr_ref, 0)
    k_all = shifted(uk_ref, pk_ref, nk_ref, RW_WIDTH)
    v_all = shifted(uv_ref, pv_ref, nv_ref, 2 * RW_WIDTH)
    wa = shifted(uwa_ref, pwa_ref, nwa_ref, 3 * RW_WIDTH)
    wd = jnp.tanh(wa[:, :LANES]).astype(BF16)
    ad = wa[:, LANES:].astype(BF16)
    for p in range(RW_WIDTH // LANES):
        sl = slice(p * LANES, (p + 1) * LANES)
        r, k, v = r_all[:, sl], k_all[:, sl], v_all[:, sl]
        kk = k * kk_ref[:, sl]
        kk = kk / jnp.maximum(jnp.sqrt(_head_sum(kk * kk)), 1e-12)
        r_o[:, p, :] = r
        v_o[:, p, :] = v
        kkn_o[:, p, :] = kk
        kd_sum = jnp.zeros_like(k)
        for d, (dec_o, kd_o, b_o) in enumerate(((dec0_o, kd0_o, b0_o), (dec1_o, kd1_o, b1_o))):
            w = w0_ref[d:d + 1, sl] + jnp.dot(wd, w2_ref[d, :, sl].astype(BF16), preferred_element_type=F32)
            dec_o[:, p, :] = jnp.exp(-jnp.exp(-_softplus(-w) - 0.5))
            a = _sigmoid(a0_ref[d:d + 1, sl] + jnp.dot(ad, a2_ref[d, :, sl].astype(BF16), preferred_element_type=F32))
            kd = k * (1.0 + (a - 1.0) * ka_ref[:, sl])
            kd_o[:, p, :] = kd
            b_o[:, p, :] = kk * a
            kd_sum = kd_sum + kd
        bonus_o[:, p, :] = _head_sum(r * kd_sum * rk_ref[:, sl]) * v


def rwkv_terms(z, n_seq, t, mu, w0, w2p, a0, a2p, k_k, k_a, r_k, tm=256):
    tm = min(tm, t)
    nt = t // tm
    npair = RW_WIDTH // LANES
    wide0 = EV_RW_BLK * LANES // RW_WIDTH
    wa_blk = (EV_RW_BLK + 3 * npair) // 2
    n8 = tm // 8

    def cur(width, cb):
        return pl.BlockSpec((tm, width), lambda b, i: (b * nt + i, cb))

    def before(width, cb):
        return pl.BlockSpec((8, width), lambda b, i: (jnp.maximum((b * nt + i) * n8 - 1, 0), cb))

    def after(width, cb):
        return pl.BlockSpec((8, width), lambda b, i: (jnp.minimum((b * nt + i + 1) * n8, n_seq * nt * n8 - 1), cb))

    cols = [(RW_WIDTH, wide0), (RW_WIDTH, wide0 + 1), (RW_WIDTH, wide0 + 2), (2 * LANES, wa_blk)]
    whole = lambda a: pl.BlockSpec(a.shape, lambda b, i: (0,) * a.ndim)
    params = (mu, w0, w2p, a0, a2p, k_k, k_a, r_k)
    out = pl.BlockSpec((tm, npair, LANES), lambda b, i: (i, b, 0))
    return pl.pallas_call(
        functools.partial(_rw_terms_kernel, nt=nt),
        out_shape=[jax.ShapeDtypeStruct((t, n_seq * npair, LANES), F32)] * 10,
        grid=(n_seq, nt),
        in_specs=([cur(*c) for c in cols] + [before(*c) for c in cols] + [after(*c) for c in cols]
                  + [whole(a) for a in params]),
        out_specs=[out] * 10,
        compiler_params=_cparams(2),
        name="rwkv_terms",
    )(*([z] * 12), *params)


def _to_lanes(n):
    q = jnp.concatenate([n, pltpu.roll(n, RW_HEAD_DIM, axis=1)], axis=0)
    return q.T[:RW_HEAD_DIM]


def _rw_scan_kernel(rf, rb, vf, vb, af, ab, w0, w1, k0, k1, b0, b1, s0_ref, yf_ref, yb_ref, sfin_ref,
                    s_ref, zc_ref, zn_ref, zl_ref, g_ref, sa_ref, ybuf_ref, *, chunk):
    n = RW_HEAD_DIM
    srcs = ((rf, rb), (w0, w1), (k0, k1), (vf, vb), (af, ab), (b0, b1))
    R, K, V, A, B = range(5)

    @pl.when(pl.program_id(0) == 0)
    def _():
        s_ref[...] = s0_ref[...]

    def convert(t_src):
        r, w, k, v, kk, b = (_to_lanes(jnp.concatenate([f[t_src], bw[chunk - 1 - t_src]], axis=0))
                             for f, bw in srcs)
        g_prev = g_ref[0]
        g = g_prev * w
        g_ref[0] = g
        inv = 1.0 / g
        zl_ref[R] = r * g
        zl_ref[K] = k * inv
        zl_ref[V] = v
        zl_ref[A] = -(kk * g_prev)
        zl_ref[B] = b * inv

    def emit_y(t):
        rows = _to_lanes(ybuf_ref[...])
        yf_ref[t] = rows[:n // 2]
        yb_ref[chunk - 1 - t] = rows[n // 2:]

    g_ref[0] = jnp.ones((n, LANES), F32)
    ybuf_ref[...] = jnp.zeros((n, LANES), F32)
    convert(0)
    zn_ref[...] = zl_ref[...]
    g_ref[1] = g_ref[0]
    convert(1)
    sa = [jnp.zeros((n, LANES), F32), jnp.zeros((n, LANES), F32)]
    for k in range(n):
        sa[k % 2] = sa[k % 2] + s_ref[k] * zn_ref[A, k:k + 1, :]
    sa_ref[...] = sa[0] + sa[1]

    def step(t, carry):
        emit_y(jnp.maximum(t - 1, 0))
        zc_ref[...] = zn_ref[...]
        zn_ref[...] = zl_ref[...]
        g_ref[1] = jnp.where(t <= chunk - 2, g_ref[0], g_ref[1])
        convert(jnp.minimum(t + 2, chunk - 1))
        sa = sa_ref[...]
        vt = zc_ref[V]
        y = [jnp.zeros((n, LANES), F32), jnp.zeros((n, LANES), F32)]
        sa_next = [jnp.zeros((n, LANES), F32), jnp.zeros((n, LANES), F32)]
        for k in range(n):
            sk = s_ref[k] + sa * zc_ref[B, k:k + 1, :] + vt * zc_ref[K, k:k + 1, :]
            s_ref[k] = sk
            y[k % 2] = y[k % 2] + sk * zc_ref[R, k:k + 1, :]
            sa_next[k % 2] = sa_next[k % 2] + sk * zn_ref[A, k:k + 1, :]
        ybuf_ref[...] = y[0] + y[1]
        sa_ref[...] = sa_next[0] + sa_next[1]
        return carry

    lax.fori_loop(0, chunk, step, 0)
    emit_y(chunk - 1)
    for k in range(n):
        s_ref[k] = s_ref[k] * g_ref[1, k:k + 1, :]
    sfin_ref[...] = s_ref[...]


def rwkv_scan(r, v, kkn, dec0, dec1, kd0, kd1, b0, b1, s0):
    t, nq, _ = r.shape
    n = RW_HEAD_DIM
    chunk = RW_SCAN_CHUNK
    nc = t // chunk
    fwd = pl.BlockSpec((chunk, nq, LANES), lambda i: (i, 0, 0))
    bwd = pl.BlockSpec((chunk, nq, LANES), lambda i: (nc - 1 - i, 0, 0))
    state = pl.BlockSpec((n, n, LANES), lambda i: (0, 0, 0))
    slab = jax.ShapeDtypeStruct((t, nq, LANES), F32)
    return pl.pallas_call(
        functools.partial(_rw_scan_kernel, chunk=chunk),
        out_shape=[slab, slab, jax.ShapeDtypeStruct((n, n, LANES), F32)],
        grid=(nc,),
        in_specs=[fwd, bwd, fwd, bwd, fwd, bwd, fwd, bwd, fwd, bwd, fwd, bwd, state],
        out_specs=[fwd, bwd, state],
        scratch_shapes=[pltpu.VMEM((n, n, LANES), F32), pltpu.VMEM((5, n, LANES), F32),
                        pltpu.VMEM((5, n, LANES), F32), pltpu.VMEM((5, n, LANES), F32),
                        pltpu.VMEM((2, n, LANES), F32), pltpu.VMEM((n, LANES), F32), pltpu.VMEM((n, LANES), F32)],
        compiler_params=_cparams(1),
        name="rwkv_scan",
    )(r, r, v, v, kkn, kkn, dec0, dec1, kd0, kd1, b0, b1, s0)


def _rw_readout_kernel(yf_ref, yb_ref, bonus_ref, lw_ref, lb_ref, o_ref):
    npair = RW_WIDTH // LANES
    for q in range(yf_ref.shape[1]):
        b, p = divmod(q, npair)
        sl = slice(p * LANES, (p + 1) * LANES)
        y = yf_ref[:, q, :] + yb_ref[:, q, :]
        mu = _head_sum(y) * (1.0 / RW_HEAD_DIM)
        d = y - mu
        var = _head_sum(d * d) * (1.0 / RW_HEAD_DIM)
        o_ref[b, :, sl] = d * lax.rsqrt(var + RW_GN_EPS) * lw_ref[:, sl] + lb_ref[:, sl] + bonus_ref[:, q, :]


def rwkv_readout(y_f, y_b, bonus, ln_w, ln_b, tm=256):
    t, nq, _ = y_f.shape
    tm = min(tm, t)
    batch = nq // (RW_WIDTH // LANES)
    slab = pl.BlockSpec((tm, nq, LANES), lambda i: (i, 0, 0))
    vec = pl.BlockSpec((1, RW_WIDTH), lambda i: (0, 0))
    return pl.pallas_call(
        _rw_readout_kernel,
        out_shape=jax.ShapeDtypeStruct((batch, t, RW_WIDTH), F32),
        grid=(t // tm,),
        in_specs=[slab, slab, slab, vec, vec],
        out_specs=pl.BlockSpec((batch, tm, RW_WIDTH), lambda i: (0, i, 0)),
        compiler_params=_cparams(1),
        name="rwkv_readout",
    )(y_f, y_b, bonus, ln_w.reshape(1, -1), ln_b.reshape(1, -1))


def _rms(x, w):
    return x * lax.rsqrt(jnp.mean(x * x, axis=-1, keepdims=True) + NORM_EPS) * w


def _rope(x, cos, sin):
    odd = (lax.broadcasted_iota(jnp.int32, x.shape, 1) & 1) == 1
    swapped = jnp.where(odd, pltpu.roll(x, 1, axis=1), pltpu.roll(x, LANES - 1, axis=1))
    return x * cos + swapped * sin


def _mla_q_kernel(c_ref, nw_ref, w_ref, cos_ref, sin_ref, q_ref, h_ref):
    @pl.when(pl.program_id(1) == 0)
    def _():
        h_ref[...] = _rms(c_ref[...], nw_ref[...]).astype(BF16)

    acc = jnp.dot(h_ref[...], w_ref[...], preferred_element_type=F32)
    q_ref[:, :MLA_NOPE] = acc[:, :MLA_NOPE].astype(BF16)
    q_ref[:, MLA_NOPE:] = _rope(acc[:, MLA_NOPE:], cos_ref[...], sin_ref[...]).astype(BF16)


def mla_queries(z, q_norm, w_uq_pad, cos, sin, t, tm=512):
    m = z.shape[0]
    tm = min(tm, t)
    nt = t // tm
    return pl.pallas_call(
        _mla_q_kernel,
        out_shape=jax.ShapeDtypeStruct((m, MLA_HEADS * MLA_QK_PAD), BF16),
        grid=(m // tm, MLA_HEADS),
        in_specs=[pl.BlockSpec((tm, MLA_Q_RANK), lambda i, h: (i, GATE_COLS // MLA_Q_RANK)),
                  pl.BlockSpec((1, MLA_Q_RANK), lambda i, h: (0, 0)),
                  pl.BlockSpec((MLA_Q_RANK, MLA_QK_PAD), lambda i, h: (0, h)),
                  pl.BlockSpec((tm, LANES), lambda i, h: (i % nt, 0)),
                  pl.BlockSpec((tm, LANES), lambda i, h: (i % nt, 0))],
        out_specs=pl.BlockSpec((tm, MLA_QK_PAD), lambda i, h: (i, h)),
        scratch_shapes=[pltpu.VMEM((tm, MLA_Q_RANK), BF16)],
        compiler_params=_cparams(2),
        name="mla_queries",
    )(z, q_norm.reshape(1, -1), w_uq_pad, cos, sin)


def _mla_kv_kernel(c_ref, pe_ref, nw_ref, w_ref, cos_ref, sin_ref, k_ref, v_ref, h_ref):
    @pl.when(pl.program_id(1) == 0)
    def _():
        h_ref[...] = _rms(c_ref[...], nw_ref[...]).astype(BF16)

    acc = jnp.dot(h_ref[...], w_ref[...], preferred_element_type=F32)
    k_ref[:, :MLA_NOPE] = acc[:, :MLA_NOPE].astype(BF16)
    k_ref[:, MLA_NOPE:] = _rope(pe_ref[...], cos_ref[...], sin_ref[...]).astype(BF16)
    v_ref[...] = acc[:, MLA_NOPE:].astype(BF16)


def mla_keys_values(z, kv_norm, w_ukv, cos, sin, t, tm=512):
    m = z.shape[0]
    tm = min(tm, t)
    nt = t // tm
    return pl.pallas_call(
        _mla_kv_kernel,
        out_shape=[jax.ShapeDtypeStruct((m, MLA_HEADS * MLA_QK_PAD), BF16),
                   jax.ShapeDtypeStruct((m, MLA_WIDTH), BF16)],
        grid=(m // tm, MLA_HEADS),
        in_specs=[pl.BlockSpec((tm, MLA_KV_RANK), lambda i, h: (i, (GATE_COLS + MLA_Q_RANK) // MLA_KV_RANK)),
                  pl.BlockSpec((tm, LANES), lambda i, h: (i, OD_A_BLK + (MLA_Q_RANK + MLA_KV_RANK) // LANES)),
                  pl.BlockSpec((1, MLA_KV_RANK), lambda i, h: (0, 0)),
                  pl.BlockSpec((MLA_KV_RANK, MLA_NOPE + MLA_V), lambda i, h: (0, h)),
                  pl.BlockSpec((tm, LANES), lambda i, h: (i % nt, 0)),
                  pl.BlockSpec((tm, LANES), lambda i, h: (i % nt, 0))],
        out_specs=[pl.BlockSpec((tm, MLA_QK_PAD), lambda i, h: (i, h)),
                   pl.BlockSpec((tm, MLA_V), lambda i, h: (i, h))],
        scratch_shapes=[pltpu.VMEM((tm, MLA_KV_RANK), BF16)],
        compiler_params=_cparams(2),
        name="mla_keys_values",
    )(z, z, kv_norm.reshape(1, -1), w_ukv, cos, sin)


def _mla_attn_kernel(q_ref, k_ref, v_ref, kc_ref, vc_ref, o_ref, *, scale):
    q = q_ref[...]
    o_ref[...] = _softmax_pv([(_dot_nt(q, k_ref[...]), None, v_ref[...]),
                              (_dot_nt(q, kc_ref[...]), None, vc_ref[...])], scale)


def mla_attention(q, k, v, kc, vc, batch, t, l, tq=512):
    tq = min(tq, t)
    nq = t // tq
    return pl.pallas_call(
        functools.partial(_mla_attn_kernel, scale=(MLA_NOPE + MLA_ROPE) ** -0.5),
        out_shape=jax.ShapeDtypeStruct((batch * t, MLA_WIDTH), F32),
        grid=(batch, MLA_HEADS, nq),
        in_specs=[pl.BlockSpec((tq, MLA_QK_PAD), lambda b, h, i: (b * nq + i, h)),
                  pl.BlockSpec((t, MLA_QK_PAD), lambda b, h, i: (b, h)),
                  pl.BlockSpec((t, MLA_V), lambda b, h, i: (b, h)),
                  pl.BlockSpec((l, MLA_QK_PAD), lambda b, h, i: (b, h)),
                  pl.BlockSpec((l, MLA_V), lambda b, h, i: (b, h))],
        out_specs=pl.BlockSpec((tq, MLA_V), lambda b, h, i: (b * nq + i, h)),
        compiler_params=_cparams(3),
        name="mla_attention",
    )(q, k, v, kc, vc)


def _rope_tables(t):
    tok = np.arange(t)
    pos = np.stack([tok // GRID_W, tok % GRID_W], axis=-1).astype(np.float32)
    n_freq = MLA_ROPE // 4
    inv = (ROPE_BASE ** (-jnp.arange(n_freq, dtype=F32) / n_freq))
    ang = (jnp.asarray(pos)[:, :, None] * inv).reshape(t, MLA_ROPE // 2)
    cos = jnp.repeat(jnp.cos(ang), 2, axis=-1)
    sin = jnp.repeat(jnp.sin(ang), 2, axis=-1) * jnp.tile(jnp.array([-1.0, 1.0], F32), MLA_ROPE // 2)
    pad = ((0, 0), (0, LANES - MLA_ROPE))
    return jnp.pad(cos, pad), jnp.pad(sin, pad)


def _split3(x):
    hi = x.astype(BF16)
    r1 = x - hi.astype(F32)
    mid = r1.astype(BF16)
    lo = (r1 - mid.astype(F32)).astype(BF16)
    return hi, mid, lo


def _hg_kernel(q_ref, f_ref, v_ref, lb_ref, s0_ref, *rest, reverse, finalize, emit_state):
    st_ref = rest[-1]
    rest = rest[:-1]
    if finalize:
        prev_ref, nw_ref = rest[0], rest[1]
        rest = rest[2:]
    o_ref = rest[0]

    @pl.when(pl.program_id(2) == 0)
    def _():
        st_ref[...] = s0_ref[...]

    bt = q_ref.shape[0]
    nchunk = bt // HG_CHUNK
    order = range(nchunk - 1, -1, -1) if reverse else range(nchunk)
    ri = lax.broadcasted_iota(jnp.int32, (bt, bt), 0)
    ci = lax.broadcasted_iota(jnp.int32, (bt, bt), 1)
    same = (ri // HG_CHUNK) == (ci // HG_CHUNK)
    tri = same & ((ci >= ri) if reverse else (ci <= ri))
    tri_f = tri.astype(F32)
    tri_b = tri.astype(BF16)

    lb = lb_ref[...]
    fz = f_ref[...]
    log_sig = jnp.minimum(fz, 0.0) - jnp.log1p(jnp.exp(-jnp.abs(fz)))
    la = jnp.log(lb)
    lbb = jnp.log1p(-lb) + log_sig
    log_f = jnp.maximum(la, lbb) + jnp.log1p(jnp.exp(-jnp.abs(la - lbb)))
    kf_all = (1.0 - lb) * _sigmoid(-fz)
    pieces = jnp.dot(tri_b, jnp.concatenate(_split3(log_f), axis=1), preferred_element_type=F32)
    width = log_f.shape[1]
    cum_all = pieces[:, :width] + pieces[:, width:2 * width] + pieces[:, 2 * width:]

    for hh in range(HG_STEP_HEADS):
        sl = slice(hh * HG_DK, (hh + 1) * HG_DK)
        cum, kf, v = cum_all[:, sl], kf_all[:, sl], v_ref[:, sl]
        last = [cum[c * HG_CHUNK:c * HG_CHUNK + 1] if reverse else cum[(c + 1) * HG_CHUNK - 1:(c + 1) * HG_CHUNK]
                for c in range(nchunk)]
        tot = jnp.concatenate([jnp.broadcast_to(x, (HG_CHUNK, HG_DK)) for x in last], axis=0)
        q_in = q_ref[:, sl] * jnp.exp(cum)
        k_in = kf * jnp.exp(-cum)
        k_end = kf * jnp.exp(tot - cum)
        o_intra = _dot(_dot_nt(q_in, k_in) * tri_f, v)
        st = st_ref[hh]
        for c in order:
            rows = slice(c * HG_CHUNK, (c + 1) * HG_CHUNK)
            o_c = o_intra[rows] + _dot_nt(q_in[rows], st)
            if finalize:
                o_c = o_c + prev_ref[rows, sl]
                o_c = o_c * lax.rsqrt(jnp.mean(o_c * o_c, axis=-1, keepdims=True) + NORM_EPS) * nw_ref[...]
            o_ref[rows, sl] = o_c
            st = st * jnp.exp(last[c]) + _dot_tn(v[rows], k_end[rows])
        st_ref[hh] = st
        if emit_state:
            rest[1][hh] = st


def hgrn2_pass(z, lb, s0, batch, t, f_off, reverse, prev=None, norm_w=None, emit_state=False):
    bt = min(HG_BLOCK, t)
    nblk = t // bt
    nh = HG_STEP_HEADS
    wide = nh * LANES
    finalize = prev is not None
    pos = (lambda i: nblk - 1 - i) if reverse else (lambda i: i)
    blk = lambda off: pl.BlockSpec((bt, wide), lambda b, h, i, off=off: (b * nblk + pos(i), off // nh + h))
    state = pl.BlockSpec((None, nh, HG_DV, HG_DK), lambda b, h, i: (b, h, 0, 0))
    ins = [z, z, z, lb.reshape(1, -1), s0]
    specs = [blk(OD_HG_BLK), blk(OD_HG_BLK + f_off), blk(OD_HG_BLK + 3 * HG_HEADS),
             pl.BlockSpec((1, wide), lambda b, h, i: (0, h)), state]
    if finalize:
        ins += [prev, norm_w.reshape(1, -1)]
        specs += [blk(0), pl.BlockSpec((1, HG_DV), lambda b, h, i: (0, 0))]
    out_shape = [jax.ShapeDtypeStruct((batch * t, HG_WIDTH), F32)]
    out_specs = [blk(0)]
    if emit_state:
        out_shape.append(jax.ShapeDtypeStruct((batch, HG_HEADS, HG_DV, HG_DK), F32))
        out_specs.append(state)
    return pl.pallas_call(
        functools.partial(_hg_kernel, reverse=reverse, finalize=finalize, emit_state=emit_state),
        out_shape=out_shape,
        grid=(batch, HG_HEADS // nh, nblk),
        in_specs=specs,
        out_specs=out_specs,
        scratch_shapes=[pltpu.VMEM((nh, HG_DV, HG_DK), F32)],
        compiler_params=_cparams(3),
        name="hgrn2_" + ("bwd" if reverse else "fwd"),
    )(*ins)


def _even_layer(x, xc, mod, norm_w, w_in, w_out, rpb, rw_params, batch, t, l):
    d = x.shape[1]
    s2 = 3 * NA_WIDTH + RW_SHIFT_COLS
    zero = jnp.zeros((d, RW_PAD_COLS - RW_SHIFT_COLS), BF16)
    w_in = w_in.astype(BF16)
    w_ext = jnp.concatenate([w_in[:, s2:], w_in[:, :s2], zero], axis=1)
    tiles_per_seq = max(t // 1024, 1)
    z = in_projection(x, norm_w, mod, lambda i: i // tiles_per_seq, w_ext, "even_in_proj")
    z_c = in_projection(xc, norm_w, mod, lambda i: batch, w_ext, "even_in_proj_ctx")

    y_na = na_attention(z, z_c, rpb, batch, t, l)
    yc_na = ctx_attention(z_c, batch, l)

    mu, w0, w2, a0, a2, k_k, k_a, r_k, ln_w, ln_b = rw_params
    mu = jnp.pad(mu, ((0, 0), (0, RW_PAD_COLS - RW_SHIFT_COLS)))
    zpad = jnp.zeros((RW_RANK, RW_WIDTH), F32)
    w2p = jnp.stack([jnp.concatenate([w2[0], zpad]), jnp.concatenate([zpad, w2[1]])])
    a2p = jnp.stack([jnp.concatenate([a2[0], zpad]), jnp.concatenate([zpad, a2[1]])])
    vecs = (mu, w0, w2p, a0, a2p, k_k.reshape(1, -1), k_a.reshape(1, -1), r_k.reshape(1, -1))
    terms = rwkv_terms(z, batch, t, *vecs)
    terms_c = rwkv_terms(z_c, batch, l, *vecs)
    s_zero = jnp.zeros((RW_HEAD_DIM, RW_HEAD_DIM, LANES), F32)
    yc_f, yc_b, s_ctx = rwkv_scan(*terms_c[:9], s_zero)
    y_f, y_b, _ = rwkv_scan(*terms[:9], s_ctx)
    y_rw = rwkv_readout(y_f, y_b, terms[9], ln_w, ln_b).reshape(batch * t, RW_WIDTH)
    yc_rw = rwkv_readout(yc_f, yc_b, terms_c[9], ln_w, ln_b).reshape(batch * l, RW_WIDTH)

    w_out = w_out.astype(BF16)
    rows_per_seq = max(t // 256, 1)
    x = out_projection(y_na, y_rw, z, w_out, x, mod, lambda i: i // rows_per_seq, "even_out_proj")
    xc = out_projection(yc_na, yc_rw, z_c, w_out, xc, mod, lambda i: batch, "even_out_proj_ctx")
    return x, xc


def _odd_layer(x, xc, mod, norm_w, w_in, w_out, q_norm, w_uq, kv_norm, w_ukv, lb, hg_norm_w, final_w,
               batch, t, l):
    d = x.shape[1]
    o3 = MLA_Q_RANK + MLA_KV_RANK + MLA_ROPE
    o4 = o3 + 4 * HG_FDIM
    zero = jnp.zeros((d, MLA_A_COLS - o3), BF16)
    w_in = w_in.astype(BF16)
    w_ext = jnp.concatenate([w_in[:, o4:], w_in[:, :o3], zero, w_in[:, o3:o4]], axis=1)
    tiles_per_seq = max(t // 1024, 1)
    z = in_projection(x, norm_w, mod, lambda i: i // tiles_per_seq, w_ext, "odd_in_proj")
    z_c = in_projection(xc, norm_w, mod, lambda i: batch, w_ext, "odd_in_proj_ctx")

    cos, sin = _rope_tables(t)
    qk = MLA_NOPE + MLA_ROPE
    w_uq_pad = jnp.pad(w_uq.reshape(MLA_Q_RANK, MLA_HEADS, qk), ((0, 0), (0, 0), (0, MLA_QK_PAD - qk)))
    w_uq_pad = w_uq_pad.reshape(MLA_Q_RANK, MLA_HEADS * MLA_QK_PAD).astype(BF16)
    w_ukv = w_ukv.astype(BF16)
    q = mla_queries(z, q_norm, w_uq_pad, cos, sin, t)
    k, v = mla_keys_values(z, kv_norm, w_ukv, cos, sin, t)
    ones = jnp.ones((l, LANES), F32)
    kc, vc = mla_keys_values(z_c, kv_norm, w_ukv, ones, jnp.zeros_like(ones), l)
    y_mla = mla_attention(q, k, v, kc, vc, batch, t, l)

    s_zero = jnp.zeros((batch, HG_HEADS, HG_DV, HG_DK), F32)
    _, s_f = hgrn2_pass(z_c, lb, s_zero, batch, l, HG_HEADS, False, emit_state=True)
    _, s_b = hgrn2_pass(z_c, lb, s_zero, batch, l, 2 * HG_HEADS, True, emit_state=True)
    o_f, = hgrn2_pass(z, lb, s_f, batch, t, HG_HEADS, False)
    y_hg, = hgrn2_pass(z, lb, s_b, batch, t, 2 * HG_HEADS, True, prev=o_f, norm_w=hg_norm_w)

    rows_per_seq = max(t // 256, 1)
    return out_projection(y_mla, y_hg, z, w_out.astype(BF16), x, mod, lambda i: i // rows_per_seq,
                          "odd_out_proj", final_w=final_w)


def kernel(x, c, ctx, c_ctx, ada_w, ada_b, norm_w, e_w_in, e_w_out, na_rpb, rw_mu, rw_w0, rw_w2, rw_a0, rw_a2, rw_k_k, rw_k_a, rw_r_k, rw_ln_w, rw_ln_b, o_w_in, o_w_out, mla_q_norm, mla_w_uq, mla_kv_norm, mla_w_ukv, hg_lower_bounds, hg_norm_w, final_norm_w):
    batch, t, d = x.shape
    l = ctx.shape[1]
    assert ada_w.shape[0] == 2, "one even and one odd layer"
    cond = jnp.concatenate([c, c_ctx[None, :], jnp.zeros((8 - batch - 1, d), F32)], axis=0)
    mod = modulation(cond, ada_w, ada_b)
    s = jax.nn.softmax(hg_lower_bounds.astype(F32), axis=0)
    lower = jnp.cumsum(s, axis=0) - s[0]

    xf, xcf = x.reshape(batch * t, d), ctx.reshape(batch * l, d)
    rw_params = (rw_mu[0], rw_w0[0], rw_w2[0], rw_a0[0], rw_a2[0], rw_k_k[0], rw_k_a[0], rw_r_k[0],
                 rw_ln_w[0], rw_ln_b[0])
    xf, xcf = _even_layer(xf, xcf, mod[0].reshape(8, 1, 3 * d), norm_w[0], e_w_in[0], e_w_out[0], na_rpb[0],
                          rw_params, batch, t, l)
    out = _odd_layer(xf, xcf, mod[1].reshape(8, 1, 3 * d), norm_w[1], o_w_in[0], o_w_out[0], mla_q_norm[0],
                     mla_w_uq[0], mla_kv_norm[0], mla_w_ukv[0], lower[1], hg_norm_w[0], final_norm_w,
                     batch, t, l)
    return out.reshape(batch, t, d)
```

```python
import functools
import math

import numpy as np
import jax
import jax.numpy as jnp
from jax import lax
from jax.experimental import pallas as pl
from jax.experimental.pallas import tpu as pltpu

F32 = jnp.float32
BF16 = jnp.bfloat16

GRID_W = 64
NORM_EPS = 1e-6
ROPE_BASE = 10000.0

NA_HEADS = 8
NA_HEAD_DIM = 128
NA_WIDTH = NA_HEADS * NA_HEAD_DIM
NA_WIN_ROWS = 8
NA_WIN_COLS = 16
NA_QROWS = 4
NA_KROWS = 12

RW_HEAD_DIM = 64
RW_HEADS = 16
RW_WIDTH = RW_HEADS * RW_HEAD_DIM
RW_RANK = 64
RW_SHIFT_COLS = 3 * RW_WIDTH + 4 * RW_RANK
RW_PAD_COLS = 3584
RW_GN_EPS = 64e-5
RW_SCAN_CHUNK = 32

MLA_HEADS = 8
MLA_Q_RANK = 512
MLA_KV_RANK = 512
MLA_NOPE = 128
MLA_ROPE = 64
MLA_V = 128
MLA_QK_PAD = 256
MLA_WIDTH = MLA_HEADS * MLA_V
MLA_A_COLS = 1536
MLA_KEY_CHUNK = 512

HG_HEADS = 8
HG_DK = 128
HG_DV = 128
HG_FDIM = HG_HEADS * HG_DK
HG_WIDTH = HG_HEADS * HG_DV
HG_CHUNK = 32
HG_BLOCK = 256
HG_STEP_HEADS = 4

LANES = 128

GATE_COLS = 2048
EV_Q_BLK = GATE_COLS // LANES
EV_RW_BLK = EV_Q_BLK + 3 * NA_WIDTH // LANES
OD_A_BLK = GATE_COLS // LANES
OD_HG_BLK = OD_A_BLK + MLA_A_COLS // LANES
VMEM_LIMIT = 48 * 1024 * 1024
NEG_INF = -1e30
LOG2E = math.log2(math.e)


def _cparams(n_axes):
    return pltpu.CompilerParams(dimension_semantics=("arbitrary",) * n_axes, vmem_limit_bytes=VMEM_LIMIT)


def _sigmoid(x):
    return 1.0 / (1.0 + jnp.exp(-x))


def _silu(x):
    return x * _sigmoid(x)


def _dot(a, b):
    return jnp.dot(a.astype(BF16), b.astype(BF16), preferred_element_type=F32)


def _dot_nt(a, b):
    return lax.dot_general(a.astype(BF16), b.astype(BF16), (((1,), (1,)), ((), ())), preferred_element_type=F32)


def _dot_tn(a, b):
    return lax.dot_general(a.astype(BF16), b.astype(BF16), (((0,), (0,)), ((), ())), preferred_element_type=F32)


def _mod_kernel(c_ref, w_ref, b_ref, o_ref):
    o_ref[...] = _dot(_silu(c_ref[...]), w_ref[...]) + b_ref[...]


def modulation(cond, ada_w, ada_b, tn=512):
    depth, d, n = ada_w.shape
    return pl.pallas_call(
        _mod_kernel,
        out_shape=jax.ShapeDtypeStruct((depth, 8, n), F32),
        grid=(depth, n // tn),
        in_specs=[pl.BlockSpec((8, d), lambda l, j: (0, 0)),
                  pl.BlockSpec((None, d, tn), lambda l, j: (l, 0, j)),
                  pl.BlockSpec((None, 1, tn), lambda l, j: (l, 0, j))],
        out_specs=pl.BlockSpec((None, 8, tn), lambda l, j: (l, 0, j)),
        compiler_params=_cparams(2),
        name="adaln_modulation",
    )(cond, ada_w, ada_b.reshape(depth, 1, n))


def _inproj_kernel(x_ref, nw_ref, shift_ref, scale_ref, w_ref, o_ref, h_ref):
    @pl.when(pl.program_id(1) == 0)
    def _():
        x = x_ref[...]
        y = x * lax.rsqrt(jnp.mean(x * x, axis=-1, keepdims=True) + NORM_EPS) * nw_ref[...]
        h_ref[...] = (y * (1.0 + scale_ref[...]) + shift_ref[...]).astype(BF16)

    o_ref[...] = jnp.dot(h_ref[...], w_ref[...], preferred_element_type=F32)


def in_projection(x, norm_w, mod, mod_row, w, name, tm=1024, tn=512):
    m, d = x.shape
    n = w.shape[1]
    tm = min(tm, m)
    return pl.pallas_call(
        _inproj_kernel,
        out_shape=jax.ShapeDtypeStruct((m, n), F32),
        grid=(m // tm, n // tn),
        in_specs=[pl.BlockSpec((tm, d), lambda i, j: (i, 0)),
                  pl.BlockSpec((1, d), lambda i, j: (0, 0)),
                  pl.BlockSpec((None, 1, d), lambda i, j: (mod_row(i), 0, 0)),
                  pl.BlockSpec((None, 1, d), lambda i, j: (mod_row(i), 0, 1)),
                  pl.BlockSpec((d, tn), lambda i, j: (0, j))],
        out_specs=pl.BlockSpec((tm, tn), lambda i, j: (i, j)),
        scratch_shapes=[pltpu.VMEM((tm, d), BF16)],
        compiler_params=_cparams(2),
        name=name,
    )(x, norm_w.reshape(1, d), mod, mod, w)


def _outproj_kernel(ya_ref, yb_ref, g_ref, w_ref, x_ref, gm_ref, *rest, ka, final):
    o_ref = rest[-1]
    sg = _silu(g_ref[...])
    acc = _dot(ya_ref[...].astype(F32) * sg[:, :ka], w_ref[:ka, :])
    acc += _dot(yb_ref[...].astype(F32) * sg[:, ka:], w_ref[ka:, :])
    out = x_ref[...] + gm_ref[...] * acc
    if final:
        out = out * lax.rsqrt(jnp.mean(out * out, axis=-1, keepdims=True) + NORM_EPS) * rest[0][...]
    o_ref[...] = out


def out_projection(ya, yb, gate, w, x, mod, mod_row, name, final_w=None, tm=256):
    m, d = x.shape
    ka, kb = ya.shape[1], yb.shape[1]
    tm = min(tm, m)
    ins = [ya, yb, gate, w, x, mod]
    specs = [pl.BlockSpec((tm, ka), lambda i: (i, 0)),
             pl.BlockSpec((tm, kb), lambda i: (i, 0)),
             pl.BlockSpec((tm, ka + kb), lambda i: (i, 0)),
             pl.BlockSpec((ka + kb, d), lambda i: (0, 0)),
             pl.BlockSpec((tm, d), lambda i: (i, 0)),
             pl.BlockSpec((None, 1, d), lambda i: (mod_row(i), 0, 2))]
    if final_w is not None:
        ins.append(final_w.reshape(1, d))
        specs.append(pl.BlockSpec((1, d), lambda i: (0, 0)))
    return pl.pallas_call(
        functools.partial(_outproj_kernel, ka=ka, final=final_w is not None),
        out_shape=jax.ShapeDtypeStruct((m, d), F32),
        grid=(m // tm,),
        in_specs=specs,
        out_specs=pl.BlockSpec((tm, d), lambda i: (i, 0)),
        compiler_params=_cparams(1),
        name=name,
    )(*ins)


def _softmax_pv(q, parts, scale):
    c = scale * LOG2E
    m = denom = num = None
    for k, bias, v in parts:
        x = _dot_nt(q, k) * c
        if bias is not None:
            x = x + bias
        m_part = jnp.max(x, axis=-1, keepdims=True)
        if m is None:
            m = m_part
            p = jnp.exp2(x - m)
            denom = jnp.sum(p, axis=-1, keepdims=True)
            num = _dot(p, v)
        else:
            m_new = jnp.maximum(m, m_part)
            alpha = jnp.exp2(m - m_new)
            p = jnp.exp2(x - m_new)
            denom = denom * alpha + jnp.sum(p, axis=-1, keepdims=True)
            num = num * alpha + _dot(p, v)
            m = m_new
    return num / denom


def _na_kernel(idx_ref, q_ref, k_ref, v_ref, kc_ref, vc_ref, tab_ref, o_ref, bias_ref, *, rows, scale):
    j = pl.program_id(1)
    npair = NA_KROWS // 2

    @pl.when(pl.program_id(2) == 0)
    def _():
        for qr in range(NA_QROWS):
            for m in range(npair):
                tile = tab_ref[idx_ref[(j * NA_QROWS + qr) * npair + m]]
                bias_ref[qr * GRID_W:(qr + 1) * GRID_W, m * LANES:(m + 1) * LANES] = tile

    start = pl.multiple_of(jnp.clip(NA_QROWS * j - NA_WIN_ROWS // 2, 0, rows - NA_KROWS) * GRID_W, GRID_W)
    half = NA_KROWS * GRID_W // 2
    parts = [(kc_ref[...], None, vc_ref[...])]
    for c in range(2):
        rows_c = pl.ds(start + c * half, half)
        parts.append((k_ref[rows_c, :], bias_ref[:, c * half:(c + 1) * half], v_ref[rows_c, :]))
    o_ref[...] = _softmax_pv(q_ref[...], parts, scale)


def _na_tile_ids(rows):
    kh = min(NA_WIN_ROWS, rows)
    n = 2 * NA_WIN_ROWS - 1
    ids = []
    for j in range(rows // NA_QROWS):
        ks = int(np.clip(NA_QROWS * j - NA_WIN_ROWS // 2, 0, rows - NA_KROWS))
        for qr in range(NA_QROWS):
            r = NA_QROWS * j + qr
            r0 = int(np.clip(r - kh // 2, 0, rows - kh))
            for m in range(NA_KROWS // 2):
                kr = ks + 2 * m
                dy = kr - r + NA_WIN_ROWS - 1
                first, second = r0 <= kr < r0 + kh, r0 <= kr + 1 < r0 + kh
                ids.append(dy if first and second else n + dy if first else 2 * n + dy + 1 if second else 3 * n)
    return np.array(ids, np.int32)


def _na_bias_tiles(rpb):
    col = np.arange(GRID_W)
    cs = np.clip(col - NA_WIN_COLS // 2, 0, GRID_W - NA_WIN_COLS)
    kc = col[None, :]
    in_win = (kc >= cs[:, None]) & (kc < cs[:, None] + NA_WIN_COLS)
    dx = np.where(in_win, kc - col[:, None] + NA_WIN_COLS - 1, 0)
    tab = jnp.where(in_win[None, None], rpb[:, :, dx] * LOG2E, NEG_INF)
    neg = jnp.full_like(tab[:, :1], NEG_INF)
    nxt = jnp.concatenate([tab[:, 1:], neg], axis=1)
    both = jnp.concatenate([tab, nxt], axis=-1)
    first = jnp.concatenate([tab, jnp.broadcast_to(neg, tab.shape)], axis=-1)
    second = jnp.concatenate([jnp.broadcast_to(neg, tab.shape), tab], axis=-1)
    return jnp.concatenate([both, first, second, jnp.concatenate([neg, neg], axis=-1)], axis=1)


def na_attention(z, z_c, rpb, batch, t, l):
    rows = t // GRID_W
    nj = rows // NA_QROWS
    tq = NA_QROWS * GRID_W
    ids = jnp.asarray(_na_tile_ids(rows))
    tiles = _na_bias_tiles(rpb)
    h8 = NA_HEADS
    return pl.pallas_call(
        functools.partial(_na_kernel, rows=rows, scale=NA_HEAD_DIM ** -0.5),
        out_shape=jax.ShapeDtypeStruct((batch * t, NA_WIDTH), F32),
        grid_spec=pltpu.PrefetchScalarGridSpec(
            num_scalar_prefetch=1,
            grid=(h8, nj, batch),
            in_specs=[pl.BlockSpec((tq, LANES), lambda h, j, b, g: (b * nj + j, EV_Q_BLK + h)),
                      pl.BlockSpec((t, LANES), lambda h, j, b, g: (b, EV_Q_BLK + h8 + h)),
                      pl.BlockSpec((t, LANES), lambda h, j, b, g: (b, EV_Q_BLK + 2 * h8 + h)),
                      pl.BlockSpec((l, LANES), lambda h, j, b, g: (b, EV_Q_BLK + h8 + h)),
                      pl.BlockSpec((l, LANES), lambda h, j, b, g: (b, EV_Q_BLK + 2 * h8 + h)),
                      pl.BlockSpec((None,) + tiles.shape[1:], lambda h, j, b, g: (h, 0, 0, 0))],
            out_specs=pl.BlockSpec((tq, LANES), lambda h, j, b, g: (b * nj + j, h)),
            scratch_shapes=[pltpu.VMEM((tq, NA_KROWS * GRID_W), F32)]),
        compiler_params=_cparams(3),
        name="na_attention",
    )(ids, z, z, z, z_c, z_c, tiles)


def _dense_attn_kernel(q_ref, k_ref, v_ref, o_ref, *, scale):
    o_ref[...] = _softmax_pv(q_ref[...], [(k_ref[...], None, v_ref[...])], scale)


def ctx_attention(z_c, batch, l):
    h8 = NA_HEADS
    return pl.pallas_call(
        functools.partial(_dense_attn_kernel, scale=NA_HEAD_DIM ** -0.5),
        out_shape=jax.ShapeDtypeStruct((batch * l, NA_WIDTH), F32),
        grid=(batch, h8),
        in_specs=[pl.BlockSpec((l, LANES), lambda b, h: (b, EV_Q_BLK + h)),
                  pl.BlockSpec((l, LANES), lambda b, h: (b, EV_Q_BLK + h8 + h)),
                  pl.BlockSpec((l, LANES), lambda b, h: (b, EV_Q_BLK + 2 * h8 + h))],
        out_specs=pl.BlockSpec((l, LANES), lambda b, h: (b, h)),
        compiler_params=_cparams(2),
        name="ctx_attention",
    )(z_c, z_c, z_c)


def _head_sum(x):
    first = lax.broadcasted_iota(jnp.int32, x.shape, 1) < RW_HEAD_DIM
    s0 = jnp.sum(jnp.where(first, x, 0.0), axis=-1, keepdims=True)
    s1 = jnp.sum(jnp.where(first, 0.0, x), axis=-1, keepdims=True)
    return jnp.where(first, s0, s1)


def _softplus(x):
    return jnp.maximum(x, 0.0) + jnp.log1p(jnp.exp(-jnp.abs(x)))


def _rw_terms_kernel(ur_ref, uk_ref, uv_ref, uwa_ref, pr_ref, pk_ref, pv_ref, pwa_ref, nr_ref, nk_ref, nv_ref,
                     nwa_ref, mu_ref, w0_ref, w2_ref, a0_ref, a2_ref, kk_ref, ka_ref, rk_ref,
                     r_o, v_o, kkn_o, dec0_o, dec1_o, kd0_o, kd1_o, b0_o, b1_o, bonus_o, *, nt):
    i = pl.program_id(1)
    tm = ur_ref.shape[0]

    def shifted(u_ref, p_ref, n_ref, col0):
        u = u_ref[...]
        mu = mu_ref[:, col0:col0 + u.shape[1]]
        row = lax.broadcasted_iota(jnp.int32, u.shape, 0)
        before = jnp.where(i == 0, 0.0, p_ref[7:8, :])
        after = jnp.where(i == nt - 1, 0.0, n_ref[0:1, :])
        prev = jnp.where(row == 0, before, pltpu.roll(u, 1, axis=0))
        nxt = jnp.where(row == tm - 1, after, pltpu.roll(u, tm - 1, axis=0))
        return u + mu[0:1] * (prev - u) + mu[1:2] * (nxt - u)

    r_all = shifted(ur_ref, pr_ref, nr_ref, 0)
    k_all = shifted(uk_ref, pk_ref, nk_ref, RW_WIDTH)
    v_all = shifted(uv_ref, pv_ref, nv_ref, 2 * RW_WIDTH)
    wa = shifted(uwa_ref, pwa_ref, nwa_ref, 3 * RW_WIDTH)
    wd = jnp.tanh(wa[:, :LANES]).astype(BF16)
    ad = wa[:, LANES:].astype(BF16)
    for p in range(RW_WIDTH // LANES):
        sl = slice(p * LANES, (p + 1) * LANES)
        r, k, v = r_all[:, sl], k_all[:, sl], v_all[:, sl]
        kk = k * kk_ref[:, sl]
        kk = kk / jnp.maximum(jnp.sqrt(_head_sum(kk * kk)), 1e-12)
        r_o[:, p, :] = r
        v_o[:, p, :] = v
        kkn_o[:, p, :] = kk
        kd_sum = jnp.zeros_like(k)
        for d, (dec_o, kd_o, b_o) in enumerate(((dec0_o, kd0_o, b0_o), (dec1_o, kd1_o, b1_o))):
            w = w0_ref[d:d + 1, sl] + jnp.dot(wd, w2_ref[d, :, sl].astype(BF16), preferred_element_type=F32)
            dec_o[:, p, :] = jnp.exp(-jnp.exp(-_softplus(-w) - 0.5))
            a = _sigmoid(a0_ref[d:d + 1, sl] + jnp.dot(ad, a2_ref[d, :, sl].astype(BF16), preferred_element_type=F32))
            kd = k * (1.0 + (a - 1.0) * ka_ref[:, sl])
            kd_o[:, p, :] = kd
            b_o[:, p, :] = kk * a
            kd_sum = kd_sum + kd
        bonus_o[:, p, :] = _head_sum(r * kd_sum * rk_ref[:, sl]) * v


def rwkv_terms(z, n_seq, t, mu, w0, w2p, a0, a2p, k_k, k_a, r_k, tm=256):
    tm = min(tm, t)
    nt = t // tm
    npair = RW_WIDTH // LANES
    wide0 = EV_RW_BLK * LANES // RW_WIDTH
    wa_blk = (EV_RW_BLK + 3 * npair) // 2
    n8 = tm // 8

    def cur(width, cb):
        return pl.BlockSpec((tm, width), lambda b, i: (b * nt + i, cb))

    def before(width, cb):
        return pl.BlockSpec((8, width), lambda b, i: (jnp.maximum((b * nt + i) * n8 - 1, 0), cb))

    def after(width, cb):
        return pl.BlockSpec((8, width), lambda b, i: (jnp.minimum((b * nt + i + 1) * n8, n_seq * nt * n8 - 1), cb))

    cols = [(RW_WIDTH, wide0), (RW_WIDTH, wide0 + 1), (RW_WIDTH, wide0 + 2), (2 * LANES, wa_blk)]
    whole = lambda a: pl.BlockSpec(a.shape, lambda b, i: (0,) * a.ndim)
    params = (mu, w0, w2p, a0, a2p, k_k, k_a, r_k)
    out = pl.BlockSpec((tm, npair, LANES), lambda b, i: (i, b, 0))
    return pl.pallas_call(
        functools.partial(_rw_terms_kernel, nt=nt),
        out_shape=[jax.ShapeDtypeStruct((t, n_seq * npair, LANES), F32)] * 10,
        grid=(n_seq, nt),
        in_specs=([cur(*c) for c in cols] + [before(*c) for c in cols] + [after(*c) for c in cols]
                  + [whole(a) for a in params]),
        out_specs=[out] * 10,
        compiler_params=_cparams(2),
        name="rwkv_terms",
    )(*([z] * 12), *params)


def _to_lanes(n):
    q = jnp.concatenate([n, pltpu.roll(n, RW_HEAD_DIM, axis=1)], axis=0)
    return q.T[:RW_HEAD_DIM]


def _rw_scan_kernel(rf, rb, vf, vb, af, ab, w0, w1, k0, k1, b0, b1, s0_ref, yf_ref, yb_ref, sfin_ref,
                    s_ref, z_ref, znext_ref, g_ref, ybuf_ref, *, chunk):
    n = RW_HEAD_DIM
    srcs = ((rf, rb), (w0, w1), (k0, k1), (vf, vb), (af, ab), (b0, b1))
    R, K, V, A, B = range(5)

    @pl.when(pl.program_id(0) == 0)
    def _():
        s_ref[...] = s0_ref[...]

    def convert(t_src):
        r, w, k, v, kk, b = (_to_lanes(jnp.concatenate([f[t_src], bw[chunk - 1 - t_src]], axis=0))
                             for f, bw in srcs)
        g_prev = g_ref[0]
        g = g_prev * w
        g_ref[0] = g
        inv = 1.0 / g
        znext_ref[R] = r * g
        znext_ref[K] = k * inv
        znext_ref[V] = v
        znext_ref[A] = -(kk * g_prev)
        znext_ref[B] = b * inv

    def emit_y(t):
        rows = _to_lanes(ybuf_ref[...])
        yf_ref[t] = rows[:n // 2]
        yb_ref[chunk - 1 - t] = rows[n // 2:]

    g_ref[0] = jnp.ones((n, LANES), F32)
    ybuf_ref[...] = jnp.zeros((n, LANES), F32)
    convert(0)

    def step(t, carry):
        emit_y(jnp.maximum(t - 1, 0))
        z_ref[...] = znext_ref[...]
        g_ref[1] = g_ref[0]
        convert(jnp.minimum(t + 1, chunk - 1))
        sa = [jnp.zeros((n, LANES), F32), jnp.zeros((n, LANES), F32)]
        for k in range(n):
            sa[k % 2] = sa[k % 2] + s_ref[k] * z_ref[A, k:k + 1, :]
        sa = sa[0] + sa[1]
        vt = z_ref[V]
        y = [jnp.zeros((n, LANES), F32), jnp.zeros((n, LANES), F32)]
        for k in range(n):
            sk = s_ref[k] + sa * z_ref[B, k:k + 1, :] + vt * z_ref[K, k:k + 1, :]
            s_ref[k] = sk
            y[k % 2] = y[k % 2] + sk * z_ref[R, k:k + 1, :]
        ybuf_ref[...] = y[0] + y[1]
        return carry

    lax.fori_loop(0, chunk, step, 0)
    emit_y(chunk - 1)
    for k in range(n):
        s_ref[k] = s_ref[k] * g_ref[1, k:k + 1, :]
    sfin_ref[...] = s_ref[...]


def rwkv_scan(r, v, kkn, dec0, dec1, kd0, kd1, b0, b1, s0):
    t, nq, _ = r.shape
    n = RW_HEAD_DIM
    chunk = RW_SCAN_CHUNK
    nc = t // chunk
    fwd = pl.BlockSpec((chunk, nq, LANES), lambda i: (i, 0, 0))
    bwd = pl.BlockSpec((chunk, nq, LANES), lambda i: (nc - 1 - i, 0, 0))
    state = pl.BlockSpec((n, n, LANES), lambda i: (0, 0, 0))
    slab = jax.ShapeDtypeStruct((t, nq, LANES), F32)
    return pl.pallas_call(
        functools.partial(_rw_scan_kernel, chunk=chunk),
        out_shape=[slab, slab, jax.ShapeDtypeStruct((n, n, LANES), F32)],
        grid=(nc,),
        in_specs=[fwd, bwd, fwd, bwd, fwd, bwd, fwd, bwd, fwd, bwd, fwd, bwd, state],
        out_specs=[fwd, bwd, state],
        scratch_shapes=[pltpu.VMEM((n, n, LANES), F32), pltpu.VMEM((5, n, LANES), F32),
                        pltpu.VMEM((5, n, LANES), F32), pltpu.VMEM((2, n, LANES), F32),
                        pltpu.VMEM((n, LANES), F32)],
        compiler_params=_cparams(1),
        name="rwkv_scan",
    )(r, r, v, v, kkn, kkn, dec0, dec1, kd0, kd1, b0, b1, s0)


def _rw_readout_kernel(yf_ref, yb_ref, bonus_ref, lw_ref, lb_ref, o_ref):
    npair = RW_WIDTH // LANES
    y_all = pltpu.einshape("tqc->qtc", yf_ref[...] + yb_ref[...])
    bonus_all = pltpu.einshape("tqc->qtc", bonus_ref[...])
    for q in range(yf_ref.shape[1]):
        b, p = divmod(q, npair)
        sl = slice(p * LANES, (p + 1) * LANES)
        y = y_all[q]
        mu = _head_sum(y) * (1.0 / RW_HEAD_DIM)
        d = y - mu
        var = _head_sum(d * d) * (1.0 / RW_HEAD_DIM)
        o_ref[b, :, sl] = d * lax.rsqrt(var + RW_GN_EPS) * lw_ref[:, sl] + lb_ref[:, sl] + bonus_all[q]


def rwkv_readout(y_f, y_b, bonus, ln_w, ln_b, tm=256):
    t, nq, _ = y_f.shape
    tm = min(tm, t)
    batch = nq // (RW_WIDTH // LANES)
    slab = pl.BlockSpec((tm, nq, LANES), lambda i: (i, 0, 0))
    vec = pl.BlockSpec((1, RW_WIDTH), lambda i: (0, 0))
    return pl.pallas_call(
        _rw_readout_kernel,
        out_shape=jax.ShapeDtypeStruct((batch, t, RW_WIDTH), F32),
        grid=(t // tm,),
        in_specs=[slab, slab, slab, vec, vec],
        out_specs=pl.BlockSpec((batch, tm, RW_WIDTH), lambda i: (0, i, 0)),
        compiler_params=_cparams(1),
        name="rwkv_readout",
    )(y_f, y_b, bonus, ln_w.reshape(1, -1), ln_b.reshape(1, -1))


def _rms(x, w):
    return x * lax.rsqrt(jnp.mean(x * x, axis=-1, keepdims=True) + NORM_EPS) * w


def _rope(x, cos, sin):
    odd = (lax.broadcasted_iota(jnp.int32, x.shape, 1) & 1) == 1
    swapped = jnp.where(odd, pltpu.roll(x, 1, axis=1), pltpu.roll(x, LANES - 1, axis=1))
    return x * cos + swapped * sin


def _mla_q_kernel(c_ref, nw_ref, w_ref, cos_ref, sin_ref, q_ref, h_ref):
    @pl.when(pl.program_id(1) == 0)
    def _():
        h_ref[...] = _rms(c_ref[...], nw_ref[...]).astype(BF16)

    acc = jnp.dot(h_ref[...], w_ref[...], preferred_element_type=F32)
    q_ref[:, :MLA_NOPE] = acc[:, :MLA_NOPE].astype(BF16)
    q_ref[:, MLA_NOPE:] = _rope(acc[:, MLA_NOPE:], cos_ref[...], sin_ref[...]).astype(BF16)


def mla_queries(z, q_norm, w_uq_pad, cos, sin, t, tm=512):
    m = z.shape[0]
    tm = min(tm, t)
    nt = t // tm
    return pl.pallas_call(
        _mla_q_kernel,
        out_shape=jax.ShapeDtypeStruct((m, MLA_HEADS * MLA_QK_PAD), BF16),
        grid=(m // tm, MLA_HEADS),
        in_specs=[pl.BlockSpec((tm, MLA_Q_RANK), lambda i, h: (i, GATE_COLS // MLA_Q_RANK)),
                  pl.BlockSpec((1, MLA_Q_RANK), lambda i, h: (0, 0)),
                  pl.BlockSpec((MLA_Q_RANK, MLA_QK_PAD), lambda i, h: (0, h)),
                  pl.BlockSpec((tm, LANES), lambda i, h: (i % nt, 0)),
                  pl.BlockSpec((tm, LANES), lambda i, h: (i % nt, 0))],
        out_specs=pl.BlockSpec((tm, MLA_QK_PAD), lambda i, h: (i, h)),
        scratch_shapes=[pltpu.VMEM((tm, MLA_Q_RANK), BF16)],
        compiler_params=_cparams(2),
        name="mla_queries",
    )(z, q_norm.reshape(1, -1), w_uq_pad, cos, sin)


def _mla_kv_kernel(c_ref, pe_ref, nw_ref, w_ref, cos_ref, sin_ref, k_ref, v_ref, h_ref):
    @pl.when(pl.program_id(1) == 0)
    def _():
        h_ref[...] = _rms(c_ref[...], nw_ref[...]).astype(BF16)

    acc = jnp.dot(h_ref[...], w_ref[...], preferred_element_type=F32)
    k_ref[:, :MLA_NOPE] = acc[:, :MLA_NOPE].astype(BF16)
    k_ref[:, MLA_NOPE:] = _rope(pe_ref[...], cos_ref[...], sin_ref[...]).astype(BF16)
    v_ref[...] = acc[:, MLA_NOPE:].astype(BF16)


def mla_keys_values(z, kv_norm, w_ukv, cos, sin, t, tm=512):
    m = z.shape[0]
    tm = min(tm, t)
    nt = t // tm
    return pl.pallas_call(
        _mla_kv_kernel,
        out_shape=[jax.ShapeDtypeStruct((m, MLA_HEADS * MLA_QK_PAD), BF16),
                   jax.ShapeDtypeStruct((m, MLA_WIDTH), BF16)],
        grid=(m // tm, MLA_HEADS),
        in_specs=[pl.BlockSpec((tm, MLA_KV_RANK), lambda i, h: (i, (GATE_COLS + MLA_Q_RANK) // MLA_KV_RANK)),
                  pl.BlockSpec((tm, LANES), lambda i, h: (i, OD_A_BLK + (MLA_Q_RANK + MLA_KV_RANK) // LANES)),
                  pl.BlockSpec((1, MLA_KV_RANK), lambda i, h: (0, 0)),
                  pl.BlockSpec((MLA_KV_RANK, MLA_NOPE + MLA_V), lambda i, h: (0, h)),
                  pl.BlockSpec((tm, LANES), lambda i, h: (i % nt, 0)),
                  pl.BlockSpec((tm, LANES), lambda i, h: (i % nt, 0))],
        out_specs=[pl.BlockSpec((tm, MLA_QK_PAD), lambda i, h: (i, h)),
                   pl.BlockSpec((tm, MLA_V), lambda i, h: (i, h))],
        scratch_shapes=[pltpu.VMEM((tm, MLA_KV_RANK), BF16)],
        compiler_params=_cparams(2),
        name="mla_keys_values",
    )(z, z, kv_norm.reshape(1, -1), w_ukv, cos, sin)


def _mla_attn_kernel(q_ref, k_ref, v_ref, kc_ref, vc_ref, o_ref, *, scale):
    parts = [(kc_ref[...], None, vc_ref[...])]
    nkeys = k_ref.shape[0]
    step = min(MLA_KEY_CHUNK, nkeys)
    for lo in range(0, nkeys, step):
        parts.append((k_ref[lo:lo + step, :], None, v_ref[lo:lo + step, :]))
    o_ref[...] = _softmax_pv(q_ref[...], parts, scale)


def mla_attention(q, k, v, kc, vc, batch, t, l, tq=512):
    tq = min(tq, t)
    nq = t // tq
    return pl.pallas_call(
        functools.partial(_mla_attn_kernel, scale=(MLA_NOPE + MLA_ROPE) ** -0.5),
        out_shape=jax.ShapeDtypeStruct((batch * t, MLA_WIDTH), F32),
        grid=(batch, MLA_HEADS, nq),
        in_specs=[pl.BlockSpec((tq, MLA_QK_PAD), lambda b, h, i: (b * nq + i, h)),
                  pl.BlockSpec((t, MLA_QK_PAD), lambda b, h, i: (b, h)),
                  pl.BlockSpec((t, MLA_V), lambda b, h, i: (b, h)),
                  pl.BlockSpec((l, MLA_QK_PAD), lambda b, h, i: (b, h)),
                  pl.BlockSpec((l, MLA_V), lambda b, h, i: (b, h))],
        out_specs=pl.BlockSpec((tq, MLA_V), lambda b, h, i: (b * nq + i, h)),
        compiler_params=_cparams(3),
        name="mla_attention",
    )(q, k, v, kc, vc)


def _rope_tables(t):
    tok = np.arange(t)
    pos = np.stack([tok // GRID_W, tok % GRID_W], axis=-1).astype(np.float32)
    n_freq = MLA_ROPE // 4
    inv = (ROPE_BASE ** (-jnp.arange(n_freq, dtype=F32) / n_freq))
    ang = (jnp.asarray(pos)[:, :, None] * inv).reshape(t, MLA_ROPE // 2)
    cos = jnp.repeat(jnp.cos(ang), 2, axis=-1)
    sin = jnp.repeat(jnp.sin(ang), 2, axis=-1) * jnp.tile(jnp.array([-1.0, 1.0], F32), MLA_ROPE // 2)
    pad = ((0, 0), (0, LANES - MLA_ROPE))
    return jnp.pad(cos, pad), jnp.pad(sin, pad)


def _split3(x):
    hi = x.astype(BF16)
    r1 = x - hi.astype(F32)
    mid = r1.astype(BF16)
    lo = (r1 - mid.astype(F32)).astype(BF16)
    return hi, mid, lo


def _hg_kernel(q_ref, f_ref, v_ref, lb_ref, s0_ref, *rest, reverse, finalize, emit_state):
    st_ref = rest[-1]
    rest = rest[:-1]
    if finalize:
        prev_ref, nw_ref = rest[0], rest[1]
        rest = rest[2:]
    o_ref = rest[0]

    @pl.when(pl.program_id(2) == 0)
    def _():
        st_ref[...] = s0_ref[...]

    bt = q_ref.shape[0]
    nchunk = bt // HG_CHUNK
    order = range(nchunk - 1, -1, -1) if reverse else range(nchunk)
    ri = lax.broadcasted_iota(jnp.int32, (bt, bt), 0)
    ci = lax.broadcasted_iota(jnp.int32, (bt, bt), 1)
    same = (ri // HG_CHUNK) == (ci // HG_CHUNK)
    tri = same & ((ci >= ri) if reverse else (ci <= ri))
    tri_f = tri.astype(F32)
    tri_b = tri.astype(BF16)

    lb = lb_ref[...]
    fz = f_ref[...]
    log_sig = jnp.minimum(fz, 0.0) - jnp.log1p(jnp.exp(-jnp.abs(fz)))
    la = jnp.log(lb)
    lbb = jnp.log1p(-lb) + log_sig
    log_f = jnp.maximum(la, lbb) + jnp.log1p(jnp.exp(-jnp.abs(la - lbb)))
    kf_all = (1.0 - lb) * _sigmoid(-fz)
    pieces = jnp.dot(tri_b, jnp.concatenate(_split3(log_f), axis=1), preferred_element_type=F32)
    width = log_f.shape[1]
    cum_all = pieces[:, :width] + pieces[:, width:2 * width] + pieces[:, 2 * width:]

    for hh in range(HG_STEP_HEADS):
        sl = slice(hh * HG_DK, (hh + 1) * HG_DK)
        cum, kf, v = cum_all[:, sl], kf_all[:, sl], v_ref[:, sl]
        last = [cum[c * HG_CHUNK:c * HG_CHUNK + 1] if reverse else cum[(c + 1) * HG_CHUNK - 1:(c + 1) * HG_CHUNK]
                for c in range(nchunk)]
        tot = jnp.concatenate([jnp.broadcast_to(x, (HG_CHUNK, HG_DK)) for x in last], axis=0)
        q_in = q_ref[:, sl] * jnp.exp(cum)
        k_in = kf * jnp.exp(-cum)
        k_end = kf * jnp.exp(tot - cum)
        o_intra = _dot(_dot_nt(q_in, k_in) * tri_f, v)
        st = st_ref[hh]
        for c in order:
            rows = slice(c * HG_CHUNK, (c + 1) * HG_CHUNK)
            o_c = o_intra[rows] + _dot_nt(q_in[rows], st)
            if finalize:
                o_c = o_c + prev_ref[rows, sl]
                o_c = o_c * lax.rsqrt(jnp.mean(o_c * o_c, axis=-1, keepdims=True) + NORM_EPS) * nw_ref[...]
            o_ref[rows, sl] = o_c
            st = st * jnp.exp(last[c]) + _dot_tn(v[rows], k_end[rows])
        st_ref[hh] = st
        if emit_state:
            rest[1][hh] = st


def hgrn2_pass(z, lb, s0, batch, t, f_off, reverse, prev=None, norm_w=None, emit_state=False):
    bt = min(HG_BLOCK, t)
    nblk = t // bt
    nh = HG_STEP_HEADS
    wide = nh * LANES
    finalize = prev is not None
    pos = (lambda i: nblk - 1 - i) if reverse else (lambda i: i)
    blk = lambda off: pl.BlockSpec((bt, wide), lambda b, h, i, off=off: (b * nblk + pos(i), off // nh + h))
    state = pl.BlockSpec((None, nh, HG_DV, HG_DK), lambda b, h, i: (b, h, 0, 0))
    ins = [z, z, z, lb.reshape(1, -1), s0]
    specs = [blk(OD_HG_BLK), blk(OD_HG_BLK + f_off), blk(OD_HG_BLK + 3 * HG_HEADS),
             pl.BlockSpec((1, wide), lambda b, h, i: (0, h)), state]
    if finalize:
        ins += [prev, norm_w.reshape(1, -1)]
        specs += [blk(0), pl.BlockSpec((1, HG_DV), lambda b, h, i: (0, 0))]
    out_shape = [jax.ShapeDtypeStruct((batch * t, HG_WIDTH), F32)]
    out_specs = [blk(0)]
    if emit_state:
        out_shape.append(jax.ShapeDtypeStruct((batch, HG_HEADS, HG_DV, HG_DK), F32))
        out_specs.append(state)
    return pl.pallas_call(
        functools.partial(_hg_kernel, reverse=reverse, finalize=finalize, emit_state=emit_state),
        out_shape=out_shape,
        grid=(batch, HG_HEADS // nh, nblk),
        in_specs=specs,
        out_specs=out_specs,
        scratch_shapes=[pltpu.VMEM((nh, HG_DV, HG_DK), F32)],
        compiler_params=_cparams(3),
        name="hgrn2_" + ("bwd" if reverse else "fwd"),
    )(*ins)


def _even_layer(x, xc, mod, norm_w, w_in, w_out, rpb, rw_params, batch, t, l):
    d = x.shape[1]
    s2 = 3 * NA_WIDTH + RW_SHIFT_COLS
    zero = jnp.zeros((d, RW_PAD_COLS - RW_SHIFT_COLS), BF16)
    w_in = w_in.astype(BF16)
    w_ext = jnp.concatenate([w_in[:, s2:], w_in[:, :s2], zero], axis=1)
    tiles_per_seq = max(t // 1024, 1)
    z = in_projection(x, norm_w, mod, lambda i: i // tiles_per_seq, w_ext, "even_in_proj")
    z_c = in_projection(xc, norm_w, mod, lambda i: batch, w_ext, "even_in_proj_ctx")

    y_na = na_attention(z, z_c, rpb, batch, t, l)
    yc_na = ctx_attention(z_c, batch, l)

    mu, w0, w2, a0, a2, k_k, k_a, r_k, ln_w, ln_b = rw_params
    mu = jnp.pad(mu, ((0, 0), (0, RW_PAD_COLS - RW_SHIFT_COLS)))
    zpad = jnp.zeros((RW_RANK, RW_WIDTH), F32)
    w2p = jnp.stack([jnp.concatenate([w2[0], zpad]), jnp.concatenate([zpad, w2[1]])])
    a2p = jnp.stack([jnp.concatenate([a2[0], zpad]), jnp.concatenate([zpad, a2[1]])])
    vecs = (mu, w0, w2p, a0, a2p, k_k.reshape(1, -1), k_a.reshape(1, -1), r_k.reshape(1, -1))
    terms = rwkv_terms(z, batch, t, *vecs)
    terms_c = rwkv_terms(z_c, batch, l, *vecs)
    s_zero = jnp.zeros((RW_HEAD_DIM, RW_HEAD_DIM, LANES), F32)
    yc_f, yc_b, s_ctx = rwkv_scan(*terms_c[:9], s_zero)
    y_f, y_b, _ = rwkv_scan(*terms[:9], s_ctx)
    y_rw = rwkv_readout(y_f, y_b, terms[9], ln_w, ln_b).reshape(batch * t, RW_WIDTH)
    yc_rw = rwkv_readout(yc_f, yc_b, terms_c[9], ln_w, ln_b).reshape(batch * l, RW_WIDTH)

    w_out = w_out.astype(BF16)
    rows_per_seq = max(t // 256, 1)
    x = out_projection(y_na, y_rw, z, w_out, x, mod, lambda i: i // rows_per_seq, "even_out_proj")
    xc = out_projection(yc_na, yc_rw, z_c, w_out, xc, mod, lambda i: batch, "even_out_proj_ctx")
    return x, xc


def _odd_layer(x, xc, mod, norm_w, w_in, w_out, q_norm, w_uq, kv_norm, w_ukv, lb, hg_norm_w, final_w,
               batch, t, l):
    d = x.shape[1]
    o3 = MLA_Q_RANK + MLA_KV_RANK + MLA_ROPE
    o4 = o3 + 4 * HG_FDIM
    zero = jnp.zeros((d, MLA_A_COLS - o3), BF16)
    w_in = w_in.astype(BF16)
    w_ext = jnp.concatenate([w_in[:, o4:], w_in[:, :o3], zero, w_in[:, o3:o4]], axis=1)
    tiles_per_seq = max(t // 1024, 1)
    z = in_projection(x, norm_w, mod, lambda i: i // tiles_per_seq, w_ext, "odd_in_proj")
    z_c = in_projection(xc, norm_w, mod, lambda i: batch, w_ext, "odd_in_proj_ctx")

    cos, sin = _rope_tables(t)
    qk = MLA_NOPE + MLA_ROPE
    w_uq_pad = jnp.pad(w_uq.reshape(MLA_Q_RANK, MLA_HEADS, qk), ((0, 0), (0, 0), (0, MLA_QK_PAD - qk)))
    w_uq_pad = w_uq_pad.reshape(MLA_Q_RANK, MLA_HEADS * MLA_QK_PAD).astype(BF16)
    w_ukv = w_ukv.astype(BF16)
    q = mla_queries(z, q_norm, w_uq_pad, cos, sin, t)
    k, v = mla_keys_values(z, kv_norm, w_ukv, cos, sin, t)
    ones = jnp.ones((l, LANES), F32)
    kc, vc = mla_keys_values(z_c, kv_norm, w_ukv, ones, jnp.zeros_like(ones), l)
    y_mla = mla_attention(q, k, v, kc, vc, batch, t, l)

    s_zero = jnp.zeros((batch, HG_HEADS, HG_DV, HG_DK), F32)
    _, s_f = hgrn2_pass(z_c, lb, s_zero, batch, l, HG_HEADS, False, emit_state=True)
    _, s_b = hgrn2_pass(z_c, lb, s_zero, batch, l, 2 * HG_HEADS, True, emit_state=True)
    o_f, = hgrn2_pass(z, lb, s_f, batch, t, HG_HEADS, False)
    y_hg, = hgrn2_pass(z, lb, s_b, batch, t, 2 * HG_HEADS, True, prev=o_f, norm_w=hg_norm_w)

    rows_per_seq = max(t // 256, 1)
    return out_projection(y_mla, y_hg, z, w_out.astype(BF16), x, mod, lambda i: i // rows_per_seq,
                          "odd_out_proj", final_w=final_w)


def kernel(x, c, ctx, c_ctx, ada_w, ada_b, norm_w, e_w_in, e_w_out, na_rpb, rw_mu, rw_w0, rw_w2, rw_a0, rw_a2, rw_k_k, rw_k_a, rw_r_k, rw_ln_w, rw_ln_b, o_w_in, o_w_out, mla_q_norm, mla_w_uq, mla_kv_norm, mla_w_ukv, hg_lower_bounds, hg_norm_w, final_norm_w):
    batch, t, d = x.shape
    l = ctx.shape[1]
    assert ada_w.shape[0] == 2, "one even and one odd layer"
    cond = jnp.concatenate([c, c_ctx[None, :], jnp.zeros((8 - batch - 1, d), F32)], axis=0)
    mod = modulation(cond, ada_w, ada_b)
    s = jax.nn.softmax(hg_lower_bounds.astype(F32), axis=0)
    lower = jnp.cumsum(s, axis=0) - s[0]

    xf, xcf = x.reshape(batch * t, d), ctx.reshape(batch * l, d)
    rw_params = (rw_mu[0], rw_w0[0], rw_w2[0], rw_a0[0], rw_a2[0], rw_k_k[0], rw_k_a[0], rw_r_k[0],
                 rw_ln_w[0], rw_ln_b[0])
    xf, xcf = _even_layer(xf, xcf, mod[0].reshape(8, 1, 3 * d), norm_w[0], e_w_in[0], e_w_out[0], na_rpb[0],
                          rw_params, batch, t, l)
    out = _odd_layer(xf, xcf, mod[1].reshape(8, 1, 3 * d), norm_w[1], o_w_in[0], o_w_out[0], mla_q_norm[0],
                     mla_w_uq[0], mla_kv_norm[0], mla_w_ukv[0], lower[1], hg_norm_w[0], final_norm_w,
                     batch, t, l)
    return out.reshape(batch, t, d)
```

```python
import functools
import math

import numpy as np
import jax
import jax.numpy as jnp
from jax import lax
from jax.experimental import pallas as pl
from jax.experimental.pallas import tpu as pltpu

F32 = jnp.float32
BF16 = jnp.bfloat16

GRID_W = 64
NORM_EPS = 1e-6
ROPE_BASE = 10000.0

NA_HEADS = 8
NA_HEAD_DIM = 128
NA_WIDTH = NA_HEADS * NA_HEAD_DIM
NA_WIN_ROWS = 8
NA_WIN_COLS = 16
NA_QROWS = 4
NA_KROWS = 12

RW_HEAD_DIM = 64
RW_HEADS = 16
RW_WIDTH = RW_HEADS * RW_HEAD_DIM
RW_RANK = 64
RW_SHIFT_COLS = 3 * RW_WIDTH + 4 * RW_RANK
RW_PAD_COLS = 3584
RW_GN_EPS = 64e-5
RW_SCAN_CHUNK = 32

MLA_HEADS = 8
MLA_Q_RANK = 512
MLA_KV_RANK = 512
MLA_NOPE = 128
MLA_ROPE = 64
MLA_V = 128
MLA_QK_PAD = 256
MLA_WIDTH = MLA_HEADS * MLA_V
MLA_A_COLS = 1536
MLA_KEY_CHUNK = 512

HG_HEADS = 8
HG_DK = 128
HG_DV = 128
HG_FDIM = HG_HEADS * HG_DK
HG_WIDTH = HG_HEADS * HG_DV
HG_CHUNK = 32
HG_BLOCK = 256
HG_STEP_HEADS = 4

LANES = 128

GATE_COLS = 2048
EV_Q_BLK = GATE_COLS // LANES
EV_RW_BLK = EV_Q_BLK + 3 * NA_WIDTH // LANES
OD_A_BLK = GATE_COLS // LANES
OD_HG_BLK = OD_A_BLK + MLA_A_COLS // LANES
VMEM_LIMIT = 48 * 1024 * 1024
NEG_INF = -1e30
LOG2E = math.log2(math.e)


def _cparams(n_axes):
    return pltpu.CompilerParams(dimension_semantics=("arbitrary",) * n_axes, vmem_limit_bytes=VMEM_LIMIT)


def _sigmoid(x):
    return 1.0 / (1.0 + jnp.exp(-x))


def _silu(x):
    return x * _sigmoid(x)


def _dot(a, b):
    return jnp.dot(a.astype(BF16), b.astype(BF16), preferred_element_type=F32)


def _dot_nt(a, b):
    return lax.dot_general(a.astype(BF16), b.astype(BF16), (((1,), (1,)), ((), ())), preferred_element_type=F32)


def _dot_tn(a, b):
    return lax.dot_general(a.astype(BF16), b.astype(BF16), (((0,), (0,)), ((), ())), preferred_element_type=F32)


def _mod_kernel(c_ref, w_ref, b_ref, o_ref):
    o_ref[...] = _dot(_silu(c_ref[...]), w_ref[...]) + b_ref[...]


def modulation(cond, ada_w, ada_b, tn=512):
    depth, d, n = ada_w.shape
    return pl.pallas_call(
        _mod_kernel,
        out_shape=jax.ShapeDtypeStruct((depth, 8, n), F32),
        grid=(depth, n // tn),
        in_specs=[pl.BlockSpec((8, d), lambda l, j: (0, 0)),
                  pl.BlockSpec((None, d, tn), lambda l, j: (l, 0, j)),
                  pl.BlockSpec((None, 1, tn), lambda l, j: (l, 0, j))],
        out_specs=pl.BlockSpec((None, 8, tn), lambda l, j: (l, 0, j)),
        compiler_params=_cparams(2),
        name="adaln_modulation",
    )(cond, ada_w, ada_b.reshape(depth, 1, n))


def _inproj_kernel(x_ref, nw_ref, shift_ref, scale_ref, w_ref, o_ref, h_ref):
    @pl.when(pl.program_id(1) == 0)
    def _():
        x = x_ref[...]
        y = x * lax.rsqrt(jnp.mean(x * x, axis=-1, keepdims=True) + NORM_EPS) * nw_ref[...]
        h_ref[...] = (y * (1.0 + scale_ref[...]) + shift_ref[...]).astype(BF16)

    o_ref[...] = jnp.dot(h_ref[...], w_ref[...], preferred_element_type=F32)


def in_projection(x, norm_w, mod, mod_row, w, name, tm=1024, tn=512):
    m, d = x.shape
    n = w.shape[1]
    tm = min(tm, m)
    return pl.pallas_call(
        _inproj_kernel,
        out_shape=jax.ShapeDtypeStruct((m, n), F32),
        grid=(m // tm, n // tn),
        in_specs=[pl.BlockSpec((tm, d), lambda i, j: (i, 0)),
                  pl.BlockSpec((1, d), lambda i, j: (0, 0)),
                  pl.BlockSpec((None, 1, d), lambda i, j: (mod_row(i), 0, 0)),
                  pl.BlockSpec((None, 1, d), lambda i, j: (mod_row(i), 0, 1)),
                  pl.BlockSpec((d, tn), lambda i, j: (0, j))],
        out_specs=pl.BlockSpec((tm, tn), lambda i, j: (i, j)),
        scratch_shapes=[pltpu.VMEM((tm, d), BF16)],
        compiler_params=_cparams(2),
        name=name,
    )(x, norm_w.reshape(1, d), mod, mod, w)


def _outproj_kernel(ya_ref, yb_ref, g_ref, w_ref, x_ref, gm_ref, *rest, ka, final):
    o_ref = rest[-1]
    sg = _silu(g_ref[...])
    acc = _dot(ya_ref[...].astype(F32) * sg[:, :ka], w_ref[:ka, :])
    acc += _dot(yb_ref[...].astype(F32) * sg[:, ka:], w_ref[ka:, :])
    out = x_ref[...] + gm_ref[...] * acc
    if final:
        out = out * lax.rsqrt(jnp.mean(out * out, axis=-1, keepdims=True) + NORM_EPS) * rest[0][...]
    o_ref[...] = out


def out_projection(ya, yb, gate, w, x, mod, mod_row, name, final_w=None, tm=256):
    m, d = x.shape
    ka, kb = ya.shape[1], yb.shape[1]
    tm = min(tm, m)
    ins = [ya, yb, gate, w, x, mod]
    specs = [pl.BlockSpec((tm, ka), lambda i: (i, 0)),
             pl.BlockSpec((tm, kb), lambda i: (i, 0)),
             pl.BlockSpec((tm, ka + kb), lambda i: (i, 0)),
             pl.BlockSpec((ka + kb, d), lambda i: (0, 0)),
             pl.BlockSpec((tm, d), lambda i: (i, 0)),
             pl.BlockSpec((None, 1, d), lambda i: (mod_row(i), 0, 2))]
    if final_w is not None:
        ins.append(final_w.reshape(1, d))
        specs.append(pl.BlockSpec((1, d), lambda i: (0, 0)))
    return pl.pallas_call(
        functools.partial(_outproj_kernel, ka=ka, final=final_w is not None),
        out_shape=jax.ShapeDtypeStruct((m, d), F32),
        grid=(m // tm,),
        in_specs=specs,
        out_specs=pl.BlockSpec((tm, d), lambda i: (i, 0)),
        compiler_params=_cparams(1),
        name=name,
    )(*ins)


def _softmax_pv(q, parts, scale):
    c = scale * LOG2E
    m = denom = num = None
    for k, bias, v in parts:
        x = _dot_nt(q, k) * c
        if bias is not None:
            x = x + bias
        m_part = jnp.max(x, axis=-1, keepdims=True)
        if m is None:
            m = m_part
            p = jnp.exp2(x - m)
            denom = jnp.sum(p, axis=-1, keepdims=True)
            num = _dot(p, v)
        else:
            m_new = jnp.maximum(m, m_part)
            alpha = jnp.exp2(m - m_new)
            p = jnp.exp2(x - m_new)
            denom = denom * alpha + jnp.sum(p, axis=-1, keepdims=True)
            num = num * alpha + _dot(p, v)
            m = m_new
    return num / denom


def _softmax_pv_joint(q, parts, scale):
    c = scale * LOG2E
    logits = [_dot_nt(q, k) * c if bias is None else _dot_nt(q, k) * c + bias for k, bias, _ in parts]
    m = functools.reduce(jnp.maximum, [jnp.max(x, axis=-1, keepdims=True) for x in logits])
    ps = [jnp.exp2(x - m) for x in logits]
    denom = functools.reduce(lambda a, b: a + b, [jnp.sum(p, axis=-1, keepdims=True) for p in ps])
    num = functools.reduce(lambda a, b: a + b, [_dot(p, v) for p, (_, _, v) in zip(ps, parts)])
    return num / denom


def _na_kernel(idx_ref, q_ref, k_ref, v_ref, kc_ref, vc_ref, tab_ref, o_ref, bias_ref, *, rows, scale):
    j = pl.program_id(1)
    npair = NA_KROWS // 2

    @pl.when(pl.program_id(2) == 0)
    def _():
        for qr in range(NA_QROWS):
            for m in range(npair):
                tile = tab_ref[idx_ref[(j * NA_QROWS + qr) * npair + m]]
                bias_ref[qr * GRID_W:(qr + 1) * GRID_W, m * LANES:(m + 1) * LANES] = tile

    start = pl.multiple_of(jnp.clip(NA_QROWS * j - NA_WIN_ROWS // 2, 0, rows - NA_KROWS) * GRID_W, GRID_W)
    win = pl.ds(start, NA_KROWS * GRID_W)
    parts = [(k_ref[win, :], bias_ref[...], v_ref[win, :]), (kc_ref[...], None, vc_ref[...])]
    o_ref[...] = _softmax_pv_joint(q_ref[...], parts, scale)


def _na_tile_ids(rows):
    kh = min(NA_WIN_ROWS, rows)
    n = 2 * NA_WIN_ROWS - 1
    ids = []
    for j in range(rows // NA_QROWS):
        ks = int(np.clip(NA_QROWS * j - NA_WIN_ROWS // 2, 0, rows - NA_KROWS))
        for qr in range(NA_QROWS):
            r = NA_QROWS * j + qr
            r0 = int(np.clip(r - kh // 2, 0, rows - kh))
            for m in range(NA_KROWS // 2):
                kr = ks + 2 * m
                dy = kr - r + NA_WIN_ROWS - 1
                first, second = r0 <= kr < r0 + kh, r0 <= kr + 1 < r0 + kh
                ids.append(dy if first and second else n + dy if first else 2 * n + dy + 1 if second else 3 * n)
    return np.array(ids, np.int32)


def _na_bias_tiles(rpb):
    col = np.arange(GRID_W)
    cs = np.clip(col - NA_WIN_COLS // 2, 0, GRID_W - NA_WIN_COLS)
    kc = col[None, :]
    in_win = (kc >= cs[:, None]) & (kc < cs[:, None] + NA_WIN_COLS)
    dx = np.where(in_win, kc - col[:, None] + NA_WIN_COLS - 1, 0)
    tab = jnp.where(in_win[None, None], rpb[:, :, dx] * LOG2E, NEG_INF)
    neg = jnp.full_like(tab[:, :1], NEG_INF)
    nxt = jnp.concatenate([tab[:, 1:], neg], axis=1)
    both = jnp.concatenate([tab, nxt], axis=-1)
    first = jnp.concatenate([tab, jnp.broadcast_to(neg, tab.shape)], axis=-1)
    second = jnp.concatenate([jnp.broadcast_to(neg, tab.shape), tab], axis=-1)
    return jnp.concatenate([both, first, second, jnp.concatenate([neg, neg], axis=-1)], axis=1)


def na_attention(z, z_c, rpb, batch, t, l):
    rows = t // GRID_W
    nj = rows // NA_QROWS
    tq = NA_QROWS * GRID_W
    ids = jnp.asarray(_na_tile_ids(rows))
    tiles = _na_bias_tiles(rpb)
    h8 = NA_HEADS
    return pl.pallas_call(
        functools.partial(_na_kernel, rows=rows, scale=NA_HEAD_DIM ** -0.5),
        out_shape=jax.ShapeDtypeStruct((batch * t, NA_WIDTH), F32),
        grid_spec=pltpu.PrefetchScalarGridSpec(
            num_scalar_prefetch=1,
            grid=(h8, nj, batch),
            in_specs=[pl.BlockSpec((tq, LANES), lambda h, j, b, g: (b * nj + j, EV_Q_BLK + h)),
                      pl.BlockSpec((t, LANES), lambda h, j, b, g: (b, EV_Q_BLK + h8 + h)),
                      pl.BlockSpec((t, LANES), lambda h, j, b, g: (b, EV_Q_BLK + 2 * h8 + h)),
                      pl.BlockSpec((l, LANES), lambda h, j, b, g: (b, EV_Q_BLK + h8 + h)),
                      pl.BlockSpec((l, LANES), lambda h, j, b, g: (b, EV_Q_BLK + 2 * h8 + h)),
                      pl.BlockSpec((None,) + tiles.shape[1:], lambda h, j, b, g: (h, 0, 0, 0))],
            out_specs=pl.BlockSpec((tq, LANES), lambda h, j, b, g: (b * nj + j, h)),
            scratch_shapes=[pltpu.VMEM((tq, NA_KROWS * GRID_W), F32)]),
        compiler_params=_cparams(3),
        name="na_attention",
    )(ids, z, z, z, z_c, z_c, tiles)


def _dense_attn_kernel(q_ref, k_ref, v_ref, o_ref, *, scale):
    o_ref[...] = _softmax_pv(q_ref[...], [(k_ref[...], None, v_ref[...])], scale)


def ctx_attention(z_c, batch, l):
    h8 = NA_HEADS
    return pl.pallas_call(
        functools.partial(_dense_attn_kernel, scale=NA_HEAD_DIM ** -0.5),
        out_shape=jax.ShapeDtypeStruct((batch * l, NA_WIDTH), F32),
        grid=(batch, h8),
        in_specs=[pl.BlockSpec((l, LANES), lambda b, h: (b, EV_Q_BLK + h)),
                  pl.BlockSpec((l, LANES), lambda b, h: (b, EV_Q_BLK + h8 + h)),
                  pl.BlockSpec((l, LANES), lambda b, h: (b, EV_Q_BLK + 2 * h8 + h))],
        out_specs=pl.BlockSpec((l, LANES), lambda b, h: (b, h)),
        compiler_params=_cparams(2),
        name="ctx_attention",
    )(z_c, z_c, z_c)


def _head_sum(x):
    first = lax.broadcasted_iota(jnp.int32, x.shape, 1) < RW_HEAD_DIM
    s0 = jnp.sum(jnp.where(first, x, 0.0), axis=-1, keepdims=True)
    s1 = jnp.sum(jnp.where(first, 0.0, x), axis=-1, keepdims=True)
    return jnp.where(first, s0, s1)


def _softplus(x):
    return jnp.maximum(x, 0.0) + jnp.log1p(jnp.exp(-jnp.abs(x)))


def _rw_terms_kernel(ur_ref, uk_ref, uv_ref, uwa_ref, pr_ref, pk_ref, pv_ref, pwa_ref, nr_ref, nk_ref, nv_ref,
                     nwa_ref, mu_ref, w0_ref, w2_ref, a0_ref, a2_ref, kk_ref, ka_ref, rk_ref,
                     r_o, v_o, kkn_o, dec0_o, dec1_o, kd0_o, kd1_o, b0_o, b1_o, bonus_o, *, nt):
    i = pl.program_id(1)
    tm = ur_ref.shape[0]

    def shifted(u_ref, p_ref, n_ref, col0):
        u = u_ref[...]
        mu = mu_ref[:, col0:col0 + u.shape[1]]
        row = lax.broadcasted_iota(jnp.int32, u.shape, 0)
        before = jnp.where(i == 0, 0.0, p_ref[7:8, :])
        after = jnp.where(i == nt - 1, 0.0, n_ref[0:1, :])
        prev = jnp.where(row == 0, before, pltpu.roll(u, 1, axis=0))
        nxt = jnp.where(row == tm - 1, after, pltpu.roll(u, tm - 1, axis=0))
        return u + mu[0:1] * (prev - u) + mu[1:2] * (nxt - u)

    r_all = shifted(ur_ref, pr_ref, nr_ref, 0)
    k_all = shifted(uk_ref, pk_ref, nk_ref, RW_WIDTH)
    v_all = shifted(uv_ref, pv_ref, nv_ref, 2 * RW_WIDTH)
    wa = shifted(uwa_ref, pwa_ref, nwa_ref, 3 * RW_WIDTH)
    wd = jnp.tanh(wa[:, :LANES]).astype(BF16)
    ad = wa[:, LANES:].astype(BF16)
    outs = (r_o, v_o, kkn_o, dec0_o, dec1_o, kd0_o, kd1_o, b0_o, b1_o, bonus_o)
    vals = [[] for _ in outs]
    for p in range(RW_WIDTH // LANES):
        sl = slice(p * LANES, (p + 1) * LANES)
        r, k, v = r_all[:, sl], k_all[:, sl], v_all[:, sl]
        kk = k * kk_ref[:, sl]
        kk = kk / jnp.maximum(jnp.sqrt(_head_sum(kk * kk)), 1e-12)
        per_dir = []
        kd_sum = jnp.zeros_like(k)
        for d in range(2):
            w = w0_ref[d:d + 1, sl] + jnp.dot(wd, w2_ref[d, :, sl].astype(BF16), preferred_element_type=F32)
            a = _sigmoid(a0_ref[d:d + 1, sl] + jnp.dot(ad, a2_ref[d, :, sl].astype(BF16), preferred_element_type=F32))
            kd = k * (1.0 + (a - 1.0) * ka_ref[:, sl])
            per_dir.append((jnp.exp(-jnp.exp(-_softplus(-w) - 0.5)), kd, kk * a))
            kd_sum = kd_sum + kd
        bonus = _head_sum(r * kd_sum * rk_ref[:, sl]) * v
        (dec0, kd0, b0), (dec1, kd1, b1) = per_dir
        for lst, val in zip(vals, (r, v, kk, dec0, dec1, kd0, kd1, b0, b1, bonus)):
            lst.append(val)
    for o_ref, lst in zip(outs, vals):
        o_ref[...] = jnp.swapaxes(jnp.stack(lst, axis=0), 0, 1)


def rwkv_terms(z, n_seq, t, mu, w0, w2p, a0, a2p, k_k, k_a, r_k, tm=256):
    tm = min(tm, t)
    nt = t // tm
    npair = RW_WIDTH // LANES
    wide0 = EV_RW_BLK * LANES // RW_WIDTH
    wa_blk = (EV_RW_BLK + 3 * npair) // 2
    n8 = tm // 8

    def cur(width, cb):
        return pl.BlockSpec((tm, width), lambda b, i: (b * nt + i, cb))

    def before(width, cb):
        return pl.BlockSpec((8, width), lambda b, i: (jnp.maximum((b * nt + i) * n8 - 1, 0), cb))

    def after(width, cb):
        return pl.BlockSpec((8, width), lambda b, i: (jnp.minimum((b * nt + i + 1) * n8, n_seq * nt * n8 - 1), cb))

    cols = [(RW_WIDTH, wide0), (RW_WIDTH, wide0 + 1), (RW_WIDTH, wide0 + 2), (2 * LANES, wa_blk)]
    whole = lambda a: pl.BlockSpec(a.shape, lambda b, i: (0,) * a.ndim)
    params = (mu, w0, w2p, a0, a2p, k_k, k_a, r_k)
    out = pl.BlockSpec((tm, npair, LANES), lambda b, i: (i, b, 0))
    return pl.pallas_call(
        functools.partial(_rw_terms_kernel, nt=nt),
        out_shape=[jax.ShapeDtypeStruct((t, n_seq * npair, LANES), F32)] * 10,
        grid=(n_seq, nt),
        in_specs=([cur(*c) for c in cols] + [before(*c) for c in cols] + [after(*c) for c in cols]
                  + [whole(a) for a in params]),
        out_specs=[out] * 10,
        compiler_params=_cparams(2),
        name="rwkv_terms",
    )(*([z] * 12), *params)


def _to_lanes(n):
    q = jnp.concatenate([n, pltpu.roll(n, RW_HEAD_DIM, axis=1)], axis=0)
    return q.T[:RW_HEAD_DIM]


def _rw_scan_kernel(rf, rb, vf, vb, af, ab, w0, w1, k0, k1, b0, b1, s0_ref, yf_ref, yb_ref, sfin_ref,
                    s_ref, z_ref, znext_ref, g_ref, ybuf_ref, *, chunk):
    n = RW_HEAD_DIM
    srcs = ((rf, rb), (w0, w1), (k0, k1), (vf, vb), (af, ab), (b0, b1))
    R, K, V, A, B = range(5)

    @pl.when(pl.program_id(0) == 0)
    def _():
        s_ref[...] = s0_ref[...]

    def convert(t_src):
        r, w, k, v, kk, b = (_to_lanes(jnp.concatenate([f[t_src], bw[chunk - 1 - t_src]], axis=0))
                             for f, bw in srcs)
        g_prev = g_ref[0]
        g = g_prev * w
        g_ref[0] = g
        inv = 1.0 / g
        znext_ref[R] = r * g
        znext_ref[K] = k * inv
        znext_ref[V] = v
        znext_ref[A] = -(kk * g_prev)
        znext_ref[B] = b * inv

    def emit_y(t):
        rows = _to_lanes(ybuf_ref[...])
        yf_ref[t] = rows[:n // 2]
        yb_ref[chunk - 1 - t] = rows[n // 2:]

    g_ref[0] = jnp.ones((n, LANES), F32)
    ybuf_ref[...] = jnp.zeros((n, LANES), F32)
    convert(0)

    def step(t, carry):
        emit_y(jnp.maximum(t - 1, 0))
        z_ref[...] = znext_ref[...]
        g_ref[1] = g_ref[0]
        convert(jnp.minimum(t + 1, chunk - 1))
        sa = [jnp.zeros((n, LANES), F32), jnp.zeros((n, LANES), F32)]
        for k in range(n):
            sa[k % 2] = sa[k % 2] + s_ref[k] * z_ref[A, k:k + 1, :]
        sa = sa[0] + sa[1]
        vt = z_ref[V]
        y = [jnp.zeros((n, LANES), F32), jnp.zeros((n, LANES), F32)]
        for k in range(n):
            sk = s_ref[k] + sa * z_ref[B, k:k + 1, :] + vt * z_ref[K, k:k + 1, :]
            s_ref[k] = sk
            y[k % 2] = y[k % 2] + sk * z_ref[R, k:k + 1, :]
        ybuf_ref[...] = y[0] + y[1]
        return carry

    lax.fori_loop(0, chunk, step, 0)
    emit_y(chunk - 1)
    for k in range(n):
        s_ref[k] = s_ref[k] * g_ref[1, k:k + 1, :]
    sfin_ref[...] = s_ref[...]


def rwkv_scan(r, v, kkn, dec0, dec1, kd0, kd1, b0, b1, s0):
    t, nq, _ = r.shape
    n = RW_HEAD_DIM
    chunk = RW_SCAN_CHUNK
    nc = t // chunk
    fwd = pl.BlockSpec((chunk, nq, LANES), lambda i: (i, 0, 0))
    bwd = pl.BlockSpec((chunk, nq, LANES), lambda i: (nc - 1 - i, 0, 0))
    state = pl.BlockSpec((n, n, LANES), lambda i: (0, 0, 0))
    slab = jax.ShapeDtypeStruct((t, nq, LANES), F32)
    return pl.pallas_call(
        functools.partial(_rw_scan_kernel, chunk=chunk),
        out_shape=[slab, slab, jax.ShapeDtypeStruct((n, n, LANES), F32)],
        grid=(nc,),
        in_specs=[fwd, bwd, fwd, bwd, fwd, bwd, fwd, bwd, fwd, bwd, fwd, bwd, state],
        out_specs=[fwd, bwd, state],
        scratch_shapes=[pltpu.VMEM((n, n, LANES), F32), pltpu.VMEM((5, n, LANES), F32),
                        pltpu.VMEM((5, n, LANES), F32), pltpu.VMEM((2, n, LANES), F32),
                        pltpu.VMEM((n, LANES), F32)],
        compiler_params=_cparams(1),
        name="rwkv_scan",
    )(r, r, v, v, kkn, kkn, dec0, dec1, kd0, kd1, b0, b1, s0)


def _rw_readout_kernel(yf_ref, yb_ref, bonus_ref, lw_ref, lb_ref, o_ref):
    npair = RW_WIDTH // LANES
    y_all = jnp.swapaxes(yf_ref[...] + yb_ref[...], 0, 1)
    bonus_all = jnp.swapaxes(bonus_ref[...], 0, 1)
    for q in range(yf_ref.shape[1]):
        b, p = divmod(q, npair)
        sl = slice(p * LANES, (p + 1) * LANES)
        y = y_all[q]
        mu = _head_sum(y) * (1.0 / RW_HEAD_DIM)
        d = y - mu
        var = _head_sum(d * d) * (1.0 / RW_HEAD_DIM)
        o_ref[b, :, sl] = d * lax.rsqrt(var + RW_GN_EPS) * lw_ref[:, sl] + lb_ref[:, sl] + bonus_all[q]


def rwkv_readout(y_f, y_b, bonus, ln_w, ln_b, tm=256):
    t, nq, _ = y_f.shape
    tm = min(tm, t)
    batch = nq // (RW_WIDTH // LANES)
    slab = pl.BlockSpec((tm, nq, LANES), lambda i: (i, 0, 0))
    vec = pl.BlockSpec((1, RW_WIDTH), lambda i: (0, 0))
    return pl.pallas_call(
        _rw_readout_kernel,
        out_shape=jax.ShapeDtypeStruct((batch, t, RW_WIDTH), F32),
        grid=(t // tm,),
        in_specs=[slab, slab, slab, vec, vec],
        out_specs=pl.BlockSpec((batch, tm, RW_WIDTH), lambda i: (0, i, 0)),
        compiler_params=_cparams(1),
        name="rwkv_readout",
    )(y_f, y_b, bonus, ln_w.reshape(1, -1), ln_b.reshape(1, -1))


def _rms(x, w):
    return x * lax.rsqrt(jnp.mean(x * x, axis=-1, keepdims=True) + NORM_EPS) * w


def _rope(x, cos, sin):
    odd = (lax.broadcasted_iota(jnp.int32, x.shape, 1) & 1) == 1
    swapped = jnp.where(odd, pltpu.roll(x, 1, axis=1), pltpu.roll(x, LANES - 1, axis=1))
    return x * cos + swapped * sin


def _mla_q_kernel(c_ref, nw_ref, w_ref, cos_ref, sin_ref, q_ref, h_ref):
    @pl.when(pl.program_id(1) == 0)
    def _():
        h_ref[...] = _rms(c_ref[...], nw_ref[...]).astype(BF16)

    acc = jnp.dot(h_ref[...], w_ref[...], preferred_element_type=F32)
    q_ref[:, :MLA_NOPE] = acc[:, :MLA_NOPE].astype(BF16)
    q_ref[:, MLA_NOPE:] = _rope(acc[:, MLA_NOPE:], cos_ref[...], sin_ref[...]).astype(BF16)


def mla_queries(z, q_norm, w_uq_pad, cos, sin, t, tm=512):
    m = z.shape[0]
    tm = min(tm, t)
    nt = t // tm
    return pl.pallas_call(
        _mla_q_kernel,
        out_shape=jax.ShapeDtypeStruct((m, MLA_HEADS * MLA_QK_PAD), BF16),
        grid=(m // tm, MLA_HEADS),
        in_specs=[pl.BlockSpec((tm, MLA_Q_RANK), lambda i, h: (i, GATE_COLS // MLA_Q_RANK)),
                  pl.BlockSpec((1, MLA_Q_RANK), lambda i, h: (0, 0)),
                  pl.BlockSpec((MLA_Q_RANK, MLA_QK_PAD), lambda i, h: (0, h)),
                  pl.BlockSpec((tm, LANES), lambda i, h: (i % nt, 0)),
                  pl.BlockSpec((tm, LANES), lambda i, h: (i % nt, 0))],
        out_specs=pl.BlockSpec((tm, MLA_QK_PAD), lambda i, h: (i, h)),
        scratch_shapes=[pltpu.VMEM((tm, MLA_Q_RANK), BF16)],
        compiler_params=_cparams(2),
        name="mla_queries",
    )(z, q_norm.reshape(1, -1), w_uq_pad, cos, sin)


def _mla_kv_kernel(c_ref, pe_ref, nw_ref, w_ref, cos_ref, sin_ref, k_ref, v_ref, h_ref):
    @pl.when(pl.program_id(1) == 0)
    def _():
        h_ref[...] = _rms(c_ref[...], nw_ref[...]).astype(BF16)

    acc = jnp.dot(h_ref[...], w_ref[...], preferred_element_type=F32)
    k_ref[:, :MLA_NOPE] = acc[:, :MLA_NOPE].astype(BF16)
    k_ref[:, MLA_NOPE:] = _rope(pe_ref[...], cos_ref[...], sin_ref[...]).astype(BF16)
    v_ref[...] = acc[:, MLA_NOPE:].astype(BF16)


def mla_keys_values(z, kv_norm, w_ukv, cos, sin, t, tm=512):
    m = z.shape[0]
    tm = min(tm, t)
    nt = t // tm
    return pl.pallas_call(
        _mla_kv_kernel,
        out_shape=[jax.ShapeDtypeStruct((m, MLA_HEADS * MLA_QK_PAD), BF16),
                   jax.ShapeDtypeStruct((m, MLA_WIDTH), BF16)],
        grid=(m // tm, MLA_HEADS),
        in_specs=[pl.BlockSpec((tm, MLA_KV_RANK), lambda i, h: (i, (GATE_COLS + MLA_Q_RANK) // MLA_KV_RANK)),
                  pl.BlockSpec((tm, LANES), lambda i, h: (i, OD_A_BLK + (MLA_Q_RANK + MLA_KV_RANK) // LANES)),
                  pl.BlockSpec((1, MLA_KV_RANK), lambda i, h: (0, 0)),
                  pl.BlockSpec((MLA_KV_RANK, MLA_NOPE + MLA_V), lambda i, h: (0, h)),
                  pl.BlockSpec((tm, LANES), lambda i, h: (i % nt, 0)),
                  pl.BlockSpec((tm, LANES), lambda i, h: (i % nt, 0))],
        out_specs=[pl.BlockSpec((tm, MLA_QK_PAD), lambda i, h: (i, h)),
                   pl.BlockSpec((tm, MLA_V), lambda i, h: (i, h))],
        scratch_shapes=[pltpu.VMEM((tm, MLA_KV_RANK), BF16)],
        compiler_params=_cparams(2),
        name="mla_keys_values",
    )(z, z, kv_norm.reshape(1, -1), w_ukv, cos, sin)


def _mla_attn_kernel(q_ref, k_ref, v_ref, kc_ref, vc_ref, o_ref, *, scale):
    parts = [(kc_ref[...], None, vc_ref[...])]
    nkeys = k_ref.shape[0]
    step = min(MLA_KEY_CHUNK, nkeys)
    for lo in range(0, nkeys, step):
        parts.append((k_ref[lo:lo + step, :], None, v_ref[lo:lo + step, :]))
    o_ref[...] = _softmax_pv(q_ref[...], parts, scale)


def mla_attention(q, k, v, kc, vc, batch, t, l, tq=512):
    tq = min(tq, t)
    nq = t // tq
    return pl.pallas_call(
        functools.partial(_mla_attn_kernel, scale=(MLA_NOPE + MLA_ROPE) ** -0.5),
        out_shape=jax.ShapeDtypeStruct((batch * t, MLA_WIDTH), F32),
        grid=(batch, MLA_HEADS, nq),
        in_specs=[pl.BlockSpec((tq, MLA_QK_PAD), lambda b, h, i: (b * nq + i, h)),
                  pl.BlockSpec((t, MLA_QK_PAD), lambda b, h, i: (b, h)),
                  pl.BlockSpec((t, MLA_V), lambda b, h, i: (b, h)),
                  pl.BlockSpec((l, MLA_QK_PAD), lambda b, h, i: (b, h)),
                  pl.BlockSpec((l, MLA_V), lambda b, h, i: (b, h))],
        out_specs=pl.BlockSpec((tq, MLA_V), lambda b, h, i: (b * nq + i, h)),
        compiler_params=_cparams(3),
        name="mla_attention",
    )(q, k, v, kc, vc)


def _rope_tables(t):
    tok = np.arange(t)
    pos = np.stack([tok // GRID_W, tok % GRID_W], axis=-1).astype(np.float32)
    n_freq = MLA_ROPE // 4
    inv = (ROPE_BASE ** (-jnp.arange(n_freq, dtype=F32) / n_freq))
    ang = (jnp.asarray(pos)[:, :, None] * inv).reshape(t, MLA_ROPE // 2)
    cos = jnp.repeat(jnp.cos(ang), 2, axis=-1)
    sin = jnp.repeat(jnp.sin(ang), 2, axis=-1) * jnp.tile(jnp.array([-1.0, 1.0], F32), MLA_ROPE // 2)
    pad = ((0, 0), (0, LANES - MLA_ROPE))
    return jnp.pad(cos, pad), jnp.pad(sin, pad)


def _split3(x):
    hi = x.astype(BF16)
    r1 = x - hi.astype(F32)
    mid = r1.astype(BF16)
    lo = (r1 - mid.astype(F32)).astype(BF16)
    return hi, mid, lo


def _hg_kernel(q_ref, f_ref, v_ref, lb_ref, s0_ref, *rest, reverse, finalize, emit_state):
    st_ref = rest[-1]
    rest = rest[:-1]
    if finalize:
        prev_ref, nw_ref = rest[0], rest[1]
        rest = rest[2:]
    o_ref = rest[0]

    @pl.when(pl.program_id(2) == 0)
    def _():
        st_ref[...] = s0_ref[...]

    bt = q_ref.shape[0]
    nchunk = bt // HG_CHUNK
    order = range(nchunk - 1, -1, -1) if reverse else range(nchunk)
    ri = lax.broadcasted_iota(jnp.int32, (bt, bt), 0)
    ci = lax.broadcasted_iota(jnp.int32, (bt, bt), 1)
    same = (ri // HG_CHUNK) == (ci // HG_CHUNK)
    tri = same & ((ci >= ri) if reverse else (ci <= ri))
    tri_f = tri.astype(F32)
    tri_b = tri.astype(BF16)

    lb = lb_ref[...]
    fz = f_ref[...]
    log_sig = jnp.minimum(fz, 0.0) - jnp.log1p(jnp.exp(-jnp.abs(fz)))
    la = jnp.log(lb)
    lbb = jnp.log1p(-lb) + log_sig
    log_f = jnp.maximum(la, lbb) + jnp.log1p(jnp.exp(-jnp.abs(la - lbb)))
    kf_all = (1.0 - lb) * _sigmoid(-fz)
    pieces = jnp.dot(tri_b, jnp.concatenate(_split3(log_f), axis=1), preferred_element_type=F32)
    width = log_f.shape[1]
    cum_all = pieces[:, :width] + pieces[:, width:2 * width] + pieces[:, 2 * width:]

    for hh in range(HG_STEP_HEADS):
        sl = slice(hh * HG_DK, (hh + 1) * HG_DK)
        cum, kf, v = cum_all[:, sl], kf_all[:, sl], v_ref[:, sl]
        last = [cum[c * HG_CHUNK:c * HG_CHUNK + 1] if reverse else cum[(c + 1) * HG_CHUNK - 1:(c + 1) * HG_CHUNK]
                for c in range(nchunk)]
        tot = jnp.concatenate([jnp.broadcast_to(x, (HG_CHUNK, HG_DK)) for x in last], axis=0)
        q_in = q_ref[:, sl] * jnp.exp(cum)
        k_in = kf * jnp.exp(-cum)
        k_end = kf * jnp.exp(tot - cum)
        o_intra = _dot(_dot_nt(q_in, k_in) * tri_f, v)
        st = st_ref[hh]
        for c in order:
            rows = slice(c * HG_CHUNK, (c + 1) * HG_CHUNK)
            o_c = o_intra[rows] + _dot_nt(q_in[rows], st)
            if finalize:
                o_c = o_c + prev_ref[rows, sl]
                o_c = o_c * lax.rsqrt(jnp.mean(o_c * o_c, axis=-1, keepdims=True) + NORM_EPS) * nw_ref[...]
            o_ref[rows, sl] = o_c
            st = st * jnp.exp(last[c]) + _dot_tn(v[rows], k_end[rows])
        st_ref[hh] = st
        if emit_state:
            rest[1][hh] = st


def hgrn2_pass(z, lb, s0, batch, t, f_off, reverse, prev=None, norm_w=None, emit_state=False):
    bt = min(HG_BLOCK, t)
    nblk = t // bt
    nh = HG_STEP_HEADS
    wide = nh * LANES
    finalize = prev is not None
    assert OD_HG_BLK % nh == 0 and HG_HEADS % nh == 0, "head groups must start on a whole column block"
    pos = (lambda i: nblk - 1 - i) if reverse else (lambda i: i)
    blk = lambda off: pl.BlockSpec((bt, wide), lambda b, h, i, off=off: (b * nblk + pos(i), off // nh + h))
    state = pl.BlockSpec((None, nh, HG_DV, HG_DK), lambda b, h, i: (b, h, 0, 0))
    ins = [z, z, z, lb.reshape(1, -1), s0]
    specs = [blk(OD_HG_BLK), blk(OD_HG_BLK + f_off), blk(OD_HG_BLK + 3 * HG_HEADS),
             pl.BlockSpec((1, wide), lambda b, h, i: (0, h)), state]
    if finalize:
        ins += [prev, norm_w.reshape(1, -1)]
        specs += [blk(0), pl.BlockSpec((1, HG_DV), lambda b, h, i: (0, 0))]
    out_shape = [jax.ShapeDtypeStruct((batch * t, HG_WIDTH), F32)]
    out_specs = [blk(0)]
    if emit_state:
        out_shape.append(jax.ShapeDtypeStruct((batch, HG_HEADS, HG_DV, HG_DK), F32))
        out_specs.append(state)
    return pl.pallas_call(
        functools.partial(_hg_kernel, reverse=reverse, finalize=finalize, emit_state=emit_state),
        out_shape=out_shape,
        grid=(batch, HG_HEADS // nh, nblk),
        in_specs=specs,
        out_specs=out_specs,
        scratch_shapes=[pltpu.VMEM((nh, HG_DV, HG_DK), F32)],
        compiler_params=_cparams(3),
        name="hgrn2_" + ("bwd" if reverse else "fwd"),
    )(*ins)


def _even_layer(x, xc, mod, norm_w, w_in, w_out, rpb, rw_params, batch, t, l):
    d = x.shape[1]
    s2 = 3 * NA_WIDTH + RW_SHIFT_COLS
    zero = jnp.zeros((d, RW_PAD_COLS - RW_SHIFT_COLS), BF16)
    w_in = w_in.astype(BF16)
    w_ext = jnp.concatenate([w_in[:, s2:], w_in[:, :s2], zero], axis=1)
    tiles_per_seq = max(t // 1024, 1)
    z = in_projection(x, norm_w, mod, lambda i: i // tiles_per_seq, w_ext, "even_in_proj")
    z_c = in_projection(xc, norm_w, mod, lambda i: batch, w_ext, "even_in_proj_ctx")

    y_na = na_attention(z, z_c, rpb, batch, t, l)
    yc_na = ctx_attention(z_c, batch, l)

    mu, w0, w2, a0, a2, k_k, k_a, r_k, ln_w, ln_b = rw_params
    mu = jnp.pad(mu, ((0, 0), (0, RW_PAD_COLS - RW_SHIFT_COLS)))
    zpad = jnp.zeros((RW_RANK, RW_WIDTH), F32)
    w2p = jnp.stack([jnp.concatenate([w2[0], zpad]), jnp.concatenate([zpad, w2[1]])])
    a2p = jnp.stack([jnp.concatenate([a2[0], zpad]), jnp.concatenate([zpad, a2[1]])])
    vecs = (mu, w0, w2p, a0, a2p, k_k.reshape(1, -1), k_a.reshape(1, -1), r_k.reshape(1, -1))
    terms = rwkv_terms(z, batch, t, *vecs)
    terms_c = rwkv_terms(z_c, batch, l, *vecs)
    s_zero = jnp.zeros((RW_HEAD_DIM, RW_HEAD_DIM, LANES), F32)
    yc_f, yc_b, s_ctx = rwkv_scan(*terms_c[:9], s_zero)
    y_f, y_b, _ = rwkv_scan(*terms[:9], s_ctx)
    y_rw = rwkv_readout(y_f, y_b, terms[9], ln_w, ln_b).reshape(batch * t, RW_WIDTH)
    yc_rw = rwkv_readout(yc_f, yc_b, terms_c[9], ln_w, ln_b).reshape(batch * l, RW_WIDTH)

    w_out = w_out.astype(BF16)
    rows_per_seq = max(t // 256, 1)
    x = out_projection(y_na, y_rw, z, w_out, x, mod, lambda i: i // rows_per_seq, "even_out_proj")
    xc = out_projection(yc_na, yc_rw, z_c, w_out, xc, mod, lambda i: batch, "even_out_proj_ctx")
    return x, xc


def _odd_layer(x, xc, mod, norm_w, w_in, w_out, q_norm, w_uq, kv_norm, w_ukv, lb, hg_norm_w, final_w,
               batch, t, l):
    d = x.shape[1]
    o3 = MLA_Q_RANK + MLA_KV_RANK + MLA_ROPE
    o4 = o3 + 4 * HG_FDIM
    zero = jnp.zeros((d, MLA_A_COLS - o3), BF16)
    w_in = w_in.astype(BF16)
    w_ext = jnp.concatenate([w_in[:, o4:], w_in[:, :o3], zero, w_in[:, o3:o4]], axis=1)
    tiles_per_seq = max(t // 1024, 1)
    z = in_projection(x, norm_w, mod, lambda i: i // tiles_per_seq, w_ext, "odd_in_proj")
    z_c = in_projection(xc, norm_w, mod, lambda i: batch, w_ext, "odd_in_proj_ctx")

    cos, sin = _rope_tables(t)
    qk = MLA_NOPE + MLA_ROPE
    w_uq_pad = jnp.pad(w_uq.reshape(MLA_Q_RANK, MLA_HEADS, qk), ((0, 0), (0, 0), (0, MLA_QK_PAD - qk)))
    w_uq_pad = w_uq_pad.reshape(MLA_Q_RANK, MLA_HEADS * MLA_QK_PAD).astype(BF16)
    w_ukv = w_ukv.astype(BF16)
    q = mla_queries(z, q_norm, w_uq_pad, cos, sin, t)
    k, v = mla_keys_values(z, kv_norm, w_ukv, cos, sin, t)
    ones = jnp.ones((l, LANES), F32)
    kc, vc = mla_keys_values(z_c, kv_norm, w_ukv, ones, jnp.zeros_like(ones), l)
    y_mla = mla_attention(q, k, v, kc, vc, batch, t, l)

    s_zero = jnp.zeros((batch, HG_HEADS, HG_DV, HG_DK), F32)
    _, s_f = hgrn2_pass(z_c, lb, s_zero, batch, l, HG_HEADS, False, emit_state=True)
    _, s_b = hgrn2_pass(z_c, lb, s_zero, batch, l, 2 * HG_HEADS, True, emit_state=True)
    o_f, = hgrn2_pass(z, lb, s_f, batch, t, HG_HEADS, False)
    y_hg, = hgrn2_pass(z, lb, s_b, batch, t, 2 * HG_HEADS, True, prev=o_f, norm_w=hg_norm_w)

    rows_per_seq = max(t // 256, 1)
    return out_projection(y_mla, y_hg, z, w_out.astype(BF16), x, mod, lambda i: i // rows_per_seq,
                          "odd_out_proj", final_w=final_w)


def kernel(x, c, ctx, c_ctx, ada_w, ada_b, norm_w, e_w_in, e_w_out, na_rpb, rw_mu, rw_w0, rw_w2, rw_a0, rw_a2, rw_k_k, rw_k_a, rw_r_k, rw_ln_w, rw_ln_b, o_w_in, o_w_out, mla_q_norm, mla_w_uq, mla_kv_norm, mla_w_ukv, hg_lower_bounds, hg_norm_w, final_norm_w):
    batch, t, d = x.shape
    l = ctx.shape[1]
    assert ada_w.shape[0] == 2, "one even and one odd layer"
    cond = jnp.concatenate([c, c_ctx[None, :], jnp.zeros((8 - batch - 1, d), F32)], axis=0)
    mod = modulation(cond, ada_w, ada_b)
    s = jax.nn.softmax(hg_lower_bounds.astype(F32), axis=0)
    lower = jnp.cumsum(s, axis=0) - s[0]

    xf, xcf = x.reshape(batch * t, d), ctx.reshape(batch * l, d)
    rw_params = (rw_mu[0], rw_w0[0], rw_w2[0], rw_a0[0], rw_a2[0], rw_k_k[0], rw_k_a[0], rw_r_k[0],
                 rw_ln_w[0], rw_ln_b[0])
    xf, xcf = _even_layer(xf, xcf, mod[0].reshape(8, 1, 3 * d), norm_w[0], e_w_in[0], e_w_out[0], na_rpb[0],
                          rw_params, batch, t, l)
    out = _odd_layer(xf, xcf, mod[1].reshape(8, 1, 3 * d), norm_w[1], o_w_in[0], o_w_out[0], mla_q_norm[0],
                     mla_w_uq[0], mla_kv_norm[0], mla_w_ukv[0], lower[1], hg_norm_w[0], final_norm_w,
                     batch, t, l)
    return out.reshape(batch, t, d)
```

```python
import functools
import math

import numpy as np
import jax
import jax.numpy as jnp
from jax import lax
from jax.experimental import pallas as pl
from jax.experimental.pallas import tpu as pltpu

F32 = jnp.float32
BF16 = jnp.bfloat16

GRID_W = 64
NORM_EPS = 1e-6
ROPE_BASE = 10000.0

NA_HEADS = 8
NA_HEAD_DIM = 128
NA_WIDTH = NA_HEADS * NA_HEAD_DIM
NA_WIN_ROWS = 8
NA_WIN_COLS = 16
NA_QROWS = 4
NA_KROWS = 12

RW_HEAD_DIM = 64
RW_HEADS = 16
RW_WIDTH = RW_HEADS * RW_HEAD_DIM
RW_RANK = 64
RW_SHIFT_COLS = 3 * RW_WIDTH + 4 * RW_RANK
RW_PAD_COLS = 3584
RW_GN_EPS = 64e-5
RW_SCAN_CHUNK = 32

MLA_HEADS = 8
MLA_Q_RANK = 512
MLA_KV_RANK = 512
MLA_NOPE = 128
MLA_ROPE = 64
MLA_V = 128
MLA_QK_PAD = 256
MLA_WIDTH = MLA_HEADS * MLA_V
MLA_A_COLS = 1536
MLA_KEY_CHUNK = 512

HG_HEADS = 8
HG_DK = 128
HG_DV = 128
HG_FDIM = HG_HEADS * HG_DK
HG_WIDTH = HG_HEADS * HG_DV
HG_CHUNK = 32
HG_BLOCK = 256
HG_STEP_HEADS = 4

LANES = 128

GATE_COLS = 2048
EV_Q_BLK = GATE_COLS // LANES
EV_RW_BLK = EV_Q_BLK + 3 * NA_WIDTH // LANES
OD_A_BLK = GATE_COLS // LANES
OD_HG_BLK = OD_A_BLK + MLA_A_COLS // LANES
VMEM_LIMIT = 48 * 1024 * 1024
NEG_INF = -1e30
LOG2E = math.log2(math.e)


def _cparams(n_axes):
    return pltpu.CompilerParams(dimension_semantics=("arbitrary",) * n_axes, vmem_limit_bytes=VMEM_LIMIT)


def _sigmoid(x):
    return 1.0 / (1.0 + jnp.exp(-x))


def _silu(x):
    return x * _sigmoid(x)


def _dot(a, b):
    return jnp.dot(a.astype(BF16), b.astype(BF16), preferred_element_type=F32)


def _dot_nt(a, b):
    return lax.dot_general(a.astype(BF16), b.astype(BF16), (((1,), (1,)), ((), ())), preferred_element_type=F32)


def _dot_tn(a, b):
    return lax.dot_general(a.astype(BF16), b.astype(BF16), (((0,), (0,)), ((), ())), preferred_element_type=F32)


def _mod_kernel(c_ref, w_ref, b_ref, o_ref):
    o_ref[...] = _dot(_silu(c_ref[...]), w_ref[...]) + b_ref[...]


def modulation(cond, ada_w, ada_b, tn=512):
    depth, d, n = ada_w.shape
    return pl.pallas_call(
        _mod_kernel,
        out_shape=jax.ShapeDtypeStruct((depth, 8, n), F32),
        grid=(depth, n // tn),
        in_specs=[pl.BlockSpec((8, d), lambda l, j: (0, 0)),
                  pl.BlockSpec((None, d, tn), lambda l, j: (l, 0, j)),
                  pl.BlockSpec((None, 1, tn), lambda l, j: (l, 0, j))],
        out_specs=pl.BlockSpec((None, 8, tn), lambda l, j: (l, 0, j)),
        compiler_params=_cparams(2),
        name="adaln_modulation",
    )(cond, ada_w, ada_b.reshape(depth, 1, n))


def _inproj_kernel(x_ref, nw_ref, shift_ref, scale_ref, w_ref, o_ref, h_ref):
    @pl.when(pl.program_id(1) == 0)
    def _():
        x = x_ref[...]
        y = x * lax.rsqrt(jnp.mean(x * x, axis=-1, keepdims=True) + NORM_EPS) * nw_ref[...]
        h_ref[...] = (y * (1.0 + scale_ref[...]) + shift_ref[...]).astype(BF16)

    o_ref[...] = jnp.dot(h_ref[...], w_ref[...], preferred_element_type=F32)


def in_projection(x, norm_w, mod, mod_row, w, name, tm=1024, tn=512):
    m, d = x.shape
    n = w.shape[1]
    tm = min(tm, m)
    return pl.pallas_call(
        _inproj_kernel,
        out_shape=jax.ShapeDtypeStruct((m, n), F32),
        grid=(m // tm, n // tn),
        in_specs=[pl.BlockSpec((tm, d), lambda i, j: (i, 0)),
                  pl.BlockSpec((1, d), lambda i, j: (0, 0)),
                  pl.BlockSpec((None, 1, d), lambda i, j: (mod_row(i), 0, 0)),
                  pl.BlockSpec((None, 1, d), lambda i, j: (mod_row(i), 0, 1)),
                  pl.BlockSpec((d, tn), lambda i, j: (0, j))],
        out_specs=pl.BlockSpec((tm, tn), lambda i, j: (i, j)),
        scratch_shapes=[pltpu.VMEM((tm, d), BF16)],
        compiler_params=_cparams(2),
        name=name,
    )(x, norm_w.reshape(1, d), mod, mod, w)


def _outproj_kernel(ya_ref, yb_ref, g_ref, w_ref, x_ref, gm_ref, *rest, ka, final):
    o_ref = rest[-1]
    sg = _silu(g_ref[...])
    acc = _dot(ya_ref[...].astype(F32) * sg[:, :ka], w_ref[:ka, :])
    acc += _dot(yb_ref[...].astype(F32) * sg[:, ka:], w_ref[ka:, :])
    out = x_ref[...] + gm_ref[...] * acc
    if final:
        out = out * lax.rsqrt(jnp.mean(out * out, axis=-1, keepdims=True) + NORM_EPS) * rest[0][...]
    o_ref[...] = out


def out_projection(ya, yb, gate, w, x, mod, mod_row, name, final_w=None, tm=256):
    m, d = x.shape
    ka, kb = ya.shape[1], yb.shape[1]
    tm = min(tm, m)
    ins = [ya, yb, gate, w, x, mod]
    specs = [pl.BlockSpec((tm, ka), lambda i: (i, 0)),
             pl.BlockSpec((tm, kb), lambda i: (i, 0)),
             pl.BlockSpec((tm, ka + kb), lambda i: (i, 0)),
             pl.BlockSpec((ka + kb, d), lambda i: (0, 0)),
             pl.BlockSpec((tm, d), lambda i: (i, 0)),
             pl.BlockSpec((None, 1, d), lambda i: (mod_row(i), 0, 2))]
    if final_w is not None:
        ins.append(final_w.reshape(1, d))
        specs.append(pl.BlockSpec((1, d), lambda i: (0, 0)))
    return pl.pallas_call(
        functools.partial(_outproj_kernel, ka=ka, final=final_w is not None),
        out_shape=jax.ShapeDtypeStruct((m, d), F32),
        grid=(m // tm,),
        in_specs=specs,
        out_specs=pl.BlockSpec((tm, d), lambda i: (i, 0)),
        compiler_params=_cparams(1),
        name=name,
    )(*ins)


def _softmax_pv(q, parts, scale):
    c = scale * LOG2E
    m = denom = num = None
    for k, bias, v in parts:
        x = _dot_nt(q, k) * c
        if bias is not None:
            x = x + bias
        m_part = jnp.max(x, axis=-1, keepdims=True)
        if m is None:
            m = m_part
            p = jnp.exp2(x - m)
            denom = jnp.sum(p, axis=-1, keepdims=True)
            num = _dot(p, v)
        else:
            m_new = jnp.maximum(m, m_part)
            alpha = jnp.exp2(m - m_new)
            p = jnp.exp2(x - m_new)
            denom = denom * alpha + jnp.sum(p, axis=-1, keepdims=True)
            num = num * alpha + _dot(p, v)
            m = m_new
    return num / denom


def _softmax_pv_joint(q, parts, scale):
    c = scale * LOG2E
    logits = [_dot_nt(q, k) * c if bias is None else _dot_nt(q, k) * c + bias for k, bias, _ in parts]
    m = functools.reduce(jnp.maximum, [jnp.max(x, axis=-1, keepdims=True) for x in logits])
    ps = [jnp.exp2(x - m) for x in logits]
    denom = functools.reduce(lambda a, b: a + b, [jnp.sum(p, axis=-1, keepdims=True) for p in ps])
    num = functools.reduce(lambda a, b: a + b, [_dot(p, v) for p, (_, _, v) in zip(ps, parts)])
    return num / denom


def _na_kernel(idx_ref, q_ref, k_ref, v_ref, kc_ref, vc_ref, tab_ref, o_ref, bias_ref, *, rows, scale):
    j = pl.program_id(2)
    npair = NA_KROWS // 2
    for qr in range(NA_QROWS):
        for m in range(npair):
            tile = tab_ref[idx_ref[(j * NA_QROWS + qr) * npair + m]]
            bias_ref[qr * GRID_W:(qr + 1) * GRID_W, m * LANES:(m + 1) * LANES] = tile

    start = pl.multiple_of(jnp.clip(NA_QROWS * j - NA_WIN_ROWS // 2, 0, rows - NA_KROWS) * GRID_W, GRID_W)
    win = pl.ds(start, NA_KROWS * GRID_W)
    parts = [(k_ref[win, :], bias_ref[...], v_ref[win, :]), (kc_ref[...], None, vc_ref[...])]
    o_ref[...] = _softmax_pv_joint(q_ref[...], parts, scale)


def _na_tile_ids(rows):
    kh = min(NA_WIN_ROWS, rows)
    n = 2 * NA_WIN_ROWS - 1
    ids = []
    for j in range(rows // NA_QROWS):
        ks = int(np.clip(NA_QROWS * j - NA_WIN_ROWS // 2, 0, rows - NA_KROWS))
        for qr in range(NA_QROWS):
            r = NA_QROWS * j + qr
            r0 = int(np.clip(r - kh // 2, 0, rows - kh))
            for m in range(NA_KROWS // 2):
                kr = ks + 2 * m
                dy = kr - r + NA_WIN_ROWS - 1
                first, second = r0 <= kr < r0 + kh, r0 <= kr + 1 < r0 + kh
                ids.append(dy if first and second else n + dy if first else 2 * n + dy + 1 if second else 3 * n)
    return np.array(ids, np.int32)


def _na_bias_tiles(rpb):
    col = np.arange(GRID_W)
    cs = np.clip(col - NA_WIN_COLS // 2, 0, GRID_W - NA_WIN_COLS)
    kc = col[None, :]
    in_win = (kc >= cs[:, None]) & (kc < cs[:, None] + NA_WIN_COLS)
    dx = np.where(in_win, kc - col[:, None] + NA_WIN_COLS - 1, 0)
    tab = jnp.where(in_win[None, None], rpb[:, :, dx] * LOG2E, NEG_INF)
    neg = jnp.full_like(tab[:, :1], NEG_INF)
    nxt = jnp.concatenate([tab[:, 1:], neg], axis=1)
    both = jnp.concatenate([tab, nxt], axis=-1)
    first = jnp.concatenate([tab, jnp.broadcast_to(neg, tab.shape)], axis=-1)
    second = jnp.concatenate([jnp.broadcast_to(neg, tab.shape), tab], axis=-1)
    return jnp.concatenate([both, first, second, jnp.concatenate([neg, neg], axis=-1)], axis=1)


def na_attention(z, z_c, rpb, batch, t, l):
    rows = t // GRID_W
    nj = rows // NA_QROWS
    tq = NA_QROWS * GRID_W
    ids = jnp.asarray(_na_tile_ids(rows))
    tiles = _na_bias_tiles(rpb)
    h8 = NA_HEADS
    return pl.pallas_call(
        functools.partial(_na_kernel, rows=rows, scale=NA_HEAD_DIM ** -0.5),
        out_shape=jax.ShapeDtypeStruct((batch * t, NA_WIDTH), F32),
        grid_spec=pltpu.PrefetchScalarGridSpec(
            num_scalar_prefetch=1,
            grid=(h8, batch, nj),
            in_specs=[pl.BlockSpec((tq, LANES), lambda h, b, j, g: (b * nj + j, EV_Q_BLK + h)),
                      pl.BlockSpec((t, LANES), lambda h, b, j, g: (b, EV_Q_BLK + h8 + h)),
                      pl.BlockSpec((t, LANES), lambda h, b, j, g: (b, EV_Q_BLK + 2 * h8 + h)),
                      pl.BlockSpec((l, LANES), lambda h, b, j, g: (b, EV_Q_BLK + h8 + h)),
                      pl.BlockSpec((l, LANES), lambda h, b, j, g: (b, EV_Q_BLK + 2 * h8 + h)),
                      pl.BlockSpec((None,) + tiles.shape[1:], lambda h, b, j, g: (h, 0, 0, 0))],
            out_specs=pl.BlockSpec((tq, LANES), lambda h, b, j, g: (b * nj + j, h)),
            scratch_shapes=[pltpu.VMEM((tq, NA_KROWS * GRID_W), F32)]),
        compiler_params=_cparams(3),
        name="na_attention",
    )(ids, z, z, z, z_c, z_c, tiles)


def _dense_attn_kernel(q_ref, k_ref, v_ref, o_ref, *, scale):
    o_ref[...] = _softmax_pv(q_ref[...], [(k_ref[...], None, v_ref[...])], scale)


def ctx_attention(z_c, batch, l):
    h8 = NA_HEADS
    return pl.pallas_call(
        functools.partial(_dense_attn_kernel, scale=NA_HEAD_DIM ** -0.5),
        out_shape=jax.ShapeDtypeStruct((batch * l, NA_WIDTH), F32),
        grid=(batch, h8),
        in_specs=[pl.BlockSpec((l, LANES), lambda b, h: (b, EV_Q_BLK + h)),
                  pl.BlockSpec((l, LANES), lambda b, h: (b, EV_Q_BLK + h8 + h)),
                  pl.BlockSpec((l, LANES), lambda b, h: (b, EV_Q_BLK + 2 * h8 + h))],
        out_specs=pl.BlockSpec((l, LANES), lambda b, h: (b, h)),
        compiler_params=_cparams(2),
        name="ctx_attention",
    )(z_c, z_c, z_c)


def _head_sum(x):
    first = lax.broadcasted_iota(jnp.int32, x.shape, 1) < RW_HEAD_DIM
    s0 = jnp.sum(jnp.where(first, x, 0.0), axis=-1, keepdims=True)
    s1 = jnp.sum(jnp.where(first, 0.0, x), axis=-1, keepdims=True)
    return jnp.where(first, s0, s1)


def _softplus(x):
    return jnp.maximum(x, 0.0) + jnp.log1p(jnp.exp(-jnp.abs(x)))


def _rw_terms_kernel(ur_ref, uk_ref, uv_ref, uwa_ref, pr_ref, pk_ref, pv_ref, pwa_ref, nr_ref, nk_ref, nv_ref,
                     nwa_ref, mu_ref, w0_ref, w2_ref, a0_ref, a2_ref, kk_ref, ka_ref, rk_ref,
                     rv_o, wk0_o, wk1_o, bk0_o, bk1_o, bonus_o, *, nt):
    i = pl.program_id(1)
    tm = ur_ref.shape[0]

    def shifted(u_ref, p_ref, n_ref, col0):
        u = u_ref[...]
        mu = mu_ref[:, col0:col0 + u.shape[1]]
        row = lax.broadcasted_iota(jnp.int32, u.shape, 0)
        before = jnp.where(i == 0, 0.0, p_ref[7:8, :])
        after = jnp.where(i == nt - 1, 0.0, n_ref[0:1, :])
        prev = jnp.where(row == 0, before, pltpu.roll(u, 1, axis=0))
        nxt = jnp.where(row == tm - 1, after, pltpu.roll(u, tm - 1, axis=0))
        return u + mu[0:1] * (prev - u) + mu[1:2] * (nxt - u)

    r_all = shifted(ur_ref, pr_ref, nr_ref, 0)
    k_all = shifted(uk_ref, pk_ref, nk_ref, RW_WIDTH)
    v_all = shifted(uv_ref, pv_ref, nv_ref, 2 * RW_WIDTH)
    wa = shifted(uwa_ref, pwa_ref, nwa_ref, 3 * RW_WIDTH)
    wd = jnp.tanh(wa[:, :LANES]).astype(BF16)
    ad = wa[:, LANES:].astype(BF16)
    first = lax.broadcasted_iota(jnp.int32, (tm, LANES), 1) < RW_HEAD_DIM

    def head_rows(x, y):
        return (jnp.where(first, x, pltpu.roll(y, RW_HEAD_DIM, axis=1)),
                jnp.where(first, pltpu.roll(x, RW_HEAD_DIM, axis=1), y))

    outs = (rv_o, wk0_o, wk1_o, bk0_o, bk1_o)
    even = [[] for _ in outs]
    odd = [[] for _ in outs]
    bonus = []
    for p in range(RW_WIDTH // LANES):
        sl = slice(p * LANES, (p + 1) * LANES)
        r, k, v = r_all[:, sl], k_all[:, sl], v_all[:, sl]
        kk = k * kk_ref[:, sl]
        kk = kk / jnp.maximum(jnp.sqrt(_head_sum(kk * kk)), 1e-12)
        kd_sum = jnp.zeros_like(k)
        per_dir = []
        for d in range(2):
            w = w0_ref[d:d + 1, sl] + jnp.dot(wd, w2_ref[d, :, sl].astype(BF16), preferred_element_type=F32)
            a = _sigmoid(a0_ref[d:d + 1, sl] + jnp.dot(ad, a2_ref[d, :, sl].astype(BF16), preferred_element_type=F32))
            kd = k * (1.0 + (a - 1.0) * ka_ref[:, sl])
            per_dir.append(((jnp.exp(-jnp.exp(-_softplus(-w) - 0.5)), kd), (kk * a, kk)))
            kd_sum = kd_sum + kd
        pairs = [(r, v), per_dir[0][0], per_dir[1][0], per_dir[0][1], per_dir[1][1]]
        bonus.append(_head_sum(r * kd_sum * rk_ref[:, sl]) * v)
        for ev, od, (x, y) in zip(even, odd, pairs):
            e, o = head_rows(x, y)
            ev.append(e)
            od.append(o)
    for o_ref, ev, od in zip(outs, even, odd):
        o_ref[...] = jnp.swapaxes(jnp.stack(ev + od, axis=0), 0, 1)
    bonus_o[...] = jnp.swapaxes(jnp.stack(bonus, axis=0), 0, 1)


def rwkv_terms(z, n_seq, t, mu, w0, w2p, a0, a2p, k_k, k_a, r_k, tm=256):
    tm = min(tm, t)
    nt = t // tm
    npair = RW_WIDTH // LANES
    wide0 = EV_RW_BLK * LANES // RW_WIDTH
    wa_blk = (EV_RW_BLK + 3 * npair) // 2
    n8 = tm // 8

    def cur(width, cb):
        return pl.BlockSpec((tm, width), lambda b, i: (b * nt + i, cb))

    def before(width, cb):
        return pl.BlockSpec((8, width), lambda b, i: (jnp.maximum((b * nt + i) * n8 - 1, 0), cb))

    def after(width, cb):
        return pl.BlockSpec((8, width), lambda b, i: (jnp.minimum((b * nt + i + 1) * n8, n_seq * nt * n8 - 1), cb))

    cols = [(RW_WIDTH, wide0), (RW_WIDTH, wide0 + 1), (RW_WIDTH, wide0 + 2), (2 * LANES, wa_blk)]
    whole = lambda a: pl.BlockSpec(a.shape, lambda b, i: (0,) * a.ndim)
    params = (mu, w0, w2p, a0, a2p, k_k, k_a, r_k)
    out = pl.BlockSpec((tm, npair, LANES), lambda b, i: (i, b, 0))
    out2 = pl.BlockSpec((tm, 2 * npair, LANES), lambda b, i: (i, b, 0))
    return pl.pallas_call(
        functools.partial(_rw_terms_kernel, nt=nt),
        out_shape=[jax.ShapeDtypeStruct((t, n_seq * 2 * npair, LANES), F32)] * 5
        + [jax.ShapeDtypeStruct((t, n_seq * npair, LANES), F32)],
        grid=(n_seq, nt),
        in_specs=([cur(*c) for c in cols] + [before(*c) for c in cols] + [after(*c) for c in cols]
                  + [whole(a) for a in params]),
        out_specs=[out2] * 5 + [out],
        compiler_params=_cparams(2),
        name="rwkv_terms",
    )(*([z] * 12), *params)


def _to_lanes(n):
    q = jnp.concatenate([n, pltpu.roll(n, RW_HEAD_DIM, axis=1)], axis=0)
    return q.T[:RW_HEAD_DIM]


def _rw_scan_kernel(rvf, rvb, wk0, wk1, bk0, bk1, s0_ref, yf_ref, yb_ref, sfin_ref,
                    s_ref, z_ref, znext_ref, g_ref, ybuf_ref, *, chunk):
    n = RW_HEAD_DIM
    nb = rvf.shape[1] // (2 * RW_WIDTH // LANES)
    R, K, V, A, B = range(5)

    @pl.when(pl.program_id(0) == 0)
    def _():
        s_ref[...] = s0_ref[...]

    def convert(t_src):
        def pair_tiles(f, bw):
            both = (f[t_src], bw[chunk - 1 - t_src])
            x = jnp.concatenate([src[(b * 2 + par) * 8:(b * 2 + par) * 8 + 8]
                                 for par in range(2) for src in both for b in range(nb)], axis=0)
            xt = x.T
            return xt[:n], xt[n:]

        r, v = pair_tiles(rvf, rvb)
        w, k = pair_tiles(wk0, wk1)
        b, kk = pair_tiles(bk0, bk1)
        g_prev = g_ref[0]
        g = g_prev * w
        g_ref[0] = g
        inv = 1.0 / g
        znext_ref[R] = r * g
        znext_ref[K] = k * inv
        znext_ref[V] = v
        znext_ref[A] = -(kk * g_prev)
        znext_ref[B] = b * inv

    def emit_y(t):
        rows = _to_lanes(ybuf_ref[...])
        yf_ref[t] = rows[:n // 2]
        yb_ref[chunk - 1 - t] = rows[n // 2:]

    g_ref[0] = jnp.ones((n, LANES), F32)
    ybuf_ref[...] = jnp.zeros((n, LANES), F32)
    convert(0)

    def step(t, carry):
        emit_y(jnp.maximum(t - 1, 0))
        z_ref[...] = znext_ref[...]
        g_ref[1] = g_ref[0]
        convert(jnp.minimum(t + 1, chunk - 1))
        sa = [jnp.zeros((n, LANES), F32), jnp.zeros((n, LANES), F32)]
        for k in range(n):
            sa[k % 2] = sa[k % 2] + s_ref[k] * z_ref[A, k:k + 1, :]
        sa = sa[0] + sa[1]
        vt = z_ref[V]
        y = [jnp.zeros((n, LANES), F32), jnp.zeros((n, LANES), F32)]
        for k in range(n):
            sk = s_ref[k] + sa * z_ref[B, k:k + 1, :] + vt * z_ref[K, k:k + 1, :]
            s_ref[k] = sk
            y[k % 2] = y[k % 2] + sk * z_ref[R, k:k + 1, :]
        ybuf_ref[...] = y[0] + y[1]
        return carry

    lax.fori_loop(0, chunk, step, 0)
    emit_y(chunk - 1)
    for k in range(n):
        s_ref[k] = s_ref[k] * g_ref[1, k:k + 1, :]
    sfin_ref[...] = s_ref[...]


def rwkv_scan(rv, wk0, wk1, bk0, bk1, s0):
    t, nrow, _ = rv.shape
    n = RW_HEAD_DIM
    chunk = RW_SCAN_CHUNK
    nc = t // chunk
    assert nrow == LANES // 2, "the states of both directions fill the 128 lanes"
    fwd = pl.BlockSpec((chunk, nrow, LANES), lambda i: (i, 0, 0))
    bwd = pl.BlockSpec((chunk, nrow, LANES), lambda i: (nc - 1 - i, 0, 0))
    yfwd = pl.BlockSpec((chunk, nrow // 2, LANES), lambda i: (i, 0, 0))
    ybwd = pl.BlockSpec((chunk, nrow // 2, LANES), lambda i: (nc - 1 - i, 0, 0))
    state = pl.BlockSpec((n, n, LANES), lambda i: (0, 0, 0))
    slab = jax.ShapeDtypeStruct((t, nrow // 2, LANES), F32)
    return pl.pallas_call(
        functools.partial(_rw_scan_kernel, chunk=chunk),
        out_shape=[slab, slab, jax.ShapeDtypeStruct((n, n, LANES), F32)],
        grid=(nc,),
        in_specs=[fwd, bwd, fwd, bwd, fwd, bwd, state],
        out_specs=[yfwd, ybwd, state],
        scratch_shapes=[pltpu.VMEM((n, n, LANES), F32), pltpu.VMEM((5, n, LANES), F32),
                        pltpu.VMEM((5, n, LANES), F32), pltpu.VMEM((2, n, LANES), F32),
                        pltpu.VMEM((n, LANES), F32)],
        compiler_params=_cparams(1),
        name="rwkv_scan",
    )(rv, rv, wk0, wk1, bk0, bk1, s0)


def _rw_readout_kernel(yf_ref, yb_ref, bonus_ref, lw_ref, lb_ref, o_ref):
    npair = RW_WIDTH // LANES
    y_all = jnp.swapaxes(yf_ref[...] + yb_ref[...], 0, 1)
    bonus_all = jnp.swapaxes(bonus_ref[...], 0, 1)
    for q in range(yf_ref.shape[1]):
        b, p = divmod(q, npair)
        sl = slice(p * LANES, (p + 1) * LANES)
        y = y_all[q]
        mu = _head_sum(y) * (1.0 / RW_HEAD_DIM)
        d = y - mu
        var = _head_sum(d * d) * (1.0 / RW_HEAD_DIM)
        o_ref[b, :, sl] = d * lax.rsqrt(var + RW_GN_EPS) * lw_ref[:, sl] + lb_ref[:, sl] + bonus_all[q]


def rwkv_readout(y_f, y_b, bonus, ln_w, ln_b, tm=256):
    t, nq, _ = y_f.shape
    tm = min(tm, t)
    batch = nq // (RW_WIDTH // LANES)
    slab = pl.BlockSpec((tm, nq, LANES), lambda i: (i, 0, 0))
    vec = pl.BlockSpec((1, RW_WIDTH), lambda i: (0, 0))
    return pl.pallas_call(
        _rw_readout_kernel,
        out_shape=jax.ShapeDtypeStruct((batch, t, RW_WIDTH), F32),
        grid=(t // tm,),
        in_specs=[slab, slab, slab, vec, vec],
        out_specs=pl.BlockSpec((batch, tm, RW_WIDTH), lambda i: (0, i, 0)),
        compiler_params=_cparams(1),
        name="rwkv_readout",
    )(y_f, y_b, bonus, ln_w.reshape(1, -1), ln_b.reshape(1, -1))


def _rms(x, w):
    return x * lax.rsqrt(jnp.mean(x * x, axis=-1, keepdims=True) + NORM_EPS) * w


def _rope(x, cos, sin):
    odd = (lax.broadcasted_iota(jnp.int32, x.shape, 1) & 1) == 1
    swapped = jnp.where(odd, pltpu.roll(x, 1, axis=1), pltpu.roll(x, LANES - 1, axis=1))
    return x * cos + swapped * sin


def _mla_q_kernel(c_ref, nw_ref, w_ref, cos_ref, sin_ref, q_ref, h_ref):
    @pl.when(pl.program_id(1) == 0)
    def _():
        h_ref[...] = _rms(c_ref[...], nw_ref[...]).astype(BF16)

    acc = jnp.dot(h_ref[...], w_ref[...], preferred_element_type=F32)
    q_ref[:, :MLA_NOPE] = acc[:, :MLA_NOPE].astype(BF16)
    q_ref[:, MLA_NOPE:] = _rope(acc[:, MLA_NOPE:], cos_ref[...], sin_ref[...]).astype(BF16)


def mla_queries(z, q_norm, w_uq_pad, cos, sin, t, tm=512):
    m = z.shape[0]
    tm = min(tm, t)
    nt = t // tm
    return pl.pallas_call(
        _mla_q_kernel,
        out_shape=jax.ShapeDtypeStruct((m, MLA_HEADS * MLA_QK_PAD), BF16),
        grid=(m // tm, MLA_HEADS),
        in_specs=[pl.BlockSpec((tm, MLA_Q_RANK), lambda i, h: (i, GATE_COLS // MLA_Q_RANK)),
                  pl.BlockSpec((1, MLA_Q_RANK), lambda i, h: (0, 0)),
                  pl.BlockSpec((MLA_Q_RANK, MLA_QK_PAD), lambda i, h: (0, h)),
                  pl.BlockSpec((tm, LANES), lambda i, h: (i % nt, 0)),
                  pl.BlockSpec((tm, LANES), lambda i, h: (i % nt, 0))],
        out_specs=pl.BlockSpec((tm, MLA_QK_PAD), lambda i, h: (i, h)),
        scratch_shapes=[pltpu.VMEM((tm, MLA_Q_RANK), BF16)],
        compiler_params=_cparams(2),
        name="mla_queries",
    )(z, q_norm.reshape(1, -1), w_uq_pad, cos, sin)


def _mla_kv_kernel(c_ref, pe_ref, nw_ref, w_ref, cos_ref, sin_ref, k_ref, v_ref, h_ref):
    @pl.when(pl.program_id(1) == 0)
    def _():
        h_ref[...] = _rms(c_ref[...], nw_ref[...]).astype(BF16)

    acc = jnp.dot(h_ref[...], w_ref[...], preferred_element_type=F32)
    k_ref[:, :MLA_NOPE] = acc[:, :MLA_NOPE].astype(BF16)
    k_ref[:, MLA_NOPE:] = _rope(pe_ref[...], cos_ref[...], sin_ref[...]).astype(BF16)
    v_ref[...] = acc[:, MLA_NOPE:].astype(BF16)


def mla_keys_values(z, kv_norm, w_ukv, cos, sin, t, tm=512):
    m = z.shape[0]
    tm = min(tm, t)
    nt = t // tm
    return pl.pallas_call(
        _mla_kv_kernel,
        out_shape=[jax.ShapeDtypeStruct((m, MLA_HEADS * MLA_QK_PAD), BF16),
                   jax.ShapeDtypeStruct((m, MLA_WIDTH), BF16)],
        grid=(m // tm, MLA_HEADS),
        in_specs=[pl.BlockSpec((tm, MLA_KV_RANK), lambda i, h: (i, (GATE_COLS + MLA_Q_RANK) // MLA_KV_RANK)),
                  pl.BlockSpec((tm, LANES), lambda i, h: (i, OD_A_BLK + (MLA_Q_RANK + MLA_KV_RANK) // LANES)),
                  pl.BlockSpec((1, MLA_KV_RANK), lambda i, h: (0, 0)),
                  pl.BlockSpec((MLA_KV_RANK, MLA_NOPE + MLA_V), lambda i, h: (0, h)),
                  pl.BlockSpec((tm, LANES), lambda i, h: (i % nt, 0)),
                  pl.BlockSpec((tm, LANES), lambda i, h: (i % nt, 0))],
        out_specs=[pl.BlockSpec((tm, MLA_QK_PAD), lambda i, h: (i, h)),
                   pl.BlockSpec((tm, MLA_V), lambda i, h: (i, h))],
        scratch_shapes=[pltpu.VMEM((tm, MLA_KV_RANK), BF16)],
        compiler_params=_cparams(2),
        name="mla_keys_values",
    )(z, z, kv_norm.reshape(1, -1), w_ukv, cos, sin)


def _mla_attn_kernel(q_ref, k_ref, v_ref, kc_ref, vc_ref, o_ref, *, scale):
    parts = [(kc_ref[...], None, vc_ref[...])]
    nkeys = k_ref.shape[0]
    step = min(MLA_KEY_CHUNK, nkeys)
    for lo in range(0, nkeys, step):
        parts.append((k_ref[lo:lo + step, :], None, v_ref[lo:lo + step, :]))
    o_ref[...] = _softmax_pv(q_ref[...], parts, scale)


def mla_attention(q, k, v, kc, vc, batch, t, l, tq=512):
    tq = min(tq, t)
    nq = t // tq
    return pl.pallas_call(
        functools.partial(_mla_attn_kernel, scale=(MLA_NOPE + MLA_ROPE) ** -0.5),
        out_shape=jax.ShapeDtypeStruct((batch * t, MLA_WIDTH), F32),
        grid=(batch, MLA_HEADS, nq),
        in_specs=[pl.BlockSpec((tq, MLA_QK_PAD), lambda b, h, i: (b * nq + i, h)),
                  pl.BlockSpec((t, MLA_QK_PAD), lambda b, h, i: (b, h)),
                  pl.BlockSpec((t, MLA_V), lambda b, h, i: (b, h)),
                  pl.BlockSpec((l, MLA_QK_PAD), lambda b, h, i: (b, h)),
                  pl.BlockSpec((l, MLA_V), lambda b, h, i: (b, h))],
        out_specs=pl.BlockSpec((tq, MLA_V), lambda b, h, i: (b * nq + i, h)),
        compiler_params=_cparams(3),
        name="mla_attention",
    )(q, k, v, kc, vc)


def _rope_tables(t):
    tok = np.arange(t)
    pos = np.stack([tok // GRID_W, tok % GRID_W], axis=-1).astype(np.float32)
    n_freq = MLA_ROPE // 4
    inv = (ROPE_BASE ** (-jnp.arange(n_freq, dtype=F32) / n_freq))
    ang = (jnp.asarray(pos)[:, :, None] * inv).reshape(t, MLA_ROPE // 2)
    cos = jnp.repeat(jnp.cos(ang), 2, axis=-1)
    sin = jnp.repeat(jnp.sin(ang), 2, axis=-1) * jnp.tile(jnp.array([-1.0, 1.0], F32), MLA_ROPE // 2)
    pad = ((0, 0), (0, LANES - MLA_ROPE))
    return jnp.pad(cos, pad), jnp.pad(sin, pad)


def _split3(x):
    hi = x.astype(BF16)
    r1 = x - hi.astype(F32)
    mid = r1.astype(BF16)
    lo = (r1 - mid.astype(F32)).astype(BF16)
    return hi, mid, lo


def _hg_kernel(q_ref, f_ref, v_ref, lb_ref, s0_ref, *rest, reverse, finalize, emit_state):
    st_ref = rest[-1]
    rest = rest[:-1]
    if finalize:
        prev_ref, nw_ref = rest[0], rest[1]
        rest = rest[2:]
    o_ref = rest[0]

    @pl.when(pl.program_id(2) == 0)
    def _():
        st_ref[...] = s0_ref[...]

    bt = q_ref.shape[0]
    nchunk = bt // HG_CHUNK
    order = range(nchunk - 1, -1, -1) if reverse else range(nchunk)
    ri = lax.broadcasted_iota(jnp.int32, (bt, bt), 0)
    ci = lax.broadcasted_iota(jnp.int32, (bt, bt), 1)
    same = (ri // HG_CHUNK) == (ci // HG_CHUNK)
    tri = same & ((ci >= ri) if reverse else (ci <= ri))
    tri_f = tri.astype(F32)
    tri_b = tri.astype(BF16)

    lb = lb_ref[...]
    fz = f_ref[...]
    log_sig = jnp.minimum(fz, 0.0) - jnp.log1p(jnp.exp(-jnp.abs(fz)))
    la = jnp.log(lb)
    lbb = jnp.log1p(-lb) + log_sig
    log_f = jnp.maximum(la, lbb) + jnp.log1p(jnp.exp(-jnp.abs(la - lbb)))
    kf_all = (1.0 - lb) * _sigmoid(-fz)
    pieces = jnp.dot(tri_b, jnp.concatenate(_split3(log_f), axis=1), preferred_element_type=F32)
    width = log_f.shape[1]
    cum_all = pieces[:, :width] + pieces[:, width:2 * width] + pieces[:, 2 * width:]

    for hh in range(HG_STEP_HEADS):
        sl = slice(hh * HG_DK, (hh + 1) * HG_DK)
        cum, kf, v = cum_all[:, sl], kf_all[:, sl], v_ref[:, sl]
        last = [cum[c * HG_CHUNK:c * HG_CHUNK + 1] if reverse else cum[(c + 1) * HG_CHUNK - 1:(c + 1) * HG_CHUNK]
                for c in range(nchunk)]
        tot = jnp.concatenate([jnp.broadcast_to(x, (HG_CHUNK, HG_DK)) for x in last], axis=0)
        q_in = q_ref[:, sl] * jnp.exp(cum)
        k_in = kf * jnp.exp(-cum)
        k_end = kf * jnp.exp(tot - cum)
        o_intra = _dot(_dot_nt(q_in, k_in) * tri_f, v)
        st = st_ref[hh]
        for c in order:
            rows = slice(c * HG_CHUNK, (c + 1) * HG_CHUNK)
            o_c = o_intra[rows] + _dot_nt(q_in[rows], st)
            if finalize:
                o_c = o_c + prev_ref[rows, sl]
                o_c = o_c * lax.rsqrt(jnp.mean(o_c * o_c, axis=-1, keepdims=True) + NORM_EPS) * nw_ref[...]
            o_ref[rows, sl] = o_c
            st = st * jnp.exp(last[c]) + _dot_tn(v[rows], k_end[rows])
        st_ref[hh] = st
        if emit_state:
            rest[1][hh] = st


def hgrn2_pass(z, lb, s0, batch, t, f_off, reverse, prev=None, norm_w=None, emit_state=False):
    bt = min(HG_BLOCK, t)
    nblk = t // bt
    nh = HG_STEP_HEADS
    wide = nh * LANES
    finalize = prev is not None
    assert OD_HG_BLK % nh == 0 and HG_HEADS % nh == 0, "head groups must start on a whole column block"
    pos = (lambda i: nblk - 1 - i) if reverse else (lambda i: i)
    blk = lambda off: pl.BlockSpec((bt, wide), lambda b, h, i, off=off: (b * nblk + pos(i), off // nh + h))
    state = pl.BlockSpec((None, nh, HG_DV, HG_DK), lambda b, h, i: (b, h, 0, 0))
    ins = [z, z, z, lb.reshape(1, -1), s0]
    specs = [blk(OD_HG_BLK), blk(OD_HG_BLK + f_off), blk(OD_HG_BLK + 3 * HG_HEADS),
             pl.BlockSpec((1, wide), lambda b, h, i: (0, h)), state]
    if finalize:
        ins += [prev, norm_w.reshape(1, -1)]
        specs += [blk(0), pl.BlockSpec((1, HG_DV), lambda b, h, i: (0, 0))]
    out_shape = [jax.ShapeDtypeStruct((batch * t, HG_WIDTH), F32)]
    out_specs = [blk(0)]
    if emit_state:
        out_shape.append(jax.ShapeDtypeStruct((batch, HG_HEADS, HG_DV, HG_DK), F32))
        out_specs.append(state)
    return pl.pallas_call(
        functools.partial(_hg_kernel, reverse=reverse, finalize=finalize, emit_state=emit_state),
        out_shape=out_shape,
        grid=(batch, HG_HEADS // nh, nblk),
        in_specs=specs,
        out_specs=out_specs,
        scratch_shapes=[pltpu.VMEM((nh, HG_DV, HG_DK), F32)],
        compiler_params=_cparams(3),
        name="hgrn2_" + ("bwd" if reverse else "fwd"),
    )(*ins)


def _even_layer(x, xc, mod, norm_w, w_in, w_out, rpb, rw_params, batch, t, l):
    d = x.shape[1]
    s2 = 3 * NA_WIDTH + RW_SHIFT_COLS
    zero = jnp.zeros((d, RW_PAD_COLS - RW_SHIFT_COLS), BF16)
    w_in = w_in.astype(BF16)
    w_ext = jnp.concatenate([w_in[:, s2:], w_in[:, :s2], zero], axis=1)
    tiles_per_seq = max(t // 1024, 1)
    z = in_projection(x, norm_w, mod, lambda i: i // tiles_per_seq, w_ext, "even_in_proj")
    z_c = in_projection(xc, norm_w, mod, lambda i: batch, w_ext, "even_in_proj_ctx")

    y_na = na_attention(z, z_c, rpb, batch, t, l)
    yc_na = ctx_attention(z_c, batch, l)

    mu, w0, w2, a0, a2, k_k, k_a, r_k, ln_w, ln_b = rw_params
    mu = jnp.pad(mu, ((0, 0), (0, RW_PAD_COLS - RW_SHIFT_COLS)))
    zpad = jnp.zeros((RW_RANK, RW_WIDTH), F32)
    w2p = jnp.stack([jnp.concatenate([w2[0], zpad]), jnp.concatenate([zpad, w2[1]])])
    a2p = jnp.stack([jnp.concatenate([a2[0], zpad]), jnp.concatenate([zpad, a2[1]])])
    vecs = (mu, w0, w2p, a0, a2p, k_k.reshape(1, -1), k_a.reshape(1, -1), r_k.reshape(1, -1))
    terms = rwkv_terms(z, batch, t, *vecs)
    terms_c = rwkv_terms(z_c, batch, l, *vecs)
    s_zero = jnp.zeros((RW_HEAD_DIM, RW_HEAD_DIM, LANES), F32)
    yc_f, yc_b, s_ctx = rwkv_scan(*terms_c[:5], s_zero)
    y_f, y_b, _ = rwkv_scan(*terms[:5], s_ctx)
    y_rw = rwkv_readout(y_f, y_b, terms[5], ln_w, ln_b).reshape(batch * t, RW_WIDTH)
    yc_rw = rwkv_readout(yc_f, yc_b, terms_c[5], ln_w, ln_b).reshape(batch * l, RW_WIDTH)

    w_out = w_out.astype(BF16)
    rows_per_seq = max(t // 256, 1)
    x = out_projection(y_na, y_rw, z, w_out, x, mod, lambda i: i // rows_per_seq, "even_out_proj")
    xc = out_projection(yc_na, yc_rw, z_c, w_out, xc, mod, lambda i: batch, "even_out_proj_ctx")
    return x, xc


def _odd_layer(x, xc, mod, norm_w, w_in, w_out, q_norm, w_uq, kv_norm, w_ukv, lb, hg_norm_w, final_w,
               batch, t, l):
    d = x.shape[1]
    o3 = MLA_Q_RANK + MLA_KV_RANK + MLA_ROPE
    o4 = o3 + 4 * HG_FDIM
    zero = jnp.zeros((d, MLA_A_COLS - o3), BF16)
    w_in = w_in.astype(BF16)
    w_ext = jnp.concatenate([w_in[:, o4:], w_in[:, :o3], zero, w_in[:, o3:o4]], axis=1)
    tiles_per_seq = max(t // 1024, 1)
    z = in_projection(x, norm_w, mod, lambda i: i // tiles_per_seq, w_ext, "odd_in_proj")
    z_c = in_projection(xc, norm_w, mod, lambda i: batch, w_ext, "odd_in_proj_ctx")

    cos, sin = _rope_tables(t)
    qk = MLA_NOPE + MLA_ROPE
    w_uq_pad = jnp.pad(w_uq.reshape(MLA_Q_RANK, MLA_HEADS, qk), ((0, 0), (0, 0), (0, MLA_QK_PAD - qk)))
    w_uq_pad = w_uq_pad.reshape(MLA_Q_RANK, MLA_HEADS * MLA_QK_PAD).astype(BF16)
    w_ukv = w_ukv.astype(BF16)
    q = mla_queries(z, q_norm, w_uq_pad, cos, sin, t)
    k, v = mla_keys_values(z, kv_norm, w_ukv, cos, sin, t)
    ones = jnp.ones((l, LANES), F32)
    kc, vc = mla_keys_values(z_c, kv_norm, w_ukv, ones, jnp.zeros_like(ones), l)
    y_mla = mla_attention(q, k, v, kc, vc, batch, t, l)

    s_zero = jnp.zeros((batch, HG_HEADS, HG_DV, HG_DK), F32)
    _, s_f = hgrn2_pass(z_c, lb, s_zero, batch, l, HG_HEADS, False, emit_state=True)
    _, s_b = hgrn2_pass(z_c, lb, s_zero, batch, l, 2 * HG_HEADS, True, emit_state=True)
    o_f, = hgrn2_pass(z, lb, s_f, batch, t, HG_HEADS, False)
    y_hg, = hgrn2_pass(z, lb, s_b, batch, t, 2 * HG_HEADS, True, prev=o_f, norm_w=hg_norm_w)

    rows_per_seq = max(t // 256, 1)
    return out_projection(y_mla, y_hg, z, w_out.astype(BF16), x, mod, lambda i: i // rows_per_seq,
                          "odd_out_proj", final_w=final_w)


def kernel(x, c, ctx, c_ctx, ada_w, ada_b, norm_w, e_w_in, e_w_out, na_rpb, rw_mu, rw_w0, rw_w2, rw_a0, rw_a2, rw_k_k, rw_k_a, rw_r_k, rw_ln_w, rw_ln_b, o_w_in, o_w_out, mla_q_norm, mla_w_uq, mla_kv_norm, mla_w_ukv, hg_lower_bounds, hg_norm_w, final_norm_w):
    batch, t, d = x.shape
    l = ctx.shape[1]
    assert ada_w.shape[0] == 2, "one even and one odd layer"
    cond = jnp.concatenate([c, c_ctx[None, :], jnp.zeros((8 - batch - 1, d), F32)], axis=0)
    mod = modulation(cond, ada_w, ada_b)
    s = jax.nn.softmax(hg_lower_bounds.astype(F32), axis=0)
    lower = jnp.cumsum(s, axis=0) - s[0]

    xf, xcf = x.reshape(batch * t, d), ctx.reshape(batch * l, d)
    rw_params = (rw_mu[0], rw_w0[0], rw_w2[0], rw_a0[0], rw_a2[0], rw_k_k[0], rw_k_a[0], rw_r_k[0],
                 rw_ln_w[0], rw_ln_b[0])
    xf, xcf = _even_layer(xf, xcf, mod[0].reshape(8, 1, 3 * d), norm_w[0], e_w_in[0], e_w_out[0], na_rpb[0],
                          rw_params, batch, t, l)
    out = _odd_layer(xf, xcf, mod[1].reshape(8, 1, 3 * d), norm_w[1], o_w_in[0], o_w_out[0], mla_q_norm[0],
                     mla_w_uq[0], mla_kv_norm[0], mla_w_ukv[0], lower[1], hg_norm_w[0], final_norm_w,
                     batch, t, l)
    return out.reshape(batch, t, d)
```

```python
import functools
import math

import numpy as np
import jax
import jax.numpy as jnp
from jax import lax
from jax.experimental import pallas as pl
from jax.experimental.pallas import tpu as pltpu

F32 = jnp.float32
BF16 = jnp.bfloat16

GRID_W = 64
NORM_EPS = 1e-6
ROPE_BASE = 10000.0

NA_HEADS = 8
NA_HEAD_DIM = 128
NA_WIDTH = NA_HEADS * NA_HEAD_DIM
NA_WIN_ROWS = 8
NA_WIN_COLS = 16
NA_QROWS = 4
NA_KROWS = 12

RW_HEAD_DIM = 64
RW_HEADS = 16
RW_WIDTH = RW_HEADS * RW_HEAD_DIM
RW_RANK = 64
RW_SHIFT_COLS = 3 * RW_WIDTH + 4 * RW_RANK
RW_PAD_COLS = 3584
RW_GN_EPS = 64e-5
RW_SCAN_CHUNK = 32

MLA_HEADS = 8
MLA_Q_RANK = 512
MLA_KV_RANK = 512
MLA_NOPE = 128
MLA_ROPE = 64
MLA_V = 128
MLA_QK_PAD = 256
MLA_WIDTH = MLA_HEADS * MLA_V
MLA_A_COLS = 1536
MLA_KEY_CHUNK = 512

HG_HEADS = 8
HG_DK = 128
HG_DV = 128
HG_FDIM = HG_HEADS * HG_DK
HG_WIDTH = HG_HEADS * HG_DV
HG_CHUNK = 32
HG_BLOCK = 256
HG_STEP_HEADS = 4

LANES = 128

GATE_COLS = 2048
EV_Q_BLK = GATE_COLS // LANES
EV_RW_BLK = EV_Q_BLK + 3 * NA_WIDTH // LANES
OD_A_BLK = GATE_COLS // LANES
OD_HG_BLK = OD_A_BLK + MLA_A_COLS // LANES
VMEM_LIMIT = 48 * 1024 * 1024
NEG_INF = -1e30
LOG2E = math.log2(math.e)


def _cparams(n_axes):
    return pltpu.CompilerParams(dimension_semantics=("arbitrary",) * n_axes, vmem_limit_bytes=VMEM_LIMIT)


def _sigmoid(x):
    return 1.0 / (1.0 + jnp.exp(-x))


def _silu(x):
    return x * _sigmoid(x)


def _dot(a, b):
    return jnp.dot(a.astype(BF16), b.astype(BF16), preferred_element_type=F32)


def _dot_nt(a, b):
    return lax.dot_general(a.astype(BF16), b.astype(BF16), (((1,), (1,)), ((), ())), preferred_element_type=F32)


def _dot_tn(a, b):
    return lax.dot_general(a.astype(BF16), b.astype(BF16), (((0,), (0,)), ((), ())), preferred_element_type=F32)


def _mod_kernel(c_ref, w_ref, b_ref, o_ref):
    o_ref[...] = _dot(_silu(c_ref[...]), w_ref[...]) + b_ref[...]


def modulation(cond, ada_w, ada_b, tn=512):
    depth, d, n = ada_w.shape
    return pl.pallas_call(
        _mod_kernel,
        out_shape=jax.ShapeDtypeStruct((depth, 8, n), F32),
        grid=(depth, n // tn),
        in_specs=[pl.BlockSpec((8, d), lambda l, j: (0, 0)),
                  pl.BlockSpec((None, d, tn), lambda l, j: (l, 0, j)),
                  pl.BlockSpec((None, 1, tn), lambda l, j: (l, 0, j))],
        out_specs=pl.BlockSpec((None, 8, tn), lambda l, j: (l, 0, j)),
        compiler_params=_cparams(2),
        name="adaln_modulation",
    )(cond, ada_w, ada_b.reshape(depth, 1, n))


def _inproj_kernel(x_ref, nw_ref, shift_ref, scale_ref, w_ref, o_ref, h_ref):
    @pl.when(pl.program_id(1) == 0)
    def _():
        x = x_ref[...]
        y = x * lax.rsqrt(jnp.mean(x * x, axis=-1, keepdims=True) + NORM_EPS) * nw_ref[...]
        h_ref[...] = (y * (1.0 + scale_ref[...]) + shift_ref[...]).astype(BF16)

    o_ref[...] = jnp.dot(h_ref[...], w_ref[...], preferred_element_type=F32)


def in_projection(x, norm_w, mod, mod_row, w, name, tm=1024, tn=512):
    m, d = x.shape
    n = w.shape[1]
    tm = min(tm, m)
    return pl.pallas_call(
        _inproj_kernel,
        out_shape=jax.ShapeDtypeStruct((m, n), F32),
        grid=(m // tm, n // tn),
        in_specs=[pl.BlockSpec((tm, d), lambda i, j: (i, 0)),
                  pl.BlockSpec((1, d), lambda i, j: (0, 0)),
                  pl.BlockSpec((None, 1, d), lambda i, j: (mod_row(i), 0, 0)),
                  pl.BlockSpec((None, 1, d), lambda i, j: (mod_row(i), 0, 1)),
                  pl.BlockSpec((d, tn), lambda i, j: (0, j))],
        out_specs=pl.BlockSpec((tm, tn), lambda i, j: (i, j)),
        scratch_shapes=[pltpu.VMEM((tm, d), BF16)],
        compiler_params=_cparams(2),
        name=name,
    )(x, norm_w.reshape(1, d), mod, mod, w)


def _outproj_kernel(ya_ref, yb_ref, g_ref, w_ref, x_ref, gm_ref, *rest, ka, final):
    o_ref = rest[-1]
    sg = _silu(g_ref[...])
    acc = _dot(ya_ref[...].astype(F32) * sg[:, :ka], w_ref[:ka, :])
    acc += _dot(yb_ref[...].astype(F32) * sg[:, ka:], w_ref[ka:, :])
    out = x_ref[...] + gm_ref[...] * acc
    if final:
        out = out * lax.rsqrt(jnp.mean(out * out, axis=-1, keepdims=True) + NORM_EPS) * rest[0][...]
    o_ref[...] = out


def out_projection(ya, yb, gate, w, x, mod, mod_row, name, final_w=None, tm=256):
    m, d = x.shape
    ka, kb = ya.shape[1], yb.shape[1]
    tm = min(tm, m)
    ins = [ya, yb, gate, w, x, mod]
    specs = [pl.BlockSpec((tm, ka), lambda i: (i, 0)),
             pl.BlockSpec((tm, kb), lambda i: (i, 0)),
             pl.BlockSpec((tm, ka + kb), lambda i: (i, 0)),
             pl.BlockSpec((ka + kb, d), lambda i: (0, 0)),
             pl.BlockSpec((tm, d), lambda i: (i, 0)),
             pl.BlockSpec((None, 1, d), lambda i: (mod_row(i), 0, 2))]
    if final_w is not None:
        ins.append(final_w.reshape(1, d))
        specs.append(pl.BlockSpec((1, d), lambda i: (0, 0)))
    return pl.pallas_call(
        functools.partial(_outproj_kernel, ka=ka, final=final_w is not None),
        out_shape=jax.ShapeDtypeStruct((m, d), F32),
        grid=(m // tm,),
        in_specs=specs,
        out_specs=pl.BlockSpec((tm, d), lambda i: (i, 0)),
        compiler_params=_cparams(1),
        name=name,
    )(*ins)


def _softmax_pv(q, parts, scale):
    c = scale * LOG2E
    m = denom = num = None
    for k, bias, v in parts:
        x = _dot_nt(q, k) * c
        if bias is not None:
            x = x + bias
        m_part = jnp.max(x, axis=-1, keepdims=True)
        if m is None:
            m = m_part
            p = jnp.exp2(x - m)
            denom = jnp.sum(p, axis=-1, keepdims=True)
            num = _dot(p, v)
        else:
            m_new = jnp.maximum(m, m_part)
            alpha = jnp.exp2(m - m_new)
            p = jnp.exp2(x - m_new)
            denom = denom * alpha + jnp.sum(p, axis=-1, keepdims=True)
            num = num * alpha + _dot(p, v)
            m = m_new
    return num / denom


def _softmax_pv_joint(q, parts, scale):
    c = scale * LOG2E
    logits = [_dot_nt(q, k) * c if bias is None else _dot_nt(q, k) * c + bias for k, bias, _ in parts]
    m = functools.reduce(jnp.maximum, [jnp.max(x, axis=-1, keepdims=True) for x in logits])
    ps = [jnp.exp2(x - m) for x in logits]
    denom = functools.reduce(lambda a, b: a + b, [jnp.sum(p, axis=-1, keepdims=True) for p in ps])
    num = functools.reduce(lambda a, b: a + b, [_dot(p, v) for p, (_, _, v) in zip(ps, parts)])
    return num / denom


def _na_kernel(idx_ref, q_ref, k_ref, v_ref, kc_ref, vc_ref, tab_ref, o_ref, bias_ref, *, rows, scale):
    j = pl.program_id(2)
    npair = NA_KROWS // 2
    for qr in range(NA_QROWS):
        for m in range(npair):
            tile = tab_ref[idx_ref[(j * NA_QROWS + qr) * npair + m]]
            bias_ref[qr * GRID_W:(qr + 1) * GRID_W, m * LANES:(m + 1) * LANES] = tile

    start = pl.multiple_of(jnp.clip(NA_QROWS * j - NA_WIN_ROWS // 2, 0, rows - NA_KROWS) * GRID_W, GRID_W)
    win = pl.ds(start, NA_KROWS * GRID_W)
    parts = [(k_ref[win, :], bias_ref[...], v_ref[win, :]), (kc_ref[...], None, vc_ref[...])]
    o_ref[...] = _softmax_pv_joint(q_ref[...], parts, scale)


def _na_tile_ids(rows):
    kh = min(NA_WIN_ROWS, rows)
    n = 2 * NA_WIN_ROWS - 1
    ids = []
    for j in range(rows // NA_QROWS):
        ks = int(np.clip(NA_QROWS * j - NA_WIN_ROWS // 2, 0, rows - NA_KROWS))
        for qr in range(NA_QROWS):
            r = NA_QROWS * j + qr
            r0 = int(np.clip(r - kh // 2, 0, rows - kh))
            for m in range(NA_KROWS // 2):
                kr = ks + 2 * m
                dy = kr - r + NA_WIN_ROWS - 1
                first, second = r0 <= kr < r0 + kh, r0 <= kr + 1 < r0 + kh
                ids.append(dy if first and second else n + dy if first else 2 * n + dy + 1 if second else 3 * n)
    return np.array(ids, np.int32)


def _na_bias_tiles(rpb):
    col = np.arange(GRID_W)
    cs = np.clip(col - NA_WIN_COLS // 2, 0, GRID_W - NA_WIN_COLS)
    kc = col[None, :]
    in_win = (kc >= cs[:, None]) & (kc < cs[:, None] + NA_WIN_COLS)
    dx = np.where(in_win, kc - col[:, None] + NA_WIN_COLS - 1, 0)
    tab = jnp.where(in_win[None, None], rpb[:, :, dx] * LOG2E, NEG_INF)
    neg = jnp.full_like(tab[:, :1], NEG_INF)
    nxt = jnp.concatenate([tab[:, 1:], neg], axis=1)
    both = jnp.concatenate([tab, nxt], axis=-1)
    first = jnp.concatenate([tab, jnp.broadcast_to(neg, tab.shape)], axis=-1)
    second = jnp.concatenate([jnp.broadcast_to(neg, tab.shape), tab], axis=-1)
    return jnp.concatenate([both, first, second, jnp.concatenate([neg, neg], axis=-1)], axis=1)


def na_attention(z, z_c, rpb, batch, t, l):
    rows = t // GRID_W
    nj = rows // NA_QROWS
    tq = NA_QROWS * GRID_W
    ids = jnp.asarray(_na_tile_ids(rows))
    tiles = _na_bias_tiles(rpb)
    h8 = NA_HEADS
    return pl.pallas_call(
        functools.partial(_na_kernel, rows=rows, scale=NA_HEAD_DIM ** -0.5),
        out_shape=jax.ShapeDtypeStruct((batch * t, NA_WIDTH), F32),
        grid_spec=pltpu.PrefetchScalarGridSpec(
            num_scalar_prefetch=1,
            grid=(h8, batch, nj),
            in_specs=[pl.BlockSpec((tq, LANES), lambda h, b, j, g: (b * nj + j, EV_Q_BLK + h)),
                      pl.BlockSpec((t, LANES), lambda h, b, j, g: (b, EV_Q_BLK + h8 + h)),
                      pl.BlockSpec((t, LANES), lambda h, b, j, g: (b, EV_Q_BLK + 2 * h8 + h)),
                      pl.BlockSpec((l, LANES), lambda h, b, j, g: (b, EV_Q_BLK + h8 + h)),
                      pl.BlockSpec((l, LANES), lambda h, b, j, g: (b, EV_Q_BLK + 2 * h8 + h)),
                      pl.BlockSpec((None,) + tiles.shape[1:], lambda h, b, j, g: (h, 0, 0, 0))],
            out_specs=pl.BlockSpec((tq, LANES), lambda h, b, j, g: (b * nj + j, h)),
            scratch_shapes=[pltpu.VMEM((tq, NA_KROWS * GRID_W), F32)]),
        compiler_params=_cparams(3),
        name="na_attention",
    )(ids, z, z, z, z_c, z_c, tiles)


def _dense_attn_kernel(q_ref, k_ref, v_ref, o_ref, *, scale):
    o_ref[...] = _softmax_pv(q_ref[...], [(k_ref[...], None, v_ref[...])], scale)


def ctx_attention(z_c, batch, l):
    h8 = NA_HEADS
    return pl.pallas_call(
        functools.partial(_dense_attn_kernel, scale=NA_HEAD_DIM ** -0.5),
        out_shape=jax.ShapeDtypeStruct((batch * l, NA_WIDTH), F32),
        grid=(batch, h8),
        in_specs=[pl.BlockSpec((l, LANES), lambda b, h: (b, EV_Q_BLK + h)),
                  pl.BlockSpec((l, LANES), lambda b, h: (b, EV_Q_BLK + h8 + h)),
                  pl.BlockSpec((l, LANES), lambda b, h: (b, EV_Q_BLK + 2 * h8 + h))],
        out_specs=pl.BlockSpec((l, LANES), lambda b, h: (b, h)),
        compiler_params=_cparams(2),
        name="ctx_attention",
    )(z_c, z_c, z_c)


def _head_sum(x):
    first = lax.broadcasted_iota(jnp.int32, x.shape, 1) < RW_HEAD_DIM
    s0 = jnp.sum(jnp.where(first, x, 0.0), axis=-1, keepdims=True)
    s1 = jnp.sum(jnp.where(first, 0.0, x), axis=-1, keepdims=True)
    return jnp.where(first, s0, s1)


def _softplus(x):
    return jnp.maximum(x, 0.0) + jnp.log1p(jnp.exp(-jnp.abs(x)))


def _rw_terms_kernel(ur_ref, uk_ref, uv_ref, uwa_ref, pr_ref, pk_ref, pv_ref, pwa_ref, nr_ref, nk_ref, nv_ref,
                     nwa_ref, mu_ref, w0_ref, w2_ref, a0_ref, a2_ref, kk_ref, ka_ref, rk_ref,
                     rv_o, wk0_o, wk1_o, bk0_o, bk1_o, bonus_o, *, nt):
    i = pl.program_id(1)
    tm = ur_ref.shape[0]

    def shifted(u_ref, p_ref, n_ref, col0):
        u = u_ref[...]
        mu = mu_ref[:, col0:col0 + u.shape[1]]
        row = lax.broadcasted_iota(jnp.int32, u.shape, 0)
        before = jnp.where(i == 0, 0.0, p_ref[7:8, :])
        after = jnp.where(i == nt - 1, 0.0, n_ref[0:1, :])
        prev = jnp.where(row == 0, before, pltpu.roll(u, 1, axis=0))
        nxt = jnp.where(row == tm - 1, after, pltpu.roll(u, tm - 1, axis=0))
        return u + mu[0:1] * (prev - u) + mu[1:2] * (nxt - u)

    r_all = shifted(ur_ref, pr_ref, nr_ref, 0)
    k_all = shifted(uk_ref, pk_ref, nk_ref, RW_WIDTH)
    v_all = shifted(uv_ref, pv_ref, nv_ref, 2 * RW_WIDTH)
    wa = shifted(uwa_ref, pwa_ref, nwa_ref, 3 * RW_WIDTH)
    wd = jnp.tanh(wa[:, :LANES]).astype(BF16)
    ad = wa[:, LANES:].astype(BF16)
    first = lax.broadcasted_iota(jnp.int32, (tm, LANES), 1) < RW_HEAD_DIM

    def head_rows(x, y):
        return (jnp.where(first, x, pltpu.roll(y, RW_HEAD_DIM, axis=1)),
                jnp.where(first, pltpu.roll(x, RW_HEAD_DIM, axis=1), y))

    outs = (rv_o, wk0_o, wk1_o, bk0_o, bk1_o)
    even = [[] for _ in outs]
    odd = [[] for _ in outs]
    bonus = []
    for p in range(RW_WIDTH // LANES):
        sl = slice(p * LANES, (p + 1) * LANES)
        r, k, v = r_all[:, sl], k_all[:, sl], v_all[:, sl]
        kk = k * kk_ref[:, sl]
        kk = kk / jnp.maximum(jnp.sqrt(_head_sum(kk * kk)), 1e-12)
        kd_sum = jnp.zeros_like(k)
        per_dir = []
        for d in range(2):
            w = w0_ref[d:d + 1, sl] + jnp.dot(wd, w2_ref[d, :, sl].astype(BF16), preferred_element_type=F32)
            a = _sigmoid(a0_ref[d:d + 1, sl] + jnp.dot(ad, a2_ref[d, :, sl].astype(BF16), preferred_element_type=F32))
            kd = k * (1.0 + (a - 1.0) * ka_ref[:, sl])
            per_dir.append(((jnp.exp(-jnp.exp(-_softplus(-w) - 0.5)), kd), (kk * a, kk)))
            kd_sum = kd_sum + kd
        pairs = [(r, v), per_dir[0][0], per_dir[1][0], per_dir[0][1], per_dir[1][1]]
        bonus.append(_head_sum(r * kd_sum * rk_ref[:, sl]) * v)
        for ev, od, (x, y) in zip(even, odd, pairs):
            e, o = head_rows(x, y)
            ev.append(e)
            od.append(o)
    for o_ref, ev, od in zip(outs, even, odd):
        o_ref[...] = jnp.swapaxes(jnp.stack(ev + od, axis=0), 0, 1)
    bonus_o[...] = jnp.swapaxes(jnp.stack(bonus, axis=0), 0, 1)


def rwkv_terms(z, n_seq, t, mu, w0, w2p, a0, a2p, k_k, k_a, r_k, tm=256):
    tm = min(tm, t)
    nt = t // tm
    npair = RW_WIDTH // LANES
    wide0 = EV_RW_BLK * LANES // RW_WIDTH
    wa_blk = (EV_RW_BLK + 3 * npair) // 2
    n8 = tm // 8

    def cur(width, cb):
        return pl.BlockSpec((tm, width), lambda b, i: (b * nt + i, cb))

    def before(width, cb):
        return pl.BlockSpec((8, width), lambda b, i: (jnp.maximum((b * nt + i) * n8 - 1, 0), cb))

    def after(width, cb):
        return pl.BlockSpec((8, width), lambda b, i: (jnp.minimum((b * nt + i + 1) * n8, n_seq * nt * n8 - 1), cb))

    cols = [(RW_WIDTH, wide0), (RW_WIDTH, wide0 + 1), (RW_WIDTH, wide0 + 2), (2 * LANES, wa_blk)]
    whole = lambda a: pl.BlockSpec(a.shape, lambda b, i: (0,) * a.ndim)
    params = (mu, w0, w2p, a0, a2p, k_k, k_a, r_k)
    out = pl.BlockSpec((tm, npair, LANES), lambda b, i: (i, b, 0))
    out2 = pl.BlockSpec((tm, 2 * npair, LANES), lambda b, i: (i, b, 0))
    return pl.pallas_call(
        functools.partial(_rw_terms_kernel, nt=nt),
        out_shape=[jax.ShapeDtypeStruct((t, n_seq * 2 * npair, LANES), F32)] * 5
        + [jax.ShapeDtypeStruct((t, n_seq * npair, LANES), F32)],
        grid=(n_seq, nt),
        in_specs=([cur(*c) for c in cols] + [before(*c) for c in cols] + [after(*c) for c in cols]
                  + [whole(a) for a in params]),
        out_specs=[out2] * 5 + [out],
        compiler_params=_cparams(2),
        name="rwkv_terms",
    )(*([z] * 12), *params)


def _to_lanes(n):
    q = jnp.concatenate([n, pltpu.roll(n, RW_HEAD_DIM, axis=1)], axis=0)
    return q.T[:RW_HEAD_DIM]


def _rw_scan_kernel(rvf, rvb, wk0, wk1, bk0, bk1, s0_ref, yf_ref, yb_ref, sfin_ref,
                    s_ref, z_ref, znext_ref, g_ref, ybuf_ref, *, chunk):
    n = RW_HEAD_DIM
    nb = rvf.shape[1] // (2 * RW_WIDTH // LANES)
    R, K, V, A, B = range(5)

    @pl.when(pl.program_id(0) == 0)
    def _():
        s_ref[...] = s0_ref[...]

    def convert(t_src):
        def pair_tiles(f, bw):
            both = (f[t_src], bw[chunk - 1 - t_src])
            x = jnp.concatenate([src[(b * 2 + par) * 8:(b * 2 + par) * 8 + 8]
                                 for par in range(2) for src in both for b in range(nb)], axis=0)
            xt = x.T
            return xt[:n], xt[n:]

        r, v = pair_tiles(rvf, rvb)
        w, k = pair_tiles(wk0, wk1)
        b, kk = pair_tiles(bk0, bk1)
        g_prev = g_ref[0]
        g = g_prev * w
        g_ref[0] = g
        inv = 1.0 / g
        znext_ref[R] = r * g
        znext_ref[K] = k * inv
        znext_ref[V] = v
        znext_ref[A] = -(kk * g_prev)
        znext_ref[B] = b * inv

    def emit_y(t):
        rows = _to_lanes(ybuf_ref[...])
        yf_ref[t] = rows[:n // 2]
        yb_ref[chunk - 1 - t] = rows[n // 2:]

    g_ref[0] = jnp.ones((n, LANES), F32)
    ybuf_ref[...] = jnp.zeros((n, LANES), F32)
    convert(0)

    def step(t, carry):
        emit_y(jnp.maximum(t - 1, 0))
        z_ref[...] = znext_ref[...]
        g_ref[1] = g_ref[0]
        convert(jnp.minimum(t + 1, chunk - 1))
        sa = [jnp.zeros((n, LANES), F32), jnp.zeros((n, LANES), F32)]
        for k in range(n):
            sa[k % 2] = sa[k % 2] + s_ref[k] * z_ref[A, k:k + 1, :]
        sa = sa[0] + sa[1]
        vt = z_ref[V]
        y = [jnp.zeros((n, LANES), F32), jnp.zeros((n, LANES), F32)]
        for k in range(n):
            sk = s_ref[k] + sa * z_ref[B, k:k + 1, :] + vt * z_ref[K, k:k + 1, :]
            s_ref[k] = sk
            y[k % 2] = y[k % 2] + sk * z_ref[R, k:k + 1, :]
        ybuf_ref[...] = y[0] + y[1]
        return carry

    lax.fori_loop(0, chunk, step, 0)
    emit_y(chunk - 1)
    for k in range(n):
        s_ref[k] = s_ref[k] * g_ref[1, k:k + 1, :]
    sfin_ref[...] = s_ref[...]


def rwkv_scan(rv, wk0, wk1, bk0, bk1, s0):
    t, nrow, _ = rv.shape
    n = RW_HEAD_DIM
    chunk = RW_SCAN_CHUNK
    nc = t // chunk
    assert nrow == LANES // 2, "the states of both directions fill the 128 lanes"
    fwd = pl.BlockSpec((chunk, nrow, LANES), lambda i: (i, 0, 0))
    bwd = pl.BlockSpec((chunk, nrow, LANES), lambda i: (nc - 1 - i, 0, 0))
    yfwd = pl.BlockSpec((chunk, nrow // 2, LANES), lambda i: (i, 0, 0))
    ybwd = pl.BlockSpec((chunk, nrow // 2, LANES), lambda i: (nc - 1 - i, 0, 0))
    state = pl.BlockSpec((n, n, LANES), lambda i: (0, 0, 0))
    slab = jax.ShapeDtypeStruct((t, nrow // 2, LANES), F32)
    return pl.pallas_call(
        functools.partial(_rw_scan_kernel, chunk=chunk),
        out_shape=[slab, slab, jax.ShapeDtypeStruct((n, n, LANES), F32)],
        grid=(nc,),
        in_specs=[fwd, bwd, fwd, bwd, fwd, bwd, state],
        out_specs=[yfwd, ybwd, state],
        scratch_shapes=[pltpu.VMEM((n, n, LANES), F32), pltpu.VMEM((5, n, LANES), F32),
                        pltpu.VMEM((5, n, LANES), F32), pltpu.VMEM((2, n, LANES), F32),
                        pltpu.VMEM((n, LANES), F32)],
        compiler_params=_cparams(1),
        name="rwkv_scan",
    )(rv, rv, wk0, wk1, bk0, bk1, s0)


def _rw_readout_kernel(yf_ref, yb_ref, bonus_ref, lw_ref, lb_ref, o_ref):
    npair = RW_WIDTH // LANES
    y_all = jnp.swapaxes(yf_ref[...] + yb_ref[...], 0, 1)
    bonus_all = jnp.swapaxes(bonus_ref[...], 0, 1)
    for q in range(yf_ref.shape[1]):
        b, p = divmod(q, npair)
        sl = slice(p * LANES, (p + 1) * LANES)
        y = y_all[q]
        mu = _head_sum(y) * (1.0 / RW_HEAD_DIM)
        d = y - mu
        var = _head_sum(d * d) * (1.0 / RW_HEAD_DIM)
        o_ref[b, :, sl] = d * lax.rsqrt(var + RW_GN_EPS) * lw_ref[:, sl] + lb_ref[:, sl] + bonus_all[q]


def rwkv_readout(y_f, y_b, bonus, ln_w, ln_b, tm=256):
    t, nq, _ = y_f.shape
    tm = min(tm, t)
    batch = nq // (RW_WIDTH // LANES)
    slab = pl.BlockSpec((tm, nq, LANES), lambda i: (i, 0, 0))
    vec = pl.BlockSpec((1, RW_WIDTH), lambda i: (0, 0))
    return pl.pallas_call(
        _rw_readout_kernel,
        out_shape=jax.ShapeDtypeStruct((batch, t, RW_WIDTH), F32),
        grid=(t // tm,),
        in_specs=[slab, slab, slab, vec, vec],
        out_specs=pl.BlockSpec((batch, tm, RW_WIDTH), lambda i: (0, i, 0)),
        compiler_params=_cparams(1),
        name="rwkv_readout",
    )(y_f, y_b, bonus, ln_w.reshape(1, -1), ln_b.reshape(1, -1))


def _rms(x, w):
    return x * lax.rsqrt(jnp.mean(x * x, axis=-1, keepdims=True) + NORM_EPS) * w


def _rope(x, cos, sin):
    odd = (lax.broadcasted_iota(jnp.int32, x.shape, 1) & 1) == 1
    swapped = jnp.where(odd, pltpu.roll(x, 1, axis=1), pltpu.roll(x, LANES - 1, axis=1))
    return x * cos + swapped * sin


def _mla_q_kernel(c_ref, nw_ref, w_ref, cos_ref, sin_ref, q_ref, h_ref):
    @pl.when(pl.program_id(1) == 0)
    def _():
        h_ref[...] = _rms(c_ref[...], nw_ref[...]).astype(BF16)

    acc = jnp.dot(h_ref[...], w_ref[...], preferred_element_type=F32)
    q_ref[:, :MLA_NOPE] = acc[:, :MLA_NOPE].astype(BF16)
    q_ref[:, MLA_NOPE:] = _rope(acc[:, MLA_NOPE:], cos_ref[...], sin_ref[...]).astype(BF16)


def mla_queries(z, q_norm, w_uq_pad, cos, sin, t, tm=512):
    m = z.shape[0]
    tm = min(tm, t)
    nt = t // tm
    return pl.pallas_call(
        _mla_q_kernel,
        out_shape=jax.ShapeDtypeStruct((m, MLA_HEADS * MLA_QK_PAD), BF16),
        grid=(m // tm, MLA_HEADS),
        in_specs=[pl.BlockSpec((tm, MLA_Q_RANK), lambda i, h: (i, GATE_COLS // MLA_Q_RANK)),
                  pl.BlockSpec((1, MLA_Q_RANK), lambda i, h: (0, 0)),
                  pl.BlockSpec((MLA_Q_RANK, MLA_QK_PAD), lambda i, h: (0, h)),
                  pl.BlockSpec((tm, LANES), lambda i, h: (i % nt, 0)),
                  pl.BlockSpec((tm, LANES), lambda i, h: (i % nt, 0))],
        out_specs=pl.BlockSpec((tm, MLA_QK_PAD), lambda i, h: (i, h)),
        scratch_shapes=[pltpu.VMEM((tm, MLA_Q_RANK), BF16)],
        compiler_params=_cparams(2),
        name="mla_queries",
    )(z, q_norm.reshape(1, -1), w_uq_pad, cos, sin)


def _mla_kv_kernel(c_ref, pe_ref, nw_ref, w_ref, cos_ref, sin_ref, k_ref, v_ref, h_ref):
    @pl.when(pl.program_id(1) == 0)
    def _():
        h_ref[...] = _rms(c_ref[...], nw_ref[...]).astype(BF16)

    acc = jnp.dot(h_ref[...], w_ref[...], preferred_element_type=F32)
    k_ref[:, :MLA_NOPE] = acc[:, :MLA_NOPE].astype(BF16)
    k_ref[:, MLA_NOPE:] = _rope(pe_ref[...], cos_ref[...], sin_ref[...]).astype(BF16)
    v_ref[...] = acc[:, MLA_NOPE:].astype(BF16)


def mla_keys_values(z, kv_norm, w_ukv, cos, sin, t, tm=512):
    m = z.shape[0]
    tm = min(tm, t)
    nt = t // tm
    return pl.pallas_call(
        _mla_kv_kernel,
        out_shape=[jax.ShapeDtypeStruct((m, MLA_HEADS * MLA_QK_PAD), BF16),
                   jax.ShapeDtypeStruct((m, MLA_WIDTH), BF16)],
        grid=(m // tm, MLA_HEADS),
        in_specs=[pl.BlockSpec((tm, MLA_KV_RANK), lambda i, h: (i, (GATE_COLS + MLA_Q_RANK) // MLA_KV_RANK)),
                  pl.BlockSpec((tm, LANES), lambda i, h: (i, OD_A_BLK + (MLA_Q_RANK + MLA_KV_RANK) // LANES)),
                  pl.BlockSpec((1, MLA_KV_RANK), lambda i, h: (0, 0)),
                  pl.BlockSpec((MLA_KV_RANK, MLA_NOPE + MLA_V), lambda i, h: (0, h)),
                  pl.BlockSpec((tm, LANES), lambda i, h: (i % nt, 0)),
                  pl.BlockSpec((tm, LANES), lambda i, h: (i % nt, 0))],
        out_specs=[pl.BlockSpec((tm, MLA_QK_PAD), lambda i, h: (i, h)),
                   pl.BlockSpec((tm, MLA_V), lambda i, h: (i, h))],
        scratch_shapes=[pltpu.VMEM((tm, MLA_KV_RANK), BF16)],
        compiler_params=_cparams(2),
        name="mla_keys_values",
    )(z, z, kv_norm.reshape(1, -1), w_ukv, cos, sin)


def _mla_attn_kernel(q_ref, k_ref, v_ref, kc_ref, vc_ref, o_ref, *, scale):
    parts = [(kc_ref[...], None, vc_ref[...])]
    nkeys = k_ref.shape[0]
    step = min(MLA_KEY_CHUNK, nkeys)
    for lo in range(0, nkeys, step):
        parts.append((k_ref[lo:lo + step, :], None, v_ref[lo:lo + step, :]))
    o_ref[...] = _softmax_pv(q_ref[...], parts, scale)


def mla_attention(q, k, v, kc, vc, batch, t, l, tq=1024):
    tq = min(tq, t)
    nq = t // tq
    return pl.pallas_call(
        functools.partial(_mla_attn_kernel, scale=(MLA_NOPE + MLA_ROPE) ** -0.5),
        out_shape=jax.ShapeDtypeStruct((batch * t, MLA_WIDTH), F32),
        grid=(batch, MLA_HEADS, nq),
        in_specs=[pl.BlockSpec((tq, MLA_QK_PAD), lambda b, h, i: (b * nq + i, h)),
                  pl.BlockSpec((t, MLA_QK_PAD), lambda b, h, i: (b, h)),
                  pl.BlockSpec((t, MLA_V), lambda b, h, i: (b, h)),
                  pl.BlockSpec((l, MLA_QK_PAD), lambda b, h, i: (b, h)),
                  pl.BlockSpec((l, MLA_V), lambda b, h, i: (b, h))],
        out_specs=pl.BlockSpec((tq, MLA_V), lambda b, h, i: (b * nq + i, h)),
        compiler_params=_cparams(3),
        name="mla_attention",
    )(q, k, v, kc, vc)


def _rope_tables(t):
    tok = np.arange(t)
    pos = np.stack([tok // GRID_W, tok % GRID_W], axis=-1).astype(np.float32)
    n_freq = MLA_ROPE // 4
    inv = (ROPE_BASE ** (-jnp.arange(n_freq, dtype=F32) / n_freq))
    ang = (jnp.asarray(pos)[:, :, None] * inv).reshape(t, MLA_ROPE // 2)
    cos = jnp.repeat(jnp.cos(ang), 2, axis=-1)
    sin = jnp.repeat(jnp.sin(ang), 2, axis=-1) * jnp.tile(jnp.array([-1.0, 1.0], F32), MLA_ROPE // 2)
    pad = ((0, 0), (0, LANES - MLA_ROPE))
    return jnp.pad(cos, pad), jnp.pad(sin, pad)


def _split3(x):
    hi = x.astype(BF16)
    r1 = x - hi.astype(F32)
    mid = r1.astype(BF16)
    lo = (r1 - mid.astype(F32)).astype(BF16)
    return hi, mid, lo


def _hg_kernel(q_ref, f_ref, v_ref, lb_ref, s0_ref, *rest, reverse, finalize, emit_state):
    st_ref = rest[-1]
    rest = rest[:-1]
    if finalize:
        prev_ref, nw_ref = rest[0], rest[1]
        rest = rest[2:]
    o_ref = rest[0]

    @pl.when(pl.program_id(2) == 0)
    def _():
        st_ref[...] = s0_ref[...]

    bt = q_ref.shape[0]
    nchunk = bt // HG_CHUNK
    order = range(nchunk - 1, -1, -1) if reverse else range(nchunk)
    ri = lax.broadcasted_iota(jnp.int32, (bt, bt), 0)
    ci = lax.broadcasted_iota(jnp.int32, (bt, bt), 1)
    same = (ri // HG_CHUNK) == (ci // HG_CHUNK)
    tri = same & ((ci >= ri) if reverse else (ci <= ri))
    tri_f = tri.astype(F32)
    tri_b = tri.astype(BF16)
    row_chunk = lax.broadcasted_iota(jnp.int32, (bt, HG_DK), 0) // HG_CHUNK
    in_chunk = [row_chunk == c for c in range(nchunk)]

    lb = lb_ref[...]
    fz = f_ref[...]
    log_sig = jnp.minimum(fz, 0.0) - jnp.log1p(jnp.exp(-jnp.abs(fz)))
    la = jnp.log(lb)
    lbb = jnp.log1p(-lb) + log_sig
    log_f = jnp.maximum(la, lbb) + jnp.log1p(jnp.exp(-jnp.abs(la - lbb)))
    kf_all = (1.0 - lb) * _sigmoid(-fz)
    pieces = jnp.dot(tri_b, jnp.concatenate(_split3(log_f), axis=1), preferred_element_type=F32)
    width = log_f.shape[1]
    cum_all = pieces[:, :width] + pieces[:, width:2 * width] + pieces[:, 2 * width:]

    for hh in range(HG_STEP_HEADS):
        sl = slice(hh * HG_DK, (hh + 1) * HG_DK)
        cum, kf, v = cum_all[:, sl], kf_all[:, sl], v_ref[:, sl]
        last = [cum[c * HG_CHUNK:c * HG_CHUNK + 1] if reverse else cum[(c + 1) * HG_CHUNK - 1:(c + 1) * HG_CHUNK]
                for c in range(nchunk)]
        tot = jnp.concatenate([jnp.broadcast_to(x, (HG_CHUNK, HG_DK)) for x in last], axis=0)
        q_in = q_ref[:, sl] * jnp.exp(cum)
        k_in = kf * jnp.exp(-cum)
        k_end = kf * jnp.exp(tot - cum)
        o_intra = _dot(_dot_nt(q_in, k_in) * tri_f, v)
        v_by_chunk = jnp.concatenate([jnp.where(in_chunk[c], v, 0.0) for c in range(nchunk)], axis=1)
        d_st = _dot_tn(v_by_chunk, k_end)
        st = st_ref[hh]
        before = [None] * nchunk
        for c in order:
            before[c] = st
            st = st * jnp.exp(last[c]) + d_st[c * HG_DV:(c + 1) * HG_DV]
        st_ref[hh] = st
        q_by_chunk = jnp.concatenate([jnp.where(in_chunk[c], q_in, 0.0) for c in range(nchunk)], axis=1)
        o = o_intra + _dot_nt(q_by_chunk, jnp.concatenate(before, axis=1))
        if finalize:
            o = o + prev_ref[:, sl]
            o = o * lax.rsqrt(jnp.mean(o * o, axis=-1, keepdims=True) + NORM_EPS) * nw_ref[...]
        o_ref[:, sl] = o
        if emit_state:
            rest[1][hh] = st


def hgrn2_pass(z, lb, s0, batch, t, f_off, reverse, prev=None, norm_w=None, emit_state=False):
    bt = min(HG_BLOCK, t)
    nblk = t // bt
    nh = HG_STEP_HEADS
    wide = nh * LANES
    finalize = prev is not None
    assert OD_HG_BLK % nh == 0 and HG_HEADS % nh == 0, "head groups must start on a whole column block"
    pos = (lambda i: nblk - 1 - i) if reverse else (lambda i: i)
    blk = lambda off: pl.BlockSpec((bt, wide), lambda b, h, i, off=off: (b * nblk + pos(i), off // nh + h))
    state = pl.BlockSpec((None, nh, HG_DV, HG_DK), lambda b, h, i: (b, h, 0, 0))
    ins = [z, z, z, lb.reshape(1, -1), s0]
    specs = [blk(OD_HG_BLK), blk(OD_HG_BLK + f_off), blk(OD_HG_BLK + 3 * HG_HEADS),
             pl.BlockSpec((1, wide), lambda b, h, i: (0, h)), state]
    if finalize:
        ins += [prev, norm_w.reshape(1, -1)]
        specs += [blk(0), pl.BlockSpec((1, HG_DV), lambda b, h, i: (0, 0))]
    out_shape = [jax.ShapeDtypeStruct((batch * t, HG_WIDTH), F32)]
    out_specs = [blk(0)]
    if emit_state:
        out_shape.append(jax.ShapeDtypeStruct((batch, HG_HEADS, HG_DV, HG_DK), F32))
        out_specs.append(state)
    return pl.pallas_call(
        functools.partial(_hg_kernel, reverse=reverse, finalize=finalize, emit_state=emit_state),
        out_shape=out_shape,
        grid=(batch, HG_HEADS // nh, nblk),
        in_specs=specs,
        out_specs=out_specs,
        scratch_shapes=[pltpu.VMEM((nh, HG_DV, HG_DK), F32)],
        compiler_params=_cparams(3),
        name="hgrn2_" + ("bwd" if reverse else "fwd"),
    )(*ins)


def _even_layer(x, xc, mod, norm_w, w_in, w_out, rpb, rw_params, batch, t, l):
    d = x.shape[1]
    s2 = 3 * NA_WIDTH + RW_SHIFT_COLS
    zero = jnp.zeros((d, RW_PAD_COLS - RW_SHIFT_COLS), BF16)
    w_in = w_in.astype(BF16)
    w_ext = jnp.concatenate([w_in[:, s2:], w_in[:, :s2], zero], axis=1)
    tiles_per_seq = max(t // 1024, 1)
    z = in_projection(x, norm_w, mod, lambda i: i // tiles_per_seq, w_ext, "even_in_proj")
    z_c = in_projection(xc, norm_w, mod, lambda i: batch, w_ext, "even_in_proj_ctx")

    y_na = na_attention(z, z_c, rpb, batch, t, l)
    yc_na = ctx_attention(z_c, batch, l)

    mu, w0, w2, a0, a2, k_k, k_a, r_k, ln_w, ln_b = rw_params
    mu = jnp.pad(mu, ((0, 0), (0, RW_PAD_COLS - RW_SHIFT_COLS)))
    zpad = jnp.zeros((RW_RANK, RW_WIDTH), F32)
    w2p = jnp.stack([jnp.concatenate([w2[0], zpad]), jnp.concatenate([zpad, w2[1]])])
    a2p = jnp.stack([jnp.concatenate([a2[0], zpad]), jnp.concatenate([zpad, a2[1]])])
    vecs = (mu, w0, w2p, a0, a2p, k_k.reshape(1, -1), k_a.reshape(1, -1), r_k.reshape(1, -1))
    terms = rwkv_terms(z, batch, t, *vecs)
    terms_c = rwkv_terms(z_c, batch, l, *vecs)
    s_zero = jnp.zeros((RW_HEAD_DIM, RW_HEAD_DIM, LANES), F32)
    yc_f, yc_b, s_ctx = rwkv_scan(*terms_c[:5], s_zero)
    y_f, y_b, _ = rwkv_scan(*terms[:5], s_ctx)
    y_rw = rwkv_readout(y_f, y_b, terms[5], ln_w, ln_b).reshape(batch * t, RW_WIDTH)
    yc_rw = rwkv_readout(yc_f, yc_b, terms_c[5], ln_w, ln_b).reshape(batch * l, RW_WIDTH)

    w_out = w_out.astype(BF16)
    rows_per_seq = max(t // 256, 1)
    x = out_projection(y_na, y_rw, z, w_out, x, mod, lambda i: i // rows_per_seq, "even_out_proj")
    xc = out_projection(yc_na, yc_rw, z_c, w_out, xc, mod, lambda i: batch, "even_out_proj_ctx")
    return x, xc


def _odd_layer(x, xc, mod, norm_w, w_in, w_out, q_norm, w_uq, kv_norm, w_ukv, lb, hg_norm_w, final_w,
               batch, t, l):
    d = x.shape[1]
    o3 = MLA_Q_RANK + MLA_KV_RANK + MLA_ROPE
    o4 = o3 + 4 * HG_FDIM
    zero = jnp.zeros((d, MLA_A_COLS - o3), BF16)
    w_in = w_in.astype(BF16)
    w_ext = jnp.concatenate([w_in[:, o4:], w_in[:, :o3], zero, w_in[:, o3:o4]], axis=1)
    tiles_per_seq = max(t // 1024, 1)
    z = in_projection(x, norm_w, mod, lambda i: i // tiles_per_seq, w_ext, "odd_in_proj")
    z_c = in_projection(xc, norm_w, mod, lambda i: batch, w_ext, "odd_in_proj_ctx")

    cos, sin = _rope_tables(t)
    qk = MLA_NOPE + MLA_ROPE
    w_uq_pad = jnp.pad(w_uq.reshape(MLA_Q_RANK, MLA_HEADS, qk), ((0, 0), (0, 0), (0, MLA_QK_PAD - qk)))
    w_uq_pad = w_uq_pad.reshape(MLA_Q_RANK, MLA_HEADS * MLA_QK_PAD).astype(BF16)
    w_ukv = w_ukv.astype(BF16)
    q = mla_queries(z, q_norm, w_uq_pad, cos, sin, t)
    k, v = mla_keys_values(z, kv_norm, w_ukv, cos, sin, t)
    ones = jnp.ones((l, LANES), F32)
    kc, vc = mla_keys_values(z_c, kv_norm, w_ukv, ones, jnp.zeros_like(ones), l)
    y_mla = mla_attention(q, k, v, kc, vc, batch, t, l)

    s_zero = jnp.zeros((batch, HG_HEADS, HG_DV, HG_DK), F32)
    _, s_f = hgrn2_pass(z_c, lb, s_zero, batch, l, HG_HEADS, False, emit_state=True)
    _, s_b = hgrn2_pass(z_c, lb, s_zero, batch, l, 2 * HG_HEADS, True, emit_state=True)
    o_f, = hgrn2_pass(z, lb, s_f, batch, t, HG_HEADS, False)
    y_hg, = hgrn2_pass(z, lb, s_b, batch, t, 2 * HG_HEADS, True, prev=o_f, norm_w=hg_norm_w)

    rows_per_seq = max(t // 256, 1)
    return out_projection(y_mla, y_hg, z, w_out.astype(BF16), x, mod, lambda i: i // rows_per_seq,
                          "odd_out_proj", final_w=final_w)


def kernel(x, c, ctx, c_ctx, ada_w, ada_b, norm_w, e_w_in, e_w_out, na_rpb, rw_mu, rw_w0, rw_w2, rw_a0, rw_a2, rw_k_k, rw_k_a, rw_r_k, rw_ln_w, rw_ln_b, o_w_in, o_w_out, mla_q_norm, mla_w_uq, mla_kv_norm, mla_w_ukv, hg_lower_bounds, hg_norm_w, final_norm_w):
    batch, t, d = x.shape
    l = ctx.shape[1]
    assert ada_w.shape[0] == 2, "one even and one odd layer"
    cond = jnp.concatenate([c, c_ctx[None, :], jnp.zeros((8 - batch - 1, d), F32)], axis=0)
    mod = modulation(cond, ada_w, ada_b)
    s = jax.nn.softmax(hg_lower_bounds.astype(F32), axis=0)
    lower = jnp.cumsum(s, axis=0) - s[0]

    xf, xcf = x.reshape(batch * t, d), ctx.reshape(batch * l, d)
    rw_params = (rw_mu[0], rw_w0[0], rw_w2[0], rw_a0[0], rw_a2[0], rw_k_k[0], rw_k_a[0], rw_r_k[0],
                 rw_ln_w[0], rw_ln_b[0])
    xf, xcf = _even_layer(xf, xcf, mod[0].reshape(8, 1, 3 * d), norm_w[0], e_w_in[0], e_w_out[0], na_rpb[0],
                          rw_params, batch, t, l)
    out = _odd_layer(xf, xcf, mod[1].reshape(8, 1, 3 * d), norm_w[1], o_w_in[0], o_w_out[0], mla_q_norm[0],
                     mla_w_uq[0], mla_kv_norm[0], mla_w_ukv[0], lower[1], hg_norm_w[0], final_norm_w,
                     batch, t, l)
    return out.reshape(batch, t, d)
```

```python
import functools
import math

import numpy as np
import jax
import jax.numpy as jnp
from jax import lax
from jax.experimental import pallas as pl
from jax.experimental.pallas import tpu as pltpu

F32 = jnp.float32
BF16 = jnp.bfloat16

GRID_W = 64
NORM_EPS = 1e-6
ROPE_BASE = 10000.0

NA_HEADS = 8
NA_HEAD_DIM = 128
NA_WIDTH = NA_HEADS * NA_HEAD_DIM
NA_WIN_ROWS = 8
NA_WIN_COLS = 16
NA_QROWS = 4
NA_KROWS = 12

RW_HEAD_DIM = 64
RW_HEADS = 16
RW_WIDTH = RW_HEADS * RW_HEAD_DIM
RW_RANK = 64
RW_SHIFT_COLS = 3 * RW_WIDTH + 4 * RW_RANK
RW_PAD_COLS = 3584
RW_GN_EPS = 64e-5
RW_SCAN_CHUNK = 64
RW_DECAY_SCALE = math.exp(-0.5)

MLA_HEADS = 8
MLA_Q_RANK = 512
MLA_KV_RANK = 512
MLA_NOPE = 128
MLA_ROPE = 64
MLA_V = 128
MLA_QK_PAD = 256
MLA_WIDTH = MLA_HEADS * MLA_V
MLA_A_COLS = 1536
MLA_KEY_CHUNK = 512

HG_HEADS = 8
HG_DK = 128
HG_DV = 128
HG_FDIM = HG_HEADS * HG_DK
HG_WIDTH = HG_HEADS * HG_DV
HG_CHUNK = 32
HG_BLOCK = 256
HG_STEP_HEADS = 4

LANES = 128

GATE_COLS = 2048
EV_Q_BLK = GATE_COLS // LANES
EV_RW_BLK = EV_Q_BLK + 3 * NA_WIDTH // LANES
OD_A_BLK = GATE_COLS // LANES
OD_HG_BLK = OD_A_BLK + MLA_A_COLS // LANES
VMEM_LIMIT = 48 * 1024 * 1024
NEG_INF = -1e30
LOG2E = math.log2(math.e)


def _cparams(n_axes):
    return pltpu.CompilerParams(dimension_semantics=("arbitrary",) * n_axes, vmem_limit_bytes=VMEM_LIMIT)


def _sigmoid(x):
    return 1.0 / (1.0 + jnp.exp(-x))


def _silu(x):
    return x * _sigmoid(x)


def _dot(a, b):
    return jnp.dot(a.astype(BF16), b.astype(BF16), preferred_element_type=F32)


def _dot_nt(a, b):
    return lax.dot_general(a.astype(BF16), b.astype(BF16), (((1,), (1,)), ((), ())), preferred_element_type=F32)


def _dot_tn(a, b):
    return lax.dot_general(a.astype(BF16), b.astype(BF16), (((0,), (0,)), ((), ())), preferred_element_type=F32)


def _mod_kernel(c_ref, w_ref, b_ref, o_ref):
    o_ref[...] = _dot(_silu(c_ref[...]), w_ref[...]) + b_ref[...]


def modulation(cond, ada_w, ada_b, tn=512):
    depth, d, n = ada_w.shape
    return pl.pallas_call(
        _mod_kernel,
        out_shape=jax.ShapeDtypeStruct((depth, 8, n), F32),
        grid=(depth, n // tn),
        in_specs=[pl.BlockSpec((8, d), lambda l, j: (0, 0)),
                  pl.BlockSpec((None, d, tn), lambda l, j: (l, 0, j)),
                  pl.BlockSpec((None, 1, tn), lambda l, j: (l, 0, j))],
        out_specs=pl.BlockSpec((None, 8, tn), lambda l, j: (l, 0, j)),
        compiler_params=_cparams(2),
        name="adaln_modulation",
    )(cond, ada_w, ada_b.reshape(depth, 1, n))


def _inproj_kernel(x_ref, nw_ref, shift_ref, scale_ref, w_ref, o_ref, h_ref):
    @pl.when(pl.program_id(1) == 0)
    def _():
        x = x_ref[...]
        y = x * lax.rsqrt(jnp.mean(x * x, axis=-1, keepdims=True) + NORM_EPS) * nw_ref[...]
        h_ref[...] = (y * (1.0 + scale_ref[...]) + shift_ref[...]).astype(BF16)

    o_ref[...] = jnp.dot(h_ref[...], w_ref[...], preferred_element_type=F32)


def in_projection(x, norm_w, mod, mod_row, w, name, tm=1024, tn=512):
    m, d = x.shape
    n = w.shape[1]
    tm = min(tm, m)
    return pl.pallas_call(
        _inproj_kernel,
        out_shape=jax.ShapeDtypeStruct((m, n), F32),
        grid=(m // tm, n // tn),
        in_specs=[pl.BlockSpec((tm, d), lambda i, j: (i, 0)),
                  pl.BlockSpec((1, d), lambda i, j: (0, 0)),
                  pl.BlockSpec((None, 1, d), lambda i, j: (mod_row(i), 0, 0)),
                  pl.BlockSpec((None, 1, d), lambda i, j: (mod_row(i), 0, 1)),
                  pl.BlockSpec((d, tn), lambda i, j: (0, j))],
        out_specs=pl.BlockSpec((tm, tn), lambda i, j: (i, j)),
        scratch_shapes=[pltpu.VMEM((tm, d), BF16)],
        compiler_params=_cparams(2),
        name=name,
    )(x, norm_w.reshape(1, d), mod, mod, w)


def _outproj_kernel(ya_ref, yb_ref, g_ref, w_ref, x_ref, gm_ref, *rest, ka, final):
    o_ref = rest[-1]
    sg = _silu(g_ref[...])
    acc = _dot(ya_ref[...].astype(F32) * sg[:, :ka], w_ref[:ka, :])
    acc += _dot(yb_ref[...].astype(F32) * sg[:, ka:], w_ref[ka:, :])
    out = x_ref[...] + gm_ref[...] * acc
    if final:
        out = out * lax.rsqrt(jnp.mean(out * out, axis=-1, keepdims=True) + NORM_EPS) * rest[0][...]
    o_ref[...] = out


def out_projection(ya, yb, gate, w, x, mod, mod_row, name, final_w=None, tm=256):
    m, d = x.shape
    ka, kb = ya.shape[1], yb.shape[1]
    tm = min(tm, m)
    ins = [ya, yb, gate, w, x, mod]
    specs = [pl.BlockSpec((tm, ka), lambda i: (i, 0)),
             pl.BlockSpec((tm, kb), lambda i: (i, 0)),
             pl.BlockSpec((tm, ka + kb), lambda i: (i, 0)),
             pl.BlockSpec((ka + kb, d), lambda i: (0, 0)),
             pl.BlockSpec((tm, d), lambda i: (i, 0)),
             pl.BlockSpec((None, 1, d), lambda i: (mod_row(i), 0, 2))]
    if final_w is not None:
        ins.append(final_w.reshape(1, d))
        specs.append(pl.BlockSpec((1, d), lambda i: (0, 0)))
    return pl.pallas_call(
        functools.partial(_outproj_kernel, ka=ka, final=final_w is not None),
        out_shape=jax.ShapeDtypeStruct((m, d), F32),
        grid=(m // tm,),
        in_specs=specs,
        out_specs=pl.BlockSpec((tm, d), lambda i: (i, 0)),
        compiler_params=_cparams(1),
        name=name,
    )(*ins)


def _softmax_pv(q, parts, scale):
    c = scale * LOG2E
    m = denom = num = None
    for k, bias, v in parts:
        x = _dot_nt(q, k) * c
        if bias is not None:
            x = x + bias
        m_part = jnp.max(x, axis=-1, keepdims=True)
        if m is None:
            m = m_part
            p = jnp.exp2(x - m)
            denom = jnp.sum(p, axis=-1, keepdims=True)
            num = _dot(p, v)
        else:
            m_new = jnp.maximum(m, m_part)
            alpha = jnp.exp2(m - m_new)
            p = jnp.exp2(x - m_new)
            denom = denom * alpha + jnp.sum(p, axis=-1, keepdims=True)
            num = num * alpha + _dot(p, v)
            m = m_new
    return num / denom


def _softmax_pv_joint(q, parts, scale):
    c = scale * LOG2E
    logits = [_dot_nt(q, k) * c if bias is None else _dot_nt(q, k) * c + bias for k, bias, _ in parts]
    m = functools.reduce(jnp.maximum, [jnp.max(x, axis=-1, keepdims=True) for x in logits])
    ps = [jnp.exp2(x - m) for x in logits]
    denom = functools.reduce(lambda a, b: a + b, [jnp.sum(p, axis=-1, keepdims=True) for p in ps])
    num = functools.reduce(lambda a, b: a + b, [_dot(p, v) for p, (_, _, v) in zip(ps, parts)])
    return num / denom


def _na_kernel(idx_ref, q_ref, k_ref, v_ref, kc_ref, vc_ref, tab_ref, o_ref, bias_ref, *, rows, scale):
    j = pl.program_id(2)
    npair = NA_KROWS // 2
    for qr in range(NA_QROWS):
        for m in range(npair):
            tile = tab_ref[idx_ref[(j * NA_QROWS + qr) * npair + m]]
            bias_ref[qr * GRID_W:(qr + 1) * GRID_W, m * LANES:(m + 1) * LANES] = tile

    start = pl.multiple_of(jnp.clip(NA_QROWS * j - NA_WIN_ROWS // 2, 0, rows - NA_KROWS) * GRID_W, GRID_W)
    win = pl.ds(start, NA_KROWS * GRID_W)
    parts = [(k_ref[win, :], bias_ref[...], v_ref[win, :]), (kc_ref[...], None, vc_ref[...])]
    o_ref[...] = _softmax_pv_joint(q_ref[...], parts, scale)


def _na_tile_ids(rows):
    kh = min(NA_WIN_ROWS, rows)
    n = 2 * NA_WIN_ROWS - 1
    ids = []
    for j in range(rows // NA_QROWS):
        ks = int(np.clip(NA_QROWS * j - NA_WIN_ROWS // 2, 0, rows - NA_KROWS))
        for qr in range(NA_QROWS):
            r = NA_QROWS * j + qr
            r0 = int(np.clip(r - kh // 2, 0, rows - kh))
            for m in range(NA_KROWS // 2):
                kr = ks + 2 * m
                dy = kr - r + NA_WIN_ROWS - 1
                first, second = r0 <= kr < r0 + kh, r0 <= kr + 1 < r0 + kh
                ids.append(dy if first and second else n + dy if first else 2 * n + dy + 1 if second else 3 * n)
    return np.array(ids, np.int32)


def _na_bias_tiles(rpb):
    col = np.arange(GRID_W)
    cs = np.clip(col - NA_WIN_COLS // 2, 0, GRID_W - NA_WIN_COLS)
    kc = col[None, :]
    in_win = (kc >= cs[:, None]) & (kc < cs[:, None] + NA_WIN_COLS)
    dx = np.where(in_win, kc - col[:, None] + NA_WIN_COLS - 1, 0)
    tab = jnp.where(in_win[None, None], rpb[:, :, dx] * LOG2E, NEG_INF)
    neg = jnp.full_like(tab[:, :1], NEG_INF)
    nxt = jnp.concatenate([tab[:, 1:], neg], axis=1)
    both = jnp.concatenate([tab, nxt], axis=-1)
    first = jnp.concatenate([tab, jnp.broadcast_to(neg, tab.shape)], axis=-1)
    second = jnp.concatenate([jnp.broadcast_to(neg, tab.shape), tab], axis=-1)
    return jnp.concatenate([both, first, second, jnp.concatenate([neg, neg], axis=-1)], axis=1)


def na_attention(z, z_c, rpb, batch, t, l):
    rows = t // GRID_W
    nj = rows // NA_QROWS
    tq = NA_QROWS * GRID_W
    ids = jnp.asarray(_na_tile_ids(rows))
    tiles = _na_bias_tiles(rpb)
    h8 = NA_HEADS
    return pl.pallas_call(
        functools.partial(_na_kernel, rows=rows, scale=NA_HEAD_DIM ** -0.5),
        out_shape=jax.ShapeDtypeStruct((batch * t, NA_WIDTH), F32),
        grid_spec=pltpu.PrefetchScalarGridSpec(
            num_scalar_prefetch=1,
            grid=(h8, batch, nj),
            in_specs=[pl.BlockSpec((tq, LANES), lambda h, b, j, g: (b * nj + j, EV_Q_BLK + h)),
                      pl.BlockSpec((t, LANES), lambda h, b, j, g: (b, EV_Q_BLK + h8 + h)),
                      pl.BlockSpec((t, LANES), lambda h, b, j, g: (b, EV_Q_BLK + 2 * h8 + h)),
                      pl.BlockSpec((l, LANES), lambda h, b, j, g: (b, EV_Q_BLK + h8 + h)),
                      pl.BlockSpec((l, LANES), lambda h, b, j, g: (b, EV_Q_BLK + 2 * h8 + h)),
                      pl.BlockSpec((None,) + tiles.shape[1:], lambda h, b, j, g: (h, 0, 0, 0))],
            out_specs=pl.BlockSpec((tq, LANES), lambda h, b, j, g: (b * nj + j, h)),
            scratch_shapes=[pltpu.VMEM((tq, NA_KROWS * GRID_W), F32)]),
        compiler_params=_cparams(3),
        name="na_attention",
    )(ids, z, z, z, z_c, z_c, tiles)


def _dense_attn_kernel(q_ref, k_ref, v_ref, o_ref, *, scale):
    o_ref[...] = _softmax_pv(q_ref[...], [(k_ref[...], None, v_ref[...])], scale)


def ctx_attention(z_c, batch, l):
    h8 = NA_HEADS
    return pl.pallas_call(
        functools.partial(_dense_attn_kernel, scale=NA_HEAD_DIM ** -0.5),
        out_shape=jax.ShapeDtypeStruct((batch * l, NA_WIDTH), F32),
        grid=(batch, h8),
        in_specs=[pl.BlockSpec((l, LANES), lambda b, h: (b, EV_Q_BLK + h)),
                  pl.BlockSpec((l, LANES), lambda b, h: (b, EV_Q_BLK + h8 + h)),
                  pl.BlockSpec((l, LANES), lambda b, h: (b, EV_Q_BLK + 2 * h8 + h))],
        out_specs=pl.BlockSpec((l, LANES), lambda b, h: (b, h)),
        compiler_params=_cparams(2),
        name="ctx_attention",
    )(z_c, z_c, z_c)


def _head_sum(x):
    first = lax.broadcasted_iota(jnp.int32, x.shape, 1) < RW_HEAD_DIM
    s0 = jnp.sum(jnp.where(first, x, 0.0), axis=-1, keepdims=True)
    s1 = jnp.sum(jnp.where(first, 0.0, x), axis=-1, keepdims=True)
    return jnp.where(first, s0, s1)


def _rw_terms_kernel(ur_ref, uk_ref, uv_ref, uwa_ref, pr_ref, pk_ref, pv_ref, pwa_ref, nr_ref, nk_ref, nv_ref,
                     nwa_ref, mu_ref, w0_ref, w2_ref, a0_ref, a2_ref, kk_ref, ka_ref, rk_ref,
                     rv_o, wk0_o, wk1_o, bk0_o, bk1_o, bonus_o, *, nt):
    i = pl.program_id(1)
    tm = ur_ref.shape[0]

    def shifted(u_ref, p_ref, n_ref, col0):
        u = u_ref[...]
        mu = mu_ref[:, col0:col0 + u.shape[1]]
        row = lax.broadcasted_iota(jnp.int32, u.shape, 0)
        before = jnp.where(i == 0, 0.0, p_ref[7:8, :])
        after = jnp.where(i == nt - 1, 0.0, n_ref[0:1, :])
        prev = jnp.where(row == 0, before, pltpu.roll(u, 1, axis=0))
        nxt = jnp.where(row == tm - 1, after, pltpu.roll(u, tm - 1, axis=0))
        return u + mu[0:1] * (prev - u) + mu[1:2] * (nxt - u)

    r_all = shifted(ur_ref, pr_ref, nr_ref, 0)
    k_all = shifted(uk_ref, pk_ref, nk_ref, RW_WIDTH)
    v_all = shifted(uv_ref, pv_ref, nv_ref, 2 * RW_WIDTH)
    wa = shifted(uwa_ref, pwa_ref, nwa_ref, 3 * RW_WIDTH)
    wd = jnp.tanh(wa[:, :LANES]).astype(BF16)
    ad = wa[:, LANES:].astype(BF16)
    first = lax.broadcasted_iota(jnp.int32, (tm, LANES), 1) < RW_HEAD_DIM

    def head_rows(x, y):
        return (jnp.where(first, x, pltpu.roll(y, RW_HEAD_DIM, axis=1)),
                jnp.where(first, pltpu.roll(x, RW_HEAD_DIM, axis=1), y))

    outs = (rv_o, wk0_o, wk1_o, bk0_o, bk1_o)
    even = [[] for _ in outs]
    odd = [[] for _ in outs]
    bonus = []
    for p in range(RW_WIDTH // LANES):
        sl = slice(p * LANES, (p + 1) * LANES)
        r, k, v = r_all[:, sl], k_all[:, sl], v_all[:, sl]
        kk = k * kk_ref[:, sl]
        kk = kk / jnp.maximum(jnp.sqrt(_head_sum(kk * kk)), 1e-12)
        kd_sum = jnp.zeros_like(k)
        per_dir = []
        for d in range(2):
            w = w0_ref[d:d + 1, sl] + jnp.dot(wd, w2_ref[d, :, sl].astype(BF16), preferred_element_type=F32)
            a = _sigmoid(a0_ref[d:d + 1, sl] + jnp.dot(ad, a2_ref[d, :, sl].astype(BF16), preferred_element_type=F32))
            kd = k * (1.0 + (a - 1.0) * ka_ref[:, sl])
            dec = jnp.exp(-RW_DECAY_SCALE * _sigmoid(w))
            per_dir.append(((dec, kd), (kk * a, kk)))
            kd_sum = kd_sum + kd
        pairs = [(r, v), per_dir[0][0], per_dir[1][0], per_dir[0][1], per_dir[1][1]]
        bonus.append(_head_sum(r * kd_sum * rk_ref[:, sl]) * v)
        for ev, od, (x, y) in zip(even, odd, pairs):
            e, o = head_rows(x, y)
            ev.append(e)
            od.append(o)
    for o_ref, ev, od in zip(outs, even, odd):
        o_ref[...] = jnp.swapaxes(jnp.stack(ev + od, axis=0), 0, 1)
    bonus_o[...] = jnp.swapaxes(jnp.stack(bonus, axis=0), 0, 1)


def rwkv_terms(z, n_seq, t, mu, w0, w2p, a0, a2p, k_k, k_a, r_k, tm=256):
    tm = min(tm, t)
    nt = t // tm
    npair = RW_WIDTH // LANES
    wide0 = EV_RW_BLK * LANES // RW_WIDTH
    wa_blk = (EV_RW_BLK + 3 * npair) // 2
    n8 = tm // 8

    def cur(width, cb):
        return pl.BlockSpec((tm, width), lambda b, i: (b * nt + i, cb))

    def before(width, cb):
        return pl.BlockSpec((8, width), lambda b, i: (jnp.maximum((b * nt + i) * n8 - 1, 0), cb))

    def after(width, cb):
        return pl.BlockSpec((8, width), lambda b, i: (jnp.minimum((b * nt + i + 1) * n8, n_seq * nt * n8 - 1), cb))

    cols = [(RW_WIDTH, wide0), (RW_WIDTH, wide0 + 1), (RW_WIDTH, wide0 + 2), (2 * LANES, wa_blk)]
    whole = lambda a: pl.BlockSpec(a.shape, lambda b, i: (0,) * a.ndim)
    params = (mu, w0, w2p, a0, a2p, k_k, k_a, r_k)
    out = pl.BlockSpec((tm, npair, LANES), lambda b, i: (i, b, 0))
    out2 = pl.BlockSpec((tm, 2 * npair, LANES), lambda b, i: (i, b, 0))
    return pl.pallas_call(
        functools.partial(_rw_terms_kernel, nt=nt),
        out_shape=[jax.ShapeDtypeStruct((t, n_seq * 2 * npair, LANES), F32)] * 5
        + [jax.ShapeDtypeStruct((t, n_seq * npair, LANES), F32)],
        grid=(n_seq, nt),
        in_specs=([cur(*c) for c in cols] + [before(*c) for c in cols] + [after(*c) for c in cols]
                  + [whole(a) for a in params]),
        out_specs=[out2] * 5 + [out],
        compiler_params=_cparams(2),
        name="rwkv_terms",
    )(*([z] * 12), *params)


def _to_lanes(n):
    q = jnp.concatenate([n, pltpu.roll(n, RW_HEAD_DIM, axis=1)], axis=0)
    return q.T[:RW_HEAD_DIM]


def _rw_scan_kernel(rvf, rvb, wk0, wk1, bk0, bk1, s0_ref, yf_ref, yb_ref, sfin_ref,
                    s_ref, z_ref, znext_ref, g_ref, ybuf_ref, *, chunk):
    n = RW_HEAD_DIM
    nb = rvf.shape[1] // (2 * RW_WIDTH // LANES)
    R, K, V, A, B = range(5)

    @pl.when(pl.program_id(0) == 0)
    def _():
        s_ref[...] = s0_ref[...]

    def convert(t_src):
        def pair_tiles(f, bw):
            both = (f[t_src], bw[chunk - 1 - t_src])
            x = jnp.concatenate([src[(b * 2 + par) * 8:(b * 2 + par) * 8 + 8]
                                 for par in range(2) for src in both for b in range(nb)], axis=0)
            xt = x.T
            return xt[:n], xt[n:]

        r, v = pair_tiles(rvf, rvb)
        w, k = pair_tiles(wk0, wk1)
        b, kk = pair_tiles(bk0, bk1)
        g_prev = g_ref[0]
        g = g_prev * w
        g_ref[0] = g
        inv = 1.0 / g
        znext_ref[R] = r * g
        znext_ref[K] = k * inv
        znext_ref[V] = v
        znext_ref[A] = -(kk * g_prev)
        znext_ref[B] = b * inv

    def emit_y(t):
        rows = _to_lanes(ybuf_ref[...])
        yf_ref[t] = rows[:n // 2]
        yb_ref[chunk - 1 - t] = rows[n // 2:]

    g_ref[0] = jnp.ones((n, LANES), F32)
    ybuf_ref[...] = jnp.zeros((n, LANES), F32)
    convert(0)

    def step(t, carry):
        emit_y(jnp.maximum(t - 1, 0))
        z_ref[...] = znext_ref[...]
        g_ref[1] = g_ref[0]
        convert(jnp.minimum(t + 1, chunk - 1))
        sa = [jnp.zeros((n, LANES), F32), jnp.zeros((n, LANES), F32)]
        for k in range(n):
            sa[k % 2] = sa[k % 2] + s_ref[k] * z_ref[A, k:k + 1, :]
        sa = sa[0] + sa[1]
        vt = z_ref[V]
        y = [jnp.zeros((n, LANES), F32), jnp.zeros((n, LANES), F32)]
        for k in range(n):
            sk = s_ref[k] + sa * z_ref[B, k:k + 1, :] + vt * z_ref[K, k:k + 1, :]
            s_ref[k] = sk
            y[k % 2] = y[k % 2] + sk * z_ref[R, k:k + 1, :]
        ybuf_ref[...] = y[0] + y[1]
        return carry

    lax.fori_loop(0, chunk, step, 0)
    emit_y(chunk - 1)
    for k in range(n):
        s_ref[k] = s_ref[k] * g_ref[1, k:k + 1, :]
    sfin_ref[...] = s_ref[...]


def rwkv_scan(rv, wk0, wk1, bk0, bk1, s0):
    t, nrow, _ = rv.shape
    n = RW_HEAD_DIM
    chunk = RW_SCAN_CHUNK
    nc = t // chunk
    assert nrow == LANES // 2, "the states of both directions fill the 128 lanes"
    fwd = pl.BlockSpec((chunk, nrow, LANES), lambda i: (i, 0, 0))
    bwd = pl.BlockSpec((chunk, nrow, LANES), lambda i: (nc - 1 - i, 0, 0))
    yfwd = pl.BlockSpec((chunk, nrow // 2, LANES), lambda i: (i, 0, 0))
    ybwd = pl.BlockSpec((chunk, nrow // 2, LANES), lambda i: (nc - 1 - i, 0, 0))
    state = pl.BlockSpec((n, n, LANES), lambda i: (0, 0, 0))
    slab = jax.ShapeDtypeStruct((t, nrow // 2, LANES), F32)
    return pl.pallas_call(
        functools.partial(_rw_scan_kernel, chunk=chunk),
        out_shape=[slab, slab, jax.ShapeDtypeStruct((n, n, LANES), F32)],
        grid=(nc,),
        in_specs=[fwd, bwd, fwd, bwd, fwd, bwd, state],
        out_specs=[yfwd, ybwd, state],
        scratch_shapes=[pltpu.VMEM((n, n, LANES), F32), pltpu.VMEM((5, n, LANES), F32),
                        pltpu.VMEM((5, n, LANES), F32), pltpu.VMEM((2, n, LANES), F32),
                        pltpu.VMEM((n, LANES), F32)],
        compiler_params=_cparams(1),
        name="rwkv_scan",
    )(rv, rv, wk0, wk1, bk0, bk1, s0)


def _rw_readout_kernel(yf_ref, yb_ref, bonus_ref, lw_ref, lb_ref, o_ref):
    npair = RW_WIDTH // LANES
    y_all = jnp.swapaxes(yf_ref[...] + yb_ref[...], 0, 1)
    bonus_all = jnp.swapaxes(bonus_ref[...], 0, 1)
    for q in range(yf_ref.shape[1]):
        b, p = divmod(q, npair)
        sl = slice(p * LANES, (p + 1) * LANES)
        y = y_all[q]
        mu = _head_sum(y) * (1.0 / RW_HEAD_DIM)
        d = y - mu
        var = _head_sum(d * d) * (1.0 / RW_HEAD_DIM)
        o_ref[b, :, sl] = d * lax.rsqrt(var + RW_GN_EPS) * lw_ref[:, sl] + lb_ref[:, sl] + bonus_all[q]


def rwkv_readout(y_f, y_b, bonus, ln_w, ln_b, tm=256):
    t, nq, _ = y_f.shape
    tm = min(tm, t)
    batch = nq // (RW_WIDTH // LANES)
    slab = pl.BlockSpec((tm, nq, LANES), lambda i: (i, 0, 0))
    vec = pl.BlockSpec((1, RW_WIDTH), lambda i: (0, 0))
    return pl.pallas_call(
        _rw_readout_kernel,
        out_shape=jax.ShapeDtypeStruct((batch, t, RW_WIDTH), F32),
        grid=(t // tm,),
        in_specs=[slab, slab, slab, vec, vec],
        out_specs=pl.BlockSpec((batch, tm, RW_WIDTH), lambda i: (0, i, 0)),
        compiler_params=_cparams(1),
        name="rwkv_readout",
    )(y_f, y_b, bonus, ln_w.reshape(1, -1), ln_b.reshape(1, -1))


def _rms(x, w):
    return x * lax.rsqrt(jnp.mean(x * x, axis=-1, keepdims=True) + NORM_EPS) * w


def _rope(x, cos, sin):
    odd = (lax.broadcasted_iota(jnp.int32, x.shape, 1) & 1) == 1
    swapped = jnp.where(odd, pltpu.roll(x, 1, axis=1), pltpu.roll(x, LANES - 1, axis=1))
    return x * cos + swapped * sin


def _mla_q_kernel(c_ref, nw_ref, w_ref, cos_ref, sin_ref, q_ref):
    h = _rms(c_ref[...], nw_ref[...]).astype(BF16)
    acc = jnp.dot(h, w_ref[...], preferred_element_type=F32)
    cos, sin = cos_ref[...], sin_ref[...]
    for hd in range(MLA_HEADS):
        lo = hd * MLA_QK_PAD
        q_ref[:, lo:lo + MLA_NOPE] = acc[:, lo:lo + MLA_NOPE].astype(BF16)
        q_ref[:, lo + MLA_NOPE:lo + MLA_QK_PAD] = _rope(acc[:, lo + MLA_NOPE:lo + MLA_QK_PAD], cos, sin).astype(BF16)


def mla_queries(z, q_norm, w_uq_pad, cos, sin, t, tm=512):
    m = z.shape[0]
    tm = min(tm, t)
    nt = t // tm
    width = MLA_HEADS * MLA_QK_PAD
    return pl.pallas_call(
        _mla_q_kernel,
        out_shape=jax.ShapeDtypeStruct((m, width), BF16),
        grid=(m // tm,),
        in_specs=[pl.BlockSpec((tm, MLA_Q_RANK), lambda i: (i, GATE_COLS // MLA_Q_RANK)),
                  pl.BlockSpec((1, MLA_Q_RANK), lambda i: (0, 0)),
                  pl.BlockSpec((MLA_Q_RANK, width), lambda i: (0, 0)),
                  pl.BlockSpec((tm, LANES), lambda i: (i % nt, 0)),
                  pl.BlockSpec((tm, LANES), lambda i: (i % nt, 0))],
        out_specs=pl.BlockSpec((tm, width), lambda i: (i, 0)),
        compiler_params=_cparams(1),
        name="mla_queries",
    )(z, q_norm.reshape(1, -1), w_uq_pad, cos, sin)


def _mla_kv_kernel(c_ref, pe_ref, nw_ref, w_ref, cos_ref, sin_ref, k_ref, v_ref):
    h = _rms(c_ref[...], nw_ref[...]).astype(BF16)
    acc = jnp.dot(h, w_ref[...], preferred_element_type=F32)
    pe = _rope(pe_ref[...], cos_ref[...], sin_ref[...]).astype(BF16)
    for hd in range(MLA_HEADS):
        src = hd * (MLA_NOPE + MLA_V)
        dst = hd * MLA_QK_PAD
        k_ref[:, dst:dst + MLA_NOPE] = acc[:, src:src + MLA_NOPE].astype(BF16)
        k_ref[:, dst + MLA_NOPE:dst + MLA_QK_PAD] = pe
        v_ref[:, hd * MLA_V:(hd + 1) * MLA_V] = acc[:, src + MLA_NOPE:src + MLA_NOPE + MLA_V].astype(BF16)


def mla_keys_values(z, kv_norm, w_ukv, cos, sin, t, tm=512):
    m = z.shape[0]
    tm = min(tm, t)
    nt = t // tm
    return pl.pallas_call(
        _mla_kv_kernel,
        out_shape=[jax.ShapeDtypeStruct((m, MLA_HEADS * MLA_QK_PAD), BF16),
                   jax.ShapeDtypeStruct((m, MLA_WIDTH), BF16)],
        grid=(m // tm,),
        in_specs=[pl.BlockSpec((tm, MLA_KV_RANK), lambda i: (i, (GATE_COLS + MLA_Q_RANK) // MLA_KV_RANK)),
                  pl.BlockSpec((tm, LANES), lambda i: (i, OD_A_BLK + (MLA_Q_RANK + MLA_KV_RANK) // LANES)),
                  pl.BlockSpec((1, MLA_KV_RANK), lambda i: (0, 0)),
                  pl.BlockSpec(w_ukv.shape, lambda i: (0, 0)),
                  pl.BlockSpec((tm, LANES), lambda i: (i % nt, 0)),
                  pl.BlockSpec((tm, LANES), lambda i: (i % nt, 0))],
        out_specs=[pl.BlockSpec((tm, MLA_HEADS * MLA_QK_PAD), lambda i: (i, 0)),
                   pl.BlockSpec((tm, MLA_WIDTH), lambda i: (i, 0))],
        compiler_params=_cparams(1),
        name="mla_keys_values",
    )(z, z, kv_norm.reshape(1, -1), w_ukv, cos, sin)


def _mla_attn_kernel(q_ref, k_ref, v_ref, kc_ref, vc_ref, o_ref, *, scale):
    parts = [(kc_ref[...], None, vc_ref[...])]
    nkeys = k_ref.shape[0]
    step = min(MLA_KEY_CHUNK, nkeys)
    for lo in range(0, nkeys, step):
        parts.append((k_ref[lo:lo + step, :], None, v_ref[lo:lo + step, :]))
    o_ref[...] = _softmax_pv(q_ref[...], parts, scale)


def mla_attention(q, k, v, kc, vc, batch, t, l, tq=1024):
    tq = min(tq, t)
    nq = t // tq
    return pl.pallas_call(
        functools.partial(_mla_attn_kernel, scale=(MLA_NOPE + MLA_ROPE) ** -0.5),
        out_shape=jax.ShapeDtypeStruct((batch * t, MLA_WIDTH), F32),
        grid=(batch, MLA_HEADS, nq),
        in_specs=[pl.BlockSpec((tq, MLA_QK_PAD), lambda b, h, i: (b * nq + i, h)),
                  pl.BlockSpec((t, MLA_QK_PAD), lambda b, h, i: (b, h)),
                  pl.BlockSpec((t, MLA_V), lambda b, h, i: (b, h)),
                  pl.BlockSpec((l, MLA_QK_PAD), lambda b, h, i: (b, h)),
                  pl.BlockSpec((l, MLA_V), lambda b, h, i: (b, h))],
        out_specs=pl.BlockSpec((tq, MLA_V), lambda b, h, i: (b * nq + i, h)),
        compiler_params=_cparams(3),
        name="mla_attention",
    )(q, k, v, kc, vc)


def _rope_tables(t):
    tok = np.arange(t)
    pos = np.stack([tok // GRID_W, tok % GRID_W], axis=-1).astype(np.float32)
    n_freq = MLA_ROPE // 4
    inv = (ROPE_BASE ** (-jnp.arange(n_freq, dtype=F32) / n_freq))
    ang = (jnp.asarray(pos)[:, :, None] * inv).reshape(t, MLA_ROPE // 2)
    cos = jnp.repeat(jnp.cos(ang), 2, axis=-1)
    sin = jnp.repeat(jnp.sin(ang), 2, axis=-1) * jnp.tile(jnp.array([-1.0, 1.0], F32), MLA_ROPE // 2)
    pad = ((0, 0), (0, LANES - MLA_ROPE))
    return jnp.pad(cos, pad), jnp.pad(sin, pad)


def _split3(x):
    hi = x.astype(BF16)
    r1 = x - hi.astype(F32)
    mid = r1.astype(BF16)
    lo = (r1 - mid.astype(F32)).astype(BF16)
    return hi, mid, lo


def _hg_kernel(q_ref, f_ref, v_ref, lb_ref, s0_ref, *rest, reverse, finalize, emit_state):
    st_ref = rest[-1]
    rest = rest[:-1]
    if finalize:
        prev_ref, nw_ref = rest[0], rest[1]
        rest = rest[2:]
    o_ref = rest[0]

    @pl.when(pl.program_id(2) == 0)
    def _():
        st_ref[...] = s0_ref[...]

    bt = q_ref.shape[0]
    nchunk = bt // HG_CHUNK
    order = range(nchunk - 1, -1, -1) if reverse else range(nchunk)
    ri = lax.broadcasted_iota(jnp.int32, (bt, bt), 0)
    ci = lax.broadcasted_iota(jnp.int32, (bt, bt), 1)
    same = (ri // HG_CHUNK) == (ci // HG_CHUNK)
    tri = same & ((ci >= ri) if reverse else (ci <= ri))
    tri_f = tri.astype(F32)
    tri_b = tri.astype(BF16)
    row_chunk = lax.broadcasted_iota(jnp.int32, (bt, HG_DK), 0) // HG_CHUNK
    in_chunk = [row_chunk == c for c in range(nchunk)]

    lb = lb_ref[...]
    fz = f_ref[...]
    log_sig = jnp.minimum(fz, 0.0) - jnp.log1p(jnp.exp(-jnp.abs(fz)))
    la = jnp.log(lb)
    lbb = jnp.log1p(-lb) + log_sig
    log_f = jnp.maximum(la, lbb) + jnp.log1p(jnp.exp(-jnp.abs(la - lbb)))
    kf_all = (1.0 - lb) * _sigmoid(-fz)
    pieces = jnp.dot(tri_b, jnp.concatenate(_split3(log_f), axis=1), preferred_element_type=F32)
    width = log_f.shape[1]
    cum_all = pieces[:, :width] + pieces[:, width:2 * width] + pieces[:, 2 * width:]

    for hh in range(HG_STEP_HEADS):
        sl = slice(hh * HG_DK, (hh + 1) * HG_DK)
        cum, kf, v = cum_all[:, sl], kf_all[:, sl], v_ref[:, sl]
        last = [cum[c * HG_CHUNK:c * HG_CHUNK + 1] if reverse else cum[(c + 1) * HG_CHUNK - 1:(c + 1) * HG_CHUNK]
                for c in range(nchunk)]
        tot = jnp.concatenate([jnp.broadcast_to(x, (HG_CHUNK, HG_DK)) for x in last], axis=0)
        q_in = q_ref[:, sl] * jnp.exp(cum)
        k_in = kf * jnp.exp(-cum)
        k_end = kf * jnp.exp(tot - cum)
        o_intra = _dot(_dot_nt(q_in, k_in) * tri_f, v)
        v_by_chunk = jnp.concatenate([jnp.where(in_chunk[c], v, 0.0) for c in range(nchunk)], axis=1)
        d_st = _dot_tn(v_by_chunk, k_end)
        st = st_ref[hh]
        before = [None] * nchunk
        for c in order:
            before[c] = st
            st = st * jnp.exp(last[c]) + d_st[c * HG_DV:(c + 1) * HG_DV]
        st_ref[hh] = st
        q_by_chunk = jnp.concatenate([jnp.where(in_chunk[c], q_in, 0.0) for c in range(nchunk)], axis=1)
        o = o_intra + _dot_nt(q_by_chunk, jnp.concatenate(before, axis=1))
        if finalize:
            o = o + prev_ref[:, sl]
            o = o * lax.rsqrt(jnp.mean(o * o, axis=-1, keepdims=True) + NORM_EPS) * nw_ref[...]
        o_ref[:, sl] = o
        if emit_state:
            rest[1][hh] = st


def hgrn2_pass(z, lb, s0, batch, t, f_off, reverse, prev=None, norm_w=None, emit_state=False):
    bt = min(HG_BLOCK, t)
    nblk = t // bt
    nh = HG_STEP_HEADS
    wide = nh * LANES
    finalize = prev is not None
    assert OD_HG_BLK % nh == 0 and HG_HEADS % nh == 0, "head groups must start on a whole column block"
    pos = (lambda i: nblk - 1 - i) if reverse else (lambda i: i)
    blk = lambda off: pl.BlockSpec((bt, wide), lambda b, h, i, off=off: (b * nblk + pos(i), off // nh + h))
    state = pl.BlockSpec((None, nh, HG_DV, HG_DK), lambda b, h, i: (b, h, 0, 0))
    ins = [z, z, z, lb.reshape(1, -1), s0]
    specs = [blk(OD_HG_BLK), blk(OD_HG_BLK + f_off), blk(OD_HG_BLK + 3 * HG_HEADS),
             pl.BlockSpec((1, wide), lambda b, h, i: (0, h)), state]
    if finalize:
        ins += [prev, norm_w.reshape(1, -1)]
        specs += [blk(0), pl.BlockSpec((1, HG_DV), lambda b, h, i: (0, 0))]
    out_shape = [jax.ShapeDtypeStruct((batch * t, HG_WIDTH), F32)]
    out_specs = [blk(0)]
    if emit_state:
        out_shape.append(jax.ShapeDtypeStruct((batch, HG_HEADS, HG_DV, HG_DK), F32))
        out_specs.append(state)
    return pl.pallas_call(
        functools.partial(_hg_kernel, reverse=reverse, finalize=finalize, emit_state=emit_state),
        out_shape=out_shape,
        grid=(batch, HG_HEADS // nh, nblk),
        in_specs=specs,
        out_specs=out_specs,
        scratch_shapes=[pltpu.VMEM((nh, HG_DV, HG_DK), F32)],
        compiler_params=_cparams(3),
        name="hgrn2_" + ("bwd" if reverse else "fwd"),
    )(*ins)


def _even_layer(x, xc, mod, norm_w, w_in, w_out, rpb, rw_params, batch, t, l):
    d = x.shape[1]
    s2 = 3 * NA_WIDTH + RW_SHIFT_COLS
    zero = jnp.zeros((d, RW_PAD_COLS - RW_SHIFT_COLS), BF16)
    w_in = w_in.astype(BF16)
    w_ext = jnp.concatenate([w_in[:, s2:], w_in[:, :s2], zero], axis=1)
    tiles_per_seq = max(t // 1024, 1)
    z = in_projection(x, norm_w, mod, lambda i: i // tiles_per_seq, w_ext, "even_in_proj")
    z_c = in_projection(xc, norm_w, mod, lambda i: batch, w_ext, "even_in_proj_ctx")

    y_na = na_attention(z, z_c, rpb, batch, t, l)
    yc_na = ctx_attention(z_c, batch, l)

    mu, w0, w2, a0, a2, k_k, k_a, r_k, ln_w, ln_b = rw_params
    mu = jnp.pad(mu, ((0, 0), (0, RW_PAD_COLS - RW_SHIFT_COLS)))
    zpad = jnp.zeros((RW_RANK, RW_WIDTH), F32)
    w2p = jnp.stack([jnp.concatenate([w2[0], zpad]), jnp.concatenate([zpad, w2[1]])])
    a2p = jnp.stack([jnp.concatenate([a2[0], zpad]), jnp.concatenate([zpad, a2[1]])])
    vecs = (mu, w0, w2p, a0, a2p, k_k.reshape(1, -1), k_a.reshape(1, -1), r_k.reshape(1, -1))
    terms = rwkv_terms(z, batch, t, *vecs)
    terms_c = rwkv_terms(z_c, batch, l, *vecs)
    s_zero = jnp.zeros((RW_HEAD_DIM, RW_HEAD_DIM, LANES), F32)
    yc_f, yc_b, s_ctx = rwkv_scan(*terms_c[:5], s_zero)
    y_f, y_b, _ = rwkv_scan(*terms[:5], s_ctx)
    y_rw = rwkv_readout(y_f, y_b, terms[5], ln_w, ln_b).reshape(batch * t, RW_WIDTH)
    yc_rw = rwkv_readout(yc_f, yc_b, terms_c[5], ln_w, ln_b).reshape(batch * l, RW_WIDTH)

    w_out = w_out.astype(BF16)
    rows_per_seq = max(t // 256, 1)
    x = out_projection(y_na, y_rw, z, w_out, x, mod, lambda i: i // rows_per_seq, "even_out_proj")
    xc = out_projection(yc_na, yc_rw, z_c, w_out, xc, mod, lambda i: batch, "even_out_proj_ctx")
    return x, xc


def _odd_layer(x, xc, mod, norm_w, w_in, w_out, q_norm, w_uq, kv_norm, w_ukv, lb, hg_norm_w, final_w,
               batch, t, l):
    d = x.shape[1]
    o3 = MLA_Q_RANK + MLA_KV_RANK + MLA_ROPE
    o4 = o3 + 4 * HG_FDIM
    zero = jnp.zeros((d, MLA_A_COLS - o3), BF16)
    w_in = w_in.astype(BF16)
    w_ext = jnp.concatenate([w_in[:, o4:], w_in[:, :o3], zero, w_in[:, o3:o4]], axis=1)
    tiles_per_seq = max(t // 1024, 1)
    z = in_projection(x, norm_w, mod, lambda i: i // tiles_per_seq, w_ext, "odd_in_proj")
    z_c = in_projection(xc, norm_w, mod, lambda i: batch, w_ext, "odd_in_proj_ctx")

    cos, sin = _rope_tables(t)
    qk = MLA_NOPE + MLA_ROPE
    w_uq_pad = jnp.pad(w_uq.reshape(MLA_Q_RANK, MLA_HEADS, qk), ((0, 0), (0, 0), (0, MLA_QK_PAD - qk)))
    w_uq_pad = w_uq_pad.reshape(MLA_Q_RANK, MLA_HEADS * MLA_QK_PAD).astype(BF16)
    w_ukv = w_ukv.astype(BF16)
    q = mla_queries(z, q_norm, w_uq_pad, cos, sin, t)
    k, v = mla_keys_values(z, kv_norm, w_ukv, cos, sin, t)
    ones = jnp.ones((l, LANES), F32)
    kc, vc = mla_keys_values(z_c, kv_norm, w_ukv, ones, jnp.zeros_like(ones), l)
    y_mla = mla_attention(q, k, v, kc, vc, batch, t, l)

    s_zero = jnp.zeros((batch, HG_HEADS, HG_DV, HG_DK), F32)
    _, s_f = hgrn2_pass(z_c, lb, s_zero, batch, l, HG_HEADS, False, emit_state=True)
    _, s_b = hgrn2_pass(z_c, lb, s_zero, batch, l, 2 * HG_HEADS, True, emit_state=True)
    o_f, = hgrn2_pass(z, lb, s_f, batch, t, HG_HEADS, False)
    y_hg, = hgrn2_pass(z, lb, s_b, batch, t, 2 * HG_HEADS, True, prev=o_f, norm_w=hg_norm_w)

    rows_per_seq = max(t // 256, 1)
    return out_projection(y_mla, y_hg, z, w_out.astype(BF16), x, mod, lambda i: i // rows_per_seq,
                          "odd_out_proj", final_w=final_w)


def kernel(x, c, ctx, c_ctx, ada_w, ada_b, norm_w, e_w_in, e_w_out, na_rpb, rw_mu, rw_w0, rw_w2, rw_a0, rw_a2, rw_k_k, rw_k_a, rw_r_k, rw_ln_w, rw_ln_b, o_w_in, o_w_out, mla_q_norm, mla_w_uq, mla_kv_norm, mla_w_ukv, hg_lower_bounds, hg_norm_w, final_norm_w):
    batch, t, d = x.shape
    l = ctx.shape[1]
    assert ada_w.shape[0] == 2, "one even and one odd layer"
    cond = jnp.concatenate([c, c_ctx[None, :], jnp.zeros((8 - batch - 1, d), F32)], axis=0)
    mod = modulation(cond, ada_w, ada_b)
    s = jax.nn.softmax(hg_lower_bounds.astype(F32), axis=0)
    lower = jnp.cumsum(s, axis=0) - s[0]

    xf, xcf = x.reshape(batch * t, d), ctx.reshape(batch * l, d)
    rw_params = (rw_mu[0], rw_w0[0], rw_w2[0], rw_a0[0], rw_a2[0], rw_k_k[0], rw_k_a[0], rw_r_k[0],
                 rw_ln_w[0], rw_ln_b[0])
    xf, xcf = _even_layer(xf, xcf, mod[0].reshape(8, 1, 3 * d), norm_w[0], e_w_in[0], e_w_out[0], na_rpb[0],
                          rw_params, batch, t, l)
    out = _odd_layer(xf, xcf, mod[1].reshape(8, 1, 3 * d), norm_w[1], o_w_in[0], o_w_out[0], mla_q_norm[0],
                     mla_w_uq[0], mla_kv_norm[0], mla_w_ukv[0], lower[1], hg_norm_w[0], final_norm_w,
                     batch, t, l)
    return out.reshape(batch, t, d)
```

```python
import functools
import math

import numpy as np
import jax
import jax.numpy as jnp
from jax import lax
from jax.experimental import pallas as pl
from jax.experimental.pallas import tpu as pltpu

F32 = jnp.float32
BF16 = jnp.bfloat16

GRID_W = 64
NORM_EPS = 1e-6
ROPE_BASE = 10000.0

NA_HEADS = 8
NA_HEAD_DIM = 128
NA_WIDTH = NA_HEADS * NA_HEAD_DIM
NA_WIN_ROWS = 8
NA_WIN_COLS = 16
NA_QROWS = 4
NA_KROWS = 12
NA_STEP_HEADS = 2

RW_HEAD_DIM = 64
RW_HEADS = 16
RW_WIDTH = RW_HEADS * RW_HEAD_DIM
RW_RANK = 64
RW_SHIFT_COLS = 3 * RW_WIDTH + 4 * RW_RANK
RW_PAD_COLS = 3584
RW_GN_EPS = 64e-5
RW_SCAN_CHUNK = 64
RW_DECAY_SCALE = math.exp(-0.5)
RW_STATE_PAD = 8

MLA_HEADS = 8
MLA_Q_RANK = 512
MLA_KV_RANK = 512
MLA_NOPE = 128
MLA_ROPE = 64
MLA_V = 128
MLA_QK_PAD = 256
MLA_WIDTH = MLA_HEADS * MLA_V
MLA_A_COLS = 1536
MLA_KEY_CHUNK = 512

HG_HEADS = 8
HG_DK = 128
HG_DV = 128
HG_FDIM = HG_HEADS * HG_DK
HG_WIDTH = HG_HEADS * HG_DV
HG_CHUNK = 32
HG_BLOCK = 256
HG_STEP_HEADS = 4

LANES = 128

GATE_COLS = 2048
EV_Q_BLK = GATE_COLS // LANES
EV_RW_BLK = EV_Q_BLK + 3 * NA_WIDTH // LANES
OD_A_BLK = GATE_COLS // LANES
OD_HG_BLK = OD_A_BLK + MLA_A_COLS // LANES
VMEM_LIMIT = 48 * 1024 * 1024
NEG_INF = -1e30
LOG2E = math.log2(math.e)


def _cparams(n_axes):
    return pltpu.CompilerParams(dimension_semantics=("arbitrary",) * n_axes, vmem_limit_bytes=VMEM_LIMIT)


def _sigmoid(x):
    return 1.0 / (1.0 + jnp.exp(-x))


def _silu(x):
    return x * _sigmoid(x)


def _dot(a, b):
    return jnp.dot(a.astype(BF16), b.astype(BF16), preferred_element_type=F32)


def _dot_nt(a, b):
    return lax.dot_general(a.astype(BF16), b.astype(BF16), (((1,), (1,)), ((), ())), preferred_element_type=F32)


def _dot_tn(a, b):
    return lax.dot_general(a.astype(BF16), b.astype(BF16), (((0,), (0,)), ((), ())), preferred_element_type=F32)


def _mod_kernel(c_ref, w_ref, b_ref, o_ref):
    o_ref[...] = _dot(_silu(c_ref[...]), w_ref[...]) + b_ref[...]


def modulation(cond, ada_w, ada_b, tn=512):
    depth, d, n = ada_w.shape
    return pl.pallas_call(
        _mod_kernel,
        out_shape=jax.ShapeDtypeStruct((depth, 8, n), F32),
        grid=(depth, n // tn),
        in_specs=[pl.BlockSpec((8, d), lambda l, j: (0, 0)),
                  pl.BlockSpec((None, d, tn), lambda l, j: (l, 0, j)),
                  pl.BlockSpec((None, 1, tn), lambda l, j: (l, 0, j))],
        out_specs=pl.BlockSpec((None, 8, tn), lambda l, j: (l, 0, j)),
        compiler_params=_cparams(2),
        name="adaln_modulation",
    )(cond, ada_w, ada_b.reshape(depth, 1, n))


def _inproj_kernel(x_ref, nw_ref, shift_ref, scale_ref, w_ref, o_ref, h_ref):
    @pl.when(pl.program_id(1) == 0)
    def _():
        x = x_ref[...]
        y = x * lax.rsqrt(jnp.mean(x * x, axis=-1, keepdims=True) + NORM_EPS) * nw_ref[...]
        h_ref[...] = (y * (1.0 + scale_ref[...]) + shift_ref[...]).astype(BF16)

    o_ref[...] = jnp.dot(h_ref[...], w_ref[...], preferred_element_type=F32)


def in_projection(x, norm_w, mod, mod_row, w, name, tm=1024, tn=512):
    m, d = x.shape
    n = w.shape[1]
    tm = min(tm, m)
    return pl.pallas_call(
        _inproj_kernel,
        out_shape=jax.ShapeDtypeStruct((m, n), F32),
        grid=(m // tm, n // tn),
        in_specs=[pl.BlockSpec((tm, d), lambda i, j: (i, 0)),
                  pl.BlockSpec((1, d), lambda i, j: (0, 0)),
                  pl.BlockSpec((None, 1, d), lambda i, j: (mod_row(i), 0, 0)),
                  pl.BlockSpec((None, 1, d), lambda i, j: (mod_row(i), 0, 1)),
                  pl.BlockSpec((d, tn), lambda i, j: (0, j))],
        out_specs=pl.BlockSpec((tm, tn), lambda i, j: (i, j)),
        scratch_shapes=[pltpu.VMEM((tm, d), BF16)],
        compiler_params=_cparams(2),
        name=name,
    )(x, norm_w.reshape(1, d), mod, mod, w)


def _outproj_kernel(ya_ref, yb_ref, g_ref, w_ref, x_ref, gm_ref, *rest, ka, final):
    o_ref = rest[-1]
    sg = _silu(g_ref[...])
    acc = _dot(ya_ref[...].astype(F32) * sg[:, :ka], w_ref[:ka, :])
    acc += _dot(yb_ref[...].astype(F32) * sg[:, ka:], w_ref[ka:, :])
    out = x_ref[...] + gm_ref[...] * acc
    if final:
        out = out * lax.rsqrt(jnp.mean(out * out, axis=-1, keepdims=True) + NORM_EPS) * rest[0][...]
    o_ref[...] = out


def out_projection(ya, yb, gate, w, x, mod, mod_row, name, final_w=None, tm=256):
    m, d = x.shape
    ka, kb = ya.shape[1], yb.shape[1]
    tm = min(tm, m)
    ins = [ya, yb, gate, w, x, mod]
    specs = [pl.BlockSpec((tm, ka), lambda i: (i, 0)),
             pl.BlockSpec((tm, kb), lambda i: (i, 0)),
             pl.BlockSpec((tm, ka + kb), lambda i: (i, 0)),
             pl.BlockSpec((ka + kb, d), lambda i: (0, 0)),
             pl.BlockSpec((tm, d), lambda i: (i, 0)),
             pl.BlockSpec((None, 1, d), lambda i: (mod_row(i), 0, 2))]
    if final_w is not None:
        ins.append(final_w.reshape(1, d))
        specs.append(pl.BlockSpec((1, d), lambda i: (0, 0)))
    return pl.pallas_call(
        functools.partial(_outproj_kernel, ka=ka, final=final_w is not None),
        out_shape=jax.ShapeDtypeStruct((m, d), F32),
        grid=(m // tm,),
        in_specs=specs,
        out_specs=pl.BlockSpec((tm, d), lambda i: (i, 0)),
        compiler_params=_cparams(1),
        name=name,
    )(*ins)


def _softmax_pv(q, parts, scale):
    c = scale * LOG2E
    m = denom = num = None
    for k, bias, v in parts:
        x = _dot_nt(q, k) * c
        if bias is not None:
            x = x + bias
        m_part = jnp.max(x, axis=-1, keepdims=True)
        if m is None:
            m = m_part
            p = jnp.exp2(x - m)
            denom = jnp.sum(p, axis=-1, keepdims=True)
            num = _dot(p, v)
        else:
            m_new = jnp.maximum(m, m_part)
            alpha = jnp.exp2(m - m_new)
            p = jnp.exp2(x - m_new)
            denom = denom * alpha + jnp.sum(p, axis=-1, keepdims=True)
            num = num * alpha + _dot(p, v)
            m = m_new
    return num / denom


def _softmax_pv_joint(q, parts, scale):
    c = scale * LOG2E
    logits = [_dot_nt(q, k) * c if bias is None else _dot_nt(q, k) * c + bias for k, bias, _ in parts]
    m = functools.reduce(jnp.maximum, [jnp.max(x, axis=-1, keepdims=True) for x in logits])
    ps = [jnp.exp2(x - m) for x in logits]
    denom = functools.reduce(lambda a, b: a + b, [jnp.sum(p, axis=-1, keepdims=True) for p in ps])
    num = functools.reduce(lambda a, b: a + b, [_dot(p, v) for p, (_, _, v) in zip(ps, parts)])
    return num / denom


def _na_kernel(idx_ref, q_ref, k_ref, v_ref, kc_ref, vc_ref, tab_ref, o_ref, bias_ref, *, rows, scale):
    j = pl.program_id(2)
    npair = NA_KROWS // 2
    start = pl.multiple_of(jnp.clip(NA_QROWS * j - NA_WIN_ROWS // 2, 0, rows - NA_KROWS) * GRID_W, GRID_W)
    win = pl.ds(start, NA_KROWS * GRID_W)
    for hd in range(NA_STEP_HEADS):
        sl = slice(hd * LANES, (hd + 1) * LANES)
        for qr in range(NA_QROWS):
            for m in range(npair):
                tile = tab_ref[hd, idx_ref[(j * NA_QROWS + qr) * npair + m]]
                bias_ref[hd, qr * GRID_W:(qr + 1) * GRID_W, m * LANES:(m + 1) * LANES] = tile
        parts = [(k_ref[win, sl], bias_ref[hd], v_ref[win, sl]), (kc_ref[:, sl], None, vc_ref[:, sl])]
        o_ref[:, sl] = _softmax_pv_joint(q_ref[:, sl], parts, scale)


def _na_tile_ids(rows):
    kh = min(NA_WIN_ROWS, rows)
    n = 2 * NA_WIN_ROWS - 1
    ids = []
    for j in range(rows // NA_QROWS):
        ks = int(np.clip(NA_QROWS * j - NA_WIN_ROWS // 2, 0, rows - NA_KROWS))
        for qr in range(NA_QROWS):
            r = NA_QROWS * j + qr
            r0 = int(np.clip(r - kh // 2, 0, rows - kh))
            for m in range(NA_KROWS // 2):
                kr = ks + 2 * m
                dy = kr - r + NA_WIN_ROWS - 1
                first, second = r0 <= kr < r0 + kh, r0 <= kr + 1 < r0 + kh
                ids.append(dy if first and second else n + dy if first else 2 * n + dy + 1 if second else 3 * n)
    return np.array(ids, np.int32)


def _na_bias_tiles(rpb):
    col = np.arange(GRID_W)
    cs = np.clip(col - NA_WIN_COLS // 2, 0, GRID_W - NA_WIN_COLS)
    kc = col[None, :]
    in_win = (kc >= cs[:, None]) & (kc < cs[:, None] + NA_WIN_COLS)
    dx = np.where(in_win, kc - col[:, None] + NA_WIN_COLS - 1, 0)
    tab = jnp.where(in_win[None, None], rpb[:, :, dx] * LOG2E, NEG_INF)
    neg = jnp.full_like(tab[:, :1], NEG_INF)
    nxt = jnp.concatenate([tab[:, 1:], neg], axis=1)
    both = jnp.concatenate([tab, nxt], axis=-1)
    first = jnp.concatenate([tab, jnp.broadcast_to(neg, tab.shape)], axis=-1)
    second = jnp.concatenate([jnp.broadcast_to(neg, tab.shape), tab], axis=-1)
    return jnp.concatenate([both, first, second, jnp.concatenate([neg, neg], axis=-1)], axis=1)


def na_attention(z, z_c, rpb, batch, t, l):
    rows = t // GRID_W
    nj = rows // NA_QROWS
    tq = NA_QROWS * GRID_W
    ids = jnp.asarray(_na_tile_ids(rows))
    tiles = _na_bias_tiles(rpb)
    nh = NA_STEP_HEADS
    wide = nh * LANES
    q0, k0, v0 = (EV_Q_BLK + i * NA_HEADS for i in range(3))
    assert q0 % nh == 0 and NA_HEADS % nh == 0, "head groups must start on a whole column block"
    return pl.pallas_call(
        functools.partial(_na_kernel, rows=rows, scale=NA_HEAD_DIM ** -0.5),
        out_shape=jax.ShapeDtypeStruct((batch * t, NA_WIDTH), F32),
        grid_spec=pltpu.PrefetchScalarGridSpec(
            num_scalar_prefetch=1,
            grid=(NA_HEADS // nh, batch, nj),
            in_specs=[pl.BlockSpec((tq, wide), lambda h, b, j, g: (b * nj + j, q0 // nh + h)),
                      pl.BlockSpec((t, wide), lambda h, b, j, g: (b, k0 // nh + h)),
                      pl.BlockSpec((t, wide), lambda h, b, j, g: (b, v0 // nh + h)),
                      pl.BlockSpec((l, wide), lambda h, b, j, g: (b, k0 // nh + h)),
                      pl.BlockSpec((l, wide), lambda h, b, j, g: (b, v0 // nh + h)),
                      pl.BlockSpec((nh,) + tiles.shape[1:], lambda h, b, j, g: (h, 0, 0, 0))],
            out_specs=pl.BlockSpec((tq, wide), lambda h, b, j, g: (b * nj + j, h)),
            scratch_shapes=[pltpu.VMEM((nh, tq, NA_KROWS * GRID_W), F32)]),
        compiler_params=_cparams(3),
        name="na_attention",
    )(ids, z, z, z, z_c, z_c, tiles)


def _dense_attn_kernel(q_ref, k_ref, v_ref, o_ref, *, scale):
    o_ref[...] = _softmax_pv(q_ref[...], [(k_ref[...], None, v_ref[...])], scale)


def ctx_attention(z_c, batch, l):
    h8 = NA_HEADS
    return pl.pallas_call(
        functools.partial(_dense_attn_kernel, scale=NA_HEAD_DIM ** -0.5),
        out_shape=jax.ShapeDtypeStruct((batch * l, NA_WIDTH), F32),
        grid=(batch, h8),
        in_specs=[pl.BlockSpec((l, LANES), lambda b, h: (b, EV_Q_BLK + h)),
                  pl.BlockSpec((l, LANES), lambda b, h: (b, EV_Q_BLK + h8 + h)),
                  pl.BlockSpec((l, LANES), lambda b, h: (b, EV_Q_BLK + 2 * h8 + h))],
        out_specs=pl.BlockSpec((l, LANES), lambda b, h: (b, h)),
        compiler_params=_cparams(2),
        name="ctx_attention",
    )(z_c, z_c, z_c)


def _head_sum(x):
    first = lax.broadcasted_iota(jnp.int32, x.shape, 1) < RW_HEAD_DIM
    s0 = jnp.sum(jnp.where(first, x, 0.0), axis=-1, keepdims=True)
    s1 = jnp.sum(jnp.where(first, 0.0, x), axis=-1, keepdims=True)
    return jnp.where(first, s0, s1)


def _rw_terms_kernel(ur_ref, uk_ref, uv_ref, uwa_ref, pr_ref, pk_ref, pv_ref, pwa_ref, nr_ref, nk_ref, nv_ref,
                     nwa_ref, mu_ref, w0_ref, w2_ref, a0_ref, a2_ref, kk_ref, ka_ref, rk_ref,
                     rv_o, wk0_o, wk1_o, bk0_o, bk1_o, bonus_o, *, nt):
    i = pl.program_id(1)
    tm = ur_ref.shape[0]

    def shifted(u_ref, p_ref, n_ref, col0):
        u = u_ref[...]
        mu = mu_ref[:, col0:col0 + u.shape[1]]
        row = lax.broadcasted_iota(jnp.int32, u.shape, 0)
        before = jnp.where(i == 0, 0.0, p_ref[7:8, :])
        after = jnp.where(i == nt - 1, 0.0, n_ref[0:1, :])
        prev = jnp.where(row == 0, before, pltpu.roll(u, 1, axis=0))
        nxt = jnp.where(row == tm - 1, after, pltpu.roll(u, tm - 1, axis=0))
        return u + mu[0:1] * (prev - u) + mu[1:2] * (nxt - u)

    r_all = shifted(ur_ref, pr_ref, nr_ref, 0)
    k_all = shifted(uk_ref, pk_ref, nk_ref, RW_WIDTH)
    v_all = shifted(uv_ref, pv_ref, nv_ref, 2 * RW_WIDTH)
    wa = shifted(uwa_ref, pwa_ref, nwa_ref, 3 * RW_WIDTH)
    wd = jnp.tanh(wa[:, :LANES]).astype(BF16)
    ad = wa[:, LANES:].astype(BF16)
    first = lax.broadcasted_iota(jnp.int32, (tm, LANES), 1) < RW_HEAD_DIM

    def head_rows(x, y):
        return (jnp.where(first, x, pltpu.roll(y, RW_HEAD_DIM, axis=1)),
                jnp.where(first, pltpu.roll(x, RW_HEAD_DIM, axis=1), y))

    outs = (rv_o, wk0_o, wk1_o, bk0_o, bk1_o)
    even = [[] for _ in outs]
    odd = [[] for _ in outs]
    bonus = []
    for p in range(RW_WIDTH // LANES):
        sl = slice(p * LANES, (p + 1) * LANES)
        r, k, v = r_all[:, sl], k_all[:, sl], v_all[:, sl]
        kk = k * kk_ref[:, sl]
        kk = kk / jnp.maximum(jnp.sqrt(_head_sum(kk * kk)), 1e-12)
        kd_sum = jnp.zeros_like(k)
        per_dir = []
        for d in range(2):
            w = w0_ref[d:d + 1, sl] + jnp.dot(wd, w2_ref[d, :, sl].astype(BF16), preferred_element_type=F32)
            a = _sigmoid(a0_ref[d:d + 1, sl] + jnp.dot(ad, a2_ref[d, :, sl].astype(BF16), preferred_element_type=F32))
            kd = k * (1.0 + (a - 1.0) * ka_ref[:, sl])
            dec = jnp.exp(-RW_DECAY_SCALE * _sigmoid(w))
            per_dir.append(((dec, kd), (kk * a, kk)))
            kd_sum = kd_sum + kd
        pairs = [(r, v), per_dir[0][0], per_dir[1][0], per_dir[0][1], per_dir[1][1]]
        bonus.append(_head_sum(r * kd_sum * rk_ref[:, sl]) * v)
        for ev, od, (x, y) in zip(even, odd, pairs):
            e, o = head_rows(x, y)
            ev.append(e)
            od.append(o)
    for o_ref, ev, od in zip(outs, even, odd):
        o_ref[...] = jnp.swapaxes(jnp.stack(ev + od, axis=0), 0, 1)
    bonus_o[...] = jnp.swapaxes(jnp.stack(bonus, axis=0), 0, 1)


def rwkv_terms(z, n_seq, t, mu, w0, w2p, a0, a2p, k_k, k_a, r_k, tm=256):
    tm = min(tm, t)
    nt = t // tm
    npair = RW_WIDTH // LANES
    wide0 = EV_RW_BLK * LANES // RW_WIDTH
    wa_blk = (EV_RW_BLK + 3 * npair) // 2
    n8 = tm // 8

    def cur(width, cb):
        return pl.BlockSpec((tm, width), lambda b, i: (b * nt + i, cb))

    def before(width, cb):
        return pl.BlockSpec((8, width), lambda b, i: (jnp.maximum((b * nt + i) * n8 - 1, 0), cb))

    def after(width, cb):
        return pl.BlockSpec((8, width), lambda b, i: (jnp.minimum((b * nt + i + 1) * n8, n_seq * nt * n8 - 1), cb))

    cols = [(RW_WIDTH, wide0), (RW_WIDTH, wide0 + 1), (RW_WIDTH, wide0 + 2), (2 * LANES, wa_blk)]
    whole = lambda a: pl.BlockSpec(a.shape, lambda b, i: (0,) * a.ndim)
    params = (mu, w0, w2p, a0, a2p, k_k, k_a, r_k)
    out = pl.BlockSpec((tm, npair, LANES), lambda b, i: (i, b, 0))
    out2 = pl.BlockSpec((tm, 2 * npair, LANES), lambda b, i: (i, b, 0))
    return pl.pallas_call(
        functools.partial(_rw_terms_kernel, nt=nt),
        out_shape=[jax.ShapeDtypeStruct((t, n_seq * 2 * npair, LANES), F32)] * 5
        + [jax.ShapeDtypeStruct((t, n_seq * npair, LANES), F32)],
        grid=(n_seq, nt),
        in_specs=([cur(*c) for c in cols] + [before(*c) for c in cols] + [after(*c) for c in cols]
                  + [whole(a) for a in params]),
        out_specs=[out2] * 5 + [out],
        compiler_params=_cparams(2),
        name="rwkv_terms",
    )(*([z] * 12), *params)


def _to_lanes(n):
    q = jnp.concatenate([n, pltpu.roll(n, RW_HEAD_DIM, axis=1)], axis=0)
    return q.T[:RW_HEAD_DIM]


def _rw_scan_kernel(rvf, rvb, wk0, wk1, bk0, bk1, s0_ref, yf_ref, yb_ref, sfin_ref,
                    s_ref, z_ref, znext_ref, g_ref, ybuf_ref, *, chunk):
    n = RW_HEAD_DIM
    nb = rvf.shape[1] // (2 * RW_WIDTH // LANES)
    R, K, V, A, B = range(5)

    @pl.when(pl.program_id(0) == 0)
    def _():
        s_ref[:, :n, :] = s0_ref[...]

    def convert(t_src):
        def pair_tiles(f, bw):
            both = (f[t_src], bw[chunk - 1 - t_src])
            x = jnp.concatenate([src[(b * 2 + par) * 8:(b * 2 + par) * 8 + 8]
                                 for par in range(2) for src in both for b in range(nb)], axis=0)
            xt = x.T
            return xt[:n], xt[n:]

        r, v = pair_tiles(rvf, rvb)
        w, k = pair_tiles(wk0, wk1)
        b, kk = pair_tiles(bk0, bk1)
        g_prev = g_ref[0]
        g = g_prev * w
        g_ref[0] = g
        inv = 1.0 / g
        znext_ref[R] = r * g
        znext_ref[K] = k * inv
        znext_ref[V] = v
        znext_ref[A] = -(kk * g_prev)
        znext_ref[B] = b * inv

    def emit_y(t):
        rows = _to_lanes(ybuf_ref[...])
        yf_ref[t] = rows[:n // 2]
        yb_ref[chunk - 1 - t] = rows[n // 2:]

    g_ref[0] = jnp.ones((n, LANES), F32)
    ybuf_ref[...] = jnp.zeros((n, LANES), F32)
    convert(0)

    def step(t, carry):
        emit_y(jnp.maximum(t - 1, 0))
        z_ref[...] = znext_ref[...]
        g_ref[1] = g_ref[0]
        convert(jnp.minimum(t + 1, chunk - 1))
        sa = [jnp.zeros((n, LANES), F32), jnp.zeros((n, LANES), F32)]
        for k in range(n):
            sa[k % 2] = sa[k % 2] + s_ref[k, :n, :] * z_ref[A, k:k + 1, :]
        sa = sa[0] + sa[1]
        vt = z_ref[V]
        y = [jnp.zeros((n, LANES), F32), jnp.zeros((n, LANES), F32)]
        for k in range(n):
            sk = s_ref[k, :n, :] + sa * z_ref[B, k:k + 1, :] + vt * z_ref[K, k:k + 1, :]
            s_ref[k, :n, :] = sk
            y[k % 2] = y[k % 2] + sk * z_ref[R, k:k + 1, :]
        ybuf_ref[...] = y[0] + y[1]
        return carry

    lax.fori_loop(0, chunk, step, 0)
    emit_y(chunk - 1)
    for k in range(n):
        s_ref[k, :n, :] = s_ref[k, :n, :] * g_ref[1, k:k + 1, :]
    sfin_ref[...] = s_ref[:, :n, :]


def rwkv_scan(rv, wk0, wk1, bk0, bk1, s0):
    t, nrow, _ = rv.shape
    n = RW_HEAD_DIM
    chunk = RW_SCAN_CHUNK
    nc = t // chunk
    assert nrow == LANES // 2, "the states of both directions fill the 128 lanes"
    fwd = pl.BlockSpec((chunk, nrow, LANES), lambda i: (i, 0, 0))
    bwd = pl.BlockSpec((chunk, nrow, LANES), lambda i: (nc - 1 - i, 0, 0))
    yfwd = pl.BlockSpec((chunk, nrow // 2, LANES), lambda i: (i, 0, 0))
    ybwd = pl.BlockSpec((chunk, nrow // 2, LANES), lambda i: (nc - 1 - i, 0, 0))
    state = pl.BlockSpec((n, n, LANES), lambda i: (0, 0, 0))
    slab = jax.ShapeDtypeStruct((t, nrow // 2, LANES), F32)
    return pl.pallas_call(
        functools.partial(_rw_scan_kernel, chunk=chunk),
        out_shape=[slab, slab, jax.ShapeDtypeStruct((n, n, LANES), F32)],
        grid=(nc,),
        in_specs=[fwd, bwd, fwd, bwd, fwd, bwd, state],
        out_specs=[yfwd, ybwd, state],
        scratch_shapes=[pltpu.VMEM((n, n + RW_STATE_PAD, LANES), F32), pltpu.VMEM((5, n, LANES), F32),
                        pltpu.VMEM((5, n, LANES), F32), pltpu.VMEM((2, n, LANES), F32),
                        pltpu.VMEM((n, LANES), F32)],
        compiler_params=_cparams(1),
        name="rwkv_scan",
    )(rv, rv, wk0, wk1, bk0, bk1, s0)


def _rw_readout_kernel(yf_ref, yb_ref, bonus_ref, lw_ref, lb_ref, o_ref):
    npair = RW_WIDTH // LANES
    y_all = jnp.swapaxes(yf_ref[...] + yb_ref[...], 0, 1)
    bonus_all = jnp.swapaxes(bonus_ref[...], 0, 1)
    for q in range(yf_ref.shape[1]):
        b, p = divmod(q, npair)
        sl = slice(p * LANES, (p + 1) * LANES)
        y = y_all[q]
        mu = _head_sum(y) * (1.0 / RW_HEAD_DIM)
        d = y - mu
        var = _head_sum(d * d) * (1.0 / RW_HEAD_DIM)
        o_ref[b, :, sl] = d * lax.rsqrt(var + RW_GN_EPS) * lw_ref[:, sl] + lb_ref[:, sl] + bonus_all[q]


def rwkv_readout(y_f, y_b, bonus, ln_w, ln_b, tm=256):
    t, nq, _ = y_f.shape
    tm = min(tm, t)
    batch = nq // (RW_WIDTH // LANES)
    slab = pl.BlockSpec((tm, nq, LANES), lambda i: (i, 0, 0))
    vec = pl.BlockSpec((1, RW_WIDTH), lambda i: (0, 0))
    return pl.pallas_call(
        _rw_readout_kernel,
        out_shape=jax.ShapeDtypeStruct((batch, t, RW_WIDTH), F32),
        grid=(t // tm,),
        in_specs=[slab, slab, slab, vec, vec],
        out_specs=pl.BlockSpec((batch, tm, RW_WIDTH), lambda i: (0, i, 0)),
        compiler_params=_cparams(1),
        name="rwkv_readout",
    )(y_f, y_b, bonus, ln_w.reshape(1, -1), ln_b.reshape(1, -1))


def _rms(x, w):
    return x * lax.rsqrt(jnp.mean(x * x, axis=-1, keepdims=True) + NORM_EPS) * w


def _rope(x, cos, sin):
    odd = (lax.broadcasted_iota(jnp.int32, x.shape, 1) & 1) == 1
    swapped = jnp.where(odd, pltpu.roll(x, 1, axis=1), pltpu.roll(x, LANES - 1, axis=1))
    return x * cos + swapped * sin


def _mla_q_kernel(c_ref, nw_ref, w_ref, cos_ref, sin_ref, q_ref):
    h = _rms(c_ref[...], nw_ref[...]).astype(BF16)
    acc = jnp.dot(h, w_ref[...], preferred_element_type=F32)
    cos, sin = cos_ref[...], sin_ref[...]
    for hd in range(MLA_HEADS):
        lo = hd * MLA_QK_PAD
        q_ref[:, lo:lo + MLA_NOPE] = acc[:, lo:lo + MLA_NOPE].astype(BF16)
        q_ref[:, lo + MLA_NOPE:lo + MLA_QK_PAD] = _rope(acc[:, lo + MLA_NOPE:lo + MLA_QK_PAD], cos, sin).astype(BF16)


def mla_queries(z, q_norm, w_uq_pad, cos, sin, t, tm=512):
    m = z.shape[0]
    tm = min(tm, t)
    nt = t // tm
    width = MLA_HEADS * MLA_QK_PAD
    return pl.pallas_call(
        _mla_q_kernel,
        out_shape=jax.ShapeDtypeStruct((m, width), BF16),
        grid=(m // tm,),
        in_specs=[pl.BlockSpec((tm, MLA_Q_RANK), lambda i: (i, GATE_COLS // MLA_Q_RANK)),
                  pl.BlockSpec((1, MLA_Q_RANK), lambda i: (0, 0)),
                  pl.BlockSpec((MLA_Q_RANK, width), lambda i: (0, 0)),
                  pl.BlockSpec((tm, LANES), lambda i: (i % nt, 0)),
                  pl.BlockSpec((tm, LANES), lambda i: (i % nt, 0))],
        out_specs=pl.BlockSpec((tm, width), lambda i: (i, 0)),
        compiler_params=_cparams(1),
        name="mla_queries",
    )(z, q_norm.reshape(1, -1), w_uq_pad, cos, sin)


def _mla_kv_kernel(c_ref, pe_ref, nw_ref, w_ref, cos_ref, sin_ref, k_ref, v_ref):
    h = _rms(c_ref[...], nw_ref[...]).astype(BF16)
    acc = jnp.dot(h, w_ref[...], preferred_element_type=F32)
    pe = _rope(pe_ref[...], cos_ref[...], sin_ref[...]).astype(BF16)
    for hd in range(MLA_HEADS):
        src = hd * (MLA_NOPE + MLA_V)
        dst = hd * MLA_QK_PAD
        k_ref[:, dst:dst + MLA_NOPE] = acc[:, src:src + MLA_NOPE].astype(BF16)
        k_ref[:, dst + MLA_NOPE:dst + MLA_QK_PAD] = pe
        v_ref[:, hd * MLA_V:(hd + 1) * MLA_V] = acc[:, src + MLA_NOPE:src + MLA_NOPE + MLA_V].astype(BF16)


def mla_keys_values(z, kv_norm, w_ukv, cos, sin, t, tm=512):
    m = z.shape[0]
    tm = min(tm, t)
    nt = t // tm
    return pl.pallas_call(
        _mla_kv_kernel,
        out_shape=[jax.ShapeDtypeStruct((m, MLA_HEADS * MLA_QK_PAD), BF16),
                   jax.ShapeDtypeStruct((m, MLA_WIDTH), BF16)],
        grid=(m // tm,),
        in_specs=[pl.BlockSpec((tm, MLA_KV_RANK), lambda i: (i, (GATE_COLS + MLA_Q_RANK) // MLA_KV_RANK)),
                  pl.BlockSpec((tm, LANES), lambda i: (i, OD_A_BLK + (MLA_Q_RANK + MLA_KV_RANK) // LANES)),
                  pl.BlockSpec((1, MLA_KV_RANK), lambda i: (0, 0)),
                  pl.BlockSpec(w_ukv.shape, lambda i: (0, 0)),
                  pl.BlockSpec((tm, LANES), lambda i: (i % nt, 0)),
                  pl.BlockSpec((tm, LANES), lambda i: (i % nt, 0))],
        out_specs=[pl.BlockSpec((tm, MLA_HEADS * MLA_QK_PAD), lambda i: (i, 0)),
                   pl.BlockSpec((tm, MLA_WIDTH), lambda i: (i, 0))],
        compiler_params=_cparams(1),
        name="mla_keys_values",
    )(z, z, kv_norm.reshape(1, -1), w_ukv, cos, sin)


def _mla_attn_kernel(q_ref, k_ref, v_ref, kc_ref, vc_ref, o_ref, *, scale):
    parts = [(kc_ref[...], None, vc_ref[...])]
    nkeys = k_ref.shape[0]
    step = min(MLA_KEY_CHUNK, nkeys)
    for lo in range(0, nkeys, step):
        parts.append((k_ref[lo:lo + step, :], None, v_ref[lo:lo + step, :]))
    o_ref[...] = _softmax_pv(q_ref[...], parts, scale)


def mla_attention(q, k, v, kc, vc, batch, t, l, tq=2048):
    tq = min(tq, t)
    nq = t // tq
    return pl.pallas_call(
        functools.partial(_mla_attn_kernel, scale=(MLA_NOPE + MLA_ROPE) ** -0.5),
        out_shape=jax.ShapeDtypeStruct((batch * t, MLA_WIDTH), F32),
        grid=(batch, MLA_HEADS, nq),
        in_specs=[pl.BlockSpec((tq, MLA_QK_PAD), lambda b, h, i: (b * nq + i, h)),
                  pl.BlockSpec((t, MLA_QK_PAD), lambda b, h, i: (b, h)),
                  pl.BlockSpec((t, MLA_V), lambda b, h, i: (b, h)),
                  pl.BlockSpec((l, MLA_QK_PAD), lambda b, h, i: (b, h)),
                  pl.BlockSpec((l, MLA_V), lambda b, h, i: (b, h))],
        out_specs=pl.BlockSpec((tq, MLA_V), lambda b, h, i: (b * nq + i, h)),
        compiler_params=_cparams(3),
        name="mla_attention",
    )(q, k, v, kc, vc)


def _rope_tables(t):
    tok = np.arange(t)
    pos = np.stack([tok // GRID_W, tok % GRID_W], axis=-1).astype(np.float32)
    n_freq = MLA_ROPE // 4
    inv = (ROPE_BASE ** (-jnp.arange(n_freq, dtype=F32) / n_freq))
    ang = (jnp.asarray(pos)[:, :, None] * inv).reshape(t, MLA_ROPE // 2)
    cos = jnp.repeat(jnp.cos(ang), 2, axis=-1)
    sin = jnp.repeat(jnp.sin(ang), 2, axis=-1) * jnp.tile(jnp.array([-1.0, 1.0], F32), MLA_ROPE // 2)
    pad = ((0, 0), (0, LANES - MLA_ROPE))
    return jnp.pad(cos, pad), jnp.pad(sin, pad)


def _split3(x):
    hi = x.astype(BF16)
    r1 = x - hi.astype(F32)
    mid = r1.astype(BF16)
    lo = (r1 - mid.astype(F32)).astype(BF16)
    return hi, mid, lo


def _hg_kernel(q_ref, f_ref, v_ref, lb_ref, s0_ref, *rest, reverse, finalize, emit_state):
    st_ref = rest[-1]
    rest = rest[:-1]
    if finalize:
        prev_ref, nw_ref = rest[0], rest[1]
        rest = rest[2:]
    o_ref = rest[0]

    @pl.when(pl.program_id(2) == 0)
    def _():
        st_ref[...] = s0_ref[...]

    bt = q_ref.shape[0]
    nchunk = bt // HG_CHUNK
    order = range(nchunk - 1, -1, -1) if reverse else range(nchunk)
    ri = lax.broadcasted_iota(jnp.int32, (bt, bt), 0)
    ci = lax.broadcasted_iota(jnp.int32, (bt, bt), 1)
    same = (ri // HG_CHUNK) == (ci // HG_CHUNK)
    tri = same & ((ci >= ri) if reverse else (ci <= ri))
    tri_f = tri.astype(F32)
    tri_b = tri.astype(BF16)
    row_chunk = lax.broadcasted_iota(jnp.int32, (bt, HG_DK), 0) // HG_CHUNK
    in_chunk = [row_chunk == c for c in range(nchunk)]

    lb = lb_ref[...]
    fz = f_ref[...]
    log_sig = jnp.minimum(fz, 0.0) - jnp.log1p(jnp.exp(-jnp.abs(fz)))
    la = jnp.log(lb)
    lbb = jnp.log1p(-lb) + log_sig
    log_f = jnp.maximum(la, lbb) + jnp.log1p(jnp.exp(-jnp.abs(la - lbb)))
    kf_all = (1.0 - lb) * _sigmoid(-fz)
    pieces = jnp.dot(tri_b, jnp.concatenate(_split3(log_f), axis=1), preferred_element_type=F32)
    width = log_f.shape[1]
    cum_all = pieces[:, :width] + pieces[:, width:2 * width] + pieces[:, 2 * width:]

    for hh in range(HG_STEP_HEADS):
        sl = slice(hh * HG_DK, (hh + 1) * HG_DK)
        cum, kf, v = cum_all[:, sl], kf_all[:, sl], v_ref[:, sl]
        last = [cum[c * HG_CHUNK:c * HG_CHUNK + 1] if reverse else cum[(c + 1) * HG_CHUNK - 1:(c + 1) * HG_CHUNK]
                for c in range(nchunk)]
        tot = jnp.concatenate([jnp.broadcast_to(x, (HG_CHUNK, HG_DK)) for x in last], axis=0)
        q_in = q_ref[:, sl] * jnp.exp(cum)
        k_in = kf * jnp.exp(-cum)
        k_end = kf * jnp.exp(tot - cum)
        o_intra = _dot(_dot_nt(q_in, k_in) * tri_f, v)
        v_by_chunk = jnp.concatenate([jnp.where(in_chunk[c], v, 0.0) for c in range(nchunk)], axis=1)
        d_st = _dot_tn(v_by_chunk, k_end)
        st = st_ref[hh]
        before = [None] * nchunk
        for c in order:
            before[c] = st
            st = st * jnp.exp(last[c]) + d_st[c * HG_DV:(c + 1) * HG_DV]
        st_ref[hh] = st
        q_by_chunk = jnp.concatenate([jnp.where(in_chunk[c], q_in, 0.0) for c in range(nchunk)], axis=1)
        o = o_intra + _dot_nt(q_by_chunk, jnp.concatenate(before, axis=1))
        if finalize:
            o = o + prev_ref[:, sl]
            o = o * lax.rsqrt(jnp.mean(o * o, axis=-1, keepdims=True) + NORM_EPS) * nw_ref[...]
        o_ref[:, sl] = o
        if emit_state:
            rest[1][hh] = st


def hgrn2_pass(z, lb, s0, batch, t, f_off, reverse, prev=None, norm_w=None, emit_state=False):
    bt = min(HG_BLOCK, t)
    nblk = t // bt
    nh = HG_STEP_HEADS
    wide = nh * LANES
    finalize = prev is not None
    assert OD_HG_BLK % nh == 0 and HG_HEADS % nh == 0, "head groups must start on a whole column block"
    pos = (lambda i: nblk - 1 - i) if reverse else (lambda i: i)
    blk = lambda off: pl.BlockSpec((bt, wide), lambda b, h, i, off=off: (b * nblk + pos(i), off // nh + h))
    state = pl.BlockSpec((None, nh, HG_DV, HG_DK), lambda b, h, i: (b, h, 0, 0))
    ins = [z, z, z, lb.reshape(1, -1), s0]
    specs = [blk(OD_HG_BLK), blk(OD_HG_BLK + f_off), blk(OD_HG_BLK + 3 * HG_HEADS),
             pl.BlockSpec((1, wide), lambda b, h, i: (0, h)), state]
    if finalize:
        ins += [prev, norm_w.reshape(1, -1)]
        specs += [blk(0), pl.BlockSpec((1, HG_DV), lambda b, h, i: (0, 0))]
    out_shape = [jax.ShapeDtypeStruct((batch * t, HG_WIDTH), F32)]
    out_specs = [blk(0)]
    if emit_state:
        out_shape.append(jax.ShapeDtypeStruct((batch, HG_HEADS, HG_DV, HG_DK), F32))
        out_specs.append(state)
    return pl.pallas_call(
        functools.partial(_hg_kernel, reverse=reverse, finalize=finalize, emit_state=emit_state),
        out_shape=out_shape,
        grid=(batch, HG_HEADS // nh, nblk),
        in_specs=specs,
        out_specs=out_specs,
        scratch_shapes=[pltpu.VMEM((nh, HG_DV, HG_DK), F32)],
        compiler_params=_cparams(3),
        name="hgrn2_" + ("bwd" if reverse else "fwd"),
    )(*ins)


def _even_layer(x, xc, mod, norm_w, w_in, w_out, rpb, rw_params, batch, t, l):
    d = x.shape[1]
    s2 = 3 * NA_WIDTH + RW_SHIFT_COLS
    zero = jnp.zeros((d, RW_PAD_COLS - RW_SHIFT_COLS), BF16)
    w_in = w_in.astype(BF16)
    w_ext = jnp.concatenate([w_in[:, s2:], w_in[:, :s2], zero], axis=1)
    tiles_per_seq = max(t // 1024, 1)
    z = in_projection(x, norm_w, mod, lambda i: i // tiles_per_seq, w_ext, "even_in_proj")
    z_c = in_projection(xc, norm_w, mod, lambda i: batch, w_ext, "even_in_proj_ctx")

    y_na = na_attention(z, z_c, rpb, batch, t, l)
    yc_na = ctx_attention(z_c, batch, l)

    mu, w0, w2, a0, a2, k_k, k_a, r_k, ln_w, ln_b = rw_params
    mu = jnp.pad(mu, ((0, 0), (0, RW_PAD_COLS - RW_SHIFT_COLS)))
    zpad = jnp.zeros((RW_RANK, RW_WIDTH), F32)
    w2p = jnp.stack([jnp.concatenate([w2[0], zpad]), jnp.concatenate([zpad, w2[1]])])
    a2p = jnp.stack([jnp.concatenate([a2[0], zpad]), jnp.concatenate([zpad, a2[1]])])
    vecs = (mu, w0, w2p, a0, a2p, k_k.reshape(1, -1), k_a.reshape(1, -1), r_k.reshape(1, -1))
    terms = rwkv_terms(z, batch, t, *vecs)
    terms_c = rwkv_terms(z_c, batch, l, *vecs)
    s_zero = jnp.zeros((RW_HEAD_DIM, RW_HEAD_DIM, LANES), F32)
    yc_f, yc_b, s_ctx = rwkv_scan(*terms_c[:5], s_zero)
    y_f, y_b, _ = rwkv_scan(*terms[:5], s_ctx)
    y_rw = rwkv_readout(y_f, y_b, terms[5], ln_w, ln_b).reshape(batch * t, RW_WIDTH)
    yc_rw = rwkv_readout(yc_f, yc_b, terms_c[5], ln_w, ln_b).reshape(batch * l, RW_WIDTH)

    w_out = w_out.astype(BF16)
    rows_per_seq = max(t // 256, 1)
    x = out_projection(y_na, y_rw, z, w_out, x, mod, lambda i: i // rows_per_seq, "even_out_proj")
    xc = out_projection(yc_na, yc_rw, z_c, w_out, xc, mod, lambda i: batch, "even_out_proj_ctx")
    return x, xc


def _odd_layer(x, xc, mod, norm_w, w_in, w_out, q_norm, w_uq, kv_norm, w_ukv, lb, hg_norm_w, final_w,
               batch, t, l):
    d = x.shape[1]
    o3 = MLA_Q_RANK + MLA_KV_RANK + MLA_ROPE
    o4 = o3 + 4 * HG_FDIM
    zero = jnp.zeros((d, MLA_A_COLS - o3), BF16)
    w_in = w_in.astype(BF16)
    w_ext = jnp.concatenate([w_in[:, o4:], w_in[:, :o3], zero, w_in[:, o3:o4]], axis=1)
    tiles_per_seq = max(t // 1024, 1)
    z = in_projection(x, norm_w, mod, lambda i: i // tiles_per_seq, w_ext, "odd_in_proj")
    z_c = in_projection(xc, norm_w, mod, lambda i: batch, w_ext, "odd_in_proj_ctx")

    cos, sin = _rope_tables(t)
    qk = MLA_NOPE + MLA_ROPE
    w_uq_pad = jnp.pad(w_uq.reshape(MLA_Q_RANK, MLA_HEADS, qk), ((0, 0), (0, 0), (0, MLA_QK_PAD - qk)))
    w_uq_pad = w_uq_pad.reshape(MLA_Q_RANK, MLA_HEADS * MLA_QK_PAD).astype(BF16)
    w_ukv = w_ukv.astype(BF16)
    q = mla_queries(z, q_norm, w_uq_pad, cos, sin, t)
    k, v = mla_keys_values(z, kv_norm, w_ukv, cos, sin, t)
    ones = jnp.ones((l, LANES), F32)
    kc, vc = mla_keys_values(z_c, kv_norm, w_ukv, ones, jnp.zeros_like(ones), l)
    y_mla = mla_attention(q, k, v, kc, vc, batch, t, l)

    s_zero = jnp.zeros((batch, HG_HEADS, HG_DV, HG_DK), F32)
    _, s_f = hgrn2_pass(z_c, lb, s_zero, batch, l, HG_HEADS, False, emit_state=True)
    _, s_b = hgrn2_pass(z_c, lb, s_zero, batch, l, 2 * HG_HEADS, True, emit_state=True)
    o_f, = hgrn2_pass(z, lb, s_f, batch, t, HG_HEADS, False)
    y_hg, = hgrn2_pass(z, lb, s_b, batch, t, 2 * HG_HEADS, True, prev=o_f, norm_w=hg_norm_w)

    rows_per_seq = max(t // 256, 1)
    return out_projection(y_mla, y_hg, z, w_out.astype(BF16), x, mod, lambda i: i // rows_per_seq,
                          "odd_out_proj", final_w=final_w)


def kernel(x, c, ctx, c_ctx, ada_w, ada_b, norm_w, e_w_in, e_w_out, na_rpb, rw_mu, rw_w0, rw_w2, rw_a0, rw_a2, rw_k_k, rw_k_a, rw_r_k, rw_ln_w, rw_ln_b, o_w_in, o_w_out, mla_q_norm, mla_w_uq, mla_kv_norm, mla_w_ukv, hg_lower_bounds, hg_norm_w, final_norm_w):
    batch, t, d = x.shape
    l = ctx.shape[1]
    assert ada_w.shape[0] == 2, "one even and one odd layer"
    cond = jnp.concatenate([c, c_ctx[None, :], jnp.zeros((8 - batch - 1, d), F32)], axis=0)
    mod = modulation(cond, ada_w, ada_b)
    s = jax.nn.softmax(hg_lower_bounds.astype(F32), axis=0)
    lower = jnp.cumsum(s, axis=0) - s[0]

    xf, xcf = x.reshape(batch * t, d), ctx.reshape(batch * l, d)
    rw_params = (rw_mu[0], rw_w0[0], rw_w2[0], rw_a0[0], rw_a2[0], rw_k_k[0], rw_k_a[0], rw_r_k[0],
                 rw_ln_w[0], rw_ln_b[0])
    xf, xcf = _even_layer(xf, xcf, mod[0].reshape(8, 1, 3 * d), norm_w[0], e_w_in[0], e_w_out[0], na_rpb[0],
                          rw_params, batch, t, l)
    out = _odd_layer(xf, xcf, mod[1].reshape(8, 1, 3 * d), norm_w[1], o_w_in[0], o_w_out[0], mla_q_norm[0],
                     mla_w_uq[0], mla_kv_norm[0], mla_w_ukv[0], lower[1], hg_norm_w[0], final_norm_w,
                     batch, t, l)
    return out.reshape(batch, t, d)
```

```python
import functools
import math

import numpy as np
import jax
import jax.numpy as jnp
from jax import lax
from jax.experimental import pallas as pl
from jax.experimental.pallas import tpu as pltpu

F32 = jnp.float32
BF16 = jnp.bfloat16

GRID_W = 64
NORM_EPS = 1e-6
ROPE_BASE = 10000.0

NA_HEADS = 8
NA_HEAD_DIM = 128
NA_WIDTH = NA_HEADS * NA_HEAD_DIM
NA_WIN_ROWS = 8
NA_WIN_COLS = 16
NA_QROWS = 4
NA_KROWS = 12
NA_STEP_HEADS = 4

RW_HEAD_DIM = 64
RW_HEADS = 16
RW_WIDTH = RW_HEADS * RW_HEAD_DIM
RW_RANK = 64
RW_SHIFT_COLS = 3 * RW_WIDTH + 4 * RW_RANK
RW_PAD_COLS = 3584
RW_GN_EPS = 64e-5
RW_SCAN_CHUNK = 64
RW_DECAY_SCALE = math.exp(-0.5)
RW_STATE_PAD = 0

MLA_HEADS = 8
MLA_Q_RANK = 512
MLA_KV_RANK = 512
MLA_NOPE = 128
MLA_ROPE = 64
MLA_V = 128
MLA_QK_PAD = 256
MLA_WIDTH = MLA_HEADS * MLA_V
MLA_A_COLS = 1536
MLA_KEY_CHUNK = 512

HG_HEADS = 8
HG_DK = 128
HG_DV = 128
HG_FDIM = HG_HEADS * HG_DK
HG_WIDTH = HG_HEADS * HG_DV
HG_CHUNK = 32
HG_BLOCK = 256
HG_STEP_HEADS = 4

LANES = 128

GATE_COLS = 2048
EV_Q_BLK = GATE_COLS // LANES
EV_RW_BLK = EV_Q_BLK + 3 * NA_WIDTH // LANES
OD_A_BLK = GATE_COLS // LANES
OD_HG_BLK = OD_A_BLK + MLA_A_COLS // LANES
VMEM_LIMIT = 48 * 1024 * 1024
NEG_INF = -1e30
LOG2E = math.log2(math.e)


def _cparams(n_axes):
    return pltpu.CompilerParams(dimension_semantics=("arbitrary",) * n_axes, vmem_limit_bytes=VMEM_LIMIT)


def _sigmoid(x):
    return 1.0 / (1.0 + jnp.exp(-x))


def _silu(x):
    return x * _sigmoid(x)


def _dot(a, b):
    return jnp.dot(a.astype(BF16), b.astype(BF16), preferred_element_type=F32)


def _dot_nt(a, b):
    return lax.dot_general(a.astype(BF16), b.astype(BF16), (((1,), (1,)), ((), ())), preferred_element_type=F32)


def _dot_tn(a, b):
    return lax.dot_general(a.astype(BF16), b.astype(BF16), (((0,), (0,)), ((), ())), preferred_element_type=F32)


def _mod_kernel(c_ref, w_ref, b_ref, o_ref):
    o_ref[...] = _dot(_silu(c_ref[...]), w_ref[...]) + b_ref[...]


def modulation(cond, ada_w, ada_b, tn=512):
    depth, d, n = ada_w.shape
    return pl.pallas_call(
        _mod_kernel,
        out_shape=jax.ShapeDtypeStruct((depth, 8, n), F32),
        grid=(depth, n // tn),
        in_specs=[pl.BlockSpec((8, d), lambda l, j: (0, 0)),
                  pl.BlockSpec((None, d, tn), lambda l, j: (l, 0, j)),
                  pl.BlockSpec((None, 1, tn), lambda l, j: (l, 0, j))],
        out_specs=pl.BlockSpec((None, 8, tn), lambda l, j: (l, 0, j)),
        compiler_params=_cparams(2),
        name="adaln_modulation",
    )(cond, ada_w, ada_b.reshape(depth, 1, n))


_COPY, _SHIFTED, _ZERO, _FIRST_HALF = range(4)


def _cast_reorder_kernel(src_ref, kind_ref, x0_ref, *rest, shifted):
    del src_ref
    o_ref = rest[-1]
    kind = kind_ref[pl.program_id(0)]
    x0 = x0_ref[...]
    half = LANES // 2
    first = lax.broadcasted_iota(jnp.int32, x0.shape, 1) < half
    out = jnp.where(kind == _FIRST_HALF, jnp.where(first, x0, 0.0), x0)
    if shifted:
        moved = jnp.where(first, pltpu.roll(x0, half, axis=1), pltpu.roll(rest[0][...], half, axis=1))
        out = jnp.where(kind == _SHIFTED, moved, out)
    o_ref[...] = jnp.where(kind == _ZERO, 0.0, out).astype(BF16)


def cast_reorder(w, plan, name):
    d, n_in = w.shape
    kinds = np.array([k for k, _ in plan], np.int32)
    src = np.array([a for _, a in plan], np.int32)
    shifted = bool((kinds == _SHIFTED).any())
    last = pl.cdiv(n_in, LANES) - 1
    specs = [pl.BlockSpec((d, LANES), lambda j, src, kind: (0, src[j]))]
    if shifted:
        specs.append(pl.BlockSpec((d, LANES), lambda j, src, kind: (0, jnp.minimum(src[j] + 1, last))))
    return pl.pallas_call(
        functools.partial(_cast_reorder_kernel, shifted=shifted),
        out_shape=jax.ShapeDtypeStruct((d, LANES * len(plan)), BF16),
        grid_spec=pltpu.PrefetchScalarGridSpec(
            num_scalar_prefetch=2,
            grid=(len(plan),),
            in_specs=specs,
            out_specs=pl.BlockSpec((d, LANES), lambda j, src, kind: (0, j))),
        compiler_params=_cparams(1),
        name=name,
    )(jnp.asarray(src), jnp.asarray(kinds), *([w] * len(specs)))


def _inproj_kernel(x_ref, nw_ref, shift_ref, scale_ref, w_ref, o_ref, h_ref):
    @pl.when(pl.program_id(1) == 0)
    def _():
        x = x_ref[...]
        y = x * lax.rsqrt(jnp.mean(x * x, axis=-1, keepdims=True) + NORM_EPS) * nw_ref[...]
        h_ref[...] = (y * (1.0 + scale_ref[...]) + shift_ref[...]).astype(BF16)

    o_ref[...] = jnp.dot(h_ref[...], w_ref[...], preferred_element_type=F32)


def in_projection(x, norm_w, mod, mod_row, w, name, tm=1024, tn=512):
    m, d = x.shape
    n = w.shape[1]
    tm = min(tm, m)
    return pl.pallas_call(
        _inproj_kernel,
        out_shape=jax.ShapeDtypeStruct((m, n), F32),
        grid=(m // tm, n // tn),
        in_specs=[pl.BlockSpec((tm, d), lambda i, j: (i, 0)),
                  pl.BlockSpec((1, d), lambda i, j: (0, 0)),
                  pl.BlockSpec((None, 1, d), lambda i, j: (mod_row(i), 0, 0)),
                  pl.BlockSpec((None, 1, d), lambda i, j: (mod_row(i), 0, 1)),
                  pl.BlockSpec((d, tn), lambda i, j: (0, j))],
        out_specs=pl.BlockSpec((tm, tn), lambda i, j: (i, j)),
        scratch_shapes=[pltpu.VMEM((tm, d), BF16)],
        compiler_params=_cparams(2),
        name=name,
    )(x, norm_w.reshape(1, d), mod, mod, w)


def _outproj_kernel(ya_ref, yb_ref, g_ref, w_ref, x_ref, gm_ref, *rest, ka, final):
    o_ref = rest[-1]
    sg = _silu(g_ref[...])
    acc = _dot(ya_ref[...].astype(F32) * sg[:, :ka], w_ref[:ka, :])
    acc += _dot(yb_ref[...].astype(F32) * sg[:, ka:], w_ref[ka:, :])
    out = x_ref[...] + gm_ref[...] * acc
    if final:
        out = out * lax.rsqrt(jnp.mean(out * out, axis=-1, keepdims=True) + NORM_EPS) * rest[0][...]
    o_ref[...] = out


def out_projection(ya, yb, gate, w, x, mod, mod_row, name, final_w=None, tm=256):
    m, d = x.shape
    ka, kb = ya.shape[1], yb.shape[1]
    tm = min(tm, m)
    ins = [ya, yb, gate, w, x, mod]
    specs = [pl.BlockSpec((tm, ka), lambda i: (i, 0)),
             pl.BlockSpec((tm, kb), lambda i: (i, 0)),
             pl.BlockSpec((tm, ka + kb), lambda i: (i, 0)),
             pl.BlockSpec((ka + kb, d), lambda i: (0, 0)),
             pl.BlockSpec((tm, d), lambda i: (i, 0)),
             pl.BlockSpec((None, 1, d), lambda i: (mod_row(i), 0, 2))]
    if final_w is not None:
        ins.append(final_w.reshape(1, d))
        specs.append(pl.BlockSpec((1, d), lambda i: (0, 0)))
    return pl.pallas_call(
        functools.partial(_outproj_kernel, ka=ka, final=final_w is not None),
        out_shape=jax.ShapeDtypeStruct((m, d), F32),
        grid=(m // tm,),
        in_specs=specs,
        out_specs=pl.BlockSpec((tm, d), lambda i: (i, 0)),
        compiler_params=_cparams(1),
        name=name,
    )(*ins)


def _softmax_pv(q, parts, scale):
    c = scale * LOG2E
    m = denom = num = None
    for k, bias, v in parts:
        x = _dot_nt(q, k) * c
        if bias is not None:
            x = x + bias
        m_part = jnp.max(x, axis=-1, keepdims=True)
        if m is None:
            m = m_part
            p = jnp.exp2(x - m)
            denom = jnp.sum(p, axis=-1, keepdims=True)
            num = _dot(p, v)
        else:
            m_new = jnp.maximum(m, m_part)
            alpha = jnp.exp2(m - m_new)
            p = jnp.exp2(x - m_new)
            denom = denom * alpha + jnp.sum(p, axis=-1, keepdims=True)
            num = num * alpha + _dot(p, v)
            m = m_new
    return num / denom


def _softmax_pv_joint(q, parts, scale):
    c = scale * LOG2E
    logits = [_dot_nt(q, k) * c if bias is None else _dot_nt(q, k) * c + bias for k, bias, _ in parts]
    m = functools.reduce(jnp.maximum, [jnp.max(x, axis=-1, keepdims=True) for x in logits])
    ps = [jnp.exp2(x - m) for x in logits]
    denom = functools.reduce(lambda a, b: a + b, [jnp.sum(p, axis=-1, keepdims=True) for p in ps])
    num = functools.reduce(lambda a, b: a + b, [_dot(p, v) for p, (_, _, v) in zip(ps, parts)])
    return num / denom


def _na_kernel(idx_ref, q_ref, k_ref, v_ref, kc_ref, vc_ref, tab_ref, o_ref, bias_ref, *, rows, scale):
    j = pl.program_id(2)
    npair = NA_KROWS // 2
    start = pl.multiple_of(jnp.clip(NA_QROWS * j - NA_WIN_ROWS // 2, 0, rows - NA_KROWS) * GRID_W, GRID_W)
    win = pl.ds(start, NA_KROWS * GRID_W)
    for hd in range(NA_STEP_HEADS):
        sl = slice(hd * LANES, (hd + 1) * LANES)
        for qr in range(NA_QROWS):
            for m in range(npair):
                tile = tab_ref[hd, idx_ref[(j * NA_QROWS + qr) * npair + m]]
                bias_ref[hd, qr * GRID_W:(qr + 1) * GRID_W, m * LANES:(m + 1) * LANES] = tile
        parts = [(k_ref[win, sl], bias_ref[hd], v_ref[win, sl]), (kc_ref[:, sl], None, vc_ref[:, sl])]
        o_ref[:, sl] = _softmax_pv_joint(q_ref[:, sl], parts, scale)


def _na_tile_ids(rows):
    kh = min(NA_WIN_ROWS, rows)
    n = 2 * NA_WIN_ROWS - 1
    ids = []
    for j in range(rows // NA_QROWS):
        ks = int(np.clip(NA_QROWS * j - NA_WIN_ROWS // 2, 0, rows - NA_KROWS))
        for qr in range(NA_QROWS):
            r = NA_QROWS * j + qr
            r0 = int(np.clip(r - kh // 2, 0, rows - kh))
            for m in range(NA_KROWS // 2):
                kr = ks + 2 * m
                dy = kr - r + NA_WIN_ROWS - 1
                first, second = r0 <= kr < r0 + kh, r0 <= kr + 1 < r0 + kh
                ids.append(dy if first and second else n + dy if first else 2 * n + dy + 1 if second else 3 * n)
    return np.array(ids, np.int32)


def _na_bias_tiles(rpb):
    col = np.arange(GRID_W)
    cs = np.clip(col - NA_WIN_COLS // 2, 0, GRID_W - NA_WIN_COLS)
    kc = col[None, :]
    in_win = (kc >= cs[:, None]) & (kc < cs[:, None] + NA_WIN_COLS)
    dx = np.where(in_win, kc - col[:, None] + NA_WIN_COLS - 1, 0)
    tab = jnp.where(in_win[None, None], rpb[:, :, dx] * LOG2E, NEG_INF)
    neg = jnp.full_like(tab[:, :1], NEG_INF)
    nxt = jnp.concatenate([tab[:, 1:], neg], axis=1)
    both = jnp.concatenate([tab, nxt], axis=-1)
    first = jnp.concatenate([tab, jnp.broadcast_to(neg, tab.shape)], axis=-1)
    second = jnp.concatenate([jnp.broadcast_to(neg, tab.shape), tab], axis=-1)
    return jnp.concatenate([both, first, second, jnp.concatenate([neg, neg], axis=-1)], axis=1)


def na_attention(z, z_c, rpb, batch, t, l):
    rows = t // GRID_W
    nj = rows // NA_QROWS
    tq = NA_QROWS * GRID_W
    ids = jnp.asarray(_na_tile_ids(rows))
    tiles = _na_bias_tiles(rpb)
    nh = NA_STEP_HEADS
    wide = nh * LANES
    q0, k0, v0 = (EV_Q_BLK + i * NA_HEADS for i in range(3))
    assert q0 % nh == 0 and NA_HEADS % nh == 0, "head groups must start on a whole column block"
    return pl.pallas_call(
        functools.partial(_na_kernel, rows=rows, scale=NA_HEAD_DIM ** -0.5),
        out_shape=jax.ShapeDtypeStruct((batch * t, NA_WIDTH), F32),
        grid_spec=pltpu.PrefetchScalarGridSpec(
            num_scalar_prefetch=1,
            grid=(NA_HEADS // nh, batch, nj),
            in_specs=[pl.BlockSpec((tq, wide), lambda h, b, j, g: (b * nj + j, q0 // nh + h)),
                      pl.BlockSpec((t, wide), lambda h, b, j, g: (b, k0 // nh + h)),
                      pl.BlockSpec((t, wide), lambda h, b, j, g: (b, v0 // nh + h)),
                      pl.BlockSpec((l, wide), lambda h, b, j, g: (b, k0 // nh + h)),
                      pl.BlockSpec((l, wide), lambda h, b, j, g: (b, v0 // nh + h)),
                      pl.BlockSpec((nh,) + tiles.shape[1:], lambda h, b, j, g: (h, 0, 0, 0))],
            out_specs=pl.BlockSpec((tq, wide), lambda h, b, j, g: (b * nj + j, h)),
            scratch_shapes=[pltpu.VMEM((nh, tq, NA_KROWS * GRID_W), F32)]),
        compiler_params=_cparams(3),
        name="na_attention",
    )(ids, z, z, z, z_c, z_c, tiles)


def _dense_attn_kernel(q_ref, k_ref, v_ref, o_ref, *, scale):
    o_ref[...] = _softmax_pv(q_ref[...], [(k_ref[...], None, v_ref[...])], scale)


def ctx_attention(z_c, batch, l):
    h8 = NA_HEADS
    return pl.pallas_call(
        functools.partial(_dense_attn_kernel, scale=NA_HEAD_DIM ** -0.5),
        out_shape=jax.ShapeDtypeStruct((batch * l, NA_WIDTH), F32),
        grid=(batch, h8),
        in_specs=[pl.BlockSpec((l, LANES), lambda b, h: (b, EV_Q_BLK + h)),
                  pl.BlockSpec((l, LANES), lambda b, h: (b, EV_Q_BLK + h8 + h)),
                  pl.BlockSpec((l, LANES), lambda b, h: (b, EV_Q_BLK + 2 * h8 + h))],
        out_specs=pl.BlockSpec((l, LANES), lambda b, h: (b, h)),
        compiler_params=_cparams(2),
        name="ctx_attention",
    )(z_c, z_c, z_c)


def _head_sum(x):
    first = lax.broadcasted_iota(jnp.int32, x.shape, 1) < RW_HEAD_DIM
    s0 = jnp.sum(jnp.where(first, x, 0.0), axis=-1, keepdims=True)
    s1 = jnp.sum(jnp.where(first, 0.0, x), axis=-1, keepdims=True)
    return jnp.where(first, s0, s1)


def _rw_terms_kernel(ur_ref, uk_ref, uv_ref, uwa_ref, pr_ref, pk_ref, pv_ref, pwa_ref, nr_ref, nk_ref, nv_ref,
                     nwa_ref, mu_ref, w0_ref, w2_ref, a0_ref, a2_ref, kk_ref, ka_ref, rk_ref,
                     rv_o, wk0_o, wk1_o, bk0_o, bk1_o, bonus_o, *, nt):
    i = pl.program_id(1)
    tm = ur_ref.shape[0]

    def shifted(u_ref, p_ref, n_ref, col0):
        u = u_ref[...]
        mu = mu_ref[:, col0:col0 + u.shape[1]]
        row = lax.broadcasted_iota(jnp.int32, u.shape, 0)
        before = jnp.where(i == 0, 0.0, p_ref[7:8, :])
        after = jnp.where(i == nt - 1, 0.0, n_ref[0:1, :])
        prev = jnp.where(row == 0, before, pltpu.roll(u, 1, axis=0))
        nxt = jnp.where(row == tm - 1, after, pltpu.roll(u, tm - 1, axis=0))
        return u + mu[0:1] * (prev - u) + mu[1:2] * (nxt - u)

    r_all = shifted(ur_ref, pr_ref, nr_ref, 0)
    k_all = shifted(uk_ref, pk_ref, nk_ref, RW_WIDTH)
    v_all = shifted(uv_ref, pv_ref, nv_ref, 2 * RW_WIDTH)
    wa = shifted(uwa_ref, pwa_ref, nwa_ref, 3 * RW_WIDTH)
    wd = jnp.tanh(wa[:, :LANES]).astype(BF16)
    ad = wa[:, LANES:].astype(BF16)
    first = lax.broadcasted_iota(jnp.int32, (tm, LANES), 1) < RW_HEAD_DIM

    def head_rows(x, y):
        return (jnp.where(first, x, pltpu.roll(y, RW_HEAD_DIM, axis=1)),
                jnp.where(first, pltpu.roll(x, RW_HEAD_DIM, axis=1), y))

    outs = (rv_o, wk0_o, wk1_o, bk0_o, bk1_o)
    even = [[] for _ in outs]
    odd = [[] for _ in outs]
    bonus = []
    for p in range(RW_WIDTH // LANES):
        sl = slice(p * LANES, (p + 1) * LANES)
        r, k, v = r_all[:, sl], k_all[:, sl], v_all[:, sl]
        kk = k * kk_ref[:, sl]
        kk = kk / jnp.maximum(jnp.sqrt(_head_sum(kk * kk)), 1e-12)
        kd_sum = jnp.zeros_like(k)
        per_dir = []
        for d in range(2):
            w = w0_ref[d:d + 1, sl] + jnp.dot(wd, w2_ref[d, :, sl].astype(BF16), preferred_element_type=F32)
            a = _sigmoid(a0_ref[d:d + 1, sl] + jnp.dot(ad, a2_ref[d, :, sl].astype(BF16), preferred_element_type=F32))
            kd = k * (1.0 + (a - 1.0) * ka_ref[:, sl])
            dec = jnp.exp(-RW_DECAY_SCALE * _sigmoid(w))
            per_dir.append(((dec, kd), (kk * a, kk)))
            kd_sum = kd_sum + kd
        pairs = [(r, v), per_dir[0][0], per_dir[1][0], per_dir[0][1], per_dir[1][1]]
        bonus.append(_head_sum(r * kd_sum * rk_ref[:, sl]) * v)
        for ev, od, (x, y) in zip(even, odd, pairs):
            e, o = head_rows(x, y)
            ev.append(e)
            od.append(o)
    for o_ref, ev, od in zip(outs, even, odd):
        o_ref[...] = jnp.swapaxes(jnp.stack(ev + od, axis=0), 0, 1)
    bonus_o[...] = jnp.swapaxes(jnp.stack(bonus, axis=0), 0, 1)


def rwkv_terms(z, n_seq, t, mu, w0, w2p, a0, a2p, k_k, k_a, r_k, tm=256):
    tm = min(tm, t)
    nt = t // tm
    npair = RW_WIDTH // LANES
    wide0 = EV_RW_BLK * LANES // RW_WIDTH
    wa_blk = (EV_RW_BLK + 3 * npair) // 2
    n8 = tm // 8

    def cur(width, cb):
        return pl.BlockSpec((tm, width), lambda b, i: (b * nt + i, cb))

    def before(width, cb):
        return pl.BlockSpec((8, width), lambda b, i: (jnp.maximum((b * nt + i) * n8 - 1, 0), cb))

    def after(width, cb):
        return pl.BlockSpec((8, width), lambda b, i: (jnp.minimum((b * nt + i + 1) * n8, n_seq * nt * n8 - 1), cb))

    cols = [(RW_WIDTH, wide0), (RW_WIDTH, wide0 + 1), (RW_WIDTH, wide0 + 2), (2 * LANES, wa_blk)]
    whole = lambda a: pl.BlockSpec(a.shape, lambda b, i: (0,) * a.ndim)
    params = (mu, w0, w2p, a0, a2p, k_k, k_a, r_k)
    out = pl.BlockSpec((tm, npair, LANES), lambda b, i: (i, b, 0))
    out2 = pl.BlockSpec((tm, 2 * npair, LANES), lambda b, i: (i, b, 0))
    return pl.pallas_call(
        functools.partial(_rw_terms_kernel, nt=nt),
        out_shape=[jax.ShapeDtypeStruct((t, n_seq * 2 * npair, LANES), F32)] * 5
        + [jax.ShapeDtypeStruct((t, n_seq * npair, LANES), F32)],
        grid=(n_seq, nt),
        in_specs=([cur(*c) for c in cols] + [before(*c) for c in cols] + [after(*c) for c in cols]
                  + [whole(a) for a in params]),
        out_specs=[out2] * 5 + [out],
        compiler_params=_cparams(2),
        name="rwkv_terms",
    )(*([z] * 12), *params)


def _to_lanes(n):
    q = jnp.concatenate([n, pltpu.roll(n, RW_HEAD_DIM, axis=1)], axis=0)
    return q.T[:RW_HEAD_DIM]


def _rw_scan_kernel(rvf, rvb, wk0, wk1, bk0, bk1, s0_ref, yf_ref, yb_ref, sfin_ref,
                    s_ref, z_ref, znext_ref, g_ref, ybuf_ref, *, chunk):
    n = RW_HEAD_DIM
    nb = rvf.shape[1] // (2 * RW_WIDTH // LANES)
    R, K, V, A, B = range(5)

    @pl.when(pl.program_id(0) == 0)
    def _():
        s_ref[:, :n, :] = s0_ref[...]

    def convert(t_src):
        def pair_tiles(f, bw):
            both = (f[t_src], bw[chunk - 1 - t_src])
            x = jnp.concatenate([src[(b * 2 + par) * 8:(b * 2 + par) * 8 + 8]
                                 for par in range(2) for src in both for b in range(nb)], axis=0)
            xt = x.T
            return xt[:n], xt[n:]

        r, v = pair_tiles(rvf, rvb)
        w, k = pair_tiles(wk0, wk1)
        b, kk = pair_tiles(bk0, bk1)
        g_prev = g_ref[0]
        g = g_prev * w
        g_ref[0] = g
        inv = 1.0 / g
        znext_ref[R] = r * g
        znext_ref[K] = k * inv
        znext_ref[V] = v
        znext_ref[A] = -(kk * g_prev)
        znext_ref[B] = b * inv

    def emit_y(t):
        rows = _to_lanes(ybuf_ref[...])
        yf_ref[t] = rows[:n // 2]
        yb_ref[chunk - 1 - t] = rows[n // 2:]

    g_ref[0] = jnp.ones((n, LANES), F32)
    ybuf_ref[...] = jnp.zeros((n, LANES), F32)
    convert(0)

    def step(t, carry):
        emit_y(jnp.maximum(t - 1, 0))
        z_ref[...] = znext_ref[...]
        g_ref[1] = g_ref[0]
        convert(jnp.minimum(t + 1, chunk - 1))
        sa = [jnp.zeros((n, LANES), F32), jnp.zeros((n, LANES), F32)]
        for k in range(n):
            sa[k % 2] = sa[k % 2] + s_ref[k, :n, :] * z_ref[A, k:k + 1, :]
        sa = sa[0] + sa[1]
        vt = z_ref[V]
        y = [jnp.zeros((n, LANES), F32), jnp.zeros((n, LANES), F32)]
        for k in range(n):
            sk = s_ref[k, :n, :] + sa * z_ref[B, k:k + 1, :] + vt * z_ref[K, k:k + 1, :]
            s_ref[k, :n, :] = sk
            y[k % 2] = y[k % 2] + sk * z_ref[R, k:k + 1, :]
        ybuf_ref[...] = y[0] + y[1]
        return carry

    lax.fori_loop(0, chunk, step, 0)
    emit_y(chunk - 1)
    for k in range(n):
        s_ref[k, :n, :] = s_ref[k, :n, :] * g_ref[1, k:k + 1, :]
    sfin_ref[...] = s_ref[:, :n, :]


def rwkv_scan(rv, wk0, wk1, bk0, bk1, s0):
    t, nrow, _ = rv.shape
    n = RW_HEAD_DIM
    chunk = RW_SCAN_CHUNK
    nc = t // chunk
    assert nrow == LANES // 2, "the states of both directions fill the 128 lanes"
    fwd = pl.BlockSpec((chunk, nrow, LANES), lambda i: (i, 0, 0))
    bwd = pl.BlockSpec((chunk, nrow, LANES), lambda i: (nc - 1 - i, 0, 0))
    yfwd = pl.BlockSpec((chunk, nrow // 2, LANES), lambda i: (i, 0, 0))
    ybwd = pl.BlockSpec((chunk, nrow // 2, LANES), lambda i: (nc - 1 - i, 0, 0))
    state = pl.BlockSpec((n, n, LANES), lambda i: (0, 0, 0))
    slab = jax.ShapeDtypeStruct((t, nrow // 2, LANES), F32)
    return pl.pallas_call(
        functools.partial(_rw_scan_kernel, chunk=chunk),
        out_shape=[slab, slab, jax.ShapeDtypeStruct((n, n, LANES), F32)],
        grid=(nc,),
        in_specs=[fwd, bwd, fwd, bwd, fwd, bwd, state],
        out_specs=[yfwd, ybwd, state],
        scratch_shapes=[pltpu.VMEM((n, n + RW_STATE_PAD, LANES), F32), pltpu.VMEM((5, n, LANES), F32),
                        pltpu.VMEM((5, n, LANES), F32), pltpu.VMEM((2, n, LANES), F32),
                        pltpu.VMEM((n, LANES), F32)],
        compiler_params=_cparams(1),
        name="rwkv_scan",
    )(rv, rv, wk0, wk1, bk0, bk1, s0)


def _rw_readout_kernel(yf_ref, yb_ref, bonus_ref, lw_ref, lb_ref, o_ref):
    npair = RW_WIDTH // LANES
    y_all = jnp.swapaxes(yf_ref[...] + yb_ref[...], 0, 1)
    bonus_all = jnp.swapaxes(bonus_ref[...], 0, 1)
    for q in range(yf_ref.shape[1]):
        b, p = divmod(q, npair)
        sl = slice(p * LANES, (p + 1) * LANES)
        y = y_all[q]
        mu = _head_sum(y) * (1.0 / RW_HEAD_DIM)
        d = y - mu
        var = _head_sum(d * d) * (1.0 / RW_HEAD_DIM)
        o_ref[b, :, sl] = d * lax.rsqrt(var + RW_GN_EPS) * lw_ref[:, sl] + lb_ref[:, sl] + bonus_all[q]


def rwkv_readout(y_f, y_b, bonus, ln_w, ln_b, tm=256):
    t, nq, _ = y_f.shape
    tm = min(tm, t)
    batch = nq // (RW_WIDTH // LANES)
    slab = pl.BlockSpec((tm, nq, LANES), lambda i: (i, 0, 0))
    vec = pl.BlockSpec((1, RW_WIDTH), lambda i: (0, 0))
    return pl.pallas_call(
        _rw_readout_kernel,
        out_shape=jax.ShapeDtypeStruct((batch, t, RW_WIDTH), F32),
        grid=(t // tm,),
        in_specs=[slab, slab, slab, vec, vec],
        out_specs=pl.BlockSpec((batch, tm, RW_WIDTH), lambda i: (0, i, 0)),
        compiler_params=_cparams(1),
        name="rwkv_readout",
    )(y_f, y_b, bonus, ln_w.reshape(1, -1), ln_b.reshape(1, -1))


def _rms(x, w):
    return x * lax.rsqrt(jnp.mean(x * x, axis=-1, keepdims=True) + NORM_EPS) * w


def _rope(x, cos, sin):
    odd = (lax.broadcasted_iota(jnp.int32, x.shape, 1) & 1) == 1
    swapped = jnp.where(odd, pltpu.roll(x, 1, axis=1), pltpu.roll(x, LANES - 1, axis=1))
    return x * cos + swapped * sin


def _mla_q_kernel(c_ref, nw_ref, w_ref, cos_ref, sin_ref, q_ref):
    h = _rms(c_ref[...], nw_ref[...]).astype(BF16)
    acc = jnp.dot(h, w_ref[...], preferred_element_type=F32)
    cos, sin = cos_ref[...], sin_ref[...]
    for hd in range(MLA_HEADS):
        lo = hd * MLA_QK_PAD
        q_ref[:, lo:lo + MLA_NOPE] = acc[:, lo:lo + MLA_NOPE].astype(BF16)
        q_ref[:, lo + MLA_NOPE:lo + MLA_QK_PAD] = _rope(acc[:, lo + MLA_NOPE:lo + MLA_QK_PAD], cos, sin).astype(BF16)


def mla_queries(z, q_norm, w_uq_pad, cos, sin, t, tm=512):
    m = z.shape[0]
    tm = min(tm, t)
    nt = t // tm
    width = MLA_HEADS * MLA_QK_PAD
    return pl.pallas_call(
        _mla_q_kernel,
        out_shape=jax.ShapeDtypeStruct((m, width), BF16),
        grid=(m // tm,),
        in_specs=[pl.BlockSpec((tm, MLA_Q_RANK), lambda i: (i, GATE_COLS // MLA_Q_RANK)),
                  pl.BlockSpec((1, MLA_Q_RANK), lambda i: (0, 0)),
                  pl.BlockSpec((MLA_Q_RANK, width), lambda i: (0, 0)),
                  pl.BlockSpec((tm, LANES), lambda i: (i % nt, 0)),
                  pl.BlockSpec((tm, LANES), lambda i: (i % nt, 0))],
        out_specs=pl.BlockSpec((tm, width), lambda i: (i, 0)),
        compiler_params=_cparams(1),
        name="mla_queries",
    )(z, q_norm.reshape(1, -1), w_uq_pad, cos, sin)


def _mla_kv_kernel(c_ref, pe_ref, nw_ref, w_ref, cos_ref, sin_ref, k_ref, v_ref):
    h = _rms(c_ref[...], nw_ref[...]).astype(BF16)
    acc = jnp.dot(h, w_ref[...], preferred_element_type=F32)
    pe = _rope(pe_ref[...], cos_ref[...], sin_ref[...]).astype(BF16)
    for hd in range(MLA_HEADS):
        src = hd * (MLA_NOPE + MLA_V)
        dst = hd * MLA_QK_PAD
        k_ref[:, dst:dst + MLA_NOPE] = acc[:, src:src + MLA_NOPE].astype(BF16)
        k_ref[:, dst + MLA_NOPE:dst + MLA_QK_PAD] = pe
        v_ref[:, hd * MLA_V:(hd + 1) * MLA_V] = acc[:, src + MLA_NOPE:src + MLA_NOPE + MLA_V].astype(BF16)


def mla_keys_values(z, kv_norm, w_ukv, cos, sin, t, tm=512):
    m = z.shape[0]
    tm = min(tm, t)
    nt = t // tm
    return pl.pallas_call(
        _mla_kv_kernel,
        out_shape=[jax.ShapeDtypeStruct((m, MLA_HEADS * MLA_QK_PAD), BF16),
                   jax.ShapeDtypeStruct((m, MLA_WIDTH), BF16)],
        grid=(m // tm,),
        in_specs=[pl.BlockSpec((tm, MLA_KV_RANK), lambda i: (i, (GATE_COLS + MLA_Q_RANK) // MLA_KV_RANK)),
                  pl.BlockSpec((tm, LANES), lambda i: (i, OD_A_BLK + (MLA_Q_RANK + MLA_KV_RANK) // LANES)),
                  pl.BlockSpec((1, MLA_KV_RANK), lambda i: (0, 0)),
                  pl.BlockSpec(w_ukv.shape, lambda i: (0, 0)),
                  pl.BlockSpec((tm, LANES), lambda i: (i % nt, 0)),
                  pl.BlockSpec((tm, LANES), lambda i: (i % nt, 0))],
        out_specs=[pl.BlockSpec((tm, MLA_HEADS * MLA_QK_PAD), lambda i: (i, 0)),
                   pl.BlockSpec((tm, MLA_WIDTH), lambda i: (i, 0))],
        compiler_params=_cparams(1),
        name="mla_keys_values",
    )(z, z, kv_norm.reshape(1, -1), w_ukv, cos, sin)


def _mla_attn_kernel(q_ref, k_ref, v_ref, kc_ref, vc_ref, o_ref, *, scale):
    parts = [(kc_ref[...], None, vc_ref[...])]
    nkeys = k_ref.shape[0]
    step = min(MLA_KEY_CHUNK, nkeys)
    for lo in range(0, nkeys, step):
        parts.append((k_ref[lo:lo + step, :], None, v_ref[lo:lo + step, :]))
    o_ref[...] = _softmax_pv(q_ref[...], parts, scale)


def mla_attention(q, k, v, kc, vc, batch, t, l, tq=2048):
    tq = min(tq, t)
    nq = t // tq
    return pl.pallas_call(
        functools.partial(_mla_attn_kernel, scale=(MLA_NOPE + MLA_ROPE) ** -0.5),
        out_shape=jax.ShapeDtypeStruct((batch * t, MLA_WIDTH), F32),
        grid=(batch, MLA_HEADS, nq),
        in_specs=[pl.BlockSpec((tq, MLA_QK_PAD), lambda b, h, i: (b * nq + i, h)),
                  pl.BlockSpec((t, MLA_QK_PAD), lambda b, h, i: (b, h)),
                  pl.BlockSpec((t, MLA_V), lambda b, h, i: (b, h)),
                  pl.BlockSpec((l, MLA_QK_PAD), lambda b, h, i: (b, h)),
                  pl.BlockSpec((l, MLA_V), lambda b, h, i: (b, h))],
        out_specs=pl.BlockSpec((tq, MLA_V), lambda b, h, i: (b * nq + i, h)),
        compiler_params=_cparams(3),
        name="mla_attention",
    )(q, k, v, kc, vc)


def _rope_tables(t):
    tok = np.arange(t)
    pos = np.stack([tok // GRID_W, tok % GRID_W], axis=-1).astype(np.float32)
    n_freq = MLA_ROPE // 4
    inv = (ROPE_BASE ** (-jnp.arange(n_freq, dtype=F32) / n_freq))
    ang = (jnp.asarray(pos)[:, :, None] * inv).reshape(t, MLA_ROPE // 2)
    cos = jnp.repeat(jnp.cos(ang), 2, axis=-1)
    sin = jnp.repeat(jnp.sin(ang), 2, axis=-1) * jnp.tile(jnp.array([-1.0, 1.0], F32), MLA_ROPE // 2)
    pad = ((0, 0), (0, LANES - MLA_ROPE))
    return jnp.pad(cos, pad), jnp.pad(sin, pad)


def _split3(x):
    hi = x.astype(BF16)
    r1 = x - hi.astype(F32)
    mid = r1.astype(BF16)
    lo = (r1 - mid.astype(F32)).astype(BF16)
    return hi, mid, lo


def _hg_kernel(q_ref, f_ref, v_ref, lb_ref, s0_ref, *rest, reverse, finalize, emit_state):
    st_ref = rest[-1]
    rest = rest[:-1]
    if finalize:
        prev_ref, nw_ref = rest[0], rest[1]
        rest = rest[2:]
    o_ref = rest[0]

    @pl.when(pl.program_id(2) == 0)
    def _():
        st_ref[...] = s0_ref[...]

    bt = q_ref.shape[0]
    nchunk = bt // HG_CHUNK
    order = range(nchunk - 1, -1, -1) if reverse else range(nchunk)
    ri = lax.broadcasted_iota(jnp.int32, (bt, bt), 0)
    ci = lax.broadcasted_iota(jnp.int32, (bt, bt), 1)
    same = (ri // HG_CHUNK) == (ci // HG_CHUNK)
    tri = same & ((ci >= ri) if reverse else (ci <= ri))
    tri_f = tri.astype(F32)
    tri_b = tri.astype(BF16)
    row_chunk = lax.broadcasted_iota(jnp.int32, (bt, HG_DK), 0) // HG_CHUNK
    in_chunk = [row_chunk == c for c in range(nchunk)]

    lb = lb_ref[...]
    fz = f_ref[...]
    log_sig = jnp.minimum(fz, 0.0) - jnp.log1p(jnp.exp(-jnp.abs(fz)))
    la = jnp.log(lb)
    lbb = jnp.log1p(-lb) + log_sig
    log_f = jnp.maximum(la, lbb) + jnp.log1p(jnp.exp(-jnp.abs(la - lbb)))
    kf_all = (1.0 - lb) * _sigmoid(-fz)
    pieces = jnp.dot(tri_b, jnp.concatenate(_split3(log_f), axis=1), preferred_element_type=F32)
    width = log_f.shape[1]
    cum_all = pieces[:, :width] + pieces[:, width:2 * width] + pieces[:, 2 * width:]

    for hh in range(HG_STEP_HEADS):
        sl = slice(hh * HG_DK, (hh + 1) * HG_DK)
        cum, kf, v = cum_all[:, sl], kf_all[:, sl], v_ref[:, sl]
        last = [cum[c * HG_CHUNK:c * HG_CHUNK + 1] if reverse else cum[(c + 1) * HG_CHUNK - 1:(c + 1) * HG_CHUNK]
                for c in range(nchunk)]
        tot = jnp.concatenate([jnp.broadcast_to(x, (HG_CHUNK, HG_DK)) for x in last], axis=0)
        q_in = q_ref[:, sl] * jnp.exp(cum)
        k_in = kf * jnp.exp(-cum)
        k_end = kf * jnp.exp(tot - cum)
        o_intra = _dot(_dot_nt(q_in, k_in) * tri_f, v)
        v_by_chunk = jnp.concatenate([jnp.where(in_chunk[c], v, 0.0) for c in range(nchunk)], axis=1)
        d_st = _dot_tn(v_by_chunk, k_end)
        st = st_ref[hh]
        before = [None] * nchunk
        for c in order:
            before[c] = st
            st = st * jnp.exp(last[c]) + d_st[c * HG_DV:(c + 1) * HG_DV]
        st_ref[hh] = st
        q_by_chunk = jnp.concatenate([jnp.where(in_chunk[c], q_in, 0.0) for c in range(nchunk)], axis=1)
        o = o_intra + _dot_nt(q_by_chunk, jnp.concatenate(before, axis=1))
        if finalize:
            o = o + prev_ref[:, sl]
            o = o * lax.rsqrt(jnp.mean(o * o, axis=-1, keepdims=True) + NORM_EPS) * nw_ref[...]
        o_ref[:, sl] = o
        if emit_state:
            rest[1][hh] = st


def hgrn2_pass(z, lb, s0, batch, t, f_off, reverse, prev=None, norm_w=None, emit_state=False):
    bt = min(HG_BLOCK, t)
    nblk = t // bt
    nh = HG_STEP_HEADS
    wide = nh * LANES
    finalize = prev is not None
    assert OD_HG_BLK % nh == 0 and HG_HEADS % nh == 0, "head groups must start on a whole column block"
    pos = (lambda i: nblk - 1 - i) if reverse else (lambda i: i)
    blk = lambda off: pl.BlockSpec((bt, wide), lambda b, h, i, off=off: (b * nblk + pos(i), off // nh + h))
    state = pl.BlockSpec((None, nh, HG_DV, HG_DK), lambda b, h, i: (b, h, 0, 0))
    ins = [z, z, z, lb.reshape(1, -1), s0]
    specs = [blk(OD_HG_BLK), blk(OD_HG_BLK + f_off), blk(OD_HG_BLK + 3 * HG_HEADS),
             pl.BlockSpec((1, wide), lambda b, h, i: (0, h)), state]
    if finalize:
        ins += [prev, norm_w.reshape(1, -1)]
        specs += [blk(0), pl.BlockSpec((1, HG_DV), lambda b, h, i: (0, 0))]
    out_shape = [jax.ShapeDtypeStruct((batch * t, HG_WIDTH), F32)]
    out_specs = [blk(0)]
    if emit_state:
        out_shape.append(jax.ShapeDtypeStruct((batch, HG_HEADS, HG_DV, HG_DK), F32))
        out_specs.append(state)
    return pl.pallas_call(
        functools.partial(_hg_kernel, reverse=reverse, finalize=finalize, emit_state=emit_state),
        out_shape=out_shape,
        grid=(batch, HG_HEADS // nh, nblk),
        in_specs=specs,
        out_specs=out_specs,
        scratch_shapes=[pltpu.VMEM((nh, HG_DV, HG_DK), F32)],
        compiler_params=_cparams(3),
        name="hgrn2_" + ("bwd" if reverse else "fwd"),
    )(*ins)


def _even_layer(x, xc, mod, norm_w, w_in, w_out, rpb, rw_params, batch, t, l):
    s2 = 3 * NA_WIDTH + RW_SHIFT_COLS
    n_gate, n_rest = GATE_COLS // LANES, s2 // LANES
    plan = ([(_COPY, n_rest + i) for i in range(n_gate)] + [(_COPY, i) for i in range(n_rest)]
            + [(_ZERO, 0)] * ((RW_PAD_COLS - RW_SHIFT_COLS) // LANES))
    w_ext = cast_reorder(w_in, plan, "even_w_in")
    tiles_per_seq = max(t // 1024, 1)
    z = in_projection(x, norm_w, mod, lambda i: i // tiles_per_seq, w_ext, "even_in_proj")
    z_c = in_projection(xc, norm_w, mod, lambda i: batch, w_ext, "even_in_proj_ctx")

    y_na = na_attention(z, z_c, rpb, batch, t, l)
    yc_na = ctx_attention(z_c, batch, l)

    mu, w0, w2, a0, a2, k_k, k_a, r_k, ln_w, ln_b = rw_params
    mu = jnp.pad(mu, ((0, 0), (0, RW_PAD_COLS - RW_SHIFT_COLS)))
    zpad = jnp.zeros((RW_RANK, RW_WIDTH), F32)
    w2p = jnp.stack([jnp.concatenate([w2[0], zpad]), jnp.concatenate([zpad, w2[1]])])
    a2p = jnp.stack([jnp.concatenate([a2[0], zpad]), jnp.concatenate([zpad, a2[1]])])
    vecs = (mu, w0, w2p, a0, a2p, k_k.reshape(1, -1), k_a.reshape(1, -1), r_k.reshape(1, -1))
    terms = rwkv_terms(z, batch, t, *vecs)
    terms_c = rwkv_terms(z_c, batch, l, *vecs)
    s_zero = jnp.zeros((RW_HEAD_DIM, RW_HEAD_DIM, LANES), F32)
    yc_f, yc_b, s_ctx = rwkv_scan(*terms_c[:5], s_zero)
    y_f, y_b, _ = rwkv_scan(*terms[:5], s_ctx)
    y_rw = rwkv_readout(y_f, y_b, terms[5], ln_w, ln_b).reshape(batch * t, RW_WIDTH)
    yc_rw = rwkv_readout(yc_f, yc_b, terms_c[5], ln_w, ln_b).reshape(batch * l, RW_WIDTH)

    w_out = w_out.astype(BF16)
    rows_per_seq = max(t // 256, 1)
    x = out_projection(y_na, y_rw, z, w_out, x, mod, lambda i: i // rows_per_seq, "even_out_proj")
    xc = out_projection(yc_na, yc_rw, z_c, w_out, xc, mod, lambda i: batch, "even_out_proj_ctx")
    return x, xc


def _odd_layer(x, xc, mod, norm_w, w_in, w_out, q_norm, w_uq, kv_norm, w_ukv, lb, hg_norm_w, final_w,
               batch, t, l):
    o3 = MLA_Q_RANK + MLA_KV_RANK + MLA_ROPE
    o4 = o3 + 4 * HG_FDIM
    assert o3 % LANES == LANES // 2
    a_full = o3 // LANES
    plan = ([(_SHIFTED, o4 // LANES + i) for i in range(GATE_COLS // LANES)]
            + [(_COPY, i) for i in range(a_full)] + [(_FIRST_HALF, a_full)]
            + [(_ZERO, 0)] * (MLA_A_COLS // LANES - a_full - 1)
            + [(_SHIFTED, a_full + i) for i in range(4 * HG_FDIM // LANES)])
    w_ext = cast_reorder(w_in, plan, "odd_w_in")
    tiles_per_seq = max(t // 1024, 1)
    z = in_projection(x, norm_w, mod, lambda i: i // tiles_per_seq, w_ext, "odd_in_proj")
    z_c = in_projection(xc, norm_w, mod, lambda i: batch, w_ext, "odd_in_proj_ctx")

    cos, sin = _rope_tables(t)
    qk = MLA_NOPE + MLA_ROPE
    w_uq_pad = jnp.pad(w_uq.reshape(MLA_Q_RANK, MLA_HEADS, qk), ((0, 0), (0, 0), (0, MLA_QK_PAD - qk)))
    w_uq_pad = w_uq_pad.reshape(MLA_Q_RANK, MLA_HEADS * MLA_QK_PAD).astype(BF16)
    w_ukv = w_ukv.astype(BF16)
    q = mla_queries(z, q_norm, w_uq_pad, cos, sin, t)
    k, v = mla_keys_values(z, kv_norm, w_ukv, cos, sin, t)
    ones = jnp.ones((l, LANES), F32)
    kc, vc = mla_keys_values(z_c, kv_norm, w_ukv, ones, jnp.zeros_like(ones), l)
    y_mla = mla_attention(q, k, v, kc, vc, batch, t, l)

    s_zero = jnp.zeros((batch, HG_HEADS, HG_DV, HG_DK), F32)
    _, s_f = hgrn2_pass(z_c, lb, s_zero, batch, l, HG_HEADS, False, emit_state=True)
    _, s_b = hgrn2_pass(z_c, lb, s_zero, batch, l, 2 * HG_HEADS, True, emit_state=True)
    o_f, = hgrn2_pass(z, lb, s_f, batch, t, HG_HEADS, False)
    y_hg, = hgrn2_pass(z, lb, s_b, batch, t, 2 * HG_HEADS, True, prev=o_f, norm_w=hg_norm_w)

    rows_per_seq = max(t // 256, 1)
    return out_projection(y_mla, y_hg, z, w_out.astype(BF16), x, mod, lambda i: i // rows_per_seq,
                          "odd_out_proj", final_w=final_w)


def kernel(x, c, ctx, c_ctx, ada_w, ada_b, norm_w, e_w_in, e_w_out, na_rpb, rw_mu, rw_w0, rw_w2, rw_a0, rw_a2, rw_k_k, rw_k_a, rw_r_k, rw_ln_w, rw_ln_b, o_w_in, o_w_out, mla_q_norm, mla_w_uq, mla_kv_norm, mla_w_ukv, hg_lower_bounds, hg_norm_w, final_norm_w):
    batch, t, d = x.shape
    l = ctx.shape[1]
    assert ada_w.shape[0] == 2, "one even and one odd layer"
    cond = jnp.concatenate([c, c_ctx[None, :], jnp.zeros((8 - batch - 1, d), F32)], axis=0)
    mod = modulation(cond, ada_w, ada_b)
    s = jax.nn.softmax(hg_lower_bounds.astype(F32), axis=0)
    lower = jnp.cumsum(s, axis=0) - s[0]

    xf, xcf = x.reshape(batch * t, d), ctx.reshape(batch * l, d)
    rw_params = (rw_mu[0], rw_w0[0], rw_w2[0], rw_a0[0], rw_a2[0], rw_k_k[0], rw_k_a[0], rw_r_k[0],
                 rw_ln_w[0], rw_ln_b[0])
    xf, xcf = _even_layer(xf, xcf, mod[0].reshape(8, 1, 3 * d), norm_w[0], e_w_in[0], e_w_out[0], na_rpb[0],
                          rw_params, batch, t, l)
    out = _odd_layer(xf, xcf, mod[1].reshape(8, 1, 3 * d), norm_w[1], o_w_in[0], o_w_out[0], mla_q_norm[0],
                     mla_w_uq[0], mla_kv_norm[0], mla_w_ukv[0], lower[1], hg_norm_w[0], final_norm_w,
                     batch, t, l)
    return out.reshape(batch, t, d)
```

```python
import functools
import math

import numpy as np
import jax
import jax.numpy as jnp
from jax import lax
from jax.experimental import pallas as pl
from jax.experimental.pallas import tpu as pltpu

F32 = jnp.float32
BF16 = jnp.bfloat16

GRID_W = 64
NORM_EPS = 1e-6
ROPE_BASE = 10000.0

NA_HEADS = 8
NA_HEAD_DIM = 128
NA_WIDTH = NA_HEADS * NA_HEAD_DIM
NA_WIN_ROWS = 8
NA_WIN_COLS = 16
NA_QROWS = 4
NA_KROWS = 12
NA_STEP_HEADS = 4

RW_HEAD_DIM = 64
RW_HEADS = 16
RW_WIDTH = RW_HEADS * RW_HEAD_DIM
RW_RANK = 64
RW_SHIFT_COLS = 3 * RW_WIDTH + 4 * RW_RANK
RW_PAD_COLS = 3584
RW_GN_EPS = 64e-5
RW_SCAN_CHUNK = 64
RW_DECAY_SCALE = math.exp(-0.5)
RW_STATE_PAD = 0

MLA_HEADS = 8
MLA_Q_RANK = 512
MLA_KV_RANK = 512
MLA_NOPE = 128
MLA_ROPE = 64
MLA_V = 128
MLA_QK_PAD = 256
MLA_WIDTH = MLA_HEADS * MLA_V
MLA_A_COLS = 1536
MLA_KEY_CHUNK = 512

HG_HEADS = 8
HG_DK = 128
HG_DV = 128
HG_FDIM = HG_HEADS * HG_DK
HG_WIDTH = HG_HEADS * HG_DV
HG_CHUNK = 32
HG_BLOCK = 256
HG_STEP_HEADS = 4

LANES = 128

GATE_COLS = 2048
EV_Q_BLK = GATE_COLS // LANES
EV_RW_BLK = EV_Q_BLK + 3 * NA_WIDTH // LANES
OD_A_BLK = GATE_COLS // LANES
OD_HG_BLK = OD_A_BLK + MLA_A_COLS // LANES
VMEM_LIMIT = 48 * 1024 * 1024
NEG_INF = -1e30
LOG2E = math.log2(math.e)


def _cparams(n_axes):
    return pltpu.CompilerParams(dimension_semantics=("arbitrary",) * n_axes, vmem_limit_bytes=VMEM_LIMIT)


def _sigmoid(x):
    return 1.0 / (1.0 + jnp.exp(-x))


def _silu(x):
    return x * _sigmoid(x)


def _dot(a, b):
    return jnp.dot(a.astype(BF16), b.astype(BF16), preferred_element_type=F32)


def _dot_nt(a, b):
    return lax.dot_general(a.astype(BF16), b.astype(BF16), (((1,), (1,)), ((), ())), preferred_element_type=F32)


def _dot_tn(a, b):
    return lax.dot_general(a.astype(BF16), b.astype(BF16), (((0,), (0,)), ((), ())), preferred_element_type=F32)


def _mod_kernel(c_ref, w_ref, b_ref, o_ref):
    o_ref[...] = _dot(_silu(c_ref[...]), w_ref[...]) + b_ref[...]


def modulation(cond, ada_w, ada_b, tn=512):
    depth, d, n = ada_w.shape
    return pl.pallas_call(
        _mod_kernel,
        out_shape=jax.ShapeDtypeStruct((depth, 8, n), F32),
        grid=(depth, n // tn),
        in_specs=[pl.BlockSpec((8, d), lambda l, j: (0, 0)),
                  pl.BlockSpec((None, d, tn), lambda l, j: (l, 0, j)),
                  pl.BlockSpec((None, 1, tn), lambda l, j: (l, 0, j))],
        out_specs=pl.BlockSpec((None, 8, tn), lambda l, j: (l, 0, j)),
        compiler_params=_cparams(2),
        name="adaln_modulation",
    )(cond, ada_w, ada_b.reshape(depth, 1, n))


def _cast_reorder_kernel(src_ref, zero_ref, x_ref, o_ref):
    del src_ref
    o_ref[...] = jnp.where(zero_ref[pl.program_id(0)] != 0, 0.0, x_ref[...]).astype(BF16)


def cast_reorder(w, plan, width, name):
    d = w.shape[0]
    src = np.array([0 if a is None else a for a in plan], np.int32)
    zero = np.array([a is None for a in plan], np.int32)
    return pl.pallas_call(
        _cast_reorder_kernel,
        out_shape=jax.ShapeDtypeStruct((d, width * len(plan)), BF16),
        grid_spec=pltpu.PrefetchScalarGridSpec(
            num_scalar_prefetch=2,
            grid=(len(plan),),
            in_specs=[pl.BlockSpec((d, width), lambda j, src, zero: (0, src[j]))],
            out_specs=pl.BlockSpec((d, width), lambda j, src, zero: (0, j))),
        compiler_params=_cparams(1),
        name=name,
    )(jnp.asarray(src), jnp.asarray(zero), w)


def _inproj_kernel(x_ref, nw_ref, shift_ref, scale_ref, w_ref, o_ref, h_ref):
    @pl.when(pl.program_id(1) == 0)
    def _():
        x = x_ref[...]
        y = x * lax.rsqrt(jnp.mean(x * x, axis=-1, keepdims=True) + NORM_EPS) * nw_ref[...]
        h_ref[...] = (y * (1.0 + scale_ref[...]) + shift_ref[...]).astype(BF16)

    o_ref[...] = jnp.dot(h_ref[...], w_ref[...], preferred_element_type=F32)


def in_projection(x, norm_w, mod, mod_row, w, name, tm=1024, tn=512):
    m, d = x.shape
    n = w.shape[1]
    tm = min(tm, m)
    return pl.pallas_call(
        _inproj_kernel,
        out_shape=jax.ShapeDtypeStruct((m, n), F32),
        grid=(m // tm, n // tn),
        in_specs=[pl.BlockSpec((tm, d), lambda i, j: (i, 0)),
                  pl.BlockSpec((1, d), lambda i, j: (0, 0)),
                  pl.BlockSpec((None, 1, d), lambda i, j: (mod_row(i), 0, 0)),
                  pl.BlockSpec((None, 1, d), lambda i, j: (mod_row(i), 0, 1)),
                  pl.BlockSpec((d, tn), lambda i, j: (0, j))],
        out_specs=pl.BlockSpec((tm, tn), lambda i, j: (i, j)),
        scratch_shapes=[pltpu.VMEM((tm, d), BF16)],
        compiler_params=_cparams(2),
        name=name,
    )(x, norm_w.reshape(1, d), mod, mod, w)


def _outproj_kernel(ya_ref, yb_ref, g_ref, w_ref, x_ref, gm_ref, *rest, ka, final):
    o_ref = rest[-1]
    sg = _silu(g_ref[...])
    acc = _dot(ya_ref[...].astype(F32) * sg[:, :ka], w_ref[:ka, :])
    acc += _dot(yb_ref[...].astype(F32) * sg[:, ka:], w_ref[ka:, :])
    out = x_ref[...] + gm_ref[...] * acc
    if final:
        out = out * lax.rsqrt(jnp.mean(out * out, axis=-1, keepdims=True) + NORM_EPS) * rest[0][...]
    o_ref[...] = out


def out_projection(ya, yb, gate, w, x, mod, mod_row, name, final_w=None, tm=256):
    m, d = x.shape
    ka, kb = ya.shape[1], yb.shape[1]
    tm = min(tm, m)
    ins = [ya, yb, gate, w, x, mod]
    specs = [pl.BlockSpec((tm, ka), lambda i: (i, 0)),
             pl.BlockSpec((tm, kb), lambda i: (i, 0)),
             pl.BlockSpec((tm, ka + kb), lambda i: (i, 0)),
             pl.BlockSpec((ka + kb, d), lambda i: (0, 0)),
             pl.BlockSpec((tm, d), lambda i: (i, 0)),
             pl.BlockSpec((None, 1, d), lambda i: (mod_row(i), 0, 2))]
    if final_w is not None:
        ins.append(final_w.reshape(1, d))
        specs.append(pl.BlockSpec((1, d), lambda i: (0, 0)))
    return pl.pallas_call(
        functools.partial(_outproj_kernel, ka=ka, final=final_w is not None),
        out_shape=jax.ShapeDtypeStruct((m, d), F32),
        grid=(m // tm,),
        in_specs=specs,
        out_specs=pl.BlockSpec((tm, d), lambda i: (i, 0)),
        compiler_params=_cparams(1),
        name=name,
    )(*ins)


def _softmax_pv(q, parts, scale):
    c = scale * LOG2E
    m = denom = num = None
    for k, bias, v in parts:
        x = _dot_nt(q, k) * c
        if bias is not None:
            x = x + bias
        m_part = jnp.max(x, axis=-1, keepdims=True)
        if m is None:
            m = m_part
            p = jnp.exp2(x - m)
            denom = jnp.sum(p, axis=-1, keepdims=True)
            num = _dot(p, v)
        else:
            m_new = jnp.maximum(m, m_part)
            alpha = jnp.exp2(m - m_new)
            p = jnp.exp2(x - m_new)
            denom = denom * alpha + jnp.sum(p, axis=-1, keepdims=True)
            num = num * alpha + _dot(p, v)
            m = m_new
    return num / denom


def _softmax_pv_joint(q, parts, scale):
    c = scale * LOG2E
    logits = [_dot_nt(q, k) * c if bias is None else _dot_nt(q, k) * c + bias for k, bias, _ in parts]
    m = functools.reduce(jnp.maximum, [jnp.max(x, axis=-1, keepdims=True) for x in logits])
    ps = [jnp.exp2(x - m) for x in logits]
    denom = functools.reduce(lambda a, b: a + b, [jnp.sum(p, axis=-1, keepdims=True) for p in ps])
    num = functools.reduce(lambda a, b: a + b, [_dot(p, v) for p, (_, _, v) in zip(ps, parts)])
    return num / denom


def _na_kernel(idx_ref, q_ref, k_ref, v_ref, kc_ref, vc_ref, tab_ref, o_ref, bias_ref, *, rows, scale):
    j = pl.program_id(2)
    npair = NA_KROWS // 2
    start = pl.multiple_of(jnp.clip(NA_QROWS * j - NA_WIN_ROWS // 2, 0, rows - NA_KROWS) * GRID_W, GRID_W)
    win = pl.ds(start, NA_KROWS * GRID_W)
    for hd in range(NA_STEP_HEADS):
        sl = slice(hd * LANES, (hd + 1) * LANES)
        for qr in range(NA_QROWS):
            for m in range(npair):
                tile = tab_ref[hd, idx_ref[(j * NA_QROWS + qr) * npair + m]]
                bias_ref[hd, qr * GRID_W:(qr + 1) * GRID_W, m * LANES:(m + 1) * LANES] = tile
        parts = [(k_ref[win, sl], bias_ref[hd], v_ref[win, sl]), (kc_ref[:, sl], None, vc_ref[:, sl])]
        o_ref[:, sl] = _softmax_pv_joint(q_ref[:, sl], parts, scale)


def _na_tile_ids(rows):
    kh = min(NA_WIN_ROWS, rows)
    n = 2 * NA_WIN_ROWS - 1
    ids = []
    for j in range(rows // NA_QROWS):
        ks = int(np.clip(NA_QROWS * j - NA_WIN_ROWS // 2, 0, rows - NA_KROWS))
        for qr in range(NA_QROWS):
            r = NA_QROWS * j + qr
            r0 = int(np.clip(r - kh // 2, 0, rows - kh))
            for m in range(NA_KROWS // 2):
                kr = ks + 2 * m
                dy = kr - r + NA_WIN_ROWS - 1
                first, second = r0 <= kr < r0 + kh, r0 <= kr + 1 < r0 + kh
                ids.append(dy if first and second else n + dy if first else 2 * n + dy + 1 if second else 3 * n)
    return np.array(ids, np.int32)


def _na_bias_tiles(rpb):
    col = np.arange(GRID_W)
    cs = np.clip(col - NA_WIN_COLS // 2, 0, GRID_W - NA_WIN_COLS)
    kc = col[None, :]
    in_win = (kc >= cs[:, None]) & (kc < cs[:, None] + NA_WIN_COLS)
    dx = np.where(in_win, kc - col[:, None] + NA_WIN_COLS - 1, 0)
    tab = jnp.where(in_win[None, None], rpb[:, :, dx] * LOG2E, NEG_INF)
    neg = jnp.full_like(tab[:, :1], NEG_INF)
    nxt = jnp.concatenate([tab[:, 1:], neg], axis=1)
    both = jnp.concatenate([tab, nxt], axis=-1)
    first = jnp.concatenate([tab, jnp.broadcast_to(neg, tab.shape)], axis=-1)
    second = jnp.concatenate([jnp.broadcast_to(neg, tab.shape), tab], axis=-1)
    return jnp.concatenate([both, first, second, jnp.concatenate([neg, neg], axis=-1)], axis=1)


def na_attention(z, z_c, rpb, batch, t, l):
    rows = t // GRID_W
    nj = rows // NA_QROWS
    tq = NA_QROWS * GRID_W
    ids = jnp.asarray(_na_tile_ids(rows))
    tiles = _na_bias_tiles(rpb)
    nh = NA_STEP_HEADS
    wide = nh * LANES
    q0, k0, v0 = (EV_Q_BLK + i * NA_HEADS for i in range(3))
    assert q0 % nh == 0 and NA_HEADS % nh == 0, "head groups must start on a whole column block"
    return pl.pallas_call(
        functools.partial(_na_kernel, rows=rows, scale=NA_HEAD_DIM ** -0.5),
        out_shape=jax.ShapeDtypeStruct((batch * t, NA_WIDTH), F32),
        grid_spec=pltpu.PrefetchScalarGridSpec(
            num_scalar_prefetch=1,
            grid=(NA_HEADS // nh, batch, nj),
            in_specs=[pl.BlockSpec((tq, wide), lambda h, b, j, g: (b * nj + j, q0 // nh + h)),
                      pl.BlockSpec((t, wide), lambda h, b, j, g: (b, k0 // nh + h)),
                      pl.BlockSpec((t, wide), lambda h, b, j, g: (b, v0 // nh + h)),
                      pl.BlockSpec((l, wide), lambda h, b, j, g: (b, k0 // nh + h)),
                      pl.BlockSpec((l, wide), lambda h, b, j, g: (b, v0 // nh + h)),
                      pl.BlockSpec((nh,) + tiles.shape[1:], lambda h, b, j, g: (h, 0, 0, 0))],
            out_specs=pl.BlockSpec((tq, wide), lambda h, b, j, g: (b * nj + j, h)),
            scratch_shapes=[pltpu.VMEM((nh, tq, NA_KROWS * GRID_W), F32)]),
        compiler_params=_cparams(3),
        name="na_attention",
    )(ids, z, z, z, z_c, z_c, tiles)


def _dense_attn_kernel(q_ref, k_ref, v_ref, o_ref, *, scale):
    o_ref[...] = _softmax_pv(q_ref[...], [(k_ref[...], None, v_ref[...])], scale)


def ctx_attention(z_c, batch, l):
    h8 = NA_HEADS
    return pl.pallas_call(
        functools.partial(_dense_attn_kernel, scale=NA_HEAD_DIM ** -0.5),
        out_shape=jax.ShapeDtypeStruct((batch * l, NA_WIDTH), F32),
        grid=(batch, h8),
        in_specs=[pl.BlockSpec((l, LANES), lambda b, h: (b, EV_Q_BLK + h)),
                  pl.BlockSpec((l, LANES), lambda b, h: (b, EV_Q_BLK + h8 + h)),
                  pl.BlockSpec((l, LANES), lambda b, h: (b, EV_Q_BLK + 2 * h8 + h))],
        out_specs=pl.BlockSpec((l, LANES), lambda b, h: (b, h)),
        compiler_params=_cparams(2),
        name="ctx_attention",
    )(z_c, z_c, z_c)


def _head_sum(x):
    first = lax.broadcasted_iota(jnp.int32, x.shape, 1) < RW_HEAD_DIM
    s0 = jnp.sum(jnp.where(first, x, 0.0), axis=-1, keepdims=True)
    s1 = jnp.sum(jnp.where(first, 0.0, x), axis=-1, keepdims=True)
    return jnp.where(first, s0, s1)


def _rw_terms_kernel(ur_ref, uk_ref, uv_ref, uwa_ref, pr_ref, pk_ref, pv_ref, pwa_ref, nr_ref, nk_ref, nv_ref,
                     nwa_ref, mu_ref, w0_ref, w2_ref, a0_ref, a2_ref, kk_ref, ka_ref, rk_ref,
                     rv_o, wk0_o, wk1_o, bk0_o, bk1_o, bonus_o, *, nt):
    i = pl.program_id(1)
    tm = ur_ref.shape[0]

    def shifted(u_ref, p_ref, n_ref, col0):
        u = u_ref[...]
        mu = mu_ref[:, col0:col0 + u.shape[1]]
        row = lax.broadcasted_iota(jnp.int32, u.shape, 0)
        before = jnp.where(i == 0, 0.0, p_ref[7:8, :])
        after = jnp.where(i == nt - 1, 0.0, n_ref[0:1, :])
        prev = jnp.where(row == 0, before, pltpu.roll(u, 1, axis=0))
        nxt = jnp.where(row == tm - 1, after, pltpu.roll(u, tm - 1, axis=0))
        return u + mu[0:1] * (prev - u) + mu[1:2] * (nxt - u)

    r_all = shifted(ur_ref, pr_ref, nr_ref, 0)
    k_all = shifted(uk_ref, pk_ref, nk_ref, RW_WIDTH)
    v_all = shifted(uv_ref, pv_ref, nv_ref, 2 * RW_WIDTH)
    wa = shifted(uwa_ref, pwa_ref, nwa_ref, 3 * RW_WIDTH)
    wd = jnp.tanh(wa[:, :LANES]).astype(BF16)
    ad = wa[:, LANES:].astype(BF16)
    first = lax.broadcasted_iota(jnp.int32, (tm, LANES), 1) < RW_HEAD_DIM

    def head_rows(x, y):
        return (jnp.where(first, x, pltpu.roll(y, RW_HEAD_DIM, axis=1)),
                jnp.where(first, pltpu.roll(x, RW_HEAD_DIM, axis=1), y))

    outs = (rv_o, wk0_o, wk1_o, bk0_o, bk1_o)
    even = [[] for _ in outs]
    odd = [[] for _ in outs]
    bonus = []
    for p in range(RW_WIDTH // LANES):
        sl = slice(p * LANES, (p + 1) * LANES)
        r, k, v = r_all[:, sl], k_all[:, sl], v_all[:, sl]
        kk = k * kk_ref[:, sl]
        kk = kk / jnp.maximum(jnp.sqrt(_head_sum(kk * kk)), 1e-12)
        kd_sum = jnp.zeros_like(k)
        per_dir = []
        for d in range(2):
            w = w0_ref[d:d + 1, sl] + jnp.dot(wd, w2_ref[d, :, sl].astype(BF16), preferred_element_type=F32)
            a = _sigmoid(a0_ref[d:d + 1, sl] + jnp.dot(ad, a2_ref[d, :, sl].astype(BF16), preferred_element_type=F32))
            kd = k * (1.0 + (a - 1.0) * ka_ref[:, sl])
            dec = jnp.exp(-RW_DECAY_SCALE * _sigmoid(w))
            per_dir.append(((dec, kd), (kk * a, kk)))
            kd_sum = kd_sum + kd
        pairs = [(r, v), per_dir[0][0], per_dir[1][0], per_dir[0][1], per_dir[1][1]]
        bonus.append(_head_sum(r * kd_sum * rk_ref[:, sl]) * v)
        for ev, od, (x, y) in zip(even, odd, pairs):
            e, o = head_rows(x, y)
            ev.append(e)
            od.append(o)
    for o_ref, ev, od in zip(outs, even, odd):
        o_ref[...] = jnp.swapaxes(jnp.stack(ev + od, axis=0), 0, 1)
    bonus_o[...] = jnp.swapaxes(jnp.stack(bonus, axis=0), 0, 1)


def rwkv_terms(z, n_seq, t, mu, w0, w2p, a0, a2p, k_k, k_a, r_k, tm=256):
    tm = min(tm, t)
    nt = t // tm
    npair = RW_WIDTH // LANES
    wide0 = EV_RW_BLK * LANES // RW_WIDTH
    wa_blk = (EV_RW_BLK + 3 * npair) // 2
    n8 = tm // 8

    def cur(width, cb):
        return pl.BlockSpec((tm, width), lambda b, i: (b * nt + i, cb))

    def before(width, cb):
        return pl.BlockSpec((8, width), lambda b, i: (jnp.maximum((b * nt + i) * n8 - 1, 0), cb))

    def after(width, cb):
        return pl.BlockSpec((8, width), lambda b, i: (jnp.minimum((b * nt + i + 1) * n8, n_seq * nt * n8 - 1), cb))

    cols = [(RW_WIDTH, wide0), (RW_WIDTH, wide0 + 1), (RW_WIDTH, wide0 + 2), (2 * LANES, wa_blk)]
    whole = lambda a: pl.BlockSpec(a.shape, lambda b, i: (0,) * a.ndim)
    params = (mu, w0, w2p, a0, a2p, k_k, k_a, r_k)
    out = pl.BlockSpec((tm, npair, LANES), lambda b, i: (i, b, 0))
    out2 = pl.BlockSpec((tm, 2 * npair, LANES), lambda b, i: (i, b, 0))
    return pl.pallas_call(
        functools.partial(_rw_terms_kernel, nt=nt),
        out_shape=[jax.ShapeDtypeStruct((t, n_seq * 2 * npair, LANES), F32)] * 5
        + [jax.ShapeDtypeStruct((t, n_seq * npair, LANES), F32)],
        grid=(n_seq, nt),
        in_specs=([cur(*c) for c in cols] + [before(*c) for c in cols] + [after(*c) for c in cols]
                  + [whole(a) for a in params]),
        out_specs=[out2] * 5 + [out],
        compiler_params=_cparams(2),
        name="rwkv_terms",
    )(*([z] * 12), *params)


def _to_lanes(n):
    q = jnp.concatenate([n, pltpu.roll(n, RW_HEAD_DIM, axis=1)], axis=0)
    return q.T[:RW_HEAD_DIM]


def _rw_scan_kernel(rvf, rvb, wk0, wk1, bk0, bk1, s0_ref, yf_ref, yb_ref, sfin_ref,
                    s_ref, z_ref, znext_ref, g_ref, ybuf_ref, *, chunk):
    n = RW_HEAD_DIM
    nb = rvf.shape[1] // (2 * RW_WIDTH // LANES)
    R, K, V, A, B = range(5)

    @pl.when(pl.program_id(0) == 0)
    def _():
        s_ref[:, :n, :] = s0_ref[...]

    def convert(t_src):
        def pair_tiles(f, bw):
            both = (f[t_src], bw[chunk - 1 - t_src])
            x = jnp.concatenate([src[(b * 2 + par) * 8:(b * 2 + par) * 8 + 8]
                                 for par in range(2) for src in both for b in range(nb)], axis=0)
            xt = x.T
            return xt[:n], xt[n:]

        r, v = pair_tiles(rvf, rvb)
        w, k = pair_tiles(wk0, wk1)
        b, kk = pair_tiles(bk0, bk1)
        g_prev = g_ref[0]
        g = g_prev * w
        g_ref[0] = g
        inv = 1.0 / g
        znext_ref[R] = r * g
        znext_ref[K] = k * inv
        znext_ref[V] = v
        znext_ref[A] = -(kk * g_prev)
        znext_ref[B] = b * inv

    def emit_y(t):
        rows = _to_lanes(ybuf_ref[...])
        yf_ref[t] = rows[:n // 2]
        yb_ref[chunk - 1 - t] = rows[n // 2:]

    g_ref[0] = jnp.ones((n, LANES), F32)
    ybuf_ref[...] = jnp.zeros((n, LANES), F32)
    convert(0)

    def step(t, carry):
        emit_y(jnp.maximum(t - 1, 0))
        z_ref[...] = znext_ref[...]
        g_ref[1] = g_ref[0]
        convert(jnp.minimum(t + 1, chunk - 1))
        sa = [jnp.zeros((n, LANES), F32), jnp.zeros((n, LANES), F32)]
        for k in range(n):
            sa[k % 2] = sa[k % 2] + s_ref[k, :n, :] * z_ref[A, k:k + 1, :]
        sa = sa[0] + sa[1]
        vt = z_ref[V]
        y = [jnp.zeros((n, LANES), F32), jnp.zeros((n, LANES), F32)]
        for k in range(n):
            sk = s_ref[k, :n, :] + sa * z_ref[B, k:k + 1, :] + vt * z_ref[K, k:k + 1, :]
            s_ref[k, :n, :] = sk
            y[k % 2] = y[k % 2] + sk * z_ref[R, k:k + 1, :]
        ybuf_ref[...] = y[0] + y[1]
        return carry

    lax.fori_loop(0, chunk, step, 0)
    emit_y(chunk - 1)
    for k in range(n):
        s_ref[k, :n, :] = s_ref[k, :n, :] * g_ref[1, k:k + 1, :]
    sfin_ref[...] = s_ref[:, :n, :]


def rwkv_scan(rv, wk0, wk1, bk0, bk1, s0):
    t, nrow, _ = rv.shape
    n = RW_HEAD_DIM
    chunk = RW_SCAN_CHUNK
    nc = t // chunk
    assert nrow == LANES // 2, "the states of both directions fill the 128 lanes"
    fwd = pl.BlockSpec((chunk, nrow, LANES), lambda i: (i, 0, 0))
    bwd = pl.BlockSpec((chunk, nrow, LANES), lambda i: (nc - 1 - i, 0, 0))
    yfwd = pl.BlockSpec((chunk, nrow // 2, LANES), lambda i: (i, 0, 0))
    ybwd = pl.BlockSpec((chunk, nrow // 2, LANES), lambda i: (nc - 1 - i, 0, 0))
    state = pl.BlockSpec((n, n, LANES), lambda i: (0, 0, 0))
    slab = jax.ShapeDtypeStruct((t, nrow // 2, LANES), F32)
    return pl.pallas_call(
        functools.partial(_rw_scan_kernel, chunk=chunk),
        out_shape=[slab, slab, jax.ShapeDtypeStruct((n, n, LANES), F32)],
        grid=(nc,),
        in_specs=[fwd, bwd, fwd, bwd, fwd, bwd, state],
        out_specs=[yfwd, ybwd, state],
        scratch_shapes=[pltpu.VMEM((n, n + RW_STATE_PAD, LANES), F32), pltpu.VMEM((5, n, LANES), F32),
                        pltpu.VMEM((5, n, LANES), F32), pltpu.VMEM((2, n, LANES), F32),
                        pltpu.VMEM((n, LANES), F32)],
        compiler_params=_cparams(1),
        name="rwkv_scan",
    )(rv, rv, wk0, wk1, bk0, bk1, s0)


def _rw_readout_kernel(yf_ref, yb_ref, bonus_ref, lw_ref, lb_ref, o_ref):
    npair = RW_WIDTH // LANES
    y_all = jnp.swapaxes(yf_ref[...] + yb_ref[...], 0, 1)
    bonus_all = jnp.swapaxes(bonus_ref[...], 0, 1)
    for q in range(yf_ref.shape[1]):
        b, p = divmod(q, npair)
        sl = slice(p * LANES, (p + 1) * LANES)
        y = y_all[q]
        mu = _head_sum(y) * (1.0 / RW_HEAD_DIM)
        d = y - mu
        var = _head_sum(d * d) * (1.0 / RW_HEAD_DIM)
        o_ref[b, :, sl] = d * lax.rsqrt(var + RW_GN_EPS) * lw_ref[:, sl] + lb_ref[:, sl] + bonus_all[q]


def rwkv_readout(y_f, y_b, bonus, ln_w, ln_b, tm=256):
    t, nq, _ = y_f.shape
    tm = min(tm, t)
    batch = nq // (RW_WIDTH // LANES)
    slab = pl.BlockSpec((tm, nq, LANES), lambda i: (i, 0, 0))
    vec = pl.BlockSpec((1, RW_WIDTH), lambda i: (0, 0))
    return pl.pallas_call(
        _rw_readout_kernel,
        out_shape=jax.ShapeDtypeStruct((batch, t, RW_WIDTH), F32),
        grid=(t // tm,),
        in_specs=[slab, slab, slab, vec, vec],
        out_specs=pl.BlockSpec((batch, tm, RW_WIDTH), lambda i: (0, i, 0)),
        compiler_params=_cparams(1),
        name="rwkv_readout",
    )(y_f, y_b, bonus, ln_w.reshape(1, -1), ln_b.reshape(1, -1))


def _rms(x, w):
    return x * lax.rsqrt(jnp.mean(x * x, axis=-1, keepdims=True) + NORM_EPS) * w


def _rope(x, cos, sin):
    odd = (lax.broadcasted_iota(jnp.int32, x.shape, 1) & 1) == 1
    swapped = jnp.where(odd, pltpu.roll(x, 1, axis=1), pltpu.roll(x, LANES - 1, axis=1))
    return x * cos + swapped * sin


def _mla_q_kernel(c_ref, nw_ref, w_ref, cos_ref, sin_ref, q_ref):
    h = _rms(c_ref[...], nw_ref[...]).astype(BF16)
    acc = jnp.dot(h, w_ref[...], preferred_element_type=F32)
    cos, sin = cos_ref[...], sin_ref[...]
    for hd in range(MLA_HEADS):
        lo = hd * MLA_QK_PAD
        q_ref[:, lo:lo + MLA_NOPE] = acc[:, lo:lo + MLA_NOPE].astype(BF16)
        q_ref[:, lo + MLA_NOPE:lo + MLA_QK_PAD] = _rope(acc[:, lo + MLA_NOPE:lo + MLA_QK_PAD], cos, sin).astype(BF16)


def mla_queries(z, q_norm, w_uq_pad, cos, sin, t, tm=512):
    m = z.shape[0]
    tm = min(tm, t)
    nt = t // tm
    width = MLA_HEADS * MLA_QK_PAD
    return pl.pallas_call(
        _mla_q_kernel,
        out_shape=jax.ShapeDtypeStruct((m, width), BF16),
        grid=(m // tm,),
        in_specs=[pl.BlockSpec((tm, MLA_Q_RANK), lambda i: (i, GATE_COLS // MLA_Q_RANK)),
                  pl.BlockSpec((1, MLA_Q_RANK), lambda i: (0, 0)),
                  pl.BlockSpec((MLA_Q_RANK, width), lambda i: (0, 0)),
                  pl.BlockSpec((tm, LANES), lambda i: (i % nt, 0)),
                  pl.BlockSpec((tm, LANES), lambda i: (i % nt, 0))],
        out_specs=pl.BlockSpec((tm, width), lambda i: (i, 0)),
        compiler_params=_cparams(1),
        name="mla_queries",
    )(z, q_norm.reshape(1, -1), w_uq_pad, cos, sin)


def _mla_kv_kernel(c_ref, pe_ref, nw_ref, w_ref, cos_ref, sin_ref, k_ref, v_ref):
    h = _rms(c_ref[...], nw_ref[...]).astype(BF16)
    acc = jnp.dot(h, w_ref[...], preferred_element_type=F32)
    pe = _rope(pe_ref[...], cos_ref[...], sin_ref[...]).astype(BF16)
    for hd in range(MLA_HEADS):
        src = hd * (MLA_NOPE + MLA_V)
        dst = hd * MLA_QK_PAD
        k_ref[:, dst:dst + MLA_NOPE] = acc[:, src:src + MLA_NOPE].astype(BF16)
        k_ref[:, dst + MLA_NOPE:dst + MLA_QK_PAD] = pe
        v_ref[:, hd * MLA_V:(hd + 1) * MLA_V] = acc[:, src + MLA_NOPE:src + MLA_NOPE + MLA_V].astype(BF16)


def mla_keys_values(z, kv_norm, w_ukv, cos, sin, t, tm=512):
    m = z.shape[0]
    tm = min(tm, t)
    nt = t // tm
    return pl.pallas_call(
        _mla_kv_kernel,
        out_shape=[jax.ShapeDtypeStruct((m, MLA_HEADS * MLA_QK_PAD), BF16),
                   jax.ShapeDtypeStruct((m, MLA_WIDTH), BF16)],
        grid=(m // tm,),
        in_specs=[pl.BlockSpec((tm, MLA_KV_RANK), lambda i: (i, (GATE_COLS + MLA_Q_RANK) // MLA_KV_RANK)),
                  pl.BlockSpec((tm, LANES), lambda i: (i, OD_A_BLK + (MLA_Q_RANK + MLA_KV_RANK) // LANES)),
                  pl.BlockSpec((1, MLA_KV_RANK), lambda i: (0, 0)),
                  pl.BlockSpec(w_ukv.shape, lambda i: (0, 0)),
                  pl.BlockSpec((tm, LANES), lambda i: (i % nt, 0)),
                  pl.BlockSpec((tm, LANES), lambda i: (i % nt, 0))],
        out_specs=[pl.BlockSpec((tm, MLA_HEADS * MLA_QK_PAD), lambda i: (i, 0)),
                   pl.BlockSpec((tm, MLA_WIDTH), lambda i: (i, 0))],
        compiler_params=_cparams(1),
        name="mla_keys_values",
    )(z, z, kv_norm.reshape(1, -1), w_ukv, cos, sin)


def _mla_attn_kernel(q_ref, k_ref, v_ref, kc_ref, vc_ref, o_ref, *, scale):
    parts = [(kc_ref[...], None, vc_ref[...])]
    nkeys = k_ref.shape[0]
    step = min(MLA_KEY_CHUNK, nkeys)
    for lo in range(0, nkeys, step):
        parts.append((k_ref[lo:lo + step, :], None, v_ref[lo:lo + step, :]))
    o_ref[...] = _softmax_pv(q_ref[...], parts, scale)


def mla_attention(q, k, v, kc, vc, batch, t, l, tq=2048):
    tq = min(tq, t)
    nq = t // tq
    return pl.pallas_call(
        functools.partial(_mla_attn_kernel, scale=(MLA_NOPE + MLA_ROPE) ** -0.5),
        out_shape=jax.ShapeDtypeStruct((batch * t, MLA_WIDTH), F32),
        grid=(batch, MLA_HEADS, nq),
        in_specs=[pl.BlockSpec((tq, MLA_QK_PAD), lambda b, h, i: (b * nq + i, h)),
                  pl.BlockSpec((t, MLA_QK_PAD), lambda b, h, i: (b, h)),
                  pl.BlockSpec((t, MLA_V), lambda b, h, i: (b, h)),
                  pl.BlockSpec((l, MLA_QK_PAD), lambda b, h, i: (b, h)),
                  pl.BlockSpec((l, MLA_V), lambda b, h, i: (b, h))],
        out_specs=pl.BlockSpec((tq, MLA_V), lambda b, h, i: (b * nq + i, h)),
        compiler_params=_cparams(3),
        name="mla_attention",
    )(q, k, v, kc, vc)


def _rope_tables(t):
    tok = np.arange(t)
    pos = np.stack([tok // GRID_W, tok % GRID_W], axis=-1).astype(np.float32)
    n_freq = MLA_ROPE // 4
    inv = (ROPE_BASE ** (-jnp.arange(n_freq, dtype=F32) / n_freq))
    ang = (jnp.asarray(pos)[:, :, None] * inv).reshape(t, MLA_ROPE // 2)
    cos = jnp.repeat(jnp.cos(ang), 2, axis=-1)
    sin = jnp.repeat(jnp.sin(ang), 2, axis=-1) * jnp.tile(jnp.array([-1.0, 1.0], F32), MLA_ROPE // 2)
    pad = ((0, 0), (0, LANES - MLA_ROPE))
    return jnp.pad(cos, pad), jnp.pad(sin, pad)


def _split3(x):
    hi = x.astype(BF16)
    r1 = x - hi.astype(F32)
    mid = r1.astype(BF16)
    lo = (r1 - mid.astype(F32)).astype(BF16)
    return hi, mid, lo


def _hg_kernel(q_ref, f_ref, v_ref, lb_ref, s0_ref, *rest, reverse, finalize, emit_state):
    st_ref = rest[-1]
    rest = rest[:-1]
    if finalize:
        prev_ref, nw_ref = rest[0], rest[1]
        rest = rest[2:]
    o_ref = rest[0]

    @pl.when(pl.program_id(2) == 0)
    def _():
        st_ref[...] = s0_ref[...]

    bt = q_ref.shape[0]
    nchunk = bt // HG_CHUNK
    order = range(nchunk - 1, -1, -1) if reverse else range(nchunk)
    ri = lax.broadcasted_iota(jnp.int32, (bt, bt), 0)
    ci = lax.broadcasted_iota(jnp.int32, (bt, bt), 1)
    same = (ri // HG_CHUNK) == (ci // HG_CHUNK)
    tri = same & ((ci >= ri) if reverse else (ci <= ri))
    tri_f = tri.astype(F32)
    tri_b = tri.astype(BF16)
    row_chunk = lax.broadcasted_iota(jnp.int32, (bt, HG_DK), 0) // HG_CHUNK
    in_chunk = [row_chunk == c for c in range(nchunk)]

    lb = lb_ref[...]
    fz = f_ref[...]
    log_sig = jnp.minimum(fz, 0.0) - jnp.log1p(jnp.exp(-jnp.abs(fz)))
    la = jnp.log(lb)
    lbb = jnp.log1p(-lb) + log_sig
    log_f = jnp.maximum(la, lbb) + jnp.log1p(jnp.exp(-jnp.abs(la - lbb)))
    kf_all = (1.0 - lb) * _sigmoid(-fz)
    pieces = jnp.dot(tri_b, jnp.concatenate(_split3(log_f), axis=1), preferred_element_type=F32)
    width = log_f.shape[1]
    cum_all = pieces[:, :width] + pieces[:, width:2 * width] + pieces[:, 2 * width:]

    for hh in range(HG_STEP_HEADS):
        sl = slice(hh * HG_DK, (hh + 1) * HG_DK)
        cum, kf, v = cum_all[:, sl], kf_all[:, sl], v_ref[:, sl]
        last = [cum[c * HG_CHUNK:c * HG_CHUNK + 1] if reverse else cum[(c + 1) * HG_CHUNK - 1:(c + 1) * HG_CHUNK]
                for c in range(nchunk)]
        tot = jnp.concatenate([jnp.broadcast_to(x, (HG_CHUNK, HG_DK)) for x in last], axis=0)
        q_in = q_ref[:, sl] * jnp.exp(cum)
        k_in = kf * jnp.exp(-cum)
        k_end = kf * jnp.exp(tot - cum)
        o_intra = _dot(_dot_nt(q_in, k_in) * tri_f, v)
        v_by_chunk = jnp.concatenate([jnp.where(in_chunk[c], v, 0.0) for c in range(nchunk)], axis=1)
        d_st = _dot_tn(v_by_chunk, k_end)
        st = st_ref[hh]
        before = [None] * nchunk
        for c in order:
            before[c] = st
            st = st * jnp.exp(last[c]) + d_st[c * HG_DV:(c + 1) * HG_DV]
        st_ref[hh] = st
        q_by_chunk = jnp.concatenate([jnp.where(in_chunk[c], q_in, 0.0) for c in range(nchunk)], axis=1)
        o = o_intra + _dot_nt(q_by_chunk, jnp.concatenate(before, axis=1))
        if finalize:
            o = o + prev_ref[:, sl]
            o = o * lax.rsqrt(jnp.mean(o * o, axis=-1, keepdims=True) + NORM_EPS) * nw_ref[...]
        o_ref[:, sl] = o
        if emit_state:
            rest[1][hh] = st


def hgrn2_pass(z, lb, s0, batch, t, f_off, reverse, prev=None, norm_w=None, emit_state=False):
    bt = min(HG_BLOCK, t)
    nblk = t // bt
    nh = HG_STEP_HEADS
    wide = nh * LANES
    finalize = prev is not None
    assert OD_HG_BLK % nh == 0 and HG_HEADS % nh == 0, "head groups must start on a whole column block"
    pos = (lambda i: nblk - 1 - i) if reverse else (lambda i: i)
    blk = lambda off: pl.BlockSpec((bt, wide), lambda b, h, i, off=off: (b * nblk + pos(i), off // nh + h))
    state = pl.BlockSpec((None, nh, HG_DV, HG_DK), lambda b, h, i: (b, h, 0, 0))
    ins = [z, z, z, lb.reshape(1, -1), s0]
    specs = [blk(OD_HG_BLK), blk(OD_HG_BLK + f_off), blk(OD_HG_BLK + 3 * HG_HEADS),
             pl.BlockSpec((1, wide), lambda b, h, i: (0, h)), state]
    if finalize:
        ins += [prev, norm_w.reshape(1, -1)]
        specs += [blk(0), pl.BlockSpec((1, HG_DV), lambda b, h, i: (0, 0))]
    out_shape = [jax.ShapeDtypeStruct((batch * t, HG_WIDTH), F32)]
    out_specs = [blk(0)]
    if emit_state:
        out_shape.append(jax.ShapeDtypeStruct((batch, HG_HEADS, HG_DV, HG_DK), F32))
        out_specs.append(state)
    return pl.pallas_call(
        functools.partial(_hg_kernel, reverse=reverse, finalize=finalize, emit_state=emit_state),
        out_shape=out_shape,
        grid=(batch, HG_HEADS // nh, nblk),
        in_specs=specs,
        out_specs=out_specs,
        scratch_shapes=[pltpu.VMEM((nh, HG_DV, HG_DK), F32)],
        compiler_params=_cparams(3),
        name="hgrn2_" + ("bwd" if reverse else "fwd"),
    )(*ins)


def _even_layer(x, xc, mod, norm_w, w_in, w_out, rpb, rw_params, batch, t, l):
    s2 = 3 * NA_WIDTH + RW_SHIFT_COLS
    width = RW_PAD_COLS - RW_SHIFT_COLS
    n_gate, n_rest = GATE_COLS // width, s2 // width
    plan = [n_rest + i for i in range(n_gate)] + list(range(n_rest)) + [None]
    w_ext = cast_reorder(w_in, plan, width, "even_w_in")
    tiles_per_seq = max(t // 1024, 1)
    z = in_projection(x, norm_w, mod, lambda i: i // tiles_per_seq, w_ext, "even_in_proj")
    z_c = in_projection(xc, norm_w, mod, lambda i: batch, w_ext, "even_in_proj_ctx")

    y_na = na_attention(z, z_c, rpb, batch, t, l)
    yc_na = ctx_attention(z_c, batch, l)

    mu, w0, w2, a0, a2, k_k, k_a, r_k, ln_w, ln_b = rw_params
    mu = jnp.pad(mu, ((0, 0), (0, RW_PAD_COLS - RW_SHIFT_COLS)))
    zpad = jnp.zeros((RW_RANK, RW_WIDTH), F32)
    w2p = jnp.stack([jnp.concatenate([w2[0], zpad]), jnp.concatenate([zpad, w2[1]])])
    a2p = jnp.stack([jnp.concatenate([a2[0], zpad]), jnp.concatenate([zpad, a2[1]])])
    vecs = (mu, w0, w2p, a0, a2p, k_k.reshape(1, -1), k_a.reshape(1, -1), r_k.reshape(1, -1))
    terms = rwkv_terms(z, batch, t, *vecs)
    terms_c = rwkv_terms(z_c, batch, l, *vecs)
    s_zero = jnp.zeros((RW_HEAD_DIM, RW_HEAD_DIM, LANES), F32)
    yc_f, yc_b, s_ctx = rwkv_scan(*terms_c[:5], s_zero)
    y_f, y_b, _ = rwkv_scan(*terms[:5], s_ctx)
    y_rw = rwkv_readout(y_f, y_b, terms[5], ln_w, ln_b).reshape(batch * t, RW_WIDTH)
    yc_rw = rwkv_readout(yc_f, yc_b, terms_c[5], ln_w, ln_b).reshape(batch * l, RW_WIDTH)

    w_out = w_out.astype(BF16)
    rows_per_seq = max(t // 256, 1)
    x = out_projection(y_na, y_rw, z, w_out, x, mod, lambda i: i // rows_per_seq, "even_out_proj")
    xc = out_projection(yc_na, yc_rw, z_c, w_out, xc, mod, lambda i: batch, "even_out_proj_ctx")
    return x, xc


def _odd_layer(x, xc, mod, norm_w, w_in, w_out, q_norm, w_uq, kv_norm, w_ukv, lb, hg_norm_w, final_w,
               batch, t, l):
    o3 = MLA_Q_RANK + MLA_KV_RANK + MLA_ROPE
    o4 = o3 + 4 * HG_FDIM
    zero = jnp.zeros((x.shape[1], MLA_A_COLS - o3), BF16)
    w_in = w_in.astype(BF16)
    w_ext = jnp.concatenate([w_in[:, o4:], w_in[:, :o3], zero, w_in[:, o3:o4]], axis=1)
    tiles_per_seq = max(t // 1024, 1)
    z = in_projection(x, norm_w, mod, lambda i: i // tiles_per_seq, w_ext, "odd_in_proj")
    z_c = in_projection(xc, norm_w, mod, lambda i: batch, w_ext, "odd_in_proj_ctx")

    cos, sin = _rope_tables(t)
    qk = MLA_NOPE + MLA_ROPE
    w_uq_pad = jnp.pad(w_uq.reshape(MLA_Q_RANK, MLA_HEADS, qk), ((0, 0), (0, 0), (0, MLA_QK_PAD - qk)))
    w_uq_pad = w_uq_pad.reshape(MLA_Q_RANK, MLA_HEADS * MLA_QK_PAD).astype(BF16)
    w_ukv = w_ukv.astype(BF16)
    q = mla_queries(z, q_norm, w_uq_pad, cos, sin, t)
    k, v = mla_keys_values(z, kv_norm, w_ukv, cos, sin, t)
    ones = jnp.ones((l, LANES), F32)
    kc, vc = mla_keys_values(z_c, kv_norm, w_ukv, ones, jnp.zeros_like(ones), l)
    y_mla = mla_attention(q, k, v, kc, vc, batch, t, l)

    s_zero = jnp.zeros((batch, HG_HEADS, HG_DV, HG_DK), F32)
    _, s_f = hgrn2_pass(z_c, lb, s_zero, batch, l, HG_HEADS, False, emit_state=True)
    _, s_b = hgrn2_pass(z_c, lb, s_zero, batch, l, 2 * HG_HEADS, True, emit_state=True)
    o_f, = hgrn2_pass(z, lb, s_f, batch, t, HG_HEADS, False)
    y_hg, = hgrn2_pass(z, lb, s_b, batch, t, 2 * HG_HEADS, True, prev=o_f, norm_w=hg_norm_w)

    rows_per_seq = max(t // 256, 1)
    return out_projection(y_mla, y_hg, z, w_out.astype(BF16), x, mod, lambda i: i // rows_per_seq,
                          "odd_out_proj", final_w=final_w)


def kernel(x, c, ctx, c_ctx, ada_w, ada_b, norm_w, e_w_in, e_w_out, na_rpb, rw_mu, rw_w0, rw_w2, rw_a0, rw_a2, rw_k_k, rw_k_a, rw_r_k, rw_ln_w, rw_ln_b, o_w_in, o_w_out, mla_q_norm, mla_w_uq, mla_kv_norm, mla_w_ukv, hg_lower_bounds, hg_norm_w, final_norm_w):
    batch, t, d = x.shape
    l = ctx.shape[1]
    assert ada_w.shape[0] == 2, "one even and one odd layer"
    cond = jnp.concatenate([c, c_ctx[None, :], jnp.zeros((8 - batch - 1, d), F32)], axis=0)
    mod = modulation(cond, ada_w, ada_b)
    s = jax.nn.softmax(hg_lower_bounds.astype(F32), axis=0)
    lower = jnp.cumsum(s, axis=0) - s[0]

    xf, xcf = x.reshape(batch * t, d), ctx.reshape(batch * l, d)
    rw_params = (rw_mu[0], rw_w0[0], rw_w2[0], rw_a0[0], rw_a2[0], rw_k_k[0], rw_k_a[0], rw_r_k[0],
                 rw_ln_w[0], rw_ln_b[0])
    xf, xcf = _even_layer(xf, xcf, mod[0].reshape(8, 1, 3 * d), norm_w[0], e_w_in[0], e_w_out[0], na_rpb[0],
                          rw_params, batch, t, l)
    out = _odd_layer(xf, xcf, mod[1].reshape(8, 1, 3 * d), norm_w[1], o_w_in[0], o_w_out[0], mla_q_norm[0],
                     mla_w_uq[0], mla_kv_norm[0], mla_w_ukv[0], lower[1], hg_norm_w[0], final_norm_w,
                     batch, t, l)
    return out.reshape(batch, t, d)
```

```python
import functools
import math

import numpy as np
import jax
import jax.numpy as jnp
from jax import lax
from jax.experimental import pallas as pl
from jax.experimental.pallas import tpu as pltpu

F32 = jnp.float32
BF16 = jnp.bfloat16

GRID_W = 64
NORM_EPS = 1e-6
ROPE_BASE = 10000.0

NA_HEADS = 8
NA_HEAD_DIM = 128
NA_WIDTH = NA_HEADS * NA_HEAD_DIM
NA_WIN_ROWS = 8
NA_WIN_COLS = 16
NA_QROWS = 4
NA_KROWS = 12
NA_STEP_HEADS = 4

RW_HEAD_DIM = 64
RW_HEADS = 16
RW_WIDTH = RW_HEADS * RW_HEAD_DIM
RW_RANK = 64
RW_SHIFT_COLS = 3 * RW_WIDTH + 4 * RW_RANK
RW_PAD_COLS = 3584
RW_GN_EPS = 64e-5
RW_SCAN_CHUNK = 64
RW_DECAY_SCALE = math.exp(-0.5)
RW_STATE_PAD = 0

MLA_HEADS = 8
MLA_Q_RANK = 512
MLA_KV_RANK = 512
MLA_NOPE = 128
MLA_ROPE = 64
MLA_V = 128
MLA_QK_PAD = 256
MLA_WIDTH = MLA_HEADS * MLA_V
MLA_A_COLS = 1536
MLA_KEY_CHUNK = 512

HG_HEADS = 8
HG_DK = 128
HG_DV = 128
HG_FDIM = HG_HEADS * HG_DK
HG_WIDTH = HG_HEADS * HG_DV
HG_CHUNK = 32
HG_BLOCK = 256
HG_STEP_HEADS = 4

LANES = 128

GATE_COLS = 2048
EV_Q_BLK = GATE_COLS // LANES
EV_RW_BLK = EV_Q_BLK + 3 * NA_WIDTH // LANES
OD_A_BLK = GATE_COLS // LANES
OD_HG_BLK = OD_A_BLK + MLA_A_COLS // LANES
VMEM_LIMIT = 48 * 1024 * 1024
NEG_INF = -1e30
LOG2E = math.log2(math.e)


def _cparams(n_axes):
    return pltpu.CompilerParams(dimension_semantics=("arbitrary",) * n_axes, vmem_limit_bytes=VMEM_LIMIT)


def _sigmoid(x):
    return 1.0 / (1.0 + jnp.exp(-x))


def _silu(x):
    return x * _sigmoid(x)


def _dot(a, b):
    return jnp.dot(a.astype(BF16), b.astype(BF16), preferred_element_type=F32)


def _dot_nt(a, b):
    return lax.dot_general(a.astype(BF16), b.astype(BF16), (((1,), (1,)), ((), ())), preferred_element_type=F32)


def _dot_tn(a, b):
    return lax.dot_general(a.astype(BF16), b.astype(BF16), (((0,), (0,)), ((), ())), preferred_element_type=F32)


def _mod_kernel(c_ref, w_ref, b_ref, o_ref):
    o_ref[...] = _dot(_silu(c_ref[...]), w_ref[...]) + b_ref[...]


def modulation(cond, ada_w, ada_b, tn=512):
    depth, d, n = ada_w.shape
    return pl.pallas_call(
        _mod_kernel,
        out_shape=jax.ShapeDtypeStruct((depth, 8, n), F32),
        grid=(depth, n // tn),
        in_specs=[pl.BlockSpec((8, d), lambda l, j: (0, 0)),
                  pl.BlockSpec((None, d, tn), lambda l, j: (l, 0, j)),
                  pl.BlockSpec((None, 1, tn), lambda l, j: (l, 0, j))],
        out_specs=pl.BlockSpec((None, 8, tn), lambda l, j: (l, 0, j)),
        compiler_params=_cparams(2),
        name="adaln_modulation",
    )(cond, ada_w, ada_b.reshape(depth, 1, n))


def _cast_reorder_kernel(src_ref, zero_ref, x_ref, o_ref):
    del src_ref
    o_ref[...] = jnp.where(zero_ref[pl.program_id(0)] != 0, 0.0, x_ref[...]).astype(BF16)


def cast_reorder(w, plan, width, name):
    d = w.shape[0]
    src = np.array([0 if a is None else a for a in plan], np.int32)
    zero = np.array([a is None for a in plan], np.int32)
    return pl.pallas_call(
        _cast_reorder_kernel,
        out_shape=jax.ShapeDtypeStruct((d, width * len(plan)), BF16),
        grid_spec=pltpu.PrefetchScalarGridSpec(
            num_scalar_prefetch=2,
            grid=(len(plan),),
            in_specs=[pl.BlockSpec((d, width), lambda j, src, zero: (0, src[j]))],
            out_specs=pl.BlockSpec((d, width), lambda j, src, zero: (0, j))),
        compiler_params=_cparams(1),
        name=name,
    )(jnp.asarray(src), jnp.asarray(zero), w)


def _inproj_kernel(x_ref, nw_ref, shift_ref, scale_ref, w_ref, o_ref, h_ref):
    @pl.when(pl.program_id(1) == 0)
    def _():
        x = x_ref[...]
        y = x * lax.rsqrt(jnp.mean(x * x, axis=-1, keepdims=True) + NORM_EPS) * nw_ref[...]
        h_ref[...] = (y * (1.0 + scale_ref[...]) + shift_ref[...]).astype(BF16)

    o_ref[...] = jnp.dot(h_ref[...], w_ref[...], preferred_element_type=F32)


def in_projection(x, norm_w, mod, mod_row, w, name, tm=1024, tn=512):
    m, d = x.shape
    n = w.shape[1]
    tm = min(tm, m)
    return pl.pallas_call(
        _inproj_kernel,
        out_shape=jax.ShapeDtypeStruct((m, n), F32),
        grid=(m // tm, n // tn),
        in_specs=[pl.BlockSpec((tm, d), lambda i, j: (i, 0)),
                  pl.BlockSpec((1, d), lambda i, j: (0, 0)),
                  pl.BlockSpec((None, 1, d), lambda i, j: (mod_row(i), 0, 0)),
                  pl.BlockSpec((None, 1, d), lambda i, j: (mod_row(i), 0, 1)),
                  pl.BlockSpec((d, tn), lambda i, j: (0, j))],
        out_specs=pl.BlockSpec((tm, tn), lambda i, j: (i, j)),
        scratch_shapes=[pltpu.VMEM((tm, d), BF16)],
        compiler_params=_cparams(2),
        name=name,
    )(x, norm_w.reshape(1, d), mod, mod, w)


def _outproj_kernel(ya_ref, yb_ref, g_ref, w_ref, x_ref, gm_ref, *rest, ka, final):
    o_ref = rest[-1]
    sg = _silu(g_ref[...])
    acc = _dot(ya_ref[...].astype(F32) * sg[:, :ka], w_ref[:ka, :])
    acc += _dot(yb_ref[...].astype(F32) * sg[:, ka:], w_ref[ka:, :])
    out = x_ref[...] + gm_ref[...] * acc
    if final:
        out = out * lax.rsqrt(jnp.mean(out * out, axis=-1, keepdims=True) + NORM_EPS) * rest[0][...]
    o_ref[...] = out


def out_projection(ya, yb, gate, w, x, mod, mod_row, name, final_w=None, tm=256):
    m, d = x.shape
    ka, kb = ya.shape[1], yb.shape[1]
    tm = min(tm, m)
    ins = [ya, yb, gate, w, x, mod]
    specs = [pl.BlockSpec((tm, ka), lambda i: (i, 0)),
             pl.BlockSpec((tm, kb), lambda i: (i, 0)),
             pl.BlockSpec((tm, ka + kb), lambda i: (i, 0)),
             pl.BlockSpec((ka + kb, d), lambda i: (0, 0)),
             pl.BlockSpec((tm, d), lambda i: (i, 0)),
             pl.BlockSpec((None, 1, d), lambda i: (mod_row(i), 0, 2))]
    if final_w is not None:
        ins.append(final_w.reshape(1, d))
        specs.append(pl.BlockSpec((1, d), lambda i: (0, 0)))
    return pl.pallas_call(
        functools.partial(_outproj_kernel, ka=ka, final=final_w is not None),
        out_shape=jax.ShapeDtypeStruct((m, d), F32),
        grid=(m // tm,),
        in_specs=specs,
        out_specs=pl.BlockSpec((tm, d), lambda i: (i, 0)),
        compiler_params=_cparams(1),
        name=name,
    )(*ins)


def _softmax_pv(q, parts, scale):
    c = scale * LOG2E
    m = denom = num = None
    for k, bias, v in parts:
        x = _dot_nt(q, k) * c
        if bias is not None:
            x = x + bias
        m_part = jnp.max(x, axis=-1, keepdims=True)
        if m is None:
            m = m_part
            p = jnp.exp2(x - m)
            denom = jnp.sum(p, axis=-1, keepdims=True)
            num = _dot(p, v)
        else:
            m_new = jnp.maximum(m, m_part)
            alpha = jnp.exp2(m - m_new)
            p = jnp.exp2(x - m_new)
            denom = denom * alpha + jnp.sum(p, axis=-1, keepdims=True)
            num = num * alpha + _dot(p, v)
            m = m_new
    return num / denom


def _softmax_pv_joint(q, parts, scale):
    c = scale * LOG2E
    logits = [_dot_nt(q, k) * c if bias is None else _dot_nt(q, k) * c + bias for k, bias, _ in parts]
    m = functools.reduce(jnp.maximum, [jnp.max(x, axis=-1, keepdims=True) for x in logits])
    ps = [jnp.exp2(x - m) for x in logits]
    denom = functools.reduce(lambda a, b: a + b, [jnp.sum(p, axis=-1, keepdims=True) for p in ps])
    num = functools.reduce(lambda a, b: a + b, [_dot(p, v) for p, (_, _, v) in zip(ps, parts)])
    return num / denom


def _na_kernel(idx_ref, q_ref, k_ref, v_ref, kc_ref, vc_ref, tab_ref, o_ref, bias_ref, *, rows, scale):
    j = pl.program_id(2)
    npair = NA_KROWS // 2
    start = pl.multiple_of(jnp.clip(NA_QROWS * j - NA_WIN_ROWS // 2, 0, rows - NA_KROWS) * GRID_W, GRID_W)
    win = pl.ds(start, NA_KROWS * GRID_W)
    for hd in range(NA_STEP_HEADS):
        sl = slice(hd * LANES, (hd + 1) * LANES)
        for qr in range(NA_QROWS):
            for m in range(npair):
                tile = tab_ref[hd, idx_ref[(j * NA_QROWS + qr) * npair + m]]
                bias_ref[hd, qr * GRID_W:(qr + 1) * GRID_W, m * LANES:(m + 1) * LANES] = tile
        parts = [(k_ref[win, sl], bias_ref[hd], v_ref[win, sl]), (kc_ref[:, sl], None, vc_ref[:, sl])]
        o_ref[:, sl] = _softmax_pv_joint(q_ref[:, sl], parts, scale)


def _na_tile_ids(rows):
    kh = min(NA_WIN_ROWS, rows)
    n = 2 * NA_WIN_ROWS - 1
    ids = []
    for j in range(rows // NA_QROWS):
        ks = int(np.clip(NA_QROWS * j - NA_WIN_ROWS // 2, 0, rows - NA_KROWS))
        for qr in range(NA_QROWS):
            r = NA_QROWS * j + qr
            r0 = int(np.clip(r - kh // 2, 0, rows - kh))
            for m in range(NA_KROWS // 2):
                kr = ks + 2 * m
                dy = kr - r + NA_WIN_ROWS - 1
                first, second = r0 <= kr < r0 + kh, r0 <= kr + 1 < r0 + kh
                ids.append(dy if first and second else n + dy if first else 2 * n + dy + 1 if second else 3 * n)
    return np.array(ids, np.int32)


def _na_bias_tiles(rpb):
    col = np.arange(GRID_W)
    cs = np.clip(col - NA_WIN_COLS // 2, 0, GRID_W - NA_WIN_COLS)
    kc = col[None, :]
    in_win = (kc >= cs[:, None]) & (kc < cs[:, None] + NA_WIN_COLS)
    dx = np.where(in_win, kc - col[:, None] + NA_WIN_COLS - 1, 0)
    tab = jnp.where(in_win[None, None], rpb[:, :, dx] * LOG2E, NEG_INF)
    neg = jnp.full_like(tab[:, :1], NEG_INF)
    nxt = jnp.concatenate([tab[:, 1:], neg], axis=1)
    both = jnp.concatenate([tab, nxt], axis=-1)
    first = jnp.concatenate([tab, jnp.broadcast_to(neg, tab.shape)], axis=-1)
    second = jnp.concatenate([jnp.broadcast_to(neg, tab.shape), tab], axis=-1)
    return jnp.concatenate([both, first, second, jnp.concatenate([neg, neg], axis=-1)], axis=1)


def na_attention(z, z_c, rpb, batch, t, l):
    rows = t // GRID_W
    nj = rows // NA_QROWS
    tq = NA_QROWS * GRID_W
    ids = jnp.asarray(_na_tile_ids(rows))
    tiles = _na_bias_tiles(rpb)
    nh = NA_STEP_HEADS
    wide = nh * LANES
    q0, k0, v0 = (EV_Q_BLK + i * NA_HEADS for i in range(3))
    assert q0 % nh == 0 and NA_HEADS % nh == 0, "head groups must start on a whole column block"
    return pl.pallas_call(
        functools.partial(_na_kernel, rows=rows, scale=NA_HEAD_DIM ** -0.5),
        out_shape=jax.ShapeDtypeStruct((batch * t, NA_WIDTH), F32),
        grid_spec=pltpu.PrefetchScalarGridSpec(
            num_scalar_prefetch=1,
            grid=(NA_HEADS // nh, batch, nj),
            in_specs=[pl.BlockSpec((tq, wide), lambda h, b, j, g: (b * nj + j, q0 // nh + h)),
                      pl.BlockSpec((t, wide), lambda h, b, j, g: (b, k0 // nh + h)),
                      pl.BlockSpec((t, wide), lambda h, b, j, g: (b, v0 // nh + h)),
                      pl.BlockSpec((l, wide), lambda h, b, j, g: (b, k0 // nh + h)),
                      pl.BlockSpec((l, wide), lambda h, b, j, g: (b, v0 // nh + h)),
                      pl.BlockSpec((nh,) + tiles.shape[1:], lambda h, b, j, g: (h, 0, 0, 0))],
            out_specs=pl.BlockSpec((tq, wide), lambda h, b, j, g: (b * nj + j, h)),
            scratch_shapes=[pltpu.VMEM((nh, tq, NA_KROWS * GRID_W), F32)]),
        compiler_params=_cparams(3),
        name="na_attention",
    )(ids, z, z, z, z_c, z_c, tiles)


def _dense_attn_kernel(q_ref, k_ref, v_ref, o_ref, *, scale):
    o_ref[...] = _softmax_pv(q_ref[...], [(k_ref[...], None, v_ref[...])], scale)


def ctx_attention(z_c, batch, l):
    h8 = NA_HEADS
    return pl.pallas_call(
        functools.partial(_dense_attn_kernel, scale=NA_HEAD_DIM ** -0.5),
        out_shape=jax.ShapeDtypeStruct((batch * l, NA_WIDTH), F32),
        grid=(batch, h8),
        in_specs=[pl.BlockSpec((l, LANES), lambda b, h: (b, EV_Q_BLK + h)),
                  pl.BlockSpec((l, LANES), lambda b, h: (b, EV_Q_BLK + h8 + h)),
                  pl.BlockSpec((l, LANES), lambda b, h: (b, EV_Q_BLK + 2 * h8 + h))],
        out_specs=pl.BlockSpec((l, LANES), lambda b, h: (b, h)),
        compiler_params=_cparams(2),
        name="ctx_attention",
    )(z_c, z_c, z_c)


def _head_sum(x):
    first = lax.broadcasted_iota(jnp.int32, x.shape, 1) < RW_HEAD_DIM
    s0 = jnp.sum(jnp.where(first, x, 0.0), axis=-1, keepdims=True)
    s1 = jnp.sum(jnp.where(first, 0.0, x), axis=-1, keepdims=True)
    return jnp.where(first, s0, s1)


def _rw_terms_kernel(ur_ref, uk_ref, uv_ref, uwa_ref, pr_ref, pk_ref, pv_ref, pwa_ref, nr_ref, nk_ref, nv_ref,
                     nwa_ref, mu_ref, w0_ref, w2_ref, a0_ref, a2_ref, kk_ref, ka_ref, rk_ref,
                     rv_o, wk0_o, wk1_o, bk0_o, bk1_o, bonus_o, *, nt):
    i = pl.program_id(1)
    tm = ur_ref.shape[0]

    def shifted(u_ref, p_ref, n_ref, col0):
        u = u_ref[...]
        mu = mu_ref[:, col0:col0 + u.shape[1]]
        row = lax.broadcasted_iota(jnp.int32, u.shape, 0)
        before = jnp.where(i == 0, 0.0, p_ref[7:8, :])
        after = jnp.where(i == nt - 1, 0.0, n_ref[0:1, :])
        prev = jnp.where(row == 0, before, pltpu.roll(u, 1, axis=0))
        nxt = jnp.where(row == tm - 1, after, pltpu.roll(u, tm - 1, axis=0))
        return u + mu[0:1] * (prev - u) + mu[1:2] * (nxt - u)

    r_all = shifted(ur_ref, pr_ref, nr_ref, 0)
    k_all = shifted(uk_ref, pk_ref, nk_ref, RW_WIDTH)
    v_all = shifted(uv_ref, pv_ref, nv_ref, 2 * RW_WIDTH)
    wa = shifted(uwa_ref, pwa_ref, nwa_ref, 3 * RW_WIDTH)
    wd = jnp.tanh(wa[:, :LANES]).astype(BF16)
    ad = wa[:, LANES:].astype(BF16)
    first = lax.broadcasted_iota(jnp.int32, (tm, LANES), 1) < RW_HEAD_DIM

    def head_rows(x, y):
        return (jnp.where(first, x, pltpu.roll(y, RW_HEAD_DIM, axis=1)),
                jnp.where(first, pltpu.roll(x, RW_HEAD_DIM, axis=1), y))

    outs = (rv_o, wk0_o, wk1_o, bk0_o, bk1_o)
    even = [[] for _ in outs]
    odd = [[] for _ in outs]
    bonus = []
    for p in range(RW_WIDTH // LANES):
        sl = slice(p * LANES, (p + 1) * LANES)
        r, k, v = r_all[:, sl], k_all[:, sl], v_all[:, sl]
        kk = k * kk_ref[:, sl]
        kk = kk / jnp.maximum(jnp.sqrt(_head_sum(kk * kk)), 1e-12)
        kd_sum = jnp.zeros_like(k)
        per_dir = []
        for d in range(2):
            w = w0_ref[d:d + 1, sl] + jnp.dot(wd, w2_ref[d, :, sl].astype(BF16), preferred_element_type=F32)
            a = _sigmoid(a0_ref[d:d + 1, sl] + jnp.dot(ad, a2_ref[d, :, sl].astype(BF16), preferred_element_type=F32))
            kd = k * (1.0 + (a - 1.0) * ka_ref[:, sl])
            dec = jnp.exp(-RW_DECAY_SCALE * _sigmoid(w))
            per_dir.append(((dec, kd), (kk * a, kk)))
            kd_sum = kd_sum + kd
        pairs = [(r, v), per_dir[0][0], per_dir[1][0], per_dir[0][1], per_dir[1][1]]
        bonus.append(_head_sum(r * kd_sum * rk_ref[:, sl]) * v)
        for ev, od, (x, y) in zip(even, odd, pairs):
            e, o = head_rows(x, y)
            ev.append(e)
            od.append(o)
    for o_ref, ev, od in zip(outs, even, odd):
        o_ref[...] = jnp.swapaxes(jnp.stack(ev + od, axis=0), 0, 1)
    bonus_o[...] = jnp.swapaxes(jnp.stack(bonus, axis=0), 0, 1)


def rwkv_terms(z, n_seq, t, mu, w0, w2p, a0, a2p, k_k, k_a, r_k, tm=256):
    tm = min(tm, t)
    nt = t // tm
    npair = RW_WIDTH // LANES
    wide0 = EV_RW_BLK * LANES // RW_WIDTH
    wa_blk = (EV_RW_BLK + 3 * npair) // 2
    n8 = tm // 8

    def cur(width, cb):
        return pl.BlockSpec((tm, width), lambda b, i: (b * nt + i, cb))

    def before(width, cb):
        return pl.BlockSpec((8, width), lambda b, i: (jnp.maximum((b * nt + i) * n8 - 1, 0), cb))

    def after(width, cb):
        return pl.BlockSpec((8, width), lambda b, i: (jnp.minimum((b * nt + i + 1) * n8, n_seq * nt * n8 - 1), cb))

    cols = [(RW_WIDTH, wide0), (RW_WIDTH, wide0 + 1), (RW_WIDTH, wide0 + 2), (2 * LANES, wa_blk)]
    whole = lambda a: pl.BlockSpec(a.shape, lambda b, i: (0,) * a.ndim)
    params = (mu, w0, w2p, a0, a2p, k_k, k_a, r_k)
    out = pl.BlockSpec((tm, npair, LANES), lambda b, i: (i, b, 0))
    out2 = pl.BlockSpec((tm, 2 * npair, LANES), lambda b, i: (i, b, 0))
    return pl.pallas_call(
        functools.partial(_rw_terms_kernel, nt=nt),
        out_shape=[jax.ShapeDtypeStruct((t, n_seq * 2 * npair, LANES), F32)] * 5
        + [jax.ShapeDtypeStruct((t, n_seq * npair, LANES), F32)],
        grid=(n_seq, nt),
        in_specs=([cur(*c) for c in cols] + [before(*c) for c in cols] + [after(*c) for c in cols]
                  + [whole(a) for a in params]),
        out_specs=[out2] * 5 + [out],
        compiler_params=_cparams(2),
        name="rwkv_terms",
    )(*([z] * 12), *params)


def _to_lanes(n):
    q = jnp.concatenate([n, pltpu.roll(n, RW_HEAD_DIM, axis=1)], axis=0)
    return q.T[:RW_HEAD_DIM]


def _rw_scan_kernel(rvf, rvb, wk0, wk1, bk0, bk1, s0_ref, yf_ref, yb_ref, sfin_ref,
                    s_ref, z_ref, znext_ref, g_ref, ybuf_ref, *, chunk):
    n = RW_HEAD_DIM
    nb = rvf.shape[1] // (2 * RW_WIDTH // LANES)
    R, K, V, A, B = range(5)

    @pl.when(pl.program_id(0) == 0)
    def _():
        s_ref[:, :n, :] = s0_ref[...]

    def convert(t_src, slot):
        def pair_tiles(f, bw):
            both = (f[t_src], bw[chunk - 1 - t_src])
            x = jnp.concatenate([src[(b * 2 + par) * 8:(b * 2 + par) * 8 + 8]
                                 for par in range(2) for src in both for b in range(nb)], axis=0)
            xt = x.T
            return xt[:n], xt[n:]

        r, v = pair_tiles(rvf, rvb)
        w, k = pair_tiles(wk0, wk1)
        b, kk = pair_tiles(bk0, bk1)
        g_prev = g_ref[0]
        g = g_prev * w
        g_ref[0] = g
        inv = 1.0 / g
        znext_ref[slot, R] = r * g
        znext_ref[slot, K] = k * inv
        znext_ref[slot, V] = v
        znext_ref[slot, A] = -(kk * g_prev)
        znext_ref[slot, B] = b * inv

    def emit_y(t, slot):
        rows = _to_lanes(ybuf_ref[slot])
        yf_ref[t] = rows[:n // 2]
        yb_ref[chunk - 1 - t] = rows[n // 2:]

    def colsum(x):
        return jnp.sum(x, axis=0, keepdims=True)

    g_ref[0] = jnp.ones((n, LANES), F32)
    ybuf_ref[...] = jnp.zeros((2, n, LANES), F32)
    convert(0, 0)
    convert(1, 1)

    def two_steps(u, carry):
        t = 2 * u
        emit_y(jnp.maximum(t - 2, 0), 0)
        emit_y(jnp.maximum(t - 1, 1), 1)
        z_ref[...] = znext_ref[...]
        g_ref[1] = g_ref[0]
        convert(jnp.minimum(t + 2, chunk - 1), 0)
        convert(jnp.minimum(t + 3, chunk - 1), 1)
        acc = [jnp.zeros((n, LANES), F32) for _ in range(4)]
        for k in range(n):
            hk = s_ref[k, :n, :]
            for i, (slot, vec) in enumerate(((0, A), (1, A), (0, R), (1, R))):
                acc[i] = acc[i] + hk * z_ref[slot, vec, k:k + 1, :]
        r1, k1, v1, b1 = (z_ref[0, i] for i in (R, K, V, B))
        r2, k2, v2, a2, b2 = (z_ref[1, i] for i in (R, K, V, A, B))
        sa1 = acc[0]
        sa2 = acc[1] + sa1 * colsum(b1 * a2) + v1 * colsum(k1 * a2)
        ybuf_ref[0] = acc[2] + sa1 * colsum(b1 * r1) + v1 * colsum(k1 * r1)
        ybuf_ref[1] = (acc[3] + sa1 * colsum(b1 * r2) + v1 * colsum(k1 * r2)
                       + sa2 * colsum(b2 * r2) + v2 * colsum(k2 * r2))
        for k in range(n):
            row = slice(k, k + 1)
            s_ref[k, :n, :] = (s_ref[k, :n, :] + sa1 * z_ref[0, B, row, :] + v1 * z_ref[0, K, row, :]
                               + sa2 * z_ref[1, B, row, :] + v2 * z_ref[1, K, row, :])
        return carry

    assert chunk % 2 == 0
    lax.fori_loop(0, chunk // 2, two_steps, 0)
    emit_y(chunk - 2, 0)
    emit_y(chunk - 1, 1)
    for k in range(n):
        s_ref[k, :n, :] = s_ref[k, :n, :] * g_ref[1, k:k + 1, :]
    sfin_ref[...] = s_ref[:, :n, :]


def rwkv_scan(rv, wk0, wk1, bk0, bk1, s0):
    t, nrow, _ = rv.shape
    n = RW_HEAD_DIM
    chunk = RW_SCAN_CHUNK
    nc = t // chunk
    assert nrow == LANES // 2, "the states of both directions fill the 128 lanes"
    fwd = pl.BlockSpec((chunk, nrow, LANES), lambda i: (i, 0, 0))
    bwd = pl.BlockSpec((chunk, nrow, LANES), lambda i: (nc - 1 - i, 0, 0))
    yfwd = pl.BlockSpec((chunk, nrow // 2, LANES), lambda i: (i, 0, 0))
    ybwd = pl.BlockSpec((chunk, nrow // 2, LANES), lambda i: (nc - 1 - i, 0, 0))
    state = pl.BlockSpec((n, n, LANES), lambda i: (0, 0, 0))
    slab = jax.ShapeDtypeStruct((t, nrow // 2, LANES), F32)
    return pl.pallas_call(
        functools.partial(_rw_scan_kernel, chunk=chunk),
        out_shape=[slab, slab, jax.ShapeDtypeStruct((n, n, LANES), F32)],
        grid=(nc,),
        in_specs=[fwd, bwd, fwd, bwd, fwd, bwd, state],
        out_specs=[yfwd, ybwd, state],
        scratch_shapes=[pltpu.VMEM((n, n + RW_STATE_PAD, LANES), F32), pltpu.VMEM((2, 5, n, LANES), F32),
                        pltpu.VMEM((2, 5, n, LANES), F32), pltpu.VMEM((2, n, LANES), F32),
                        pltpu.VMEM((2, n, LANES), F32)],
        compiler_params=_cparams(1),
        name="rwkv_scan",
    )(rv, rv, wk0, wk1, bk0, bk1, s0)


def _rw_readout_kernel(yf_ref, yb_ref, bonus_ref, lw_ref, lb_ref, o_ref):
    npair = RW_WIDTH // LANES
    y_all = jnp.swapaxes(yf_ref[...] + yb_ref[...], 0, 1)
    bonus_all = jnp.swapaxes(bonus_ref[...], 0, 1)
    for q in range(yf_ref.shape[1]):
        b, p = divmod(q, npair)
        sl = slice(p * LANES, (p + 1) * LANES)
        y = y_all[q]
        mu = _head_sum(y) * (1.0 / RW_HEAD_DIM)
        d = y - mu
        var = _head_sum(d * d) * (1.0 / RW_HEAD_DIM)
        o_ref[b, :, sl] = d * lax.rsqrt(var + RW_GN_EPS) * lw_ref[:, sl] + lb_ref[:, sl] + bonus_all[q]


def rwkv_readout(y_f, y_b, bonus, ln_w, ln_b, tm=256):
    t, nq, _ = y_f.shape
    tm = min(tm, t)
    batch = nq // (RW_WIDTH // LANES)
    slab = pl.BlockSpec((tm, nq, LANES), lambda i: (i, 0, 0))
    vec = pl.BlockSpec((1, RW_WIDTH), lambda i: (0, 0))
    return pl.pallas_call(
        _rw_readout_kernel,
        out_shape=jax.ShapeDtypeStruct((batch, t, RW_WIDTH), F32),
        grid=(t // tm,),
        in_specs=[slab, slab, slab, vec, vec],
        out_specs=pl.BlockSpec((batch, tm, RW_WIDTH), lambda i: (0, i, 0)),
        compiler_params=_cparams(1),
        name="rwkv_readout",
    )(y_f, y_b, bonus, ln_w.reshape(1, -1), ln_b.reshape(1, -1))


def _rms(x, w):
    return x * lax.rsqrt(jnp.mean(x * x, axis=-1, keepdims=True) + NORM_EPS) * w


def _rope(x, cos, sin):
    odd = (lax.broadcasted_iota(jnp.int32, x.shape, 1) & 1) == 1
    swapped = jnp.where(odd, pltpu.roll(x, 1, axis=1), pltpu.roll(x, LANES - 1, axis=1))
    return x * cos + swapped * sin


def _mla_q_kernel(c_ref, nw_ref, w_ref, cos_ref, sin_ref, q_ref):
    h = _rms(c_ref[...], nw_ref[...]).astype(BF16)
    acc = jnp.dot(h, w_ref[...], preferred_element_type=F32)
    cos, sin = cos_ref[...], sin_ref[...]
    for hd in range(MLA_HEADS):
        lo = hd * MLA_QK_PAD
        q_ref[:, lo:lo + MLA_NOPE] = acc[:, lo:lo + MLA_NOPE].astype(BF16)
        q_ref[:, lo + MLA_NOPE:lo + MLA_QK_PAD] = _rope(acc[:, lo + MLA_NOPE:lo + MLA_QK_PAD], cos, sin).astype(BF16)


def mla_queries(z, q_norm, w_uq_pad, cos, sin, t, tm=512):
    m = z.shape[0]
    tm = min(tm, t)
    nt = t // tm
    width = MLA_HEADS * MLA_QK_PAD
    return pl.pallas_call(
        _mla_q_kernel,
        out_shape=jax.ShapeDtypeStruct((m, width), BF16),
        grid=(m // tm,),
        in_specs=[pl.BlockSpec((tm, MLA_Q_RANK), lambda i: (i, GATE_COLS // MLA_Q_RANK)),
                  pl.BlockSpec((1, MLA_Q_RANK), lambda i: (0, 0)),
                  pl.BlockSpec((MLA_Q_RANK, width), lambda i: (0, 0)),
                  pl.BlockSpec((tm, LANES), lambda i: (i % nt, 0)),
                  pl.BlockSpec((tm, LANES), lambda i: (i % nt, 0))],
        out_specs=pl.BlockSpec((tm, width), lambda i: (i, 0)),
        compiler_params=_cparams(1),
        name="mla_queries",
    )(z, q_norm.reshape(1, -1), w_uq_pad, cos, sin)


def _mla_kv_kernel(c_ref, pe_ref, nw_ref, w_ref, cos_ref, sin_ref, k_ref, v_ref):
    h = _rms(c_ref[...], nw_ref[...]).astype(BF16)
    acc = jnp.dot(h, w_ref[...], preferred_element_type=F32)
    pe = _rope(pe_ref[...], cos_ref[...], sin_ref[...]).astype(BF16)
    for hd in range(MLA_HEADS):
        src = hd * (MLA_NOPE + MLA_V)
        dst = hd * MLA_QK_PAD
        k_ref[:, dst:dst + MLA_NOPE] = acc[:, src:src + MLA_NOPE].astype(BF16)
        k_ref[:, dst + MLA_NOPE:dst + MLA_QK_PAD] = pe
        v_ref[:, hd * MLA_V:(hd + 1) * MLA_V] = acc[:, src + MLA_NOPE:src + MLA_NOPE + MLA_V].astype(BF16)


def mla_keys_values(z, kv_norm, w_ukv, cos, sin, t, tm=512):
    m = z.shape[0]
    tm = min(tm, t)
    nt = t // tm
    return pl.pallas_call(
        _mla_kv_kernel,
        out_shape=[jax.ShapeDtypeStruct((m, MLA_HEADS * MLA_QK_PAD), BF16),
                   jax.ShapeDtypeStruct((m, MLA_WIDTH), BF16)],
        grid=(m // tm,),
        in_specs=[pl.BlockSpec((tm, MLA_KV_RANK), lambda i: (i, (GATE_COLS + MLA_Q_RANK) // MLA_KV_RANK)),
                  pl.BlockSpec((tm, LANES), lambda i: (i, OD_A_BLK + (MLA_Q_RANK + MLA_KV_RANK) // LANES)),
                  pl.BlockSpec((1, MLA_KV_RANK), lambda i: (0, 0)),
                  pl.BlockSpec(w_ukv.shape, lambda i: (0, 0)),
                  pl.BlockSpec((tm, LANES), lambda i: (i % nt, 0)),
                  pl.BlockSpec((tm, LANES), lambda i: (i % nt, 0))],
        out_specs=[pl.BlockSpec((tm, MLA_HEADS * MLA_QK_PAD), lambda i: (i, 0)),
                   pl.BlockSpec((tm, MLA_WIDTH), lambda i: (i, 0))],
        compiler_params=_cparams(1),
        name="mla_keys_values",
    )(z, z, kv_norm.reshape(1, -1), w_ukv, cos, sin)


def _mla_attn_kernel(q_ref, k_ref, v_ref, kc_ref, vc_ref, o_ref, *, scale):
    parts = [(kc_ref[...], None, vc_ref[...])]
    nkeys = k_ref.shape[0]
    step = min(MLA_KEY_CHUNK, nkeys)
    for lo in range(0, nkeys, step):
        parts.append((k_ref[lo:lo + step, :], None, v_ref[lo:lo + step, :]))
    o_ref[...] = _softmax_pv(q_ref[...], parts, scale)


def mla_attention(q, k, v, kc, vc, batch, t, l, tq=2048):
    tq = min(tq, t)
    nq = t // tq
    return pl.pallas_call(
        functools.partial(_mla_attn_kernel, scale=(MLA_NOPE + MLA_ROPE) ** -0.5),
        out_shape=jax.ShapeDtypeStruct((batch * t, MLA_WIDTH), F32),
        grid=(batch, MLA_HEADS, nq),
        in_specs=[pl.BlockSpec((tq, MLA_QK_PAD), lambda b, h, i: (b * nq + i, h)),
                  pl.BlockSpec((t, MLA_QK_PAD), lambda b, h, i: (b, h)),
                  pl.BlockSpec((t, MLA_V), lambda b, h, i: (b, h)),
                  pl.BlockSpec((l, MLA_QK_PAD), lambda b, h, i: (b, h)),
                  pl.BlockSpec((l, MLA_V), lambda b, h, i: (b, h))],
        out_specs=pl.BlockSpec((tq, MLA_V), lambda b, h, i: (b * nq + i, h)),
        compiler_params=_cparams(3),
        name="mla_attention",
    )(q, k, v, kc, vc)


def _rope_tables(t):
    tok = np.arange(t)
    pos = np.stack([tok // GRID_W, tok % GRID_W], axis=-1).astype(np.float32)
    n_freq = MLA_ROPE // 4
    inv = (ROPE_BASE ** (-jnp.arange(n_freq, dtype=F32) / n_freq))
    ang = (jnp.asarray(pos)[:, :, None] * inv).reshape(t, MLA_ROPE // 2)
    cos = jnp.repeat(jnp.cos(ang), 2, axis=-1)
    sin = jnp.repeat(jnp.sin(ang), 2, axis=-1) * jnp.tile(jnp.array([-1.0, 1.0], F32), MLA_ROPE // 2)
    pad = ((0, 0), (0, LANES - MLA_ROPE))
    return jnp.pad(cos, pad), jnp.pad(sin, pad)


def _split3(x):
    hi = x.astype(BF16)
    r1 = x - hi.astype(F32)
    mid = r1.astype(BF16)
    lo = (r1 - mid.astype(F32)).astype(BF16)
    return hi, mid, lo


def _hg_kernel(q_ref, f_ref, v_ref, lb_ref, s0_ref, *rest, reverse, finalize, emit_state):
    st_ref = rest[-1]
    rest = rest[:-1]
    if finalize:
        prev_ref, nw_ref = rest[0], rest[1]
        rest = rest[2:]
    o_ref = rest[0]

    @pl.when(pl.program_id(2) == 0)
    def _():
        st_ref[...] = s0_ref[...]

    bt = q_ref.shape[0]
    nchunk = bt // HG_CHUNK
    order = range(nchunk - 1, -1, -1) if reverse else range(nchunk)
    ri = lax.broadcasted_iota(jnp.int32, (bt, bt), 0)
    ci = lax.broadcasted_iota(jnp.int32, (bt, bt), 1)
    same = (ri // HG_CHUNK) == (ci // HG_CHUNK)
    tri = same & ((ci >= ri) if reverse else (ci <= ri))
    tri_f = tri.astype(F32)
    tri_b = tri.astype(BF16)
    row_chunk = lax.broadcasted_iota(jnp.int32, (bt, HG_DK), 0) // HG_CHUNK
    in_chunk = [row_chunk == c for c in range(nchunk)]

    lb = lb_ref[...]
    fz = f_ref[...]
    log_sig = jnp.minimum(fz, 0.0) - jnp.log1p(jnp.exp(-jnp.abs(fz)))
    la = jnp.log(lb)
    lbb = jnp.log1p(-lb) + log_sig
    log_f = jnp.maximum(la, lbb) + jnp.log1p(jnp.exp(-jnp.abs(la - lbb)))
    kf_all = (1.0 - lb) * _sigmoid(-fz)
    pieces = jnp.dot(tri_b, jnp.concatenate(_split3(log_f), axis=1), preferred_element_type=F32)
    width = log_f.shape[1]
    cum_all = pieces[:, :width] + pieces[:, width:2 * width] + pieces[:, 2 * width:]

    for hh in range(HG_STEP_HEADS):
        sl = slice(hh * HG_DK, (hh + 1) * HG_DK)
        cum, kf, v = cum_all[:, sl], kf_all[:, sl], v_ref[:, sl]
        last = [cum[c * HG_CHUNK:c * HG_CHUNK + 1] if reverse else cum[(c + 1) * HG_CHUNK - 1:(c + 1) * HG_CHUNK]
                for c in range(nchunk)]
        tot = jnp.concatenate([jnp.broadcast_to(x, (HG_CHUNK, HG_DK)) for x in last], axis=0)
        q_in = q_ref[:, sl] * jnp.exp(cum)
        k_in = kf * jnp.exp(-cum)
        k_end = kf * jnp.exp(tot - cum)
        o_intra = _dot(_dot_nt(q_in, k_in) * tri_f, v)
        v_by_chunk = jnp.concatenate([jnp.where(in_chunk[c], v, 0.0) for c in range(nchunk)], axis=1)
        d_st = _dot_tn(v_by_chunk, k_end)
        st = st_ref[hh]
        before = [None] * nchunk
        for c in order:
            before[c] = st
            st = st * jnp.exp(last[c]) + d_st[c * HG_DV:(c + 1) * HG_DV]
        st_ref[hh] = st
        q_by_chunk = jnp.concatenate([jnp.where(in_chunk[c], q_in, 0.0) for c in range(nchunk)], axis=1)
        o = o_intra + _dot_nt(q_by_chunk, jnp.concatenate(before, axis=1))
        if finalize:
            o = o + prev_ref[:, sl]
            o = o * lax.rsqrt(jnp.mean(o * o, axis=-1, keepdims=True) + NORM_EPS) * nw_ref[...]
        o_ref[:, sl] = o
        if emit_state:
            rest[1][hh] = st


def hgrn2_pass(z, lb, s0, batch, t, f_off, reverse, prev=None, norm_w=None, emit_state=False):
    bt = min(HG_BLOCK, t)
    nblk = t // bt
    nh = HG_STEP_HEADS
    wide = nh * LANES
    finalize = prev is not None
    assert OD_HG_BLK % nh == 0 and HG_HEADS % nh == 0, "head groups must start on a whole column block"
    pos = (lambda i: nblk - 1 - i) if reverse else (lambda i: i)
    blk = lambda off: pl.BlockSpec((bt, wide), lambda b, h, i, off=off: (b * nblk + pos(i), off // nh + h))
    state = pl.BlockSpec((None, nh, HG_DV, HG_DK), lambda b, h, i: (b, h, 0, 0))
    ins = [z, z, z, lb.reshape(1, -1), s0]
    specs = [blk(OD_HG_BLK), blk(OD_HG_BLK + f_off), blk(OD_HG_BLK + 3 * HG_HEADS),
             pl.BlockSpec((1, wide), lambda b, h, i: (0, h)), state]
    if finalize:
        ins += [prev, norm_w.reshape(1, -1)]
        specs += [blk(0), pl.BlockSpec((1, HG_DV), lambda b, h, i: (0, 0))]
    out_shape = [jax.ShapeDtypeStruct((batch * t, HG_WIDTH), F32)]
    out_specs = [blk(0)]
    if emit_state:
        out_shape.append(jax.ShapeDtypeStruct((batch, HG_HEADS, HG_DV, HG_DK), F32))
        out_specs.append(state)
    return pl.pallas_call(
        functools.partial(_hg_kernel, reverse=reverse, finalize=finalize, emit_state=emit_state),
        out_shape=out_shape,
        grid=(batch, HG_HEADS // nh, nblk),
        in_specs=specs,
        out_specs=out_specs,
        scratch_shapes=[pltpu.VMEM((nh, HG_DV, HG_DK), F32)],
        compiler_params=_cparams(3),
        name="hgrn2_" + ("bwd" if reverse else "fwd"),
    )(*ins)


def _even_layer(x, xc, mod, norm_w, w_in, w_out, rpb, rw_params, batch, t, l):
    s2 = 3 * NA_WIDTH + RW_SHIFT_COLS
    width = RW_PAD_COLS - RW_SHIFT_COLS
    n_gate, n_rest = GATE_COLS // width, s2 // width
    plan = [n_rest + i for i in range(n_gate)] + list(range(n_rest)) + [None]
    w_ext = cast_reorder(w_in, plan, width, "even_w_in")
    tiles_per_seq = max(t // 1024, 1)
    z = in_projection(x, norm_w, mod, lambda i: i // tiles_per_seq, w_ext, "even_in_proj")
    z_c = in_projection(xc, norm_w, mod, lambda i: batch, w_ext, "even_in_proj_ctx")

    y_na = na_attention(z, z_c, rpb, batch, t, l)
    yc_na = ctx_attention(z_c, batch, l)

    mu, w0, w2, a0, a2, k_k, k_a, r_k, ln_w, ln_b = rw_params
    mu = jnp.pad(mu, ((0, 0), (0, RW_PAD_COLS - RW_SHIFT_COLS)))
    zpad = jnp.zeros((RW_RANK, RW_WIDTH), F32)
    w2p = jnp.stack([jnp.concatenate([w2[0], zpad]), jnp.concatenate([zpad, w2[1]])])
    a2p = jnp.stack([jnp.concatenate([a2[0], zpad]), jnp.concatenate([zpad, a2[1]])])
    vecs = (mu, w0, w2p, a0, a2p, k_k.reshape(1, -1), k_a.reshape(1, -1), r_k.reshape(1, -1))
    terms = rwkv_terms(z, batch, t, *vecs)
    terms_c = rwkv_terms(z_c, batch, l, *vecs)
    s_zero = jnp.zeros((RW_HEAD_DIM, RW_HEAD_DIM, LANES), F32)
    yc_f, yc_b, s_ctx = rwkv_scan(*terms_c[:5], s_zero)
    y_f, y_b, _ = rwkv_scan(*terms[:5], s_ctx)
    y_rw = rwkv_readout(y_f, y_b, terms[5], ln_w, ln_b).reshape(batch * t, RW_WIDTH)
    yc_rw = rwkv_readout(yc_f, yc_b, terms_c[5], ln_w, ln_b).reshape(batch * l, RW_WIDTH)

    w_out = w_out.astype(BF16)
    rows_per_seq = max(t // 256, 1)
    x = out_projection(y_na, y_rw, z, w_out, x, mod, lambda i: i // rows_per_seq, "even_out_proj")
    xc = out_projection(yc_na, yc_rw, z_c, w_out, xc, mod, lambda i: batch, "even_out_proj_ctx")
    return x, xc


def _odd_layer(x, xc, mod, norm_w, w_in, w_out, q_norm, w_uq, kv_norm, w_ukv, lb, hg_norm_w, final_w,
               batch, t, l):
    o3 = MLA_Q_RANK + MLA_KV_RANK + MLA_ROPE
    o4 = o3 + 4 * HG_FDIM
    zero = jnp.zeros((x.shape[1], MLA_A_COLS - o3), BF16)
    w_in = w_in.astype(BF16)
    w_ext = jnp.concatenate([w_in[:, o4:], w_in[:, :o3], zero, w_in[:, o3:o4]], axis=1)
    tiles_per_seq = max(t // 1024, 1)
    z = in_projection(x, norm_w, mod, lambda i: i // tiles_per_seq, w_ext, "odd_in_proj")
    z_c = in_projection(xc, norm_w, mod, lambda i: batch, w_ext, "odd_in_proj_ctx")

    cos, sin = _rope_tables(t)
    qk = MLA_NOPE + MLA_ROPE
    w_uq_pad = jnp.pad(w_uq.reshape(MLA_Q_RANK, MLA_HEADS, qk), ((0, 0), (0, 0), (0, MLA_QK_PAD - qk)))
    w_uq_pad = w_uq_pad.reshape(MLA_Q_RANK, MLA_HEADS * MLA_QK_PAD).astype(BF16)
    w_ukv = w_ukv.astype(BF16)
    q = mla_queries(z, q_norm, w_uq_pad, cos, sin, t)
    k, v = mla_keys_values(z, kv_norm, w_ukv, cos, sin, t)
    ones = jnp.ones((l, LANES), F32)
    kc, vc = mla_keys_values(z_c, kv_norm, w_ukv, ones, jnp.zeros_like(ones), l)
    y_mla = mla_attention(q, k, v, kc, vc, batch, t, l)

    s_zero = jnp.zeros((batch, HG_HEADS, HG_DV, HG_DK), F32)
    _, s_f = hgrn2_pass(z_c, lb, s_zero, batch, l, HG_HEADS, False, emit_state=True)
    _, s_b = hgrn2_pass(z_c, lb, s_zero, batch, l, 2 * HG_HEADS, True, emit_state=True)
    o_f, = hgrn2_pass(z, lb, s_f, batch, t, HG_HEADS, False)
    y_hg, = hgrn2_pass(z, lb, s_b, batch, t, 2 * HG_HEADS, True, prev=o_f, norm_w=hg_norm_w)

    rows_per_seq = max(t // 256, 1)
    return out_projection(y_mla, y_hg, z, w_out.astype(BF16), x, mod, lambda i: i // rows_per_seq,
                          "odd_out_proj", final_w=final_w)


def kernel(x, c, ctx, c_ctx, ada_w, ada_b, norm_w, e_w_in, e_w_out, na_rpb, rw_mu, rw_w0, rw_w2, rw_a0, rw_a2, rw_k_k, rw_k_a, rw_r_k, rw_ln_w, rw_ln_b, o_w_in, o_w_out, mla_q_norm, mla_w_uq, mla_kv_norm, mla_w_ukv, hg_lower_bounds, hg_norm_w, final_norm_w):
    batch, t, d = x.shape
    l = ctx.shape[1]
    assert ada_w.shape[0] == 2, "one even and one odd layer"
    cond = jnp.concatenate([c, c_ctx[None, :], jnp.zeros((8 - batch - 1, d), F32)], axis=0)
    mod = modulation(cond, ada_w, ada_b)
    s = jax.nn.softmax(hg_lower_bounds.astype(F32), axis=0)
    lower = jnp.cumsum(s, axis=0) - s[0]

    xf, xcf = x.reshape(batch * t, d), ctx.reshape(batch * l, d)
    rw_params = (rw_mu[0], rw_w0[0], rw_w2[0], rw_a0[0], rw_a2[0], rw_k_k[0], rw_k_a[0], rw_r_k[0],
                 rw_ln_w[0], rw_ln_b[0])
    xf, xcf = _even_layer(xf, xcf, mod[0].reshape(8, 1, 3 * d), norm_w[0], e_w_in[0], e_w_out[0], na_rpb[0],
                          rw_params, batch, t, l)
    out = _odd_layer(xf, xcf, mod[1].reshape(8, 1, 3 * d), norm_w[1], o_w_in[0], o_w_out[0], mla_q_norm[0],
                     mla_w_uq[0], mla_kv_norm[0], mla_w_ukv[0], lower[1], hg_norm_w[0], final_norm_w,
                     batch, t, l)
    return out.reshape(batch, t, d)
```

```python
import functools
import math

import numpy as np
import jax
import jax.numpy as jnp
from jax import lax
from jax.experimental import pallas as pl
from jax.experimental.pallas import tpu as pltpu

F32 = jnp.float32
BF16 = jnp.bfloat16

GRID_W = 64
NORM_EPS = 1e-6
ROPE_BASE = 10000.0

NA_HEADS = 8
NA_HEAD_DIM = 128
NA_WIDTH = NA_HEADS * NA_HEAD_DIM
NA_WIN_ROWS = 8
NA_WIN_COLS = 16
NA_QROWS = 4
NA_KROWS = 12
NA_STEP_HEADS = 4

RW_HEAD_DIM = 64
RW_HEADS = 16
RW_WIDTH = RW_HEADS * RW_HEAD_DIM
RW_RANK = 64
RW_SHIFT_COLS = 3 * RW_WIDTH + 4 * RW_RANK
RW_PAD_COLS = 3584
RW_GN_EPS = 64e-5
RW_SCAN_CHUNK = 64
RW_DECAY_SCALE = math.exp(-0.5)
RW_STATE_PAD = 0

MLA_HEADS = 8
MLA_Q_RANK = 512
MLA_KV_RANK = 512
MLA_NOPE = 128
MLA_ROPE = 64
MLA_V = 128
MLA_QK_PAD = 256
MLA_WIDTH = MLA_HEADS * MLA_V
MLA_A_COLS = 1536
MLA_KEY_CHUNK = 1024

HG_HEADS = 8
HG_DK = 128
HG_DV = 128
HG_FDIM = HG_HEADS * HG_DK
HG_WIDTH = HG_HEADS * HG_DV
HG_CHUNK = 32
HG_BLOCK = 256
HG_STEP_HEADS = 4

LANES = 128

GATE_COLS = 2048
EV_Q_BLK = GATE_COLS // LANES
EV_RW_BLK = EV_Q_BLK + 3 * NA_WIDTH // LANES
OD_A_BLK = GATE_COLS // LANES
OD_HG_BLK = OD_A_BLK + MLA_A_COLS // LANES
VMEM_LIMIT = 48 * 1024 * 1024
NEG_INF = -1e30
LOG2E = math.log2(math.e)


def _cparams(n_axes):
    return pltpu.CompilerParams(dimension_semantics=("arbitrary",) * n_axes, vmem_limit_bytes=VMEM_LIMIT)


def _sigmoid(x):
    return 1.0 / (1.0 + jnp.exp(-x))


def _silu(x):
    return x * _sigmoid(x)


def _dot(a, b):
    return jnp.dot(a.astype(BF16), b.astype(BF16), preferred_element_type=F32)


def _dot_nt(a, b):
    return lax.dot_general(a.astype(BF16), b.astype(BF16), (((1,), (1,)), ((), ())), preferred_element_type=F32)


def _dot_tn(a, b):
    return lax.dot_general(a.astype(BF16), b.astype(BF16), (((0,), (0,)), ((), ())), preferred_element_type=F32)


def _mod_kernel(c_ref, w_ref, b_ref, o_ref):
    o_ref[...] = _dot(_silu(c_ref[...]), w_ref[...]) + b_ref[...]


def modulation(cond, ada_w, ada_b, tn=512):
    depth, d, n = ada_w.shape
    return pl.pallas_call(
        _mod_kernel,
        out_shape=jax.ShapeDtypeStruct((depth, 8, n), F32),
        grid=(depth, n // tn),
        in_specs=[pl.BlockSpec((8, d), lambda l, j: (0, 0)),
                  pl.BlockSpec((None, d, tn), lambda l, j: (l, 0, j)),
                  pl.BlockSpec((None, 1, tn), lambda l, j: (l, 0, j))],
        out_specs=pl.BlockSpec((None, 8, tn), lambda l, j: (l, 0, j)),
        compiler_params=_cparams(2),
        name="adaln_modulation",
    )(cond, ada_w, ada_b.reshape(depth, 1, n))


def _cast_reorder_kernel(src_ref, zero_ref, x_ref, o_ref):
    del src_ref
    o_ref[...] = jnp.where(zero_ref[pl.program_id(0)] != 0, 0.0, x_ref[...]).astype(BF16)


def cast_reorder(w, plan, width, name):
    d = w.shape[0]
    src = np.array([0 if a is None else a for a in plan], np.int32)
    zero = np.array([a is None for a in plan], np.int32)
    return pl.pallas_call(
        _cast_reorder_kernel,
        out_shape=jax.ShapeDtypeStruct((d, width * len(plan)), BF16),
        grid_spec=pltpu.PrefetchScalarGridSpec(
            num_scalar_prefetch=2,
            grid=(len(plan),),
            in_specs=[pl.BlockSpec((d, width), lambda j, src, zero: (0, src[j]))],
            out_specs=pl.BlockSpec((d, width), lambda j, src, zero: (0, j))),
        compiler_params=_cparams(1),
        name=name,
    )(jnp.asarray(src), jnp.asarray(zero), w)


def _inproj_kernel(x_ref, nw_ref, shift_ref, scale_ref, w_ref, o_ref, h_ref):
    @pl.when(pl.program_id(1) == 0)
    def _():
        x = x_ref[...]
        y = x * lax.rsqrt(jnp.mean(x * x, axis=-1, keepdims=True) + NORM_EPS) * nw_ref[...]
        h_ref[...] = (y * (1.0 + scale_ref[...]) + shift_ref[...]).astype(BF16)

    o_ref[...] = jnp.dot(h_ref[...], w_ref[...], preferred_element_type=F32)


def in_projection(x, norm_w, mod, mod_row, w, name, tm=1024, tn=512):
    m, d = x.shape
    n = w.shape[1]
    tm = min(tm, m)
    return pl.pallas_call(
        _inproj_kernel,
        out_shape=jax.ShapeDtypeStruct((m, n), F32),
        grid=(m // tm, n // tn),
        in_specs=[pl.BlockSpec((tm, d), lambda i, j: (i, 0)),
                  pl.BlockSpec((1, d), lambda i, j: (0, 0)),
                  pl.BlockSpec((None, 1, d), lambda i, j: (mod_row(i), 0, 0)),
                  pl.BlockSpec((None, 1, d), lambda i, j: (mod_row(i), 0, 1)),
                  pl.BlockSpec((d, tn), lambda i, j: (0, j))],
        out_specs=pl.BlockSpec((tm, tn), lambda i, j: (i, j)),
        scratch_shapes=[pltpu.VMEM((tm, d), BF16)],
        compiler_params=_cparams(2),
        name=name,
    )(x, norm_w.reshape(1, d), mod, mod, w)


def _outproj_kernel(ya_ref, yb_ref, g_ref, w_ref, x_ref, gm_ref, *rest, ka, final):
    o_ref = rest[-1]
    sg = _silu(g_ref[...])
    acc = _dot(ya_ref[...].astype(F32) * sg[:, :ka], w_ref[:ka, :])
    acc += _dot(yb_ref[...].astype(F32) * sg[:, ka:], w_ref[ka:, :])
    out = x_ref[...] + gm_ref[...] * acc
    if final:
        out = out * lax.rsqrt(jnp.mean(out * out, axis=-1, keepdims=True) + NORM_EPS) * rest[0][...]
    o_ref[...] = out


def out_projection(ya, yb, gate, w, x, mod, mod_row, name, final_w=None, tm=256):
    m, d = x.shape
    ka, kb = ya.shape[1], yb.shape[1]
    tm = min(tm, m)
    ins = [ya, yb, gate, w, x, mod]
    specs = [pl.BlockSpec((tm, ka), lambda i: (i, 0)),
             pl.BlockSpec((tm, kb), lambda i: (i, 0)),
             pl.BlockSpec((tm, ka + kb), lambda i: (i, 0)),
             pl.BlockSpec((ka + kb, d), lambda i: (0, 0)),
             pl.BlockSpec((tm, d), lambda i: (i, 0)),
             pl.BlockSpec((None, 1, d), lambda i: (mod_row(i), 0, 2))]
    if final_w is not None:
        ins.append(final_w.reshape(1, d))
        specs.append(pl.BlockSpec((1, d), lambda i: (0, 0)))
    return pl.pallas_call(
        functools.partial(_outproj_kernel, ka=ka, final=final_w is not None),
        out_shape=jax.ShapeDtypeStruct((m, d), F32),
        grid=(m // tm,),
        in_specs=specs,
        out_specs=pl.BlockSpec((tm, d), lambda i: (i, 0)),
        compiler_params=_cparams(1),
        name=name,
    )(*ins)


def _softmax_pv(q, parts, scale):
    c = scale * LOG2E
    m = denom = num = None
    for k, bias, v in parts:
        x = _dot_nt(q, k) * c
        if bias is not None:
            x = x + bias
        m_part = jnp.max(x, axis=-1, keepdims=True)
        if m is None:
            m = m_part
            p = jnp.exp2(x - m)
            denom = jnp.sum(p, axis=-1, keepdims=True)
            num = _dot(p, v)
        else:
            m_new = jnp.maximum(m, m_part)
            alpha = jnp.exp2(m - m_new)
            p = jnp.exp2(x - m_new)
            denom = denom * alpha + jnp.sum(p, axis=-1, keepdims=True)
            num = num * alpha + _dot(p, v)
            m = m_new
    return num / denom


def _softmax_pv_joint(q, parts, scale):
    c = scale * LOG2E
    logits = [_dot_nt(q, k) * c if bias is None else _dot_nt(q, k) * c + bias for k, bias, _ in parts]
    m = functools.reduce(jnp.maximum, [jnp.max(x, axis=-1, keepdims=True) for x in logits])
    ps = [jnp.exp2(x - m) for x in logits]
    denom = functools.reduce(lambda a, b: a + b, [jnp.sum(p, axis=-1, keepdims=True) for p in ps])
    num = functools.reduce(lambda a, b: a + b, [_dot(p, v) for p, (_, _, v) in zip(ps, parts)])
    return num / denom


def _na_kernel(idx_ref, q_ref, k_ref, v_ref, kc_ref, vc_ref, tab_ref, o_ref, bias_ref, *, rows, scale):
    j = pl.program_id(2)
    npair = NA_KROWS // 2
    start = pl.multiple_of(jnp.clip(NA_QROWS * j - NA_WIN_ROWS // 2, 0, rows - NA_KROWS) * GRID_W, GRID_W)
    win = pl.ds(start, NA_KROWS * GRID_W)
    for hd in range(NA_STEP_HEADS):
        sl = slice(hd * LANES, (hd + 1) * LANES)
        for qr in range(NA_QROWS):
            for m in range(npair):
                tile = tab_ref[hd, idx_ref[(j * NA_QROWS + qr) * npair + m]]
                bias_ref[hd, qr * GRID_W:(qr + 1) * GRID_W, m * LANES:(m + 1) * LANES] = tile
        parts = [(k_ref[win, sl], bias_ref[hd], v_ref[win, sl]), (kc_ref[:, sl], None, vc_ref[:, sl])]
        o_ref[:, sl] = _softmax_pv_joint(q_ref[:, sl], parts, scale)


def _na_tile_ids(rows):
    kh = min(NA_WIN_ROWS, rows)
    n = 2 * NA_WIN_ROWS - 1
    ids = []
    for j in range(rows // NA_QROWS):
        ks = int(np.clip(NA_QROWS * j - NA_WIN_ROWS // 2, 0, rows - NA_KROWS))
        for qr in range(NA_QROWS):
            r = NA_QROWS * j + qr
            r0 = int(np.clip(r - kh // 2, 0, rows - kh))
            for m in range(NA_KROWS // 2):
                kr = ks + 2 * m
                dy = kr - r + NA_WIN_ROWS - 1
                first, second = r0 <= kr < r0 + kh, r0 <= kr + 1 < r0 + kh
                ids.append(dy if first and second else n + dy if first else 2 * n + dy + 1 if second else 3 * n)
    return np.array(ids, np.int32)


def _na_bias_tiles(rpb):
    col = np.arange(GRID_W)
    cs = np.clip(col - NA_WIN_COLS // 2, 0, GRID_W - NA_WIN_COLS)
    kc = col[None, :]
    in_win = (kc >= cs[:, None]) & (kc < cs[:, None] + NA_WIN_COLS)
    dx = np.where(in_win, kc - col[:, None] + NA_WIN_COLS - 1, 0)
    pick = jnp.asarray(dx[:, :, None] == np.arange(rpb.shape[-1]), F32)
    picked = jnp.einsum("hyj,cqj->hycq", rpb, pick, precision=lax.Precision.HIGHEST)
    tab = jnp.where(in_win[None, None], picked * LOG2E, NEG_INF)
    neg = jnp.full_like(tab[:, :1], NEG_INF)
    nxt = jnp.concatenate([tab[:, 1:], neg], axis=1)
    both = jnp.concatenate([tab, nxt], axis=-1)
    first = jnp.concatenate([tab, jnp.broadcast_to(neg, tab.shape)], axis=-1)
    second = jnp.concatenate([jnp.broadcast_to(neg, tab.shape), tab], axis=-1)
    return jnp.concatenate([both, first, second, jnp.concatenate([neg, neg], axis=-1)], axis=1)


def na_attention(z, z_c, rpb, batch, t, l):
    rows = t // GRID_W
    nj = rows // NA_QROWS
    tq = NA_QROWS * GRID_W
    ids = jnp.asarray(_na_tile_ids(rows))
    tiles = _na_bias_tiles(rpb)
    nh = NA_STEP_HEADS
    wide = nh * LANES
    q0, k0, v0 = (EV_Q_BLK + i * NA_HEADS for i in range(3))
    assert q0 % nh == 0 and NA_HEADS % nh == 0, "head groups must start on a whole column block"
    return pl.pallas_call(
        functools.partial(_na_kernel, rows=rows, scale=NA_HEAD_DIM ** -0.5),
        out_shape=jax.ShapeDtypeStruct((batch * t, NA_WIDTH), F32),
        grid_spec=pltpu.PrefetchScalarGridSpec(
            num_scalar_prefetch=1,
            grid=(NA_HEADS // nh, batch, nj),
            in_specs=[pl.BlockSpec((tq, wide), lambda h, b, j, g: (b * nj + j, q0 // nh + h)),
                      pl.BlockSpec((t, wide), lambda h, b, j, g: (b, k0 // nh + h)),
                      pl.BlockSpec((t, wide), lambda h, b, j, g: (b, v0 // nh + h)),
                      pl.BlockSpec((l, wide), lambda h, b, j, g: (b, k0 // nh + h)),
                      pl.BlockSpec((l, wide), lambda h, b, j, g: (b, v0 // nh + h)),
                      pl.BlockSpec((nh,) + tiles.shape[1:], lambda h, b, j, g: (h, 0, 0, 0))],
            out_specs=pl.BlockSpec((tq, wide), lambda h, b, j, g: (b * nj + j, h)),
            scratch_shapes=[pltpu.VMEM((nh, tq, NA_KROWS * GRID_W), F32)]),
        compiler_params=_cparams(3),
        name="na_attention",
    )(ids, z, z, z, z_c, z_c, tiles)


def _dense_attn_kernel(q_ref, k_ref, v_ref, o_ref, *, scale):
    o_ref[...] = _softmax_pv(q_ref[...], [(k_ref[...], None, v_ref[...])], scale)


def ctx_attention(z_c, batch, l):
    h8 = NA_HEADS
    return pl.pallas_call(
        functools.partial(_dense_attn_kernel, scale=NA_HEAD_DIM ** -0.5),
        out_shape=jax.ShapeDtypeStruct((batch * l, NA_WIDTH), F32),
        grid=(batch, h8),
        in_specs=[pl.BlockSpec((l, LANES), lambda b, h: (b, EV_Q_BLK + h)),
                  pl.BlockSpec((l, LANES), lambda b, h: (b, EV_Q_BLK + h8 + h)),
                  pl.BlockSpec((l, LANES), lambda b, h: (b, EV_Q_BLK + 2 * h8 + h))],
        out_specs=pl.BlockSpec((l, LANES), lambda b, h: (b, h)),
        compiler_params=_cparams(2),
        name="ctx_attention",
    )(z_c, z_c, z_c)


def _head_sum(x):
    first = lax.broadcasted_iota(jnp.int32, x.shape, 1) < RW_HEAD_DIM
    s0 = jnp.sum(jnp.where(first, x, 0.0), axis=-1, keepdims=True)
    s1 = jnp.sum(jnp.where(first, 0.0, x), axis=-1, keepdims=True)
    return jnp.where(first, s0, s1)


def _rw_terms_kernel(ur_ref, uk_ref, uv_ref, uwa_ref, pr_ref, pk_ref, pv_ref, pwa_ref, nr_ref, nk_ref, nv_ref,
                     nwa_ref, mu_ref, w0_ref, w2_ref, a0_ref, a2_ref, kk_ref, ka_ref, rk_ref,
                     rv_o, wk0_o, wk1_o, bk0_o, bk1_o, bonus_o, *, nt):
    i = pl.program_id(1)
    tm = ur_ref.shape[0]

    def shifted(u_ref, p_ref, n_ref, col0):
        u = u_ref[...]
        mu = mu_ref[:, col0:col0 + u.shape[1]]
        row = lax.broadcasted_iota(jnp.int32, u.shape, 0)
        before = jnp.where(i == 0, 0.0, p_ref[7:8, :])
        after = jnp.where(i == nt - 1, 0.0, n_ref[0:1, :])
        prev = jnp.where(row == 0, before, pltpu.roll(u, 1, axis=0))
        nxt = jnp.where(row == tm - 1, after, pltpu.roll(u, tm - 1, axis=0))
        return u + mu[0:1] * (prev - u) + mu[1:2] * (nxt - u)

    r_all = shifted(ur_ref, pr_ref, nr_ref, 0)
    k_all = shifted(uk_ref, pk_ref, nk_ref, RW_WIDTH)
    v_all = shifted(uv_ref, pv_ref, nv_ref, 2 * RW_WIDTH)
    wa = shifted(uwa_ref, pwa_ref, nwa_ref, 3 * RW_WIDTH)
    wd = jnp.tanh(wa[:, :LANES]).astype(BF16)
    ad = wa[:, LANES:].astype(BF16)
    first = lax.broadcasted_iota(jnp.int32, (tm, LANES), 1) < RW_HEAD_DIM

    def head_rows(x, y):
        return (jnp.where(first, x, pltpu.roll(y, RW_HEAD_DIM, axis=1)),
                jnp.where(first, pltpu.roll(x, RW_HEAD_DIM, axis=1), y))

    outs = (rv_o, wk0_o, wk1_o, bk0_o, bk1_o)
    even = [[] for _ in outs]
    odd = [[] for _ in outs]
    bonus = []
    for p in range(RW_WIDTH // LANES):
        sl = slice(p * LANES, (p + 1) * LANES)
        r, k, v = r_all[:, sl], k_all[:, sl], v_all[:, sl]
        kk = k * kk_ref[:, sl]
        kk = kk / jnp.maximum(jnp.sqrt(_head_sum(kk * kk)), 1e-12)
        kd_sum = jnp.zeros_like(k)
        per_dir = []
        for d in range(2):
            w = w0_ref[d:d + 1, sl] + jnp.dot(wd, w2_ref[d, :, sl].astype(BF16), preferred_element_type=F32)
            a = _sigmoid(a0_ref[d:d + 1, sl] + jnp.dot(ad, a2_ref[d, :, sl].astype(BF16), preferred_element_type=F32))
            kd = k * (1.0 + (a - 1.0) * ka_ref[:, sl])
            dec = jnp.exp(-RW_DECAY_SCALE * _sigmoid(w))
            per_dir.append(((dec, kd), (kk * a, kk)))
            kd_sum = kd_sum + kd
        pairs = [(r, v), per_dir[0][0], per_dir[1][0], per_dir[0][1], per_dir[1][1]]
        bonus.append(_head_sum(r * kd_sum * rk_ref[:, sl]) * v)
        for ev, od, (x, y) in zip(even, odd, pairs):
            e, o = head_rows(x, y)
            ev.append(e)
            od.append(o)
    for o_ref, ev, od in zip(outs, even, odd):
        o_ref[...] = jnp.swapaxes(jnp.stack(ev + od, axis=0), 0, 1)
    bonus_o[...] = jnp.swapaxes(jnp.stack(bonus, axis=0), 0, 1)


def rwkv_terms(z, n_seq, t, mu, w0, w2p, a0, a2p, k_k, k_a, r_k, tm=256):
    tm = min(tm, t)
    nt = t // tm
    npair = RW_WIDTH // LANES
    wide0 = EV_RW_BLK * LANES // RW_WIDTH
    wa_blk = (EV_RW_BLK + 3 * npair) // 2
    n8 = tm // 8

    def cur(width, cb):
        return pl.BlockSpec((tm, width), lambda b, i: (b * nt + i, cb))

    def before(width, cb):
        return pl.BlockSpec((8, width), lambda b, i: (jnp.maximum((b * nt + i) * n8 - 1, 0), cb))

    def after(width, cb):
        return pl.BlockSpec((8, width), lambda b, i: (jnp.minimum((b * nt + i + 1) * n8, n_seq * nt * n8 - 1), cb))

    cols = [(RW_WIDTH, wide0), (RW_WIDTH, wide0 + 1), (RW_WIDTH, wide0 + 2), (2 * LANES, wa_blk)]
    whole = lambda a: pl.BlockSpec(a.shape, lambda b, i: (0,) * a.ndim)
    params = (mu, w0, w2p, a0, a2p, k_k, k_a, r_k)
    out = pl.BlockSpec((tm, npair, LANES), lambda b, i: (i, b, 0))
    out2 = pl.BlockSpec((tm, 2 * npair, LANES), lambda b, i: (i, b, 0))
    return pl.pallas_call(
        functools.partial(_rw_terms_kernel, nt=nt),
        out_shape=[jax.ShapeDtypeStruct((t, n_seq * 2 * npair, LANES), F32)] * 5
        + [jax.ShapeDtypeStruct((t, n_seq * npair, LANES), F32)],
        grid=(n_seq, nt),
        in_specs=([cur(*c) for c in cols] + [before(*c) for c in cols] + [after(*c) for c in cols]
                  + [whole(a) for a in params]),
        out_specs=[out2] * 5 + [out],
        compiler_params=_cparams(2),
        name="rwkv_terms",
    )(*([z] * 12), *params)


def _to_lanes(n):
    q = jnp.concatenate([n, pltpu.roll(n, RW_HEAD_DIM, axis=1)], axis=0)
    return q.T[:RW_HEAD_DIM]


def _rw_scan_kernel(rvf, rvb, wk0, wk1, bk0, bk1, s0_ref, yf_ref, yb_ref, sfin_ref,
                    s_ref, z_ref, znext_ref, g_ref, ybuf_ref, *, chunk):
    n = RW_HEAD_DIM
    nb = rvf.shape[1] // (2 * RW_WIDTH // LANES)
    R, K, V, A, B = range(5)

    @pl.when(pl.program_id(0) == 0)
    def _():
        s_ref[:, :n, :] = s0_ref[...]

    def convert(t_src):
        def pair_tiles(f, bw):
            both = (f[t_src], bw[chunk - 1 - t_src])
            x = jnp.concatenate([src[(b * 2 + par) * 8:(b * 2 + par) * 8 + 8]
                                 for par in range(2) for src in both for b in range(nb)], axis=0)
            xt = x.T
            return xt[:n], xt[n:]

        r, v = pair_tiles(rvf, rvb)
        w, k = pair_tiles(wk0, wk1)
        b, kk = pair_tiles(bk0, bk1)
        g_prev = g_ref[0]
        g = g_prev * w
        g_ref[0] = g
        inv = 1.0 / g
        znext_ref[R] = r * g
        znext_ref[K] = k * inv
        znext_ref[V] = v
        znext_ref[A] = -(kk * g_prev)
        znext_ref[B] = b * inv

    def emit_y(t):
        rows = _to_lanes(ybuf_ref[...])
        yf_ref[t] = rows[:n // 2]
        yb_ref[chunk - 1 - t] = rows[n // 2:]

    g_ref[0] = jnp.ones((n, LANES), F32)
    ybuf_ref[...] = jnp.zeros((n, LANES), F32)
    convert(0)

    def step(t, carry):
        emit_y(jnp.maximum(t - 1, 0))
        z_ref[...] = znext_ref[...]
        g_ref[1] = g_ref[0]
        convert(jnp.minimum(t + 1, chunk - 1))
        sa = [jnp.zeros((n, LANES), F32), jnp.zeros((n, LANES), F32)]
        for k in range(n):
            sa[k % 2] = sa[k % 2] + s_ref[k, :n, :] * z_ref[A, k:k + 1, :]
        sa = sa[0] + sa[1]
        vt = z_ref[V]
        y = [jnp.zeros((n, LANES), F32), jnp.zeros((n, LANES), F32)]
        for k in range(n):
            sk = s_ref[k, :n, :] + sa * z_ref[B, k:k + 1, :] + vt * z_ref[K, k:k + 1, :]
            s_ref[k, :n, :] = sk
            y[k % 2] = y[k % 2] + sk * z_ref[R, k:k + 1, :]
        ybuf_ref[...] = y[0] + y[1]
        return carry

    lax.fori_loop(0, chunk, step, 0)
    emit_y(chunk - 1)
    for k in range(n):
        s_ref[k, :n, :] = s_ref[k, :n, :] * g_ref[1, k:k + 1, :]
    sfin_ref[...] = s_ref[:, :n, :]


def rwkv_scan(rv, wk0, wk1, bk0, bk1, s0):
    t, nrow, _ = rv.shape
    n = RW_HEAD_DIM
    chunk = RW_SCAN_CHUNK
    nc = t // chunk
    assert nrow == LANES // 2, "the states of both directions fill the 128 lanes"
    fwd = pl.BlockSpec((chunk, nrow, LANES), lambda i: (i, 0, 0))
    bwd = pl.BlockSpec((chunk, nrow, LANES), lambda i: (nc - 1 - i, 0, 0))
    yfwd = pl.BlockSpec((chunk, nrow // 2, LANES), lambda i: (i, 0, 0))
    ybwd = pl.BlockSpec((chunk, nrow // 2, LANES), lambda i: (nc - 1 - i, 0, 0))
    state = pl.BlockSpec((n, n, LANES), lambda i: (0, 0, 0))
    slab = jax.ShapeDtypeStruct((t, nrow // 2, LANES), F32)
    return pl.pallas_call(
        functools.partial(_rw_scan_kernel, chunk=chunk),
        out_shape=[slab, slab, jax.ShapeDtypeStruct((n, n, LANES), F32)],
        grid=(nc,),
        in_specs=[fwd, bwd, fwd, bwd, fwd, bwd, state],
        out_specs=[yfwd, ybwd, state],
        scratch_shapes=[pltpu.VMEM((n, n + RW_STATE_PAD, LANES), F32), pltpu.VMEM((5, n, LANES), F32),
                        pltpu.VMEM((5, n, LANES), F32), pltpu.VMEM((2, n, LANES), F32),
                        pltpu.VMEM((n, LANES), F32)],
        compiler_params=_cparams(1),
        name="rwkv_scan",
    )(rv, rv, wk0, wk1, bk0, bk1, s0)


def _rw_readout_kernel(yf_ref, yb_ref, bonus_ref, lw_ref, lb_ref, o_ref):
    npair = RW_WIDTH // LANES
    y_all = jnp.swapaxes(yf_ref[...] + yb_ref[...], 0, 1)
    bonus_all = jnp.swapaxes(bonus_ref[...], 0, 1)
    for q in range(yf_ref.shape[1]):
        b, p = divmod(q, npair)
        sl = slice(p * LANES, (p + 1) * LANES)
        y = y_all[q]
        mu = _head_sum(y) * (1.0 / RW_HEAD_DIM)
        d = y - mu
        var = _head_sum(d * d) * (1.0 / RW_HEAD_DIM)
        o_ref[b, :, sl] = d * lax.rsqrt(var + RW_GN_EPS) * lw_ref[:, sl] + lb_ref[:, sl] + bonus_all[q]


def rwkv_readout(y_f, y_b, bonus, ln_w, ln_b, tm=256):
    t, nq, _ = y_f.shape
    tm = min(tm, t)
    batch = nq // (RW_WIDTH // LANES)
    slab = pl.BlockSpec((tm, nq, LANES), lambda i: (i, 0, 0))
    vec = pl.BlockSpec((1, RW_WIDTH), lambda i: (0, 0))
    return pl.pallas_call(
        _rw_readout_kernel,
        out_shape=jax.ShapeDtypeStruct((batch, t, RW_WIDTH), F32),
        grid=(t // tm,),
        in_specs=[slab, slab, slab, vec, vec],
        out_specs=pl.BlockSpec((batch, tm, RW_WIDTH), lambda i: (0, i, 0)),
        compiler_params=_cparams(1),
        name="rwkv_readout",
    )(y_f, y_b, bonus, ln_w.reshape(1, -1), ln_b.reshape(1, -1))


def _rms(x, w):
    return x * lax.rsqrt(jnp.mean(x * x, axis=-1, keepdims=True) + NORM_EPS) * w


def _rope(x, cos, sin):
    odd = (lax.broadcasted_iota(jnp.int32, x.shape, 1) & 1) == 1
    swapped = jnp.where(odd, pltpu.roll(x, 1, axis=1), pltpu.roll(x, LANES - 1, axis=1))
    return x * cos + swapped * sin


def _mla_q_kernel(c_ref, nw_ref, w_ref, cos_ref, sin_ref, q_ref):
    h = _rms(c_ref[...], nw_ref[...]).astype(BF16)
    acc = jnp.dot(h, w_ref[...], preferred_element_type=F32)
    cos, sin = cos_ref[...], sin_ref[...]
    for hd in range(MLA_HEADS):
        lo = hd * MLA_QK_PAD
        q_ref[:, lo:lo + MLA_NOPE] = acc[:, lo:lo + MLA_NOPE].astype(BF16)
        q_ref[:, lo + MLA_NOPE:lo + MLA_QK_PAD] = _rope(acc[:, lo + MLA_NOPE:lo + MLA_QK_PAD], cos, sin).astype(BF16)


def mla_queries(z, q_norm, w_uq_pad, cos, sin, t, tm=512):
    m = z.shape[0]
    tm = min(tm, t)
    nt = t // tm
    width = MLA_HEADS * MLA_QK_PAD
    return pl.pallas_call(
        _mla_q_kernel,
        out_shape=jax.ShapeDtypeStruct((m, width), BF16),
        grid=(m // tm,),
        in_specs=[pl.BlockSpec((tm, MLA_Q_RANK), lambda i: (i, GATE_COLS // MLA_Q_RANK)),
                  pl.BlockSpec((1, MLA_Q_RANK), lambda i: (0, 0)),
                  pl.BlockSpec((MLA_Q_RANK, width), lambda i: (0, 0)),
                  pl.BlockSpec((tm, LANES), lambda i: (i % nt, 0)),
                  pl.BlockSpec((tm, LANES), lambda i: (i % nt, 0))],
        out_specs=pl.BlockSpec((tm, width), lambda i: (i, 0)),
        compiler_params=_cparams(1),
        name="mla_queries",
    )(z, q_norm.reshape(1, -1), w_uq_pad, cos, sin)


def _mla_kv_kernel(c_ref, pe_ref, nw_ref, w_ref, cos_ref, sin_ref, k_ref, v_ref):
    h = _rms(c_ref[...], nw_ref[...]).astype(BF16)
    acc = jnp.dot(h, w_ref[...], preferred_element_type=F32)
    pe = _rope(pe_ref[...], cos_ref[...], sin_ref[...]).astype(BF16)
    for hd in range(MLA_HEADS):
        src = hd * (MLA_NOPE + MLA_V)
        dst = hd * MLA_QK_PAD
        k_ref[:, dst:dst + MLA_NOPE] = acc[:, src:src + MLA_NOPE].astype(BF16)
        k_ref[:, dst + MLA_NOPE:dst + MLA_QK_PAD] = pe
        v_ref[:, hd * MLA_V:(hd + 1) * MLA_V] = acc[:, src + MLA_NOPE:src + MLA_NOPE + MLA_V].astype(BF16)


def mla_keys_values(z, kv_norm, w_ukv, cos, sin, t, tm=512):
    m = z.shape[0]
    tm = min(tm, t)
    nt = t // tm
    return pl.pallas_call(
        _mla_kv_kernel,
        out_shape=[jax.ShapeDtypeStruct((m, MLA_HEADS * MLA_QK_PAD), BF16),
                   jax.ShapeDtypeStruct((m, MLA_WIDTH), BF16)],
        grid=(m // tm,),
        in_specs=[pl.BlockSpec((tm, MLA_KV_RANK), lambda i: (i, (GATE_COLS + MLA_Q_RANK) // MLA_KV_RANK)),
                  pl.BlockSpec((tm, LANES), lambda i: (i, OD_A_BLK + (MLA_Q_RANK + MLA_KV_RANK) // LANES)),
                  pl.BlockSpec((1, MLA_KV_RANK), lambda i: (0, 0)),
                  pl.BlockSpec(w_ukv.shape, lambda i: (0, 0)),
                  pl.BlockSpec((tm, LANES), lambda i: (i % nt, 0)),
                  pl.BlockSpec((tm, LANES), lambda i: (i % nt, 0))],
        out_specs=[pl.BlockSpec((tm, MLA_HEADS * MLA_QK_PAD), lambda i: (i, 0)),
                   pl.BlockSpec((tm, MLA_WIDTH), lambda i: (i, 0))],
        compiler_params=_cparams(1),
        name="mla_keys_values",
    )(z, z, kv_norm.reshape(1, -1), w_ukv, cos, sin)


def _mla_attn_kernel(q_ref, k_ref, v_ref, kc_ref, vc_ref, o_ref, *, scale):
    parts = [(kc_ref[...], None, vc_ref[...])]
    nkeys = k_ref.shape[0]
    step = min(MLA_KEY_CHUNK, nkeys)
    for lo in range(0, nkeys, step):
        parts.append((k_ref[lo:lo + step, :], None, v_ref[lo:lo + step, :]))
    o_ref[...] = _softmax_pv(q_ref[...], parts, scale)


def mla_attention(q, k, v, kc, vc, batch, t, l, tq=2048):
    tq = min(tq, t)
    nq = t // tq
    return pl.pallas_call(
        functools.partial(_mla_attn_kernel, scale=(MLA_NOPE + MLA_ROPE) ** -0.5),
        out_shape=jax.ShapeDtypeStruct((batch * t, MLA_WIDTH), F32),
        grid=(batch, MLA_HEADS, nq),
        in_specs=[pl.BlockSpec((tq, MLA_QK_PAD), lambda b, h, i: (b * nq + i, h)),
                  pl.BlockSpec((t, MLA_QK_PAD), lambda b, h, i: (b, h)),
                  pl.BlockSpec((t, MLA_V), lambda b, h, i: (b, h)),
                  pl.BlockSpec((l, MLA_QK_PAD), lambda b, h, i: (b, h)),
                  pl.BlockSpec((l, MLA_V), lambda b, h, i: (b, h))],
        out_specs=pl.BlockSpec((tq, MLA_V), lambda b, h, i: (b * nq + i, h)),
        compiler_params=_cparams(3),
        name="mla_attention",
    )(q, k, v, kc, vc)


def _rope_tables(t):
    tok = np.arange(t)
    pos = np.stack([tok // GRID_W, tok % GRID_W], axis=-1).astype(np.float32)
    n_freq = MLA_ROPE // 4
    inv = (ROPE_BASE ** (-jnp.arange(n_freq, dtype=F32) / n_freq))
    ang = (jnp.asarray(pos)[:, :, None] * inv).reshape(t, MLA_ROPE // 2)
    cos = jnp.repeat(jnp.cos(ang), 2, axis=-1)
    sin = jnp.repeat(jnp.sin(ang), 2, axis=-1) * jnp.tile(jnp.array([-1.0, 1.0], F32), MLA_ROPE // 2)
    pad = ((0, 0), (0, LANES - MLA_ROPE))
    return jnp.pad(cos, pad), jnp.pad(sin, pad)


def _split3(x):
    hi = x.astype(BF16)
    r1 = x - hi.astype(F32)
    mid = r1.astype(BF16)
    lo = (r1 - mid.astype(F32)).astype(BF16)
    return hi, mid, lo


def _hg_kernel(q_ref, f_ref, v_ref, lb_ref, s0_ref, *rest, reverse, finalize, emit_state):
    st_ref = rest[-1]
    rest = rest[:-1]
    if finalize:
        prev_ref, nw_ref = rest[0], rest[1]
        rest = rest[2:]
    o_ref = rest[0]

    @pl.when(pl.program_id(2) == 0)
    def _():
        st_ref[...] = s0_ref[...]

    bt = q_ref.shape[0]
    nchunk = bt // HG_CHUNK
    order = range(nchunk - 1, -1, -1) if reverse else range(nchunk)
    ri = lax.broadcasted_iota(jnp.int32, (bt, bt), 0)
    ci = lax.broadcasted_iota(jnp.int32, (bt, bt), 1)
    same = (ri // HG_CHUNK) == (ci // HG_CHUNK)
    tri = same & ((ci >= ri) if reverse else (ci <= ri))
    tri_f = tri.astype(F32)
    tri_b = tri.astype(BF16)
    row_chunk = lax.broadcasted_iota(jnp.int32, (bt, HG_DK), 0) // HG_CHUNK
    in_chunk = [row_chunk == c for c in range(nchunk)]

    lb = lb_ref[...]
    fz = f_ref[...]
    log_sig = jnp.minimum(fz, 0.0) - jnp.log1p(jnp.exp(-jnp.abs(fz)))
    la = jnp.log(lb)
    lbb = jnp.log1p(-lb) + log_sig
    log_f = jnp.maximum(la, lbb) + jnp.log1p(jnp.exp(-jnp.abs(la - lbb)))
    kf_all = (1.0 - lb) * _sigmoid(-fz)
    pieces = jnp.dot(tri_b, jnp.concatenate(_split3(log_f), axis=1), preferred_element_type=F32)
    width = log_f.shape[1]
    cum_all = pieces[:, :width] + pieces[:, width:2 * width] + pieces[:, 2 * width:]

    for hh in range(HG_STEP_HEADS):
        sl = slice(hh * HG_DK, (hh + 1) * HG_DK)
        cum, kf, v = cum_all[:, sl], kf_all[:, sl], v_ref[:, sl]
        last = [cum[c * HG_CHUNK:c * HG_CHUNK + 1] if reverse else cum[(c + 1) * HG_CHUNK - 1:(c + 1) * HG_CHUNK]
                for c in range(nchunk)]
        tot = jnp.concatenate([jnp.broadcast_to(x, (HG_CHUNK, HG_DK)) for x in last], axis=0)
        q_in = q_ref[:, sl] * jnp.exp(cum)
        k_in = kf * jnp.exp(-cum)
        k_end = kf * jnp.exp(tot - cum)
        o_intra = _dot(_dot_nt(q_in, k_in) * tri_f, v)
        v_by_chunk = jnp.concatenate([jnp.where(in_chunk[c], v, 0.0) for c in range(nchunk)], axis=1)
        d_st = _dot_tn(v_by_chunk, k_end)
        st = st_ref[hh]
        before = [None] * nchunk
        for c in order:
            before[c] = st
            st = st * jnp.exp(last[c]) + d_st[c * HG_DV:(c + 1) * HG_DV]
        st_ref[hh] = st
        q_by_chunk = jnp.concatenate([jnp.where(in_chunk[c], q_in, 0.0) for c in range(nchunk)], axis=1)
        o = o_intra + _dot_nt(q_by_chunk, jnp.concatenate(before, axis=1))
        if finalize:
            o = o + prev_ref[:, sl]
            o = o * lax.rsqrt(jnp.mean(o * o, axis=-1, keepdims=True) + NORM_EPS) * nw_ref[...]
        o_ref[:, sl] = o
        if emit_state:
            rest[1][hh] = st


def hgrn2_pass(z, lb, s0, batch, t, f_off, reverse, prev=None, norm_w=None, emit_state=False):
    bt = min(HG_BLOCK, t)
    nblk = t // bt
    nh = HG_STEP_HEADS
    wide = nh * LANES
    finalize = prev is not None
    assert OD_HG_BLK % nh == 0 and HG_HEADS % nh == 0, "head groups must start on a whole column block"
    pos = (lambda i: nblk - 1 - i) if reverse else (lambda i: i)
    blk = lambda off: pl.BlockSpec((bt, wide), lambda b, h, i, off=off: (b * nblk + pos(i), off // nh + h))
    state = pl.BlockSpec((None, nh, HG_DV, HG_DK), lambda b, h, i: (b, h, 0, 0))
    ins = [z, z, z, lb.reshape(1, -1), s0]
    specs = [blk(OD_HG_BLK), blk(OD_HG_BLK + f_off), blk(OD_HG_BLK + 3 * HG_HEADS),
             pl.BlockSpec((1, wide), lambda b, h, i: (0, h)), state]
    if finalize:
        ins += [prev, norm_w.reshape(1, -1)]
        specs += [blk(0), pl.BlockSpec((1, HG_DV), lambda b, h, i: (0, 0))]
    out_shape = [jax.ShapeDtypeStruct((batch * t, HG_WIDTH), F32)]
    out_specs = [blk(0)]
    if emit_state:
        out_shape.append(jax.ShapeDtypeStruct((batch, HG_HEADS, HG_DV, HG_DK), F32))
        out_specs.append(state)
    return pl.pallas_call(
        functools.partial(_hg_kernel, reverse=reverse, finalize=finalize, emit_state=emit_state),
        out_shape=out_shape,
        grid=(batch, HG_HEADS // nh, nblk),
        in_specs=specs,
        out_specs=out_specs,
        scratch_shapes=[pltpu.VMEM((nh, HG_DV, HG_DK), F32)],
        compiler_params=_cparams(3),
        name="hgrn2_" + ("bwd" if reverse else "fwd"),
    )(*ins)


def _even_layer(x, xc, mod, norm_w, w_in, w_out, rpb, rw_params, batch, t, l):
    s2 = 3 * NA_WIDTH + RW_SHIFT_COLS
    width = RW_PAD_COLS - RW_SHIFT_COLS
    n_gate, n_rest = GATE_COLS // width, s2 // width
    plan = [n_rest + i for i in range(n_gate)] + list(range(n_rest)) + [None]
    w_ext = cast_reorder(w_in, plan, width, "even_w_in")
    tiles_per_seq = max(t // 1024, 1)
    z = in_projection(x, norm_w, mod, lambda i: i // tiles_per_seq, w_ext, "even_in_proj")
    z_c = in_projection(xc, norm_w, mod, lambda i: batch, w_ext, "even_in_proj_ctx")

    y_na = na_attention(z, z_c, rpb, batch, t, l)
    yc_na = ctx_attention(z_c, batch, l)

    mu, w0, w2, a0, a2, k_k, k_a, r_k, ln_w, ln_b = rw_params
    mu = jnp.pad(mu, ((0, 0), (0, RW_PAD_COLS - RW_SHIFT_COLS)))
    zpad = jnp.zeros((RW_RANK, RW_WIDTH), F32)
    w2p = jnp.stack([jnp.concatenate([w2[0], zpad]), jnp.concatenate([zpad, w2[1]])])
    a2p = jnp.stack([jnp.concatenate([a2[0], zpad]), jnp.concatenate([zpad, a2[1]])])
    vecs = (mu, w0, w2p, a0, a2p, k_k.reshape(1, -1), k_a.reshape(1, -1), r_k.reshape(1, -1))
    terms = rwkv_terms(z, batch, t, *vecs)
    terms_c = rwkv_terms(z_c, batch, l, *vecs)
    s_zero = jnp.zeros((RW_HEAD_DIM, RW_HEAD_DIM, LANES), F32)
    yc_f, yc_b, s_ctx = rwkv_scan(*terms_c[:5], s_zero)
    y_f, y_b, _ = rwkv_scan(*terms[:5], s_ctx)
    y_rw = rwkv_readout(y_f, y_b, terms[5], ln_w, ln_b).reshape(batch * t, RW_WIDTH)
    yc_rw = rwkv_readout(yc_f, yc_b, terms_c[5], ln_w, ln_b).reshape(batch * l, RW_WIDTH)

    w_out = w_out.astype(BF16)
    rows_per_seq = max(t // 256, 1)
    x = out_projection(y_na, y_rw, z, w_out, x, mod, lambda i: i // rows_per_seq, "even_out_proj")
    xc = out_projection(yc_na, yc_rw, z_c, w_out, xc, mod, lambda i: batch, "even_out_proj_ctx")
    return x, xc


def _odd_layer(x, xc, mod, norm_w, w_in, w_out, q_norm, w_uq, kv_norm, w_ukv, lb, hg_norm_w, final_w,
               batch, t, l):
    o3 = MLA_Q_RANK + MLA_KV_RANK + MLA_ROPE
    o4 = o3 + 4 * HG_FDIM
    zero = jnp.zeros((x.shape[1], MLA_A_COLS - o3), BF16)
    w_in = w_in.astype(BF16)
    w_ext = jnp.concatenate([w_in[:, o4:], w_in[:, :o3], zero, w_in[:, o3:o4]], axis=1)
    tiles_per_seq = max(t // 1024, 1)
    z = in_projection(x, norm_w, mod, lambda i: i // tiles_per_seq, w_ext, "odd_in_proj")
    z_c = in_projection(xc, norm_w, mod, lambda i: batch, w_ext, "odd_in_proj_ctx")

    cos, sin = _rope_tables(t)
    qk = MLA_NOPE + MLA_ROPE
    w_uq_pad = jnp.pad(w_uq.reshape(MLA_Q_RANK, MLA_HEADS, qk), ((0, 0), (0, 0), (0, MLA_QK_PAD - qk)))
    w_uq_pad = w_uq_pad.reshape(MLA_Q_RANK, MLA_HEADS * MLA_QK_PAD).astype(BF16)
    w_ukv = w_ukv.astype(BF16)
    q = mla_queries(z, q_norm, w_uq_pad, cos, sin, t)
    k, v = mla_keys_values(z, kv_norm, w_ukv, cos, sin, t)
    ones = jnp.ones((l, LANES), F32)
    kc, vc = mla_keys_values(z_c, kv_norm, w_ukv, ones, jnp.zeros_like(ones), l)
    y_mla = mla_attention(q, k, v, kc, vc, batch, t, l)

    s_zero = jnp.zeros((batch, HG_HEADS, HG_DV, HG_DK), F32)
    _, s_f = hgrn2_pass(z_c, lb, s_zero, batch, l, HG_HEADS, False, emit_state=True)
    _, s_b = hgrn2_pass(z_c, lb, s_zero, batch, l, 2 * HG_HEADS, True, emit_state=True)
    o_f, = hgrn2_pass(z, lb, s_f, batch, t, HG_HEADS, False)
    y_hg, = hgrn2_pass(z, lb, s_b, batch, t, 2 * HG_HEADS, True, prev=o_f, norm_w=hg_norm_w)

    rows_per_seq = max(t // 256, 1)
    return out_projection(y_mla, y_hg, z, w_out.astype(BF16), x, mod, lambda i: i // rows_per_seq,
                          "odd_out_proj", final_w=final_w)


def kernel(x, c, ctx, c_ctx, ada_w, ada_b, norm_w, e_w_in, e_w_out, na_rpb, rw_mu, rw_w0, rw_w2, rw_a0, rw_a2, rw_k_k, rw_k_a, rw_r_k, rw_ln_w, rw_ln_b, o_w_in, o_w_out, mla_q_norm, mla_w_uq, mla_kv_norm, mla_w_ukv, hg_lower_bounds, hg_norm_w, final_norm_w):
    batch, t, d = x.shape
    l = ctx.shape[1]
    assert ada_w.shape[0] == 2, "one even and one odd layer"
    cond = jnp.concatenate([c, c_ctx[None, :], jnp.zeros((8 - batch - 1, d), F32)], axis=0)
    mod = modulation(cond, ada_w, ada_b)
    s = jax.nn.softmax(hg_lower_bounds.astype(F32), axis=0)
    lower = jnp.cumsum(s, axis=0) - s[0]

    xf, xcf = x.reshape(batch * t, d), ctx.reshape(batch * l, d)
    rw_params = (rw_mu[0], rw_w0[0], rw_w2[0], rw_a0[0], rw_a2[0], rw_k_k[0], rw_k_a[0], rw_r_k[0],
                 rw_ln_w[0], rw_ln_b[0])
    xf, xcf = _even_layer(xf, xcf, mod[0].reshape(8, 1, 3 * d), norm_w[0], e_w_in[0], e_w_out[0], na_rpb[0],
                          rw_params, batch, t, l)
    out = _odd_layer(xf, xcf, mod[1].reshape(8, 1, 3 * d), norm_w[1], o_w_in[0], o_w_out[0], mla_q_norm[0],
                     mla_w_uq[0], mla_kv_norm[0], mla_w_ukv[0], lower[1], hg_norm_w[0], final_norm_w,
                     batch, t, l)
    return out.reshape(batch, t, d)
```

```python
import functools
import math

import numpy as np
import jax
import jax.numpy as jnp
from jax import lax
from jax.experimental import pallas as pl
from jax.experimental.pallas import tpu as pltpu

F32 = jnp.float32
BF16 = jnp.bfloat16

GRID_W = 64
NORM_EPS = 1e-6
ROPE_BASE = 10000.0

NA_HEADS = 8
NA_HEAD_DIM = 128
NA_WIDTH = NA_HEADS * NA_HEAD_DIM
NA_WIN_ROWS = 8
NA_WIN_COLS = 16
NA_QROWS = 4
NA_KROWS = 12
NA_STEP_HEADS = 4

RW_HEAD_DIM = 64
RW_HEADS = 16
RW_WIDTH = RW_HEADS * RW_HEAD_DIM
RW_RANK = 64
RW_SHIFT_COLS = 3 * RW_WIDTH + 4 * RW_RANK
RW_PAD_COLS = 3584
RW_GN_EPS = 64e-5
RW_SCAN_CHUNK = 64
RW_DECAY_SCALE = math.exp(-0.5)

MLA_HEADS = 8
MLA_Q_RANK = 512
MLA_KV_RANK = 512
MLA_NOPE = 128
MLA_ROPE = 64
MLA_V = 128
MLA_QK_PAD = 256
MLA_WIDTH = MLA_HEADS * MLA_V
MLA_A_COLS = 1536
MLA_KEY_CHUNK = 1024

HG_HEADS = 8
HG_DK = 128
HG_DV = 128
HG_FDIM = HG_HEADS * HG_DK
HG_WIDTH = HG_HEADS * HG_DV
HG_CHUNK = 32
HG_BLOCK = 256
HG_STEP_HEADS = 4

LANES = 128
OUT_K_CHUNK = 512

GATE_COLS = 2048
EV_Q_BLK = GATE_COLS // LANES
EV_RW_BLK = EV_Q_BLK + 3 * NA_WIDTH // LANES
OD_A_BLK = GATE_COLS // LANES
OD_HG_BLK = OD_A_BLK + MLA_A_COLS // LANES
VMEM_LIMIT = 48 * 1024 * 1024
NEG_INF = -1e30
LOG2E = math.log2(math.e)


def _cparams(n_axes):
    return pltpu.CompilerParams(dimension_semantics=("arbitrary",) * n_axes, vmem_limit_bytes=VMEM_LIMIT)


def _sigmoid(x):
    return 1.0 / (1.0 + jnp.exp(-x))


def _silu(x):
    return x * _sigmoid(x)


def _dot(a, b):
    return jnp.dot(a.astype(BF16), b.astype(BF16), preferred_element_type=F32)


def _dot_nt(a, b):
    return lax.dot_general(a.astype(BF16), b.astype(BF16), (((1,), (1,)), ((), ())), preferred_element_type=F32)


def _dot_tn(a, b):
    return lax.dot_general(a.astype(BF16), b.astype(BF16), (((0,), (0,)), ((), ())), preferred_element_type=F32)


def _mod_kernel(c_ref, w_ref, b_ref, o_ref):
    o_ref[...] = _dot(_silu(c_ref[...]), w_ref[...]) + b_ref[...]


def modulation(cond, ada_w, ada_b, tn=512):
    depth, d, n = ada_w.shape
    return pl.pallas_call(
        _mod_kernel,
        out_shape=jax.ShapeDtypeStruct((depth, 8, n), F32),
        grid=(depth, n // tn),
        in_specs=[pl.BlockSpec((8, d), lambda l, j: (0, 0)),
                  pl.BlockSpec((None, d, tn), lambda l, j: (l, 0, j)),
                  pl.BlockSpec((None, 1, tn), lambda l, j: (l, 0, j))],
        out_specs=pl.BlockSpec((None, 8, tn), lambda l, j: (l, 0, j)),
        compiler_params=_cparams(2),
        name="adaln_modulation",
    )(cond, ada_w, ada_b.reshape(depth, 1, n))


def _cast_reorder_kernel(src_ref, zero_ref, x_ref, o_ref):
    del src_ref
    o_ref[...] = jnp.where(zero_ref[pl.program_id(0)] != 0, 0.0, x_ref[...]).astype(BF16)


def cast_reorder(w, plan, width, name):
    d = w.shape[0]
    src = np.array([0 if a is None else a for a in plan], np.int32)
    zero = np.array([a is None for a in plan], np.int32)
    return pl.pallas_call(
        _cast_reorder_kernel,
        out_shape=jax.ShapeDtypeStruct((d, width * len(plan)), BF16),
        grid_spec=pltpu.PrefetchScalarGridSpec(
            num_scalar_prefetch=2,
            grid=(len(plan),),
            in_specs=[pl.BlockSpec((d, width), lambda j, src, zero: (0, src[j]))],
            out_specs=pl.BlockSpec((d, width), lambda j, src, zero: (0, j))),
        compiler_params=_cparams(1),
        name=name,
    )(jnp.asarray(src), jnp.asarray(zero), w)


def _inproj_kernel(x_ref, nw_ref, shift_ref, scale_ref, w_ref, o_ref, h_ref):
    @pl.when(pl.program_id(1) == 0)
    def _():
        x = x_ref[...]
        y = x * lax.rsqrt(jnp.mean(x * x, axis=-1, keepdims=True) + NORM_EPS) * nw_ref[...]
        h_ref[...] = (y * (1.0 + scale_ref[...]) + shift_ref[...]).astype(BF16)

    o_ref[...] = jnp.dot(h_ref[...], w_ref[...], preferred_element_type=F32)


def in_projection(x, norm_w, mod, mod_row, w, name, tm=1024, tn=512):
    m, d = x.shape
    n = w.shape[1]
    tm = min(tm, m)
    return pl.pallas_call(
        _inproj_kernel,
        out_shape=jax.ShapeDtypeStruct((m, n), F32),
        grid=(m // tm, n // tn),
        in_specs=[pl.BlockSpec((tm, d), lambda i, j: (i, 0)),
                  pl.BlockSpec((1, d), lambda i, j: (0, 0)),
                  pl.BlockSpec((None, 1, d), lambda i, j: (mod_row(i), 0, 0)),
                  pl.BlockSpec((None, 1, d), lambda i, j: (mod_row(i), 0, 1)),
                  pl.BlockSpec((d, tn), lambda i, j: (0, j))],
        out_specs=pl.BlockSpec((tm, tn), lambda i, j: (i, j)),
        scratch_shapes=[pltpu.VMEM((tm, d), BF16)],
        compiler_params=_cparams(2),
        name=name,
    )(x, norm_w.reshape(1, d), mod, mod, w)


def _outproj_kernel(ya_ref, yb_ref, g_ref, w_ref, x_ref, gm_ref, *rest, ka, final):
    o_ref = rest[-1]
    acc = None
    for y_ref, base in ((ya_ref, 0), (yb_ref, ka)):
        for lo in range(0, y_ref.shape[1], OUT_K_CHUNK):
            cols = slice(base + lo, base + lo + OUT_K_CHUNK)
            part = _dot(y_ref[:, lo:lo + OUT_K_CHUNK].astype(F32) * _silu(g_ref[:, cols]), w_ref[cols, :])
            acc = part if acc is None else acc + part
    out = x_ref[...] + gm_ref[...] * acc
    if final:
        out = out * lax.rsqrt(jnp.mean(out * out, axis=-1, keepdims=True) + NORM_EPS) * rest[0][...]
    o_ref[...] = out


def out_projection(ya, yb, gate, w, x, mod, mod_row, name, final_w=None, tm=256):
    m, d = x.shape
    ka, kb = ya.shape[1], yb.shape[1]
    tm = min(tm, m)
    ins = [ya, yb, gate, w, x, mod]
    specs = [pl.BlockSpec((tm, ka), lambda i: (i, 0)),
             pl.BlockSpec((tm, kb), lambda i: (i, 0)),
             pl.BlockSpec((tm, ka + kb), lambda i: (i, 0)),
             pl.BlockSpec((ka + kb, d), lambda i: (0, 0)),
             pl.BlockSpec((tm, d), lambda i: (i, 0)),
             pl.BlockSpec((None, 1, d), lambda i: (mod_row(i), 0, 2))]
    if final_w is not None:
        ins.append(final_w.reshape(1, d))
        specs.append(pl.BlockSpec((1, d), lambda i: (0, 0)))
    return pl.pallas_call(
        functools.partial(_outproj_kernel, ka=ka, final=final_w is not None),
        out_shape=jax.ShapeDtypeStruct((m, d), F32),
        grid=(m // tm,),
        in_specs=specs,
        out_specs=pl.BlockSpec((tm, d), lambda i: (i, 0)),
        compiler_params=_cparams(1),
        name=name,
    )(*ins)


def _softmax_pv(q, parts, scale):
    c = scale * LOG2E
    m = denom = num = None
    for k, bias, v in parts:
        x = _dot_nt(q, k) * c
        if bias is not None:
            x = x + bias
        m_part = jnp.max(x, axis=-1, keepdims=True)
        if m is None:
            m = m_part
            p = jnp.exp2(x - m)
            denom = jnp.sum(p, axis=-1, keepdims=True)
            num = _dot(p, v)
        else:
            m_new = jnp.maximum(m, m_part)
            alpha = jnp.exp2(m - m_new)
            p = jnp.exp2(x - m_new)
            denom = denom * alpha + jnp.sum(p, axis=-1, keepdims=True)
            num = num * alpha + _dot(p, v)
            m = m_new
    return num / denom


def _softmax_pv_joint(q, parts, scale):
    c = scale * LOG2E
    logits = [_dot_nt(q, k) * c if bias is None else _dot_nt(q, k) * c + bias for k, bias, _ in parts]
    m = functools.reduce(jnp.maximum, [jnp.max(x, axis=-1, keepdims=True) for x in logits])
    ps = [jnp.exp2(x - m) for x in logits]
    denom = functools.reduce(lambda a, b: a + b, [jnp.sum(p, axis=-1, keepdims=True) for p in ps])
    num = functools.reduce(lambda a, b: a + b, [_dot(p, v) for p, (_, _, v) in zip(ps, parts)])
    return num / denom


def _na_kernel(idx_ref, q_ref, k_ref, v_ref, kc_ref, vc_ref, tab_ref, o_ref, bias_ref, *, rows, scale):
    j = pl.program_id(2)
    npair = NA_KROWS // 2
    start = pl.multiple_of(jnp.clip(NA_QROWS * j - NA_WIN_ROWS // 2, 0, rows - NA_KROWS) * GRID_W, GRID_W)
    win = pl.ds(start, NA_KROWS * GRID_W)
    for hd in range(NA_STEP_HEADS):
        sl = slice(hd * LANES, (hd + 1) * LANES)
        for qr in range(NA_QROWS):
            for m in range(npair):
                tile = tab_ref[hd, idx_ref[(j * NA_QROWS + qr) * npair + m]]
                bias_ref[hd, qr * GRID_W:(qr + 1) * GRID_W, m * LANES:(m + 1) * LANES] = tile
        parts = [(k_ref[win, sl], bias_ref[hd], v_ref[win, sl]), (kc_ref[:, sl], None, vc_ref[:, sl])]
        o_ref[:, sl] = _softmax_pv_joint(q_ref[:, sl], parts, scale)


def _na_tile_ids(rows):
    kh = min(NA_WIN_ROWS, rows)
    n = 2 * NA_WIN_ROWS - 1
    ids = []
    for j in range(rows // NA_QROWS):
        ks = int(np.clip(NA_QROWS * j - NA_WIN_ROWS // 2, 0, rows - NA_KROWS))
        for qr in range(NA_QROWS):
            r = NA_QROWS * j + qr
            r0 = int(np.clip(r - kh // 2, 0, rows - kh))
            for m in range(NA_KROWS // 2):
                kr = ks + 2 * m
                dy = kr - r + NA_WIN_ROWS - 1
                first, second = r0 <= kr < r0 + kh, r0 <= kr + 1 < r0 + kh
                ids.append(dy if first and second else n + dy if first else 2 * n + dy + 1 if second else 3 * n)
    return np.array(ids, np.int32)


def _na_bias_tiles(rpb):
    col = np.arange(GRID_W)
    cs = np.clip(col - NA_WIN_COLS // 2, 0, GRID_W - NA_WIN_COLS)
    kc = col[None, :]
    in_win = (kc >= cs[:, None]) & (kc < cs[:, None] + NA_WIN_COLS)
    dx = np.where(in_win, kc - col[:, None] + NA_WIN_COLS - 1, 0)
    pick = jnp.asarray(dx[:, :, None] == np.arange(rpb.shape[-1]), F32)
    picked = jnp.einsum("hyj,cqj->hycq", rpb, pick, precision=lax.Precision.HIGHEST)
    tab = jnp.where(in_win[None, None], picked * LOG2E, NEG_INF)
    neg = jnp.full_like(tab[:, :1], NEG_INF)
    nxt = jnp.concatenate([tab[:, 1:], neg], axis=1)
    both = jnp.concatenate([tab, nxt], axis=-1)
    first = jnp.concatenate([tab, jnp.broadcast_to(neg, tab.shape)], axis=-1)
    second = jnp.concatenate([jnp.broadcast_to(neg, tab.shape), tab], axis=-1)
    return jnp.concatenate([both, first, second, jnp.concatenate([neg, neg], axis=-1)], axis=1)


def na_attention(z, z_c, rpb, batch, t, l):
    rows = t // GRID_W
    nj = rows // NA_QROWS
    tq = NA_QROWS * GRID_W
    ids = jnp.asarray(_na_tile_ids(rows))
    tiles = _na_bias_tiles(rpb)
    nh = NA_STEP_HEADS
    wide = nh * LANES
    q0, k0, v0 = (EV_Q_BLK + i * NA_HEADS for i in range(3))
    assert q0 % nh == 0 and NA_HEADS % nh == 0, "head groups must start on a whole column block"
    return pl.pallas_call(
        functools.partial(_na_kernel, rows=rows, scale=NA_HEAD_DIM ** -0.5),
        out_shape=jax.ShapeDtypeStruct((batch * t, NA_WIDTH), F32),
        grid_spec=pltpu.PrefetchScalarGridSpec(
            num_scalar_prefetch=1,
            grid=(NA_HEADS // nh, batch, nj),
            in_specs=[pl.BlockSpec((tq, wide), lambda h, b, j, g: (b * nj + j, q0 // nh + h)),
                      pl.BlockSpec((t, wide), lambda h, b, j, g: (b, k0 // nh + h)),
                      pl.BlockSpec((t, wide), lambda h, b, j, g: (b, v0 // nh + h)),
                      pl.BlockSpec((l, wide), lambda h, b, j, g: (b, k0 // nh + h)),
                      pl.BlockSpec((l, wide), lambda h, b, j, g: (b, v0 // nh + h)),
                      pl.BlockSpec((nh,) + tiles.shape[1:], lambda h, b, j, g: (h, 0, 0, 0))],
            out_specs=pl.BlockSpec((tq, wide), lambda h, b, j, g: (b * nj + j, h)),
            scratch_shapes=[pltpu.VMEM((nh, tq, NA_KROWS * GRID_W), F32)]),
        compiler_params=_cparams(3),
        name="na_attention",
    )(ids, z, z, z, z_c, z_c, tiles)


def _dense_attn_kernel(q_ref, k_ref, v_ref, o_ref, *, scale):
    o_ref[...] = _softmax_pv(q_ref[...], [(k_ref[...], None, v_ref[...])], scale)


def ctx_attention(z_c, batch, l):
    h8 = NA_HEADS
    return pl.pallas_call(
        functools.partial(_dense_attn_kernel, scale=NA_HEAD_DIM ** -0.5),
        out_shape=jax.ShapeDtypeStruct((batch * l, NA_WIDTH), F32),
        grid=(batch, h8),
        in_specs=[pl.BlockSpec((l, LANES), lambda b, h: (b, EV_Q_BLK + h)),
                  pl.BlockSpec((l, LANES), lambda b, h: (b, EV_Q_BLK + h8 + h)),
                  pl.BlockSpec((l, LANES), lambda b, h: (b, EV_Q_BLK + 2 * h8 + h))],
        out_specs=pl.BlockSpec((l, LANES), lambda b, h: (b, h)),
        compiler_params=_cparams(2),
        name="ctx_attention",
    )(z_c, z_c, z_c)


def _head_sum(x):
    first = lax.broadcasted_iota(jnp.int32, x.shape, 1) < RW_HEAD_DIM
    s0 = jnp.sum(jnp.where(first, x, 0.0), axis=-1, keepdims=True)
    s1 = jnp.sum(jnp.where(first, 0.0, x), axis=-1, keepdims=True)
    return jnp.where(first, s0, s1)


def _rw_terms_kernel(ur_ref, uk_ref, uv_ref, uwa_ref, pr_ref, pk_ref, pv_ref, pwa_ref, nr_ref, nk_ref, nv_ref,
                     nwa_ref, mu_ref, w0_ref, w2_ref, a0_ref, a2_ref, kk_ref, ka_ref, rk_ref,
                     rv_o, wk0_o, wk1_o, bk0_o, bk1_o, bonus_o, *, nt):
    i = pl.program_id(1)
    tm = ur_ref.shape[0]

    def shifted(u_ref, p_ref, n_ref, col0):
        u = u_ref[...]
        mu = mu_ref[:, col0:col0 + u.shape[1]]
        row = lax.broadcasted_iota(jnp.int32, u.shape, 0)
        before = jnp.where(i == 0, 0.0, p_ref[7:8, :])
        after = jnp.where(i == nt - 1, 0.0, n_ref[0:1, :])
        prev = jnp.where(row == 0, before, pltpu.roll(u, 1, axis=0))
        nxt = jnp.where(row == tm - 1, after, pltpu.roll(u, tm - 1, axis=0))
        return u + mu[0:1] * (prev - u) + mu[1:2] * (nxt - u)

    r_all = shifted(ur_ref, pr_ref, nr_ref, 0)
    k_all = shifted(uk_ref, pk_ref, nk_ref, RW_WIDTH)
    v_all = shifted(uv_ref, pv_ref, nv_ref, 2 * RW_WIDTH)
    wa = shifted(uwa_ref, pwa_ref, nwa_ref, 3 * RW_WIDTH)
    wd = jnp.tanh(wa[:, :LANES]).astype(BF16)
    ad = wa[:, LANES:].astype(BF16)
    first = lax.broadcasted_iota(jnp.int32, (tm, LANES), 1) < RW_HEAD_DIM

    def head_rows(x, y):
        return (jnp.where(first, x, pltpu.roll(y, RW_HEAD_DIM, axis=1)),
                jnp.where(first, pltpu.roll(x, RW_HEAD_DIM, axis=1), y))

    outs = (rv_o, wk0_o, wk1_o, bk0_o, bk1_o)
    even = [[] for _ in outs]
    odd = [[] for _ in outs]
    bonus = []
    for p in range(RW_WIDTH // LANES):
        sl = slice(p * LANES, (p + 1) * LANES)
        r, k, v = r_all[:, sl], k_all[:, sl], v_all[:, sl]
        kk = k * kk_ref[:, sl]
        kk = kk / jnp.maximum(jnp.sqrt(_head_sum(kk * kk)), 1e-12)
        kd_sum = jnp.zeros_like(k)
        per_dir = []
        for d in range(2):
            w = w0_ref[d:d + 1, sl] + jnp.dot(wd, w2_ref[d, :, sl].astype(BF16), preferred_element_type=F32)
            a = _sigmoid(a0_ref[d:d + 1, sl] + jnp.dot(ad, a2_ref[d, :, sl].astype(BF16), preferred_element_type=F32))
            kd = k * (1.0 + (a - 1.0) * ka_ref[:, sl])
            dec = jnp.exp(-RW_DECAY_SCALE * _sigmoid(w))
            per_dir.append(((dec, kd), (kk * a, kk)))
            kd_sum = kd_sum + kd
        pairs = [(r, v), per_dir[0][0], per_dir[1][0], per_dir[0][1], per_dir[1][1]]
        bonus.append(_head_sum(r * kd_sum * rk_ref[:, sl]) * v)
        for ev, od, (x, y) in zip(even, odd, pairs):
            e, o = head_rows(x, y)
            ev.append(e)
            od.append(o)
    for o_ref, ev, od in zip(outs, even, odd):
        o_ref[...] = jnp.swapaxes(jnp.stack(ev + od, axis=0), 0, 1)
    bonus_o[...] = jnp.swapaxes(jnp.stack(bonus, axis=0), 0, 1)


def rwkv_terms(z, n_seq, t, mu, w0, w2p, a0, a2p, k_k, k_a, r_k, tm=256):
    tm = min(tm, t)
    nt = t // tm
    npair = RW_WIDTH // LANES
    wide0 = EV_RW_BLK * LANES // RW_WIDTH
    wa_blk = (EV_RW_BLK + 3 * npair) // 2
    n8 = tm // 8

    def cur(width, cb):
        return pl.BlockSpec((tm, width), lambda b, i: (b * nt + i, cb))

    def before(width, cb):
        return pl.BlockSpec((8, width), lambda b, i: (jnp.maximum((b * nt + i) * n8 - 1, 0), cb))

    def after(width, cb):
        return pl.BlockSpec((8, width), lambda b, i: (jnp.minimum((b * nt + i + 1) * n8, n_seq * nt * n8 - 1), cb))

    cols = [(RW_WIDTH, wide0), (RW_WIDTH, wide0 + 1), (RW_WIDTH, wide0 + 2), (2 * LANES, wa_blk)]
    whole = lambda a: pl.BlockSpec(a.shape, lambda b, i: (0,) * a.ndim)
    params = (mu, w0, w2p, a0, a2p, k_k, k_a, r_k)
    out = pl.BlockSpec((tm, npair, LANES), lambda b, i: (i, b, 0))
    out2 = pl.BlockSpec((tm, 2 * npair, LANES), lambda b, i: (i, b, 0))
    return pl.pallas_call(
        functools.partial(_rw_terms_kernel, nt=nt),
        out_shape=[jax.ShapeDtypeStruct((t, n_seq * 2 * npair, LANES), F32)] * 5
        + [jax.ShapeDtypeStruct((t, n_seq * npair, LANES), F32)],
        grid=(n_seq, nt),
        in_specs=([cur(*c) for c in cols] + [before(*c) for c in cols] + [after(*c) for c in cols]
                  + [whole(a) for a in params]),
        out_specs=[out2] * 5 + [out],
        compiler_params=_cparams(2),
        name="rwkv_terms",
    )(*([z] * 12), *params)


def _to_lanes(n):
    q = jnp.concatenate([n, pltpu.roll(n, RW_HEAD_DIM, axis=1)], axis=0)
    return q.T[:RW_HEAD_DIM]


def _rw_scan_kernel(rvf, rvb, wk0, wk1, bk0, bk1, s0_ref, yf_ref, yb_ref, sfin_ref,
                    s_ref, z_ref, znext_ref, g_ref, ybuf_ref, *, chunk):
    n = RW_HEAD_DIM
    nb = rvf.shape[1] // (2 * RW_WIDTH // LANES)
    R, K, V, A, B = range(5)

    @pl.when(pl.program_id(0) == 0)
    def _():
        s_ref[...] = s0_ref[...]

    def convert(t_src):
        def pair_tiles(f, bw):
            both = (f[t_src], bw[chunk - 1 - t_src])
            x = jnp.concatenate([src[(b * 2 + par) * 8:(b * 2 + par) * 8 + 8]
                                 for par in range(2) for src in both for b in range(nb)], axis=0)
            xt = x.T
            return xt[:n], xt[n:]

        r, v = pair_tiles(rvf, rvb)
        w, k = pair_tiles(wk0, wk1)
        b, kk = pair_tiles(bk0, bk1)
        g_prev = g_ref[0]
        g = g_prev * w
        g_ref[0] = g
        inv = 1.0 / g
        znext_ref[R] = r * g
        znext_ref[K] = k * inv
        znext_ref[V] = v
        znext_ref[A] = -(kk * g_prev)
        znext_ref[B] = b * inv

    def emit_y(t):
        rows = _to_lanes(ybuf_ref[...])
        yf_ref[t] = rows[:n // 2]
        yb_ref[chunk - 1 - t] = rows[n // 2:]

    g_ref[0] = jnp.ones((n, LANES), F32)
    ybuf_ref[...] = jnp.zeros((n, LANES), F32)
    convert(0)

    def step(t, carry):
        emit_y(jnp.maximum(t - 1, 0))
        z_ref[...] = znext_ref[...]
        g_ref[1] = g_ref[0]
        convert(jnp.minimum(t + 1, chunk - 1))
        sa = [jnp.zeros((n, LANES), F32), jnp.zeros((n, LANES), F32)]
        for k in range(n):
            sa[k % 2] = sa[k % 2] + s_ref[k] * z_ref[A, k:k + 1, :]
        sa = sa[0] + sa[1]
        vt = z_ref[V]
        y = [jnp.zeros((n, LANES), F32), jnp.zeros((n, LANES), F32)]
        for k in range(n):
            sk = s_ref[k] + sa * z_ref[B, k:k + 1, :] + vt * z_ref[K, k:k + 1, :]
            s_ref[k] = sk
            y[k % 2] = y[k % 2] + sk * z_ref[R, k:k + 1, :]
        ybuf_ref[...] = y[0] + y[1]
        return carry

    lax.fori_loop(0, chunk, step, 0)
    emit_y(chunk - 1)
    for k in range(n):
        s_ref[k] = s_ref[k] * g_ref[1, k:k + 1, :]
    sfin_ref[...] = s_ref[...]


def rwkv_scan(rv, wk0, wk1, bk0, bk1, s0):
    t, nrow, _ = rv.shape
    n = RW_HEAD_DIM
    chunk = RW_SCAN_CHUNK
    nc = t // chunk
    assert nrow == LANES // 2, "the states of both directions fill the 128 lanes"
    fwd = pl.BlockSpec((chunk, nrow, LANES), lambda i: (i, 0, 0))
    bwd = pl.BlockSpec((chunk, nrow, LANES), lambda i: (nc - 1 - i, 0, 0))
    yfwd = pl.BlockSpec((chunk, nrow // 2, LANES), lambda i: (i, 0, 0))
    ybwd = pl.BlockSpec((chunk, nrow // 2, LANES), lambda i: (nc - 1 - i, 0, 0))
    state = pl.BlockSpec((n, n, LANES), lambda i: (0, 0, 0))
    slab = jax.ShapeDtypeStruct((t, nrow // 2, LANES), F32)
    return pl.pallas_call(
        functools.partial(_rw_scan_kernel, chunk=chunk),
        out_shape=[slab, slab, jax.ShapeDtypeStruct((n, n, LANES), F32)],
        grid=(nc,),
        in_specs=[fwd, bwd, fwd, bwd, fwd, bwd, state],
        out_specs=[yfwd, ybwd, state],
        scratch_shapes=[pltpu.VMEM((n, n, LANES), F32), pltpu.VMEM((5, n, LANES), F32),
                        pltpu.VMEM((5, n, LANES), F32), pltpu.VMEM((2, n, LANES), F32),
                        pltpu.VMEM((n, LANES), F32)],
        compiler_params=_cparams(1),
        name="rwkv_scan",
    )(rv, rv, wk0, wk1, bk0, bk1, s0)


def _rw_readout_kernel(yf_ref, yb_ref, bonus_ref, lw_ref, lb_ref, o_ref):
    npair = RW_WIDTH // LANES
    y_all = jnp.swapaxes(yf_ref[...] + yb_ref[...], 0, 1)
    bonus_all = jnp.swapaxes(bonus_ref[...], 0, 1)
    for q in range(yf_ref.shape[1]):
        b, p = divmod(q, npair)
        sl = slice(p * LANES, (p + 1) * LANES)
        y = y_all[q]
        mu = _head_sum(y) * (1.0 / RW_HEAD_DIM)
        d = y - mu
        var = _head_sum(d * d) * (1.0 / RW_HEAD_DIM)
        o_ref[b, :, sl] = d * lax.rsqrt(var + RW_GN_EPS) * lw_ref[:, sl] + lb_ref[:, sl] + bonus_all[q]


def rwkv_readout(y_f, y_b, bonus, ln_w, ln_b, tm=256):
    t, nq, _ = y_f.shape
    tm = min(tm, t)
    batch = nq // (RW_WIDTH // LANES)
    slab = pl.BlockSpec((tm, nq, LANES), lambda i: (i, 0, 0))
    vec = pl.BlockSpec((1, RW_WIDTH), lambda i: (0, 0))
    return pl.pallas_call(
        _rw_readout_kernel,
        out_shape=jax.ShapeDtypeStruct((batch, t, RW_WIDTH), F32),
        grid=(t // tm,),
        in_specs=[slab, slab, slab, vec, vec],
        out_specs=pl.BlockSpec((batch, tm, RW_WIDTH), lambda i: (0, i, 0)),
        compiler_params=_cparams(1),
        name="rwkv_readout",
    )(y_f, y_b, bonus, ln_w.reshape(1, -1), ln_b.reshape(1, -1))


def _rms(x, w):
    return x * lax.rsqrt(jnp.mean(x * x, axis=-1, keepdims=True) + NORM_EPS) * w


def _rope(x, cos, sin):
    odd = (lax.broadcasted_iota(jnp.int32, x.shape, 1) & 1) == 1
    swapped = jnp.where(odd, pltpu.roll(x, 1, axis=1), pltpu.roll(x, LANES - 1, axis=1))
    return x * cos + swapped * sin


def _mla_q_kernel(c_ref, nw_ref, w_ref, cos_ref, sin_ref, q_ref):
    h = _rms(c_ref[...], nw_ref[...]).astype(BF16)
    acc = jnp.dot(h, w_ref[...], preferred_element_type=F32)
    cos, sin = cos_ref[...], sin_ref[...]
    for hd in range(MLA_HEADS):
        lo = hd * MLA_QK_PAD
        q_ref[:, lo:lo + MLA_NOPE] = acc[:, lo:lo + MLA_NOPE].astype(BF16)
        q_ref[:, lo + MLA_NOPE:lo + MLA_QK_PAD] = _rope(acc[:, lo + MLA_NOPE:lo + MLA_QK_PAD], cos, sin).astype(BF16)


def mla_queries(z, q_norm, w_uq_pad, cos, sin, t, tm=512):
    m = z.shape[0]
    tm = min(tm, t)
    nt = t // tm
    width = MLA_HEADS * MLA_QK_PAD
    return pl.pallas_call(
        _mla_q_kernel,
        out_shape=jax.ShapeDtypeStruct((m, width), BF16),
        grid=(m // tm,),
        in_specs=[pl.BlockSpec((tm, MLA_Q_RANK), lambda i: (i, GATE_COLS // MLA_Q_RANK)),
                  pl.BlockSpec((1, MLA_Q_RANK), lambda i: (0, 0)),
                  pl.BlockSpec((MLA_Q_RANK, width), lambda i: (0, 0)),
                  pl.BlockSpec((tm, LANES), lambda i: (i % nt, 0)),
                  pl.BlockSpec((tm, LANES), lambda i: (i % nt, 0))],
        out_specs=pl.BlockSpec((tm, width), lambda i: (i, 0)),
        compiler_params=_cparams(1),
        name="mla_queries",
    )(z, q_norm.reshape(1, -1), w_uq_pad, cos, sin)


def _mla_kv_kernel(c_ref, pe_ref, nw_ref, w_ref, cos_ref, sin_ref, k_ref, v_ref):
    h = _rms(c_ref[...], nw_ref[...]).astype(BF16)
    acc = jnp.dot(h, w_ref[...], preferred_element_type=F32)
    pe = _rope(pe_ref[...], cos_ref[...], sin_ref[...]).astype(BF16)
    for hd in range(MLA_HEADS):
        src = hd * (MLA_NOPE + MLA_V)
        dst = hd * MLA_QK_PAD
        k_ref[:, dst:dst + MLA_NOPE] = acc[:, src:src + MLA_NOPE].astype(BF16)
        k_ref[:, dst + MLA_NOPE:dst + MLA_QK_PAD] = pe
        v_ref[:, hd * MLA_V:(hd + 1) * MLA_V] = acc[:, src + MLA_NOPE:src + MLA_NOPE + MLA_V].astype(BF16)


def mla_keys_values(z, kv_norm, w_ukv, cos, sin, t, tm=512):
    m = z.shape[0]
    tm = min(tm, t)
    nt = t // tm
    return pl.pallas_call(
        _mla_kv_kernel,
        out_shape=[jax.ShapeDtypeStruct((m, MLA_HEADS * MLA_QK_PAD), BF16),
                   jax.ShapeDtypeStruct((m, MLA_WIDTH), BF16)],
        grid=(m // tm,),
        in_specs=[pl.BlockSpec((tm, MLA_KV_RANK), lambda i: (i, (GATE_COLS + MLA_Q_RANK) // MLA_KV_RANK)),
                  pl.BlockSpec((tm, LANES), lambda i: (i, OD_A_BLK + (MLA_Q_RANK + MLA_KV_RANK) // LANES)),
                  pl.BlockSpec((1, MLA_KV_RANK), lambda i: (0, 0)),
                  pl.BlockSpec(w_ukv.shape, lambda i: (0, 0)),
                  pl.BlockSpec((tm, LANES), lambda i: (i % nt, 0)),
                  pl.BlockSpec((tm, LANES), lambda i: (i % nt, 0))],
        out_specs=[pl.BlockSpec((tm, MLA_HEADS * MLA_QK_PAD), lambda i: (i, 0)),
                   pl.BlockSpec((tm, MLA_WIDTH), lambda i: (i, 0))],
        compiler_params=_cparams(1),
        name="mla_keys_values",
    )(z, z, kv_norm.reshape(1, -1), w_ukv, cos, sin)


def _mla_attn_kernel(q_ref, k_ref, v_ref, kc_ref, vc_ref, o_ref, *, scale):
    parts = [(kc_ref[...], None, vc_ref[...])]
    nkeys = k_ref.shape[0]
    step = min(MLA_KEY_CHUNK, nkeys)
    for lo in range(0, nkeys, step):
        parts.append((k_ref[lo:lo + step, :], None, v_ref[lo:lo + step, :]))
    o_ref[...] = _softmax_pv(q_ref[...], parts, scale)


def mla_attention(q, k, v, kc, vc, batch, t, l, tq=2048):
    tq = min(tq, t)
    nq = t // tq
    return pl.pallas_call(
        functools.partial(_mla_attn_kernel, scale=(MLA_NOPE + MLA_ROPE) ** -0.5),
        out_shape=jax.ShapeDtypeStruct((batch * t, MLA_WIDTH), F32),
        grid=(batch, MLA_HEADS, nq),
        in_specs=[pl.BlockSpec((tq, MLA_QK_PAD), lambda b, h, i: (b * nq + i, h)),
                  pl.BlockSpec((t, MLA_QK_PAD), lambda b, h, i: (b, h)),
                  pl.BlockSpec((t, MLA_V), lambda b, h, i: (b, h)),
                  pl.BlockSpec((l, MLA_QK_PAD), lambda b, h, i: (b, h)),
                  pl.BlockSpec((l, MLA_V), lambda b, h, i: (b, h))],
        out_specs=pl.BlockSpec((tq, MLA_V), lambda b, h, i: (b * nq + i, h)),
        compiler_params=_cparams(3),
        name="mla_attention",
    )(q, k, v, kc, vc)


def _rope_tables(t):
    tok = np.arange(t)
    pos = np.stack([tok // GRID_W, tok % GRID_W], axis=-1).astype(np.float32)
    n_freq = MLA_ROPE // 4
    inv = (ROPE_BASE ** (-jnp.arange(n_freq, dtype=F32) / n_freq))
    ang = (jnp.asarray(pos)[:, :, None] * inv).reshape(t, MLA_ROPE // 2)
    cos = jnp.repeat(jnp.cos(ang), 2, axis=-1)
    sin = jnp.repeat(jnp.sin(ang), 2, axis=-1) * jnp.tile(jnp.array([-1.0, 1.0], F32), MLA_ROPE // 2)
    pad = ((0, 0), (0, LANES - MLA_ROPE))
    return jnp.pad(cos, pad), jnp.pad(sin, pad)


def _split3(x):
    hi = x.astype(BF16)
    r1 = x - hi.astype(F32)
    mid = r1.astype(BF16)
    lo = (r1 - mid.astype(F32)).astype(BF16)
    return hi, mid, lo


def _hg_kernel(q_ref, f_ref, v_ref, lb_ref, s0_ref, *rest, reverse, finalize, emit_state):
    st_ref = rest[-1]
    rest = rest[:-1]
    if finalize:
        prev_ref, nw_ref = rest[0], rest[1]
        rest = rest[2:]
    o_ref = rest[0]

    @pl.when(pl.program_id(2) == 0)
    def _():
        st_ref[...] = s0_ref[...]

    bt = q_ref.shape[0]
    nchunk = bt // HG_CHUNK
    order = range(nchunk - 1, -1, -1) if reverse else range(nchunk)
    ri = lax.broadcasted_iota(jnp.int32, (bt, bt), 0)
    ci = lax.broadcasted_iota(jnp.int32, (bt, bt), 1)
    same = (ri // HG_CHUNK) == (ci // HG_CHUNK)
    tri = same & ((ci >= ri) if reverse else (ci <= ri))
    tri_f = tri.astype(F32)
    tri_b = tri.astype(BF16)
    row_chunk = lax.broadcasted_iota(jnp.int32, (bt, HG_DK), 0) // HG_CHUNK
    in_chunk = [row_chunk == c for c in range(nchunk)]

    lb = lb_ref[...]
    fz = f_ref[...]
    log_sig = jnp.minimum(fz, 0.0) - jnp.log1p(jnp.exp(-jnp.abs(fz)))
    la = jnp.log(lb)
    lbb = jnp.log1p(-lb) + log_sig
    log_f = jnp.maximum(la, lbb) + jnp.log1p(jnp.exp(-jnp.abs(la - lbb)))
    kf_all = (1.0 - lb) * _sigmoid(-fz)
    pieces = jnp.dot(tri_b, jnp.concatenate(_split3(log_f), axis=1), preferred_element_type=F32)
    width = log_f.shape[1]
    cum_all = pieces[:, :width] + pieces[:, width:2 * width] + pieces[:, 2 * width:]

    for hh in range(HG_STEP_HEADS):
        sl = slice(hh * HG_DK, (hh + 1) * HG_DK)
        cum, kf, v = cum_all[:, sl], kf_all[:, sl], v_ref[:, sl]
        last = [cum[c * HG_CHUNK:c * HG_CHUNK + 1] if reverse else cum[(c + 1) * HG_CHUNK - 1:(c + 1) * HG_CHUNK]
                for c in range(nchunk)]
        tot = jnp.concatenate([jnp.broadcast_to(x, (HG_CHUNK, HG_DK)) for x in last], axis=0)
        q_in = q_ref[:, sl] * jnp.exp(cum)
        k_in = kf * jnp.exp(-cum)
        k_end = kf * jnp.exp(tot - cum)
        o_intra = _dot(_dot_nt(q_in, k_in) * tri_f, v)
        v_by_chunk = jnp.concatenate([jnp.where(in_chunk[c], v, 0.0) for c in range(nchunk)], axis=1)
        d_st = _dot_tn(v_by_chunk, k_end)
        st = st_ref[hh]
        before = [None] * nchunk
        for c in order:
            before[c] = st
            st = st * jnp.exp(last[c]) + d_st[c * HG_DV:(c + 1) * HG_DV]
        st_ref[hh] = st
        q_by_chunk = jnp.concatenate([jnp.where(in_chunk[c], q_in, 0.0) for c in range(nchunk)], axis=1)
        o = o_intra + _dot_nt(q_by_chunk, jnp.concatenate(before, axis=1))
        if finalize:
            o = o + prev_ref[:, sl]
            o = o * lax.rsqrt(jnp.mean(o * o, axis=-1, keepdims=True) + NORM_EPS) * nw_ref[...]
        o_ref[:, sl] = o
        if emit_state:
            rest[1][hh] = st


def hgrn2_pass(z, lb, s0, batch, t, f_off, reverse, prev=None, norm_w=None, emit_state=False):
    bt = min(HG_BLOCK, t)
    nblk = t // bt
    nh = HG_STEP_HEADS
    wide = nh * LANES
    finalize = prev is not None
    assert OD_HG_BLK % nh == 0 and HG_HEADS % nh == 0, "head groups must start on a whole column block"
    pos = (lambda i: nblk - 1 - i) if reverse else (lambda i: i)
    blk = lambda off: pl.BlockSpec((bt, wide), lambda b, h, i, off=off: (b * nblk + pos(i), off // nh + h))
    state = pl.BlockSpec((None, nh, HG_DV, HG_DK), lambda b, h, i: (b, h, 0, 0))
    ins = [z, z, z, lb.reshape(1, -1), s0]
    specs = [blk(OD_HG_BLK), blk(OD_HG_BLK + f_off), blk(OD_HG_BLK + 3 * HG_HEADS),
             pl.BlockSpec((1, wide), lambda b, h, i: (0, h)), state]
    if finalize:
        ins += [prev, norm_w.reshape(1, -1)]
        specs += [blk(0), pl.BlockSpec((1, HG_DV), lambda b, h, i: (0, 0))]
    out_shape = [jax.ShapeDtypeStruct((batch * t, HG_WIDTH), F32)]
    out_specs = [blk(0)]
    if emit_state:
        out_shape.append(jax.ShapeDtypeStruct((batch, HG_HEADS, HG_DV, HG_DK), F32))
        out_specs.append(state)
    return pl.pallas_call(
        functools.partial(_hg_kernel, reverse=reverse, finalize=finalize, emit_state=emit_state),
        out_shape=out_shape,
        grid=(batch, HG_HEADS // nh, nblk),
        in_specs=specs,
        out_specs=out_specs,
        scratch_shapes=[pltpu.VMEM((nh, HG_DV, HG_DK), F32)],
        compiler_params=_cparams(3),
        name="hgrn2_" + ("bwd" if reverse else "fwd"),
    )(*ins)


def _even_layer(x, xc, mod, norm_w, w_in, w_out, rpb, rw_params, batch, t, l):
    s2 = 3 * NA_WIDTH + RW_SHIFT_COLS
    width = RW_PAD_COLS - RW_SHIFT_COLS
    n_gate, n_rest = GATE_COLS // width, s2 // width
    plan = [n_rest + i for i in range(n_gate)] + list(range(n_rest)) + [None]
    w_ext = cast_reorder(w_in, plan, width, "even_w_in")
    tiles_per_seq = max(t // 1024, 1)
    z = in_projection(x, norm_w, mod, lambda i: i // tiles_per_seq, w_ext, "even_in_proj")
    z_c = in_projection(xc, norm_w, mod, lambda i: batch, w_ext, "even_in_proj_ctx")

    y_na = na_attention(z, z_c, rpb, batch, t, l)
    yc_na = ctx_attention(z_c, batch, l)

    mu, w0, w2, a0, a2, k_k, k_a, r_k, ln_w, ln_b = rw_params
    mu = jnp.pad(mu, ((0, 0), (0, RW_PAD_COLS - RW_SHIFT_COLS)))
    zpad = jnp.zeros((RW_RANK, RW_WIDTH), F32)
    w2p = jnp.stack([jnp.concatenate([w2[0], zpad]), jnp.concatenate([zpad, w2[1]])])
    a2p = jnp.stack([jnp.concatenate([a2[0], zpad]), jnp.concatenate([zpad, a2[1]])])
    vecs = (mu, w0, w2p, a0, a2p, k_k.reshape(1, -1), k_a.reshape(1, -1), r_k.reshape(1, -1))
    terms = rwkv_terms(z, batch, t, *vecs)
    terms_c = rwkv_terms(z_c, batch, l, *vecs)
    s_zero = jnp.zeros((RW_HEAD_DIM, RW_HEAD_DIM, LANES), F32)
    yc_f, yc_b, s_ctx = rwkv_scan(*terms_c[:5], s_zero)
    y_f, y_b, _ = rwkv_scan(*terms[:5], s_ctx)
    y_rw = rwkv_readout(y_f, y_b, terms[5], ln_w, ln_b).reshape(batch * t, RW_WIDTH)
    yc_rw = rwkv_readout(yc_f, yc_b, terms_c[5], ln_w, ln_b).reshape(batch * l, RW_WIDTH)

    w_out = w_out.astype(BF16)
    rows_per_seq = max(t // 256, 1)
    x = out_projection(y_na, y_rw, z, w_out, x, mod, lambda i: i // rows_per_seq, "even_out_proj")
    xc = out_projection(yc_na, yc_rw, z_c, w_out, xc, mod, lambda i: batch, "even_out_proj_ctx")
    return x, xc


def _odd_layer(x, xc, mod, norm_w, w_in, w_out, q_norm, w_uq, kv_norm, w_ukv, lb, hg_norm_w, final_w,
               batch, t, l):
    o3 = MLA_Q_RANK + MLA_KV_RANK + MLA_ROPE
    o4 = o3 + 4 * HG_FDIM
    zero = jnp.zeros((x.shape[1], MLA_A_COLS - o3), BF16)
    w_in = w_in.astype(BF16)
    w_ext = jnp.concatenate([w_in[:, o4:], w_in[:, :o3], zero, w_in[:, o3:o4]], axis=1)
    tiles_per_seq = max(t // 1024, 1)
    z = in_projection(x, norm_w, mod, lambda i: i // tiles_per_seq, w_ext, "odd_in_proj")
    z_c = in_projection(xc, norm_w, mod, lambda i: batch, w_ext, "odd_in_proj_ctx")

    cos, sin = _rope_tables(t)
    qk = MLA_NOPE + MLA_ROPE
    w_uq_pad = jnp.pad(w_uq.reshape(MLA_Q_RANK, MLA_HEADS, qk), ((0, 0), (0, 0), (0, MLA_QK_PAD - qk)))
    w_uq_pad = w_uq_pad.reshape(MLA_Q_RANK, MLA_HEADS * MLA_QK_PAD).astype(BF16)
    w_ukv = w_ukv.astype(BF16)
    q = mla_queries(z, q_norm, w_uq_pad, cos, sin, t)
    k, v = mla_keys_values(z, kv_norm, w_ukv, cos, sin, t)
    ones = jnp.ones((l, LANES), F32)
    kc, vc = mla_keys_values(z_c, kv_norm, w_ukv, ones, jnp.zeros_like(ones), l)
    y_mla = mla_attention(q, k, v, kc, vc, batch, t, l)

    s_zero = jnp.zeros((batch, HG_HEADS, HG_DV, HG_DK), F32)
    _, s_f = hgrn2_pass(z_c, lb, s_zero, batch, l, HG_HEADS, False, emit_state=True)
    _, s_b = hgrn2_pass(z_c, lb, s_zero, batch, l, 2 * HG_HEADS, True, emit_state=True)
    o_f, = hgrn2_pass(z, lb, s_f, batch, t, HG_HEADS, False)
    y_hg, = hgrn2_pass(z, lb, s_b, batch, t, 2 * HG_HEADS, True, prev=o_f, norm_w=hg_norm_w)

    rows_per_seq = max(t // 256, 1)
    return out_projection(y_mla, y_hg, z, w_out.astype(BF16), x, mod, lambda i: i // rows_per_seq,
                          "odd_out_proj", final_w=final_w)


def kernel(x, c, ctx, c_ctx, ada_w, ada_b, norm_w, e_w_in, e_w_out, na_rpb, rw_mu, rw_w0, rw_w2, rw_a0, rw_a2, rw_k_k, rw_k_a, rw_r_k, rw_ln_w, rw_ln_b, o_w_in, o_w_out, mla_q_norm, mla_w_uq, mla_kv_norm, mla_w_ukv, hg_lower_bounds, hg_norm_w, final_norm_w):
    batch, t, d = x.shape
    l = ctx.shape[1]
    assert ada_w.shape[0] == 2, "one even and one odd layer"
    cond = jnp.concatenate([c, c_ctx[None, :], jnp.zeros((8 - batch - 1, d), F32)], axis=0)
    mod = modulation(cond, ada_w, ada_b)
    s = jax.nn.softmax(hg_lower_bounds.astype(F32), axis=0)
    lower = jnp.cumsum(s, axis=0) - s[0]

    xf, xcf = x.reshape(batch * t, d), ctx.reshape(batch * l, d)
    rw_params = (rw_mu[0], rw_w0[0], rw_w2[0], rw_a0[0], rw_a2[0], rw_k_k[0], rw_k_a[0], rw_r_k[0],
                 rw_ln_w[0], rw_ln_b[0])
    xf, xcf = _even_layer(xf, xcf, mod[0].reshape(8, 1, 3 * d), norm_w[0], e_w_in[0], e_w_out[0], na_rpb[0],
                          rw_params, batch, t, l)
    out = _odd_layer(xf, xcf, mod[1].reshape(8, 1, 3 * d), norm_w[1], o_w_in[0], o_w_out[0], mla_q_norm[0],
                     mla_w_uq[0], mla_kv_norm[0], mla_w_ukv[0], lower[1], hg_norm_w[0], final_norm_w,
                     batch, t, l)
    return out.reshape(batch, t, d)
```

```python
import functools
import math

import numpy as np
import jax
import jax.numpy as jnp
from jax import lax
from jax.experimental import pallas as pl
from jax.experimental.pallas import tpu as pltpu

F32 = jnp.float32
BF16 = jnp.bfloat16

GRID_W = 64
NORM_EPS = 1e-6
ROPE_BASE = 10000.0

NA_HEADS = 8
NA_HEAD_DIM = 128
NA_WIDTH = NA_HEADS * NA_HEAD_DIM
NA_WIN_ROWS = 8
NA_WIN_COLS = 16
NA_QROWS = 8
NA_KROWS = 16
NA_STEP_HEADS = 4

RW_HEAD_DIM = 64
RW_HEADS = 16
RW_WIDTH = RW_HEADS * RW_HEAD_DIM
RW_RANK = 64
RW_SHIFT_COLS = 3 * RW_WIDTH + 4 * RW_RANK
RW_PAD_COLS = 3584
RW_GN_EPS = 64e-5
RW_SCAN_CHUNK = 64
RW_DECAY_SCALE = math.exp(-0.5)

MLA_HEADS = 8
MLA_Q_RANK = 512
MLA_KV_RANK = 512
MLA_NOPE = 128
MLA_ROPE = 64
MLA_V = 128
MLA_QK_PAD = 256
MLA_WIDTH = MLA_HEADS * MLA_V
MLA_A_COLS = 1536
MLA_KEY_CHUNK = 1024

HG_HEADS = 8
HG_DK = 128
HG_DV = 128
HG_FDIM = HG_HEADS * HG_DK
HG_WIDTH = HG_HEADS * HG_DV
HG_CHUNK = 32
HG_BLOCK = 256
HG_STEP_HEADS = 4

LANES = 128
OUT_K_CHUNK = 512

GATE_COLS = 2048
EV_Q_BLK = GATE_COLS // LANES
EV_RW_BLK = EV_Q_BLK + 3 * NA_WIDTH // LANES
OD_A_BLK = GATE_COLS // LANES
OD_HG_BLK = OD_A_BLK + MLA_A_COLS // LANES
VMEM_LIMIT = 48 * 1024 * 1024
NEG_INF = -1e30
LOG2E = math.log2(math.e)


def _cparams(n_axes):
    return pltpu.CompilerParams(dimension_semantics=("arbitrary",) * n_axes, vmem_limit_bytes=VMEM_LIMIT)


def _sigmoid(x):
    return 1.0 / (1.0 + jnp.exp(-x))


def _silu(x):
    return x * _sigmoid(x)


def _dot(a, b):
    return jnp.dot(a.astype(BF16), b.astype(BF16), preferred_element_type=F32)


def _dot_nt(a, b):
    return lax.dot_general(a.astype(BF16), b.astype(BF16), (((1,), (1,)), ((), ())), preferred_element_type=F32)


def _dot_tn(a, b):
    return lax.dot_general(a.astype(BF16), b.astype(BF16), (((0,), (0,)), ((), ())), preferred_element_type=F32)


def _mod_kernel(c_ref, w_ref, b_ref, o_ref):
    o_ref[...] = _dot(_silu(c_ref[...]), w_ref[...]) + b_ref[...]


def modulation(cond, ada_w, ada_b, tn=512):
    depth, d, n = ada_w.shape
    return pl.pallas_call(
        _mod_kernel,
        out_shape=jax.ShapeDtypeStruct((depth, 8, n), F32),
        grid=(depth, n // tn),
        in_specs=[pl.BlockSpec((8, d), lambda l, j: (0, 0)),
                  pl.BlockSpec((None, d, tn), lambda l, j: (l, 0, j)),
                  pl.BlockSpec((None, 1, tn), lambda l, j: (l, 0, j))],
        out_specs=pl.BlockSpec((None, 8, tn), lambda l, j: (l, 0, j)),
        compiler_params=_cparams(2),
        name="adaln_modulation",
    )(cond, ada_w, ada_b.reshape(depth, 1, n))


def _cast_reorder_kernel(src_ref, zero_ref, x_ref, o_ref):
    del src_ref
    o_ref[...] = jnp.where(zero_ref[pl.program_id(0)] != 0, 0.0, x_ref[...]).astype(BF16)


def cast_reorder(w, plan, width, name):
    d = w.shape[0]
    src = np.array([0 if a is None else a for a in plan], np.int32)
    zero = np.array([a is None for a in plan], np.int32)
    return pl.pallas_call(
        _cast_reorder_kernel,
        out_shape=jax.ShapeDtypeStruct((d, width * len(plan)), BF16),
        grid_spec=pltpu.PrefetchScalarGridSpec(
            num_scalar_prefetch=2,
            grid=(len(plan),),
            in_specs=[pl.BlockSpec((d, width), lambda j, src, zero: (0, src[j]))],
            out_specs=pl.BlockSpec((d, width), lambda j, src, zero: (0, j))),
        compiler_params=_cparams(1),
        name=name,
    )(jnp.asarray(src), jnp.asarray(zero), w)


def _inproj_kernel(x_ref, nw_ref, shift_ref, scale_ref, w_ref, o_ref, h_ref):
    @pl.when(pl.program_id(1) == 0)
    def _():
        x = x_ref[...]
        y = x * lax.rsqrt(jnp.mean(x * x, axis=-1, keepdims=True) + NORM_EPS) * nw_ref[...]
        h_ref[...] = (y * (1.0 + scale_ref[...]) + shift_ref[...]).astype(BF16)

    o_ref[...] = jnp.dot(h_ref[...], w_ref[...], preferred_element_type=F32)


def in_projection(x, norm_w, mod, mod_row, w, name, tm=1024, tn=512):
    m, d = x.shape
    n = w.shape[1]
    tm = min(tm, m)
    return pl.pallas_call(
        _inproj_kernel,
        out_shape=jax.ShapeDtypeStruct((m, n), F32),
        grid=(m // tm, n // tn),
        in_specs=[pl.BlockSpec((tm, d), lambda i, j: (i, 0)),
                  pl.BlockSpec((1, d), lambda i, j: (0, 0)),
                  pl.BlockSpec((None, 1, d), lambda i, j: (mod_row(i), 0, 0)),
                  pl.BlockSpec((None, 1, d), lambda i, j: (mod_row(i), 0, 1)),
                  pl.BlockSpec((d, tn), lambda i, j: (0, j))],
        out_specs=pl.BlockSpec((tm, tn), lambda i, j: (i, j)),
        scratch_shapes=[pltpu.VMEM((tm, d), BF16)],
        compiler_params=_cparams(2),
        name=name,
    )(x, norm_w.reshape(1, d), mod, mod, w)


def _outproj_kernel(ya_ref, yb_ref, g_ref, w_ref, x_ref, gm_ref, *rest, ka, final):
    o_ref = rest[-1]
    acc = None
    for y_ref, base in ((ya_ref, 0), (yb_ref, ka)):
        for lo in range(0, y_ref.shape[1], OUT_K_CHUNK):
            cols = slice(base + lo, base + lo + OUT_K_CHUNK)
            part = _dot(y_ref[:, lo:lo + OUT_K_CHUNK].astype(F32) * _silu(g_ref[:, cols]), w_ref[cols, :])
            acc = part if acc is None else acc + part
    out = x_ref[...] + gm_ref[...] * acc
    if final:
        out = out * lax.rsqrt(jnp.mean(out * out, axis=-1, keepdims=True) + NORM_EPS) * rest[0][...]
    o_ref[...] = out


def out_projection(ya, yb, gate, w, x, mod, mod_row, name, final_w=None, tm=256):
    m, d = x.shape
    ka, kb = ya.shape[1], yb.shape[1]
    tm = min(tm, m)
    ins = [ya, yb, gate, w, x, mod]
    specs = [pl.BlockSpec((tm, ka), lambda i: (i, 0)),
             pl.BlockSpec((tm, kb), lambda i: (i, 0)),
             pl.BlockSpec((tm, ka + kb), lambda i: (i, 0)),
             pl.BlockSpec((ka + kb, d), lambda i: (0, 0)),
             pl.BlockSpec((tm, d), lambda i: (i, 0)),
             pl.BlockSpec((None, 1, d), lambda i: (mod_row(i), 0, 2))]
    if final_w is not None:
        ins.append(final_w.reshape(1, d))
        specs.append(pl.BlockSpec((1, d), lambda i: (0, 0)))
    return pl.pallas_call(
        functools.partial(_outproj_kernel, ka=ka, final=final_w is not None),
        out_shape=jax.ShapeDtypeStruct((m, d), F32),
        grid=(m // tm,),
        in_specs=specs,
        out_specs=pl.BlockSpec((tm, d), lambda i: (i, 0)),
        compiler_params=_cparams(1),
        name=name,
    )(*ins)


def _softmax_pv(q, parts, scale):
    c = scale * LOG2E
    m = denom = num = None
    for k, bias, v in parts:
        x = _dot_nt(q, k) * c
        if bias is not None:
            x = x + bias
        m_part = jnp.max(x, axis=-1, keepdims=True)
        if m is None:
            m = m_part
            p = jnp.exp2(x - m)
            denom = jnp.sum(p, axis=-1, keepdims=True)
            num = _dot(p, v)
        else:
            m_new = jnp.maximum(m, m_part)
            alpha = jnp.exp2(m - m_new)
            p = jnp.exp2(x - m_new)
            denom = denom * alpha + jnp.sum(p, axis=-1, keepdims=True)
            num = num * alpha + _dot(p, v)
            m = m_new
    return num / denom


def _softmax_pv_joint(q, parts, scale):
    c = scale * LOG2E
    logits = [_dot_nt(q, k) * c if bias is None else _dot_nt(q, k) * c + bias for k, bias, _ in parts]
    m = functools.reduce(jnp.maximum, [jnp.max(x, axis=-1, keepdims=True) for x in logits])
    ps = [jnp.exp2(x - m) for x in logits]
    denom = functools.reduce(lambda a, b: a + b, [jnp.sum(p, axis=-1, keepdims=True) for p in ps])
    num = functools.reduce(lambda a, b: a + b, [_dot(p, v) for p, (_, _, v) in zip(ps, parts)])
    return num / denom


def _na_kernel(idx_ref, q_ref, k_ref, v_ref, kc_ref, vc_ref, tab_ref, o_ref, bias_ref, *, rows, scale):
    j = pl.program_id(2)
    npair = NA_KROWS // 2
    start = pl.multiple_of(jnp.clip(NA_QROWS * j - NA_WIN_ROWS // 2, 0, rows - NA_KROWS) * GRID_W, GRID_W)
    win = pl.ds(start, NA_KROWS * GRID_W)
    for hd in range(NA_STEP_HEADS):
        sl = slice(hd * LANES, (hd + 1) * LANES)
        for qr in range(NA_QROWS):
            for m in range(npair):
                tile = tab_ref[hd, idx_ref[(j * NA_QROWS + qr) * npair + m]]
                bias_ref[hd, qr * GRID_W:(qr + 1) * GRID_W, m * LANES:(m + 1) * LANES] = tile
        parts = [(k_ref[win, sl], bias_ref[hd], v_ref[win, sl]), (kc_ref[:, sl], None, vc_ref[:, sl])]
        o_ref[:, sl] = _softmax_pv_joint(q_ref[:, sl], parts, scale)


def _na_tile_ids(rows):
    kh = min(NA_WIN_ROWS, rows)
    n = 2 * NA_WIN_ROWS - 1
    ids = []
    for j in range(rows // NA_QROWS):
        ks = int(np.clip(NA_QROWS * j - NA_WIN_ROWS // 2, 0, rows - NA_KROWS))
        for qr in range(NA_QROWS):
            r = NA_QROWS * j + qr
            r0 = int(np.clip(r - kh // 2, 0, rows - kh))
            for m in range(NA_KROWS // 2):
                kr = ks + 2 * m
                dy = kr - r + NA_WIN_ROWS - 1
                first, second = r0 <= kr < r0 + kh, r0 <= kr + 1 < r0 + kh
                ids.append(dy if first and second else n + dy if first else 2 * n + dy + 1 if second else 3 * n)
    return np.array(ids, np.int32)


def _na_bias_tiles(rpb):
    col = np.arange(GRID_W)
    cs = np.clip(col - NA_WIN_COLS // 2, 0, GRID_W - NA_WIN_COLS)
    kc = col[None, :]
    in_win = (kc >= cs[:, None]) & (kc < cs[:, None] + NA_WIN_COLS)
    dx = np.where(in_win, kc - col[:, None] + NA_WIN_COLS - 1, 0)
    pick = jnp.asarray(dx[:, :, None] == np.arange(rpb.shape[-1]), F32)
    picked = jnp.einsum("hyj,cqj->hycq", rpb, pick, precision=lax.Precision.HIGHEST)
    tab = jnp.where(in_win[None, None], picked * LOG2E, NEG_INF)
    neg = jnp.full_like(tab[:, :1], NEG_INF)
    nxt = jnp.concatenate([tab[:, 1:], neg], axis=1)
    both = jnp.concatenate([tab, nxt], axis=-1)
    first = jnp.concatenate([tab, jnp.broadcast_to(neg, tab.shape)], axis=-1)
    second = jnp.concatenate([jnp.broadcast_to(neg, tab.shape), tab], axis=-1)
    return jnp.concatenate([both, first, second, jnp.concatenate([neg, neg], axis=-1)], axis=1)


def na_attention(z, z_c, rpb, batch, t, l):
    rows = t // GRID_W
    nj = rows // NA_QROWS
    tq = NA_QROWS * GRID_W
    ids = jnp.asarray(_na_tile_ids(rows))
    tiles = _na_bias_tiles(rpb)
    nh = NA_STEP_HEADS
    wide = nh * LANES
    q0, k0, v0 = (EV_Q_BLK + i * NA_HEADS for i in range(3))
    assert q0 % nh == 0 and NA_HEADS % nh == 0, "head groups must start on a whole column block"
    return pl.pallas_call(
        functools.partial(_na_kernel, rows=rows, scale=NA_HEAD_DIM ** -0.5),
        out_shape=jax.ShapeDtypeStruct((batch * t, NA_WIDTH), F32),
        grid_spec=pltpu.PrefetchScalarGridSpec(
            num_scalar_prefetch=1,
            grid=(NA_HEADS // nh, batch, nj),
            in_specs=[pl.BlockSpec((tq, wide), lambda h, b, j, g: (b * nj + j, q0 // nh + h)),
                      pl.BlockSpec((t, wide), lambda h, b, j, g: (b, k0 // nh + h)),
                      pl.BlockSpec((t, wide), lambda h, b, j, g: (b, v0 // nh + h)),
                      pl.BlockSpec((l, wide), lambda h, b, j, g: (b, k0 // nh + h)),
                      pl.BlockSpec((l, wide), lambda h, b, j, g: (b, v0 // nh + h)),
                      pl.BlockSpec((nh,) + tiles.shape[1:], lambda h, b, j, g: (h, 0, 0, 0))],
            out_specs=pl.BlockSpec((tq, wide), lambda h, b, j, g: (b * nj + j, h)),
            scratch_shapes=[pltpu.VMEM((nh, tq, NA_KROWS * GRID_W), F32)]),
        compiler_params=_cparams(3),
        name="na_attention",
    )(ids, z, z, z, z_c, z_c, tiles)


def _dense_attn_kernel(q_ref, k_ref, v_ref, o_ref, *, scale):
    o_ref[...] = _softmax_pv(q_ref[...], [(k_ref[...], None, v_ref[...])], scale)


def ctx_attention(z_c, batch, l):
    h8 = NA_HEADS
    return pl.pallas_call(
        functools.partial(_dense_attn_kernel, scale=NA_HEAD_DIM ** -0.5),
        out_shape=jax.ShapeDtypeStruct((batch * l, NA_WIDTH), F32),
        grid=(batch, h8),
        in_specs=[pl.BlockSpec((l, LANES), lambda b, h: (b, EV_Q_BLK + h)),
                  pl.BlockSpec((l, LANES), lambda b, h: (b, EV_Q_BLK + h8 + h)),
                  pl.BlockSpec((l, LANES), lambda b, h: (b, EV_Q_BLK + 2 * h8 + h))],
        out_specs=pl.BlockSpec((l, LANES), lambda b, h: (b, h)),
        compiler_params=_cparams(2),
        name="ctx_attention",
    )(z_c, z_c, z_c)


def _head_sum(x):
    first = lax.broadcasted_iota(jnp.int32, x.shape, 1) < RW_HEAD_DIM
    s0 = jnp.sum(jnp.where(first, x, 0.0), axis=-1, keepdims=True)
    s1 = jnp.sum(jnp.where(first, 0.0, x), axis=-1, keepdims=True)
    return jnp.where(first, s0, s1)


def _rw_terms_kernel(ur_ref, uk_ref, uv_ref, uwa_ref, pr_ref, pk_ref, pv_ref, pwa_ref, nr_ref, nk_ref, nv_ref,
                     nwa_ref, mu_ref, w0_ref, w2_ref, a0_ref, a2_ref, kk_ref, ka_ref, rk_ref,
                     rv_o, wk0_o, wk1_o, bk0_o, bk1_o, bonus_o, *, nt):
    i = pl.program_id(1)
    tm = ur_ref.shape[0]

    def shifted(u_ref, p_ref, n_ref, col0):
        u = u_ref[...]
        mu = mu_ref[:, col0:col0 + u.shape[1]]
        row = lax.broadcasted_iota(jnp.int32, u.shape, 0)
        before = jnp.where(i == 0, 0.0, p_ref[7:8, :])
        after = jnp.where(i == nt - 1, 0.0, n_ref[0:1, :])
        prev = jnp.where(row == 0, before, pltpu.roll(u, 1, axis=0))
        nxt = jnp.where(row == tm - 1, after, pltpu.roll(u, tm - 1, axis=0))
        return u + mu[0:1] * (prev - u) + mu[1:2] * (nxt - u)

    r_all = shifted(ur_ref, pr_ref, nr_ref, 0)
    k_all = shifted(uk_ref, pk_ref, nk_ref, RW_WIDTH)
    v_all = shifted(uv_ref, pv_ref, nv_ref, 2 * RW_WIDTH)
    wa = shifted(uwa_ref, pwa_ref, nwa_ref, 3 * RW_WIDTH)
    wd = jnp.tanh(wa[:, :LANES]).astype(BF16)
    ad = wa[:, LANES:].astype(BF16)
    first = lax.broadcasted_iota(jnp.int32, (tm, LANES), 1) < RW_HEAD_DIM

    def head_rows(x, y):
        return (jnp.where(first, x, pltpu.roll(y, RW_HEAD_DIM, axis=1)),
                jnp.where(first, pltpu.roll(x, RW_HEAD_DIM, axis=1), y))

    outs = (rv_o, wk0_o, wk1_o, bk0_o, bk1_o)
    even = [[] for _ in outs]
    odd = [[] for _ in outs]
    bonus = []
    for p in range(RW_WIDTH // LANES):
        sl = slice(p * LANES, (p + 1) * LANES)
        r, k, v = r_all[:, sl], k_all[:, sl], v_all[:, sl]
        kk = k * kk_ref[:, sl]
        kk = kk / jnp.maximum(jnp.sqrt(_head_sum(kk * kk)), 1e-12)
        kd_sum = jnp.zeros_like(k)
        per_dir = []
        for d in range(2):
            w = w0_ref[d:d + 1, sl] + jnp.dot(wd, w2_ref[d, :, sl].astype(BF16), preferred_element_type=F32)
            a = _sigmoid(a0_ref[d:d + 1, sl] + jnp.dot(ad, a2_ref[d, :, sl].astype(BF16), preferred_element_type=F32))
            kd = k * (1.0 + (a - 1.0) * ka_ref[:, sl])
            dec = jnp.exp(-RW_DECAY_SCALE * _sigmoid(w))
            per_dir.append(((dec, kd), (kk * a, kk)))
            kd_sum = kd_sum + kd
        pairs = [(r, v), per_dir[0][0], per_dir[1][0], per_dir[0][1], per_dir[1][1]]
        bonus.append(_head_sum(r * kd_sum * rk_ref[:, sl]) * v)
        for ev, od, (x, y) in zip(even, odd, pairs):
            e, o = head_rows(x, y)
            ev.append(e)
            od.append(o)
    for o_ref, ev, od in zip(outs, even, odd):
        o_ref[...] = jnp.swapaxes(jnp.stack(ev + od, axis=0), 0, 1)
    bonus_o[...] = jnp.swapaxes(jnp.stack(bonus, axis=0), 0, 1)


def rwkv_terms(z, n_seq, t, mu, w0, w2p, a0, a2p, k_k, k_a, r_k, tm=256):
    tm = min(tm, t)
    nt = t // tm
    npair = RW_WIDTH // LANES
    wide0 = EV_RW_BLK * LANES // RW_WIDTH
    wa_blk = (EV_RW_BLK + 3 * npair) // 2
    n8 = tm // 8

    def cur(width, cb):
        return pl.BlockSpec((tm, width), lambda b, i: (b * nt + i, cb))

    def before(width, cb):
        return pl.BlockSpec((8, width), lambda b, i: (jnp.maximum((b * nt + i) * n8 - 1, 0), cb))

    def after(width, cb):
        return pl.BlockSpec((8, width), lambda b, i: (jnp.minimum((b * nt + i + 1) * n8, n_seq * nt * n8 - 1), cb))

    cols = [(RW_WIDTH, wide0), (RW_WIDTH, wide0 + 1), (RW_WIDTH, wide0 + 2), (2 * LANES, wa_blk)]
    whole = lambda a: pl.BlockSpec(a.shape, lambda b, i: (0,) * a.ndim)
    params = (mu, w0, w2p, a0, a2p, k_k, k_a, r_k)
    out = pl.BlockSpec((tm, npair, LANES), lambda b, i: (i, b, 0))
    out2 = pl.BlockSpec((tm, 2 * npair, LANES), lambda b, i: (i, b, 0))
    return pl.pallas_call(
        functools.partial(_rw_terms_kernel, nt=nt),
        out_shape=[jax.ShapeDtypeStruct((t, n_seq * 2 * npair, LANES), F32)] * 5
        + [jax.ShapeDtypeStruct((t, n_seq * npair, LANES), F32)],
        grid=(n_seq, nt),
        in_specs=([cur(*c) for c in cols] + [before(*c) for c in cols] + [after(*c) for c in cols]
                  + [whole(a) for a in params]),
        out_specs=[out2] * 5 + [out],
        compiler_params=_cparams(2),
        name="rwkv_terms",
    )(*([z] * 12), *params)


def _to_lanes(n):
    q = jnp.concatenate([n, pltpu.roll(n, RW_HEAD_DIM, axis=1)], axis=0)
    return q.T[:RW_HEAD_DIM]


def _rw_scan_kernel(rvf, rvb, wk0, wk1, bk0, bk1, s0_ref, yf_ref, yb_ref, sfin_ref,
                    s_ref, z_ref, znext_ref, g_ref, ybuf_ref, *, chunk):
    n = RW_HEAD_DIM
    nb = rvf.shape[1] // (2 * RW_WIDTH // LANES)
    R, K, V, A, B = range(5)

    @pl.when(pl.program_id(0) == 0)
    def _():
        s_ref[...] = s0_ref[...]

    def convert(t_src):
        def pair_tiles(f, bw):
            both = (f[t_src], bw[chunk - 1 - t_src])
            x = jnp.concatenate([src[(b * 2 + par) * 8:(b * 2 + par) * 8 + 8]
                                 for par in range(2) for src in both for b in range(nb)], axis=0)
            xt = x.T
            return xt[:n], xt[n:]

        r, v = pair_tiles(rvf, rvb)
        w, k = pair_tiles(wk0, wk1)
        b, kk = pair_tiles(bk0, bk1)
        g_prev = g_ref[0]
        g = g_prev * w
        g_ref[0] = g
        inv = 1.0 / g
        znext_ref[R] = r * g
        znext_ref[K] = k * inv
        znext_ref[V] = v
        znext_ref[A] = -(kk * g_prev)
        znext_ref[B] = b * inv

    def emit_y(t):
        rows = _to_lanes(ybuf_ref[...])
        yf_ref[t] = rows[:n // 2]
        yb_ref[chunk - 1 - t] = rows[n // 2:]

    g_ref[0] = jnp.ones((n, LANES), F32)
    ybuf_ref[...] = jnp.zeros((n, LANES), F32)
    convert(0)

    def step(t, carry):
        emit_y(jnp.maximum(t - 1, 0))
        z_ref[...] = znext_ref[...]
        g_ref[1] = g_ref[0]
        convert(jnp.minimum(t + 1, chunk - 1))
        sa = [jnp.zeros((n, LANES), F32), jnp.zeros((n, LANES), F32)]
        for k in range(n):
            sa[k % 2] = sa[k % 2] + s_ref[k] * z_ref[A, k:k + 1, :]
        sa = sa[0] + sa[1]
        vt = z_ref[V]
        y = [jnp.zeros((n, LANES), F32), jnp.zeros((n, LANES), F32)]
        for k in range(n):
            sk = s_ref[k] + sa * z_ref[B, k:k + 1, :] + vt * z_ref[K, k:k + 1, :]
            s_ref[k] = sk
            y[k % 2] = y[k % 2] + sk * z_ref[R, k:k + 1, :]
        ybuf_ref[...] = y[0] + y[1]
        return carry

    lax.fori_loop(0, chunk, step, 0)
    emit_y(chunk - 1)
    for k in range(n):
        s_ref[k] = s_ref[k] * g_ref[1, k:k + 1, :]
    sfin_ref[...] = s_ref[...]


def rwkv_scan(rv, wk0, wk1, bk0, bk1, s0):
    t, nrow, _ = rv.shape
    n = RW_HEAD_DIM
    chunk = RW_SCAN_CHUNK
    nc = t // chunk
    assert nrow == LANES // 2, "the states of both directions fill the 128 lanes"
    fwd = pl.BlockSpec((chunk, nrow, LANES), lambda i: (i, 0, 0))
    bwd = pl.BlockSpec((chunk, nrow, LANES), lambda i: (nc - 1 - i, 0, 0))
    yfwd = pl.BlockSpec((chunk, nrow // 2, LANES), lambda i: (i, 0, 0))
    ybwd = pl.BlockSpec((chunk, nrow // 2, LANES), lambda i: (nc - 1 - i, 0, 0))
    state = pl.BlockSpec((n, n, LANES), lambda i: (0, 0, 0))
    slab = jax.ShapeDtypeStruct((t, nrow // 2, LANES), F32)
    return pl.pallas_call(
        functools.partial(_rw_scan_kernel, chunk=chunk),
        out_shape=[slab, slab, jax.ShapeDtypeStruct((n, n, LANES), F32)],
        grid=(nc,),
        in_specs=[fwd, bwd, fwd, bwd, fwd, bwd, state],
        out_specs=[yfwd, ybwd, state],
        scratch_shapes=[pltpu.VMEM((n, n, LANES), F32), pltpu.VMEM((5, n, LANES), F32),
                        pltpu.VMEM((5, n, LANES), F32), pltpu.VMEM((2, n, LANES), F32),
                        pltpu.VMEM((n, LANES), F32)],
        compiler_params=_cparams(1),
        name="rwkv_scan",
    )(rv, rv, wk0, wk1, bk0, bk1, s0)


def _rw_readout_kernel(yf_ref, yb_ref, bonus_ref, lw_ref, lb_ref, o_ref):
    npair = RW_WIDTH // LANES
    y_all = jnp.swapaxes(yf_ref[...] + yb_ref[...], 0, 1)
    bonus_all = jnp.swapaxes(bonus_ref[...], 0, 1)
    for q in range(yf_ref.shape[1]):
        b, p = divmod(q, npair)
        sl = slice(p * LANES, (p + 1) * LANES)
        y = y_all[q]
        mu = _head_sum(y) * (1.0 / RW_HEAD_DIM)
        d = y - mu
        var = _head_sum(d * d) * (1.0 / RW_HEAD_DIM)
        o_ref[b, :, sl] = d * lax.rsqrt(var + RW_GN_EPS) * lw_ref[:, sl] + lb_ref[:, sl] + bonus_all[q]


def rwkv_readout(y_f, y_b, bonus, ln_w, ln_b, tm=256):
    t, nq, _ = y_f.shape
    tm = min(tm, t)
    batch = nq // (RW_WIDTH // LANES)
    slab = pl.BlockSpec((tm, nq, LANES), lambda i: (i, 0, 0))
    vec = pl.BlockSpec((1, RW_WIDTH), lambda i: (0, 0))
    return pl.pallas_call(
        _rw_readout_kernel,
        out_shape=jax.ShapeDtypeStruct((batch, t, RW_WIDTH), F32),
        grid=(t // tm,),
        in_specs=[slab, slab, slab, vec, vec],
        out_specs=pl.BlockSpec((batch, tm, RW_WIDTH), lambda i: (0, i, 0)),
        compiler_params=_cparams(1),
        name="rwkv_readout",
    )(y_f, y_b, bonus, ln_w.reshape(1, -1), ln_b.reshape(1, -1))


def _rms(x, w):
    return x * lax.rsqrt(jnp.mean(x * x, axis=-1, keepdims=True) + NORM_EPS) * w


def _rope(x, cos, sin):
    odd = (lax.broadcasted_iota(jnp.int32, x.shape, 1) & 1) == 1
    swapped = jnp.where(odd, pltpu.roll(x, 1, axis=1), pltpu.roll(x, LANES - 1, axis=1))
    return x * cos + swapped * sin


def _mla_q_kernel(c_ref, nw_ref, w_ref, cos_ref, sin_ref, q_ref):
    h = _rms(c_ref[...], nw_ref[...]).astype(BF16)
    acc = jnp.dot(h, w_ref[...], preferred_element_type=F32)
    cos, sin = cos_ref[...], sin_ref[...]
    for hd in range(MLA_HEADS):
        lo = hd * MLA_QK_PAD
        q_ref[:, lo:lo + MLA_NOPE] = acc[:, lo:lo + MLA_NOPE].astype(BF16)
        q_ref[:, lo + MLA_NOPE:lo + MLA_QK_PAD] = _rope(acc[:, lo + MLA_NOPE:lo + MLA_QK_PAD], cos, sin).astype(BF16)


def mla_queries(z, q_norm, w_uq_pad, cos, sin, t, tm=512):
    m = z.shape[0]
    tm = min(tm, t)
    nt = t // tm
    width = MLA_HEADS * MLA_QK_PAD
    return pl.pallas_call(
        _mla_q_kernel,
        out_shape=jax.ShapeDtypeStruct((m, width), BF16),
        grid=(m // tm,),
        in_specs=[pl.BlockSpec((tm, MLA_Q_RANK), lambda i: (i, GATE_COLS // MLA_Q_RANK)),
                  pl.BlockSpec((1, MLA_Q_RANK), lambda i: (0, 0)),
                  pl.BlockSpec((MLA_Q_RANK, width), lambda i: (0, 0)),
                  pl.BlockSpec((tm, LANES), lambda i: (i % nt, 0)),
                  pl.BlockSpec((tm, LANES), lambda i: (i % nt, 0))],
        out_specs=pl.BlockSpec((tm, width), lambda i: (i, 0)),
        compiler_params=_cparams(1),
        name="mla_queries",
    )(z, q_norm.reshape(1, -1), w_uq_pad, cos, sin)


def _mla_kv_kernel(c_ref, pe_ref, nw_ref, w_ref, cos_ref, sin_ref, k_ref, v_ref):
    h = _rms(c_ref[...], nw_ref[...]).astype(BF16)
    acc = jnp.dot(h, w_ref[...], preferred_element_type=F32)
    pe = _rope(pe_ref[...], cos_ref[...], sin_ref[...]).astype(BF16)
    for hd in range(MLA_HEADS):
        src = hd * (MLA_NOPE + MLA_V)
        dst = hd * MLA_QK_PAD
        k_ref[:, dst:dst + MLA_NOPE] = acc[:, src:src + MLA_NOPE].astype(BF16)
        k_ref[:, dst + MLA_NOPE:dst + MLA_QK_PAD] = pe
        v_ref[:, hd * MLA_V:(hd + 1) * MLA_V] = acc[:, src + MLA_NOPE:src + MLA_NOPE + MLA_V].astype(BF16)


def mla_keys_values(z, kv_norm, w_ukv, cos, sin, t, tm=512):
    m = z.shape[0]
    tm = min(tm, t)
    nt = t // tm
    return pl.pallas_call(
        _mla_kv_kernel,
        out_shape=[jax.ShapeDtypeStruct((m, MLA_HEADS * MLA_QK_PAD), BF16),
                   jax.ShapeDtypeStruct((m, MLA_WIDTH), BF16)],
        grid=(m // tm,),
        in_specs=[pl.BlockSpec((tm, MLA_KV_RANK), lambda i: (i, (GATE_COLS + MLA_Q_RANK) // MLA_KV_RANK)),
                  pl.BlockSpec((tm, LANES), lambda i: (i, OD_A_BLK + (MLA_Q_RANK + MLA_KV_RANK) // LANES)),
                  pl.BlockSpec((1, MLA_KV_RANK), lambda i: (0, 0)),
                  pl.BlockSpec(w_ukv.shape, lambda i: (0, 0)),
                  pl.BlockSpec((tm, LANES), lambda i: (i % nt, 0)),
                  pl.BlockSpec((tm, LANES), lambda i: (i % nt, 0))],
        out_specs=[pl.BlockSpec((tm, MLA_HEADS * MLA_QK_PAD), lambda i: (i, 0)),
                   pl.BlockSpec((tm, MLA_WIDTH), lambda i: (i, 0))],
        compiler_params=_cparams(1),
        name="mla_keys_values",
    )(z, z, kv_norm.reshape(1, -1), w_ukv, cos, sin)


def _mla_attn_kernel(q_ref, k_ref, v_ref, kc_ref, vc_ref, o_ref, *, scale):
    parts = [(kc_ref[...], None, vc_ref[...])]
    nkeys = k_ref.shape[0]
    step = min(MLA_KEY_CHUNK, nkeys)
    for lo in range(0, nkeys, step):
        parts.append((k_ref[lo:lo + step, :], None, v_ref[lo:lo + step, :]))
    o_ref[...] = _softmax_pv(q_ref[...], parts, scale)


def mla_attention(q, k, v, kc, vc, batch, t, l, tq=2048):
    tq = min(tq, t)
    nq = t // tq
    return pl.pallas_call(
        functools.partial(_mla_attn_kernel, scale=(MLA_NOPE + MLA_ROPE) ** -0.5),
        out_shape=jax.ShapeDtypeStruct((batch * t, MLA_WIDTH), F32),
        grid=(batch, MLA_HEADS, nq),
        in_specs=[pl.BlockSpec((tq, MLA_QK_PAD), lambda b, h, i: (b * nq + i, h)),
                  pl.BlockSpec((t, MLA_QK_PAD), lambda b, h, i: (b, h)),
                  pl.BlockSpec((t, MLA_V), lambda b, h, i: (b, h)),
                  pl.BlockSpec((l, MLA_QK_PAD), lambda b, h, i: (b, h)),
                  pl.BlockSpec((l, MLA_V), lambda b, h, i: (b, h))],
        out_specs=pl.BlockSpec((tq, MLA_V), lambda b, h, i: (b * nq + i, h)),
        compiler_params=_cparams(3),
        name="mla_attention",
    )(q, k, v, kc, vc)


def _rope_tables(t):
    tok = np.arange(t)
    pos = np.stack([tok // GRID_W, tok % GRID_W], axis=-1).astype(np.float32)
    n_freq = MLA_ROPE // 4
    inv = (ROPE_BASE ** (-jnp.arange(n_freq, dtype=F32) / n_freq))
    ang = (jnp.asarray(pos)[:, :, None] * inv).reshape(t, MLA_ROPE // 2)
    cos = jnp.repeat(jnp.cos(ang), 2, axis=-1)
    sin = jnp.repeat(jnp.sin(ang), 2, axis=-1) * jnp.tile(jnp.array([-1.0, 1.0], F32), MLA_ROPE // 2)
    pad = ((0, 0), (0, LANES - MLA_ROPE))
    return jnp.pad(cos, pad), jnp.pad(sin, pad)


def _split3(x):
    hi = x.astype(BF16)
    r1 = x - hi.astype(F32)
    mid = r1.astype(BF16)
    lo = (r1 - mid.astype(F32)).astype(BF16)
    return hi, mid, lo


def _hg_kernel(q_ref, f_ref, v_ref, lb_ref, s0_ref, *rest, reverse, finalize, emit_state):
    st_ref = rest[-1]
    rest = rest[:-1]
    if finalize:
        prev_ref, nw_ref = rest[0], rest[1]
        rest = rest[2:]
    o_ref = rest[0]

    @pl.when(pl.program_id(2) == 0)
    def _():
        st_ref[...] = s0_ref[...]

    bt = q_ref.shape[0]
    nchunk = bt // HG_CHUNK
    order = range(nchunk - 1, -1, -1) if reverse else range(nchunk)
    ri = lax.broadcasted_iota(jnp.int32, (bt, bt), 0)
    ci = lax.broadcasted_iota(jnp.int32, (bt, bt), 1)
    same = (ri // HG_CHUNK) == (ci // HG_CHUNK)
    tri = same & ((ci >= ri) if reverse else (ci <= ri))
    tri_f = tri.astype(F32)
    tri_b = tri.astype(BF16)
    row_chunk = lax.broadcasted_iota(jnp.int32, (bt, HG_DK), 0) // HG_CHUNK
    in_chunk = [row_chunk == c for c in range(nchunk)]

    lb = lb_ref[...]
    fz = f_ref[...]
    log_sig = jnp.minimum(fz, 0.0) - jnp.log1p(jnp.exp(-jnp.abs(fz)))
    la = jnp.log(lb)
    lbb = jnp.log1p(-lb) + log_sig
    log_f = jnp.maximum(la, lbb) + jnp.log1p(jnp.exp(-jnp.abs(la - lbb)))
    kf_all = (1.0 - lb) * _sigmoid(-fz)
    pieces = jnp.dot(tri_b, jnp.concatenate(_split3(log_f), axis=1), preferred_element_type=F32)
    width = log_f.shape[1]
    cum_all = pieces[:, :width] + pieces[:, width:2 * width] + pieces[:, 2 * width:]

    for hh in range(HG_STEP_HEADS):
        sl = slice(hh * HG_DK, (hh + 1) * HG_DK)
        cum, kf, v = cum_all[:, sl], kf_all[:, sl], v_ref[:, sl]
        last = [cum[c * HG_CHUNK:c * HG_CHUNK + 1] if reverse else cum[(c + 1) * HG_CHUNK - 1:(c + 1) * HG_CHUNK]
                for c in range(nchunk)]
        tot = jnp.concatenate([jnp.broadcast_to(x, (HG_CHUNK, HG_DK)) for x in last], axis=0)
        q_in = q_ref[:, sl] * jnp.exp(cum)
        k_in = kf * jnp.exp(-cum)
        k_end = kf * jnp.exp(tot - cum)
        o_intra = _dot(_dot_nt(q_in, k_in) * tri_f, v)
        v_by_chunk = jnp.concatenate([jnp.where(in_chunk[c], v, 0.0) for c in range(nchunk)], axis=1)
        d_st = _dot_tn(v_by_chunk, k_end)
        st = st_ref[hh]
        before = [None] * nchunk
        for c in order:
            before[c] = st
            st = st * jnp.exp(last[c]) + d_st[c * HG_DV:(c + 1) * HG_DV]
        st_ref[hh] = st
        q_by_chunk = jnp.concatenate([jnp.where(in_chunk[c], q_in, 0.0) for c in range(nchunk)], axis=1)
        o = o_intra + _dot_nt(q_by_chunk, jnp.concatenate(before, axis=1))
        if finalize:
            o = o + prev_ref[:, sl]
            o = o * lax.rsqrt(jnp.mean(o * o, axis=-1, keepdims=True) + NORM_EPS) * nw_ref[...]
        o_ref[:, sl] = o
        if emit_state:
            rest[1][hh] = st


def hgrn2_pass(z, lb, s0, batch, t, f_off, reverse, prev=None, norm_w=None, emit_state=False):
    bt = min(HG_BLOCK, t)
    nblk = t // bt
    nh = HG_STEP_HEADS
    wide = nh * LANES
    finalize = prev is not None
    assert OD_HG_BLK % nh == 0 and HG_HEADS % nh == 0, "head groups must start on a whole column block"
    pos = (lambda i: nblk - 1 - i) if reverse else (lambda i: i)
    blk = lambda off: pl.BlockSpec((bt, wide), lambda b, h, i, off=off: (b * nblk + pos(i), off // nh + h))
    state = pl.BlockSpec((None, nh, HG_DV, HG_DK), lambda b, h, i: (b, h, 0, 0))
    ins = [z, z, z, lb.reshape(1, -1), s0]
    specs = [blk(OD_HG_BLK), blk(OD_HG_BLK + f_off), blk(OD_HG_BLK + 3 * HG_HEADS),
             pl.BlockSpec((1, wide), lambda b, h, i: (0, h)), state]
    if finalize:
        ins += [prev, norm_w.reshape(1, -1)]
        specs += [blk(0), pl.BlockSpec((1, HG_DV), lambda b, h, i: (0, 0))]
    out_shape = [jax.ShapeDtypeStruct((batch * t, HG_WIDTH), F32)]
    out_specs = [blk(0)]
    if emit_state:
        out_shape.append(jax.ShapeDtypeStruct((batch, HG_HEADS, HG_DV, HG_DK), F32))
        out_specs.append(state)
    return pl.pallas_call(
        functools.partial(_hg_kernel, reverse=reverse, finalize=finalize, emit_state=emit_state),
        out_shape=out_shape,
        grid=(batch, HG_HEADS // nh, nblk),
        in_specs=specs,
        out_specs=out_specs,
        scratch_shapes=[pltpu.VMEM((nh, HG_DV, HG_DK), F32)],
        compiler_params=_cparams(3),
        name="hgrn2_" + ("bwd" if reverse else "fwd"),
    )(*ins)


def _even_layer(x, xc, mod, norm_w, w_in, w_out, rpb, rw_params, batch, t, l):
    s2 = 3 * NA_WIDTH + RW_SHIFT_COLS
    width = RW_PAD_COLS - RW_SHIFT_COLS
    n_gate, n_rest = GATE_COLS // width, s2 // width
    plan = [n_rest + i for i in range(n_gate)] + list(range(n_rest)) + [None]
    w_ext = cast_reorder(w_in, plan, width, "even_w_in")
    tiles_per_seq = max(t // 1024, 1)
    z = in_projection(x, norm_w, mod, lambda i: i // tiles_per_seq, w_ext, "even_in_proj")
    z_c = in_projection(xc, norm_w, mod, lambda i: batch, w_ext, "even_in_proj_ctx")

    y_na = na_attention(z, z_c, rpb, batch, t, l)
    yc_na = ctx_attention(z_c, batch, l)

    mu, w0, w2, a0, a2, k_k, k_a, r_k, ln_w, ln_b = rw_params
    mu = jnp.pad(mu, ((0, 0), (0, RW_PAD_COLS - RW_SHIFT_COLS)))
    zpad = jnp.zeros((RW_RANK, RW_WIDTH), F32)
    w2p = jnp.stack([jnp.concatenate([w2[0], zpad]), jnp.concatenate([zpad, w2[1]])])
    a2p = jnp.stack([jnp.concatenate([a2[0], zpad]), jnp.concatenate([zpad, a2[1]])])
    vecs = (mu, w0, w2p, a0, a2p, k_k.reshape(1, -1), k_a.reshape(1, -1), r_k.reshape(1, -1))
    terms = rwkv_terms(z, batch, t, *vecs)
    terms_c = rwkv_terms(z_c, batch, l, *vecs)
    s_zero = jnp.zeros((RW_HEAD_DIM, RW_HEAD_DIM, LANES), F32)
    yc_f, yc_b, s_ctx = rwkv_scan(*terms_c[:5], s_zero)
    y_f, y_b, _ = rwkv_scan(*terms[:5], s_ctx)
    y_rw = rwkv_readout(y_f, y_b, terms[5], ln_w, ln_b).reshape(batch * t, RW_WIDTH)
    yc_rw = rwkv_readout(yc_f, yc_b, terms_c[5], ln_w, ln_b).reshape(batch * l, RW_WIDTH)

    w_out = w_out.astype(BF16)
    rows_per_seq = max(t // 256, 1)
    x = out_projection(y_na, y_rw, z, w_out, x, mod, lambda i: i // rows_per_seq, "even_out_proj")
    xc = out_projection(yc_na, yc_rw, z_c, w_out, xc, mod, lambda i: batch, "even_out_proj_ctx")
    return x, xc


def _odd_layer(x, xc, mod, norm_w, w_in, w_out, q_norm, w_uq, kv_norm, w_ukv, lb, hg_norm_w, final_w,
               batch, t, l):
    o3 = MLA_Q_RANK + MLA_KV_RANK + MLA_ROPE
    o4 = o3 + 4 * HG_FDIM
    zero = jnp.zeros((x.shape[1], MLA_A_COLS - o3), BF16)
    w_in = w_in.astype(BF16)
    w_ext = jnp.concatenate([w_in[:, o4:], w_in[:, :o3], zero, w_in[:, o3:o4]], axis=1)
    tiles_per_seq = max(t // 1024, 1)
    z = in_projection(x, norm_w, mod, lambda i: i // tiles_per_seq, w_ext, "odd_in_proj")
    z_c = in_projection(xc, norm_w, mod, lambda i: batch, w_ext, "odd_in_proj_ctx")

    cos, sin = _rope_tables(t)
    qk = MLA_NOPE + MLA_ROPE
    w_uq_pad = jnp.pad(w_uq.reshape(MLA_Q_RANK, MLA_HEADS, qk), ((0, 0), (0, 0), (0, MLA_QK_PAD - qk)))
    w_uq_pad = w_uq_pad.reshape(MLA_Q_RANK, MLA_HEADS * MLA_QK_PAD).astype(BF16)
    w_ukv = w_ukv.astype(BF16)
    q = mla_queries(z, q_norm, w_uq_pad, cos, sin, t)
    k, v = mla_keys_values(z, kv_norm, w_ukv, cos, sin, t)
    ones = jnp.ones((l, LANES), F32)
    kc, vc = mla_keys_values(z_c, kv_norm, w_ukv, ones, jnp.zeros_like(ones), l)
    y_mla = mla_attention(q, k, v, kc, vc, batch, t, l)

    s_zero = jnp.zeros((batch, HG_HEADS, HG_DV, HG_DK), F32)
    _, s_f = hgrn2_pass(z_c, lb, s_zero, batch, l, HG_HEADS, False, emit_state=True)
    _, s_b = hgrn2_pass(z_c, lb, s_zero, batch, l, 2 * HG_HEADS, True, emit_state=True)
    o_f, = hgrn2_pass(z, lb, s_f, batch, t, HG_HEADS, False)
    y_hg, = hgrn2_pass(z, lb, s_b, batch, t, 2 * HG_HEADS, True, prev=o_f, norm_w=hg_norm_w)

    rows_per_seq = max(t // 256, 1)
    return out_projection(y_mla, y_hg, z, w_out.astype(BF16), x, mod, lambda i: i // rows_per_seq,
                          "odd_out_proj", final_w=final_w)


def kernel(x, c, ctx, c_ctx, ada_w, ada_b, norm_w, e_w_in, e_w_out, na_rpb, rw_mu, rw_w0, rw_w2, rw_a0, rw_a2, rw_k_k, rw_k_a, rw_r_k, rw_ln_w, rw_ln_b, o_w_in, o_w_out, mla_q_norm, mla_w_uq, mla_kv_norm, mla_w_ukv, hg_lower_bounds, hg_norm_w, final_norm_w):
    batch, t, d = x.shape
    l = ctx.shape[1]
    assert ada_w.shape[0] == 2, "one even and one odd layer"
    cond = jnp.concatenate([c, c_ctx[None, :], jnp.zeros((8 - batch - 1, d), F32)], axis=0)
    mod = modulation(cond, ada_w, ada_b)
    s = jax.nn.softmax(hg_lower_bounds.astype(F32), axis=0)
    lower = jnp.cumsum(s, axis=0) - s[0]

    xf, xcf = x.reshape(batch * t, d), ctx.reshape(batch * l, d)
    rw_params = (rw_mu[0], rw_w0[0], rw_w2[0], rw_a0[0], rw_a2[0], rw_k_k[0], rw_k_a[0], rw_r_k[0],
                 rw_ln_w[0], rw_ln_b[0])
    xf, xcf = _even_layer(xf, xcf, mod[0].reshape(8, 1, 3 * d), norm_w[0], e_w_in[0], e_w_out[0], na_rpb[0],
                          rw_params, batch, t, l)
    out = _odd_layer(xf, xcf, mod[1].reshape(8, 1, 3 * d), norm_w[1], o_w_in[0], o_w_out[0], mla_q_norm[0],
                     mla_w_uq[0], mla_kv_norm[0], mla_w_ukv[0], lower[1], hg_norm_w[0], final_norm_w,
                     batch, t, l)
    return out.reshape(batch, t, d)
```

```python
import functools
import math

import numpy as np
import jax
import jax.numpy as jnp
from jax import lax
from jax.experimental import pallas as pl
from jax.experimental.pallas import tpu as pltpu

F32 = jnp.float32
BF16 = jnp.bfloat16

GRID_W = 64
NORM_EPS = 1e-6
ROPE_BASE = 10000.0

NA_HEADS = 8
NA_HEAD_DIM = 128
NA_WIDTH = NA_HEADS * NA_HEAD_DIM
NA_WIN_ROWS = 8
NA_WIN_COLS = 16
NA_QROWS = 8
NA_KROWS = 16
NA_STEP_HEADS = 4

RW_HEAD_DIM = 64
RW_HEADS = 16
RW_WIDTH = RW_HEADS * RW_HEAD_DIM
RW_RANK = 64
RW_SHIFT_COLS = 3 * RW_WIDTH + 4 * RW_RANK
RW_PAD_COLS = 3584
RW_GN_EPS = 64e-5
RW_SCAN_CHUNK = 64
RW_DECAY_SCALE = math.exp(-0.5)

MLA_HEADS = 8
MLA_Q_RANK = 512
MLA_KV_RANK = 512
MLA_NOPE = 128
MLA_ROPE = 64
MLA_V = 128
MLA_QK_PAD = 256
MLA_WIDTH = MLA_HEADS * MLA_V
MLA_A_COLS = 1536
MLA_KEY_CHUNK = 1024

HG_HEADS = 8
HG_DK = 128
HG_DV = 128
HG_FDIM = HG_HEADS * HG_DK
HG_WIDTH = HG_HEADS * HG_DV
HG_CHUNK = 32
HG_BLOCK = 256
HG_STEP_HEADS = 4

LANES = 128

GATE_COLS = 2048
EV_Q_BLK = GATE_COLS // LANES
EV_RW_BLK = EV_Q_BLK + 3 * NA_WIDTH // LANES
OD_A_BLK = GATE_COLS // LANES
OD_HG_BLK = OD_A_BLK + MLA_A_COLS // LANES
VMEM_LIMIT = 48 * 1024 * 1024
NEG_INF = -1e30
LOG2E = math.log2(math.e)


def _cparams(n_axes):
    return pltpu.CompilerParams(dimension_semantics=("arbitrary",) * n_axes, vmem_limit_bytes=VMEM_LIMIT)


def _sigmoid(x):
    return 1.0 / (1.0 + jnp.exp(-x))


def _silu(x):
    return x * _sigmoid(x)


def _dot(a, b):
    return jnp.dot(a.astype(BF16), b.astype(BF16), preferred_element_type=F32)


def _dot_nt(a, b):
    return lax.dot_general(a.astype(BF16), b.astype(BF16), (((1,), (1,)), ((), ())), preferred_element_type=F32)


def _dot_tn(a, b):
    return lax.dot_general(a.astype(BF16), b.astype(BF16), (((0,), (0,)), ((), ())), preferred_element_type=F32)


def _mod_kernel(c_ref, w_ref, b_ref, o_ref):
    o_ref[...] = _dot(_silu(c_ref[...]), w_ref[...]) + b_ref[...]


def modulation(cond, ada_w, ada_b, tn=512):
    depth, d, n = ada_w.shape
    return pl.pallas_call(
        _mod_kernel,
        out_shape=jax.ShapeDtypeStruct((depth, 8, n), F32),
        grid=(depth, n // tn),
        in_specs=[pl.BlockSpec((8, d), lambda l, j: (0, 0)),
                  pl.BlockSpec((None, d, tn), lambda l, j: (l, 0, j)),
                  pl.BlockSpec((None, 1, tn), lambda l, j: (l, 0, j))],
        out_specs=pl.BlockSpec((None, 8, tn), lambda l, j: (l, 0, j)),
        compiler_params=_cparams(2),
        name="adaln_modulation",
    )(cond, ada_w, ada_b.reshape(depth, 1, n))


def _cast_reorder_kernel(src_ref, zero_ref, x_ref, o_ref):
    del src_ref
    o_ref[...] = jnp.where(zero_ref[pl.program_id(0)] != 0, 0.0, x_ref[...]).astype(BF16)


def cast_reorder(w, plan, width, name):
    d = w.shape[0]
    src = np.array([0 if a is None else a for a in plan], np.int32)
    zero = np.array([a is None for a in plan], np.int32)
    return pl.pallas_call(
        _cast_reorder_kernel,
        out_shape=jax.ShapeDtypeStruct((d, width * len(plan)), BF16),
        grid_spec=pltpu.PrefetchScalarGridSpec(
            num_scalar_prefetch=2,
            grid=(len(plan),),
            in_specs=[pl.BlockSpec((d, width), lambda j, src, zero: (0, src[j]))],
            out_specs=pl.BlockSpec((d, width), lambda j, src, zero: (0, j))),
        compiler_params=_cparams(1),
        name=name,
    )(jnp.asarray(src), jnp.asarray(zero), w)


def _inproj_kernel(x_ref, nw_ref, shift_ref, scale_ref, w_ref, o_ref, h_ref):
    @pl.when(pl.program_id(1) == 0)
    def _():
        x = x_ref[...]
        y = x * lax.rsqrt(jnp.mean(x * x, axis=-1, keepdims=True) + NORM_EPS) * nw_ref[...]
        h_ref[...] = (y * (1.0 + scale_ref[...]) + shift_ref[...]).astype(BF16)

    o_ref[...] = jnp.dot(h_ref[...], w_ref[...], preferred_element_type=F32)


def in_projection(x, norm_w, mod, mod_row, w, name, tm=1024, tn=512):
    m, d = x.shape
    n = w.shape[1]
    tm = min(tm, m)
    return pl.pallas_call(
        _inproj_kernel,
        out_shape=jax.ShapeDtypeStruct((m, n), F32),
        grid=(m // tm, n // tn),
        in_specs=[pl.BlockSpec((tm, d), lambda i, j: (i, 0)),
                  pl.BlockSpec((1, d), lambda i, j: (0, 0)),
                  pl.BlockSpec((None, 1, d), lambda i, j: (mod_row(i), 0, 0)),
                  pl.BlockSpec((None, 1, d), lambda i, j: (mod_row(i), 0, 1)),
                  pl.BlockSpec((d, tn), lambda i, j: (0, j))],
        out_specs=pl.BlockSpec((tm, tn), lambda i, j: (i, j)),
        scratch_shapes=[pltpu.VMEM((tm, d), BF16)],
        compiler_params=_cparams(2),
        name=name,
    )(x, norm_w.reshape(1, d), mod, mod, w)


def _outproj_kernel(ya_ref, yb_ref, g_ref, w_ref, x_ref, gm_ref, *rest, ka, final):
    o_ref = rest[-1]
    sg = _silu(g_ref[...])
    acc = _dot(ya_ref[...].astype(F32) * sg[:, :ka], w_ref[:ka, :])
    acc += _dot(yb_ref[...].astype(F32) * sg[:, ka:], w_ref[ka:, :])
    out = x_ref[...] + gm_ref[...] * acc
    if final:
        out = out * lax.rsqrt(jnp.mean(out * out, axis=-1, keepdims=True) + NORM_EPS) * rest[0][...]
    o_ref[...] = out


def out_projection(ya, yb, gate, w, x, mod, mod_row, name, final_w=None, tm=256):
    m, d = x.shape
    ka, kb = ya.shape[1], yb.shape[1]
    tm = min(tm, m)
    ins = [ya, yb, gate, w, x, mod]
    specs = [pl.BlockSpec((tm, ka), lambda i: (i, 0)),
             pl.BlockSpec((tm, kb), lambda i: (i, 0)),
             pl.BlockSpec((tm, ka + kb), lambda i: (i, 0)),
             pl.BlockSpec((ka + kb, d), lambda i: (0, 0)),
             pl.BlockSpec((tm, d), lambda i: (i, 0)),
             pl.BlockSpec((None, 1, d), lambda i: (mod_row(i), 0, 2))]
    if final_w is not None:
        ins.append(final_w.reshape(1, d))
        specs.append(pl.BlockSpec((1, d), lambda i: (0, 0)))
    return pl.pallas_call(
        functools.partial(_outproj_kernel, ka=ka, final=final_w is not None),
        out_shape=jax.ShapeDtypeStruct((m, d), F32),
        grid=(m // tm,),
        in_specs=specs,
        out_specs=pl.BlockSpec((tm, d), lambda i: (i, 0)),
        compiler_params=_cparams(1),
        name=name,
    )(*ins)


def _softmax_pv(q, parts, scale):
    c = scale * LOG2E
    m = denom = num = None
    for k, bias, v in parts:
        x = _dot_nt(q, k) * c
        if bias is not None:
            x = x + bias
        m_part = jnp.max(x, axis=-1, keepdims=True)
        if m is None:
            m = m_part
            p = jnp.exp2(x - m)
            denom = jnp.sum(p, axis=-1, keepdims=True)
            num = _dot(p, v)
        else:
            m_new = jnp.maximum(m, m_part)
            alpha = jnp.exp2(m - m_new)
            p = jnp.exp2(x - m_new)
            denom = denom * alpha + jnp.sum(p, axis=-1, keepdims=True)
            num = num * alpha + _dot(p, v)
            m = m_new
    return num / denom


def _softmax_pv_joint(q, parts, scale):
    c = scale * LOG2E
    logits = [_dot_nt(q, k) * c if bias is None else _dot_nt(q, k) * c + bias for k, bias, _ in parts]
    m = functools.reduce(jnp.maximum, [jnp.max(x, axis=-1, keepdims=True) for x in logits])
    ps = [jnp.exp2(x - m) for x in logits]
    denom = functools.reduce(lambda a, b: a + b, [jnp.sum(p, axis=-1, keepdims=True) for p in ps])
    num = functools.reduce(lambda a, b: a + b, [_dot(p, v) for p, (_, _, v) in zip(ps, parts)])
    return num / denom


def _na_kernel(idx_ref, q_ref, k_ref, v_ref, kc_ref, vc_ref, tab_ref, o_ref, bias_ref, *, rows, scale):
    j = pl.program_id(2)
    npair = NA_KROWS // 2
    start = pl.multiple_of(jnp.clip(NA_QROWS * j - NA_WIN_ROWS // 2, 0, rows - NA_KROWS) * GRID_W, GRID_W)
    win = pl.ds(start, NA_KROWS * GRID_W)
    for hd in range(NA_STEP_HEADS):
        sl = slice(hd * LANES, (hd + 1) * LANES)
        for qr in range(NA_QROWS):
            for m in range(npair):
                tile = tab_ref[hd, idx_ref[(j * NA_QROWS + qr) * npair + m]]
                bias_ref[hd, qr * GRID_W:(qr + 1) * GRID_W, m * LANES:(m + 1) * LANES] = tile
        parts = [(k_ref[win, sl], bias_ref[hd], v_ref[win, sl]), (kc_ref[:, sl], None, vc_ref[:, sl])]
        o_ref[:, sl] = _softmax_pv_joint(q_ref[:, sl], parts, scale)


def _na_tile_ids(rows):
    kh = min(NA_WIN_ROWS, rows)
    n = 2 * NA_WIN_ROWS - 1
    ids = []
    for j in range(rows // NA_QROWS):
        ks = int(np.clip(NA_QROWS * j - NA_WIN_ROWS // 2, 0, rows - NA_KROWS))
        for qr in range(NA_QROWS):
            r = NA_QROWS * j + qr
            r0 = int(np.clip(r - kh // 2, 0, rows - kh))
            for m in range(NA_KROWS // 2):
                kr = ks + 2 * m
                dy = kr - r + NA_WIN_ROWS - 1
                first, second = r0 <= kr < r0 + kh, r0 <= kr + 1 < r0 + kh
                ids.append(dy if first and second else n + dy if first else 2 * n + dy + 1 if second else 3 * n)
    return np.array(ids, np.int32)


def _na_bias_tiles(rpb):
    col = np.arange(GRID_W)
    cs = np.clip(col - NA_WIN_COLS // 2, 0, GRID_W - NA_WIN_COLS)
    kc = col[None, :]
    in_win = (kc >= cs[:, None]) & (kc < cs[:, None] + NA_WIN_COLS)
    dx = np.where(in_win, kc - col[:, None] + NA_WIN_COLS - 1, 0)
    pick = jnp.asarray(dx[:, :, None] == np.arange(rpb.shape[-1]), F32)
    picked = jnp.einsum("hyj,cqj->hycq", rpb, pick, precision=lax.Precision.HIGHEST)
    tab = jnp.where(in_win[None, None], picked * LOG2E, NEG_INF)
    neg = jnp.full_like(tab[:, :1], NEG_INF)
    nxt = jnp.concatenate([tab[:, 1:], neg], axis=1)
    both = jnp.concatenate([tab, nxt], axis=-1)
    first = jnp.concatenate([tab, jnp.broadcast_to(neg, tab.shape)], axis=-1)
    second = jnp.concatenate([jnp.broadcast_to(neg, tab.shape), tab], axis=-1)
    return jnp.concatenate([both, first, second, jnp.concatenate([neg, neg], axis=-1)], axis=1)


def na_attention(z, z_c, rpb, batch, t, l):
    rows = t // GRID_W
    nj = rows // NA_QROWS
    tq = NA_QROWS * GRID_W
    ids = jnp.asarray(_na_tile_ids(rows))
    tiles = _na_bias_tiles(rpb)
    nh = NA_STEP_HEADS
    wide = nh * LANES
    q0, k0, v0 = (EV_Q_BLK + i * NA_HEADS for i in range(3))
    assert q0 % nh == 0 and NA_HEADS % nh == 0, "head groups must start on a whole column block"
    return pl.pallas_call(
        functools.partial(_na_kernel, rows=rows, scale=NA_HEAD_DIM ** -0.5),
        out_shape=jax.ShapeDtypeStruct((batch * t, NA_WIDTH), F32),
        grid_spec=pltpu.PrefetchScalarGridSpec(
            num_scalar_prefetch=1,
            grid=(NA_HEADS // nh, batch, nj),
            in_specs=[pl.BlockSpec((tq, wide), lambda h, b, j, g: (b * nj + j, q0 // nh + h)),
                      pl.BlockSpec((t, wide), lambda h, b, j, g: (b, k0 // nh + h)),
                      pl.BlockSpec((t, wide), lambda h, b, j, g: (b, v0 // nh + h)),
                      pl.BlockSpec((l, wide), lambda h, b, j, g: (b, k0 // nh + h)),
                      pl.BlockSpec((l, wide), lambda h, b, j, g: (b, v0 // nh + h)),
                      pl.BlockSpec((nh,) + tiles.shape[1:], lambda h, b, j, g: (h, 0, 0, 0))],
            out_specs=pl.BlockSpec((tq, wide), lambda h, b, j, g: (b * nj + j, h)),
            scratch_shapes=[pltpu.VMEM((nh, tq, NA_KROWS * GRID_W), F32)]),
        compiler_params=_cparams(3),
        name="na_attention",
    )(ids, z, z, z, z_c, z_c, tiles)


def _dense_attn_kernel(q_ref, k_ref, v_ref, o_ref, *, scale):
    for hd in range(NA_STEP_HEADS):
        sl = slice(hd * LANES, (hd + 1) * LANES)
        o_ref[:, sl] = _softmax_pv(q_ref[:, sl], [(k_ref[:, sl], None, v_ref[:, sl])], scale)


def ctx_attention(z_c, batch, l):
    nh = NA_STEP_HEADS
    wide = nh * LANES
    q0, k0, v0 = (EV_Q_BLK + i * NA_HEADS for i in range(3))
    assert q0 % nh == 0 and NA_HEADS % nh == 0, "head groups must start on a whole column block"
    return pl.pallas_call(
        functools.partial(_dense_attn_kernel, scale=NA_HEAD_DIM ** -0.5),
        out_shape=jax.ShapeDtypeStruct((batch * l, NA_WIDTH), F32),
        grid=(batch, NA_HEADS // nh),
        in_specs=[pl.BlockSpec((l, wide), lambda b, h: (b, q0 // nh + h)),
                  pl.BlockSpec((l, wide), lambda b, h: (b, k0 // nh + h)),
                  pl.BlockSpec((l, wide), lambda b, h: (b, v0 // nh + h))],
        out_specs=pl.BlockSpec((l, wide), lambda b, h: (b, h)),
        compiler_params=_cparams(2),
        name="ctx_attention",
    )(z_c, z_c, z_c)


def _head_sum(x):
    first = lax.broadcasted_iota(jnp.int32, x.shape, 1) < RW_HEAD_DIM
    s0 = jnp.sum(jnp.where(first, x, 0.0), axis=-1, keepdims=True)
    s1 = jnp.sum(jnp.where(first, 0.0, x), axis=-1, keepdims=True)
    return jnp.where(first, s0, s1)


def _rw_terms_kernel(ur_ref, uk_ref, uv_ref, uwa_ref, pr_ref, pk_ref, pv_ref, pwa_ref, nr_ref, nk_ref, nv_ref,
                     nwa_ref, mu_ref, w0_ref, w2_ref, a0_ref, a2_ref, kk_ref, ka_ref, rk_ref,
                     rv_o, wk0_o, wk1_o, bk0_o, bk1_o, bonus_o, *, nt):
    i = pl.program_id(1)
    tm = ur_ref.shape[0]

    def shifted(u_ref, p_ref, n_ref, col0):
        u = u_ref[...]
        mu = mu_ref[:, col0:col0 + u.shape[1]]
        row = lax.broadcasted_iota(jnp.int32, u.shape, 0)
        before = jnp.where(i == 0, 0.0, p_ref[7:8, :])
        after = jnp.where(i == nt - 1, 0.0, n_ref[0:1, :])
        prev = jnp.where(row == 0, before, pltpu.roll(u, 1, axis=0))
        nxt = jnp.where(row == tm - 1, after, pltpu.roll(u, tm - 1, axis=0))
        return u + mu[0:1] * (prev - u) + mu[1:2] * (nxt - u)

    r_all = shifted(ur_ref, pr_ref, nr_ref, 0)
    k_all = shifted(uk_ref, pk_ref, nk_ref, RW_WIDTH)
    v_all = shifted(uv_ref, pv_ref, nv_ref, 2 * RW_WIDTH)
    wa = shifted(uwa_ref, pwa_ref, nwa_ref, 3 * RW_WIDTH)
    wd = jnp.tanh(wa[:, :LANES]).astype(BF16)
    ad = wa[:, LANES:].astype(BF16)
    first = lax.broadcasted_iota(jnp.int32, (tm, LANES), 1) < RW_HEAD_DIM

    def head_rows(x, y):
        return (jnp.where(first, x, pltpu.roll(y, RW_HEAD_DIM, axis=1)),
                jnp.where(first, pltpu.roll(x, RW_HEAD_DIM, axis=1), y))

    outs = (rv_o, wk0_o, wk1_o, bk0_o, bk1_o)
    even = [[] for _ in outs]
    odd = [[] for _ in outs]
    bonus = []
    for p in range(RW_WIDTH // LANES):
        sl = slice(p * LANES, (p + 1) * LANES)
        r, k, v = r_all[:, sl], k_all[:, sl], v_all[:, sl]
        kk = k * kk_ref[:, sl]
        kk = kk / jnp.maximum(jnp.sqrt(_head_sum(kk * kk)), 1e-12)
        kd_sum = jnp.zeros_like(k)
        per_dir = []
        for d in range(2):
            w = w0_ref[d:d + 1, sl] + jnp.dot(wd, w2_ref[d, :, sl].astype(BF16), preferred_element_type=F32)
            a = _sigmoid(a0_ref[d:d + 1, sl] + jnp.dot(ad, a2_ref[d, :, sl].astype(BF16), preferred_element_type=F32))
            kd = k * (1.0 + (a - 1.0) * ka_ref[:, sl])
            dec = jnp.exp(-RW_DECAY_SCALE * _sigmoid(w))
            per_dir.append(((dec, kd), (kk * a, kk)))
            kd_sum = kd_sum + kd
        pairs = [(r, v), per_dir[0][0], per_dir[1][0], per_dir[0][1], per_dir[1][1]]
        bonus.append(_head_sum(r * kd_sum * rk_ref[:, sl]) * v)
        for ev, od, (x, y) in zip(even, odd, pairs):
            e, o = head_rows(x, y)
            ev.append(e)
            od.append(o)
    for o_ref, ev, od in zip(outs, even, odd):
        o_ref[...] = jnp.swapaxes(jnp.stack(ev + od, axis=0), 0, 1)
    bonus_o[...] = jnp.swapaxes(jnp.stack(bonus, axis=0), 0, 1)


def rwkv_terms(z, n_seq, t, mu, w0, w2p, a0, a2p, k_k, k_a, r_k, tm=256):
    tm = min(tm, t)
    nt = t // tm
    npair = RW_WIDTH // LANES
    wide0 = EV_RW_BLK * LANES // RW_WIDTH
    wa_blk = (EV_RW_BLK + 3 * npair) // 2
    n8 = tm // 8

    def cur(width, cb):
        return pl.BlockSpec((tm, width), lambda b, i: (b * nt + i, cb))

    def before(width, cb):
        return pl.BlockSpec((8, width), lambda b, i: (jnp.maximum((b * nt + i) * n8 - 1, 0), cb))

    def after(width, cb):
        return pl.BlockSpec((8, width), lambda b, i: (jnp.minimum((b * nt + i + 1) * n8, n_seq * nt * n8 - 1), cb))

    cols = [(RW_WIDTH, wide0), (RW_WIDTH, wide0 + 1), (RW_WIDTH, wide0 + 2), (2 * LANES, wa_blk)]
    whole = lambda a: pl.BlockSpec(a.shape, lambda b, i: (0,) * a.ndim)
    params = (mu, w0, w2p, a0, a2p, k_k, k_a, r_k)
    out = pl.BlockSpec((tm, npair, LANES), lambda b, i: (i, b, 0))
    out2 = pl.BlockSpec((tm, 2 * npair, LANES), lambda b, i: (i, b, 0))
    return pl.pallas_call(
        functools.partial(_rw_terms_kernel, nt=nt),
        out_shape=[jax.ShapeDtypeStruct((t, n_seq * 2 * npair, LANES), F32)] * 5
        + [jax.ShapeDtypeStruct((t, n_seq * npair, LANES), F32)],
        grid=(n_seq, nt),
        in_specs=([cur(*c) for c in cols] + [before(*c) for c in cols] + [after(*c) for c in cols]
                  + [whole(a) for a in params]),
        out_specs=[out2] * 5 + [out],
        compiler_params=_cparams(2),
        name="rwkv_terms",
    )(*([z] * 12), *params)


def _to_lanes(n):
    q = jnp.concatenate([n, pltpu.roll(n, RW_HEAD_DIM, axis=1)], axis=0)
    return q.T[:RW_HEAD_DIM]


def _rw_scan_kernel(rvf, rvb, wk0, wk1, bk0, bk1, s0_ref, yf_ref, yb_ref, sfin_ref,
                    s_ref, z_ref, znext_ref, g_ref, ybuf_ref, *, chunk):
    n = RW_HEAD_DIM
    nb = rvf.shape[1] // (2 * RW_WIDTH // LANES)
    R, K, V, A, B = range(5)

    @pl.when(pl.program_id(0) == 0)
    def _():
        s_ref[...] = s0_ref[...]

    def convert(t_src):
        def pair_tiles(f, bw):
            both = (f[t_src], bw[chunk - 1 - t_src])
            x = jnp.concatenate([src[(b * 2 + par) * 8:(b * 2 + par) * 8 + 8]
                                 for par in range(2) for src in both for b in range(nb)], axis=0)
            xt = x.T
            return xt[:n], xt[n:]

        r, v = pair_tiles(rvf, rvb)
        w, k = pair_tiles(wk0, wk1)
        b, kk = pair_tiles(bk0, bk1)
        g_prev = g_ref[0]
        g = g_prev * w
        g_ref[0] = g
        inv = 1.0 / g
        znext_ref[R] = r * g
        znext_ref[K] = k * inv
        znext_ref[V] = v
        znext_ref[A] = -(kk * g_prev)
        znext_ref[B] = b * inv

    def emit_y(t):
        rows = _to_lanes(ybuf_ref[...])
        yf_ref[t] = rows[:n // 2]
        yb_ref[chunk - 1 - t] = rows[n // 2:]

    g_ref[0] = jnp.ones((n, LANES), F32)
    ybuf_ref[...] = jnp.zeros((n, LANES), F32)
    convert(0)

    def step(t, carry):
        emit_y(jnp.maximum(t - 1, 0))
        z_ref[...] = znext_ref[...]
        g_ref[1] = g_ref[0]
        convert(jnp.minimum(t + 1, chunk - 1))
        sa = [jnp.zeros((n, LANES), F32), jnp.zeros((n, LANES), F32)]
        for k in range(n):
            sa[k % 2] = sa[k % 2] + s_ref[k] * z_ref[A, k:k + 1, :]
        sa = sa[0] + sa[1]
        vt = z_ref[V]
        y = [jnp.zeros((n, LANES), F32), jnp.zeros((n, LANES), F32)]
        for k in range(n):
            sk = s_ref[k] + sa * z_ref[B, k:k + 1, :] + vt * z_ref[K, k:k + 1, :]
            s_ref[k] = sk
            y[k % 2] = y[k % 2] + sk * z_ref[R, k:k + 1, :]
        ybuf_ref[...] = y[0] + y[1]
        return carry

    lax.fori_loop(0, chunk, step, 0)
    emit_y(chunk - 1)
    for k in range(n):
        s_ref[k] = s_ref[k] * g_ref[1, k:k + 1, :]
    sfin_ref[...] = s_ref[...]


def rwkv_scan(rv, wk0, wk1, bk0, bk1, s0):
    t, nrow, _ = rv.shape
    n = RW_HEAD_DIM
    chunk = RW_SCAN_CHUNK
    nc = t // chunk
    assert nrow == LANES // 2, "the states of both directions fill the 128 lanes"
    fwd = pl.BlockSpec((chunk, nrow, LANES), lambda i: (i, 0, 0))
    bwd = pl.BlockSpec((chunk, nrow, LANES), lambda i: (nc - 1 - i, 0, 0))
    yfwd = pl.BlockSpec((chunk, nrow // 2, LANES), lambda i: (i, 0, 0))
    ybwd = pl.BlockSpec((chunk, nrow // 2, LANES), lambda i: (nc - 1 - i, 0, 0))
    state = pl.BlockSpec((n, n, LANES), lambda i: (0, 0, 0))
    slab = jax.ShapeDtypeStruct((t, nrow // 2, LANES), F32)
    return pl.pallas_call(
        functools.partial(_rw_scan_kernel, chunk=chunk),
        out_shape=[slab, slab, jax.ShapeDtypeStruct((n, n, LANES), F32)],
        grid=(nc,),
        in_specs=[fwd, bwd, fwd, bwd, fwd, bwd, state],
        out_specs=[yfwd, ybwd, state],
        scratch_shapes=[pltpu.VMEM((n, n, LANES), F32), pltpu.VMEM((5, n, LANES), F32),
                        pltpu.VMEM((5, n, LANES), F32), pltpu.VMEM((2, n, LANES), F32),
                        pltpu.VMEM((n, LANES), F32)],
        compiler_params=_cparams(1),
        name="rwkv_scan",
    )(rv, rv, wk0, wk1, bk0, bk1, s0)


def _rw_readout_kernel(yf_ref, yb_ref, bonus_ref, lw_ref, lb_ref, o_ref):
    npair = RW_WIDTH // LANES
    y_all = jnp.swapaxes(yf_ref[...] + yb_ref[...], 0, 1)
    bonus_all = jnp.swapaxes(bonus_ref[...], 0, 1)
    for q in range(yf_ref.shape[1]):
        b, p = divmod(q, npair)
        sl = slice(p * LANES, (p + 1) * LANES)
        y = y_all[q]
        mu = _head_sum(y) * (1.0 / RW_HEAD_DIM)
        d = y - mu
        var = _head_sum(d * d) * (1.0 / RW_HEAD_DIM)
        o_ref[b, :, sl] = d * lax.rsqrt(var + RW_GN_EPS) * lw_ref[:, sl] + lb_ref[:, sl] + bonus_all[q]


def rwkv_readout(y_f, y_b, bonus, ln_w, ln_b, tm=256):
    t, nq, _ = y_f.shape
    tm = min(tm, t)
    batch = nq // (RW_WIDTH // LANES)
    slab = pl.BlockSpec((tm, nq, LANES), lambda i: (i, 0, 0))
    vec = pl.BlockSpec((1, RW_WIDTH), lambda i: (0, 0))
    return pl.pallas_call(
        _rw_readout_kernel,
        out_shape=jax.ShapeDtypeStruct((batch, t, RW_WIDTH), F32),
        grid=(t // tm,),
        in_specs=[slab, slab, slab, vec, vec],
        out_specs=pl.BlockSpec((batch, tm, RW_WIDTH), lambda i: (0, i, 0)),
        compiler_params=_cparams(1),
        name="rwkv_readout",
    )(y_f, y_b, bonus, ln_w.reshape(1, -1), ln_b.reshape(1, -1))


def _rms(x, w):
    return x * lax.rsqrt(jnp.mean(x * x, axis=-1, keepdims=True) + NORM_EPS) * w


def _rope(x, cos, sin):
    odd = (lax.broadcasted_iota(jnp.int32, x.shape, 1) & 1) == 1
    swapped = jnp.where(odd, pltpu.roll(x, 1, axis=1), pltpu.roll(x, LANES - 1, axis=1))
    return x * cos + swapped * sin


def _mla_q_kernel(c_ref, nw_ref, w_ref, cos_ref, sin_ref, q_ref):
    h = _rms(c_ref[...], nw_ref[...]).astype(BF16)
    acc = jnp.dot(h, w_ref[...], preferred_element_type=F32)
    cos, sin = cos_ref[...], sin_ref[...]
    for hd in range(MLA_HEADS):
        lo = hd * MLA_QK_PAD
        q_ref[:, lo:lo + MLA_NOPE] = acc[:, lo:lo + MLA_NOPE].astype(BF16)
        q_ref[:, lo + MLA_NOPE:lo + MLA_QK_PAD] = _rope(acc[:, lo + MLA_NOPE:lo + MLA_QK_PAD], cos, sin).astype(BF16)


def mla_queries(z, q_norm, w_uq_pad, cos, sin, t, tm=512):
    m = z.shape[0]
    tm = min(tm, t)
    nt = t // tm
    width = MLA_HEADS * MLA_QK_PAD
    return pl.pallas_call(
        _mla_q_kernel,
        out_shape=jax.ShapeDtypeStruct((m, width), BF16),
        grid=(m // tm,),
        in_specs=[pl.BlockSpec((tm, MLA_Q_RANK), lambda i: (i, GATE_COLS // MLA_Q_RANK)),
                  pl.BlockSpec((1, MLA_Q_RANK), lambda i: (0, 0)),
                  pl.BlockSpec((MLA_Q_RANK, width), lambda i: (0, 0)),
                  pl.BlockSpec((tm, LANES), lambda i: (i % nt, 0)),
                  pl.BlockSpec((tm, LANES), lambda i: (i % nt, 0))],
        out_specs=pl.BlockSpec((tm, width), lambda i: (i, 0)),
        compiler_params=_cparams(1),
        name="mla_queries",
    )(z, q_norm.reshape(1, -1), w_uq_pad, cos, sin)


def _mla_kv_kernel(c_ref, pe_ref, nw_ref, w_ref, cos_ref, sin_ref, k_ref, v_ref):
    h = _rms(c_ref[...], nw_ref[...]).astype(BF16)
    acc = jnp.dot(h, w_ref[...], preferred_element_type=F32)
    pe = _rope(pe_ref[...], cos_ref[...], sin_ref[...]).astype(BF16)
    for hd in range(MLA_HEADS):
        src = hd * (MLA_NOPE + MLA_V)
        dst = hd * MLA_QK_PAD
        k_ref[:, dst:dst + MLA_NOPE] = acc[:, src:src + MLA_NOPE].astype(BF16)
        k_ref[:, dst + MLA_NOPE:dst + MLA_QK_PAD] = pe
        v_ref[:, hd * MLA_V:(hd + 1) * MLA_V] = acc[:, src + MLA_NOPE:src + MLA_NOPE + MLA_V].astype(BF16)


def mla_keys_values(z, kv_norm, w_ukv, cos, sin, t, tm=512):
    m = z.shape[0]
    tm = min(tm, t)
    nt = t // tm
    return pl.pallas_call(
        _mla_kv_kernel,
        out_shape=[jax.ShapeDtypeStruct((m, MLA_HEADS * MLA_QK_PAD), BF16),
                   jax.ShapeDtypeStruct((m, MLA_WIDTH), BF16)],
        grid=(m // tm,),
        in_specs=[pl.BlockSpec((tm, MLA_KV_RANK), lambda i: (i, (GATE_COLS + MLA_Q_RANK) // MLA_KV_RANK)),
                  pl.BlockSpec((tm, LANES), lambda i: (i, OD_A_BLK + (MLA_Q_RANK + MLA_KV_RANK) // LANES)),
                  pl.BlockSpec((1, MLA_KV_RANK), lambda i: (0, 0)),
                  pl.BlockSpec(w_ukv.shape, lambda i: (0, 0)),
                  pl.BlockSpec((tm, LANES), lambda i: (i % nt, 0)),
                  pl.BlockSpec((tm, LANES), lambda i: (i % nt, 0))],
        out_specs=[pl.BlockSpec((tm, MLA_HEADS * MLA_QK_PAD), lambda i: (i, 0)),
                   pl.BlockSpec((tm, MLA_WIDTH), lambda i: (i, 0))],
        compiler_params=_cparams(1),
        name="mla_keys_values",
    )(z, z, kv_norm.reshape(1, -1), w_ukv, cos, sin)


def _mla_attn_kernel(q_ref, k_ref, v_ref, kc_ref, vc_ref, o_ref, *, scale):
    parts = [(kc_ref[...], None, vc_ref[...])]
    nkeys = k_ref.shape[0]
    step = min(MLA_KEY_CHUNK, nkeys)
    for lo in range(0, nkeys, step):
        parts.append((k_ref[lo:lo + step, :], None, v_ref[lo:lo + step, :]))
    o_ref[...] = _softmax_pv(q_ref[...], parts, scale)


def mla_attention(q, k, v, kc, vc, batch, t, l, tq=2048):
    tq = min(tq, t)
    nq = t // tq
    return pl.pallas_call(
        functools.partial(_mla_attn_kernel, scale=(MLA_NOPE + MLA_ROPE) ** -0.5),
        out_shape=jax.ShapeDtypeStruct((batch * t, MLA_WIDTH), F32),
        grid=(batch, MLA_HEADS, nq),
        in_specs=[pl.BlockSpec((tq, MLA_QK_PAD), lambda b, h, i: (b * nq + i, h)),
                  pl.BlockSpec((t, MLA_QK_PAD), lambda b, h, i: (b, h)),
                  pl.BlockSpec((t, MLA_V), lambda b, h, i: (b, h)),
                  pl.BlockSpec((l, MLA_QK_PAD), lambda b, h, i: (b, h)),
                  pl.BlockSpec((l, MLA_V), lambda b, h, i: (b, h))],
        out_specs=pl.BlockSpec((tq, MLA_V), lambda b, h, i: (b * nq + i, h)),
        compiler_params=_cparams(3),
        name="mla_attention",
    )(q, k, v, kc, vc)


def _rope_tables(t):
    tok = np.arange(t)
    pos = np.stack([tok // GRID_W, tok % GRID_W], axis=-1).astype(np.float32)
    n_freq = MLA_ROPE // 4
    inv = (ROPE_BASE ** (-jnp.arange(n_freq, dtype=F32) / n_freq))
    ang = (jnp.asarray(pos)[:, :, None] * inv).reshape(t, MLA_ROPE // 2)
    cos = jnp.repeat(jnp.cos(ang), 2, axis=-1)
    sin = jnp.repeat(jnp.sin(ang), 2, axis=-1) * jnp.tile(jnp.array([-1.0, 1.0], F32), MLA_ROPE // 2)
    pad = ((0, 0), (0, LANES - MLA_ROPE))
    return jnp.pad(cos, pad), jnp.pad(sin, pad)


def _split3(x):
    hi = x.astype(BF16)
    r1 = x - hi.astype(F32)
    mid = r1.astype(BF16)
    lo = (r1 - mid.astype(F32)).astype(BF16)
    return hi, mid, lo


def _hg_kernel(q_ref, f_ref, v_ref, lb_ref, s0_ref, *rest, reverse, finalize, emit_state):
    st_ref = rest[-1]
    rest = rest[:-1]
    if finalize:
        prev_ref, nw_ref = rest[0], rest[1]
        rest = rest[2:]
    o_ref = rest[0]

    @pl.when(pl.program_id(2) == 0)
    def _():
        st_ref[...] = s0_ref[...]

    bt = q_ref.shape[0]
    nchunk = bt // HG_CHUNK
    order = range(nchunk - 1, -1, -1) if reverse else range(nchunk)
    ri = lax.broadcasted_iota(jnp.int32, (bt, bt), 0)
    ci = lax.broadcasted_iota(jnp.int32, (bt, bt), 1)
    same = (ri // HG_CHUNK) == (ci // HG_CHUNK)
    tri = same & ((ci >= ri) if reverse else (ci <= ri))
    tri_f = tri.astype(F32)
    tri_b = tri.astype(BF16)
    row_chunk = lax.broadcasted_iota(jnp.int32, (bt, HG_DK), 0) // HG_CHUNK
    in_chunk = [row_chunk == c for c in range(nchunk)]

    lb = lb_ref[...]
    fz = f_ref[...]
    log_sig = jnp.minimum(fz, 0.0) - jnp.log1p(jnp.exp(-jnp.abs(fz)))
    la = jnp.log(lb)
    lbb = jnp.log1p(-lb) + log_sig
    log_f = jnp.maximum(la, lbb) + jnp.log1p(jnp.exp(-jnp.abs(la - lbb)))
    kf_all = (1.0 - lb) * _sigmoid(-fz)
    pieces = jnp.dot(tri_b, jnp.concatenate(_split3(log_f), axis=1), preferred_element_type=F32)
    width = log_f.shape[1]
    cum_all = pieces[:, :width] + pieces[:, width:2 * width] + pieces[:, 2 * width:]

    for hh in range(HG_STEP_HEADS):
        sl = slice(hh * HG_DK, (hh + 1) * HG_DK)
        cum, kf, v = cum_all[:, sl], kf_all[:, sl], v_ref[:, sl]
        last = [cum[c * HG_CHUNK:c * HG_CHUNK + 1] if reverse else cum[(c + 1) * HG_CHUNK - 1:(c + 1) * HG_CHUNK]
                for c in range(nchunk)]
        tot = jnp.concatenate([jnp.broadcast_to(x, (HG_CHUNK, HG_DK)) for x in last], axis=0)
        q_in = q_ref[:, sl] * jnp.exp(cum)
        k_in = kf * jnp.exp(-cum)
        k_end = kf * jnp.exp(tot - cum)
        o_intra = _dot(_dot_nt(q_in, k_in) * tri_f, v)
        v_by_chunk = jnp.concatenate([jnp.where(in_chunk[c], v, 0.0) for c in range(nchunk)], axis=1)
        d_st = _dot_tn(v_by_chunk, k_end)
        st = st_ref[hh]
        before = [None] * nchunk
        for c in order:
            before[c] = st
            st = st * jnp.exp(last[c]) + d_st[c * HG_DV:(c + 1) * HG_DV]
        st_ref[hh] = st
        q_by_chunk = jnp.concatenate([jnp.where(in_chunk[c], q_in, 0.0) for c in range(nchunk)], axis=1)
        o = o_intra + _dot_nt(q_by_chunk, jnp.concatenate(before, axis=1))
        if finalize:
            o = o + prev_ref[:, sl]
            o = o * lax.rsqrt(jnp.mean(o * o, axis=-1, keepdims=True) + NORM_EPS) * nw_ref[...]
        o_ref[:, sl] = o
        if emit_state:
            rest[1][hh] = st


def hgrn2_pass(z, lb, s0, batch, t, f_off, reverse, prev=None, norm_w=None, emit_state=False):
    bt = min(HG_BLOCK, t)
    nblk = t // bt
    nh = HG_STEP_HEADS
    wide = nh * LANES
    finalize = prev is not None
    assert OD_HG_BLK % nh == 0 and HG_HEADS % nh == 0, "head groups must start on a whole column block"
    pos = (lambda i: nblk - 1 - i) if reverse else (lambda i: i)
    blk = lambda off: pl.BlockSpec((bt, wide), lambda b, h, i, off=off: (b * nblk + pos(i), off // nh + h))
    state = pl.BlockSpec((None, nh, HG_DV, HG_DK), lambda b, h, i: (b, h, 0, 0))
    ins = [z, z, z, lb.reshape(1, -1), s0]
    specs = [blk(OD_HG_BLK), blk(OD_HG_BLK + f_off), blk(OD_HG_BLK + 3 * HG_HEADS),
             pl.BlockSpec((1, wide), lambda b, h, i: (0, h)), state]
    if finalize:
        ins += [prev, norm_w.reshape(1, -1)]
        specs += [blk(0), pl.BlockSpec((1, HG_DV), lambda b, h, i: (0, 0))]
    out_shape = [jax.ShapeDtypeStruct((batch * t, HG_WIDTH), F32)]
    out_specs = [blk(0)]
    if emit_state:
        out_shape.append(jax.ShapeDtypeStruct((batch, HG_HEADS, HG_DV, HG_DK), F32))
        out_specs.append(state)
    return pl.pallas_call(
        functools.partial(_hg_kernel, reverse=reverse, finalize=finalize, emit_state=emit_state),
        out_shape=out_shape,
        grid=(batch, HG_HEADS // nh, nblk),
        in_specs=specs,
        out_specs=out_specs,
        scratch_shapes=[pltpu.VMEM((nh, HG_DV, HG_DK), F32)],
        compiler_params=_cparams(3),
        name="hgrn2_" + ("bwd" if reverse else "fwd"),
    )(*ins)


def _even_layer(x, xc, mod, norm_w, w_in, w_out, rpb, rw_params, batch, t, l):
    s2 = 3 * NA_WIDTH + RW_SHIFT_COLS
    width = RW_PAD_COLS - RW_SHIFT_COLS
    n_gate, n_rest = GATE_COLS // width, s2 // width
    plan = [n_rest + i for i in range(n_gate)] + list(range(n_rest)) + [None]
    w_ext = cast_reorder(w_in, plan, width, "even_w_in")
    tiles_per_seq = max(t // 1024, 1)
    z = in_projection(x, norm_w, mod, lambda i: i // tiles_per_seq, w_ext, "even_in_proj")
    z_c = in_projection(xc, norm_w, mod, lambda i: batch, w_ext, "even_in_proj_ctx")

    y_na = na_attention(z, z_c, rpb, batch, t, l)
    yc_na = ctx_attention(z_c, batch, l)

    mu, w0, w2, a0, a2, k_k, k_a, r_k, ln_w, ln_b = rw_params
    mu = jnp.pad(mu, ((0, 0), (0, RW_PAD_COLS - RW_SHIFT_COLS)))
    zpad = jnp.zeros((RW_RANK, RW_WIDTH), F32)
    w2p = jnp.stack([jnp.concatenate([w2[0], zpad]), jnp.concatenate([zpad, w2[1]])])
    a2p = jnp.stack([jnp.concatenate([a2[0], zpad]), jnp.concatenate([zpad, a2[1]])])
    vecs = (mu, w0, w2p, a0, a2p, k_k.reshape(1, -1), k_a.reshape(1, -1), r_k.reshape(1, -1))
    terms = rwkv_terms(z, batch, t, *vecs)
    terms_c = rwkv_terms(z_c, batch, l, *vecs)
    s_zero = jnp.zeros((RW_HEAD_DIM, RW_HEAD_DIM, LANES), F32)
    yc_f, yc_b, s_ctx = rwkv_scan(*terms_c[:5], s_zero)
    y_f, y_b, _ = rwkv_scan(*terms[:5], s_ctx)
    y_rw = rwkv_readout(y_f, y_b, terms[5], ln_w, ln_b).reshape(batch * t, RW_WIDTH)
    yc_rw = rwkv_readout(yc_f, yc_b, terms_c[5], ln_w, ln_b).reshape(batch * l, RW_WIDTH)

    w_out = w_out.astype(BF16)
    rows_per_seq = max(t // 256, 1)
    x = out_projection(y_na, y_rw, z, w_out, x, mod, lambda i: i // rows_per_seq, "even_out_proj")
    xc = out_projection(yc_na, yc_rw, z_c, w_out, xc, mod, lambda i: batch, "even_out_proj_ctx")
    return x, xc


def _odd_layer(x, xc, mod, norm_w, w_in, w_out, q_norm, w_uq, kv_norm, w_ukv, lb, hg_norm_w, final_w,
               batch, t, l):
    o3 = MLA_Q_RANK + MLA_KV_RANK + MLA_ROPE
    o4 = o3 + 4 * HG_FDIM
    zero = jnp.zeros((x.shape[1], MLA_A_COLS - o3), BF16)
    w_in = w_in.astype(BF16)
    w_ext = jnp.concatenate([w_in[:, o4:], w_in[:, :o3], zero, w_in[:, o3:o4]], axis=1)
    tiles_per_seq = max(t // 1024, 1)
    z = in_projection(x, norm_w, mod, lambda i: i // tiles_per_seq, w_ext, "odd_in_proj")
    z_c = in_projection(xc, norm_w, mod, lambda i: batch, w_ext, "odd_in_proj_ctx")

    cos, sin = _rope_tables(t)
    qk = MLA_NOPE + MLA_ROPE
    w_uq_pad = jnp.pad(w_uq.reshape(MLA_Q_RANK, MLA_HEADS, qk), ((0, 0), (0, 0), (0, MLA_QK_PAD - qk)))
    w_uq_pad = w_uq_pad.reshape(MLA_Q_RANK, MLA_HEADS * MLA_QK_PAD).astype(BF16)
    w_ukv = w_ukv.astype(BF16)
    q = mla_queries(z, q_norm, w_uq_pad, cos, sin, t)
    k, v = mla_keys_values(z, kv_norm, w_ukv, cos, sin, t)
    ones = jnp.ones((l, LANES), F32)
    kc, vc = mla_keys_values(z_c, kv_norm, w_ukv, ones, jnp.zeros_like(ones), l)
    y_mla = mla_attention(q, k, v, kc, vc, batch, t, l)

    s_zero = jnp.zeros((batch, HG_HEADS, HG_DV, HG_DK), F32)
    _, s_f = hgrn2_pass(z_c, lb, s_zero, batch, l, HG_HEADS, False, emit_state=True)
    _, s_b = hgrn2_pass(z_c, lb, s_zero, batch, l, 2 * HG_HEADS, True, emit_state=True)
    o_f, = hgrn2_pass(z, lb, s_f, batch, t, HG_HEADS, False)
    y_hg, = hgrn2_pass(z, lb, s_b, batch, t, 2 * HG_HEADS, True, prev=o_f, norm_w=hg_norm_w)

    rows_per_seq = max(t // 256, 1)
    return out_projection(y_mla, y_hg, z, w_out.astype(BF16), x, mod, lambda i: i // rows_per_seq,
                          "odd_out_proj", final_w=final_w)


def kernel(x, c, ctx, c_ctx, ada_w, ada_b, norm_w, e_w_in, e_w_out, na_rpb, rw_mu, rw_w0, rw_w2, rw_a0, rw_a2, rw_k_k, rw_k_a, rw_r_k, rw_ln_w, rw_ln_b, o_w_in, o_w_out, mla_q_norm, mla_w_uq, mla_kv_norm, mla_w_ukv, hg_lower_bounds, hg_norm_w, final_norm_w):
    batch, t, d = x.shape
    l = ctx.shape[1]
    assert ada_w.shape[0] == 2, "one even and one odd layer"
    cond = jnp.concatenate([c, c_ctx[None, :], jnp.zeros((8 - batch - 1, d), F32)], axis=0)
    mod = modulation(cond, ada_w, ada_b)
    s = jax.nn.softmax(hg_lower_bounds.astype(F32), axis=0)
    lower = jnp.cumsum(s, axis=0) - s[0]

    xf, xcf = x.reshape(batch * t, d), ctx.reshape(batch * l, d)
    rw_params = (rw_mu[0], rw_w0[0], rw_w2[0], rw_a0[0], rw_a2[0], rw_k_k[0], rw_k_a[0], rw_r_k[0],
                 rw_ln_w[0], rw_ln_b[0])
    xf, xcf = _even_layer(xf, xcf, mod[0].reshape(8, 1, 3 * d), norm_w[0], e_w_in[0], e_w_out[0], na_rpb[0],
                          rw_params, batch, t, l)
    out = _odd_layer(xf, xcf, mod[1].reshape(8, 1, 3 * d), norm_w[1], o_w_in[0], o_w_out[0], mla_q_norm[0],
                     mla_w_uq[0], mla_kv_norm[0], mla_w_ukv[0], lower[1], hg_norm_w[0], final_norm_w,
                     batch, t, l)
    return out.reshape(batch, t, d)
```

```python
import functools
import math

import numpy as np
import jax
import jax.numpy as jnp
from jax import lax
from jax.experimental import pallas as pl
from jax.experimental.pallas import tpu as pltpu

F32 = jnp.float32
BF16 = jnp.bfloat16

GRID_W = 64
NORM_EPS = 1e-6
ROPE_BASE = 10000.0

NA_HEADS = 8
NA_HEAD_DIM = 128
NA_WIDTH = NA_HEADS * NA_HEAD_DIM
NA_WIN_ROWS = 8
NA_WIN_COLS = 16
NA_QROWS = 8
NA_KROWS = 16
NA_STEP_HEADS = 4

RW_HEAD_DIM = 64
RW_HEADS = 16
RW_WIDTH = RW_HEADS * RW_HEAD_DIM
RW_RANK = 64
RW_SHIFT_COLS = 3 * RW_WIDTH + 4 * RW_RANK
RW_PAD_COLS = 3584
RW_GN_EPS = 64e-5
RW_SCAN_CHUNK = 64
RW_DECAY_SCALE = math.exp(-0.5)

MLA_HEADS = 8
MLA_Q_RANK = 512
MLA_KV_RANK = 512
MLA_NOPE = 128
MLA_ROPE = 64
MLA_V = 128
MLA_QK_PAD = 256
MLA_WIDTH = MLA_HEADS * MLA_V
MLA_A_COLS = 1536
MLA_KEY_CHUNK = 1024

HG_HEADS = 8
HG_DK = 128
HG_DV = 128
HG_FDIM = HG_HEADS * HG_DK
HG_WIDTH = HG_HEADS * HG_DV
HG_CHUNK = 32
HG_BLOCK = 256
HG_STEP_HEADS = 4

LANES = 128

GATE_COLS = 2048
EV_Q_BLK = GATE_COLS // LANES
EV_RW_BLK = EV_Q_BLK + 3 * NA_WIDTH // LANES
OD_A_BLK = GATE_COLS // LANES
OD_HG_BLK = OD_A_BLK + MLA_A_COLS // LANES
VMEM_LIMIT = 48 * 1024 * 1024
NEG_INF = -1e30
LOG2E = math.log2(math.e)


def _cparams(n_axes):
    return pltpu.CompilerParams(dimension_semantics=("arbitrary",) * n_axes, vmem_limit_bytes=VMEM_LIMIT)


def _sigmoid(x):
    return 1.0 / (1.0 + jnp.exp(-x))


def _silu(x):
    return x * _sigmoid(x)


def _dot(a, b):
    return jnp.dot(a.astype(BF16), b.astype(BF16), preferred_element_type=F32)


def _dot_nt(a, b):
    return lax.dot_general(a.astype(BF16), b.astype(BF16), (((1,), (1,)), ((), ())), preferred_element_type=F32)


def _dot_tn(a, b):
    return lax.dot_general(a.astype(BF16), b.astype(BF16), (((0,), (0,)), ((), ())), preferred_element_type=F32)


def _mod_kernel(c_ref, w_ref, b_ref, o_ref):
    o_ref[...] = _dot(_silu(c_ref[...]), w_ref[...]) + b_ref[...]


def modulation(cond, ada_w, ada_b, tn=512):
    depth, d, n = ada_w.shape
    return pl.pallas_call(
        _mod_kernel,
        out_shape=jax.ShapeDtypeStruct((depth, 8, n), F32),
        grid=(depth, n // tn),
        in_specs=[pl.BlockSpec((8, d), lambda l, j: (0, 0)),
                  pl.BlockSpec((None, d, tn), lambda l, j: (l, 0, j)),
                  pl.BlockSpec((None, 1, tn), lambda l, j: (l, 0, j))],
        out_specs=pl.BlockSpec((None, 8, tn), lambda l, j: (l, 0, j)),
        compiler_params=_cparams(2),
        name="adaln_modulation",
    )(cond, ada_w, ada_b.reshape(depth, 1, n))


def _cast_reorder_kernel(src_ref, zero_ref, x_ref, o_ref):
    del src_ref
    o_ref[...] = jnp.where(zero_ref[pl.program_id(0)] != 0, 0.0, x_ref[...]).astype(BF16)


def cast_reorder(w, plan, width, name):
    d = w.shape[0]
    src = np.array([0 if a is None else a for a in plan], np.int32)
    zero = np.array([a is None for a in plan], np.int32)
    return pl.pallas_call(
        _cast_reorder_kernel,
        out_shape=jax.ShapeDtypeStruct((d, width * len(plan)), BF16),
        grid_spec=pltpu.PrefetchScalarGridSpec(
            num_scalar_prefetch=2,
            grid=(len(plan),),
            in_specs=[pl.BlockSpec((d, width), lambda j, src, zero: (0, src[j]))],
            out_specs=pl.BlockSpec((d, width), lambda j, src, zero: (0, j))),
        compiler_params=_cparams(1),
        name=name,
    )(jnp.asarray(src), jnp.asarray(zero), w)


def _inproj_kernel(x_ref, nw_ref, shift_ref, scale_ref, w_ref, o_ref, h_ref):
    @pl.when(pl.program_id(1) == 0)
    def _():
        x = x_ref[...]
        y = x * lax.rsqrt(jnp.mean(x * x, axis=-1, keepdims=True) + NORM_EPS) * nw_ref[...]
        h_ref[...] = (y * (1.0 + scale_ref[...]) + shift_ref[...]).astype(BF16)

    o_ref[...] = jnp.dot(h_ref[...], w_ref[...], preferred_element_type=F32)


def in_projection(x, norm_w, mod, mod_row, w, name, tm=1024, tn=512):
    m, d = x.shape
    n = w.shape[1]
    tm = min(tm, m)
    return pl.pallas_call(
        _inproj_kernel,
        out_shape=jax.ShapeDtypeStruct((m, n), F32),
        grid=(m // tm, n // tn),
        in_specs=[pl.BlockSpec((tm, d), lambda i, j: (i, 0)),
                  pl.BlockSpec((1, d), lambda i, j: (0, 0)),
                  pl.BlockSpec((None, 1, d), lambda i, j: (mod_row(i), 0, 0)),
                  pl.BlockSpec((None, 1, d), lambda i, j: (mod_row(i), 0, 1)),
                  pl.BlockSpec((d, tn), lambda i, j: (0, j))],
        out_specs=pl.BlockSpec((tm, tn), lambda i, j: (i, j)),
        scratch_shapes=[pltpu.VMEM((tm, d), BF16)],
        compiler_params=_cparams(2),
        name=name,
    )(x, norm_w.reshape(1, d), mod, mod, w)


def _outproj_kernel(ya_ref, yb_ref, g_ref, w_ref, x_ref, gm_ref, *rest, ka, final):
    o_ref = rest[-1]
    sg = _silu(g_ref[...])
    acc = _dot(ya_ref[...].astype(F32) * sg[:, :ka], w_ref[:ka, :])
    acc += _dot(yb_ref[...].astype(F32) * sg[:, ka:], w_ref[ka:, :])
    out = x_ref[...] + gm_ref[...] * acc
    if final:
        out = out * lax.rsqrt(jnp.mean(out * out, axis=-1, keepdims=True) + NORM_EPS) * rest[0][...]
    o_ref[...] = out


def out_projection(ya, yb, gate, w, x, mod, mod_row, name, final_w=None, tm=256):
    m, d = x.shape
    ka, kb = ya.shape[1], yb.shape[1]
    tm = min(tm, m)
    ins = [ya, yb, gate, w, x, mod]
    specs = [pl.BlockSpec((tm, ka), lambda i: (i, 0)),
             pl.BlockSpec((tm, kb), lambda i: (i, 0)),
             pl.BlockSpec((tm, ka + kb), lambda i: (i, 0)),
             pl.BlockSpec((ka + kb, d), lambda i: (0, 0)),
             pl.BlockSpec((tm, d), lambda i: (i, 0)),
             pl.BlockSpec((None, 1, d), lambda i: (mod_row(i), 0, 2))]
    if final_w is not None:
        ins.append(final_w.reshape(1, d))
        specs.append(pl.BlockSpec((1, d), lambda i: (0, 0)))
    return pl.pallas_call(
        functools.partial(_outproj_kernel, ka=ka, final=final_w is not None),
        out_shape=jax.ShapeDtypeStruct((m, d), F32),
        grid=(m // tm,),
        in_specs=specs,
        out_specs=pl.BlockSpec((tm, d), lambda i: (i, 0)),
        compiler_params=_cparams(1),
        name=name,
    )(*ins)


def _softmax_pv(q, parts, scale):
    c = scale * LOG2E
    m = denom = num = None
    for k, bias, v in parts:
        x = _dot_nt(q, k) * c
        if bias is not None:
            x = x + bias
        m_part = jnp.max(x, axis=-1, keepdims=True)
        if m is None:
            m = m_part
            p = jnp.exp2(x - m)
            denom = jnp.sum(p, axis=-1, keepdims=True)
            num = _dot(p, v)
        else:
            m_new = jnp.maximum(m, m_part)
            alpha = jnp.exp2(m - m_new)
            p = jnp.exp2(x - m_new)
            denom = denom * alpha + jnp.sum(p, axis=-1, keepdims=True)
            num = num * alpha + _dot(p, v)
            m = m_new
    return num / denom


def _softmax_pv_joint(q, parts, scale):
    c = scale * LOG2E
    logits = [_dot_nt(q, k) * c if bias is None else _dot_nt(q, k) * c + bias for k, bias, _ in parts]
    m = functools.reduce(jnp.maximum, [jnp.max(x, axis=-1, keepdims=True) for x in logits])
    ps = [jnp.exp2(x - m) for x in logits]
    denom = functools.reduce(lambda a, b: a + b, [jnp.sum(p, axis=-1, keepdims=True) for p in ps])
    num = functools.reduce(lambda a, b: a + b, [_dot(p, v) for p, (_, _, v) in zip(ps, parts)])
    return num / denom


def _na_kernel(idx_ref, q_ref, k_ref, v_ref, kc_ref, vc_ref, tab_ref, o_ref, bias_ref, *, rows, scale):
    j = pl.program_id(2)
    npair = NA_KROWS // 2
    start = pl.multiple_of(jnp.clip(NA_QROWS * j - NA_WIN_ROWS // 2, 0, rows - NA_KROWS) * GRID_W, GRID_W)
    win = pl.ds(start, NA_KROWS * GRID_W)
    for hd in range(NA_STEP_HEADS):
        sl = slice(hd * LANES, (hd + 1) * LANES)
        for qr in range(NA_QROWS):
            for m in range(npair):
                tile = tab_ref[hd, idx_ref[(j * NA_QROWS + qr) * npair + m]]
                bias_ref[hd, qr * GRID_W:(qr + 1) * GRID_W, m * LANES:(m + 1) * LANES] = tile
        parts = [(k_ref[win, sl], bias_ref[hd], v_ref[win, sl]), (kc_ref[:, sl], None, vc_ref[:, sl])]
        o_ref[:, sl] = _softmax_pv_joint(q_ref[:, sl], parts, scale)


def _na_tile_ids(rows):
    kh = min(NA_WIN_ROWS, rows)
    n = 2 * NA_WIN_ROWS - 1
    ids = []
    for j in range(rows // NA_QROWS):
        ks = int(np.clip(NA_QROWS * j - NA_WIN_ROWS // 2, 0, rows - NA_KROWS))
        for qr in range(NA_QROWS):
            r = NA_QROWS * j + qr
            r0 = int(np.clip(r - kh // 2, 0, rows - kh))
            for m in range(NA_KROWS // 2):
                kr = ks + 2 * m
                dy = kr - r + NA_WIN_ROWS - 1
                first, second = r0 <= kr < r0 + kh, r0 <= kr + 1 < r0 + kh
                ids.append(dy if first and second else n + dy if first else 2 * n + dy + 1 if second else 3 * n)
    return np.array(ids, np.int32)


def _na_bias_tiles(rpb):
    col = np.arange(GRID_W)
    cs = np.clip(col - NA_WIN_COLS // 2, 0, GRID_W - NA_WIN_COLS)
    kc = col[None, :]
    in_win = (kc >= cs[:, None]) & (kc < cs[:, None] + NA_WIN_COLS)
    dx = np.where(in_win, kc - col[:, None] + NA_WIN_COLS - 1, 0)
    pick = jnp.asarray(dx[:, :, None] == np.arange(rpb.shape[-1]), F32)
    picked = jnp.einsum("hyj,cqj->hycq", rpb, pick, precision=lax.Precision.HIGHEST)
    tab = jnp.where(in_win[None, None], picked * LOG2E, NEG_INF)
    neg = jnp.full_like(tab[:, :1], NEG_INF)
    nxt = jnp.concatenate([tab[:, 1:], neg], axis=1)
    both = jnp.concatenate([tab, nxt], axis=-1)
    first = jnp.concatenate([tab, jnp.broadcast_to(neg, tab.shape)], axis=-1)
    second = jnp.concatenate([jnp.broadcast_to(neg, tab.shape), tab], axis=-1)
    return jnp.concatenate([both, first, second, jnp.concatenate([neg, neg], axis=-1)], axis=1)


def na_attention(z, z_c, rpb, batch, t, l):
    rows = t // GRID_W
    nj = rows // NA_QROWS
    tq = NA_QROWS * GRID_W
    ids = jnp.asarray(_na_tile_ids(rows))
    tiles = _na_bias_tiles(rpb)
    nh = NA_STEP_HEADS
    wide = nh * LANES
    q0, k0, v0 = (EV_Q_BLK + i * NA_HEADS for i in range(3))
    assert q0 % nh == 0 and NA_HEADS % nh == 0, "head groups must start on a whole column block"
    return pl.pallas_call(
        functools.partial(_na_kernel, rows=rows, scale=NA_HEAD_DIM ** -0.5),
        out_shape=jax.ShapeDtypeStruct((batch * t, NA_WIDTH), F32),
        grid_spec=pltpu.PrefetchScalarGridSpec(
            num_scalar_prefetch=1,
            grid=(NA_HEADS // nh, batch, nj),
            in_specs=[pl.BlockSpec((tq, wide), lambda h, b, j, g: (b * nj + j, q0 // nh + h)),
                      pl.BlockSpec((t, wide), lambda h, b, j, g: (b, k0 // nh + h)),
                      pl.BlockSpec((t, wide), lambda h, b, j, g: (b, v0 // nh + h)),
                      pl.BlockSpec((l, wide), lambda h, b, j, g: (b, k0 // nh + h)),
                      pl.BlockSpec((l, wide), lambda h, b, j, g: (b, v0 // nh + h)),
                      pl.BlockSpec((nh,) + tiles.shape[1:], lambda h, b, j, g: (h, 0, 0, 0))],
            out_specs=pl.BlockSpec((tq, wide), lambda h, b, j, g: (b * nj + j, h)),
            scratch_shapes=[pltpu.VMEM((nh, tq, NA_KROWS * GRID_W), F32)]),
        compiler_params=_cparams(3),
        name="na_attention",
    )(ids, z, z, z, z_c, z_c, tiles)


def _dense_attn_kernel(q_ref, k_ref, v_ref, o_ref, *, scale):
    for hd in range(NA_STEP_HEADS):
        sl = slice(hd * LANES, (hd + 1) * LANES)
        o_ref[:, sl] = _softmax_pv(q_ref[:, sl], [(k_ref[:, sl], None, v_ref[:, sl])], scale)


def ctx_attention(z_c, batch, l):
    nh = NA_STEP_HEADS
    wide = nh * LANES
    q0, k0, v0 = (EV_Q_BLK + i * NA_HEADS for i in range(3))
    assert q0 % nh == 0 and NA_HEADS % nh == 0, "head groups must start on a whole column block"
    return pl.pallas_call(
        functools.partial(_dense_attn_kernel, scale=NA_HEAD_DIM ** -0.5),
        out_shape=jax.ShapeDtypeStruct((batch * l, NA_WIDTH), F32),
        grid=(batch, NA_HEADS // nh),
        in_specs=[pl.BlockSpec((l, wide), lambda b, h: (b, q0 // nh + h)),
                  pl.BlockSpec((l, wide), lambda b, h: (b, k0 // nh + h)),
                  pl.BlockSpec((l, wide), lambda b, h: (b, v0 // nh + h))],
        out_specs=pl.BlockSpec((l, wide), lambda b, h: (b, h)),
        compiler_params=_cparams(2),
        name="ctx_attention",
    )(z_c, z_c, z_c)


def _head_sum(x):
    first = lax.broadcasted_iota(jnp.int32, x.shape, 1) < RW_HEAD_DIM
    s0 = jnp.sum(jnp.where(first, x, 0.0), axis=-1, keepdims=True)
    s1 = jnp.sum(jnp.where(first, 0.0, x), axis=-1, keepdims=True)
    return jnp.where(first, s0, s1)


def _rw_terms_kernel(ur_ref, uk_ref, uv_ref, uwa_ref, pr_ref, pk_ref, pv_ref, pwa_ref, nr_ref, nk_ref, nv_ref,
                     nwa_ref, mu_ref, w0_ref, w2_ref, a0_ref, a2_ref, kk_ref, ka_ref, rk_ref,
                     rv_o, wk0_o, wk1_o, bk0_o, bk1_o, bonus_o, *, nt):
    i = pl.program_id(1)
    tm = ur_ref.shape[0]

    def shifted(u_ref, p_ref, n_ref, col0):
        u = u_ref[...]
        mu = mu_ref[:, col0:col0 + u.shape[1]]
        row = lax.broadcasted_iota(jnp.int32, u.shape, 0)
        before = jnp.where(i == 0, 0.0, p_ref[7:8, :])
        after = jnp.where(i == nt - 1, 0.0, n_ref[0:1, :])
        prev = jnp.where(row == 0, before, pltpu.roll(u, 1, axis=0))
        nxt = jnp.where(row == tm - 1, after, pltpu.roll(u, tm - 1, axis=0))
        return u + mu[0:1] * (prev - u) + mu[1:2] * (nxt - u)

    r_all = shifted(ur_ref, pr_ref, nr_ref, 0)
    k_all = shifted(uk_ref, pk_ref, nk_ref, RW_WIDTH)
    v_all = shifted(uv_ref, pv_ref, nv_ref, 2 * RW_WIDTH)
    wa = shifted(uwa_ref, pwa_ref, nwa_ref, 3 * RW_WIDTH)
    wd = jnp.tanh(wa[:, :LANES]).astype(BF16)
    ad = wa[:, LANES:].astype(BF16)
    first = lax.broadcasted_iota(jnp.int32, (tm, LANES), 1) < RW_HEAD_DIM

    def head_rows(x, y):
        return (jnp.where(first, x, pltpu.roll(y, RW_HEAD_DIM, axis=1)),
                jnp.where(first, pltpu.roll(x, RW_HEAD_DIM, axis=1), y))

    outs = (rv_o, wk0_o, wk1_o, bk0_o, bk1_o)
    even = [[] for _ in outs]
    odd = [[] for _ in outs]
    bonus = []
    for p in range(RW_WIDTH // LANES):
        sl = slice(p * LANES, (p + 1) * LANES)
        r, k, v = r_all[:, sl], k_all[:, sl], v_all[:, sl]
        kk = k * kk_ref[:, sl]
        kk = kk / jnp.maximum(jnp.sqrt(_head_sum(kk * kk)), 1e-12)
        kd_sum = jnp.zeros_like(k)
        per_dir = []
        for d in range(2):
            w = w0_ref[d:d + 1, sl] + jnp.dot(wd, w2_ref[d, :, sl].astype(BF16), preferred_element_type=F32)
            a = _sigmoid(a0_ref[d:d + 1, sl] + jnp.dot(ad, a2_ref[d, :, sl].astype(BF16), preferred_element_type=F32))
            kd = k * (1.0 + (a - 1.0) * ka_ref[:, sl])
            dec = jnp.exp(-RW_DECAY_SCALE * _sigmoid(w))
            per_dir.append(((dec, kd), (kk * a, kk)))
            kd_sum = kd_sum + kd
        pairs = [(r, v), per_dir[0][0], per_dir[1][0], per_dir[0][1], per_dir[1][1]]
        bonus.append(_head_sum(r * kd_sum * rk_ref[:, sl]) * v)
        for ev, od, (x, y) in zip(even, odd, pairs):
            e, o = head_rows(x, y)
            ev.append(e)
            od.append(o)
    for o_ref, ev, od in zip(outs, even, odd):
        o_ref[...] = jnp.swapaxes(jnp.stack(ev + od, axis=0), 0, 1)
    bonus_o[...] = jnp.swapaxes(jnp.stack(bonus, axis=0), 0, 1)


def rwkv_terms(z, n_seq, t, mu, w0, w2p, a0, a2p, k_k, k_a, r_k, tm=256):
    tm = min(tm, t)
    nt = t // tm
    npair = RW_WIDTH // LANES
    wide0 = EV_RW_BLK * LANES // RW_WIDTH
    wa_blk = (EV_RW_BLK + 3 * npair) // 2
    n8 = tm // 8

    def cur(width, cb):
        return pl.BlockSpec((tm, width), lambda b, i: (b * nt + i, cb))

    def before(width, cb):
        return pl.BlockSpec((8, width), lambda b, i: (jnp.maximum((b * nt + i) * n8 - 1, 0), cb))

    def after(width, cb):
        return pl.BlockSpec((8, width), lambda b, i: (jnp.minimum((b * nt + i + 1) * n8, n_seq * nt * n8 - 1), cb))

    cols = [(RW_WIDTH, wide0), (RW_WIDTH, wide0 + 1), (RW_WIDTH, wide0 + 2), (2 * LANES, wa_blk)]
    whole = lambda a: pl.BlockSpec(a.shape, lambda b, i: (0,) * a.ndim)
    params = (mu, w0, w2p, a0, a2p, k_k, k_a, r_k)
    out = pl.BlockSpec((tm, npair, LANES), lambda b, i: (i, b, 0))
    out2 = pl.BlockSpec((tm, 2 * npair, LANES), lambda b, i: (i, b, 0))
    return pl.pallas_call(
        functools.partial(_rw_terms_kernel, nt=nt),
        out_shape=[jax.ShapeDtypeStruct((t, n_seq * 2 * npair, LANES), F32)] * 5
        + [jax.ShapeDtypeStruct((t, n_seq * npair, LANES), F32)],
        grid=(n_seq, nt),
        in_specs=([cur(*c) for c in cols] + [before(*c) for c in cols] + [after(*c) for c in cols]
                  + [whole(a) for a in params]),
        out_specs=[out2] * 5 + [out],
        compiler_params=_cparams(2),
        name="rwkv_terms",
    )(*([z] * 12), *params)


def _to_lanes(n):
    q = jnp.concatenate([n, pltpu.roll(n, RW_HEAD_DIM, axis=1)], axis=0)
    return q.T[:RW_HEAD_DIM]


def _rw_scan_kernel(rvf, rvb, wk0, wk1, bk0, bk1, s0_ref, yf_ref, yb_ref, sfin_ref,
                    s_ref, z_ref, znext_ref, g_ref, ybuf_ref, *, chunk):
    n = RW_HEAD_DIM
    nb = rvf.shape[1] // (2 * RW_WIDTH // LANES)
    R, K, V, A, B = range(5)

    @pl.when(pl.program_id(0) == 0)
    def _():
        s_ref[...] = s0_ref[...]

    def convert(t_src):
        def pair_tiles(f, bw):
            both = (f[t_src], bw[chunk - 1 - t_src])
            x = jnp.concatenate([src[(b * 2 + par) * 8:(b * 2 + par) * 8 + 8]
                                 for par in range(2) for src in both for b in range(nb)], axis=0)
            xt = x.T
            return xt[:n], xt[n:]

        r, v = pair_tiles(rvf, rvb)
        w, k = pair_tiles(wk0, wk1)
        b, kk = pair_tiles(bk0, bk1)
        g_prev = g_ref[0]
        g = g_prev * w
        g_ref[0] = g
        inv = 1.0 / g
        znext_ref[R] = r * g
        znext_ref[K] = k * inv
        znext_ref[V] = v
        znext_ref[A] = -(kk * g_prev)
        znext_ref[B] = b * inv

    def emit_y(t):
        rows = _to_lanes(ybuf_ref[...])
        yf_ref[t] = rows[:n // 2]
        yb_ref[chunk - 1 - t] = rows[n // 2:]

    g_ref[0] = jnp.ones((n, LANES), F32)
    ybuf_ref[...] = jnp.zeros((n, LANES), F32)
    convert(0)

    def step(t, carry):
        emit_y(jnp.maximum(t - 1, 0))
        z_ref[...] = znext_ref[...]
        g_ref[1] = g_ref[0]
        convert(jnp.minimum(t + 1, chunk - 1))
        sa = [jnp.zeros((n, LANES), F32), jnp.zeros((n, LANES), F32)]
        for k in range(n):
            sa[k % 2] = sa[k % 2] + s_ref[k] * z_ref[A, k:k + 1, :]
        sa = sa[0] + sa[1]
        vt = z_ref[V]
        y = jnp.zeros((n, LANES), F32)
        for k in range(n):
            sk = s_ref[k] + sa * z_ref[B, k:k + 1, :] + vt * z_ref[K, k:k + 1, :]
            s_ref[k] = sk
            y = y + sk * z_ref[R, k:k + 1, :]
        ybuf_ref[...] = y
        return carry

    lax.fori_loop(0, chunk, step, 0)
    emit_y(chunk - 1)
    for k in range(n):
        s_ref[k] = s_ref[k] * g_ref[1, k:k + 1, :]
    sfin_ref[...] = s_ref[...]


def rwkv_scan(rv, wk0, wk1, bk0, bk1, s0):
    t, nrow, _ = rv.shape
    n = RW_HEAD_DIM
    chunk = RW_SCAN_CHUNK
    nc = t // chunk
    assert nrow == LANES // 2, "the states of both directions fill the 128 lanes"
    fwd = pl.BlockSpec((chunk, nrow, LANES), lambda i: (i, 0, 0))
    bwd = pl.BlockSpec((chunk, nrow, LANES), lambda i: (nc - 1 - i, 0, 0))
    yfwd = pl.BlockSpec((chunk, nrow // 2, LANES), lambda i: (i, 0, 0))
    ybwd = pl.BlockSpec((chunk, nrow // 2, LANES), lambda i: (nc - 1 - i, 0, 0))
    state = pl.BlockSpec((n, n, LANES), lambda i: (0, 0, 0))
    slab = jax.ShapeDtypeStruct((t, nrow // 2, LANES), F32)
    return pl.pallas_call(
        functools.partial(_rw_scan_kernel, chunk=chunk),
        out_shape=[slab, slab, jax.ShapeDtypeStruct((n, n, LANES), F32)],
        grid=(nc,),
        in_specs=[fwd, bwd, fwd, bwd, fwd, bwd, state],
        out_specs=[yfwd, ybwd, state],
        scratch_shapes=[pltpu.VMEM((n, n, LANES), F32), pltpu.VMEM((5, n, LANES), F32),
                        pltpu.VMEM((5, n, LANES), F32), pltpu.VMEM((2, n, LANES), F32),
                        pltpu.VMEM((n, LANES), F32)],
        compiler_params=_cparams(1),
        name="rwkv_scan",
    )(rv, rv, wk0, wk1, bk0, bk1, s0)


def _rw_readout_kernel(yf_ref, yb_ref, bonus_ref, lw_ref, lb_ref, o_ref):
    npair = RW_WIDTH // LANES
    y_all = jnp.swapaxes(yf_ref[...] + yb_ref[...], 0, 1)
    bonus_all = jnp.swapaxes(bonus_ref[...], 0, 1)
    for q in range(yf_ref.shape[1]):
        b, p = divmod(q, npair)
        sl = slice(p * LANES, (p + 1) * LANES)
        y = y_all[q]
        mu = _head_sum(y) * (1.0 / RW_HEAD_DIM)
        d = y - mu
        var = _head_sum(d * d) * (1.0 / RW_HEAD_DIM)
        o_ref[b, :, sl] = d * lax.rsqrt(var + RW_GN_EPS) * lw_ref[:, sl] + lb_ref[:, sl] + bonus_all[q]


def rwkv_readout(y_f, y_b, bonus, ln_w, ln_b, tm=256):
    t, nq, _ = y_f.shape
    tm = min(tm, t)
    batch = nq // (RW_WIDTH // LANES)
    slab = pl.BlockSpec((tm, nq, LANES), lambda i: (i, 0, 0))
    vec = pl.BlockSpec((1, RW_WIDTH), lambda i: (0, 0))
    return pl.pallas_call(
        _rw_readout_kernel,
        out_shape=jax.ShapeDtypeStruct((batch, t, RW_WIDTH), F32),
        grid=(t // tm,),
        in_specs=[slab, slab, slab, vec, vec],
        out_specs=pl.BlockSpec((batch, tm, RW_WIDTH), lambda i: (0, i, 0)),
        compiler_params=_cparams(1),
        name="rwkv_readout",
    )(y_f, y_b, bonus, ln_w.reshape(1, -1), ln_b.reshape(1, -1))


def _rms(x, w):
    return x * lax.rsqrt(jnp.mean(x * x, axis=-1, keepdims=True) + NORM_EPS) * w


def _rope(x, cos, sin):
    odd = (lax.broadcasted_iota(jnp.int32, x.shape, 1) & 1) == 1
    swapped = jnp.where(odd, pltpu.roll(x, 1, axis=1), pltpu.roll(x, LANES - 1, axis=1))
    return x * cos + swapped * sin


def _mla_q_kernel(c_ref, nw_ref, w_ref, cos_ref, sin_ref, q_ref):
    h = _rms(c_ref[...], nw_ref[...]).astype(BF16)
    acc = jnp.dot(h, w_ref[...], preferred_element_type=F32)
    cos, sin = cos_ref[...], sin_ref[...]
    for hd in range(MLA_HEADS):
        lo = hd * MLA_QK_PAD
        q_ref[:, lo:lo + MLA_NOPE] = acc[:, lo:lo + MLA_NOPE].astype(BF16)
        q_ref[:, lo + MLA_NOPE:lo + MLA_QK_PAD] = _rope(acc[:, lo + MLA_NOPE:lo + MLA_QK_PAD], cos, sin).astype(BF16)


def mla_queries(z, q_norm, w_uq_pad, cos, sin, t, tm=512):
    m = z.shape[0]
    tm = min(tm, t)
    nt = t // tm
    width = MLA_HEADS * MLA_QK_PAD
    return pl.pallas_call(
        _mla_q_kernel,
        out_shape=jax.ShapeDtypeStruct((m, width), BF16),
        grid=(m // tm,),
        in_specs=[pl.BlockSpec((tm, MLA_Q_RANK), lambda i: (i, GATE_COLS // MLA_Q_RANK)),
                  pl.BlockSpec((1, MLA_Q_RANK), lambda i: (0, 0)),
                  pl.BlockSpec((MLA_Q_RANK, width), lambda i: (0, 0)),
                  pl.BlockSpec((tm, LANES), lambda i: (i % nt, 0)),
                  pl.BlockSpec((tm, LANES), lambda i: (i % nt, 0))],
        out_specs=pl.BlockSpec((tm, width), lambda i: (i, 0)),
        compiler_params=_cparams(1),
        name="mla_queries",
    )(z, q_norm.reshape(1, -1), w_uq_pad, cos, sin)


def _mla_kv_kernel(c_ref, pe_ref, nw_ref, w_ref, cos_ref, sin_ref, k_ref, v_ref):
    h = _rms(c_ref[...], nw_ref[...]).astype(BF16)
    acc = jnp.dot(h, w_ref[...], preferred_element_type=F32)
    pe = _rope(pe_ref[...], cos_ref[...], sin_ref[...]).astype(BF16)
    for hd in range(MLA_HEADS):
        src = hd * (MLA_NOPE + MLA_V)
        dst = hd * MLA_QK_PAD
        k_ref[:, dst:dst + MLA_NOPE] = acc[:, src:src + MLA_NOPE].astype(BF16)
        k_ref[:, dst + MLA_NOPE:dst + MLA_QK_PAD] = pe
        v_ref[:, hd * MLA_V:(hd + 1) * MLA_V] = acc[:, src + MLA_NOPE:src + MLA_NOPE + MLA_V].astype(BF16)


def mla_keys_values(z, kv_norm, w_ukv, cos, sin, t, tm=512):
    m = z.shape[0]
    tm = min(tm, t)
    nt = t // tm
    return pl.pallas_call(
        _mla_kv_kernel,
        out_shape=[jax.ShapeDtypeStruct((m, MLA_HEADS * MLA_QK_PAD), BF16),
                   jax.ShapeDtypeStruct((m, MLA_WIDTH), BF16)],
        grid=(m // tm,),
        in_specs=[pl.BlockSpec((tm, MLA_KV_RANK), lambda i: (i, (GATE_COLS + MLA_Q_RANK) // MLA_KV_RANK)),
                  pl.BlockSpec((tm, LANES), lambda i: (i, OD_A_BLK + (MLA_Q_RANK + MLA_KV_RANK) // LANES)),
                  pl.BlockSpec((1, MLA_KV_RANK), lambda i: (0, 0)),
                  pl.BlockSpec(w_ukv.shape, lambda i: (0, 0)),
                  pl.BlockSpec((tm, LANES), lambda i: (i % nt, 0)),
                  pl.BlockSpec((tm, LANES), lambda i: (i % nt, 0))],
        out_specs=[pl.BlockSpec((tm, MLA_HEADS * MLA_QK_PAD), lambda i: (i, 0)),
                   pl.BlockSpec((tm, MLA_WIDTH), lambda i: (i, 0))],
        compiler_params=_cparams(1),
        name="mla_keys_values",
    )(z, z, kv_norm.reshape(1, -1), w_ukv, cos, sin)


def _mla_attn_kernel(q_ref, k_ref, v_ref, kc_ref, vc_ref, o_ref, *, scale):
    parts = [(kc_ref[...], None, vc_ref[...])]
    nkeys = k_ref.shape[0]
    step = min(MLA_KEY_CHUNK, nkeys)
    for lo in range(0, nkeys, step):
        parts.append((k_ref[lo:lo + step, :], None, v_ref[lo:lo + step, :]))
    o_ref[...] = _softmax_pv(q_ref[...], parts, scale)


def mla_attention(q, k, v, kc, vc, batch, t, l, tq=2048):
    tq = min(tq, t)
    nq = t // tq
    return pl.pallas_call(
        functools.partial(_mla_attn_kernel, scale=(MLA_NOPE + MLA_ROPE) ** -0.5),
        out_shape=jax.ShapeDtypeStruct((batch * t, MLA_WIDTH), F32),
        grid=(batch, MLA_HEADS, nq),
        in_specs=[pl.BlockSpec((tq, MLA_QK_PAD), lambda b, h, i: (b * nq + i, h)),
                  pl.BlockSpec((t, MLA_QK_PAD), lambda b, h, i: (b, h)),
                  pl.BlockSpec((t, MLA_V), lambda b, h, i: (b, h)),
                  pl.BlockSpec((l, MLA_QK_PAD), lambda b, h, i: (b, h)),
                  pl.BlockSpec((l, MLA_V), lambda b, h, i: (b, h))],
        out_specs=pl.BlockSpec((tq, MLA_V), lambda b, h, i: (b * nq + i, h)),
        compiler_params=_cparams(3),
        name="mla_attention",
    )(q, k, v, kc, vc)


def _rope_tables(t):
    tok = np.arange(t)
    pos = np.stack([tok // GRID_W, tok % GRID_W], axis=-1).astype(np.float32)
    n_freq = MLA_ROPE // 4
    inv = (ROPE_BASE ** (-jnp.arange(n_freq, dtype=F32) / n_freq))
    ang = (jnp.asarray(pos)[:, :, None] * inv).reshape(t, MLA_ROPE // 2)
    cos = jnp.repeat(jnp.cos(ang), 2, axis=-1)
    sin = jnp.repeat(jnp.sin(ang), 2, axis=-1) * jnp.tile(jnp.array([-1.0, 1.0], F32), MLA_ROPE // 2)
    pad = ((0, 0), (0, LANES - MLA_ROPE))
    return jnp.pad(cos, pad), jnp.pad(sin, pad)


def _split3(x):
    hi = x.astype(BF16)
    r1 = x - hi.astype(F32)
    mid = r1.astype(BF16)
    lo = (r1 - mid.astype(F32)).astype(BF16)
    return hi, mid, lo


def _hg_kernel(q_ref, f_ref, v_ref, lb_ref, s0_ref, *rest, reverse, finalize, emit_state):
    st_ref = rest[-1]
    rest = rest[:-1]
    if finalize:
        prev_ref, nw_ref = rest[0], rest[1]
        rest = rest[2:]
    o_ref = rest[0]

    @pl.when(pl.program_id(2) == 0)
    def _():
        st_ref[...] = s0_ref[...]

    bt = q_ref.shape[0]
    nchunk = bt // HG_CHUNK
    order = range(nchunk - 1, -1, -1) if reverse else range(nchunk)
    ri = lax.broadcasted_iota(jnp.int32, (bt, bt), 0)
    ci = lax.broadcasted_iota(jnp.int32, (bt, bt), 1)
    same = (ri // HG_CHUNK) == (ci // HG_CHUNK)
    tri = same & ((ci >= ri) if reverse else (ci <= ri))
    tri_f = tri.astype(F32)
    tri_b = tri.astype(BF16)
    row_chunk = lax.broadcasted_iota(jnp.int32, (bt, HG_DK), 0) // HG_CHUNK
    in_chunk = [row_chunk == c for c in range(nchunk)]

    lb = lb_ref[...]
    fz = f_ref[...]
    log_sig = jnp.minimum(fz, 0.0) - jnp.log1p(jnp.exp(-jnp.abs(fz)))
    la = jnp.log(lb)
    lbb = jnp.log1p(-lb) + log_sig
    log_f = jnp.maximum(la, lbb) + jnp.log1p(jnp.exp(-jnp.abs(la - lbb)))
    kf_all = (1.0 - lb) * _sigmoid(-fz)
    pieces = jnp.dot(tri_b, jnp.concatenate(_split3(log_f), axis=1), preferred_element_type=F32)
    width = log_f.shape[1]
    cum_all = pieces[:, :width] + pieces[:, width:2 * width] + pieces[:, 2 * width:]

    for hh in range(HG_STEP_HEADS):
        sl = slice(hh * HG_DK, (hh + 1) * HG_DK)
        cum, kf, v = cum_all[:, sl], kf_all[:, sl], v_ref[:, sl]
        last = [cum[c * HG_CHUNK:c * HG_CHUNK + 1] if reverse else cum[(c + 1) * HG_CHUNK - 1:(c + 1) * HG_CHUNK]
                for c in range(nchunk)]
        tot = jnp.concatenate([jnp.broadcast_to(x, (HG_CHUNK, HG_DK)) for x in last], axis=0)
        q_in = q_ref[:, sl] * jnp.exp(cum)
        k_in = kf * jnp.exp(-cum)
        k_end = kf * jnp.exp(tot - cum)
        o_intra = _dot(_dot_nt(q_in, k_in) * tri_f, v)
        v_by_chunk = jnp.concatenate([jnp.where(in_chunk[c], v, 0.0) for c in range(nchunk)], axis=1)
        d_st = _dot_tn(v_by_chunk, k_end)
        st = st_ref[hh]
        before = [None] * nchunk
        for c in order:
            before[c] = st
            st = st * jnp.exp(last[c]) + d_st[c * HG_DV:(c + 1) * HG_DV]
        st_ref[hh] = st
        q_by_chunk = jnp.concatenate([jnp.where(in_chunk[c], q_in, 0.0) for c in range(nchunk)], axis=1)
        o = o_intra + _dot_nt(q_by_chunk, jnp.concatenate(before, axis=1))
        if finalize:
            o = o + prev_ref[:, sl]
            o = o * lax.rsqrt(jnp.mean(o * o, axis=-1, keepdims=True) + NORM_EPS) * nw_ref[...]
        o_ref[:, sl] = o
        if emit_state:
            rest[1][hh] = st


def hgrn2_pass(z, lb, s0, batch, t, f_off, reverse, prev=None, norm_w=None, emit_state=False):
    bt = min(HG_BLOCK, t)
    nblk = t // bt
    nh = HG_STEP_HEADS
    wide = nh * LANES
    finalize = prev is not None
    assert OD_HG_BLK % nh == 0 and HG_HEADS % nh == 0, "head groups must start on a whole column block"
    pos = (lambda i: nblk - 1 - i) if reverse else (lambda i: i)
    blk = lambda off: pl.BlockSpec((bt, wide), lambda b, h, i, off=off: (b * nblk + pos(i), off // nh + h))
    state = pl.BlockSpec((None, nh, HG_DV, HG_DK), lambda b, h, i: (b, h, 0, 0))
    ins = [z, z, z, lb.reshape(1, -1), s0]
    specs = [blk(OD_HG_BLK), blk(OD_HG_BLK + f_off), blk(OD_HG_BLK + 3 * HG_HEADS),
             pl.BlockSpec((1, wide), lambda b, h, i: (0, h)), state]
    if finalize:
        ins += [prev, norm_w.reshape(1, -1)]
        specs += [blk(0), pl.BlockSpec((1, HG_DV), lambda b, h, i: (0, 0))]
    out_shape = [jax.ShapeDtypeStruct((batch * t, HG_WIDTH), F32)]
    out_specs = [blk(0)]
    if emit_state:
        out_shape.append(jax.ShapeDtypeStruct((batch, HG_HEADS, HG_DV, HG_DK), F32))
        out_specs.append(state)
    return pl.pallas_call(
        functools.partial(_hg_kernel, reverse=reverse, finalize=finalize, emit_state=emit_state),
        out_shape=out_shape,
        grid=(batch, HG_HEADS // nh, nblk),
        in_specs=specs,
        out_specs=out_specs,
        scratch_shapes=[pltpu.VMEM((nh, HG_DV, HG_DK), F32)],
        compiler_params=_cparams(3),
        name="hgrn2_" + ("bwd" if reverse else "fwd"),
    )(*ins)


def _even_layer(x, xc, mod, norm_w, w_in, w_out, rpb, rw_params, batch, t, l):
    s2 = 3 * NA_WIDTH + RW_SHIFT_COLS
    width = RW_PAD_COLS - RW_SHIFT_COLS
    n_gate, n_rest = GATE_COLS // width, s2 // width
    plan = [n_rest + i for i in range(n_gate)] + list(range(n_rest)) + [None]
    w_ext = cast_reorder(w_in, plan, width, "even_w_in")
    tiles_per_seq = max(t // 1024, 1)
    z = in_projection(x, norm_w, mod, lambda i: i // tiles_per_seq, w_ext, "even_in_proj")
    z_c = in_projection(xc, norm_w, mod, lambda i: batch, w_ext, "even_in_proj_ctx")

    y_na = na_attention(z, z_c, rpb, batch, t, l)
    yc_na = ctx_attention(z_c, batch, l)

    mu, w0, w2, a0, a2, k_k, k_a, r_k, ln_w, ln_b = rw_params
    mu = jnp.pad(mu, ((0, 0), (0, RW_PAD_COLS - RW_SHIFT_COLS)))
    zpad = jnp.zeros((RW_RANK, RW_WIDTH), F32)
    w2p = jnp.stack([jnp.concatenate([w2[0], zpad]), jnp.concatenate([zpad, w2[1]])])
    a2p = jnp.stack([jnp.concatenate([a2[0], zpad]), jnp.concatenate([zpad, a2[1]])])
    vecs = (mu, w0, w2p, a0, a2p, k_k.reshape(1, -1), k_a.reshape(1, -1), r_k.reshape(1, -1))
    terms = rwkv_terms(z, batch, t, *vecs)
    terms_c = rwkv_terms(z_c, batch, l, *vecs)
    s_zero = jnp.zeros((RW_HEAD_DIM, RW_HEAD_DIM, LANES), F32)
    yc_f, yc_b, s_ctx = rwkv_scan(*terms_c[:5], s_zero)
    y_f, y_b, _ = rwkv_scan(*terms[:5], s_ctx)
    y_rw = rwkv_readout(y_f, y_b, terms[5], ln_w, ln_b).reshape(batch * t, RW_WIDTH)
    yc_rw = rwkv_readout(yc_f, yc_b, terms_c[5], ln_w, ln_b).reshape(batch * l, RW_WIDTH)

    w_out = w_out.astype(BF16)
    rows_per_seq = max(t // 256, 1)
    x = out_projection(y_na, y_rw, z, w_out, x, mod, lambda i: i // rows_per_seq, "even_out_proj")
    xc = out_projection(yc_na, yc_rw, z_c, w_out, xc, mod, lambda i: batch, "even_out_proj_ctx")
    return x, xc


def _odd_layer(x, xc, mod, norm_w, w_in, w_out, q_norm, w_uq, kv_norm, w_ukv, lb, hg_norm_w, final_w,
               batch, t, l):
    o3 = MLA_Q_RANK + MLA_KV_RANK + MLA_ROPE
    o4 = o3 + 4 * HG_FDIM
    zero = jnp.zeros((x.shape[1], MLA_A_COLS - o3), BF16)
    w_in = w_in.astype(BF16)
    w_ext = jnp.concatenate([w_in[:, o4:], w_in[:, :o3], zero, w_in[:, o3:o4]], axis=1)
    tiles_per_seq = max(t // 1024, 1)
    z = in_projection(x, norm_w, mod, lambda i: i // tiles_per_seq, w_ext, "odd_in_proj")
    z_c = in_projection(xc, norm_w, mod, lambda i: batch, w_ext, "odd_in_proj_ctx")

    cos, sin = _rope_tables(t)
    qk = MLA_NOPE + MLA_ROPE
    w_uq_pad = jnp.pad(w_uq.reshape(MLA_Q_RANK, MLA_HEADS, qk), ((0, 0), (0, 0), (0, MLA_QK_PAD - qk)))
    w_uq_pad = w_uq_pad.reshape(MLA_Q_RANK, MLA_HEADS * MLA_QK_PAD).astype(BF16)
    w_ukv = w_ukv.astype(BF16)
    q = mla_queries(z, q_norm, w_uq_pad, cos, sin, t)
    k, v = mla_keys_values(z, kv_norm, w_ukv, cos, sin, t)
    ones = jnp.ones((l, LANES), F32)
    kc, vc = mla_keys_values(z_c, kv_norm, w_ukv, ones, jnp.zeros_like(ones), l)
    y_mla = mla_attention(q, k, v, kc, vc, batch, t, l)

    s_zero = jnp.zeros((batch, HG_HEADS, HG_DV, HG_DK), F32)
    _, s_f = hgrn2_pass(z_c, lb, s_zero, batch, l, HG_HEADS, False, emit_state=True)
    _, s_b = hgrn2_pass(z_c, lb, s_zero, batch, l, 2 * HG_HEADS, True, emit_state=True)
    o_f, = hgrn2_pass(z, lb, s_f, batch, t, HG_HEADS, False)
    y_hg, = hgrn2_pass(z, lb, s_b, batch, t, 2 * HG_HEADS, True, prev=o_f, norm_w=hg_norm_w)

    rows_per_seq = max(t // 256, 1)
    return out_projection(y_mla, y_hg, z, w_out.astype(BF16), x, mod, lambda i: i // rows_per_seq,
                          "odd_out_proj", final_w=final_w)


def kernel(x, c, ctx, c_ctx, ada_w, ada_b, norm_w, e_w_in, e_w_out, na_rpb, rw_mu, rw_w0, rw_w2, rw_a0, rw_a2, rw_k_k, rw_k_a, rw_r_k, rw_ln_w, rw_ln_b, o_w_in, o_w_out, mla_q_norm, mla_w_uq, mla_kv_norm, mla_w_ukv, hg_lower_bounds, hg_norm_w, final_norm_w):
    batch, t, d = x.shape
    l = ctx.shape[1]
    assert ada_w.shape[0] == 2, "one even and one odd layer"
    cond = jnp.concatenate([c, c_ctx[None, :], jnp.zeros((8 - batch - 1, d), F32)], axis=0)
    mod = modulation(cond, ada_w, ada_b)
    s = jax.nn.softmax(hg_lower_bounds.astype(F32), axis=0)
    lower = jnp.cumsum(s, axis=0) - s[0]

    xf, xcf = x.reshape(batch * t, d), ctx.reshape(batch * l, d)
    rw_params = (rw_mu[0], rw_w0[0], rw_w2[0], rw_a0[0], rw_a2[0], rw_k_k[0], rw_k_a[0], rw_r_k[0],
                 rw_ln_w[0], rw_ln_b[0])
    xf, xcf = _even_layer(xf, xcf, mod[0].reshape(8, 1, 3 * d), norm_w[0], e_w_in[0], e_w_out[0], na_rpb[0],
                          rw_params, batch, t, l)
    out = _odd_layer(xf, xcf, mod[1].reshape(8, 1, 3 * d), norm_w[1], o_w_in[0], o_w_out[0], mla_q_norm[0],
                     mla_w_uq[0], mla_kv_norm[0], mla_w_ukv[0], lower[1], hg_norm_w[0], final_norm_w,
                     batch, t, l)
    return out.reshape(batch, t, d)
```

```python
import functools
import math

import numpy as np
import jax
import jax.numpy as jnp
from jax import lax
from jax.experimental import pallas as pl
from jax.experimental.pallas import tpu as pltpu

F32 = jnp.float32
BF16 = jnp.bfloat16

GRID_W = 64
NORM_EPS = 1e-6
ROPE_BASE = 10000.0

NA_HEADS = 8
NA_HEAD_DIM = 128
NA_WIDTH = NA_HEADS * NA_HEAD_DIM
NA_WIN_ROWS = 8
NA_WIN_COLS = 16
NA_QROWS = 8
NA_KROWS = 16
NA_STEP_HEADS = 4

RW_HEAD_DIM = 64
RW_HEADS = 16
RW_WIDTH = RW_HEADS * RW_HEAD_DIM
RW_RANK = 64
RW_SHIFT_COLS = 3 * RW_WIDTH + 4 * RW_RANK
RW_PAD_COLS = 3584
RW_GN_EPS = 64e-5
RW_SCAN_CHUNK = 64
RW_DECAY_SCALE = math.exp(-0.5)

MLA_HEADS = 8
MLA_Q_RANK = 512
MLA_KV_RANK = 512
MLA_NOPE = 128
MLA_ROPE = 64
MLA_V = 128
MLA_QK_PAD = 256
MLA_WIDTH = MLA_HEADS * MLA_V
MLA_A_COLS = 1536
MLA_KEY_CHUNK = 1024

HG_HEADS = 8
HG_DK = 128
HG_DV = 128
HG_FDIM = HG_HEADS * HG_DK
HG_WIDTH = HG_HEADS * HG_DV
HG_CHUNK = 32
HG_BLOCK = 256
HG_STEP_HEADS = 4

LANES = 128

GATE_COLS = 2048
EV_Q_BLK = GATE_COLS // LANES
EV_RW_BLK = EV_Q_BLK + 3 * NA_WIDTH // LANES
OD_A_BLK = GATE_COLS // LANES
OD_HG_BLK = OD_A_BLK + MLA_A_COLS // LANES
VMEM_LIMIT = 48 * 1024 * 1024
NEG_INF = -1e30
LOG2E = math.log2(math.e)


def _cparams(n_axes):
    return pltpu.CompilerParams(dimension_semantics=("arbitrary",) * n_axes, vmem_limit_bytes=VMEM_LIMIT)


def _sigmoid(x):
    return 1.0 / (1.0 + jnp.exp(-x))


def _silu(x):
    return x * _sigmoid(x)


def _dot(a, b):
    return jnp.dot(a.astype(BF16), b.astype(BF16), preferred_element_type=F32)


def _dot_nt(a, b):
    return lax.dot_general(a.astype(BF16), b.astype(BF16), (((1,), (1,)), ((), ())), preferred_element_type=F32)


def _dot_tn(a, b):
    return lax.dot_general(a.astype(BF16), b.astype(BF16), (((0,), (0,)), ((), ())), preferred_element_type=F32)


def _mod_kernel(c_ref, w_ref, b_ref, o_ref):
    o_ref[...] = _dot(_silu(c_ref[...]), w_ref[...]) + b_ref[...]


def modulation(cond, ada_w, ada_b, tn=512):
    depth, d, n = ada_w.shape
    return pl.pallas_call(
        _mod_kernel,
        out_shape=jax.ShapeDtypeStruct((depth, 8, n), F32),
        grid=(depth, n // tn),
        in_specs=[pl.BlockSpec((8, d), lambda l, j: (0, 0)),
                  pl.BlockSpec((None, d, tn), lambda l, j: (l, 0, j)),
                  pl.BlockSpec((None, 1, tn), lambda l, j: (l, 0, j))],
        out_specs=pl.BlockSpec((None, 8, tn), lambda l, j: (l, 0, j)),
        compiler_params=_cparams(2),
        name="adaln_modulation",
    )(cond, ada_w, ada_b.reshape(depth, 1, n))


def _cast_reorder_kernel(src_ref, zero_ref, x_ref, o_ref):
    del src_ref
    o_ref[...] = jnp.where(zero_ref[pl.program_id(0)] != 0, 0.0, x_ref[...]).astype(BF16)


def cast_reorder(w, plan, width, name):
    d = w.shape[0]
    src = np.array([0 if a is None else a for a in plan], np.int32)
    zero = np.array([a is None for a in plan], np.int32)
    return pl.pallas_call(
        _cast_reorder_kernel,
        out_shape=jax.ShapeDtypeStruct((d, width * len(plan)), BF16),
        grid_spec=pltpu.PrefetchScalarGridSpec(
            num_scalar_prefetch=2,
            grid=(len(plan),),
            in_specs=[pl.BlockSpec((d, width), lambda j, src, zero: (0, src[j]))],
            out_specs=pl.BlockSpec((d, width), lambda j, src, zero: (0, j))),
        compiler_params=_cparams(1),
        name=name,
    )(jnp.asarray(src), jnp.asarray(zero), w)


def _inproj_kernel(x_ref, nw_ref, shift_ref, scale_ref, w_ref, o_ref, h_ref):
    @pl.when(pl.program_id(1) == 0)
    def _():
        x = x_ref[...]
        y = x * lax.rsqrt(jnp.mean(x * x, axis=-1, keepdims=True) + NORM_EPS) * nw_ref[...]
        h_ref[...] = (y * (1.0 + scale_ref[...]) + shift_ref[...]).astype(BF16)

    o_ref[...] = jnp.dot(h_ref[...], w_ref[...], preferred_element_type=F32)


def in_projection(x, norm_w, mod, mod_row, w, name, tm=1024, tn=512):
    m, d = x.shape
    n = w.shape[1]
    tm = min(tm, m)
    return pl.pallas_call(
        _inproj_kernel,
        out_shape=jax.ShapeDtypeStruct((m, n), F32),
        grid=(m // tm, n // tn),
        in_specs=[pl.BlockSpec((tm, d), lambda i, j: (i, 0)),
                  pl.BlockSpec((1, d), lambda i, j: (0, 0)),
                  pl.BlockSpec((None, 1, d), lambda i, j: (mod_row(i), 0, 0)),
                  pl.BlockSpec((None, 1, d), lambda i, j: (mod_row(i), 0, 1)),
                  pl.BlockSpec((d, tn), lambda i, j: (0, j))],
        out_specs=pl.BlockSpec((tm, tn), lambda i, j: (i, j)),
        scratch_shapes=[pltpu.VMEM((tm, d), BF16)],
        compiler_params=_cparams(2),
        name=name,
    )(x, norm_w.reshape(1, d), mod, mod, w)


def _outproj_kernel(ya_ref, yb_ref, g_ref, w_ref, x_ref, gm_ref, *rest, ka, final):
    o_ref = rest[-1]
    sg = _silu(g_ref[...])
    acc = _dot(ya_ref[...].astype(F32) * sg[:, :ka], w_ref[:ka, :])
    acc += _dot(yb_ref[...].astype(F32) * sg[:, ka:], w_ref[ka:, :])
    out = x_ref[...] + gm_ref[...] * acc
    if final:
        out = out * lax.rsqrt(jnp.mean(out * out, axis=-1, keepdims=True) + NORM_EPS) * rest[0][...]
    o_ref[...] = out


def out_projection(ya, yb, gate, w, x, mod, mod_row, name, final_w=None, tm=256):
    m, d = x.shape
    ka, kb = ya.shape[1], yb.shape[1]
    tm = min(tm, m)
    ins = [ya, yb, gate, w, x, mod]
    specs = [pl.BlockSpec((tm, ka), lambda i: (i, 0)),
             pl.BlockSpec((tm, kb), lambda i: (i, 0)),
             pl.BlockSpec((tm, ka + kb), lambda i: (i, 0)),
             pl.BlockSpec((ka + kb, d), lambda i: (0, 0)),
             pl.BlockSpec((tm, d), lambda i: (i, 0)),
             pl.BlockSpec((None, 1, d), lambda i: (mod_row(i), 0, 2))]
    if final_w is not None:
        ins.append(final_w.reshape(1, d))
        specs.append(pl.BlockSpec((1, d), lambda i: (0, 0)))
    return pl.pallas_call(
        functools.partial(_outproj_kernel, ka=ka, final=final_w is not None),
        out_shape=jax.ShapeDtypeStruct((m, d), F32),
        grid=(m // tm,),
        in_specs=specs,
        out_specs=pl.BlockSpec((tm, d), lambda i: (i, 0)),
        compiler_params=_cparams(1),
        name=name,
    )(*ins)


def _softmax_pv(q, parts, scale):
    c = scale * LOG2E
    m = denom = num = None
    for k, bias, v in parts:
        x = _dot_nt(q, k) * c
        if bias is not None:
            x = x + bias
        m_part = jnp.max(x, axis=-1, keepdims=True)
        if m is None:
            m = m_part
            p = jnp.exp2(x - m)
            denom = jnp.sum(p, axis=-1, keepdims=True)
            num = _dot(p, v)
        else:
            m_new = jnp.maximum(m, m_part)
            alpha = jnp.exp2(m - m_new)
            p = jnp.exp2(x - m_new)
            denom = denom * alpha + jnp.sum(p, axis=-1, keepdims=True)
            num = num * alpha + _dot(p, v)
            m = m_new
    return num / denom


def _softmax_pv_joint(q, parts, scale):
    c = scale * LOG2E
    logits = [_dot_nt(q, k) * c if bias is None else _dot_nt(q, k) * c + bias for k, bias, _ in parts]
    m = functools.reduce(jnp.maximum, [jnp.max(x, axis=-1, keepdims=True) for x in logits])
    ps = [jnp.exp2(x - m) for x in logits]
    denom = functools.reduce(lambda a, b: a + b, [jnp.sum(p, axis=-1, keepdims=True) for p in ps])
    num = functools.reduce(lambda a, b: a + b, [_dot(p, v) for p, (_, _, v) in zip(ps, parts)])
    return num / denom


def _na_kernel(idx_ref, q_ref, k_ref, v_ref, kc_ref, vc_ref, tab_ref, o_ref, bias_ref, *, rows, scale):
    j = pl.program_id(2)
    npair = NA_KROWS // 2
    start = pl.multiple_of(jnp.clip(NA_QROWS * j - NA_WIN_ROWS // 2, 0, rows - NA_KROWS) * GRID_W, GRID_W)
    win = pl.ds(start, NA_KROWS * GRID_W)
    for hd in range(NA_STEP_HEADS):
        sl = slice(hd * LANES, (hd + 1) * LANES)
        for qr in range(NA_QROWS):
            for m in range(npair):
                tile = tab_ref[hd, idx_ref[(j * NA_QROWS + qr) * npair + m]]
                bias_ref[hd, qr * GRID_W:(qr + 1) * GRID_W, m * LANES:(m + 1) * LANES] = tile
        parts = [(k_ref[win, sl], bias_ref[hd], v_ref[win, sl]), (kc_ref[:, sl], None, vc_ref[:, sl])]
        o_ref[:, sl] = _softmax_pv_joint(q_ref[:, sl], parts, scale)


def _na_tile_ids(rows):
    kh = min(NA_WIN_ROWS, rows)
    n = 2 * NA_WIN_ROWS - 1
    ids = []
    for j in range(rows // NA_QROWS):
        ks = int(np.clip(NA_QROWS * j - NA_WIN_ROWS // 2, 0, rows - NA_KROWS))
        for qr in range(NA_QROWS):
            r = NA_QROWS * j + qr
            r0 = int(np.clip(r - kh // 2, 0, rows - kh))
            for m in range(NA_KROWS // 2):
                kr = ks + 2 * m
                dy = kr - r + NA_WIN_ROWS - 1
                first, second = r0 <= kr < r0 + kh, r0 <= kr + 1 < r0 + kh
                ids.append(dy if first and second else n + dy if first else 2 * n + dy + 1 if second else 3 * n)
    return np.array(ids, np.int32)


def _na_bias_tiles(rpb):
    col = np.arange(GRID_W)
    cs = np.clip(col - NA_WIN_COLS // 2, 0, GRID_W - NA_WIN_COLS)
    kc = col[None, :]
    in_win = (kc >= cs[:, None]) & (kc < cs[:, None] + NA_WIN_COLS)
    dx = np.where(in_win, kc - col[:, None] + NA_WIN_COLS - 1, 0)
    pick = jnp.asarray(dx[:, :, None] == np.arange(rpb.shape[-1]), F32)
    picked = jnp.einsum("hyj,cqj->hycq", rpb, pick, precision=lax.Precision.HIGHEST)
    tab = jnp.where(in_win[None, None], picked * LOG2E, NEG_INF)
    neg = jnp.full_like(tab[:, :1], NEG_INF)
    nxt = jnp.concatenate([tab[:, 1:], neg], axis=1)
    both = jnp.concatenate([tab, nxt], axis=-1)
    first = jnp.concatenate([tab, jnp.broadcast_to(neg, tab.shape)], axis=-1)
    second = jnp.concatenate([jnp.broadcast_to(neg, tab.shape), tab], axis=-1)
    return jnp.concatenate([both, first, second, jnp.concatenate([neg, neg], axis=-1)], axis=1)


def na_attention(z, z_c, rpb, batch, t, l):
    rows = t // GRID_W
    nj = rows // NA_QROWS
    tq = NA_QROWS * GRID_W
    ids = jnp.asarray(_na_tile_ids(rows))
    tiles = _na_bias_tiles(rpb)
    nh = NA_STEP_HEADS
    wide = nh * LANES
    q0, k0, v0 = (EV_Q_BLK + i * NA_HEADS for i in range(3))
    assert q0 % nh == 0 and NA_HEADS % nh == 0, "head groups must start on a whole column block"
    return pl.pallas_call(
        functools.partial(_na_kernel, rows=rows, scale=NA_HEAD_DIM ** -0.5),
        out_shape=jax.ShapeDtypeStruct((batch * t, NA_WIDTH), F32),
        grid_spec=pltpu.PrefetchScalarGridSpec(
            num_scalar_prefetch=1,
            grid=(NA_HEADS // nh, batch, nj),
            in_specs=[pl.BlockSpec((tq, wide), lambda h, b, j, g: (b * nj + j, q0 // nh + h)),
                      pl.BlockSpec((t, wide), lambda h, b, j, g: (b, k0 // nh + h)),
                      pl.BlockSpec((t, wide), lambda h, b, j, g: (b, v0 // nh + h)),
                      pl.BlockSpec((l, wide), lambda h, b, j, g: (b, k0 // nh + h)),
                      pl.BlockSpec((l, wide), lambda h, b, j, g: (b, v0 // nh + h)),
                      pl.BlockSpec((nh,) + tiles.shape[1:], lambda h, b, j, g: (h, 0, 0, 0))],
            out_specs=pl.BlockSpec((tq, wide), lambda h, b, j, g: (b * nj + j, h)),
            scratch_shapes=[pltpu.VMEM((nh, tq, NA_KROWS * GRID_W), F32)]),
        compiler_params=_cparams(3),
        name="na_attention",
    )(ids, z, z, z, z_c, z_c, tiles)


def _dense_attn_kernel(q_ref, k_ref, v_ref, o_ref, *, scale):
    for hd in range(NA_STEP_HEADS):
        sl = slice(hd * LANES, (hd + 1) * LANES)
        o_ref[:, sl] = _softmax_pv(q_ref[:, sl], [(k_ref[:, sl], None, v_ref[:, sl])], scale)


def ctx_attention(z_c, batch, l):
    nh = NA_STEP_HEADS
    wide = nh * LANES
    q0, k0, v0 = (EV_Q_BLK + i * NA_HEADS for i in range(3))
    assert q0 % nh == 0 and NA_HEADS % nh == 0, "head groups must start on a whole column block"
    return pl.pallas_call(
        functools.partial(_dense_attn_kernel, scale=NA_HEAD_DIM ** -0.5),
        out_shape=jax.ShapeDtypeStruct((batch * l, NA_WIDTH), F32),
        grid=(batch, NA_HEADS // nh),
        in_specs=[pl.BlockSpec((l, wide), lambda b, h: (b, q0 // nh + h)),
                  pl.BlockSpec((l, wide), lambda b, h: (b, k0 // nh + h)),
                  pl.BlockSpec((l, wide), lambda b, h: (b, v0 // nh + h))],
        out_specs=pl.BlockSpec((l, wide), lambda b, h: (b, h)),
        compiler_params=_cparams(2),
        name="ctx_attention",
    )(z_c, z_c, z_c)


def _head_sum(x):
    first = lax.broadcasted_iota(jnp.int32, x.shape, 1) < RW_HEAD_DIM
    s0 = jnp.sum(jnp.where(first, x, 0.0), axis=-1, keepdims=True)
    s1 = jnp.sum(jnp.where(first, 0.0, x), axis=-1, keepdims=True)
    return jnp.where(first, s0, s1)


def _rw_terms_kernel(ur_ref, uk_ref, uv_ref, uwa_ref, pr_ref, pk_ref, pv_ref, pwa_ref, nr_ref, nk_ref, nv_ref,
                     nwa_ref, mu_ref, w0_ref, w2_ref, a0_ref, a2_ref, kk_ref, ka_ref, rk_ref,
                     rv_o, wk0_o, wk1_o, bk0_o, bk1_o, bonus_o, *, nt):
    i = pl.program_id(1)
    tm = ur_ref.shape[0]

    def shifted(u_ref, p_ref, n_ref, col0):
        u = u_ref[...]
        mu = mu_ref[:, col0:col0 + u.shape[1]]
        row = lax.broadcasted_iota(jnp.int32, u.shape, 0)
        before = jnp.where(i == 0, 0.0, p_ref[7:8, :])
        after = jnp.where(i == nt - 1, 0.0, n_ref[0:1, :])
        prev = jnp.where(row == 0, before, pltpu.roll(u, 1, axis=0))
        nxt = jnp.where(row == tm - 1, after, pltpu.roll(u, tm - 1, axis=0))
        return u + mu[0:1] * (prev - u) + mu[1:2] * (nxt - u)

    r_all = shifted(ur_ref, pr_ref, nr_ref, 0)
    k_all = shifted(uk_ref, pk_ref, nk_ref, RW_WIDTH)
    v_all = shifted(uv_ref, pv_ref, nv_ref, 2 * RW_WIDTH)
    wa = shifted(uwa_ref, pwa_ref, nwa_ref, 3 * RW_WIDTH)
    wd = jnp.tanh(wa[:, :LANES]).astype(BF16)
    ad = wa[:, LANES:].astype(BF16)
    first = lax.broadcasted_iota(jnp.int32, (tm, LANES), 1) < RW_HEAD_DIM

    def head_rows(x, y):
        return (jnp.where(first, x, pltpu.roll(y, RW_HEAD_DIM, axis=1)),
                jnp.where(first, pltpu.roll(x, RW_HEAD_DIM, axis=1), y))

    outs = (rv_o, wk0_o, wk1_o, bk0_o, bk1_o)
    even = [[] for _ in outs]
    odd = [[] for _ in outs]
    bonus = []
    for p in range(RW_WIDTH // LANES):
        sl = slice(p * LANES, (p + 1) * LANES)
        r, k, v = r_all[:, sl], k_all[:, sl], v_all[:, sl]
        kk = k * kk_ref[:, sl]
        kk = kk / jnp.maximum(jnp.sqrt(_head_sum(kk * kk)), 1e-12)
        kd_sum = jnp.zeros_like(k)
        per_dir = []
        for d in range(2):
            w = w0_ref[d:d + 1, sl] + jnp.dot(wd, w2_ref[d, :, sl].astype(BF16), preferred_element_type=F32)
            a = _sigmoid(a0_ref[d:d + 1, sl] + jnp.dot(ad, a2_ref[d, :, sl].astype(BF16), preferred_element_type=F32))
            kd = k * (1.0 + (a - 1.0) * ka_ref[:, sl])
            dec = jnp.exp(-RW_DECAY_SCALE * _sigmoid(w))
            per_dir.append(((dec, kd), (kk * a, kk)))
            kd_sum = kd_sum + kd
        pairs = [(r, v), per_dir[0][0], per_dir[1][0], per_dir[0][1], per_dir[1][1]]
        bonus.append(_head_sum(r * kd_sum * rk_ref[:, sl]) * v)
        for ev, od, (x, y) in zip(even, odd, pairs):
            e, o = head_rows(x, y)
            ev.append(e)
            od.append(o)
    for o_ref, ev, od in zip(outs, even, odd):
        o_ref[...] = jnp.swapaxes(jnp.stack(ev + od, axis=0), 0, 1)
    bonus_o[...] = jnp.swapaxes(jnp.stack(bonus, axis=0), 0, 1)


def rwkv_terms(z, n_seq, t, mu, w0, w2p, a0, a2p, k_k, k_a, r_k, tm=256):
    tm = min(tm, t)
    nt = t // tm
    npair = RW_WIDTH // LANES
    wide0 = EV_RW_BLK * LANES // RW_WIDTH
    wa_blk = (EV_RW_BLK + 3 * npair) // 2
    n8 = tm // 8

    def cur(width, cb):
        return pl.BlockSpec((tm, width), lambda b, i: (b * nt + i, cb))

    def before(width, cb):
        return pl.BlockSpec((8, width), lambda b, i: (jnp.maximum((b * nt + i) * n8 - 1, 0), cb))

    def after(width, cb):
        return pl.BlockSpec((8, width), lambda b, i: (jnp.minimum((b * nt + i + 1) * n8, n_seq * nt * n8 - 1), cb))

    cols = [(RW_WIDTH, wide0), (RW_WIDTH, wide0 + 1), (RW_WIDTH, wide0 + 2), (2 * LANES, wa_blk)]
    whole = lambda a: pl.BlockSpec(a.shape, lambda b, i: (0,) * a.ndim)
    params = (mu, w0, w2p, a0, a2p, k_k, k_a, r_k)
    out = pl.BlockSpec((tm, npair, LANES), lambda b, i: (i, b, 0))
    out2 = pl.BlockSpec((tm, 2 * npair, LANES), lambda b, i: (i, b, 0))
    return pl.pallas_call(
        functools.partial(_rw_terms_kernel, nt=nt),
        out_shape=[jax.ShapeDtypeStruct((t, n_seq * 2 * npair, LANES), F32)] * 5
        + [jax.ShapeDtypeStruct((t, n_seq * npair, LANES), F32)],
        grid=(n_seq, nt),
        in_specs=([cur(*c) for c in cols] + [before(*c) for c in cols] + [after(*c) for c in cols]
                  + [whole(a) for a in params]),
        out_specs=[out2] * 5 + [out],
        compiler_params=_cparams(2),
        name="rwkv_terms",
    )(*([z] * 12), *params)


def _to_lanes(n):
    q = jnp.concatenate([n, pltpu.roll(n, RW_HEAD_DIM, axis=1)], axis=0)
    return q.T[:RW_HEAD_DIM]


def _rw_scan_kernel(rvf, rvb, wk0, wk1, bk0, bk1, s0_ref, yf_ref, yb_ref, sfin_ref,
                    s_ref, z_ref, znext_ref, g_ref, ybuf_ref, *, chunk):
    n = RW_HEAD_DIM
    nb = rvf.shape[1] // (2 * RW_WIDTH // LANES)
    R, K, V, A, B = range(5)

    @pl.when(pl.program_id(0) == 0)
    def _():
        s_ref[...] = s0_ref[...]

    def convert(t_src):
        def pair_tiles(f, bw):
            both = (f[t_src], bw[chunk - 1 - t_src])
            x = jnp.concatenate([src[(b * 2 + par) * 8:(b * 2 + par) * 8 + 8]
                                 for par in range(2) for src in both for b in range(nb)], axis=0)
            xt = x.T
            return xt[:n], xt[n:]

        r, v = pair_tiles(rvf, rvb)
        w, k = pair_tiles(wk0, wk1)
        b, kk = pair_tiles(bk0, bk1)
        g_prev = g_ref[0]
        g = g_prev * w
        g_ref[0] = g
        inv = 1.0 / g
        znext_ref[R] = r * g
        znext_ref[K] = k * inv
        znext_ref[V] = v
        znext_ref[A] = -(kk * g_prev)
        znext_ref[B] = b * inv

    def emit_y(t):
        rows = _to_lanes(ybuf_ref[...])
        yf_ref[t] = rows[:n // 2]
        yb_ref[chunk - 1 - t] = rows[n // 2:]

    g_ref[0] = jnp.ones((n, LANES), F32)
    ybuf_ref[...] = jnp.zeros((n, LANES), F32)
    convert(0)

    def step(t, carry):
        emit_y(jnp.maximum(t - 1, 0))
        z_ref[...] = znext_ref[...]
        g_ref[1] = g_ref[0]
        convert(jnp.minimum(t + 1, chunk - 1))
        sa = jnp.zeros((n, LANES), F32)
        for k in range(n):
            sa = sa + s_ref[k] * z_ref[A, k:k + 1, :]
        vt = z_ref[V]
        y = jnp.zeros((n, LANES), F32)
        for k in range(n):
            sk = s_ref[k] + sa * z_ref[B, k:k + 1, :] + vt * z_ref[K, k:k + 1, :]
            s_ref[k] = sk
            y = y + sk * z_ref[R, k:k + 1, :]
        ybuf_ref[...] = y
        return carry

    lax.fori_loop(0, chunk, step, 0)
    emit_y(chunk - 1)
    for k in range(n):
        s_ref[k] = s_ref[k] * g_ref[1, k:k + 1, :]
    sfin_ref[...] = s_ref[...]


def rwkv_scan(rv, wk0, wk1, bk0, bk1, s0):
    t, nrow, _ = rv.shape
    n = RW_HEAD_DIM
    chunk = RW_SCAN_CHUNK
    nc = t // chunk
    assert nrow == LANES // 2, "the states of both directions fill the 128 lanes"
    fwd = pl.BlockSpec((chunk, nrow, LANES), lambda i: (i, 0, 0))
    bwd = pl.BlockSpec((chunk, nrow, LANES), lambda i: (nc - 1 - i, 0, 0))
    yfwd = pl.BlockSpec((chunk, nrow // 2, LANES), lambda i: (i, 0, 0))
    ybwd = pl.BlockSpec((chunk, nrow // 2, LANES), lambda i: (nc - 1 - i, 0, 0))
    state = pl.BlockSpec((n, n, LANES), lambda i: (0, 0, 0))
    slab = jax.ShapeDtypeStruct((t, nrow // 2, LANES), F32)
    return pl.pallas_call(
        functools.partial(_rw_scan_kernel, chunk=chunk),
        out_shape=[slab, slab, jax.ShapeDtypeStruct((n, n, LANES), F32)],
        grid=(nc,),
        in_specs=[fwd, bwd, fwd, bwd, fwd, bwd, state],
        out_specs=[yfwd, ybwd, state],
        scratch_shapes=[pltpu.VMEM((n, n, LANES), F32), pltpu.VMEM((5, n, LANES), F32),
                        pltpu.VMEM((5, n, LANES), F32), pltpu.VMEM((2, n, LANES), F32),
                        pltpu.VMEM((n, LANES), F32)],
        compiler_params=_cparams(1),
        name="rwkv_scan",
    )(rv, rv, wk0, wk1, bk0, bk1, s0)


def _rw_readout_kernel(yf_ref, yb_ref, bonus_ref, lw_ref, lb_ref, o_ref):
    npair = RW_WIDTH // LANES
    y_all = jnp.swapaxes(yf_ref[...] + yb_ref[...], 0, 1)
    bonus_all = jnp.swapaxes(bonus_ref[...], 0, 1)
    for q in range(yf_ref.shape[1]):
        b, p = divmod(q, npair)
        sl = slice(p * LANES, (p + 1) * LANES)
        y = y_all[q]
        mu = _head_sum(y) * (1.0 / RW_HEAD_DIM)
        d = y - mu
        var = _head_sum(d * d) * (1.0 / RW_HEAD_DIM)
        o_ref[b, :, sl] = d * lax.rsqrt(var + RW_GN_EPS) * lw_ref[:, sl] + lb_ref[:, sl] + bonus_all[q]


def rwkv_readout(y_f, y_b, bonus, ln_w, ln_b, tm=256):
    t, nq, _ = y_f.shape
    tm = min(tm, t)
    batch = nq // (RW_WIDTH // LANES)
    slab = pl.BlockSpec((tm, nq, LANES), lambda i: (i, 0, 0))
    vec = pl.BlockSpec((1, RW_WIDTH), lambda i: (0, 0))
    return pl.pallas_call(
        _rw_readout_kernel,
        out_shape=jax.ShapeDtypeStruct((batch, t, RW_WIDTH), F32),
        grid=(t // tm,),
        in_specs=[slab, slab, slab, vec, vec],
        out_specs=pl.BlockSpec((batch, tm, RW_WIDTH), lambda i: (0, i, 0)),
        compiler_params=_cparams(1),
        name="rwkv_readout",
    )(y_f, y_b, bonus, ln_w.reshape(1, -1), ln_b.reshape(1, -1))


def _rms(x, w):
    return x * lax.rsqrt(jnp.mean(x * x, axis=-1, keepdims=True) + NORM_EPS) * w


def _rope(x, cos, sin):
    odd = (lax.broadcasted_iota(jnp.int32, x.shape, 1) & 1) == 1
    swapped = jnp.where(odd, pltpu.roll(x, 1, axis=1), pltpu.roll(x, LANES - 1, axis=1))
    return x * cos + swapped * sin


def _mla_q_kernel(c_ref, nw_ref, w_ref, cos_ref, sin_ref, q_ref):
    h = _rms(c_ref[...], nw_ref[...]).astype(BF16)
    acc = jnp.dot(h, w_ref[...], preferred_element_type=F32)
    cos, sin = cos_ref[...], sin_ref[...]
    for hd in range(MLA_HEADS):
        lo = hd * MLA_QK_PAD
        q_ref[:, lo:lo + MLA_NOPE] = acc[:, lo:lo + MLA_NOPE].astype(BF16)
        q_ref[:, lo + MLA_NOPE:lo + MLA_QK_PAD] = _rope(acc[:, lo + MLA_NOPE:lo + MLA_QK_PAD], cos, sin).astype(BF16)


def mla_queries(z, q_norm, w_uq_pad, cos, sin, t, tm=512):
    m = z.shape[0]
    tm = min(tm, t)
    nt = t // tm
    width = MLA_HEADS * MLA_QK_PAD
    return pl.pallas_call(
        _mla_q_kernel,
        out_shape=jax.ShapeDtypeStruct((m, width), BF16),
        grid=(m // tm,),
        in_specs=[pl.BlockSpec((tm, MLA_Q_RANK), lambda i: (i, GATE_COLS // MLA_Q_RANK)),
                  pl.BlockSpec((1, MLA_Q_RANK), lambda i: (0, 0)),
                  pl.BlockSpec((MLA_Q_RANK, width), lambda i: (0, 0)),
                  pl.BlockSpec((tm, LANES), lambda i: (i % nt, 0)),
                  pl.BlockSpec((tm, LANES), lambda i: (i % nt, 0))],
        out_specs=pl.BlockSpec((tm, width), lambda i: (i, 0)),
        compiler_params=_cparams(1),
        name="mla_queries",
    )(z, q_norm.reshape(1, -1), w_uq_pad, cos, sin)


def _mla_kv_kernel(c_ref, pe_ref, nw_ref, w_ref, cos_ref, sin_ref, k_ref, v_ref):
    h = _rms(c_ref[...], nw_ref[...]).astype(BF16)
    acc = jnp.dot(h, w_ref[...], preferred_element_type=F32)
    pe = _rope(pe_ref[...], cos_ref[...], sin_ref[...]).astype(BF16)
    for hd in range(MLA_HEADS):
        src = hd * (MLA_NOPE + MLA_V)
        dst = hd * MLA_QK_PAD
        k_ref[:, dst:dst + MLA_NOPE] = acc[:, src:src + MLA_NOPE].astype(BF16)
        k_ref[:, dst + MLA_NOPE:dst + MLA_QK_PAD] = pe
        v_ref[:, hd * MLA_V:(hd + 1) * MLA_V] = acc[:, src + MLA_NOPE:src + MLA_NOPE + MLA_V].astype(BF16)


def mla_keys_values(z, kv_norm, w_ukv, cos, sin, t, tm=512):
    m = z.shape[0]
    tm = min(tm, t)
    nt = t // tm
    return pl.pallas_call(
        _mla_kv_kernel,
        out_shape=[jax.ShapeDtypeStruct((m, MLA_HEADS * MLA_QK_PAD), BF16),
                   jax.ShapeDtypeStruct((m, MLA_WIDTH), BF16)],
        grid=(m // tm,),
        in_specs=[pl.BlockSpec((tm, MLA_KV_RANK), lambda i: (i, (GATE_COLS + MLA_Q_RANK) // MLA_KV_RANK)),
                  pl.BlockSpec((tm, LANES), lambda i: (i, OD_A_BLK + (MLA_Q_RANK + MLA_KV_RANK) // LANES)),
                  pl.BlockSpec((1, MLA_KV_RANK), lambda i: (0, 0)),
                  pl.BlockSpec(w_ukv.shape, lambda i: (0, 0)),
                  pl.BlockSpec((tm, LANES), lambda i: (i % nt, 0)),
                  pl.BlockSpec((tm, LANES), lambda i: (i % nt, 0))],
        out_specs=[pl.BlockSpec((tm, MLA_HEADS * MLA_QK_PAD), lambda i: (i, 0)),
                   pl.BlockSpec((tm, MLA_WIDTH), lambda i: (i, 0))],
        compiler_params=_cparams(1),
        name="mla_keys_values",
    )(z, z, kv_norm.reshape(1, -1), w_ukv, cos, sin)


def _mla_attn_kernel(q_ref, k_ref, v_ref, kc_ref, vc_ref, o_ref, *, scale):
    parts = [(kc_ref[...], None, vc_ref[...])]
    nkeys = k_ref.shape[0]
    step = min(MLA_KEY_CHUNK, nkeys)
    for lo in range(0, nkeys, step):
        parts.append((k_ref[lo:lo + step, :], None, v_ref[lo:lo + step, :]))
    o_ref[...] = _softmax_pv(q_ref[...], parts, scale)


def mla_attention(q, k, v, kc, vc, batch, t, l, tq=2048):
    tq = min(tq, t)
    nq = t // tq
    return pl.pallas_call(
        functools.partial(_mla_attn_kernel, scale=(MLA_NOPE + MLA_ROPE) ** -0.5),
        out_shape=jax.ShapeDtypeStruct((batch * t, MLA_WIDTH), F32),
        grid=(batch, MLA_HEADS, nq),
        in_specs=[pl.BlockSpec((tq, MLA_QK_PAD), lambda b, h, i: (b * nq + i, h)),
                  pl.BlockSpec((t, MLA_QK_PAD), lambda b, h, i: (b, h)),
                  pl.BlockSpec((t, MLA_V), lambda b, h, i: (b, h)),
                  pl.BlockSpec((l, MLA_QK_PAD), lambda b, h, i: (b, h)),
                  pl.BlockSpec((l, MLA_V), lambda b, h, i: (b, h))],
        out_specs=pl.BlockSpec((tq, MLA_V), lambda b, h, i: (b * nq + i, h)),
        compiler_params=_cparams(3),
        name="mla_attention",
    )(q, k, v, kc, vc)


def _rope_tables(t):
    tok = np.arange(t)
    pos = np.stack([tok // GRID_W, tok % GRID_W], axis=-1).astype(np.float32)
    n_freq = MLA_ROPE // 4
    inv = (ROPE_BASE ** (-jnp.arange(n_freq, dtype=F32) / n_freq))
    ang = (jnp.asarray(pos)[:, :, None] * inv).reshape(t, MLA_ROPE // 2)
    cos = jnp.repeat(jnp.cos(ang), 2, axis=-1)
    sin = jnp.repeat(jnp.sin(ang), 2, axis=-1) * jnp.tile(jnp.array([-1.0, 1.0], F32), MLA_ROPE // 2)
    pad = ((0, 0), (0, LANES - MLA_ROPE))
    return jnp.pad(cos, pad), jnp.pad(sin, pad)


def _split3(x):
    hi = x.astype(BF16)
    r1 = x - hi.astype(F32)
    mid = r1.astype(BF16)
    lo = (r1 - mid.astype(F32)).astype(BF16)
    return hi, mid, lo


def _hg_kernel(q_ref, f_ref, v_ref, lb_ref, s0_ref, *rest, reverse, finalize, emit_state):
    st_ref = rest[-1]
    rest = rest[:-1]
    if finalize:
        prev_ref, nw_ref = rest[0], rest[1]
        rest = rest[2:]
    o_ref = rest[0]

    @pl.when(pl.program_id(2) == 0)
    def _():
        st_ref[...] = s0_ref[...]

    bt = q_ref.shape[0]
    nchunk = bt // HG_CHUNK
    order = range(nchunk - 1, -1, -1) if reverse else range(nchunk)
    ri = lax.broadcasted_iota(jnp.int32, (bt, bt), 0)
    ci = lax.broadcasted_iota(jnp.int32, (bt, bt), 1)
    same = (ri // HG_CHUNK) == (ci // HG_CHUNK)
    tri = same & ((ci >= ri) if reverse else (ci <= ri))
    tri_f = tri.astype(F32)
    tri_b = tri.astype(BF16)
    row_chunk = lax.broadcasted_iota(jnp.int32, (bt, HG_DK), 0) // HG_CHUNK
    in_chunk = [row_chunk == c for c in range(nchunk)]

    lb = lb_ref[...]
    fz = f_ref[...]
    log_sig = jnp.minimum(fz, 0.0) - jnp.log1p(jnp.exp(-jnp.abs(fz)))
    la = jnp.log(lb)
    lbb = jnp.log1p(-lb) + log_sig
    log_f = jnp.maximum(la, lbb) + jnp.log1p(jnp.exp(-jnp.abs(la - lbb)))
    kf_all = (1.0 - lb) * _sigmoid(-fz)
    pieces = jnp.dot(tri_b, jnp.concatenate(_split3(log_f), axis=1), preferred_element_type=F32)
    width = log_f.shape[1]
    cum_all = pieces[:, :width] + pieces[:, width:2 * width] + pieces[:, 2 * width:]

    for hh in range(HG_STEP_HEADS):
        sl = slice(hh * HG_DK, (hh + 1) * HG_DK)
        cum, kf, v = cum_all[:, sl], kf_all[:, sl], v_ref[:, sl]
        last = [cum[c * HG_CHUNK:c * HG_CHUNK + 1] if reverse else cum[(c + 1) * HG_CHUNK - 1:(c + 1) * HG_CHUNK]
                for c in range(nchunk)]
        tot = jnp.concatenate([jnp.broadcast_to(x, (HG_CHUNK, HG_DK)) for x in last], axis=0)
        q_in = q_ref[:, sl] * jnp.exp(cum)
        k_in = kf * jnp.exp(-cum)
        k_end = kf * jnp.exp(tot - cum)
        o_intra = _dot(_dot_nt(q_in, k_in) * tri_f, v)
        v_by_chunk = jnp.concatenate([jnp.where(in_chunk[c], v, 0.0) for c in range(nchunk)], axis=1)
        d_st = _dot_tn(v_by_chunk, k_end)
        st = st_ref[hh]
        before = [None] * nchunk
        for c in order:
            before[c] = st
            st = st * jnp.exp(last[c]) + d_st[c * HG_DV:(c + 1) * HG_DV]
        st_ref[hh] = st
        q_by_chunk = jnp.concatenate([jnp.where(in_chunk[c], q_in, 0.0) for c in range(nchunk)], axis=1)
        o = o_intra + _dot_nt(q_by_chunk, jnp.concatenate(before, axis=1))
        if finalize:
            o = o + prev_ref[:, sl]
            o = o * lax.rsqrt(jnp.mean(o * o, axis=-1, keepdims=True) + NORM_EPS) * nw_ref[...]
        o_ref[:, sl] = o
        if emit_state:
            rest[1][hh] = st


def hgrn2_pass(z, lb, s0, batch, t, f_off, reverse, prev=None, norm_w=None, emit_state=False):
    bt = min(HG_BLOCK, t)
    nblk = t // bt
    nh = HG_STEP_HEADS
    wide = nh * LANES
    finalize = prev is not None
    assert OD_HG_BLK % nh == 0 and HG_HEADS % nh == 0, "head groups must start on a whole column block"
    pos = (lambda i: nblk - 1 - i) if reverse else (lambda i: i)
    blk = lambda off: pl.BlockSpec((bt, wide), lambda b, h, i, off=off: (b * nblk + pos(i), off // nh + h))
    state = pl.BlockSpec((None, nh, HG_DV, HG_DK), lambda b, h, i: (b, h, 0, 0))
    ins = [z, z, z, lb.reshape(1, -1), s0]
    specs = [blk(OD_HG_BLK), blk(OD_HG_BLK + f_off), blk(OD_HG_BLK + 3 * HG_HEADS),
             pl.BlockSpec((1, wide), lambda b, h, i: (0, h)), state]
    if finalize:
        ins += [prev, norm_w.reshape(1, -1)]
        specs += [blk(0), pl.BlockSpec((1, HG_DV), lambda b, h, i: (0, 0))]
    out_shape = [jax.ShapeDtypeStruct((batch * t, HG_WIDTH), F32)]
    out_specs = [blk(0)]
    if emit_state:
        out_shape.append(jax.ShapeDtypeStruct((batch, HG_HEADS, HG_DV, HG_DK), F32))
        out_specs.append(state)
    return pl.pallas_call(
        functools.partial(_hg_kernel, reverse=reverse, finalize=finalize, emit_state=emit_state),
        out_shape=out_shape,
        grid=(batch, HG_HEADS // nh, nblk),
        in_specs=specs,
        out_specs=out_specs,
        scratch_shapes=[pltpu.VMEM((nh, HG_DV, HG_DK), F32)],
        compiler_params=_cparams(3),
        name="hgrn2_" + ("bwd" if reverse else "fwd"),
    )(*ins)


def _even_layer(x, xc, mod, norm_w, w_in, w_out, rpb, rw_params, batch, t, l):
    s2 = 3 * NA_WIDTH + RW_SHIFT_COLS
    width = RW_PAD_COLS - RW_SHIFT_COLS
    n_gate, n_rest = GATE_COLS // width, s2 // width
    plan = [n_rest + i for i in range(n_gate)] + list(range(n_rest)) + [None]
    w_ext = cast_reorder(w_in, plan, width, "even_w_in")
    tiles_per_seq = max(t // 1024, 1)
    z = in_projection(x, norm_w, mod, lambda i: i // tiles_per_seq, w_ext, "even_in_proj")
    z_c = in_projection(xc, norm_w, mod, lambda i: batch, w_ext, "even_in_proj_ctx")

    y_na = na_attention(z, z_c, rpb, batch, t, l)
    yc_na = ctx_attention(z_c, batch, l)

    mu, w0, w2, a0, a2, k_k, k_a, r_k, ln_w, ln_b = rw_params
    mu = jnp.pad(mu, ((0, 0), (0, RW_PAD_COLS - RW_SHIFT_COLS)))
    zpad = jnp.zeros((RW_RANK, RW_WIDTH), F32)
    w2p = jnp.stack([jnp.concatenate([w2[0], zpad]), jnp.concatenate([zpad, w2[1]])])
    a2p = jnp.stack([jnp.concatenate([a2[0], zpad]), jnp.concatenate([zpad, a2[1]])])
    vecs = (mu, w0, w2p, a0, a2p, k_k.reshape(1, -1), k_a.reshape(1, -1), r_k.reshape(1, -1))
    terms = rwkv_terms(z, batch, t, *vecs)
    terms_c = rwkv_terms(z_c, batch, l, *vecs)
    s_zero = jnp.zeros((RW_HEAD_DIM, RW_HEAD_DIM, LANES), F32)
    yc_f, yc_b, s_ctx = rwkv_scan(*terms_c[:5], s_zero)
    y_f, y_b, _ = rwkv_scan(*terms[:5], s_ctx)
    y_rw = rwkv_readout(y_f, y_b, terms[5], ln_w, ln_b).reshape(batch * t, RW_WIDTH)
    yc_rw = rwkv_readout(yc_f, yc_b, terms_c[5], ln_w, ln_b).reshape(batch * l, RW_WIDTH)

    w_out = w_out.astype(BF16)
    rows_per_seq = max(t // 256, 1)
    x = out_projection(y_na, y_rw, z, w_out, x, mod, lambda i: i // rows_per_seq, "even_out_proj")
    xc = out_projection(yc_na, yc_rw, z_c, w_out, xc, mod, lambda i: batch, "even_out_proj_ctx")
    return x, xc


def _odd_layer(x, xc, mod, norm_w, w_in, w_out, q_norm, w_uq, kv_norm, w_ukv, lb, hg_norm_w, final_w,
               batch, t, l):
    o3 = MLA_Q_RANK + MLA_KV_RANK + MLA_ROPE
    o4 = o3 + 4 * HG_FDIM
    zero = jnp.zeros((x.shape[1], MLA_A_COLS - o3), BF16)
    w_in = w_in.astype(BF16)
    w_ext = jnp.concatenate([w_in[:, o4:], w_in[:, :o3], zero, w_in[:, o3:o4]], axis=1)
    tiles_per_seq = max(t // 1024, 1)
    z = in_projection(x, norm_w, mod, lambda i: i // tiles_per_seq, w_ext, "odd_in_proj")
    z_c = in_projection(xc, norm_w, mod, lambda i: batch, w_ext, "odd_in_proj_ctx")

    cos, sin = _rope_tables(t)
    qk = MLA_NOPE + MLA_ROPE
    w_uq_pad = jnp.pad(w_uq.reshape(MLA_Q_RANK, MLA_HEADS, qk), ((0, 0), (0, 0), (0, MLA_QK_PAD - qk)))
    w_uq_pad = w_uq_pad.reshape(MLA_Q_RANK, MLA_HEADS * MLA_QK_PAD).astype(BF16)
    w_ukv = w_ukv.astype(BF16)
    q = mla_queries(z, q_norm, w_uq_pad, cos, sin, t)
    k, v = mla_keys_values(z, kv_norm, w_ukv, cos, sin, t)
    ones = jnp.ones((l, LANES), F32)
    kc, vc = mla_keys_values(z_c, kv_norm, w_ukv, ones, jnp.zeros_like(ones), l)
    y_mla = mla_attention(q, k, v, kc, vc, batch, t, l)

    s_zero = jnp.zeros((batch, HG_HEADS, HG_DV, HG_DK), F32)
    _, s_f = hgrn2_pass(z_c, lb, s_zero, batch, l, HG_HEADS, False, emit_state=True)
    _, s_b = hgrn2_pass(z_c, lb, s_zero, batch, l, 2 * HG_HEADS, True, emit_state=True)
    o_f, = hgrn2_pass(z, lb, s_f, batch, t, HG_HEADS, False)
    y_hg, = hgrn2_pass(z, lb, s_b, batch, t, 2 * HG_HEADS, True, prev=o_f, norm_w=hg_norm_w)

    rows_per_seq = max(t // 256, 1)
    return out_projection(y_mla, y_hg, z, w_out.astype(BF16), x, mod, lambda i: i // rows_per_seq,
                          "odd_out_proj", final_w=final_w)


def kernel(x, c, ctx, c_ctx, ada_w, ada_b, norm_w, e_w_in, e_w_out, na_rpb, rw_mu, rw_w0, rw_w2, rw_a0, rw_a2, rw_k_k, rw_k_a, rw_r_k, rw_ln_w, rw_ln_b, o_w_in, o_w_out, mla_q_norm, mla_w_uq, mla_kv_norm, mla_w_ukv, hg_lower_bounds, hg_norm_w, final_norm_w):
    batch, t, d = x.shape
    l = ctx.shape[1]
    assert ada_w.shape[0] == 2, "one even and one odd layer"
    cond = jnp.concatenate([c, c_ctx[None, :], jnp.zeros((8 - batch - 1, d), F32)], axis=0)
    mod = modulation(cond, ada_w, ada_b)
    s = jax.nn.softmax(hg_lower_bounds.astype(F32), axis=0)
    lower = jnp.cumsum(s, axis=0) - s[0]

    xf, xcf = x.reshape(batch * t, d), ctx.reshape(batch * l, d)
    rw_params = (rw_mu[0], rw_w0[0], rw_w2[0], rw_a0[0], rw_a2[0], rw_k_k[0], rw_k_a[0], rw_r_k[0],
                 rw_ln_w[0], rw_ln_b[0])
    xf, xcf = _even_layer(xf, xcf, mod[0].reshape(8, 1, 3 * d), norm_w[0], e_w_in[0], e_w_out[0], na_rpb[0],
                          rw_params, batch, t, l)
    out = _odd_layer(xf, xcf, mod[1].reshape(8, 1, 3 * d), norm_w[1], o_w_in[0], o_w_out[0], mla_q_norm[0],
                     mla_w_uq[0], mla_kv_norm[0], mla_w_ukv[0], lower[1], hg_norm_w[0], final_norm_w,
                     batch, t, l)
    return out.reshape(batch, t, d)
```

```python
import functools
import math

import numpy as np
import jax
import jax.numpy as jnp
from jax import lax
from jax.experimental import pallas as pl
from jax.experimental.pallas import tpu as pltpu

F32 = jnp.float32
BF16 = jnp.bfloat16

GRID_W = 64
NORM_EPS = 1e-6
ROPE_BASE = 10000.0

NA_HEADS = 8
NA_HEAD_DIM = 128
NA_WIDTH = NA_HEADS * NA_HEAD_DIM
NA_WIN_ROWS = 8
NA_WIN_COLS = 16
NA_QROWS = 8
NA_KROWS = 16
NA_STEP_HEADS = 4

RW_HEAD_DIM = 64
RW_HEADS = 16
RW_WIDTH = RW_HEADS * RW_HEAD_DIM
RW_RANK = 64
RW_SHIFT_COLS = 3 * RW_WIDTH + 4 * RW_RANK
RW_PAD_COLS = 3584
RW_GN_EPS = 64e-5
RW_SCAN_CHUNK = 64
RW_DECAY_SCALE = math.exp(-0.5)

MLA_HEADS = 8
MLA_Q_RANK = 512
MLA_KV_RANK = 512
MLA_NOPE = 128
MLA_ROPE = 64
MLA_V = 128
MLA_QK_PAD = 256
MLA_WIDTH = MLA_HEADS * MLA_V
MLA_A_COLS = 1536
MLA_KEY_CHUNK = 1024

HG_HEADS = 8
HG_DK = 128
HG_DV = 128
HG_FDIM = HG_HEADS * HG_DK
HG_WIDTH = HG_HEADS * HG_DV
HG_CHUNK = 32
HG_BLOCK = 256
HG_STEP_HEADS = 4

LANES = 128

GATE_COLS = 2048
EV_Q_BLK = GATE_COLS // LANES
EV_RW_BLK = EV_Q_BLK + 3 * NA_WIDTH // LANES
OD_A_BLK = GATE_COLS // LANES
OD_HG_BLK = OD_A_BLK + MLA_A_COLS // LANES
VMEM_LIMIT = 48 * 1024 * 1024
NEG_INF = -1e30
LOG2E = math.log2(math.e)


def _cparams(n_axes):
    return pltpu.CompilerParams(dimension_semantics=("arbitrary",) * n_axes, vmem_limit_bytes=VMEM_LIMIT)


def _sigmoid(x):
    return 1.0 / (1.0 + jnp.exp(-x))


def _silu(x):
    return x * _sigmoid(x)


def _dot(a, b):
    return jnp.dot(a.astype(BF16), b.astype(BF16), preferred_element_type=F32)


def _dot_nt(a, b):
    return lax.dot_general(a.astype(BF16), b.astype(BF16), (((1,), (1,)), ((), ())), preferred_element_type=F32)


def _dot_tn(a, b):
    return lax.dot_general(a.astype(BF16), b.astype(BF16), (((0,), (0,)), ((), ())), preferred_element_type=F32)


def _mod_kernel(c_ref, w_ref, b_ref, o_ref):
    o_ref[...] = _dot(_silu(c_ref[...]), w_ref[...]) + b_ref[...]


def modulation(cond, ada_w, ada_b, tn=512):
    depth, d, n = ada_w.shape
    return pl.pallas_call(
        _mod_kernel,
        out_shape=jax.ShapeDtypeStruct((depth, 8, n), F32),
        grid=(depth, n // tn),
        in_specs=[pl.BlockSpec((8, d), lambda l, j: (0, 0)),
                  pl.BlockSpec((None, d, tn), lambda l, j: (l, 0, j)),
                  pl.BlockSpec((None, 1, tn), lambda l, j: (l, 0, j))],
        out_specs=pl.BlockSpec((None, 8, tn), lambda l, j: (l, 0, j)),
        compiler_params=_cparams(2),
        name="adaln_modulation",
    )(cond, ada_w, ada_b.reshape(depth, 1, n))


def _cast_reorder_kernel(src_ref, zero_ref, x_ref, o_ref):
    del src_ref
    o_ref[...] = jnp.where(zero_ref[pl.program_id(0)] != 0, 0.0, x_ref[...]).astype(BF16)


def cast_reorder(w, plan, width, name):
    d = w.shape[0]
    src = np.array([0 if a is None else a for a in plan], np.int32)
    zero = np.array([a is None for a in plan], np.int32)
    return pl.pallas_call(
        _cast_reorder_kernel,
        out_shape=jax.ShapeDtypeStruct((d, width * len(plan)), BF16),
        grid_spec=pltpu.PrefetchScalarGridSpec(
            num_scalar_prefetch=2,
            grid=(len(plan),),
            in_specs=[pl.BlockSpec((d, width), lambda j, src, zero: (0, src[j]))],
            out_specs=pl.BlockSpec((d, width), lambda j, src, zero: (0, j))),
        compiler_params=_cparams(1),
        name=name,
    )(jnp.asarray(src), jnp.asarray(zero), w)


def _inproj_kernel(x_ref, nw_ref, shift_ref, scale_ref, w_ref, o_ref, h_ref):
    @pl.when(pl.program_id(1) == 0)
    def _():
        x = x_ref[...]
        y = x * lax.rsqrt(jnp.mean(x * x, axis=-1, keepdims=True) + NORM_EPS) * nw_ref[...]
        h_ref[...] = (y * (1.0 + scale_ref[...]) + shift_ref[...]).astype(BF16)

    o_ref[...] = jnp.dot(h_ref[...], w_ref[...], preferred_element_type=F32)


def in_projection(x, norm_w, mod, mod_row, w, name, tm=1024, tn=512):
    m, d = x.shape
    n = w.shape[1]
    tm = min(tm, m)
    return pl.pallas_call(
        _inproj_kernel,
        out_shape=jax.ShapeDtypeStruct((m, n), F32),
        grid=(m // tm, n // tn),
        in_specs=[pl.BlockSpec((tm, d), lambda i, j: (i, 0)),
                  pl.BlockSpec((1, d), lambda i, j: (0, 0)),
                  pl.BlockSpec((None, 1, d), lambda i, j: (mod_row(i), 0, 0)),
                  pl.BlockSpec((None, 1, d), lambda i, j: (mod_row(i), 0, 1)),
                  pl.BlockSpec((d, tn), lambda i, j: (0, j))],
        out_specs=pl.BlockSpec((tm, tn), lambda i, j: (i, j)),
        scratch_shapes=[pltpu.VMEM((tm, d), BF16)],
        compiler_params=_cparams(2),
        name=name,
    )(x, norm_w.reshape(1, d), mod, mod, w)


def _outproj_kernel(ya_ref, yb_ref, g_ref, w_ref, x_ref, gm_ref, *rest, ka, final):
    o_ref = rest[-1]
    sg = _silu(g_ref[...])
    acc = _dot(ya_ref[...].astype(F32) * sg[:, :ka], w_ref[:ka, :])
    acc += _dot(yb_ref[...].astype(F32) * sg[:, ka:], w_ref[ka:, :])
    out = x_ref[...] + gm_ref[...] * acc
    if final:
        out = out * lax.rsqrt(jnp.mean(out * out, axis=-1, keepdims=True) + NORM_EPS) * rest[0][...]
    o_ref[...] = out


def out_projection(ya, yb, gate, w, x, mod, mod_row, name, final_w=None, tm=256):
    m, d = x.shape
    ka, kb = ya.shape[1], yb.shape[1]
    tm = min(tm, m)
    ins = [ya, yb, gate, w, x, mod]
    specs = [pl.BlockSpec((tm, ka), lambda i: (i, 0)),
             pl.BlockSpec((tm, kb), lambda i: (i, 0)),
             pl.BlockSpec((tm, ka + kb), lambda i: (i, 0)),
             pl.BlockSpec((ka + kb, d), lambda i: (0, 0)),
             pl.BlockSpec((tm, d), lambda i: (i, 0)),
             pl.BlockSpec((None, 1, d), lambda i: (mod_row(i), 0, 2))]
    if final_w is not None:
        ins.append(final_w.reshape(1, d))
        specs.append(pl.BlockSpec((1, d), lambda i: (0, 0)))
    return pl.pallas_call(
        functools.partial(_outproj_kernel, ka=ka, final=final_w is not None),
        out_shape=jax.ShapeDtypeStruct((m, d), F32),
        grid=(m // tm,),
        in_specs=specs,
        out_specs=pl.BlockSpec((tm, d), lambda i: (i, 0)),
        compiler_params=_cparams(1),
        name=name,
    )(*ins)


def _softmax_pv(q, parts, scale):
    c = scale * LOG2E
    m = denom = num = None
    for k, bias, v in parts:
        x = _dot_nt(q, k) * c
        if bias is not None:
            x = x + bias
        m_part = jnp.max(x, axis=-1, keepdims=True)
        if m is None:
            m = m_part
            p = jnp.exp2(x - m)
            denom = jnp.sum(p, axis=-1, keepdims=True)
            num = _dot(p, v)
        else:
            m_new = jnp.maximum(m, m_part)
            alpha = jnp.exp2(m - m_new)
            p = jnp.exp2(x - m_new)
            denom = denom * alpha + jnp.sum(p, axis=-1, keepdims=True)
            num = num * alpha + _dot(p, v)
            m = m_new
    return num / denom


def _softmax_pv_joint(q, parts, scale):
    c = scale * LOG2E
    logits = [_dot_nt(q, k) * c if bias is None else _dot_nt(q, k) * c + bias for k, bias, _ in parts]
    m = functools.reduce(jnp.maximum, [jnp.max(x, axis=-1, keepdims=True) for x in logits])
    ps = [jnp.exp2(x - m) for x in logits]
    denom = functools.reduce(lambda a, b: a + b, [jnp.sum(p, axis=-1, keepdims=True) for p in ps])
    num = functools.reduce(lambda a, b: a + b, [_dot(p, v) for p, (_, _, v) in zip(ps, parts)])
    return num / denom


def _na_kernel(idx_ref, q_ref, k_ref, v_ref, kc_ref, vc_ref, tab_ref, o_ref, bias_ref, *, rows, scale):
    j = pl.program_id(2)
    npair = NA_KROWS // 2
    start = pl.multiple_of(jnp.clip(NA_QROWS * j - NA_WIN_ROWS // 2, 0, rows - NA_KROWS) * GRID_W, GRID_W)
    win = pl.ds(start, NA_KROWS * GRID_W)
    for hd in range(NA_STEP_HEADS):
        sl = slice(hd * LANES, (hd + 1) * LANES)
        for qr in range(NA_QROWS):
            for m in range(npair):
                tile = tab_ref[hd, idx_ref[(j * NA_QROWS + qr) * npair + m]]
                bias_ref[hd, qr * GRID_W:(qr + 1) * GRID_W, m * LANES:(m + 1) * LANES] = tile
        parts = [(k_ref[win, sl], bias_ref[hd], v_ref[win, sl]), (kc_ref[:, sl], None, vc_ref[:, sl])]
        o_ref[:, sl] = _softmax_pv_joint(q_ref[:, sl], parts, scale)


def _na_tile_ids(rows):
    kh = min(NA_WIN_ROWS, rows)
    n = 2 * NA_WIN_ROWS - 1
    ids = []
    for j in range(rows // NA_QROWS):
        ks = int(np.clip(NA_QROWS * j - NA_WIN_ROWS // 2, 0, rows - NA_KROWS))
        for qr in range(NA_QROWS):
            r = NA_QROWS * j + qr
            r0 = int(np.clip(r - kh // 2, 0, rows - kh))
            for m in range(NA_KROWS // 2):
                kr = ks + 2 * m
                dy = kr - r + NA_WIN_ROWS - 1
                first, second = r0 <= kr < r0 + kh, r0 <= kr + 1 < r0 + kh
                ids.append(dy if first and second else n + dy if first else 2 * n + dy + 1 if second else 3 * n)
    return np.array(ids, np.int32)


def _na_bias_tiles(rpb):
    col = np.arange(GRID_W)
    cs = np.clip(col - NA_WIN_COLS // 2, 0, GRID_W - NA_WIN_COLS)
    kc = col[None, :]
    in_win = (kc >= cs[:, None]) & (kc < cs[:, None] + NA_WIN_COLS)
    dx = np.where(in_win, kc - col[:, None] + NA_WIN_COLS - 1, 0)
    pick = jnp.asarray(dx[:, :, None] == np.arange(rpb.shape[-1]), F32)
    picked = jnp.einsum("hyj,cqj->hycq", rpb, pick, precision=lax.Precision.HIGHEST)
    tab = jnp.where(in_win[None, None], picked * LOG2E, NEG_INF)
    neg = jnp.full_like(tab[:, :1], NEG_INF)
    nxt = jnp.concatenate([tab[:, 1:], neg], axis=1)
    both = jnp.concatenate([tab, nxt], axis=-1)
    first = jnp.concatenate([tab, jnp.broadcast_to(neg, tab.shape)], axis=-1)
    second = jnp.concatenate([jnp.broadcast_to(neg, tab.shape), tab], axis=-1)
    return jnp.concatenate([both, first, second, jnp.concatenate([neg, neg], axis=-1)], axis=1)


def na_attention(z, z_c, rpb, batch, t, l):
    rows = t // GRID_W
    nj = rows // NA_QROWS
    tq = NA_QROWS * GRID_W
    ids = jnp.asarray(_na_tile_ids(rows))
    tiles = _na_bias_tiles(rpb)
    nh = NA_STEP_HEADS
    wide = nh * LANES
    q0, k0, v0 = (EV_Q_BLK + i * NA_HEADS for i in range(3))
    assert q0 % nh == 0 and NA_HEADS % nh == 0, "head groups must start on a whole column block"
    return pl.pallas_call(
        functools.partial(_na_kernel, rows=rows, scale=NA_HEAD_DIM ** -0.5),
        out_shape=jax.ShapeDtypeStruct((batch * t, NA_WIDTH), F32),
        grid_spec=pltpu.PrefetchScalarGridSpec(
            num_scalar_prefetch=1,
            grid=(NA_HEADS // nh, batch, nj),
            in_specs=[pl.BlockSpec((tq, wide), lambda h, b, j, g: (b * nj + j, q0 // nh + h)),
                      pl.BlockSpec((t, wide), lambda h, b, j, g: (b, k0 // nh + h)),
                      pl.BlockSpec((t, wide), lambda h, b, j, g: (b, v0 // nh + h)),
                      pl.BlockSpec((l, wide), lambda h, b, j, g: (b, k0 // nh + h)),
                      pl.BlockSpec((l, wide), lambda h, b, j, g: (b, v0 // nh + h)),
                      pl.BlockSpec((nh,) + tiles.shape[1:], lambda h, b, j, g: (h, 0, 0, 0))],
            out_specs=pl.BlockSpec((tq, wide), lambda h, b, j, g: (b * nj + j, h)),
            scratch_shapes=[pltpu.VMEM((nh, tq, NA_KROWS * GRID_W), F32)]),
        compiler_params=_cparams(3),
        name="na_attention",
    )(ids, z, z, z, z_c, z_c, tiles)


def _dense_attn_kernel(q_ref, k_ref, v_ref, o_ref, *, scale):
    for hd in range(NA_STEP_HEADS):
        sl = slice(hd * LANES, (hd + 1) * LANES)
        o_ref[:, sl] = _softmax_pv(q_ref[:, sl], [(k_ref[:, sl], None, v_ref[:, sl])], scale)


def ctx_attention(z_c, batch, l):
    nh = NA_STEP_HEADS
    wide = nh * LANES
    q0, k0, v0 = (EV_Q_BLK + i * NA_HEADS for i in range(3))
    assert q0 % nh == 0 and NA_HEADS % nh == 0, "head groups must start on a whole column block"
    return pl.pallas_call(
        functools.partial(_dense_attn_kernel, scale=NA_HEAD_DIM ** -0.5),
        out_shape=jax.ShapeDtypeStruct((batch * l, NA_WIDTH), F32),
        grid=(batch, NA_HEADS // nh),
        in_specs=[pl.BlockSpec((l, wide), lambda b, h: (b, q0 // nh + h)),
                  pl.BlockSpec((l, wide), lambda b, h: (b, k0 // nh + h)),
                  pl.BlockSpec((l, wide), lambda b, h: (b, v0 // nh + h))],
        out_specs=pl.BlockSpec((l, wide), lambda b, h: (b, h)),
        compiler_params=_cparams(2),
        name="ctx_attention",
    )(z_c, z_c, z_c)


def _head_sum(x):
    first = lax.broadcasted_iota(jnp.int32, x.shape, 1) < RW_HEAD_DIM
    s0 = jnp.sum(jnp.where(first, x, 0.0), axis=-1, keepdims=True)
    s1 = jnp.sum(jnp.where(first, 0.0, x), axis=-1, keepdims=True)
    return jnp.where(first, s0, s1)


def _rw_terms_kernel(ur_ref, uk_ref, uv_ref, uwa_ref, pr_ref, pk_ref, pv_ref, pwa_ref, nr_ref, nk_ref, nv_ref,
                     nwa_ref, mu_ref, w0_ref, w2_ref, a0_ref, a2_ref, kk_ref, ka_ref, rk_ref,
                     rv_o, wk0_o, wk1_o, bk0_o, bk1_o, bonus_o, *, nt):
    i = pl.program_id(1)
    tm = ur_ref.shape[0]

    def shifted(u_ref, p_ref, n_ref, col0):
        u = u_ref[...]
        mu = mu_ref[:, col0:col0 + u.shape[1]]
        row = lax.broadcasted_iota(jnp.int32, u.shape, 0)
        before = jnp.where(i == 0, 0.0, p_ref[7:8, :])
        after = jnp.where(i == nt - 1, 0.0, n_ref[0:1, :])
        prev = jnp.where(row == 0, before, pltpu.roll(u, 1, axis=0))
        nxt = jnp.where(row == tm - 1, after, pltpu.roll(u, tm - 1, axis=0))
        return u + mu[0:1] * (prev - u) + mu[1:2] * (nxt - u)

    r_all = shifted(ur_ref, pr_ref, nr_ref, 0)
    k_all = shifted(uk_ref, pk_ref, nk_ref, RW_WIDTH)
    v_all = shifted(uv_ref, pv_ref, nv_ref, 2 * RW_WIDTH)
    wa = shifted(uwa_ref, pwa_ref, nwa_ref, 3 * RW_WIDTH)
    wd = jnp.tanh(wa[:, :LANES]).astype(BF16)
    ad = wa[:, LANES:].astype(BF16)
    first = lax.broadcasted_iota(jnp.int32, (tm, LANES), 1) < RW_HEAD_DIM

    def head_rows(x, y):
        return (jnp.where(first, x, pltpu.roll(y, RW_HEAD_DIM, axis=1)),
                jnp.where(first, pltpu.roll(x, RW_HEAD_DIM, axis=1), y))

    outs = (rv_o, wk0_o, wk1_o, bk0_o, bk1_o)
    even = [[] for _ in outs]
    odd = [[] for _ in outs]
    bonus = []
    for p in range(RW_WIDTH // LANES):
        sl = slice(p * LANES, (p + 1) * LANES)
        r, k, v = r_all[:, sl], k_all[:, sl], v_all[:, sl]
        kk = k * kk_ref[:, sl]
        kk = kk / jnp.maximum(jnp.sqrt(_head_sum(kk * kk)), 1e-12)
        kd_sum = jnp.zeros_like(k)
        per_dir = []
        for d in range(2):
            w = w0_ref[d:d + 1, sl] + jnp.dot(wd, w2_ref[d, :, sl].astype(BF16), preferred_element_type=F32)
            a = _sigmoid(a0_ref[d:d + 1, sl] + jnp.dot(ad, a2_ref[d, :, sl].astype(BF16), preferred_element_type=F32))
            kd = k * (1.0 + (a - 1.0) * ka_ref[:, sl])
            dec = jnp.exp(-RW_DECAY_SCALE * _sigmoid(w))
            per_dir.append(((dec, kd), (kk * a, kk)))
            kd_sum = kd_sum + kd
        pairs = [(r, v), per_dir[0][0], per_dir[1][0], per_dir[0][1], per_dir[1][1]]
        bonus.append(_head_sum(r * kd_sum * rk_ref[:, sl]) * v)
        for ev, od, (x, y) in zip(even, odd, pairs):
            e, o = head_rows(x, y)
            ev.append(e)
            od.append(o)
    for o_ref, ev, od in zip(outs, even, odd):
        o_ref[...] = jnp.swapaxes(jnp.stack(ev + od, axis=0), 0, 1)
    bonus_o[...] = jnp.swapaxes(jnp.stack(bonus, axis=0), 0, 1)


def rwkv_terms(z, n_seq, t, mu, w0, w2p, a0, a2p, k_k, k_a, r_k, tm=256):
    tm = min(tm, t)
    nt = t // tm
    npair = RW_WIDTH // LANES
    wide0 = EV_RW_BLK * LANES // RW_WIDTH
    wa_blk = (EV_RW_BLK + 3 * npair) // 2
    n8 = tm // 8

    def cur(width, cb):
        return pl.BlockSpec((tm, width), lambda b, i: (b * nt + i, cb))

    def before(width, cb):
        return pl.BlockSpec((8, width), lambda b, i: (jnp.maximum((b * nt + i) * n8 - 1, 0), cb))

    def after(width, cb):
        return pl.BlockSpec((8, width), lambda b, i: (jnp.minimum((b * nt + i + 1) * n8, n_seq * nt * n8 - 1), cb))

    cols = [(RW_WIDTH, wide0), (RW_WIDTH, wide0 + 1), (RW_WIDTH, wide0 + 2), (2 * LANES, wa_blk)]
    whole = lambda a: pl.BlockSpec(a.shape, lambda b, i: (0,) * a.ndim)
    params = (mu, w0, w2p, a0, a2p, k_k, k_a, r_k)
    out = pl.BlockSpec((tm, npair, LANES), lambda b, i: (i, b, 0))
    out2 = pl.BlockSpec((tm, 2 * npair, LANES), lambda b, i: (i, b, 0))
    return pl.pallas_call(
        functools.partial(_rw_terms_kernel, nt=nt),
        out_shape=[jax.ShapeDtypeStruct((t, n_seq * 2 * npair, LANES), F32)] * 5
        + [jax.ShapeDtypeStruct((t, n_seq * npair, LANES), F32)],
        grid=(n_seq, nt),
        in_specs=([cur(*c) for c in cols] + [before(*c) for c in cols] + [after(*c) for c in cols]
                  + [whole(a) for a in params]),
        out_specs=[out2] * 5 + [out],
        compiler_params=_cparams(2),
        name="rwkv_terms",
    )(*([z] * 12), *params)


def _to_lanes(n):
    q = jnp.concatenate([n, pltpu.roll(n, RW_HEAD_DIM, axis=1)], axis=0)
    return q.T[:RW_HEAD_DIM]


def _rw_scan_kernel(rvf, rvb, wk0, wk1, bk0, bk1, s0_ref, yf_ref, yb_ref, sfin_ref,
                    s_ref, z_ref, znext_ref, zland_ref, g_ref, sa_ref, ybuf_ref, *, chunk):
    n = RW_HEAD_DIM
    nb = rvf.shape[1] // (2 * RW_WIDTH // LANES)
    R, K, V, A, B = range(5)

    @pl.when(pl.program_id(0) == 0)
    def _():
        s_ref[...] = s0_ref[...]

    def convert(t_src):
        def pair_tiles(f, bw):
            both = (f[t_src], bw[chunk - 1 - t_src])
            x = jnp.concatenate([src[(b * 2 + par) * 8:(b * 2 + par) * 8 + 8]
                                 for par in range(2) for src in both for b in range(nb)], axis=0)
            xt = x.T
            return xt[:n], xt[n:]

        r, v = pair_tiles(rvf, rvb)
        w, k = pair_tiles(wk0, wk1)
        b, kk = pair_tiles(bk0, bk1)
        g_prev = g_ref[0]
        g = g_prev * w
        g_ref[0] = g
        inv = 1.0 / g
        zland_ref[R] = r * g
        zland_ref[K] = k * inv
        zland_ref[V] = v
        zland_ref[A] = -(kk * g_prev)
        zland_ref[B] = b * inv

    def emit_y(t):
        rows = _to_lanes(ybuf_ref[...])
        yf_ref[t] = rows[:n // 2]
        yb_ref[chunk - 1 - t] = rows[n // 2:]

    g_ref[0] = jnp.ones((n, LANES), F32)
    ybuf_ref[...] = jnp.zeros((n, LANES), F32)
    convert(0)
    znext_ref[...] = zland_ref[...]
    g_ref[1] = g_ref[0]
    convert(1)
    sa = jnp.zeros((n, LANES), F32)
    for k in range(n):
        sa = sa + s_ref[k] * znext_ref[A, k:k + 1, :]
    sa_ref[...] = sa

    def step(t, carry):
        emit_y(jnp.maximum(t - 1, 0))
        z_ref[...] = znext_ref[...]
        znext_ref[...] = zland_ref[...]
        g_ref[1] = jnp.where(t <= chunk - 2, g_ref[0], g_ref[1])
        convert(jnp.minimum(t + 2, chunk - 1))
        sa = sa_ref[...]
        vt = z_ref[V]
        y = jnp.zeros((n, LANES), F32)
        sa_next = jnp.zeros((n, LANES), F32)
        for k in range(n):
            sk = s_ref[k] + sa * z_ref[B, k:k + 1, :] + vt * z_ref[K, k:k + 1, :]
            s_ref[k] = sk
            y = y + sk * z_ref[R, k:k + 1, :]
            sa_next = sa_next + sk * znext_ref[A, k:k + 1, :]
        ybuf_ref[...] = y
        sa_ref[...] = sa_next
        return carry

    lax.fori_loop(0, chunk, step, 0)
    emit_y(chunk - 1)
    for k in range(n):
        s_ref[k] = s_ref[k] * g_ref[1, k:k + 1, :]
    sfin_ref[...] = s_ref[...]


def rwkv_scan(rv, wk0, wk1, bk0, bk1, s0):
    t, nrow, _ = rv.shape
    n = RW_HEAD_DIM
    chunk = RW_SCAN_CHUNK
    nc = t // chunk
    assert nrow == LANES // 2, "the states of both directions fill the 128 lanes"
    fwd = pl.BlockSpec((chunk, nrow, LANES), lambda i: (i, 0, 0))
    bwd = pl.BlockSpec((chunk, nrow, LANES), lambda i: (nc - 1 - i, 0, 0))
    yfwd = pl.BlockSpec((chunk, nrow // 2, LANES), lambda i: (i, 0, 0))
    ybwd = pl.BlockSpec((chunk, nrow // 2, LANES), lambda i: (nc - 1 - i, 0, 0))
    state = pl.BlockSpec((n, n, LANES), lambda i: (0, 0, 0))
    slab = jax.ShapeDtypeStruct((t, nrow // 2, LANES), F32)
    return pl.pallas_call(
        functools.partial(_rw_scan_kernel, chunk=chunk),
        out_shape=[slab, slab, jax.ShapeDtypeStruct((n, n, LANES), F32)],
        grid=(nc,),
        in_specs=[fwd, bwd, fwd, bwd, fwd, bwd, state],
        out_specs=[yfwd, ybwd, state],
        scratch_shapes=[pltpu.VMEM((n, n, LANES), F32), pltpu.VMEM((5, n, LANES), F32),
                        pltpu.VMEM((5, n, LANES), F32), pltpu.VMEM((5, n, LANES), F32),
                        pltpu.VMEM((2, n, LANES), F32), pltpu.VMEM((n, LANES), F32), pltpu.VMEM((n, LANES), F32)],
        compiler_params=_cparams(1),
        name="rwkv_scan",
    )(rv, rv, wk0, wk1, bk0, bk1, s0)


def _rw_readout_kernel(yf_ref, yb_ref, bonus_ref, lw_ref, lb_ref, o_ref):
    npair = RW_WIDTH // LANES
    y_all = jnp.swapaxes(yf_ref[...] + yb_ref[...], 0, 1)
    bonus_all = jnp.swapaxes(bonus_ref[...], 0, 1)
    for q in range(yf_ref.shape[1]):
        b, p = divmod(q, npair)
        sl = slice(p * LANES, (p + 1) * LANES)
        y = y_all[q]
        mu = _head_sum(y) * (1.0 / RW_HEAD_DIM)
        d = y - mu
        var = _head_sum(d * d) * (1.0 / RW_HEAD_DIM)
        o_ref[b, :, sl] = d * lax.rsqrt(var + RW_GN_EPS) * lw_ref[:, sl] + lb_ref[:, sl] + bonus_all[q]


def rwkv_readout(y_f, y_b, bonus, ln_w, ln_b, tm=256):
    t, nq, _ = y_f.shape
    tm = min(tm, t)
    batch = nq // (RW_WIDTH // LANES)
    slab = pl.BlockSpec((tm, nq, LANES), lambda i: (i, 0, 0))
    vec = pl.BlockSpec((1, RW_WIDTH), lambda i: (0, 0))
    return pl.pallas_call(
        _rw_readout_kernel,
        out_shape=jax.ShapeDtypeStruct((batch, t, RW_WIDTH), F32),
        grid=(t // tm,),
        in_specs=[slab, slab, slab, vec, vec],
        out_specs=pl.BlockSpec((batch, tm, RW_WIDTH), lambda i: (0, i, 0)),
        compiler_params=_cparams(1),
        name="rwkv_readout",
    )(y_f, y_b, bonus, ln_w.reshape(1, -1), ln_b.reshape(1, -1))


def _rms(x, w):
    return x * lax.rsqrt(jnp.mean(x * x, axis=-1, keepdims=True) + NORM_EPS) * w


def _rope(x, cos, sin):
    odd = (lax.broadcasted_iota(jnp.int32, x.shape, 1) & 1) == 1
    swapped = jnp.where(odd, pltpu.roll(x, 1, axis=1), pltpu.roll(x, LANES - 1, axis=1))
    return x * cos + swapped * sin


def _mla_q_kernel(c_ref, nw_ref, w_ref, cos_ref, sin_ref, q_ref):
    h = _rms(c_ref[...], nw_ref[...]).astype(BF16)
    acc = jnp.dot(h, w_ref[...], preferred_element_type=F32)
    cos, sin = cos_ref[...], sin_ref[...]
    for hd in range(MLA_HEADS):
        lo = hd * MLA_QK_PAD
        q_ref[:, lo:lo + MLA_NOPE] = acc[:, lo:lo + MLA_NOPE].astype(BF16)
        q_ref[:, lo + MLA_NOPE:lo + MLA_QK_PAD] = _rope(acc[:, lo + MLA_NOPE:lo + MLA_QK_PAD], cos, sin).astype(BF16)


def mla_queries(z, q_norm, w_uq_pad, cos, sin, t, tm=512):
    m = z.shape[0]
    tm = min(tm, t)
    nt = t // tm
    width = MLA_HEADS * MLA_QK_PAD
    return pl.pallas_call(
        _mla_q_kernel,
        out_shape=jax.ShapeDtypeStruct((m, width), BF16),
        grid=(m // tm,),
        in_specs=[pl.BlockSpec((tm, MLA_Q_RANK), lambda i: (i, GATE_COLS // MLA_Q_RANK)),
                  pl.BlockSpec((1, MLA_Q_RANK), lambda i: (0, 0)),
                  pl.BlockSpec((MLA_Q_RANK, width), lambda i: (0, 0)),
                  pl.BlockSpec((tm, LANES), lambda i: (i % nt, 0)),
                  pl.BlockSpec((tm, LANES), lambda i: (i % nt, 0))],
        out_specs=pl.BlockSpec((tm, width), lambda i: (i, 0)),
        compiler_params=_cparams(1),
        name="mla_queries",
    )(z, q_norm.reshape(1, -1), w_uq_pad, cos, sin)


def _mla_kv_kernel(c_ref, pe_ref, nw_ref, w_ref, cos_ref, sin_ref, k_ref, v_ref):
    h = _rms(c_ref[...], nw_ref[...]).astype(BF16)
    acc = jnp.dot(h, w_ref[...], preferred_element_type=F32)
    pe = _rope(pe_ref[...], cos_ref[...], sin_ref[...]).astype(BF16)
    for hd in range(MLA_HEADS):
        src = hd * (MLA_NOPE + MLA_V)
        dst = hd * MLA_QK_PAD
        k_ref[:, dst:dst + MLA_NOPE] = acc[:, src:src + MLA_NOPE].astype(BF16)
        k_ref[:, dst + MLA_NOPE:dst + MLA_QK_PAD] = pe
        v_ref[:, hd * MLA_V:(hd + 1) * MLA_V] = acc[:, src + MLA_NOPE:src + MLA_NOPE + MLA_V].astype(BF16)


def mla_keys_values(z, kv_norm, w_ukv, cos, sin, t, tm=512):
    m = z.shape[0]
    tm = min(tm, t)
    nt = t // tm
    return pl.pallas_call(
        _mla_kv_kernel,
        out_shape=[jax.ShapeDtypeStruct((m, MLA_HEADS * MLA_QK_PAD), BF16),
                   jax.ShapeDtypeStruct((m, MLA_WIDTH), BF16)],
        grid=(m // tm,),
        in_specs=[pl.BlockSpec((tm, MLA_KV_RANK), lambda i: (i, (GATE_COLS + MLA_Q_RANK) // MLA_KV_RANK)),
                  pl.BlockSpec((tm, LANES), lambda i: (i, OD_A_BLK + (MLA_Q_RANK + MLA_KV_RANK) // LANES)),
                  pl.BlockSpec((1, MLA_KV_RANK), lambda i: (0, 0)),
                  pl.BlockSpec(w_ukv.shape, lambda i: (0, 0)),
                  pl.BlockSpec((tm, LANES), lambda i: (i % nt, 0)),
                  pl.BlockSpec((tm, LANES), lambda i: (i % nt, 0))],
        out_specs=[pl.BlockSpec((tm, MLA_HEADS * MLA_QK_PAD), lambda i: (i, 0)),
                   pl.BlockSpec((tm, MLA_WIDTH), lambda i: (i, 0))],
        compiler_params=_cparams(1),
        name="mla_keys_values",
    )(z, z, kv_norm.reshape(1, -1), w_ukv, cos, sin)


def _mla_attn_kernel(q_ref, k_ref, v_ref, kc_ref, vc_ref, o_ref, *, scale):
    parts = [(kc_ref[...], None, vc_ref[...])]
    nkeys = k_ref.shape[0]
    step = min(MLA_KEY_CHUNK, nkeys)
    for lo in range(0, nkeys, step):
        parts.append((k_ref[lo:lo + step, :], None, v_ref[lo:lo + step, :]))
    o_ref[...] = _softmax_pv(q_ref[...], parts, scale)


def mla_attention(q, k, v, kc, vc, batch, t, l, tq=2048):
    tq = min(tq, t)
    nq = t // tq
    return pl.pallas_call(
        functools.partial(_mla_attn_kernel, scale=(MLA_NOPE + MLA_ROPE) ** -0.5),
        out_shape=jax.ShapeDtypeStruct((batch * t, MLA_WIDTH), F32),
        grid=(batch, MLA_HEADS, nq),
        in_specs=[pl.BlockSpec((tq, MLA_QK_PAD), lambda b, h, i: (b * nq + i, h)),
                  pl.BlockSpec((t, MLA_QK_PAD), lambda b, h, i: (b, h)),
                  pl.BlockSpec((t, MLA_V), lambda b, h, i: (b, h)),
                  pl.BlockSpec((l, MLA_QK_PAD), lambda b, h, i: (b, h)),
                  pl.BlockSpec((l, MLA_V), lambda b, h, i: (b, h))],
        out_specs=pl.BlockSpec((tq, MLA_V), lambda b, h, i: (b * nq + i, h)),
        compiler_params=_cparams(3),
        name="mla_attention",
    )(q, k, v, kc, vc)


def _rope_tables(t):
    tok = np.arange(t)
    pos = np.stack([tok // GRID_W, tok % GRID_W], axis=-1).astype(np.float32)
    n_freq = MLA_ROPE // 4
    inv = (ROPE_BASE ** (-jnp.arange(n_freq, dtype=F32) / n_freq))
    ang = (jnp.asarray(pos)[:, :, None] * inv).reshape(t, MLA_ROPE // 2)
    cos = jnp.repeat(jnp.cos(ang), 2, axis=-1)
    sin = jnp.repeat(jnp.sin(ang), 2, axis=-1) * jnp.tile(jnp.array([-1.0, 1.0], F32), MLA_ROPE // 2)
    pad = ((0, 0), (0, LANES - MLA_ROPE))
    return jnp.pad(cos, pad), jnp.pad(sin, pad)


def _split3(x):
    hi = x.astype(BF16)
    r1 = x - hi.astype(F32)
    mid = r1.astype(BF16)
    lo = (r1 - mid.astype(F32)).astype(BF16)
    return hi, mid, lo


def _hg_kernel(q_ref, f_ref, v_ref, lb_ref, s0_ref, *rest, reverse, finalize, emit_state):
    st_ref = rest[-1]
    rest = rest[:-1]
    if finalize:
        prev_ref, nw_ref = rest[0], rest[1]
        rest = rest[2:]
    o_ref = rest[0]

    @pl.when(pl.program_id(2) == 0)
    def _():
        st_ref[...] = s0_ref[...]

    bt = q_ref.shape[0]
    nchunk = bt // HG_CHUNK
    order = range(nchunk - 1, -1, -1) if reverse else range(nchunk)
    ri = lax.broadcasted_iota(jnp.int32, (bt, bt), 0)
    ci = lax.broadcasted_iota(jnp.int32, (bt, bt), 1)
    same = (ri // HG_CHUNK) == (ci // HG_CHUNK)
    tri = same & ((ci >= ri) if reverse else (ci <= ri))
    tri_f = tri.astype(F32)
    tri_b = tri.astype(BF16)
    row_chunk = lax.broadcasted_iota(jnp.int32, (bt, HG_DK), 0) // HG_CHUNK
    in_chunk = [row_chunk == c for c in range(nchunk)]

    lb = lb_ref[...]
    fz = f_ref[...]
    log_sig = jnp.minimum(fz, 0.0) - jnp.log1p(jnp.exp(-jnp.abs(fz)))
    la = jnp.log(lb)
    lbb = jnp.log1p(-lb) + log_sig
    log_f = jnp.maximum(la, lbb) + jnp.log1p(jnp.exp(-jnp.abs(la - lbb)))
    kf_all = (1.0 - lb) * _sigmoid(-fz)
    pieces = jnp.dot(tri_b, jnp.concatenate(_split3(log_f), axis=1), preferred_element_type=F32)
    width = log_f.shape[1]
    cum_all = pieces[:, :width] + pieces[:, width:2 * width] + pieces[:, 2 * width:]

    for hh in range(HG_STEP_HEADS):
        sl = slice(hh * HG_DK, (hh + 1) * HG_DK)
        cum, kf, v = cum_all[:, sl], kf_all[:, sl], v_ref[:, sl]
        last = [cum[c * HG_CHUNK:c * HG_CHUNK + 1] if reverse else cum[(c + 1) * HG_CHUNK - 1:(c + 1) * HG_CHUNK]
                for c in range(nchunk)]
        tot = jnp.concatenate([jnp.broadcast_to(x, (HG_CHUNK, HG_DK)) for x in last], axis=0)
        q_in = q_ref[:, sl] * jnp.exp(cum)
        k_in = kf * jnp.exp(-cum)
        k_end = kf * jnp.exp(tot - cum)
        o_intra = _dot(_dot_nt(q_in, k_in) * tri_f, v)
        v_by_chunk = jnp.concatenate([jnp.where(in_chunk[c], v, 0.0) for c in range(nchunk)], axis=1)
        d_st = _dot_tn(v_by_chunk, k_end)
        st = st_ref[hh]
        before = [None] * nchunk
        for c in order:
            before[c] = st
            st = st * jnp.exp(last[c]) + d_st[c * HG_DV:(c + 1) * HG_DV]
        st_ref[hh] = st
        q_by_chunk = jnp.concatenate([jnp.where(in_chunk[c], q_in, 0.0) for c in range(nchunk)], axis=1)
        o = o_intra + _dot_nt(q_by_chunk, jnp.concatenate(before, axis=1))
        if finalize:
            o = o + prev_ref[:, sl]
            o = o * lax.rsqrt(jnp.mean(o * o, axis=-1, keepdims=True) + NORM_EPS) * nw_ref[...]
        o_ref[:, sl] = o
        if emit_state:
            rest[1][hh] = st


def hgrn2_pass(z, lb, s0, batch, t, f_off, reverse, prev=None, norm_w=None, emit_state=False):
    bt = min(HG_BLOCK, t)
    nblk = t // bt
    nh = HG_STEP_HEADS
    wide = nh * LANES
    finalize = prev is not None
    assert OD_HG_BLK % nh == 0 and HG_HEADS % nh == 0, "head groups must start on a whole column block"
    pos = (lambda i: nblk - 1 - i) if reverse else (lambda i: i)
    blk = lambda off: pl.BlockSpec((bt, wide), lambda b, h, i, off=off: (b * nblk + pos(i), off // nh + h))
    state = pl.BlockSpec((None, nh, HG_DV, HG_DK), lambda b, h, i: (b, h, 0, 0))
    ins = [z, z, z, lb.reshape(1, -1), s0]
    specs = [blk(OD_HG_BLK), blk(OD_HG_BLK + f_off), blk(OD_HG_BLK + 3 * HG_HEADS),
             pl.BlockSpec((1, wide), lambda b, h, i: (0, h)), state]
    if finalize:
        ins += [prev, norm_w.reshape(1, -1)]
        specs += [blk(0), pl.BlockSpec((1, HG_DV), lambda b, h, i: (0, 0))]
    out_shape = [jax.ShapeDtypeStruct((batch * t, HG_WIDTH), F32)]
    out_specs = [blk(0)]
    if emit_state:
        out_shape.append(jax.ShapeDtypeStruct((batch, HG_HEADS, HG_DV, HG_DK), F32))
        out_specs.append(state)
    return pl.pallas_call(
        functools.partial(_hg_kernel, reverse=reverse, finalize=finalize, emit_state=emit_state),
        out_shape=out_shape,
        grid=(batch, HG_HEADS // nh, nblk),
        in_specs=specs,
        out_specs=out_specs,
        scratch_shapes=[pltpu.VMEM((nh, HG_DV, HG_DK), F32)],
        compiler_params=_cparams(3),
        name="hgrn2_" + ("bwd" if reverse else "fwd"),
    )(*ins)


def _even_layer(x, xc, mod, norm_w, w_in, w_out, rpb, rw_params, batch, t, l):
    s2 = 3 * NA_WIDTH + RW_SHIFT_COLS
    width = RW_PAD_COLS - RW_SHIFT_COLS
    n_gate, n_rest = GATE_COLS // width, s2 // width
    plan = [n_rest + i for i in range(n_gate)] + list(range(n_rest)) + [None]
    w_ext = cast_reorder(w_in, plan, width, "even_w_in")
    tiles_per_seq = max(t // 1024, 1)
    z = in_projection(x, norm_w, mod, lambda i: i // tiles_per_seq, w_ext, "even_in_proj")
    z_c = in_projection(xc, norm_w, mod, lambda i: batch, w_ext, "even_in_proj_ctx")

    y_na = na_attention(z, z_c, rpb, batch, t, l)
    yc_na = ctx_attention(z_c, batch, l)

    mu, w0, w2, a0, a2, k_k, k_a, r_k, ln_w, ln_b = rw_params
    mu = jnp.pad(mu, ((0, 0), (0, RW_PAD_COLS - RW_SHIFT_COLS)))
    zpad = jnp.zeros((RW_RANK, RW_WIDTH), F32)
    w2p = jnp.stack([jnp.concatenate([w2[0], zpad]), jnp.concatenate([zpad, w2[1]])])
    a2p = jnp.stack([jnp.concatenate([a2[0], zpad]), jnp.concatenate([zpad, a2[1]])])
    vecs = (mu, w0, w2p, a0, a2p, k_k.reshape(1, -1), k_a.reshape(1, -1), r_k.reshape(1, -1))
    terms = rwkv_terms(z, batch, t, *vecs)
    terms_c = rwkv_terms(z_c, batch, l, *vecs)
    s_zero = jnp.zeros((RW_HEAD_DIM, RW_HEAD_DIM, LANES), F32)
    yc_f, yc_b, s_ctx = rwkv_scan(*terms_c[:5], s_zero)
    y_f, y_b, _ = rwkv_scan(*terms[:5], s_ctx)
    y_rw = rwkv_readout(y_f, y_b, terms[5], ln_w, ln_b).reshape(batch * t, RW_WIDTH)
    yc_rw = rwkv_readout(yc_f, yc_b, terms_c[5], ln_w, ln_b).reshape(batch * l, RW_WIDTH)

    w_out = w_out.astype(BF16)
    rows_per_seq = max(t // 256, 1)
    x = out_projection(y_na, y_rw, z, w_out, x, mod, lambda i: i // rows_per_seq, "even_out_proj")
    xc = out_projection(yc_na, yc_rw, z_c, w_out, xc, mod, lambda i: batch, "even_out_proj_ctx")
    return x, xc


def _odd_layer(x, xc, mod, norm_w, w_in, w_out, q_norm, w_uq, kv_norm, w_ukv, lb, hg_norm_w, final_w,
               batch, t, l):
    o3 = MLA_Q_RANK + MLA_KV_RANK + MLA_ROPE
    o4 = o3 + 4 * HG_FDIM
    zero = jnp.zeros((x.shape[1], MLA_A_COLS - o3), BF16)
    w_in = w_in.astype(BF16)
    w_ext = jnp.concatenate([w_in[:, o4:], w_in[:, :o3], zero, w_in[:, o3:o4]], axis=1)
    tiles_per_seq = max(t // 1024, 1)
    z = in_projection(x, norm_w, mod, lambda i: i // tiles_per_seq, w_ext, "odd_in_proj")
    z_c = in_projection(xc, norm_w, mod, lambda i: batch, w_ext, "odd_in_proj_ctx")

    cos, sin = _rope_tables(t)
    qk = MLA_NOPE + MLA_ROPE
    w_uq_pad = jnp.pad(w_uq.reshape(MLA_Q_RANK, MLA_HEADS, qk), ((0, 0), (0, 0), (0, MLA_QK_PAD - qk)))
    w_uq_pad = w_uq_pad.reshape(MLA_Q_RANK, MLA_HEADS * MLA_QK_PAD).astype(BF16)
    w_ukv = w_ukv.astype(BF16)
    q = mla_queries(z, q_norm, w_uq_pad, cos, sin, t)
    k, v = mla_keys_values(z, kv_norm, w_ukv, cos, sin, t)
    ones = jnp.ones((l, LANES), F32)
    kc, vc = mla_keys_values(z_c, kv_norm, w_ukv, ones, jnp.zeros_like(ones), l)
    y_mla = mla_attention(q, k, v, kc, vc, batch, t, l)

    s_zero = jnp.zeros((batch, HG_HEADS, HG_DV, HG_DK), F32)
    _, s_f = hgrn2_pass(z_c, lb, s_zero, batch, l, HG_HEADS, False, emit_state=True)
    _, s_b = hgrn2_pass(z_c, lb, s_zero, batch, l, 2 * HG_HEADS, True, emit_state=True)
    o_f, = hgrn2_pass(z, lb, s_f, batch, t, HG_HEADS, False)
    y_hg, = hgrn2_pass(z, lb, s_b, batch, t, 2 * HG_HEADS, True, prev=o_f, norm_w=hg_norm_w)

    rows_per_seq = max(t // 256, 1)
    return out_projection(y_mla, y_hg, z, w_out.astype(BF16), x, mod, lambda i: i // rows_per_seq,
                          "odd_out_proj", final_w=final_w)


def kernel(x, c, ctx, c_ctx, ada_w, ada_b, norm_w, e_w_in, e_w_out, na_rpb, rw_mu, rw_w0, rw_w2, rw_a0, rw_a2, rw_k_k, rw_k_a, rw_r_k, rw_ln_w, rw_ln_b, o_w_in, o_w_out, mla_q_norm, mla_w_uq, mla_kv_norm, mla_w_ukv, hg_lower_bounds, hg_norm_w, final_norm_w):
    batch, t, d = x.shape
    l = ctx.shape[1]
    assert ada_w.shape[0] == 2, "one even and one odd layer"
    cond = jnp.concatenate([c, c_ctx[None, :], jnp.zeros((8 - batch - 1, d), F32)], axis=0)
    mod = modulation(cond, ada_w, ada_b)
    s = jax.nn.softmax(hg_lower_bounds.astype(F32), axis=0)
    lower = jnp.cumsum(s, axis=0) - s[0]

    xf, xcf = x.reshape(batch * t, d), ctx.reshape(batch * l, d)
    rw_params = (rw_mu[0], rw_w0[0], rw_w2[0], rw_a0[0], rw_a2[0], rw_k_k[0], rw_k_a[0], rw_r_k[0],
                 rw_ln_w[0], rw_ln_b[0])
    xf, xcf = _even_layer(xf, xcf, mod[0].reshape(8, 1, 3 * d), norm_w[0], e_w_in[0], e_w_out[0], na_rpb[0],
                          rw_params, batch, t, l)
    out = _odd_layer(xf, xcf, mod[1].reshape(8, 1, 3 * d), norm_w[1], o_w_in[0], o_w_out[0], mla_q_norm[0],
                     mla_w_uq[0], mla_kv_norm[0], mla_w_ukv[0], lower[1], hg_norm_w[0], final_norm_w,
                     batch, t, l)
    return out.reshape(batch, t, d)
```

```python
import functools
import math

import numpy as np
import jax
import jax.numpy as jnp
from jax import lax
from jax.experimental import pallas as pl
from jax.experimental.pallas import tpu as pltpu

F32 = jnp.float32
BF16 = jnp.bfloat16

GRID_W = 64
NORM_EPS = 1e-6
ROPE_BASE = 10000.0

NA_HEADS = 8
NA_HEAD_DIM = 128
NA_WIDTH = NA_HEADS * NA_HEAD_DIM
NA_WIN_ROWS = 8
NA_WIN_COLS = 16
NA_QROWS = 8
NA_KROWS = 16
NA_STEP_HEADS = 4

RW_HEAD_DIM = 64
RW_HEADS = 16
RW_WIDTH = RW_HEADS * RW_HEAD_DIM
RW_RANK = 64
RW_SHIFT_COLS = 3 * RW_WIDTH + 4 * RW_RANK
RW_PAD_COLS = 3584
RW_GN_EPS = 64e-5
RW_SCAN_CHUNK = 64
RW_DECAY_SCALE = math.exp(-0.5)

MLA_HEADS = 8
MLA_Q_RANK = 512
MLA_KV_RANK = 512
MLA_NOPE = 128
MLA_ROPE = 64
MLA_V = 128
MLA_QK_PAD = 256
MLA_WIDTH = MLA_HEADS * MLA_V
MLA_A_COLS = 1536
MLA_KEY_CHUNK = 1024

HG_HEADS = 8
HG_DK = 128
HG_DV = 128
HG_FDIM = HG_HEADS * HG_DK
HG_WIDTH = HG_HEADS * HG_DV
HG_CHUNK = 32
HG_BLOCK = 256
HG_STEP_HEADS = 4

LANES = 128

GATE_COLS = 2048
EV_Q_BLK = GATE_COLS // LANES
EV_RW_BLK = EV_Q_BLK + 3 * NA_WIDTH // LANES
OD_A_BLK = GATE_COLS // LANES
OD_HG_BLK = OD_A_BLK + MLA_A_COLS // LANES
VMEM_LIMIT = 48 * 1024 * 1024
NEG_INF = -1e30
LOG2E = math.log2(math.e)


def _cparams(n_axes):
    return pltpu.CompilerParams(dimension_semantics=("arbitrary",) * n_axes, vmem_limit_bytes=VMEM_LIMIT)


def _sigmoid(x):
    return 1.0 / (1.0 + jnp.exp(-x))


def _silu(x):
    return x * _sigmoid(x)


def _dot(a, b):
    return jnp.dot(a.astype(BF16), b.astype(BF16), preferred_element_type=F32)


def _dot_nt(a, b):
    return lax.dot_general(a.astype(BF16), b.astype(BF16), (((1,), (1,)), ((), ())), preferred_element_type=F32)


def _dot_tn(a, b):
    return lax.dot_general(a.astype(BF16), b.astype(BF16), (((0,), (0,)), ((), ())), preferred_element_type=F32)


def _mod_kernel(c_ref, w_ref, b_ref, o_ref):
    o_ref[...] = _dot(_silu(c_ref[...]), w_ref[...]) + b_ref[...]


def modulation(cond, ada_w, ada_b, tn=512):
    depth, d, n = ada_w.shape
    return pl.pallas_call(
        _mod_kernel,
        out_shape=jax.ShapeDtypeStruct((depth, 8, n), F32),
        grid=(depth, n // tn),
        in_specs=[pl.BlockSpec((8, d), lambda l, j: (0, 0)),
                  pl.BlockSpec((None, d, tn), lambda l, j: (l, 0, j)),
                  pl.BlockSpec((None, 1, tn), lambda l, j: (l, 0, j))],
        out_specs=pl.BlockSpec((None, 8, tn), lambda l, j: (l, 0, j)),
        compiler_params=_cparams(2),
        name="adaln_modulation",
    )(cond, ada_w, ada_b.reshape(depth, 1, n))


def _cast_reorder_kernel(src_ref, zero_ref, x_ref, o_ref):
    del src_ref
    o_ref[...] = jnp.where(zero_ref[pl.program_id(0)] != 0, 0.0, x_ref[...]).astype(BF16)


def cast_reorder(w, plan, width, name):
    d = w.shape[0]
    src = np.array([0 if a is None else a for a in plan], np.int32)
    zero = np.array([a is None for a in plan], np.int32)
    return pl.pallas_call(
        _cast_reorder_kernel,
        out_shape=jax.ShapeDtypeStruct((d, width * len(plan)), BF16),
        grid_spec=pltpu.PrefetchScalarGridSpec(
            num_scalar_prefetch=2,
            grid=(len(plan),),
            in_specs=[pl.BlockSpec((d, width), lambda j, src, zero: (0, src[j]))],
            out_specs=pl.BlockSpec((d, width), lambda j, src, zero: (0, j))),
        compiler_params=_cparams(1),
        name=name,
    )(jnp.asarray(src), jnp.asarray(zero), w)


def _inproj_kernel(x_ref, nw_ref, shift_ref, scale_ref, w_ref, o_ref, h_ref):
    @pl.when(pl.program_id(1) == 0)
    def _():
        x = x_ref[...]
        y = x * lax.rsqrt(jnp.mean(x * x, axis=-1, keepdims=True) + NORM_EPS) * nw_ref[...]
        h_ref[...] = (y * (1.0 + scale_ref[...]) + shift_ref[...]).astype(BF16)

    o_ref[...] = jnp.dot(h_ref[...], w_ref[...], preferred_element_type=F32)


def in_projection(x, norm_w, mod, mod_row, w, name, tm=1024, tn=512):
    m, d = x.shape
    n = w.shape[1]
    tm = min(tm, m)
    return pl.pallas_call(
        _inproj_kernel,
        out_shape=jax.ShapeDtypeStruct((m, n), F32),
        grid=(m // tm, n // tn),
        in_specs=[pl.BlockSpec((tm, d), lambda i, j: (i, 0)),
                  pl.BlockSpec((1, d), lambda i, j: (0, 0)),
                  pl.BlockSpec((None, 1, d), lambda i, j: (mod_row(i), 0, 0)),
                  pl.BlockSpec((None, 1, d), lambda i, j: (mod_row(i), 0, 1)),
                  pl.BlockSpec((d, tn), lambda i, j: (0, j))],
        out_specs=pl.BlockSpec((tm, tn), lambda i, j: (i, j)),
        scratch_shapes=[pltpu.VMEM((tm, d), BF16)],
        compiler_params=_cparams(2),
        name=name,
    )(x, norm_w.reshape(1, d), mod, mod, w)


def _outproj_kernel(ya_ref, yb_ref, g_ref, w_ref, x_ref, gm_ref, *rest, ka, final):
    o_ref = rest[-1]
    sg = _silu(g_ref[...])
    acc = _dot(ya_ref[...].astype(F32) * sg[:, :ka], w_ref[:ka, :])
    acc += _dot(yb_ref[...].astype(F32) * sg[:, ka:], w_ref[ka:, :])
    out = x_ref[...] + gm_ref[...] * acc
    if final:
        out = out * lax.rsqrt(jnp.mean(out * out, axis=-1, keepdims=True) + NORM_EPS) * rest[0][...]
    o_ref[...] = out


def out_projection(ya, yb, gate, w, x, mod, mod_row, name, final_w=None, tm=256):
    m, d = x.shape
    ka, kb = ya.shape[1], yb.shape[1]
    tm = min(tm, m)
    ins = [ya, yb, gate, w, x, mod]
    specs = [pl.BlockSpec((tm, ka), lambda i: (i, 0)),
             pl.BlockSpec((tm, kb), lambda i: (i, 0)),
             pl.BlockSpec((tm, ka + kb), lambda i: (i, 0)),
             pl.BlockSpec((ka + kb, d), lambda i: (0, 0)),
             pl.BlockSpec((tm, d), lambda i: (i, 0)),
             pl.BlockSpec((None, 1, d), lambda i: (mod_row(i), 0, 2))]
    if final_w is not None:
        ins.append(final_w.reshape(1, d))
        specs.append(pl.BlockSpec((1, d), lambda i: (0, 0)))
    return pl.pallas_call(
        functools.partial(_outproj_kernel, ka=ka, final=final_w is not None),
        out_shape=jax.ShapeDtypeStruct((m, d), F32),
        grid=(m // tm,),
        in_specs=specs,
        out_specs=pl.BlockSpec((tm, d), lambda i: (i, 0)),
        compiler_params=_cparams(1),
        name=name,
    )(*ins)


def _softmax_pv(q, parts, scale):
    c = scale * LOG2E
    m = denom = num = None
    for k, bias, v in parts:
        x = _dot_nt(q, k) * c
        if bias is not None:
            x = x + bias
        m_part = jnp.max(x, axis=-1, keepdims=True)
        if m is None:
            m = m_part
            p = jnp.exp2(x - m)
            denom = jnp.sum(p, axis=-1, keepdims=True)
            num = _dot(p, v)
        else:
            m_new = jnp.maximum(m, m_part)
            alpha = jnp.exp2(m - m_new)
            p = jnp.exp2(x - m_new)
            denom = denom * alpha + jnp.sum(p, axis=-1, keepdims=True)
            num = num * alpha + _dot(p, v)
            m = m_new
    return num / denom


def _softmax_pv_joint(q, parts, scale):
    c = scale * LOG2E
    logits = [_dot_nt(q, k) * c if bias is None else _dot_nt(q, k) * c + bias for k, bias, _ in parts]
    m = functools.reduce(jnp.maximum, [jnp.max(x, axis=-1, keepdims=True) for x in logits])
    ps = [jnp.exp2(x - m) for x in logits]
    denom = functools.reduce(lambda a, b: a + b, [jnp.sum(p, axis=-1, keepdims=True) for p in ps])
    num = functools.reduce(lambda a, b: a + b, [_dot(p, v) for p, (_, _, v) in zip(ps, parts)])
    return num / denom


def _na_kernel(idx_ref, q_ref, k_ref, v_ref, kc_ref, vc_ref, tab_ref, o_ref, bias_ref, *, rows, scale):
    j = pl.program_id(2)
    npair = NA_KROWS // 2
    start = pl.multiple_of(jnp.clip(NA_QROWS * j - NA_WIN_ROWS // 2, 0, rows - NA_KROWS) * GRID_W, GRID_W)
    win = pl.ds(start, NA_KROWS * GRID_W)
    for hd in range(NA_STEP_HEADS):
        sl = slice(hd * LANES, (hd + 1) * LANES)
        for qr in range(NA_QROWS):
            for m in range(npair):
                tile = tab_ref[hd, idx_ref[(j * NA_QROWS + qr) * npair + m]]
                bias_ref[hd, qr * GRID_W:(qr + 1) * GRID_W, m * LANES:(m + 1) * LANES] = tile
        parts = [(k_ref[win, sl], bias_ref[hd], v_ref[win, sl]), (kc_ref[:, sl], None, vc_ref[:, sl])]
        o_ref[:, sl] = _softmax_pv_joint(q_ref[:, sl], parts, scale)


def _na_tile_ids(rows):
    kh = min(NA_WIN_ROWS, rows)
    n = 2 * NA_WIN_ROWS - 1
    ids = []
    for j in range(rows // NA_QROWS):
        ks = int(np.clip(NA_QROWS * j - NA_WIN_ROWS // 2, 0, rows - NA_KROWS))
        for qr in range(NA_QROWS):
            r = NA_QROWS * j + qr
            r0 = int(np.clip(r - kh // 2, 0, rows - kh))
            for m in range(NA_KROWS // 2):
                kr = ks + 2 * m
                dy = kr - r + NA_WIN_ROWS - 1
                first, second = r0 <= kr < r0 + kh, r0 <= kr + 1 < r0 + kh
                ids.append(dy if first and second else n + dy if first else 2 * n + dy + 1 if second else 3 * n)
    return np.array(ids, np.int32)


def _na_bias_tiles(rpb):
    col = np.arange(GRID_W)
    cs = np.clip(col - NA_WIN_COLS // 2, 0, GRID_W - NA_WIN_COLS)
    kc = col[None, :]
    in_win = (kc >= cs[:, None]) & (kc < cs[:, None] + NA_WIN_COLS)
    dx = np.where(in_win, kc - col[:, None] + NA_WIN_COLS - 1, 0)
    pick = jnp.asarray(dx[:, :, None] == np.arange(rpb.shape[-1]), F32)
    picked = jnp.einsum("hyj,cqj->hycq", rpb, pick, precision=lax.Precision.HIGHEST)
    tab = jnp.where(in_win[None, None], picked * LOG2E, NEG_INF)
    neg = jnp.full_like(tab[:, :1], NEG_INF)
    nxt = jnp.concatenate([tab[:, 1:], neg], axis=1)
    both = jnp.concatenate([tab, nxt], axis=-1)
    first = jnp.concatenate([tab, jnp.broadcast_to(neg, tab.shape)], axis=-1)
    second = jnp.concatenate([jnp.broadcast_to(neg, tab.shape), tab], axis=-1)
    return jnp.concatenate([both, first, second, jnp.concatenate([neg, neg], axis=-1)], axis=1)


def na_attention(z, z_c, rpb, batch, t, l):
    rows = t // GRID_W
    nj = rows // NA_QROWS
    tq = NA_QROWS * GRID_W
    ids = jnp.asarray(_na_tile_ids(rows))
    tiles = _na_bias_tiles(rpb)
    nh = NA_STEP_HEADS
    wide = nh * LANES
    q0, k0, v0 = (EV_Q_BLK + i * NA_HEADS for i in range(3))
    assert q0 % nh == 0 and NA_HEADS % nh == 0, "head groups must start on a whole column block"
    return pl.pallas_call(
        functools.partial(_na_kernel, rows=rows, scale=NA_HEAD_DIM ** -0.5),
        out_shape=jax.ShapeDtypeStruct((batch * t, NA_WIDTH), F32),
        grid_spec=pltpu.PrefetchScalarGridSpec(
            num_scalar_prefetch=1,
            grid=(NA_HEADS // nh, batch, nj),
            in_specs=[pl.BlockSpec((tq, wide), lambda h, b, j, g: (b * nj + j, q0 // nh + h)),
                      pl.BlockSpec((t, wide), lambda h, b, j, g: (b, k0 // nh + h)),
                      pl.BlockSpec((t, wide), lambda h, b, j, g: (b, v0 // nh + h)),
                      pl.BlockSpec((l, wide), lambda h, b, j, g: (b, k0 // nh + h)),
                      pl.BlockSpec((l, wide), lambda h, b, j, g: (b, v0 // nh + h)),
                      pl.BlockSpec((nh,) + tiles.shape[1:], lambda h, b, j, g: (h, 0, 0, 0))],
            out_specs=pl.BlockSpec((tq, wide), lambda h, b, j, g: (b * nj + j, h)),
            scratch_shapes=[pltpu.VMEM((nh, tq, NA_KROWS * GRID_W), F32)]),
        compiler_params=_cparams(3),
        name="na_attention",
    )(ids, z, z, z, z_c, z_c, tiles)


def _dense_attn_kernel(q_ref, k_ref, v_ref, o_ref, *, scale):
    for hd in range(NA_STEP_HEADS):
        sl = slice(hd * LANES, (hd + 1) * LANES)
        o_ref[:, sl] = _softmax_pv(q_ref[:, sl], [(k_ref[:, sl], None, v_ref[:, sl])], scale)


def ctx_attention(z_c, batch, l):
    nh = NA_STEP_HEADS
    wide = nh * LANES
    q0, k0, v0 = (EV_Q_BLK + i * NA_HEADS for i in range(3))
    assert q0 % nh == 0 and NA_HEADS % nh == 0, "head groups must start on a whole column block"
    return pl.pallas_call(
        functools.partial(_dense_attn_kernel, scale=NA_HEAD_DIM ** -0.5),
        out_shape=jax.ShapeDtypeStruct((batch * l, NA_WIDTH), F32),
        grid=(batch, NA_HEADS // nh),
        in_specs=[pl.BlockSpec((l, wide), lambda b, h: (b, q0 // nh + h)),
                  pl.BlockSpec((l, wide), lambda b, h: (b, k0 // nh + h)),
                  pl.BlockSpec((l, wide), lambda b, h: (b, v0 // nh + h))],
        out_specs=pl.BlockSpec((l, wide), lambda b, h: (b, h)),
        compiler_params=_cparams(2),
        name="ctx_attention",
    )(z_c, z_c, z_c)


def _head_sum(x):
    first = lax.broadcasted_iota(jnp.int32, x.shape, 1) < RW_HEAD_DIM
    s0 = jnp.sum(jnp.where(first, x, 0.0), axis=-1, keepdims=True)
    s1 = jnp.sum(jnp.where(first, 0.0, x), axis=-1, keepdims=True)
    return jnp.where(first, s0, s1)


def _rw_terms_kernel(ur_ref, uk_ref, uv_ref, uwa_ref, pr_ref, pk_ref, pv_ref, pwa_ref, nr_ref, nk_ref, nv_ref,
                     nwa_ref, mu_ref, w0_ref, w2_ref, a0_ref, a2_ref, kk_ref, ka_ref, rk_ref,
                     rv_o, wk0_o, wk1_o, bk0_o, bk1_o, bonus_o, *, nt):
    i = pl.program_id(1)
    tm = ur_ref.shape[0]

    def shifted(u_ref, p_ref, n_ref, col0):
        u = u_ref[...]
        mu = mu_ref[:, col0:col0 + u.shape[1]]
        row = lax.broadcasted_iota(jnp.int32, u.shape, 0)
        before = jnp.where(i == 0, 0.0, p_ref[7:8, :])
        after = jnp.where(i == nt - 1, 0.0, n_ref[0:1, :])
        prev = jnp.where(row == 0, before, pltpu.roll(u, 1, axis=0))
        nxt = jnp.where(row == tm - 1, after, pltpu.roll(u, tm - 1, axis=0))
        return u + mu[0:1] * (prev - u) + mu[1:2] * (nxt - u)

    r_all = shifted(ur_ref, pr_ref, nr_ref, 0)
    k_all = shifted(uk_ref, pk_ref, nk_ref, RW_WIDTH)
    v_all = shifted(uv_ref, pv_ref, nv_ref, 2 * RW_WIDTH)
    wa = shifted(uwa_ref, pwa_ref, nwa_ref, 3 * RW_WIDTH)
    wd = jnp.tanh(wa[:, :LANES]).astype(BF16)
    ad = wa[:, LANES:].astype(BF16)
    first = lax.broadcasted_iota(jnp.int32, (tm, LANES), 1) < RW_HEAD_DIM

    def head_rows(x, y):
        return (jnp.where(first, x, pltpu.roll(y, RW_HEAD_DIM, axis=1)),
                jnp.where(first, pltpu.roll(x, RW_HEAD_DIM, axis=1), y))

    outs = (rv_o, wk0_o, wk1_o, bk0_o, bk1_o)
    even = [[] for _ in outs]
    odd = [[] for _ in outs]
    bonus = []
    for p in range(RW_WIDTH // LANES):
        sl = slice(p * LANES, (p + 1) * LANES)
        r, k, v = r_all[:, sl], k_all[:, sl], v_all[:, sl]
        kk = k * kk_ref[:, sl]
        kk = kk / jnp.maximum(jnp.sqrt(_head_sum(kk * kk)), 1e-12)
        kd_sum = jnp.zeros_like(k)
        per_dir = []
        for d in range(2):
            w = w0_ref[d:d + 1, sl] + jnp.dot(wd, w2_ref[d, :, sl].astype(BF16), preferred_element_type=F32)
            a = _sigmoid(a0_ref[d:d + 1, sl] + jnp.dot(ad, a2_ref[d, :, sl].astype(BF16), preferred_element_type=F32))
            kd = k * (1.0 + (a - 1.0) * ka_ref[:, sl])
            dec = jnp.exp(-RW_DECAY_SCALE * _sigmoid(w))
            per_dir.append(((dec, kd), (kk * a, kk)))
            kd_sum = kd_sum + kd
        pairs = [(r, v), per_dir[0][0], per_dir[1][0], per_dir[0][1], per_dir[1][1]]
        bonus.append(_head_sum(r * kd_sum * rk_ref[:, sl]) * v)
        for ev, od, (x, y) in zip(even, odd, pairs):
            e, o = head_rows(x, y)
            ev.append(e)
            od.append(o)
    for o_ref, ev, od in zip(outs, even, odd):
        o_ref[...] = jnp.swapaxes(jnp.stack(ev + od, axis=0), 0, 1)
    bonus_o[...] = jnp.swapaxes(jnp.stack(bonus, axis=0), 0, 1)


def rwkv_terms(z, n_seq, t, mu, w0, w2p, a0, a2p, k_k, k_a, r_k, tm=256):
    tm = min(tm, t)
    nt = t // tm
    npair = RW_WIDTH // LANES
    wide0 = EV_RW_BLK * LANES // RW_WIDTH
    wa_blk = (EV_RW_BLK + 3 * npair) // 2
    n8 = tm // 8

    def cur(width, cb):
        return pl.BlockSpec((tm, width), lambda b, i: (b * nt + i, cb))

    def before(width, cb):
        return pl.BlockSpec((8, width), lambda b, i: (jnp.maximum((b * nt + i) * n8 - 1, 0), cb))

    def after(width, cb):
        return pl.BlockSpec((8, width), lambda b, i: (jnp.minimum((b * nt + i + 1) * n8, n_seq * nt * n8 - 1), cb))

    cols = [(RW_WIDTH, wide0), (RW_WIDTH, wide0 + 1), (RW_WIDTH, wide0 + 2), (2 * LANES, wa_blk)]
    whole = lambda a: pl.BlockSpec(a.shape, lambda b, i: (0,) * a.ndim)
    params = (mu, w0, w2p, a0, a2p, k_k, k_a, r_k)
    out = pl.BlockSpec((tm, npair, LANES), lambda b, i: (i, b, 0))
    out2 = pl.BlockSpec((tm, 2 * npair, LANES), lambda b, i: (i, b, 0))
    return pl.pallas_call(
        functools.partial(_rw_terms_kernel, nt=nt),
        out_shape=[jax.ShapeDtypeStruct((t, n_seq * 2 * npair, LANES), F32)] * 5
        + [jax.ShapeDtypeStruct((t, n_seq * npair, LANES), F32)],
        grid=(n_seq, nt),
        in_specs=([cur(*c) for c in cols] + [before(*c) for c in cols] + [after(*c) for c in cols]
                  + [whole(a) for a in params]),
        out_specs=[out2] * 5 + [out],
        compiler_params=_cparams(2),
        name="rwkv_terms",
    )(*([z] * 12), *params)


def _to_lanes(n):
    q = jnp.concatenate([n, pltpu.roll(n, RW_HEAD_DIM, axis=1)], axis=0)
    return q.T[:RW_HEAD_DIM]


def _rw_scan_kernel(rvf, rvb, wk0, wk1, bk0, bk1, s0_ref, yf_ref, yb_ref, sfin_ref,
                    s_ref, z_ref, znext_ref, g_ref, ybuf_ref, *, chunk):
    n = RW_HEAD_DIM
    nb = rvf.shape[1] // (2 * RW_WIDTH // LANES)
    R, K, V, A, B = range(5)

    @pl.when(pl.program_id(0) == 0)
    def _():
        s_ref[...] = s0_ref[...]

    def convert(t_src):
        def pair_tiles(f, bw):
            both = (f[t_src], bw[chunk - 1 - t_src])
            x = jnp.concatenate([src[(b * 2 + par) * 8:(b * 2 + par) * 8 + 8]
                                 for par in range(2) for src in both for b in range(nb)], axis=0)
            xt = x.T
            return xt[:n], xt[n:]

        r, v = pair_tiles(rvf, rvb)
        w, k = pair_tiles(wk0, wk1)
        b, kk = pair_tiles(bk0, bk1)
        g_prev = g_ref[0]
        g = g_prev * w
        g_ref[0] = g
        inv = 1.0 / g
        znext_ref[R] = r * g
        znext_ref[K] = k * inv
        znext_ref[V] = v
        znext_ref[A] = -(kk * g_prev)
        znext_ref[B] = b * inv

    def emit_y(t):
        rows = _to_lanes(ybuf_ref[...])
        yf_ref[t] = rows[:n // 2]
        yb_ref[chunk - 1 - t] = rows[n // 2:]

    g_ref[0] = jnp.ones((n, LANES), F32)
    ybuf_ref[...] = jnp.zeros((n, LANES), F32)
    convert(0)

    def step(t, carry):
        z_ref[...] = znext_ref[...]
        g_ref[1] = g_ref[0]
        emit_y(jnp.maximum(t - 1, 0))
        sa = jnp.zeros((n, LANES), F32)
        for k in range(n):
            sa = sa + s_ref[k] * z_ref[A, k:k + 1, :]
        convert(jnp.minimum(t + 1, chunk - 1))
        vt = z_ref[V]
        y = jnp.zeros((n, LANES), F32)
        for k in range(n):
            sk = s_ref[k] + sa * z_ref[B, k:k + 1, :] + vt * z_ref[K, k:k + 1, :]
            s_ref[k] = sk
            y = y + sk * z_ref[R, k:k + 1, :]
        ybuf_ref[...] = y
        return carry

    lax.fori_loop(0, chunk, step, 0)
    emit_y(chunk - 1)
    for k in range(n):
        s_ref[k] = s_ref[k] * g_ref[1, k:k + 1, :]
    sfin_ref[...] = s_ref[...]


def rwkv_scan(rv, wk0, wk1, bk0, bk1, s0):
    t, nrow, _ = rv.shape
    n = RW_HEAD_DIM
    chunk = RW_SCAN_CHUNK
    nc = t // chunk
    assert nrow == LANES // 2, "the states of both directions fill the 128 lanes"
    fwd = pl.BlockSpec((chunk, nrow, LANES), lambda i: (i, 0, 0))
    bwd = pl.BlockSpec((chunk, nrow, LANES), lambda i: (nc - 1 - i, 0, 0))
    yfwd = pl.BlockSpec((chunk, nrow // 2, LANES), lambda i: (i, 0, 0))
    ybwd = pl.BlockSpec((chunk, nrow // 2, LANES), lambda i: (nc - 1 - i, 0, 0))
    state = pl.BlockSpec((n, n, LANES), lambda i: (0, 0, 0))
    slab = jax.ShapeDtypeStruct((t, nrow // 2, LANES), F32)
    return pl.pallas_call(
        functools.partial(_rw_scan_kernel, chunk=chunk),
        out_shape=[slab, slab, jax.ShapeDtypeStruct((n, n, LANES), F32)],
        grid=(nc,),
        in_specs=[fwd, bwd, fwd, bwd, fwd, bwd, state],
        out_specs=[yfwd, ybwd, state],
        scratch_shapes=[pltpu.VMEM((n, n, LANES), F32), pltpu.VMEM((5, n, LANES), F32),
                        pltpu.VMEM((5, n, LANES), F32), pltpu.VMEM((2, n, LANES), F32),
                        pltpu.VMEM((n, LANES), F32)],
        compiler_params=_cparams(1),
        name="rwkv_scan",
    )(rv, rv, wk0, wk1, bk0, bk1, s0)


def _rw_readout_kernel(yf_ref, yb_ref, bonus_ref, lw_ref, lb_ref, o_ref):
    npair = RW_WIDTH // LANES
    y_all = jnp.swapaxes(yf_ref[...] + yb_ref[...], 0, 1)
    bonus_all = jnp.swapaxes(bonus_ref[...], 0, 1)
    for q in range(yf_ref.shape[1]):
        b, p = divmod(q, npair)
        sl = slice(p * LANES, (p + 1) * LANES)
        y = y_all[q]
        mu = _head_sum(y) * (1.0 / RW_HEAD_DIM)
        d = y - mu
        var = _head_sum(d * d) * (1.0 / RW_HEAD_DIM)
        o_ref[b, :, sl] = d * lax.rsqrt(var + RW_GN_EPS) * lw_ref[:, sl] + lb_ref[:, sl] + bonus_all[q]


def rwkv_readout(y_f, y_b, bonus, ln_w, ln_b, tm=256):
    t, nq, _ = y_f.shape
    tm = min(tm, t)
    batch = nq // (RW_WIDTH // LANES)
    slab = pl.BlockSpec((tm, nq, LANES), lambda i: (i, 0, 0))
    vec = pl.BlockSpec((1, RW_WIDTH), lambda i: (0, 0))
    return pl.pallas_call(
        _rw_readout_kernel,
        out_shape=jax.ShapeDtypeStruct((batch, t, RW_WIDTH), F32),
        grid=(t // tm,),
        in_specs=[slab, slab, slab, vec, vec],
        out_specs=pl.BlockSpec((batch, tm, RW_WIDTH), lambda i: (0, i, 0)),
        compiler_params=_cparams(1),
        name="rwkv_readout",
    )(y_f, y_b, bonus, ln_w.reshape(1, -1), ln_b.reshape(1, -1))


def _rms(x, w):
    return x * lax.rsqrt(jnp.mean(x * x, axis=-1, keepdims=True) + NORM_EPS) * w


def _rope(x, cos, sin):
    odd = (lax.broadcasted_iota(jnp.int32, x.shape, 1) & 1) == 1
    swapped = jnp.where(odd, pltpu.roll(x, 1, axis=1), pltpu.roll(x, LANES - 1, axis=1))
    return x * cos + swapped * sin


def _mla_q_kernel(c_ref, nw_ref, w_ref, cos_ref, sin_ref, q_ref):
    h = _rms(c_ref[...], nw_ref[...]).astype(BF16)
    acc = jnp.dot(h, w_ref[...], preferred_element_type=F32)
    cos, sin = cos_ref[...], sin_ref[...]
    for hd in range(MLA_HEADS):
        lo = hd * MLA_QK_PAD
        q_ref[:, lo:lo + MLA_NOPE] = acc[:, lo:lo + MLA_NOPE].astype(BF16)
        q_ref[:, lo + MLA_NOPE:lo + MLA_QK_PAD] = _rope(acc[:, lo + MLA_NOPE:lo + MLA_QK_PAD], cos, sin).astype(BF16)


def mla_queries(z, q_norm, w_uq_pad, cos, sin, t, tm=512):
    m = z.shape[0]
    tm = min(tm, t)
    nt = t // tm
    width = MLA_HEADS * MLA_QK_PAD
    return pl.pallas_call(
        _mla_q_kernel,
        out_shape=jax.ShapeDtypeStruct((m, width), BF16),
        grid=(m // tm,),
        in_specs=[pl.BlockSpec((tm, MLA_Q_RANK), lambda i: (i, GATE_COLS // MLA_Q_RANK)),
                  pl.BlockSpec((1, MLA_Q_RANK), lambda i: (0, 0)),
                  pl.BlockSpec((MLA_Q_RANK, width), lambda i: (0, 0)),
                  pl.BlockSpec((tm, LANES), lambda i: (i % nt, 0)),
                  pl.BlockSpec((tm, LANES), lambda i: (i % nt, 0))],
        out_specs=pl.BlockSpec((tm, width), lambda i: (i, 0)),
        compiler_params=_cparams(1),
        name="mla_queries",
    )(z, q_norm.reshape(1, -1), w_uq_pad, cos, sin)


def _mla_kv_kernel(c_ref, pe_ref, nw_ref, w_ref, cos_ref, sin_ref, k_ref, v_ref):
    h = _rms(c_ref[...], nw_ref[...]).astype(BF16)
    acc = jnp.dot(h, w_ref[...], preferred_element_type=F32)
    pe = _rope(pe_ref[...], cos_ref[...], sin_ref[...]).astype(BF16)
    for hd in range(MLA_HEADS):
        src = hd * (MLA_NOPE + MLA_V)
        dst = hd * MLA_QK_PAD
        k_ref[:, dst:dst + MLA_NOPE] = acc[:, src:src + MLA_NOPE].astype(BF16)
        k_ref[:, dst + MLA_NOPE:dst + MLA_QK_PAD] = pe
        v_ref[:, hd * MLA_V:(hd + 1) * MLA_V] = acc[:, src + MLA_NOPE:src + MLA_NOPE + MLA_V].astype(BF16)


def mla_keys_values(z, kv_norm, w_ukv, cos, sin, t, tm=512):
    m = z.shape[0]
    tm = min(tm, t)
    nt = t // tm
    return pl.pallas_call(
        _mla_kv_kernel,
        out_shape=[jax.ShapeDtypeStruct((m, MLA_HEADS * MLA_QK_PAD), BF16),
                   jax.ShapeDtypeStruct((m, MLA_WIDTH), BF16)],
        grid=(m // tm,),
        in_specs=[pl.BlockSpec((tm, MLA_KV_RANK), lambda i: (i, (GATE_COLS + MLA_Q_RANK) // MLA_KV_RANK)),
                  pl.BlockSpec((tm, LANES), lambda i: (i, OD_A_BLK + (MLA_Q_RANK + MLA_KV_RANK) // LANES)),
                  pl.BlockSpec((1, MLA_KV_RANK), lambda i: (0, 0)),
                  pl.BlockSpec(w_ukv.shape, lambda i: (0, 0)),
                  pl.BlockSpec((tm, LANES), lambda i: (i % nt, 0)),
                  pl.BlockSpec((tm, LANES), lambda i: (i % nt, 0))],
        out_specs=[pl.BlockSpec((tm, MLA_HEADS * MLA_QK_PAD), lambda i: (i, 0)),
                   pl.BlockSpec((tm, MLA_WIDTH), lambda i: (i, 0))],
        compiler_params=_cparams(1),
        name="mla_keys_values",
    )(z, z, kv_norm.reshape(1, -1), w_ukv, cos, sin)


def _mla_attn_kernel(q_ref, k_ref, v_ref, kc_ref, vc_ref, o_ref, *, scale):
    parts = [(kc_ref[...], None, vc_ref[...])]
    nkeys = k_ref.shape[0]
    step = min(MLA_KEY_CHUNK, nkeys)
    for lo in range(0, nkeys, step):
        parts.append((k_ref[lo:lo + step, :], None, v_ref[lo:lo + step, :]))
    o_ref[...] = _softmax_pv(q_ref[...], parts, scale)


def mla_attention(q, k, v, kc, vc, batch, t, l, tq=2048):
    tq = min(tq, t)
    nq = t // tq
    return pl.pallas_call(
        functools.partial(_mla_attn_kernel, scale=(MLA_NOPE + MLA_ROPE) ** -0.5),
        out_shape=jax.ShapeDtypeStruct((batch * t, MLA_WIDTH), F32),
        grid=(batch, MLA_HEADS, nq),
        in_specs=[pl.BlockSpec((tq, MLA_QK_PAD), lambda b, h, i: (b * nq + i, h)),
                  pl.BlockSpec((t, MLA_QK_PAD), lambda b, h, i: (b, h)),
                  pl.BlockSpec((t, MLA_V), lambda b, h, i: (b, h)),
                  pl.BlockSpec((l, MLA_QK_PAD), lambda b, h, i: (b, h)),
                  pl.BlockSpec((l, MLA_V), lambda b, h, i: (b, h))],
        out_specs=pl.BlockSpec((tq, MLA_V), lambda b, h, i: (b * nq + i, h)),
        compiler_params=_cparams(3),
        name="mla_attention",
    )(q, k, v, kc, vc)


def _rope_tables(t):
    tok = np.arange(t)
    pos = np.stack([tok // GRID_W, tok % GRID_W], axis=-1).astype(np.float32)
    n_freq = MLA_ROPE // 4
    inv = (ROPE_BASE ** (-jnp.arange(n_freq, dtype=F32) / n_freq))
    ang = (jnp.asarray(pos)[:, :, None] * inv).reshape(t, MLA_ROPE // 2)
    cos = jnp.repeat(jnp.cos(ang), 2, axis=-1)
    sin = jnp.repeat(jnp.sin(ang), 2, axis=-1) * jnp.tile(jnp.array([-1.0, 1.0], F32), MLA_ROPE // 2)
    pad = ((0, 0), (0, LANES - MLA_ROPE))
    return jnp.pad(cos, pad), jnp.pad(sin, pad)


def _split3(x):
    hi = x.astype(BF16)
    r1 = x - hi.astype(F32)
    mid = r1.astype(BF16)
    lo = (r1 - mid.astype(F32)).astype(BF16)
    return hi, mid, lo


def _hg_kernel(q_ref, f_ref, v_ref, lb_ref, s0_ref, *rest, reverse, finalize, emit_state):
    st_ref = rest[-1]
    rest = rest[:-1]
    if finalize:
        prev_ref, nw_ref = rest[0], rest[1]
        rest = rest[2:]
    o_ref = rest[0]

    @pl.when(pl.program_id(2) == 0)
    def _():
        st_ref[...] = s0_ref[...]

    bt = q_ref.shape[0]
    nchunk = bt // HG_CHUNK
    order = range(nchunk - 1, -1, -1) if reverse else range(nchunk)
    ri = lax.broadcasted_iota(jnp.int32, (bt, bt), 0)
    ci = lax.broadcasted_iota(jnp.int32, (bt, bt), 1)
    same = (ri // HG_CHUNK) == (ci // HG_CHUNK)
    tri = same & ((ci >= ri) if reverse else (ci <= ri))
    tri_f = tri.astype(F32)
    tri_b = tri.astype(BF16)
    row_chunk = lax.broadcasted_iota(jnp.int32, (bt, HG_DK), 0) // HG_CHUNK
    in_chunk = [row_chunk == c for c in range(nchunk)]

    lb = lb_ref[...]
    fz = f_ref[...]
    log_sig = jnp.minimum(fz, 0.0) - jnp.log1p(jnp.exp(-jnp.abs(fz)))
    la = jnp.log(lb)
    lbb = jnp.log1p(-lb) + log_sig
    log_f = jnp.maximum(la, lbb) + jnp.log1p(jnp.exp(-jnp.abs(la - lbb)))
    kf_all = (1.0 - lb) * _sigmoid(-fz)
    pieces = jnp.dot(tri_b, jnp.concatenate(_split3(log_f), axis=1), preferred_element_type=F32)
    width = log_f.shape[1]
    cum_all = pieces[:, :width] + pieces[:, width:2 * width] + pieces[:, 2 * width:]

    for hh in range(HG_STEP_HEADS):
        sl = slice(hh * HG_DK, (hh + 1) * HG_DK)
        cum, kf, v = cum_all[:, sl], kf_all[:, sl], v_ref[:, sl]
        last = [cum[c * HG_CHUNK:c * HG_CHUNK + 1] if reverse else cum[(c + 1) * HG_CHUNK - 1:(c + 1) * HG_CHUNK]
                for c in range(nchunk)]
        tot = jnp.concatenate([jnp.broadcast_to(x, (HG_CHUNK, HG_DK)) for x in last], axis=0)
        q_in = q_ref[:, sl] * jnp.exp(cum)
        k_in = kf * jnp.exp(-cum)
        k_end = kf * jnp.exp(tot - cum)
        o_intra = _dot(_dot_nt(q_in, k_in) * tri_f, v)
        v_by_chunk = jnp.concatenate([jnp.where(in_chunk[c], v, 0.0) for c in range(nchunk)], axis=1)
        d_st = _dot_tn(v_by_chunk, k_end)
        st = st_ref[hh]
        before = [None] * nchunk
        for c in order:
            before[c] = st
            st = st * jnp.exp(last[c]) + d_st[c * HG_DV:(c + 1) * HG_DV]
        st_ref[hh] = st
        q_by_chunk = jnp.concatenate([jnp.where(in_chunk[c], q_in, 0.0) for c in range(nchunk)], axis=1)
        o = o_intra + _dot_nt(q_by_chunk, jnp.concatenate(before, axis=1))
        if finalize:
            o = o + prev_ref[:, sl]
            o = o * lax.rsqrt(jnp.mean(o * o, axis=-1, keepdims=True) + NORM_EPS) * nw_ref[...]
        o_ref[:, sl] = o
        if emit_state:
            rest[1][hh] = st


def hgrn2_pass(z, lb, s0, batch, t, f_off, reverse, prev=None, norm_w=None, emit_state=False):
    bt = min(HG_BLOCK, t)
    nblk = t // bt
    nh = HG_STEP_HEADS
    wide = nh * LANES
    finalize = prev is not None
    assert OD_HG_BLK % nh == 0 and HG_HEADS % nh == 0, "head groups must start on a whole column block"
    pos = (lambda i: nblk - 1 - i) if reverse else (lambda i: i)
    blk = lambda off: pl.BlockSpec((bt, wide), lambda b, h, i, off=off: (b * nblk + pos(i), off // nh + h))
    state = pl.BlockSpec((None, nh, HG_DV, HG_DK), lambda b, h, i: (b, h, 0, 0))
    ins = [z, z, z, lb.reshape(1, -1), s0]
    specs = [blk(OD_HG_BLK), blk(OD_HG_BLK + f_off), blk(OD_HG_BLK + 3 * HG_HEADS),
             pl.BlockSpec((1, wide), lambda b, h, i: (0, h)), state]
    if finalize:
        ins += [prev, norm_w.reshape(1, -1)]
        specs += [blk(0), pl.BlockSpec((1, HG_DV), lambda b, h, i: (0, 0))]
    out_shape = [jax.ShapeDtypeStruct((batch * t, HG_WIDTH), F32)]
    out_specs = [blk(0)]
    if emit_state:
        out_shape.append(jax.ShapeDtypeStruct((batch, HG_HEADS, HG_DV, HG_DK), F32))
        out_specs.append(state)
    return pl.pallas_call(
        functools.partial(_hg_kernel, reverse=reverse, finalize=finalize, emit_state=emit_state),
        out_shape=out_shape,
        grid=(batch, HG_HEADS // nh, nblk),
        in_specs=specs,
        out_specs=out_specs,
        scratch_shapes=[pltpu.VMEM((nh, HG_DV, HG_DK), F32)],
        compiler_params=_cparams(3),
        name="hgrn2_" + ("bwd" if reverse else "fwd"),
    )(*ins)


def _even_layer(x, xc, mod, norm_w, w_in, w_out, rpb, rw_params, batch, t, l):
    s2 = 3 * NA_WIDTH + RW_SHIFT_COLS
    width = RW_PAD_COLS - RW_SHIFT_COLS
    n_gate, n_rest = GATE_COLS // width, s2 // width
    plan = [n_rest + i for i in range(n_gate)] + list(range(n_rest)) + [None]
    w_ext = cast_reorder(w_in, plan, width, "even_w_in")
    tiles_per_seq = max(t // 1024, 1)
    z = in_projection(x, norm_w, mod, lambda i: i // tiles_per_seq, w_ext, "even_in_proj")
    z_c = in_projection(xc, norm_w, mod, lambda i: batch, w_ext, "even_in_proj_ctx")

    y_na = na_attention(z, z_c, rpb, batch, t, l)
    yc_na = ctx_attention(z_c, batch, l)

    mu, w0, w2, a0, a2, k_k, k_a, r_k, ln_w, ln_b = rw_params
    mu = jnp.pad(mu, ((0, 0), (0, RW_PAD_COLS - RW_SHIFT_COLS)))
    zpad = jnp.zeros((RW_RANK, RW_WIDTH), F32)
    w2p = jnp.stack([jnp.concatenate([w2[0], zpad]), jnp.concatenate([zpad, w2[1]])])
    a2p = jnp.stack([jnp.concatenate([a2[0], zpad]), jnp.concatenate([zpad, a2[1]])])
    vecs = (mu, w0, w2p, a0, a2p, k_k.reshape(1, -1), k_a.reshape(1, -1), r_k.reshape(1, -1))
    terms = rwkv_terms(z, batch, t, *vecs)
    terms_c = rwkv_terms(z_c, batch, l, *vecs)
    s_zero = jnp.zeros((RW_HEAD_DIM, RW_HEAD_DIM, LANES), F32)
    yc_f, yc_b, s_ctx = rwkv_scan(*terms_c[:5], s_zero)
    y_f, y_b, _ = rwkv_scan(*terms[:5], s_ctx)
    y_rw = rwkv_readout(y_f, y_b, terms[5], ln_w, ln_b).reshape(batch * t, RW_WIDTH)
    yc_rw = rwkv_readout(yc_f, yc_b, terms_c[5], ln_w, ln_b).reshape(batch * l, RW_WIDTH)

    w_out = w_out.astype(BF16)
    rows_per_seq = max(t // 256, 1)
    x = out_projection(y_na, y_rw, z, w_out, x, mod, lambda i: i // rows_per_seq, "even_out_proj")
    xc = out_projection(yc_na, yc_rw, z_c, w_out, xc, mod, lambda i: batch, "even_out_proj_ctx")
    return x, xc


def _odd_layer(x, xc, mod, norm_w, w_in, w_out, q_norm, w_uq, kv_norm, w_ukv, lb, hg_norm_w, final_w,
               batch, t, l):
    o3 = MLA_Q_RANK + MLA_KV_RANK + MLA_ROPE
    o4 = o3 + 4 * HG_FDIM
    zero = jnp.zeros((x.shape[1], MLA_A_COLS - o3), BF16)
    w_in = w_in.astype(BF16)
    w_ext = jnp.concatenate([w_in[:, o4:], w_in[:, :o3], zero, w_in[:, o3:o4]], axis=1)
    tiles_per_seq = max(t // 1024, 1)
    z = in_projection(x, norm_w, mod, lambda i: i // tiles_per_seq, w_ext, "odd_in_proj")
    z_c = in_projection(xc, norm_w, mod, lambda i: batch, w_ext, "odd_in_proj_ctx")

    cos, sin = _rope_tables(t)
    qk = MLA_NOPE + MLA_ROPE
    w_uq_pad = jnp.pad(w_uq.reshape(MLA_Q_RANK, MLA_HEADS, qk), ((0, 0), (0, 0), (0, MLA_QK_PAD - qk)))
    w_uq_pad = w_uq_pad.reshape(MLA_Q_RANK, MLA_HEADS * MLA_QK_PAD).astype(BF16)
    w_ukv = w_ukv.astype(BF16)
    q = mla_queries(z, q_norm, w_uq_pad, cos, sin, t)
    k, v = mla_keys_values(z, kv_norm, w_ukv, cos, sin, t)
    ones = jnp.ones((l, LANES), F32)
    kc, vc = mla_keys_values(z_c, kv_norm, w_ukv, ones, jnp.zeros_like(ones), l)
    y_mla = mla_attention(q, k, v, kc, vc, batch, t, l)

    s_zero = jnp.zeros((batch, HG_HEADS, HG_DV, HG_DK), F32)
    _, s_f = hgrn2_pass(z_c, lb, s_zero, batch, l, HG_HEADS, False, emit_state=True)
    _, s_b = hgrn2_pass(z_c, lb, s_zero, batch, l, 2 * HG_HEADS, True, emit_state=True)
    o_f, = hgrn2_pass(z, lb, s_f, batch, t, HG_HEADS, False)
    y_hg, = hgrn2_pass(z, lb, s_b, batch, t, 2 * HG_HEADS, True, prev=o_f, norm_w=hg_norm_w)

    rows_per_seq = max(t // 256, 1)
    return out_projection(y_mla, y_hg, z, w_out.astype(BF16), x, mod, lambda i: i // rows_per_seq,
                          "odd_out_proj", final_w=final_w)


def kernel(x, c, ctx, c_ctx, ada_w, ada_b, norm_w, e_w_in, e_w_out, na_rpb, rw_mu, rw_w0, rw_w2, rw_a0, rw_a2, rw_k_k, rw_k_a, rw_r_k, rw_ln_w, rw_ln_b, o_w_in, o_w_out, mla_q_norm, mla_w_uq, mla_kv_norm, mla_w_ukv, hg_lower_bounds, hg_norm_w, final_norm_w):
    batch, t, d = x.shape
    l = ctx.shape[1]
    assert ada_w.shape[0] == 2, "one even and one odd layer"
    cond = jnp.concatenate([c, c_ctx[None, :], jnp.zeros((8 - batch - 1, d), F32)], axis=0)
    mod = modulation(cond, ada_w, ada_b)
    s = jax.nn.softmax(hg_lower_bounds.astype(F32), axis=0)
    lower = jnp.cumsum(s, axis=0) - s[0]

    xf, xcf = x.reshape(batch * t, d), ctx.reshape(batch * l, d)
    rw_params = (rw_mu[0], rw_w0[0], rw_w2[0], rw_a0[0], rw_a2[0], rw_k_k[0], rw_k_a[0], rw_r_k[0],
                 rw_ln_w[0], rw_ln_b[0])
    xf, xcf = _even_layer(xf, xcf, mod[0].reshape(8, 1, 3 * d), norm_w[0], e_w_in[0], e_w_out[0], na_rpb[0],
                          rw_params, batch, t, l)
    out = _odd_layer(xf, xcf, mod[1].reshape(8, 1, 3 * d), norm_w[1], o_w_in[0], o_w_out[0], mla_q_norm[0],
                     mla_w_uq[0], mla_kv_norm[0], mla_w_ukv[0], lower[1], hg_norm_w[0], final_norm_w,
                     batch, t, l)
    return out.reshape(batch, t, d)
```

```python
import functools
import math

import numpy as np
import jax
import jax.numpy as jnp
from jax import lax
from jax.experimental import pallas as pl
from jax.experimental.pallas import tpu as pltpu

F32 = jnp.float32
BF16 = jnp.bfloat16

GRID_W = 64
NORM_EPS = 1e-6
ROPE_BASE = 10000.0

NA_HEADS = 8
NA_HEAD_DIM = 128
NA_WIDTH = NA_HEADS * NA_HEAD_DIM
NA_WIN_ROWS = 8
NA_WIN_COLS = 16
NA_QROWS = 8
NA_KROWS = 16
NA_STEP_HEADS = 4

RW_HEAD_DIM = 64
RW_HEADS = 16
RW_WIDTH = RW_HEADS * RW_HEAD_DIM
RW_RANK = 64
RW_SHIFT_COLS = 3 * RW_WIDTH + 4 * RW_RANK
RW_PAD_COLS = 3584
RW_GN_EPS = 64e-5
RW_SCAN_CHUNK = 64
RW_DECAY_SCALE = math.exp(-0.5)

MLA_HEADS = 8
MLA_Q_RANK = 512
MLA_KV_RANK = 512
MLA_NOPE = 128
MLA_ROPE = 64
MLA_V = 128
MLA_QK_PAD = 256
MLA_WIDTH = MLA_HEADS * MLA_V
MLA_A_COLS = 1536
MLA_KEY_CHUNK = 1024

HG_HEADS = 8
HG_DK = 128
HG_DV = 128
HG_FDIM = HG_HEADS * HG_DK
HG_WIDTH = HG_HEADS * HG_DV
HG_CHUNK = 32
HG_BLOCK = 256
HG_STEP_HEADS = 4

LANES = 128

GATE_COLS = 2048
EV_Q_BLK = GATE_COLS // LANES
EV_RW_BLK = EV_Q_BLK + 3 * NA_WIDTH // LANES
OD_A_BLK = GATE_COLS // LANES
OD_HG_BLK = OD_A_BLK + MLA_A_COLS // LANES
VMEM_LIMIT = 48 * 1024 * 1024
OUT_ROWS = 512
OUT_VMEM_LIMIT = 56 * 1024 * 1024
NEG_INF = -1e30
LOG2E = math.log2(math.e)


def _cparams(n_axes):
    return pltpu.CompilerParams(dimension_semantics=("arbitrary",) * n_axes, vmem_limit_bytes=VMEM_LIMIT)


def _sigmoid(x):
    return 1.0 / (1.0 + jnp.exp(-x))


def _silu(x):
    return x * _sigmoid(x)


def _dot(a, b):
    return jnp.dot(a.astype(BF16), b.astype(BF16), preferred_element_type=F32)


def _dot_nt(a, b):
    return lax.dot_general(a.astype(BF16), b.astype(BF16), (((1,), (1,)), ((), ())), preferred_element_type=F32)


def _dot_tn(a, b):
    return lax.dot_general(a.astype(BF16), b.astype(BF16), (((0,), (0,)), ((), ())), preferred_element_type=F32)


def _mod_kernel(c_ref, w_ref, b_ref, o_ref):
    o_ref[...] = _dot(_silu(c_ref[...]), w_ref[...]) + b_ref[...]


def modulation(cond, ada_w, ada_b, tn=512):
    depth, d, n = ada_w.shape
    return pl.pallas_call(
        _mod_kernel,
        out_shape=jax.ShapeDtypeStruct((depth, 8, n), F32),
        grid=(depth, n // tn),
        in_specs=[pl.BlockSpec((8, d), lambda l, j: (0, 0)),
                  pl.BlockSpec((None, d, tn), lambda l, j: (l, 0, j)),
                  pl.BlockSpec((None, 1, tn), lambda l, j: (l, 0, j))],
        out_specs=pl.BlockSpec((None, 8, tn), lambda l, j: (l, 0, j)),
        compiler_params=_cparams(2),
        name="adaln_modulation",
    )(cond, ada_w, ada_b.reshape(depth, 1, n))


def _cast_reorder_kernel(src_ref, zero_ref, x_ref, o_ref):
    del src_ref
    o_ref[...] = jnp.where(zero_ref[pl.program_id(0)] != 0, 0.0, x_ref[...]).astype(BF16)


def cast_reorder(w, plan, width, name):
    d = w.shape[0]
    src = np.array([0 if a is None else a for a in plan], np.int32)
    zero = np.array([a is None for a in plan], np.int32)
    return pl.pallas_call(
        _cast_reorder_kernel,
        out_shape=jax.ShapeDtypeStruct((d, width * len(plan)), BF16),
        grid_spec=pltpu.PrefetchScalarGridSpec(
            num_scalar_prefetch=2,
            grid=(len(plan),),
            in_specs=[pl.BlockSpec((d, width), lambda j, src, zero: (0, src[j]))],
            out_specs=pl.BlockSpec((d, width), lambda j, src, zero: (0, j))),
        compiler_params=_cparams(1),
        name=name,
    )(jnp.asarray(src), jnp.asarray(zero), w)


def _inproj_kernel(x_ref, nw_ref, shift_ref, scale_ref, w_ref, o_ref, h_ref):
    @pl.when(pl.program_id(1) == 0)
    def _():
        x = x_ref[...]
        y = x * lax.rsqrt(jnp.mean(x * x, axis=-1, keepdims=True) + NORM_EPS) * nw_ref[...]
        h_ref[...] = (y * (1.0 + scale_ref[...]) + shift_ref[...]).astype(BF16)

    o_ref[...] = jnp.dot(h_ref[...], w_ref[...], preferred_element_type=F32)


def in_projection(x, norm_w, mod, mod_row, w, name, tm=1024, tn=512):
    m, d = x.shape
    n = w.shape[1]
    tm = min(tm, m)
    return pl.pallas_call(
        _inproj_kernel,
        out_shape=jax.ShapeDtypeStruct((m, n), F32),
        grid=(m // tm, n // tn),
        in_specs=[pl.BlockSpec((tm, d), lambda i, j: (i, 0)),
                  pl.BlockSpec((1, d), lambda i, j: (0, 0)),
                  pl.BlockSpec((None, 1, d), lambda i, j: (mod_row(i), 0, 0)),
                  pl.BlockSpec((None, 1, d), lambda i, j: (mod_row(i), 0, 1)),
                  pl.BlockSpec((d, tn), lambda i, j: (0, j))],
        out_specs=pl.BlockSpec((tm, tn), lambda i, j: (i, j)),
        scratch_shapes=[pltpu.VMEM((tm, d), BF16)],
        compiler_params=_cparams(2),
        name=name,
    )(x, norm_w.reshape(1, d), mod, mod, w)


def _outproj_kernel(ya_ref, yb_ref, g_ref, w_ref, x_ref, gm_ref, *rest, ka, final):
    o_ref = rest[-1]
    sg = _silu(g_ref[...])
    acc = _dot(ya_ref[...].astype(F32) * sg[:, :ka], w_ref[:ka, :])
    acc += _dot(yb_ref[...].astype(F32) * sg[:, ka:], w_ref[ka:, :])
    out = x_ref[...] + gm_ref[...] * acc
    if final:
        out = out * lax.rsqrt(jnp.mean(out * out, axis=-1, keepdims=True) + NORM_EPS) * rest[0][...]
    o_ref[...] = out


def out_projection(ya, yb, gate, w, x, mod, mod_row, name, final_w=None, tm=OUT_ROWS):
    m, d = x.shape
    ka, kb = ya.shape[1], yb.shape[1]
    tm = min(tm, m)
    ins = [ya, yb, gate, w, x, mod]
    specs = [pl.BlockSpec((tm, ka), lambda i: (i, 0)),
             pl.BlockSpec((tm, kb), lambda i: (i, 0)),
             pl.BlockSpec((tm, ka + kb), lambda i: (i, 0)),
             pl.BlockSpec((ka + kb, d), lambda i: (0, 0), pipeline_mode=pl.Buffered(1)),
             pl.BlockSpec((tm, d), lambda i: (i, 0)),
             pl.BlockSpec((None, 1, d), lambda i: (mod_row(i), 0, 2))]
    if final_w is not None:
        ins.append(final_w.reshape(1, d))
        specs.append(pl.BlockSpec((1, d), lambda i: (0, 0)))
    return pl.pallas_call(
        functools.partial(_outproj_kernel, ka=ka, final=final_w is not None),
        out_shape=jax.ShapeDtypeStruct((m, d), F32),
        grid=(m // tm,),
        in_specs=specs,
        out_specs=pl.BlockSpec((tm, d), lambda i: (i, 0)),
        compiler_params=pltpu.CompilerParams(dimension_semantics=("arbitrary",), vmem_limit_bytes=OUT_VMEM_LIMIT),
        name=name,
    )(*ins)


def _softmax_pv(q, parts, scale):
    c = scale * LOG2E
    m = denom = num = None
    for k, bias, v in parts:
        x = _dot_nt(q, k) * c
        if bias is not None:
            x = x + bias
        m_part = jnp.max(x, axis=-1, keepdims=True)
        if m is None:
            m = m_part
            p = jnp.exp2(x - m)
            denom = jnp.sum(p, axis=-1, keepdims=True)
            num = _dot(p, v)
        else:
            m_new = jnp.maximum(m, m_part)
            alpha = jnp.exp2(m - m_new)
            p = jnp.exp2(x - m_new)
            denom = denom * alpha + jnp.sum(p, axis=-1, keepdims=True)
            num = num * alpha + _dot(p, v)
            m = m_new
    return num / denom


def _softmax_pv_joint(q, parts, scale):
    c = scale * LOG2E
    logits = [_dot_nt(q, k) * c if bias is None else _dot_nt(q, k) * c + bias for k, bias, _ in parts]
    m = functools.reduce(jnp.maximum, [jnp.max(x, axis=-1, keepdims=True) for x in logits])
    ps = [jnp.exp2(x - m) for x in logits]
    denom = functools.reduce(lambda a, b: a + b, [jnp.sum(p, axis=-1, keepdims=True) for p in ps])
    num = functools.reduce(lambda a, b: a + b, [_dot(p, v) for p, (_, _, v) in zip(ps, parts)])
    return num / denom


def _na_kernel(idx_ref, q_ref, k_ref, v_ref, kc_ref, vc_ref, tab_ref, o_ref, bias_ref, *, rows, scale):
    j = pl.program_id(2)
    npair = NA_KROWS // 2
    start = pl.multiple_of(jnp.clip(NA_QROWS * j - NA_WIN_ROWS // 2, 0, rows - NA_KROWS) * GRID_W, GRID_W)
    win = pl.ds(start, NA_KROWS * GRID_W)
    for hd in range(NA_STEP_HEADS):
        sl = slice(hd * LANES, (hd + 1) * LANES)
        for qr in range(NA_QROWS):
            for m in range(npair):
                tile = tab_ref[hd, idx_ref[(j * NA_QROWS + qr) * npair + m]]
                bias_ref[hd, qr * GRID_W:(qr + 1) * GRID_W, m * LANES:(m + 1) * LANES] = tile
        parts = [(k_ref[win, sl], bias_ref[hd], v_ref[win, sl]), (kc_ref[:, sl], None, vc_ref[:, sl])]
        o_ref[:, sl] = _softmax_pv_joint(q_ref[:, sl], parts, scale)


def _na_tile_ids(rows):
    kh = min(NA_WIN_ROWS, rows)
    n = 2 * NA_WIN_ROWS - 1
    ids = []
    for j in range(rows // NA_QROWS):
        ks = int(np.clip(NA_QROWS * j - NA_WIN_ROWS // 2, 0, rows - NA_KROWS))
        for qr in range(NA_QROWS):
            r = NA_QROWS * j + qr
            r0 = int(np.clip(r - kh // 2, 0, rows - kh))
            for m in range(NA_KROWS // 2):
                kr = ks + 2 * m
                dy = kr - r + NA_WIN_ROWS - 1
                first, second = r0 <= kr < r0 + kh, r0 <= kr + 1 < r0 + kh
                ids.append(dy if first and second else n + dy if first else 2 * n + dy + 1 if second else 3 * n)
    return np.array(ids, np.int32)


def _na_bias_tiles(rpb):
    col = np.arange(GRID_W)
    cs = np.clip(col - NA_WIN_COLS // 2, 0, GRID_W - NA_WIN_COLS)
    kc = col[None, :]
    in_win = (kc >= cs[:, None]) & (kc < cs[:, None] + NA_WIN_COLS)
    dx = np.where(in_win, kc - col[:, None] + NA_WIN_COLS - 1, 0)
    pick = jnp.asarray(dx[:, :, None] == np.arange(rpb.shape[-1]), F32)
    picked = jnp.einsum("hyj,cqj->hycq", rpb, pick, precision=lax.Precision.HIGHEST)
    tab = jnp.where(in_win[None, None], picked * LOG2E, NEG_INF)
    neg = jnp.full_like(tab[:, :1], NEG_INF)
    nxt = jnp.concatenate([tab[:, 1:], neg], axis=1)
    both = jnp.concatenate([tab, nxt], axis=-1)
    first = jnp.concatenate([tab, jnp.broadcast_to(neg, tab.shape)], axis=-1)
    second = jnp.concatenate([jnp.broadcast_to(neg, tab.shape), tab], axis=-1)
    return jnp.concatenate([both, first, second, jnp.concatenate([neg, neg], axis=-1)], axis=1)


def na_attention(z, z_c, rpb, batch, t, l):
    rows = t // GRID_W
    nj = rows // NA_QROWS
    tq = NA_QROWS * GRID_W
    ids = jnp.asarray(_na_tile_ids(rows))
    tiles = _na_bias_tiles(rpb)
    nh = NA_STEP_HEADS
    wide = nh * LANES
    q0, k0, v0 = (EV_Q_BLK + i * NA_HEADS for i in range(3))
    assert q0 % nh == 0 and NA_HEADS % nh == 0, "head groups must start on a whole column block"
    return pl.pallas_call(
        functools.partial(_na_kernel, rows=rows, scale=NA_HEAD_DIM ** -0.5),
        out_shape=jax.ShapeDtypeStruct((batch * t, NA_WIDTH), F32),
        grid_spec=pltpu.PrefetchScalarGridSpec(
            num_scalar_prefetch=1,
            grid=(NA_HEADS // nh, batch, nj),
            in_specs=[pl.BlockSpec((tq, wide), lambda h, b, j, g: (b * nj + j, q0 // nh + h)),
                      pl.BlockSpec((t, wide), lambda h, b, j, g: (b, k0 // nh + h)),
                      pl.BlockSpec((t, wide), lambda h, b, j, g: (b, v0 // nh + h)),
                      pl.BlockSpec((l, wide), lambda h, b, j, g: (b, k0 // nh + h)),
                      pl.BlockSpec((l, wide), lambda h, b, j, g: (b, v0 // nh + h)),
                      pl.BlockSpec((nh,) + tiles.shape[1:], lambda h, b, j, g: (h, 0, 0, 0))],
            out_specs=pl.BlockSpec((tq, wide), lambda h, b, j, g: (b * nj + j, h)),
            scratch_shapes=[pltpu.VMEM((nh, tq, NA_KROWS * GRID_W), F32)]),
        compiler_params=_cparams(3),
        name="na_attention",
    )(ids, z, z, z, z_c, z_c, tiles)


def _dense_attn_kernel(q_ref, k_ref, v_ref, o_ref, *, scale):
    for hd in range(NA_STEP_HEADS):
        sl = slice(hd * LANES, (hd + 1) * LANES)
        o_ref[:, sl] = _softmax_pv(q_ref[:, sl], [(k_ref[:, sl], None, v_ref[:, sl])], scale)


def ctx_attention(z_c, batch, l):
    nh = NA_STEP_HEADS
    wide = nh * LANES
    q0, k0, v0 = (EV_Q_BLK + i * NA_HEADS for i in range(3))
    assert q0 % nh == 0 and NA_HEADS % nh == 0, "head groups must start on a whole column block"
    return pl.pallas_call(
        functools.partial(_dense_attn_kernel, scale=NA_HEAD_DIM ** -0.5),
        out_shape=jax.ShapeDtypeStruct((batch * l, NA_WIDTH), F32),
        grid=(batch, NA_HEADS // nh),
        in_specs=[pl.BlockSpec((l, wide), lambda b, h: (b, q0 // nh + h)),
                  pl.BlockSpec((l, wide), lambda b, h: (b, k0 // nh + h)),
                  pl.BlockSpec((l, wide), lambda b, h: (b, v0 // nh + h))],
        out_specs=pl.BlockSpec((l, wide), lambda b, h: (b, h)),
        compiler_params=_cparams(2),
        name="ctx_attention",
    )(z_c, z_c, z_c)


def _head_sum(x):
    first = lax.broadcasted_iota(jnp.int32, x.shape, 1) < RW_HEAD_DIM
    s0 = jnp.sum(jnp.where(first, x, 0.0), axis=-1, keepdims=True)
    s1 = jnp.sum(jnp.where(first, 0.0, x), axis=-1, keepdims=True)
    return jnp.where(first, s0, s1)


def _rw_terms_kernel(ur_ref, uk_ref, uv_ref, uwa_ref, pr_ref, pk_ref, pv_ref, pwa_ref, nr_ref, nk_ref, nv_ref,
                     nwa_ref, mu_ref, w0_ref, w2_ref, a0_ref, a2_ref, kk_ref, ka_ref, rk_ref,
                     rv_o, wk0_o, wk1_o, bk0_o, bk1_o, bonus_o, *, nt):
    i = pl.program_id(1)
    tm = ur_ref.shape[0]

    def shifted(u_ref, p_ref, n_ref, col0):
        u = u_ref[...]
        mu = mu_ref[:, col0:col0 + u.shape[1]]
        row = lax.broadcasted_iota(jnp.int32, u.shape, 0)
        before = jnp.where(i == 0, 0.0, p_ref[7:8, :])
        after = jnp.where(i == nt - 1, 0.0, n_ref[0:1, :])
        prev = jnp.where(row == 0, before, pltpu.roll(u, 1, axis=0))
        nxt = jnp.where(row == tm - 1, after, pltpu.roll(u, tm - 1, axis=0))
        return u + mu[0:1] * (prev - u) + mu[1:2] * (nxt - u)

    r_all = shifted(ur_ref, pr_ref, nr_ref, 0)
    k_all = shifted(uk_ref, pk_ref, nk_ref, RW_WIDTH)
    v_all = shifted(uv_ref, pv_ref, nv_ref, 2 * RW_WIDTH)
    wa = shifted(uwa_ref, pwa_ref, nwa_ref, 3 * RW_WIDTH)
    wd = jnp.tanh(wa[:, :LANES]).astype(BF16)
    ad = wa[:, LANES:].astype(BF16)
    first = lax.broadcasted_iota(jnp.int32, (tm, LANES), 1) < RW_HEAD_DIM

    def head_rows(x, y):
        return (jnp.where(first, x, pltpu.roll(y, RW_HEAD_DIM, axis=1)),
                jnp.where(first, pltpu.roll(x, RW_HEAD_DIM, axis=1), y))

    outs = (rv_o, wk0_o, wk1_o, bk0_o, bk1_o)
    even = [[] for _ in outs]
    odd = [[] for _ in outs]
    bonus = []
    for p in range(RW_WIDTH // LANES):
        sl = slice(p * LANES, (p + 1) * LANES)
        r, k, v = r_all[:, sl], k_all[:, sl], v_all[:, sl]
        kk = k * kk_ref[:, sl]
        kk = kk / jnp.maximum(jnp.sqrt(_head_sum(kk * kk)), 1e-12)
        kd_sum = jnp.zeros_like(k)
        per_dir = []
        for d in range(2):
            w = w0_ref[d:d + 1, sl] + jnp.dot(wd, w2_ref[d, :, sl].astype(BF16), preferred_element_type=F32)
            a = _sigmoid(a0_ref[d:d + 1, sl] + jnp.dot(ad, a2_ref[d, :, sl].astype(BF16), preferred_element_type=F32))
            kd = k * (1.0 + (a - 1.0) * ka_ref[:, sl])
            dec = jnp.exp(-RW_DECAY_SCALE * _sigmoid(w))
            per_dir.append(((dec, kd), (kk * a, kk)))
            kd_sum = kd_sum + kd
        pairs = [(r, v), per_dir[0][0], per_dir[1][0], per_dir[0][1], per_dir[1][1]]
        bonus.append(_head_sum(r * kd_sum * rk_ref[:, sl]) * v)
        for ev, od, (x, y) in zip(even, odd, pairs):
            e, o = head_rows(x, y)
            ev.append(e)
            od.append(o)
    for o_ref, ev, od in zip(outs, even, odd):
        o_ref[...] = jnp.swapaxes(jnp.stack(ev + od, axis=0), 0, 1)
    bonus_o[...] = jnp.swapaxes(jnp.stack(bonus, axis=0), 0, 1)


def rwkv_terms(z, n_seq, t, mu, w0, w2p, a0, a2p, k_k, k_a, r_k, tm=256):
    tm = min(tm, t)
    nt = t // tm
    npair = RW_WIDTH // LANES
    wide0 = EV_RW_BLK * LANES // RW_WIDTH
    wa_blk = (EV_RW_BLK + 3 * npair) // 2
    n8 = tm // 8

    def cur(width, cb):
        return pl.BlockSpec((tm, width), lambda b, i: (b * nt + i, cb))

    def before(width, cb):
        return pl.BlockSpec((8, width), lambda b, i: (jnp.maximum((b * nt + i) * n8 - 1, 0), cb))

    def after(width, cb):
        return pl.BlockSpec((8, width), lambda b, i: (jnp.minimum((b * nt + i + 1) * n8, n_seq * nt * n8 - 1), cb))

    cols = [(RW_WIDTH, wide0), (RW_WIDTH, wide0 + 1), (RW_WIDTH, wide0 + 2), (2 * LANES, wa_blk)]
    whole = lambda a: pl.BlockSpec(a.shape, lambda b, i: (0,) * a.ndim)
    params = (mu, w0, w2p, a0, a2p, k_k, k_a, r_k)
    out = pl.BlockSpec((tm, npair, LANES), lambda b, i: (i, b, 0))
    out2 = pl.BlockSpec((tm, 2 * npair, LANES), lambda b, i: (i, b, 0))
    return pl.pallas_call(
        functools.partial(_rw_terms_kernel, nt=nt),
        out_shape=[jax.ShapeDtypeStruct((t, n_seq * 2 * npair, LANES), F32)] * 5
        + [jax.ShapeDtypeStruct((t, n_seq * npair, LANES), F32)],
        grid=(n_seq, nt),
        in_specs=([cur(*c) for c in cols] + [before(*c) for c in cols] + [after(*c) for c in cols]
                  + [whole(a) for a in params]),
        out_specs=[out2] * 5 + [out],
        compiler_params=_cparams(2),
        name="rwkv_terms",
    )(*([z] * 12), *params)


def _to_lanes(n):
    q = jnp.concatenate([n, pltpu.roll(n, RW_HEAD_DIM, axis=1)], axis=0)
    return q.T[:RW_HEAD_DIM]


def _rw_scan_kernel(rvf, rvb, wk0, wk1, bk0, bk1, s0_ref, yf_ref, yb_ref, sfin_ref,
                    s_ref, z_ref, znext_ref, g_ref, ybuf_ref, *, chunk):
    n = RW_HEAD_DIM
    nb = rvf.shape[1] // (2 * RW_WIDTH // LANES)
    R, K, V, A, B = range(5)

    @pl.when(pl.program_id(0) == 0)
    def _():
        s_ref[...] = s0_ref[...]

    def convert(t_src):
        def pair_tiles(f, bw):
            both = (f[t_src], bw[chunk - 1 - t_src])
            x = jnp.concatenate([src[(b * 2 + par) * 8:(b * 2 + par) * 8 + 8]
                                 for par in range(2) for src in both for b in range(nb)], axis=0)
            xt = x.T
            return xt[:n], xt[n:]

        r, v = pair_tiles(rvf, rvb)
        w, k = pair_tiles(wk0, wk1)
        b, kk = pair_tiles(bk0, bk1)
        g_prev = g_ref[0]
        g = g_prev * w
        g_ref[0] = g
        inv = 1.0 / g
        znext_ref[R] = r * g
        znext_ref[K] = k * inv
        znext_ref[V] = v
        znext_ref[A] = -(kk * g_prev)
        znext_ref[B] = b * inv

    def emit_y(t):
        rows = _to_lanes(ybuf_ref[...])
        yf_ref[t] = rows[:n // 2]
        yb_ref[chunk - 1 - t] = rows[n // 2:]

    g_ref[0] = jnp.ones((n, LANES), F32)
    ybuf_ref[...] = jnp.zeros((n, LANES), F32)
    convert(0)

    def step(t, carry):
        emit_y(jnp.maximum(t - 1, 0))
        z_ref[...] = znext_ref[...]
        g_ref[1] = g_ref[0]
        convert(jnp.minimum(t + 1, chunk - 1))
        sa = jnp.zeros((n, LANES), F32)
        for k in range(n):
            sa = sa + s_ref[k] * z_ref[A, k:k + 1, :]
        vt = z_ref[V]
        y = jnp.zeros((n, LANES), F32)
        for k in range(n):
            sk = s_ref[k] + sa * z_ref[B, k:k + 1, :] + vt * z_ref[K, k:k + 1, :]
            s_ref[k] = sk
            y = y + sk * z_ref[R, k:k + 1, :]
        ybuf_ref[...] = y
        return carry

    lax.fori_loop(0, chunk, step, 0)
    emit_y(chunk - 1)
    for k in range(n):
        s_ref[k] = s_ref[k] * g_ref[1, k:k + 1, :]
    sfin_ref[...] = s_ref[...]


def rwkv_scan(rv, wk0, wk1, bk0, bk1, s0):
    t, nrow, _ = rv.shape
    n = RW_HEAD_DIM
    chunk = RW_SCAN_CHUNK
    nc = t // chunk
    assert nrow == LANES // 2, "the states of both directions fill the 128 lanes"
    fwd = pl.BlockSpec((chunk, nrow, LANES), lambda i: (i, 0, 0))
    bwd = pl.BlockSpec((chunk, nrow, LANES), lambda i: (nc - 1 - i, 0, 0))
    yfwd = pl.BlockSpec((chunk, nrow // 2, LANES), lambda i: (i, 0, 0))
    ybwd = pl.BlockSpec((chunk, nrow // 2, LANES), lambda i: (nc - 1 - i, 0, 0))
    state = pl.BlockSpec((n, n, LANES), lambda i: (0, 0, 0))
    slab = jax.ShapeDtypeStruct((t, nrow // 2, LANES), F32)
    return pl.pallas_call(
        functools.partial(_rw_scan_kernel, chunk=chunk),
        out_shape=[slab, slab, jax.ShapeDtypeStruct((n, n, LANES), F32)],
        grid=(nc,),
        in_specs=[fwd, bwd, fwd, bwd, fwd, bwd, state],
        out_specs=[yfwd, ybwd, state],
        scratch_shapes=[pltpu.VMEM((n, n, LANES), F32), pltpu.VMEM((5, n, LANES), F32),
                        pltpu.VMEM((5, n, LANES), F32), pltpu.VMEM((2, n, LANES), F32),
                        pltpu.VMEM((n, LANES), F32)],
        compiler_params=_cparams(1),
        name="rwkv_scan",
    )(rv, rv, wk0, wk1, bk0, bk1, s0)


def _rw_readout_kernel(yf_ref, yb_ref, bonus_ref, lw_ref, lb_ref, o_ref):
    npair = RW_WIDTH // LANES
    y_all = jnp.swapaxes(yf_ref[...] + yb_ref[...], 0, 1)
    bonus_all = jnp.swapaxes(bonus_ref[...], 0, 1)
    for q in range(yf_ref.shape[1]):
        b, p = divmod(q, npair)
        sl = slice(p * LANES, (p + 1) * LANES)
        y = y_all[q]
        mu = _head_sum(y) * (1.0 / RW_HEAD_DIM)
        d = y - mu
        var = _head_sum(d * d) * (1.0 / RW_HEAD_DIM)
        o_ref[b, :, sl] = d * lax.rsqrt(var + RW_GN_EPS) * lw_ref[:, sl] + lb_ref[:, sl] + bonus_all[q]


def rwkv_readout(y_f, y_b, bonus, ln_w, ln_b, tm=256):
    t, nq, _ = y_f.shape
    tm = min(tm, t)
    batch = nq // (RW_WIDTH // LANES)
    slab = pl.BlockSpec((tm, nq, LANES), lambda i: (i, 0, 0))
    vec = pl.BlockSpec((1, RW_WIDTH), lambda i: (0, 0))
    return pl.pallas_call(
        _rw_readout_kernel,
        out_shape=jax.ShapeDtypeStruct((batch, t, RW_WIDTH), F32),
        grid=(t // tm,),
        in_specs=[slab, slab, slab, vec, vec],
        out_specs=pl.BlockSpec((batch, tm, RW_WIDTH), lambda i: (0, i, 0)),
        compiler_params=_cparams(1),
        name="rwkv_readout",
    )(y_f, y_b, bonus, ln_w.reshape(1, -1), ln_b.reshape(1, -1))


def _rms(x, w):
    return x * lax.rsqrt(jnp.mean(x * x, axis=-1, keepdims=True) + NORM_EPS) * w


def _rope(x, cos, sin):
    odd = (lax.broadcasted_iota(jnp.int32, x.shape, 1) & 1) == 1
    swapped = jnp.where(odd, pltpu.roll(x, 1, axis=1), pltpu.roll(x, LANES - 1, axis=1))
    return x * cos + swapped * sin


def _mla_q_kernel(c_ref, nw_ref, w_ref, cos_ref, sin_ref, q_ref):
    h = _rms(c_ref[...], nw_ref[...]).astype(BF16)
    acc = jnp.dot(h, w_ref[...], preferred_element_type=F32)
    cos, sin = cos_ref[...], sin_ref[...]
    for hd in range(MLA_HEADS):
        lo = hd * MLA_QK_PAD
        q_ref[:, lo:lo + MLA_NOPE] = acc[:, lo:lo + MLA_NOPE].astype(BF16)
        q_ref[:, lo + MLA_NOPE:lo + MLA_QK_PAD] = _rope(acc[:, lo + MLA_NOPE:lo + MLA_QK_PAD], cos, sin).astype(BF16)


def mla_queries(z, q_norm, w_uq_pad, cos, sin, t, tm=512):
    m = z.shape[0]
    tm = min(tm, t)
    nt = t // tm
    width = MLA_HEADS * MLA_QK_PAD
    return pl.pallas_call(
        _mla_q_kernel,
        out_shape=jax.ShapeDtypeStruct((m, width), BF16),
        grid=(m // tm,),
        in_specs=[pl.BlockSpec((tm, MLA_Q_RANK), lambda i: (i, GATE_COLS // MLA_Q_RANK)),
                  pl.BlockSpec((1, MLA_Q_RANK), lambda i: (0, 0)),
                  pl.BlockSpec((MLA_Q_RANK, width), lambda i: (0, 0)),
                  pl.BlockSpec((tm, LANES), lambda i: (i % nt, 0)),
                  pl.BlockSpec((tm, LANES), lambda i: (i % nt, 0))],
        out_specs=pl.BlockSpec((tm, width), lambda i: (i, 0)),
        compiler_params=_cparams(1),
        name="mla_queries",
    )(z, q_norm.reshape(1, -1), w_uq_pad, cos, sin)


def _mla_kv_kernel(c_ref, pe_ref, nw_ref, w_ref, cos_ref, sin_ref, k_ref, v_ref):
    h = _rms(c_ref[...], nw_ref[...]).astype(BF16)
    acc = jnp.dot(h, w_ref[...], preferred_element_type=F32)
    pe = _rope(pe_ref[...], cos_ref[...], sin_ref[...]).astype(BF16)
    for hd in range(MLA_HEADS):
        src = hd * (MLA_NOPE + MLA_V)
        dst = hd * MLA_QK_PAD
        k_ref[:, dst:dst + MLA_NOPE] = acc[:, src:src + MLA_NOPE].astype(BF16)
        k_ref[:, dst + MLA_NOPE:dst + MLA_QK_PAD] = pe
        v_ref[:, hd * MLA_V:(hd + 1) * MLA_V] = acc[:, src + MLA_NOPE:src + MLA_NOPE + MLA_V].astype(BF16)


def mla_keys_values(z, kv_norm, w_ukv, cos, sin, t, tm=512):
    m = z.shape[0]
    tm = min(tm, t)
    nt = t // tm
    return pl.pallas_call(
        _mla_kv_kernel,
        out_shape=[jax.ShapeDtypeStruct((m, MLA_HEADS * MLA_QK_PAD), BF16),
                   jax.ShapeDtypeStruct((m, MLA_WIDTH), BF16)],
        grid=(m // tm,),
        in_specs=[pl.BlockSpec((tm, MLA_KV_RANK), lambda i: (i, (GATE_COLS + MLA_Q_RANK) // MLA_KV_RANK)),
                  pl.BlockSpec((tm, LANES), lambda i: (i, OD_A_BLK + (MLA_Q_RANK + MLA_KV_RANK) // LANES)),
                  pl.BlockSpec((1, MLA_KV_RANK), lambda i: (0, 0)),
                  pl.BlockSpec(w_ukv.shape, lambda i: (0, 0)),
                  pl.BlockSpec((tm, LANES), lambda i: (i % nt, 0)),
                  pl.BlockSpec((tm, LANES), lambda i: (i % nt, 0))],
        out_specs=[pl.BlockSpec((tm, MLA_HEADS * MLA_QK_PAD), lambda i: (i, 0)),
                   pl.BlockSpec((tm, MLA_WIDTH), lambda i: (i, 0))],
        compiler_params=_cparams(1),
        name="mla_keys_values",
    )(z, z, kv_norm.reshape(1, -1), w_ukv, cos, sin)


def _mla_attn_kernel(q_ref, k_ref, v_ref, kc_ref, vc_ref, o_ref, *, scale):
    parts = [(kc_ref[...], None, vc_ref[...])]
    nkeys = k_ref.shape[0]
    step = min(MLA_KEY_CHUNK, nkeys)
    for lo in range(0, nkeys, step):
        parts.append((k_ref[lo:lo + step, :], None, v_ref[lo:lo + step, :]))
    o_ref[...] = _softmax_pv(q_ref[...], parts, scale)


def mla_attention(q, k, v, kc, vc, batch, t, l, tq=2048):
    tq = min(tq, t)
    nq = t // tq
    return pl.pallas_call(
        functools.partial(_mla_attn_kernel, scale=(MLA_NOPE + MLA_ROPE) ** -0.5),
        out_shape=jax.ShapeDtypeStruct((batch * t, MLA_WIDTH), F32),
        grid=(batch, MLA_HEADS, nq),
        in_specs=[pl.BlockSpec((tq, MLA_QK_PAD), lambda b, h, i: (b * nq + i, h)),
                  pl.BlockSpec((t, MLA_QK_PAD), lambda b, h, i: (b, h)),
                  pl.BlockSpec((t, MLA_V), lambda b, h, i: (b, h)),
                  pl.BlockSpec((l, MLA_QK_PAD), lambda b, h, i: (b, h)),
                  pl.BlockSpec((l, MLA_V), lambda b, h, i: (b, h))],
        out_specs=pl.BlockSpec((tq, MLA_V), lambda b, h, i: (b * nq + i, h)),
        compiler_params=_cparams(3),
        name="mla_attention",
    )(q, k, v, kc, vc)


def _rope_tables(t):
    tok = np.arange(t)
    pos = np.stack([tok // GRID_W, tok % GRID_W], axis=-1).astype(np.float32)
    n_freq = MLA_ROPE // 4
    inv = (ROPE_BASE ** (-jnp.arange(n_freq, dtype=F32) / n_freq))
    ang = (jnp.asarray(pos)[:, :, None] * inv).reshape(t, MLA_ROPE // 2)
    cos = jnp.repeat(jnp.cos(ang), 2, axis=-1)
    sin = jnp.repeat(jnp.sin(ang), 2, axis=-1) * jnp.tile(jnp.array([-1.0, 1.0], F32), MLA_ROPE // 2)
    pad = ((0, 0), (0, LANES - MLA_ROPE))
    return jnp.pad(cos, pad), jnp.pad(sin, pad)


def _split3(x):
    hi = x.astype(BF16)
    r1 = x - hi.astype(F32)
    mid = r1.astype(BF16)
    lo = (r1 - mid.astype(F32)).astype(BF16)
    return hi, mid, lo


def _hg_kernel(q_ref, f_ref, v_ref, lb_ref, s0_ref, *rest, reverse, finalize, emit_state):
    st_ref = rest[-1]
    rest = rest[:-1]
    if finalize:
        prev_ref, nw_ref = rest[0], rest[1]
        rest = rest[2:]
    o_ref = rest[0]

    @pl.when(pl.program_id(2) == 0)
    def _():
        st_ref[...] = s0_ref[...]

    bt = q_ref.shape[0]
    nchunk = bt // HG_CHUNK
    order = range(nchunk - 1, -1, -1) if reverse else range(nchunk)
    ri = lax.broadcasted_iota(jnp.int32, (bt, bt), 0)
    ci = lax.broadcasted_iota(jnp.int32, (bt, bt), 1)
    same = (ri // HG_CHUNK) == (ci // HG_CHUNK)
    tri = same & ((ci >= ri) if reverse else (ci <= ri))
    tri_f = tri.astype(F32)
    tri_b = tri.astype(BF16)
    row_chunk = lax.broadcasted_iota(jnp.int32, (bt, HG_DK), 0) // HG_CHUNK
    in_chunk = [row_chunk == c for c in range(nchunk)]

    lb = lb_ref[...]
    fz = f_ref[...]
    log_sig = jnp.minimum(fz, 0.0) - jnp.log1p(jnp.exp(-jnp.abs(fz)))
    la = jnp.log(lb)
    lbb = jnp.log1p(-lb) + log_sig
    log_f = jnp.maximum(la, lbb) + jnp.log1p(jnp.exp(-jnp.abs(la - lbb)))
    kf_all = (1.0 - lb) * _sigmoid(-fz)
    pieces = jnp.dot(tri_b, jnp.concatenate(_split3(log_f), axis=1), preferred_element_type=F32)
    width = log_f.shape[1]
    cum_all = pieces[:, :width] + pieces[:, width:2 * width] + pieces[:, 2 * width:]

    for hh in range(HG_STEP_HEADS):
        sl = slice(hh * HG_DK, (hh + 1) * HG_DK)
        cum, kf, v = cum_all[:, sl], kf_all[:, sl], v_ref[:, sl]
        last = [cum[c * HG_CHUNK:c * HG_CHUNK + 1] if reverse else cum[(c + 1) * HG_CHUNK - 1:(c + 1) * HG_CHUNK]
                for c in range(nchunk)]
        tot = jnp.concatenate([jnp.broadcast_to(x, (HG_CHUNK, HG_DK)) for x in last], axis=0)
        q_in = q_ref[:, sl] * jnp.exp(cum)
        k_in = kf * jnp.exp(-cum)
        k_end = kf * jnp.exp(tot - cum)
        o_intra = _dot(_dot_nt(q_in, k_in) * tri_f, v)
        v_by_chunk = jnp.concatenate([jnp.where(in_chunk[c], v, 0.0) for c in range(nchunk)], axis=1)
        d_st = _dot_tn(v_by_chunk, k_end)
        st = st_ref[hh]
        before = [None] * nchunk
        for c in order:
            before[c] = st
            st = st * jnp.exp(last[c]) + d_st[c * HG_DV:(c + 1) * HG_DV]
        st_ref[hh] = st
        q_by_chunk = jnp.concatenate([jnp.where(in_chunk[c], q_in, 0.0) for c in range(nchunk)], axis=1)
        o = o_intra + _dot_nt(q_by_chunk, jnp.concatenate(before, axis=1))
        if finalize:
            o = o + prev_ref[:, sl]
            o = o * lax.rsqrt(jnp.mean(o * o, axis=-1, keepdims=True) + NORM_EPS) * nw_ref[...]
        o_ref[:, sl] = o
        if emit_state:
            rest[1][hh] = st


def hgrn2_pass(z, lb, s0, batch, t, f_off, reverse, prev=None, norm_w=None, emit_state=False):
    bt = min(HG_BLOCK, t)
    nblk = t // bt
    nh = HG_STEP_HEADS
    wide = nh * LANES
    finalize = prev is not None
    assert OD_HG_BLK % nh == 0 and HG_HEADS % nh == 0, "head groups must start on a whole column block"
    pos = (lambda i: nblk - 1 - i) if reverse else (lambda i: i)
    blk = lambda off: pl.BlockSpec((bt, wide), lambda b, h, i, off=off: (b * nblk + pos(i), off // nh + h))
    state = pl.BlockSpec((None, nh, HG_DV, HG_DK), lambda b, h, i: (b, h, 0, 0))
    ins = [z, z, z, lb.reshape(1, -1), s0]
    specs = [blk(OD_HG_BLK), blk(OD_HG_BLK + f_off), blk(OD_HG_BLK + 3 * HG_HEADS),
             pl.BlockSpec((1, wide), lambda b, h, i: (0, h)), state]
    if finalize:
        ins += [prev, norm_w.reshape(1, -1)]
        specs += [blk(0), pl.BlockSpec((1, HG_DV), lambda b, h, i: (0, 0))]
    out_shape = [jax.ShapeDtypeStruct((batch * t, HG_WIDTH), F32)]
    out_specs = [blk(0)]
    if emit_state:
        out_shape.append(jax.ShapeDtypeStruct((batch, HG_HEADS, HG_DV, HG_DK), F32))
        out_specs.append(state)
    return pl.pallas_call(
        functools.partial(_hg_kernel, reverse=reverse, finalize=finalize, emit_state=emit_state),
        out_shape=out_shape,
        grid=(batch, HG_HEADS // nh, nblk),
        in_specs=specs,
        out_specs=out_specs,
        scratch_shapes=[pltpu.VMEM((nh, HG_DV, HG_DK), F32)],
        compiler_params=_cparams(3),
        name="hgrn2_" + ("bwd" if reverse else "fwd"),
    )(*ins)


def _even_layer(x, xc, mod, norm_w, w_in, w_out, rpb, rw_params, batch, t, l):
    s2 = 3 * NA_WIDTH + RW_SHIFT_COLS
    width = RW_PAD_COLS - RW_SHIFT_COLS
    n_gate, n_rest = GATE_COLS // width, s2 // width
    plan = [n_rest + i for i in range(n_gate)] + list(range(n_rest)) + [None]
    w_ext = cast_reorder(w_in, plan, width, "even_w_in")
    tiles_per_seq = max(t // 1024, 1)
    z = in_projection(x, norm_w, mod, lambda i: i // tiles_per_seq, w_ext, "even_in_proj")
    z_c = in_projection(xc, norm_w, mod, lambda i: batch, w_ext, "even_in_proj_ctx")

    y_na = na_attention(z, z_c, rpb, batch, t, l)
    yc_na = ctx_attention(z_c, batch, l)

    mu, w0, w2, a0, a2, k_k, k_a, r_k, ln_w, ln_b = rw_params
    mu = jnp.pad(mu, ((0, 0), (0, RW_PAD_COLS - RW_SHIFT_COLS)))
    zpad = jnp.zeros((RW_RANK, RW_WIDTH), F32)
    w2p = jnp.stack([jnp.concatenate([w2[0], zpad]), jnp.concatenate([zpad, w2[1]])])
    a2p = jnp.stack([jnp.concatenate([a2[0], zpad]), jnp.concatenate([zpad, a2[1]])])
    vecs = (mu, w0, w2p, a0, a2p, k_k.reshape(1, -1), k_a.reshape(1, -1), r_k.reshape(1, -1))
    terms = rwkv_terms(z, batch, t, *vecs)
    terms_c = rwkv_terms(z_c, batch, l, *vecs)
    s_zero = jnp.zeros((RW_HEAD_DIM, RW_HEAD_DIM, LANES), F32)
    yc_f, yc_b, s_ctx = rwkv_scan(*terms_c[:5], s_zero)
    y_f, y_b, _ = rwkv_scan(*terms[:5], s_ctx)
    y_rw = rwkv_readout(y_f, y_b, terms[5], ln_w, ln_b).reshape(batch * t, RW_WIDTH)
    yc_rw = rwkv_readout(yc_f, yc_b, terms_c[5], ln_w, ln_b).reshape(batch * l, RW_WIDTH)

    w_out = w_out.astype(BF16)
    rows_per_seq = max(t // OUT_ROWS, 1)
    x = out_projection(y_na, y_rw, z, w_out, x, mod, lambda i: i // rows_per_seq, "even_out_proj")
    xc = out_projection(yc_na, yc_rw, z_c, w_out, xc, mod, lambda i: batch, "even_out_proj_ctx")
    return x, xc


def _odd_layer(x, xc, mod, norm_w, w_in, w_out, q_norm, w_uq, kv_norm, w_ukv, lb, hg_norm_w, final_w,
               batch, t, l):
    o3 = MLA_Q_RANK + MLA_KV_RANK + MLA_ROPE
    o4 = o3 + 4 * HG_FDIM
    zero = jnp.zeros((x.shape[1], MLA_A_COLS - o3), BF16)
    w_in = w_in.astype(BF16)
    w_ext = jnp.concatenate([w_in[:, o4:], w_in[:, :o3], zero, w_in[:, o3:o4]], axis=1)
    tiles_per_seq = max(t // 1024, 1)
    z = in_projection(x, norm_w, mod, lambda i: i // tiles_per_seq, w_ext, "odd_in_proj")
    z_c = in_projection(xc, norm_w, mod, lambda i: batch, w_ext, "odd_in_proj_ctx")

    cos, sin = _rope_tables(t)
    qk = MLA_NOPE + MLA_ROPE
    w_uq_pad = jnp.pad(w_uq.reshape(MLA_Q_RANK, MLA_HEADS, qk), ((0, 0), (0, 0), (0, MLA_QK_PAD - qk)))
    w_uq_pad = w_uq_pad.reshape(MLA_Q_RANK, MLA_HEADS * MLA_QK_PAD).astype(BF16)
    w_ukv = w_ukv.astype(BF16)
    q = mla_queries(z, q_norm, w_uq_pad, cos, sin, t)
    k, v = mla_keys_values(z, kv_norm, w_ukv, cos, sin, t)
    ones = jnp.ones((l, LANES), F32)
    kc, vc = mla_keys_values(z_c, kv_norm, w_ukv, ones, jnp.zeros_like(ones), l)
    y_mla = mla_attention(q, k, v, kc, vc, batch, t, l)

    s_zero = jnp.zeros((batch, HG_HEADS, HG_DV, HG_DK), F32)
    _, s_f = hgrn2_pass(z_c, lb, s_zero, batch, l, HG_HEADS, False, emit_state=True)
    _, s_b = hgrn2_pass(z_c, lb, s_zero, batch, l, 2 * HG_HEADS, True, emit_state=True)
    o_f, = hgrn2_pass(z, lb, s_f, batch, t, HG_HEADS, False)
    y_hg, = hgrn2_pass(z, lb, s_b, batch, t, 2 * HG_HEADS, True, prev=o_f, norm_w=hg_norm_w)

    rows_per_seq = max(t // OUT_ROWS, 1)
    return out_projection(y_mla, y_hg, z, w_out.astype(BF16), x, mod, lambda i: i // rows_per_seq,
                          "odd_out_proj", final_w=final_w)


def kernel(x, c, ctx, c_ctx, ada_w, ada_b, norm_w, e_w_in, e_w_out, na_rpb, rw_mu, rw_w0, rw_w2, rw_a0, rw_a2, rw_k_k, rw_k_a, rw_r_k, rw_ln_w, rw_ln_b, o_w_in, o_w_out, mla_q_norm, mla_w_uq, mla_kv_norm, mla_w_ukv, hg_lower_bounds, hg_norm_w, final_norm_w):
    batch, t, d = x.shape
    l = ctx.shape[1]
    assert ada_w.shape[0] == 2, "one even and one odd layer"
    cond = jnp.concatenate([c, c_ctx[None, :], jnp.zeros((8 - batch - 1, d), F32)], axis=0)
    mod = modulation(cond, ada_w, ada_b)
    s = jax.nn.softmax(hg_lower_bounds.astype(F32), axis=0)
    lower = jnp.cumsum(s, axis=0) - s[0]

    xf, xcf = x.reshape(batch * t, d), ctx.reshape(batch * l, d)
    rw_params = (rw_mu[0], rw_w0[0], rw_w2[0], rw_a0[0], rw_a2[0], rw_k_k[0], rw_k_a[0], rw_r_k[0],
                 rw_ln_w[0], rw_ln_b[0])
    xf, xcf = _even_layer(xf, xcf, mod[0].reshape(8, 1, 3 * d), norm_w[0], e_w_in[0], e_w_out[0], na_rpb[0],
                          rw_params, batch, t, l)
    out = _odd_layer(xf, xcf, mod[1].reshape(8, 1, 3 * d), norm_w[1], o_w_in[0], o_w_out[0], mla_q_norm[0],
                     mla_w_uq[0], mla_kv_norm[0], mla_w_ukv[0], lower[1], hg_norm_w[0], final_norm_w,
                     batch, t, l)
    return out.reshape(batch, t, d)
```
